```python
import math
import jax, jax.numpy as jnp
from jax import lax
import numpy as np

D_MODEL = 2048
BATCH = 8
SEQ = 4096
DEPTH = 1

CHUNK = 64
MEM_LEN = 256
EPS = 1e-6
GDN_HEAD_DIM = 128
GDN_WIDTH = D_MODEL // 2
GDN_HEADS = GDN_WIDTH // GDN_HEAD_DIM
CONV_WIDTH = 4
S5_WIDTH = D_MODEL // 2
S5_GROUP = 16
S5_GROUPS = S5_WIDTH // S5_GROUP
S5_STATE = 64
XA_HEADS = 4
XA_WIDTH = D_MODEL // 2
XA_HEAD_DIM = XA_WIDTH // XA_HEADS
N_BRANCHES = 3
IN_SPLITS = (3 * GDN_WIDTH, GDN_WIDTH, GDN_HEADS, GDN_HEADS, S5_WIDTH, S5_WIDTH, XA_WIDTH, XA_WIDTH, N_BRANCHES * D_MODEL)

kernel_name = 'hybrid_gdn_s5_memxattn_block'


def rmsnorm(x, g):
    xf = x.astype(jnp.float32)
    y = xf * lax.rsqrt(jnp.mean(xf * xf, axis=-1, keepdims=True) + EPS)
    return (y * g.astype(jnp.float32)).astype(x.dtype)


def l2norm(x):
    return x * lax.rsqrt(jnp.sum(x * x, axis=-1, keepdims=True) + EPS)


def split_columns(t, sizes):
    out = []
    start = 0
    for s in sizes:
        out.append(t[..., start:start + s])
        start += s
    return out


def causal_depthwise_conv(x, w):
    c = x.shape[-1]
    return lax.conv_general_dilated(x, w[:, None, :], window_strides=(1,), padding=[(w.shape[0] - 1, 0)],
                                    dimension_numbers=('NWC', 'WIO', 'NWC'), feature_group_count=c)


def chunked_gated_delta_rule(q, k, v, g, beta):
    bsz, s, h, dk = q.shape
    dv = v.shape[-1]
    n = s // CHUNK
    def to_chunks(t):
        t = t.reshape((bsz, n, CHUNK, h) + t.shape[3:])
        return jnp.swapaxes(t, 2, 3)
    q, k, v, g, beta = (to_chunks(t) for t in (q, k, v, g, beta))
    gc = jnp.cumsum(g, axis=-1)
    incl = jnp.tril(jnp.ones((CHUNK, CHUNK), dtype=bool))
    strict = jnp.tril(jnp.ones((CHUNK, CHUNK), dtype=bool), -1)
    decay = jnp.exp(jnp.where(incl, gc[..., :, None] - gc[..., None, :], -jnp.inf))
    k_beta = k * beta[..., None]
    lower = jnp.where(strict, jnp.einsum('bnhid,bnhjd->bnhij', k_beta, k) * decay, 0.0)
    eye = jnp.eye(CHUNK, dtype=q.dtype)
    rhs = jnp.concatenate([v * beta[..., None], k_beta * jnp.exp(gc)[..., None]], axis=-1)
    sol = lax.linalg.triangular_solve(eye + lower, rhs, left_side=True, lower=True, unit_diagonal=True)
    u_val, w_dec = sol[..., :dv], sol[..., dv:]
    qk = jnp.einsum('bnhid,bnhjd->bnhij', q, k) * decay
    q_dec = q * jnp.exp(gc)[..., None]
    k_dec = k * jnp.exp(gc[..., -1:] - gc)[..., None]
    g_last = jnp.exp(gc[..., -1])

    def step(state, inp):
        w_n, u_n, qd_n, qk_n, kd_n, gl_n = inp
        v_new = u_n - jnp.einsum('bhcd,bhde->bhce', w_n, state)
        o = jnp.einsum('bhcd,bhde->bhce', qd_n, state) + jnp.einsum('bhij,bhje->bhie', qk_n, v_new)
        state = state * gl_n[..., None, None] + jnp.einsum('bhcd,bhce->bhde', kd_n, v_new)
        return state, o

    xs = tuple(jnp.moveaxis(t, 1, 0) for t in (w_dec, u_val, q_dec, qk, k_dec, g_last))
    state0 = jnp.zeros((bsz, h, dk, dv), dtype=q.dtype)
    _, o = lax.scan(step, state0, xs)
    o = jnp.transpose(o, (1, 0, 3, 2, 4))
    return o.reshape(bsz, s, h, dv)


def gated_deltanet(qkv_raw, beta_raw, a_raw, conv_w, a_log, dt_bias, out_norm_g):
    bsz, s, _ = qkv_raw.shape
    qkv = jax.nn.silu(causal_depthwise_conv(qkv_raw, conv_w)).astype(jnp.float32)
    q, k, v = jnp.split(qkv, 3, axis=-1)
    q = q.reshape(bsz, s, GDN_HEADS, GDN_HEAD_DIM)
    k = k.reshape(bsz, s, GDN_HEADS, GDN_HEAD_DIM)
    v = v.reshape(bsz, s, GDN_HEADS, GDN_HEAD_DIM)
    q = l2norm(q) * (GDN_HEAD_DIM ** -0.5)
    k = l2norm(k)
    beta = jax.nn.sigmoid(beta_raw.astype(jnp.float32))
    g = -jnp.exp(a_log.astype(jnp.float32)) * jax.nn.softplus(a_raw.astype(jnp.float32) + dt_bias.astype(jnp.float32))
    o = chunked_gated_delta_rule(q, k, v, g, beta)
    o = rmsnorm(o, out_norm_g)
    return o.reshape(bsz, s, GDN_WIDTH).astype(qkv_raw.dtype)


def complex_linear_combine(e1, e2):
    a1r, a1i, b1r, b1i = e1
    a2r, a2i, b2r, b2i = e2
    return (a2r * a1r - a2i * a1i,
            a2r * a1i + a2i * a1r,
            a2r * b1r - a2i * b1i + b2r,
            a2r * b1i + a2i * b1r + b2i)


def s5_layer(xb, lam_re, lam_im, log_dt, b_re, b_im, c_re, c_im, d, w_glu):
    f32 = jnp.float32
    bsz, s, _ = xb.shape
    xf = xb.astype(f32)
    xg = xf.reshape(bsz, s, S5_GROUPS, S5_GROUP)
    lr, li = lam_re.astype(f32), lam_im.astype(f32)
    dt = jnp.exp(log_dt.astype(f32))[:, None]
    mag = jnp.exp(lr * dt)
    ab_re, ab_im = mag * jnp.cos(li * dt), mag * jnp.sin(li * dt)
    den = lr * lr + li * li
    nr, ni = ab_re - 1.0, ab_im
    coef_re = (nr * lr + ni * li) / den
    coef_im = (ni * lr - nr * li) / den
    br, bi = b_re.astype(f32), b_im.astype(f32)
    bb_re = coef_re[..., None] * br - coef_im[..., None] * bi
    bb_im = coef_re[..., None] * bi + coef_im[..., None] * br
    bu_re = jnp.einsum('bsgi,gpi->bsgp', xg, bb_re)
    bu_im = jnp.einsum('bsgi,gpi->bsgp', xg, bb_im)
    a_re = jnp.broadcast_to(ab_re, bu_re.shape)
    a_im = jnp.broadcast_to(ab_im, bu_im.shape)
    _, _, h_re, h_im = lax.associative_scan(complex_linear_combine, (a_re, a_im, bu_re, bu_im), axis=1)
    y = jnp.einsum('bsgp,gip->bsgi', h_re, c_re.astype(f32)) - jnp.einsum('bsgp,gip->bsgi', h_im, c_im.astype(f32))
    y = y.reshape(bsz, s, S5_WIDTH) + d.astype(f32) * xf
    y = jax.nn.gelu(y)
    val, gate = jnp.split(y @ w_glu.astype(f32), 2, axis=-1)
    return (val * jax.nn.sigmoid(gate)).astype(xb.dtype)


def memory_cross_attention(q_raw, mem_n, w_kv):
    bsz, s, _ = q_raw.shape
    k, v = jnp.split(mem_n @ w_kv, 2, axis=-1)
    q = q_raw.reshape(bsz, s, XA_HEADS, XA_HEAD_DIM)
    k = k.reshape(bsz, MEM_LEN, XA_HEADS, XA_HEAD_DIM)
    v = v.reshape(bsz, MEM_LEN, XA_HEADS, XA_HEAD_DIM)
    scores = jnp.einsum('bshd,bmhd->bhsm', q, k).astype(jnp.float32) * (XA_HEAD_DIM ** -0.5)
    p = jax.nn.softmax(scores, axis=-1).astype(v.dtype)
    o = jnp.einsum('bhsm,bmhd->bshd', p, v)
    return o.reshape(bsz, s, XA_WIDTH)


def _fwd_setup_inputs(seed: int = 0) -> dict:
    key = jax.random.key(seed)
    ks = jax.random.split(key, 28)
    f32 = jnp.float32
    L, G, P = DEPTH, S5_GROUPS, S5_STATE
    def nrm(k, shape, scale):
        return jax.random.normal(k, shape, f32) * scale
    in_width = sum(IN_SPLITS)
    dt = jnp.exp(jax.random.uniform(ks[6], (L, GDN_HEADS), f32, math.log(1e-3), math.log(1e-1)))
    n_idx = jnp.arange(P, dtype=f32)
    return {
        'x': nrm(ks[0], (BATCH, SEQ, D_MODEL), 1.0),
        'mem': nrm(ks[1], (BATCH, MEM_LEN, D_MODEL), 1.0),
        'norm_g': 1.0 + nrm(ks[2], (L, D_MODEL), 0.02),
        'w_in': nrm(ks[3], (L, D_MODEL, in_width), D_MODEL ** -0.5),
        'conv_w': nrm(ks[4], (L, CONV_WIDTH, 3 * GDN_WIDTH), CONV_WIDTH ** -0.5),
        'gdn_a_log': jnp.log(jax.random.uniform(ks[5], (L, GDN_HEADS), f32, 1.0, 16.0)),
        'gdn_dt_bias': dt + jnp.log(-jnp.expm1(-dt)),
        'gdn_norm_g': 1.0 + nrm(ks[7], (L, GDN_HEAD_DIM), 0.02),
        's5_lambda_re': -0.5 + nrm(ks[8], (L, G, P), 0.01),
        's5_lambda_im': math.pi * n_idx + nrm(ks[9], (L, G, P), 0.01),
        's5_log_dt': jax.random.uniform(ks[10], (L, G), f32, math.log(1e-3), math.log(1e-1)),
        's5_b_re': nrm(ks[11], (L, G, P, S5_GROUP), (2 * S5_GROUP) ** -0.5),
        's5_b_im': nrm(ks[12], (L, G, P, S5_GROUP), (2 * S5_GROUP) ** -0.5),
        's5_c_re': nrm(ks[13], (L, G, S5_GROUP, P), (2 * P) ** -0.5),
        's5_c_im': nrm(ks[14], (L, G, S5_GROUP, P), (2 * P) ** -0.5),
        's5_d': nrm(ks[15], (L, S5_WIDTH), 1.0),
        's5_w_glu': nrm(ks[16], (L, S5_WIDTH, 2 * S5_WIDTH), S5_WIDTH ** -0.5),
        'mem_norm_g': 1.0 + nrm(ks[17], (L, D_MODEL), 0.02),
        'w_kv_mem': nrm(ks[18], (L, D_MODEL, 2 * XA_WIDTH), D_MODEL ** -0.5),
        'w_br_a': nrm(ks[19], (L, GDN_WIDTH, D_MODEL), GDN_WIDTH ** -0.5),
        'w_br_b': nrm(ks[20], (L, S5_WIDTH, D_MODEL), S5_WIDTH ** -0.5),
        'w_br_c': nrm(ks[21], (L, XA_WIDTH, D_MODEL), XA_WIDTH ** -0.5),
        'w_out': nrm(ks[22], (L, D_MODEL, D_MODEL), D_MODEL ** -0.5),
        'final_g': 1.0 + nrm(ks[23], (D_MODEL,), 0.02),
    }


def _fwd_reference(x, mem, norm_g, w_in, conv_w, gdn_a_log, gdn_dt_bias, gdn_norm_g,
              s5_lambda_re, s5_lambda_im, s5_log_dt, s5_b_re, s5_b_im, s5_c_re, s5_c_im, s5_d, s5_w_glu,
              mem_norm_g, w_kv_mem, w_br_a, w_br_b, w_br_c, w_out, final_g):
    bsz, s, _ = x.shape
    h = x
    for l in range(DEPTH):
        u = rmsnorm(h, norm_g[l])
        proj = u @ w_in[l]
        qkv_a, z_a, beta_raw, a_raw, x_b, z_b, q_c, z_c, gate_raw = split_columns(proj, IN_SPLITS)
        o_a = gated_deltanet(qkv_a, beta_raw, a_raw, conv_w[l], gdn_a_log[l], gdn_dt_bias[l], gdn_norm_g[l])
        p_a = (o_a * jax.nn.silu(z_a)) @ w_br_a[l]
        o_b = s5_layer(x_b, s5_lambda_re[l], s5_lambda_im[l], s5_log_dt[l], s5_b_re[l], s5_b_im[l],
                       s5_c_re[l], s5_c_im[l], s5_d[l], s5_w_glu[l])
        p_b = (o_b * jax.nn.silu(z_b)) @ w_br_b[l]
        o_c = memory_cross_attention(q_c, rmsnorm(mem, mem_norm_g[l]), w_kv_mem[l])
        p_c = (o_c * jax.nn.silu(z_c)) @ w_br_c[l]
        gates = jax.nn.sigmoid(gate_raw.astype(jnp.float32)).astype(x.dtype).reshape(bsz, s, N_BRANCHES, D_MODEL)
        merged = gates[:, :, 0] * p_a + gates[:, :, 1] * p_b + gates[:, :, 2] * p_c
        h = h + merged @ w_out[l]
    return rmsnorm(h, final_g)


import jax as _jax
import jax.numpy as _jnp

TWIN_FORMAT = 'train_step'
FWD_PARAMS = ['x', 'mem', 'norm_g', 'w_in', 'conv_w', 'gdn_a_log', 'gdn_dt_bias', 'gdn_norm_g', 's5_lambda_re', 's5_lambda_im', 's5_log_dt', 's5_b_re', 's5_b_im', 's5_c_re', 's5_c_im', 's5_d', 's5_w_glu', 'mem_norm_g', 'w_kv_mem', 'w_br_a', 'w_br_b', 'w_br_c', 'w_out', 'final_g']
TWIN_WEIGHTS = ['norm_g', 'w_in', 'conv_w', 'gdn_a_log', 'gdn_dt_bias', 'gdn_norm_g', 's5_lambda_re', 's5_lambda_im', 's5_log_dt', 's5_b_re', 's5_b_im', 's5_c_re', 's5_c_im', 's5_d', 's5_w_glu', 'mem_norm_g', 'w_kv_mem', 'w_br_a', 'w_br_b', 'w_br_c', 'w_out', 'final_g']
TWIN_DIFF_INPUT = 'x'
TWIN_INPUTS = ['x', 'mem', 'norm_g', 'w_in', 'conv_w', 'gdn_a_log', 'gdn_dt_bias', 'gdn_norm_g', 's5_lambda_re', 's5_lambda_im', 's5_log_dt', 's5_b_re', 's5_b_im', 's5_c_re', 's5_c_im', 's5_d', 's5_w_glu', 'mem_norm_g', 'w_kv_mem', 'w_br_a', 'w_br_b', 'w_br_c', 'w_out', 'final_g', 'loss_target', 'm_norm_g', 'm_w_in', 'm_conv_w', 'm_gdn_a_log', 'm_gdn_dt_bias', 'm_gdn_norm_g', 'm_s5_lambda_re', 'm_s5_lambda_im', 'm_s5_log_dt', 'm_s5_b_re', 'm_s5_b_im', 'm_s5_c_re', 'm_s5_c_im', 'm_s5_d', 'm_s5_w_glu', 'm_mem_norm_g', 'm_w_kv_mem', 'm_w_br_a', 'm_w_br_b', 'm_w_br_c', 'm_w_out', 'm_final_g', 'v_norm_g', 'v_w_in', 'v_conv_w', 'v_gdn_a_log', 'v_gdn_dt_bias', 'v_gdn_norm_g', 'v_s5_lambda_re', 'v_s5_lambda_im', 'v_s5_log_dt', 'v_s5_b_re', 'v_s5_b_im', 'v_s5_c_re', 'v_s5_c_im', 'v_s5_d', 'v_s5_w_glu', 'v_mem_norm_g', 'v_w_kv_mem', 'v_w_br_a', 'v_w_br_b', 'v_w_br_c', 'v_w_out', 'v_final_g']
TWIN_OUTPUTS = ['loss', 'grad_x', 'grad_norm_g', 'grad_w_in', 'grad_conv_w', 'grad_gdn_a_log', 'grad_gdn_dt_bias', 'grad_gdn_norm_g', 'grad_s5_lambda_re', 'grad_s5_lambda_im', 'grad_s5_log_dt', 'grad_s5_b_re', 'grad_s5_b_im', 'grad_s5_c_re', 'grad_s5_c_im', 'grad_s5_d', 'grad_s5_w_glu', 'grad_mem_norm_g', 'grad_w_kv_mem', 'grad_w_br_a', 'grad_w_br_b', 'grad_w_br_c', 'grad_w_out', 'grad_final_g', 'delta_norm_g', 'delta_w_in', 'delta_conv_w', 'delta_gdn_a_log', 'delta_gdn_dt_bias', 'delta_gdn_norm_g', 'delta_s5_lambda_re', 'delta_s5_lambda_im', 'delta_s5_log_dt', 'delta_s5_b_re', 'delta_s5_b_im', 'delta_s5_c_re', 'delta_s5_c_im', 'delta_s5_d', 'delta_s5_w_glu', 'delta_mem_norm_g', 'delta_w_kv_mem', 'delta_w_br_a', 'delta_w_br_b', 'delta_w_br_c', 'delta_w_out', 'delta_final_g', 'new_m_norm_g', 'new_m_w_in', 'new_m_conv_w', 'new_m_gdn_a_log', 'new_m_gdn_dt_bias', 'new_m_gdn_norm_g', 'new_m_s5_lambda_re', 'new_m_s5_lambda_im', 'new_m_s5_log_dt', 'new_m_s5_b_re', 'new_m_s5_b_im', 'new_m_s5_c_re', 'new_m_s5_c_im', 'new_m_s5_d', 'new_m_s5_w_glu', 'new_m_mem_norm_g', 'new_m_w_kv_mem', 'new_m_w_br_a', 'new_m_w_br_b', 'new_m_w_br_c', 'new_m_w_out', 'new_m_final_g', 'new_v_norm_g', 'new_v_w_in', 'new_v_conv_w', 'new_v_gdn_a_log', 'new_v_gdn_dt_bias', 'new_v_gdn_norm_g', 'new_v_s5_lambda_re', 'new_v_s5_lambda_im', 'new_v_s5_log_dt', 'new_v_s5_b_re', 'new_v_s5_b_im', 'new_v_s5_c_re', 'new_v_s5_c_im', 'new_v_s5_d', 'new_v_s5_w_glu', 'new_v_mem_norm_g', 'new_v_w_kv_mem', 'new_v_w_br_a', 'new_v_w_br_b', 'new_v_w_br_c', 'new_v_w_out', 'new_v_final_g']
TWIN_LEAF_KINDS = {'loss': 'loss', 'grad_x': 'grad_x', 'grad_norm_g': 'grad_w', 'grad_w_in': 'grad_w', 'grad_conv_w': 'grad_w', 'grad_gdn_a_log': 'grad_w', 'grad_gdn_dt_bias': 'grad_w', 'grad_gdn_norm_g': 'grad_w', 'grad_s5_lambda_re': 'grad_w', 'grad_s5_lambda_im': 'grad_w', 'grad_s5_log_dt': 'grad_w', 'grad_s5_b_re': 'grad_w', 'grad_s5_b_im': 'grad_w', 'grad_s5_c_re': 'grad_w', 'grad_s5_c_im': 'grad_w', 'grad_s5_d': 'grad_w', 'grad_s5_w_glu': 'grad_w', 'grad_mem_norm_g': 'grad_w', 'grad_w_kv_mem': 'grad_w', 'grad_w_br_a': 'grad_w', 'grad_w_br_b': 'grad_w', 'grad_w_br_c': 'grad_w', 'grad_w_out': 'grad_w', 'grad_final_g': 'grad_w', 'delta_norm_g': 'delta_w', 'delta_w_in': 'delta_w', 'delta_conv_w': 'delta_w', 'delta_gdn_a_log': 'delta_w', 'delta_gdn_dt_bias': 'delta_w', 'delta_gdn_norm_g': 'delta_w', 'delta_s5_lambda_re': 'delta_w', 'delta_s5_lambda_im': 'delta_w', 'delta_s5_log_dt': 'delta_w', 'delta_s5_b_re': 'delta_w', 'delta_s5_b_im': 'delta_w', 'delta_s5_c_re': 'delta_w', 'delta_s5_c_im': 'delta_w', 'delta_s5_d': 'delta_w', 'delta_s5_w_glu': 'delta_w', 'delta_mem_norm_g': 'delta_w', 'delta_w_kv_mem': 'delta_w', 'delta_w_br_a': 'delta_w', 'delta_w_br_b': 'delta_w', 'delta_w_br_c': 'delta_w', 'delta_w_out': 'delta_w', 'delta_final_g': 'delta_w', 'new_m_norm_g': 'new_m', 'new_m_w_in': 'new_m', 'new_m_conv_w': 'new_m', 'new_m_gdn_a_log': 'new_m', 'new_m_gdn_dt_bias': 'new_m', 'new_m_gdn_norm_g': 'new_m', 'new_m_s5_lambda_re': 'new_m', 'new_m_s5_lambda_im': 'new_m', 'new_m_s5_log_dt': 'new_m', 'new_m_s5_b_re': 'new_m', 'new_m_s5_b_im': 'new_m', 'new_m_s5_c_re': 'new_m', 'new_m_s5_c_im': 'new_m', 'new_m_s5_d': 'new_m', 'new_m_s5_w_glu': 'new_m', 'new_m_mem_norm_g': 'new_m', 'new_m_w_kv_mem': 'new_m', 'new_m_w_br_a': 'new_m', 'new_m_w_br_b': 'new_m', 'new_m_w_br_c': 'new_m', 'new_m_w_out': 'new_m', 'new_m_final_g': 'new_m', 'new_v_norm_g': 'new_v', 'new_v_w_in': 'new_v', 'new_v_conv_w': 'new_v', 'new_v_gdn_a_log': 'new_v', 'new_v_gdn_dt_bias': 'new_v', 'new_v_gdn_norm_g': 'new_v', 'new_v_s5_lambda_re': 'new_v', 'new_v_s5_lambda_im': 'new_v', 'new_v_s5_log_dt': 'new_v', 'new_v_s5_b_re': 'new_v', 'new_v_s5_b_im': 'new_v', 'new_v_s5_c_re': 'new_v', 'new_v_s5_c_im': 'new_v', 'new_v_s5_d': 'new_v', 'new_v_s5_w_glu': 'new_v', 'new_v_mem_norm_g': 'new_v', 'new_v_w_kv_mem': 'new_v', 'new_v_w_br_a': 'new_v', 'new_v_w_br_b': 'new_v', 'new_v_w_br_c': 'new_v', 'new_v_w_out': 'new_v', 'new_v_final_g': 'new_v'}


def _forward(args):
    return _fwd_reference(*[args[k] for k in FWD_PARAMS])


def _output_shape():
    def fwd():
        inp = _fwd_setup_inputs(0)
        return _fwd_reference(*[inp[k] for k in FWD_PARAMS])
    out = _jax.eval_shape(fwd)
    return out.shape, out.dtype

N_MICROBATCH = 1
ADAM_LR = 0.001
ADAM_B1 = 0.9
ADAM_B2 = 0.999
ADAM_EPS = 1e-08
ADAM_WD = 0.01
ADAM_STEP = 10
PER_EXAMPLE_BATCH_AXIS = {'x': 0, 'mem': 0, 'loss_target': 0}
SHARED_INPUTS = []
_WEIGHT_DTYPES = {'norm_g': _jnp.float32, 'w_in': _jnp.float32, 'conv_w': _jnp.float32, 'gdn_a_log': _jnp.float32, 'gdn_dt_bias': _jnp.float32, 'gdn_norm_g': _jnp.float32, 's5_lambda_re': _jnp.float32, 's5_lambda_im': _jnp.float32, 's5_log_dt': _jnp.float32, 's5_b_re': _jnp.float32, 's5_b_im': _jnp.float32, 's5_c_re': _jnp.float32, 's5_c_im': _jnp.float32, 's5_d': _jnp.float32, 's5_w_glu': _jnp.float32, 'mem_norm_g': _jnp.float32, 'w_kv_mem': _jnp.float32, 'w_br_a': _jnp.float32, 'w_br_b': _jnp.float32, 'w_br_c': _jnp.float32, 'w_out': _jnp.float32, 'final_g': _jnp.float32}
MOMENT_SCALE = {'norm_g': 4.999869e-02, 'w_in': 1.841670e-02, 'conv_w': 2.888355e-02, 'gdn_a_log': 2.031927e-01, 'gdn_dt_bias': 1.959450e-01, 'gdn_norm_g': 1.115353e-01, 's5_lambda_re': 6.513336e-04, 's5_lambda_im': 7.415508e-04, 's5_log_dt': 3.327718e-01, 's5_b_re': 4.488088e-04, 's5_b_im': 4.561612e-04, 's5_c_re': 9.015034e-04, 's5_c_im': 9.077603e-04, 's5_d': 1.456132e-02, 's5_w_glu': 1.032831e-02, 'mem_norm_g': 4.110623e-03, 'w_kv_mem': 3.994858e-03, 'w_br_a': 2.608367e-02, 'w_br_b': 9.828847e-03, 'w_br_c': 2.854004e-03, 'w_out': 2.804224e-02, 'final_g': 1.597937e+01}


def _to_microbatches(a, axis):
    t = _jnp.moveaxis(a, axis, 0)
    t = t.reshape((N_MICROBATCH, t.shape[0] // N_MICROBATCH) + t.shape[1:])
    return _jnp.moveaxis(t, 1, axis + 1)


def setup_inputs(seed: int = 0) -> dict:
    inp = _fwd_setup_inputs(seed)
    key = _jax.random.fold_in(_jax.random.key(seed), 7919)
    shape, _ = _output_shape()
    out = dict(inp)
    out["loss_target"] = _jax.random.normal(_jax.random.fold_in(key, 0), shape, _jnp.float32)
    for i, name in enumerate(TWIN_WEIGHTS):
        w = inp[name].astype(_jnp.float32)
        if MOMENT_SCALE is None:
            s = _jnp.sqrt(_jnp.mean(_jnp.square(w)) + 1e-30)
        else:
            s = MOMENT_SCALE[name]
        km, kv = _jax.random.split(_jax.random.fold_in(key, i + 1))
        out[name] = w
        out["m_" + name] = s * _jax.random.normal(km, w.shape, _jnp.float32)
        out["v_" + name] = (s * s) * _jax.random.uniform(kv, w.shape, _jnp.float32, 0.5, 1.5)
    if N_MICROBATCH > 1:
        for name, axis in PER_EXAMPLE_BATCH_AXIS.items():
            out[name] = _to_microbatches(out[name], axis)
    return {'x': out['x'], 'mem': out['mem'], 'norm_g': out['norm_g'], 'w_in': out['w_in'], 'conv_w': out['conv_w'], 'gdn_a_log': out['gdn_a_log'], 'gdn_dt_bias': out['gdn_dt_bias'], 'gdn_norm_g': out['gdn_norm_g'], 's5_lambda_re': out['s5_lambda_re'], 's5_lambda_im': out['s5_lambda_im'], 's5_log_dt': out['s5_log_dt'], 's5_b_re': out['s5_b_re'], 's5_b_im': out['s5_b_im'], 's5_c_re': out['s5_c_re'], 's5_c_im': out['s5_c_im'], 's5_d': out['s5_d'], 's5_w_glu': out['s5_w_glu'], 'mem_norm_g': out['mem_norm_g'], 'w_kv_mem': out['w_kv_mem'], 'w_br_a': out['w_br_a'], 'w_br_b': out['w_br_b'], 'w_br_c': out['w_br_c'], 'w_out': out['w_out'], 'final_g': out['final_g'], 'loss_target': out['loss_target'], 'm_norm_g': out['m_norm_g'], 'm_w_in': out['m_w_in'], 'm_conv_w': out['m_conv_w'], 'm_gdn_a_log': out['m_gdn_a_log'], 'm_gdn_dt_bias': out['m_gdn_dt_bias'], 'm_gdn_norm_g': out['m_gdn_norm_g'], 'm_s5_lambda_re': out['m_s5_lambda_re'], 'm_s5_lambda_im': out['m_s5_lambda_im'], 'm_s5_log_dt': out['m_s5_log_dt'], 'm_s5_b_re': out['m_s5_b_re'], 'm_s5_b_im': out['m_s5_b_im'], 'm_s5_c_re': out['m_s5_c_re'], 'm_s5_c_im': out['m_s5_c_im'], 'm_s5_d': out['m_s5_d'], 'm_s5_w_glu': out['m_s5_w_glu'], 'm_mem_norm_g': out['m_mem_norm_g'], 'm_w_kv_mem': out['m_w_kv_mem'], 'm_w_br_a': out['m_w_br_a'], 'm_w_br_b': out['m_w_br_b'], 'm_w_br_c': out['m_w_br_c'], 'm_w_out': out['m_w_out'], 'm_final_g': out['m_final_g'], 'v_norm_g': out['v_norm_g'], 'v_w_in': out['v_w_in'], 'v_conv_w': out['v_conv_w'], 'v_gdn_a_log': out['v_gdn_a_log'], 'v_gdn_dt_bias': out['v_gdn_dt_bias'], 'v_gdn_norm_g': out['v_gdn_norm_g'], 'v_s5_lambda_re': out['v_s5_lambda_re'], 'v_s5_lambda_im': out['v_s5_lambda_im'], 'v_s5_log_dt': out['v_s5_log_dt'], 'v_s5_b_re': out['v_s5_b_re'], 'v_s5_b_im': out['v_s5_b_im'], 'v_s5_c_re': out['v_s5_c_re'], 'v_s5_c_im': out['v_s5_c_im'], 'v_s5_d': out['v_s5_d'], 'v_s5_w_glu': out['v_s5_w_glu'], 'v_mem_norm_g': out['v_mem_norm_g'], 'v_w_kv_mem': out['v_w_kv_mem'], 'v_w_br_a': out['v_w_br_a'], 'v_w_br_b': out['v_w_br_b'], 'v_w_br_c': out['v_w_br_c'], 'v_w_out': out['v_w_out'], 'v_final_g': out['v_final_g']}


def _loss(weights, diff, rest, loss_target):
    with _jax.named_scope("forward"):
        args = {**rest, TWIN_DIFF_INPUT: diff, **{k: w.astype(_WEIGHT_DTYPES[k]) for k, w in weights.items()}}
        y = _forward(args)
    with _jax.named_scope("loss_head"):
        err = _jnp.square(y.astype(_jnp.float32) - loss_target)
        return 0.5 * _jnp.sum(_jnp.mean(err, axis=-1)) if err.ndim else 0.5 * err


def _adamw(w, g, m, v):
    m = ADAM_B1 * m + (1.0 - ADAM_B1) * g
    v = ADAM_B2 * v + (1.0 - ADAM_B2) * _jnp.square(g)
    m_hat = m / (1.0 - ADAM_B1 ** ADAM_STEP)
    v_hat = v / (1.0 - ADAM_B2 ** ADAM_STEP)
    delta = -ADAM_LR * (m_hat / (_jnp.sqrt(v_hat) + ADAM_EPS) + ADAM_WD * w)
    return delta, m, v


def reference(x, mem, norm_g, w_in, conv_w, gdn_a_log, gdn_dt_bias, gdn_norm_g, s5_lambda_re, s5_lambda_im, s5_log_dt, s5_b_re, s5_b_im, s5_c_re, s5_c_im, s5_d, s5_w_glu, mem_norm_g, w_kv_mem, w_br_a, w_br_b, w_br_c, w_out, final_g, loss_target, m_norm_g, m_w_in, m_conv_w, m_gdn_a_log, m_gdn_dt_bias, m_gdn_norm_g, m_s5_lambda_re, m_s5_lambda_im, m_s5_log_dt, m_s5_b_re, m_s5_b_im, m_s5_c_re, m_s5_c_im, m_s5_d, m_s5_w_glu, m_mem_norm_g, m_w_kv_mem, m_w_br_a, m_w_br_b, m_w_br_c, m_w_out, m_final_g, v_norm_g, v_w_in, v_conv_w, v_gdn_a_log, v_gdn_dt_bias, v_gdn_norm_g, v_s5_lambda_re, v_s5_lambda_im, v_s5_log_dt, v_s5_b_re, v_s5_b_im, v_s5_c_re, v_s5_c_im, v_s5_d, v_s5_w_glu, v_mem_norm_g, v_w_kv_mem, v_w_br_a, v_w_br_b, v_w_br_c, v_w_out, v_final_g):
    given = dict(x=x, mem=mem, norm_g=norm_g, w_in=w_in, conv_w=conv_w, gdn_a_log=gdn_a_log, gdn_dt_bias=gdn_dt_bias, gdn_norm_g=gdn_norm_g, s5_lambda_re=s5_lambda_re, s5_lambda_im=s5_lambda_im, s5_log_dt=s5_log_dt, s5_b_re=s5_b_re, s5_b_im=s5_b_im, s5_c_re=s5_c_re, s5_c_im=s5_c_im, s5_d=s5_d, s5_w_glu=s5_w_glu, mem_norm_g=mem_norm_g, w_kv_mem=w_kv_mem, w_br_a=w_br_a, w_br_b=w_br_b, w_br_c=w_br_c, w_out=w_out, final_g=final_g, loss_target=loss_target, m_norm_g=m_norm_g, m_w_in=m_w_in, m_conv_w=m_conv_w, m_gdn_a_log=m_gdn_a_log, m_gdn_dt_bias=m_gdn_dt_bias, m_gdn_norm_g=m_gdn_norm_g, m_s5_lambda_re=m_s5_lambda_re, m_s5_lambda_im=m_s5_lambda_im, m_s5_log_dt=m_s5_log_dt, m_s5_b_re=m_s5_b_re, m_s5_b_im=m_s5_b_im, m_s5_c_re=m_s5_c_re, m_s5_c_im=m_s5_c_im, m_s5_d=m_s5_d, m_s5_w_glu=m_s5_w_glu, m_mem_norm_g=m_mem_norm_g, m_w_kv_mem=m_w_kv_mem, m_w_br_a=m_w_br_a, m_w_br_b=m_w_br_b, m_w_br_c=m_w_br_c, m_w_out=m_w_out, m_final_g=m_final_g, v_norm_g=v_norm_g, v_w_in=v_w_in, v_conv_w=v_conv_w, v_gdn_a_log=v_gdn_a_log, v_gdn_dt_bias=v_gdn_dt_bias, v_gdn_norm_g=v_gdn_norm_g, v_s5_lambda_re=v_s5_lambda_re, v_s5_lambda_im=v_s5_lambda_im, v_s5_log_dt=v_s5_log_dt, v_s5_b_re=v_s5_b_re, v_s5_b_im=v_s5_b_im, v_s5_c_re=v_s5_c_re, v_s5_c_im=v_s5_c_im, v_s5_d=v_s5_d, v_s5_w_glu=v_s5_w_glu, v_mem_norm_g=v_mem_norm_g, v_w_kv_mem=v_w_kv_mem, v_w_br_a=v_w_br_a, v_w_br_b=v_w_br_b, v_w_br_c=v_w_br_c, v_w_out=v_w_out, v_final_g=v_final_g)
    weights = {n: given[n] for n in TWIN_WEIGHTS}
    shared = {n: given[n] for n in SHARED_INPUTS}
    per_example = {n: given[n] for n in ['x', 'mem']}
    grad_fn = _jax.value_and_grad(_loss, argnums=(0, 1))

    def one_microbatch(ex, loss_target):
        ex = dict(ex)
        diff = ex.pop(TWIN_DIFF_INPUT)
        return grad_fn(weights, diff, {**shared, **ex}, loss_target)

    if N_MICROBATCH == 1:
        loss, (grad_w, grad_x) = one_microbatch(per_example, given["loss_target"])
    else:
        def body(carry, xs):
            loss_sum, grad_sum = carry
            l_k, (gw_k, gx_k) = one_microbatch(xs[0], xs[1])
            with _jax.named_scope("update"):
                return (loss_sum + l_k, _jax.tree.map(_jnp.add, grad_sum, gw_k)), gx_k

        init = (_jnp.zeros((), _jnp.float32), _jax.tree.map(_jnp.zeros_like, weights))
        (loss, grad_w), grad_x = _jax.lax.scan(body, init, (per_example, given["loss_target"]))
    with _jax.named_scope("update"):
        delta_w, new_m, new_v = {}, {}, {}
        for n in TWIN_WEIGHTS:
            delta_w[n], new_m[n], new_v[n] = _adamw(weights[n], grad_w[n], given["m_" + n], given["v_" + n])
    return (loss, grad_x, *[grad_w[n] for n in TWIN_WEIGHTS], *[delta_w[n] for n in TWIN_WEIGHTS],
            *[new_m[n] for n in TWIN_WEIGHTS], *[new_v[n] for n in TWIN_WEIGHTS])
```

```python
import functools
import math

import jax
import jax.numpy as jnp
from jax import lax
from jax.experimental import pallas as pl
from jax.experimental.pallas import tpu as pltpu

F32 = jnp.float32
BF16 = jnp.bfloat16
HI = lax.Precision.HIGHEST

EPS = 1e-6
CHUNK = 64
HEAD = 128
NHEAD = 8
XA_HEADS = 4
S5_GROUPS = 64
S5_STATE = 64
S5_GROUP = 16
NSEG = 8
ADAM_LR, ADAM_B1, ADAM_B2, ADAM_EPS, ADAM_WD, ADAM_STEP = 0.001, 0.9, 0.999, 1e-08, 0.01, 10
VMEM_LIMIT = 56 * 2 ** 20


def _cparams(sem=None):
    return pltpu.CompilerParams(dimension_semantics=sem, vmem_limit_bytes=VMEM_LIMIT)


def _sigmoid(x):
    return 1.0 / (1.0 + jnp.exp(-x))


def _silu(x):
    return x * _sigmoid(x)


def _dsilu(x):
    s = _sigmoid(x)
    return s * (1.0 + x * (1.0 - s))


def _softplus(x):
    return jnp.maximum(x, 0.0) + jnp.log(1.0 + jnp.exp(-jnp.abs(x)))


_GELU_C = math.sqrt(2.0 / math.pi)


def _gelu(x):
    return 0.5 * x * (1.0 + jnp.tanh(_GELU_C * (x + 0.044715 * x * x * x)))


def _dgelu(x):
    t = jnp.tanh(_GELU_C * (x + 0.044715 * x * x * x))
    return 0.5 * (1.0 + t) + 0.5 * x * (1.0 - t * t) * _GELU_C * (1.0 + 3.0 * 0.044715 * x * x)


_DIMS = {"nn": (((1,), (0,)), ((), ())), "nt": (((1,), (1,)), ((), ())), "tn": (((0,), (0,)), ((), ()))}


def matmul(a, b, *, mode, out_dtype, tm, tn, tk, name):
    if mode == "nn":
        (m, k), n = a.shape, b.shape[1]
    elif mode == "nt":
        (m, k), n = a.shape, b.shape[0]
    else:
        (k, m), n = a.shape, b.shape[1]
    tm, tn, tk = min(tm, m), min(tn, n), min(tk, k)
    assert m % tm == 0 and n % tn == 0 and k % tk == 0, (name, m, n, k, tm, tn, tk)
    nk = k // tk
    dims = _DIMS[mode]

    def body(a_ref, b_ref, o_ref, *scratch):
        prod = lax.dot_general(a_ref[...].astype(BF16), b_ref[...].astype(BF16), dims, preferred_element_type=F32)
        if nk == 1:
            o_ref[...] = prod.astype(out_dtype)
            return
        acc_ref, = scratch
        kk = pl.program_id(2)

        @pl.when(kk == 0)
        def _():
            acc_ref[...] = prod

        @pl.when(kk > 0)
        def _():
            acc_ref[...] += prod

        @pl.when(kk == nk - 1)
        def _():
            o_ref[...] = acc_ref[...].astype(out_dtype)

    a_spec = pl.BlockSpec((tk, tm), lambda i, j, q: (q, i)) if mode == "tn" else pl.BlockSpec((tm, tk), lambda i, j, q: (i, q))
    b_spec = pl.BlockSpec((tn, tk), lambda i, j, q: (j, q)) if mode == "nt" else pl.BlockSpec((tk, tn), lambda i, j, q: (q, j))
    return pl.pallas_call(
        body, name=name, grid=(m // tm, n // tn, nk),
        in_specs=[a_spec, b_spec], out_specs=pl.BlockSpec((tm, tn), lambda i, j, q: (i, j)),
        out_shape=jax.ShapeDtypeStruct((m, n), out_dtype),
        scratch_shapes=[] if nk == 1 else [pltpu.VMEM((tm, tn), F32)],
        compiler_params=_cparams(("parallel", "parallel", "arbitrary")),
    )(a, b)


def rowwise(fn, ins, outs, *, rows, tr, name, consts=(), reds=()):
    tr = min(tr, rows)
    assert rows % tr == 0, (name, rows, tr)
    n_in, n_c, n_o = len(ins), len(consts), len(outs)

    def body(*refs):
        vals = [r[...] for r in refs[:n_in + n_c]]
        res = fn(*vals)
        o_refs = refs[n_in + n_c:]
        for r, v in zip(o_refs[:n_o], res[:n_o]):
            r[...] = v.astype(r.dtype)
        if reds:
            i = pl.program_id(0)

            @pl.when(i == 0)
            def _():
                for r, v in zip(o_refs[n_o:], res[n_o:]):
                    r[...] = v.astype(r.dtype)

            @pl.when(i > 0)
            def _():
                for r, v in zip(o_refs[n_o:], res[n_o:]):
                    r[...] += v.astype(r.dtype)

    in_specs = [pl.BlockSpec((tr, w), functools.partial(lambda i, cb: (i, cb), cb=cb)) for (_, w, cb) in ins]
    in_specs += [pl.BlockSpec(c.shape, lambda i: (0, 0)) for c in consts]
    out_specs = [pl.BlockSpec((tr, w), lambda i: (i, 0)) for (w, _) in outs]
    out_specs += [pl.BlockSpec(s, lambda i: (0, 0)) for (s, _) in reds]
    out_shape = [jax.ShapeDtypeStruct((rows, w), d) for (w, d) in outs]
    out_shape += [jax.ShapeDtypeStruct(s, d) for (s, d) in reds]
    res = pl.pallas_call(
        body, name=name, grid=(rows // tr,), in_specs=in_specs, out_specs=out_specs, out_shape=out_shape,
        compiler_params=_cparams(("arbitrary",) if reds else ("parallel",)),
    )(*[a for (a, _, _) in ins], *consts)
    return res


def _colsum(x):
    return jnp.sum(x, axis=0, keepdims=True)


def rms_fwd(x, g, name):
    s, d = x.shape

    def fn(xv, gv):
        r = lax.rsqrt(jnp.mean(xv * xv, axis=-1, keepdims=True) + EPS)
        return xv * r * gv, r

    return rowwise(fn, [(x, d, 0)], [(d, BF16), (1, F32)], rows=s, tr=256, name=name, consts=[g])


def rms_bwd_x(du, x, r, g, dh):
    s, d = x.shape

    def fn(duv, xv, rv, dhv, gv):
        dyg = duv * gv
        dx = rv * dyg - xv * (rv * rv * rv) * jnp.mean(dyg * xv, axis=-1, keepdims=True)
        return dhv + dx, _colsum(duv * xv * rv)

    return rowwise(fn, [(du, d, 0), (x, d, 0), (r, 1, 0), (dh, d, 0)], [(d, F32)], rows=s, tr=256,
                   name="rms_bwd_x", consts=[g], reds=[((1, d), F32)])


def rms_bwd_g(du, x, r, name):
    s, d = x.shape

    def fn(duv, xv, rv):
        return (_colsum(duv * xv * rv),)

    return rowwise(fn, [(du, d, 0), (x, d, 0), (r, 1, 0)], [], rows=s, tr=256, name=name, reds=[((1, d), F32)])[0]


def final_stage(x, hres, target, g):
    s, d = x.shape

    def fn(xv, hv, tv, gv):
        h = xv + hv
        r = lax.rsqrt(jnp.mean(h * h, axis=-1, keepdims=True) + EPS)
        y = h * r * gv
        e = y - tv
        loss = 0.5 * jnp.sum(jnp.sum(e * e, axis=-1, keepdims=True), axis=0, keepdims=True) / d
        dy = e / d
        dyg = dy * gv
        dh = r * dyg - h * (r * r * r) * jnp.mean(dyg * h, axis=-1, keepdims=True)
        return dh, dh, loss, _colsum(dy * h * r)

    return rowwise(fn, [(x, d, 0), (hres, d, 0), (target, d, 0)], [(d, F32), (d, BF16)], rows=s, tr=256,
                   name="final_stage", consts=[g], reds=[((1, 1), F32), ((1, d), F32)])


def merge_fwd(pa, pb, pc, proj, gate_cb):
    s, d = pa.shape

    def fn(a, b, c, g0, g1, g2):
        return (_sigmoid(g0) * a + _sigmoid(g1) * b + _sigmoid(g2) * c,)

    ins = [(pa, d, 0), (pb, d, 0), (pc, d, 0)] + [(proj, d, gate_cb + i) for i in range(3)]
    return rowwise(fn, ins, [(d, BF16)], rows=s, tr=256, name="merge_fwd")[0]


def merge_bwd(dm, pa, pb, pc, proj, gate_cb):
    s, d = pa.shape

    def fn(dmv, a, b, c, g0, g1, g2):
        s0, s1, s2 = _sigmoid(g0), _sigmoid(g1), _sigmoid(g2)
        return (dmv * s0, dmv * s1, dmv * s2,
                dmv * a * s0 * (1.0 - s0), dmv * b * s1 * (1.0 - s1), dmv * c * s2 * (1.0 - s2))

    ins = [(dm, d, 0), (pa, d, 0), (pb, d, 0), (pc, d, 0)] + [(proj, d, gate_cb + i) for i in range(3)]
    return rowwise(fn, ins, [(d, BF16)] * 6, rows=s, tr=128, name="merge_bwd")


def gate_fwd(o, proj, z_cb, name):
    s, w = o.shape

    def fn(ov, zv):
        return (ov * _silu(zv),)

    return rowwise(fn, [(o, w, 0), (proj, w, z_cb)], [(w, BF16)], rows=s, tr=512, name=name)[0]


def gate_bwd(dgo, o, proj, z_cb, name):
    s, w = o.shape

    def fn(dv, ov, zv):
        return dv * _silu(zv), dv * ov * _dsilu(zv)

    return rowwise(fn, [(dgo, w, 0), (o, w, 0), (proj, w, z_cb)], [(w, F32), (w, BF16)], rows=s, tr=512, name=name)


def gdn_out_fwd(o_raw, proj, z_cb, gn):
    s, w = o_raw.shape

    def fn(ov, zv, gv):
        outs = []
        for h in range(NHEAD):
            oh = ov[:, h * HEAD:(h + 1) * HEAD]
            r = lax.rsqrt(jnp.mean(oh * oh, axis=-1, keepdims=True) + EPS)
            outs.append(oh * r * gv)
        return (jnp.concatenate(outs, axis=1) * _silu(zv),)

    return rowwise(fn, [(o_raw, w, 0), (proj, w, z_cb)], [(w, BF16)], rows=s, tr=512, name="gdn_out_fwd", consts=[gn])[0]


def gdn_out_bwd(dga, o_raw, proj, z_cb, gn):
    s, w = o_raw.shape

    def fn(dv, ov, zv, gv):
        sz, dsz = _silu(zv), _dsilu(zv)
        do_l, dz_l = [], []
        dg = jnp.zeros((1, HEAD), F32)
        for h in range(NHEAD):
            sl = slice(h * HEAD, (h + 1) * HEAD)
            oh, dgh = ov[:, sl], dv[:, sl]
            r = lax.rsqrt(jnp.mean(oh * oh, axis=-1, keepdims=True) + EPS)
            on = oh * r * gv
            don = dgh * sz[:, sl]
            dz_l.append(dgh * on * dsz[:, sl])
            dg = dg + _colsum(don * oh * r)
            dyg = don * gv
            do_l.append(r * dyg - oh * (r * r * r) * jnp.mean(dyg * oh, axis=-1, keepdims=True))
        return jnp.concatenate(do_l, axis=1), jnp.concatenate(dz_l, axis=1), dg

    return rowwise(fn, [(dga, w, 0), (o_raw, w, 0), (proj, w, z_cb)], [(w, F32), (w, BF16)], rows=s, tr=512,
                   name="gdn_out_bwd", consts=[gn], reds=[((1, HEAD), F32)])


def s5_act_fwd(y_ssm, proj, xb_cb, dvec):
    s, w = y_ssm.shape

    def fn(yv, xv, dv):
        return (_gelu(yv + dv * xv),)

    return rowwise(fn, [(y_ssm, w, 0), (proj, w, xb_cb)], [(w, BF16)], rows=s, tr=512, name="s5_act_fwd", consts=[dvec])[0]


def s5_act_bwd(dyb, y_ssm, proj, xb_cb, dvec):
    s, w = y_ssm.shape

    def fn(dv_, yv, xv, dv):
        dpre = dv_ * _dgelu(yv + dv * xv)
        return dpre, dpre * dv, _colsum(dpre * xv)

    return rowwise(fn, [(dyb, w, 0), (y_ssm, w, 0), (proj, w, xb_cb)], [(w, F32), (w, F32)], rows=s, tr=512,
                   name="s5_act_bwd", consts=[dvec], reds=[((1, w), F32)])


def s5_glu_fwd(glu, proj, z_cb):
    s, w2 = glu.shape
    w = w2 // 2

    def fn(val, gate, zv):
        return (val * _sigmoid(gate) * _silu(zv),)

    return rowwise(fn, [(glu, w, 0), (glu, w, 1), (proj, w, z_cb)], [(w, BF16)], rows=s, tr=512, name="s5_glu_fwd")[0]


def s5_glu_bwd(dgb, glu, proj, z_cb):
    s, w2 = glu.shape
    w = w2 // 2

    def fn(dv, val, gate, zv):
        sg = _sigmoid(gate)
        ob = val * sg
        dob = dv * _silu(zv)
        return dob * sg, dob * val * sg * (1.0 - sg), dv * ob * _dsilu(zv)

    return rowwise(fn, [(dgb, w, 0), (glu, w, 0), (glu, w, 1), (proj, w, z_cb)], [(w, BF16)] * 3, rows=s, tr=512,
                   name="s5_glu_bwd")


def add_cast(a, b, name):
    s, w = a.shape

    def fn(av, bv):
        return (av + bv,)

    return rowwise(fn, [(a, w, 0), (b, w, 0)], [(w, BF16)], rows=s, tr=512, name=name)[0]


QKV_W, QKV_CB = 3072, 0
ZA_CB, XB_CB, ZB_CB, QC_CB, ZC_CB = 3, 4, 5, 6, 7
GATE_CB = 4
BA_CB, BA_W = 112, 128
BA_PAD = 1024
PROJ_W = 14336 + BA_PAD


def _dot(a, b, dims="nn", prec=None):
    if prec is None:
        a, b = a.astype(BF16), b.astype(BF16)
    return lax.dot_general(a, b, _DIMS[dims], preferred_element_type=F32, precision=prec)


def _iota2(shape, dim):
    return lax.broadcasted_iota(jnp.int32, shape, dim)


def _conv_taps(xs, tr, k):
    if k == 0:
        return xs[8:8 + tr]
    return pltpu.roll(xs, k, 0)[8:8 + tr]


def _conv_silu_parts(xv, halo, wv, first):
    tr = xv.shape[0]
    xs = jnp.concatenate([jnp.where(first, 0.0, halo), xv], axis=0)
    taps = [_conv_taps(xs, tr, 3 - j) for j in range(4)]
    c = taps[0] * wv[0:1] + taps[1] * wv[1:2] + taps[2] * wv[2:3] + taps[3] * wv[3:4]
    return taps, c


def gdn_prep_fwd(proj, conv_w, alog_pad, dt_pad):
    s = proj.shape[0]
    tr = min(256, s)
    w = NHEAD * HEAD

    def body(x_ref, halo_ref, ba_ref, w_ref, al_ref, dt_ref, q_ref, k_ref, v_ref, bg_ref, gcol_ref, gt_ref):
        first = pl.program_id(0) == 0
        _, c = _conv_silu_parts(x_ref[...], halo_ref[...], w_ref[...], first)
        sv = _silu(c)
        for h in range(NHEAD):
            sl = slice(h * HEAD, (h + 1) * HEAD)
            qh, kh = sv[:, h * HEAD:(h + 1) * HEAD], sv[:, w + h * HEAD:w + (h + 1) * HEAD]
            q_ref[:, sl] = qh * lax.rsqrt(jnp.sum(qh * qh, axis=-1, keepdims=True) + EPS) * (HEAD ** -0.5)
            k_ref[:, sl] = kh * lax.rsqrt(jnp.sum(kh * kh, axis=-1, keepdims=True) + EPS)
        v_ref[...] = sv[:, 2 * w:]
        ba = ba_ref[...]
        lane = _iota2(ba.shape, 1)
        beta = _sigmoid(ba)
        g = -jnp.exp(al_ref[...]) * _softplus(ba + dt_ref[...])
        bg = jnp.where(lane < NHEAD, beta, jnp.where(lane < 2 * NHEAD, g, 0.0))
        bg_ref[...] = bg
        er, ec = _iota2((BA_W, BA_W), 0), _iota2((BA_W, BA_W), 1)
        expand = jnp.where((er == NHEAD + ec // 8) & (ec < 8 * NHEAD), 1.0, 0.0)
        grep = _dot(bg, expand, prec=HI)
        lr, lc = _iota2((tr, tr), 0), _iota2((tr, tr), 1)
        tril = jnp.where((lr // CHUNK == lc // CHUNK) & (lr >= lc), 1.0, 0.0)
        gc = _dot(tril, grep, prec=HI)
        gcol_ref[...] = gc
        gt_ref[...] = gc.T

    nb8 = tr // 8
    return pl.pallas_call(
        body, name="gdn_prep_fwd", grid=(s // tr,),
        in_specs=[pl.BlockSpec((tr, QKV_W), lambda i: (i, QKV_CB)),
                  pl.BlockSpec((8, QKV_W), lambda i: (jnp.maximum(i * nb8 - 1, 0), QKV_CB)),
                  pl.BlockSpec((tr, BA_W), lambda i: (i, BA_CB)),
                  pl.BlockSpec(conv_w.shape, lambda i: (0, 0)),
                  pl.BlockSpec((1, BA_W), lambda i: (0, 0)), pl.BlockSpec((1, BA_W), lambda i: (0, 0))],
        out_specs=[pl.BlockSpec((tr, w), lambda i: (i, 0))] * 3 + [pl.BlockSpec((tr, BA_W), lambda i: (i, 0))] * 2
        + [pl.BlockSpec((BA_W, tr), lambda i: (0, i))],
        out_shape=[jax.ShapeDtypeStruct((s, w), F32)] * 3 + [jax.ShapeDtypeStruct((s, BA_W), F32)] * 2
        + [jax.ShapeDtypeStruct((BA_W, s), F32)],
        compiler_params=_cparams(("parallel",)),
    )(proj, proj, proj, conv_w, alog_pad, dt_pad)


def _chunk_common(qh, kh, bgv, gcolv, gtv, h):
    beta = bgv[:, h:h + 1]
    gcc = gcolv[:, 8 * h:8 * h + 1]
    gcr = jnp.concatenate([gtv[8 * h:8 * h + 8, :]] * (CHUNK // 8), axis=0)
    ii, jj = _iota2((CHUNK, CHUNK), 0), _iota2((CHUNK, CHUNK), 1)
    incl, strict = ii >= jj, ii > jj
    decay = jnp.where(incl, jnp.exp(jnp.where(incl, gcc - gcr, 0.0)), 0.0)
    gl = gcr[:, CHUNK - 1:CHUNK]
    return beta, gcc, decay, strict, gl


def gdn_intra_fwd(q, k, v, bg, gcol, gt3):
    s, w = q.shape
    n = s // CHUNK

    def body(q_ref, k_ref, v_ref, bg_ref, gcol_ref, gt_ref, u_ref, w_ref, qd_ref, kd_ref, qk_ref, t_ref):
        bgv, gcolv, gtv = bg_ref[...], gcol_ref[...], gt_ref[0]
        ii, jj = _iota2((CHUNK, CHUNK), 0), _iota2((CHUNK, CHUNK), 1)
        eye = jnp.where(ii == jj, 1.0, 0.0)
        for h in range(NHEAD):
            sl = slice(h * HEAD, (h + 1) * HEAD)
            qh, kh, vh = q_ref[:, sl], k_ref[:, sl], v_ref[:, sl]
            beta, gcc, decay, strict, gl = _chunk_common(qh, kh, bgv, gcolv, gtv, h)
            kb = kh * beta
            a = jnp.where(strict, _dot(kb, kh, "nt") * decay, 0.0)
            p = -a
            t = eye + p
            for _ in range(5):
                p = _dot(p, p, prec=HI)
                t = t + _dot(t, p, prec=HI)
            eg = jnp.exp(gcc)
            u_ref[:, sl] = _dot(t, vh * beta, prec=HI)
            w_ref[:, sl] = _dot(t, kb * eg, prec=HI)
            qd_ref[:, sl] = qh * eg
            kd_ref[:, sl] = kh * jnp.exp(gl - gcc)
            qk_ref[0, h] = _dot(qh, kh, "nt") * decay
            t_ref[0, h] = t

    tok = pl.BlockSpec((CHUNK, w), lambda i: (i, 0))
    sm = pl.BlockSpec((CHUNK, BA_W), lambda i: (i, 0))
    sq = pl.BlockSpec((1, NHEAD, CHUNK, CHUNK), lambda i: (i, 0, 0, 0))
    return pl.pallas_call(
        body, name="gdn_intra_fwd", grid=(n,),
        in_specs=[tok, tok, tok, sm, sm, pl.BlockSpec((1, BA_W, CHUNK), lambda i: (i, 0, 0))],
        out_specs=[tok] * 4 + [sq, sq],
        out_shape=[jax.ShapeDtypeStruct((s, w), F32)] * 4 + [jax.ShapeDtypeStruct((n, NHEAD, CHUNK, CHUNK), F32)] * 2,
        compiler_params=_cparams(("parallel",)),
    )(q, k, v, bg, gcol, gt3)


def _state_decay(gtv, h):
    g8 = gtv[8 * h:8 * h + 8, CHUNK - 1:CHUNK]
    return jnp.exp(jnp.concatenate([g8] * (HEAD // 8), axis=0))


def gdn_seq_fwd(u, wd, qd, kd, qk, gt3):
    s, w = u.shape
    n = s // CHUNK

    def body(u_ref, w_ref, qd_ref, kd_ref, qk_ref, gt_ref, o_ref, st_ref, s_ref):
        @pl.when(pl.program_id(0) == 0)
        def _():
            s_ref[...] = jnp.zeros_like(s_ref)

        gtv = gt_ref[0]
        for h in range(NHEAD):
            sl = slice(h * HEAD, (h + 1) * HEAD)
            sh = s_ref[h]
            st_ref[0, h] = sh
            vn = u_ref[:, sl] - _dot(w_ref[:, sl], sh)
            o_ref[:, sl] = _dot(qd_ref[:, sl], sh) + _dot(qk_ref[0, h], vn)
            s_ref[h] = sh * _state_decay(gtv, h) + _dot(kd_ref[:, sl], vn, "tn")

    tok = pl.BlockSpec((CHUNK, w), lambda i: (i, 0))
    return pl.pallas_call(
        body, name="gdn_seq_fwd", grid=(n,),
        in_specs=[tok] * 4 + [pl.BlockSpec((1, NHEAD, CHUNK, CHUNK), lambda i: (i, 0, 0, 0)),
                              pl.BlockSpec((1, BA_W, CHUNK), lambda i: (i, 0, 0))],
        out_specs=[tok, pl.BlockSpec((1, NHEAD, HEAD, HEAD), lambda i: (i, 0, 0, 0))],
        out_shape=[jax.ShapeDtypeStruct((s, w), F32), jax.ShapeDtypeStruct((n, NHEAD, HEAD, HEAD), F32)],
        scratch_shapes=[pltpu.VMEM((NHEAD, HEAD, HEAD), F32)],
        compiler_params=_cparams(("arbitrary",)),
    )(u, wd, qd, kd, qk, gt3)


def gdn_seq_bwd(do, u, wd, qd, kd, qk, gt3, states):
    s, w = u.shape
    n = s // CHUNK

    def body(do_ref, u_ref, w_ref, qd_ref, kd_ref, qk_ref, gt_ref, st_ref,
             du_ref, dw_ref, dqd_ref, dkd_ref, dqk_ref, dgl_ref, ds_ref):
        @pl.when(pl.program_id(0) == 0)
        def _():
            ds_ref[...] = jnp.zeros_like(ds_ref)

        gtv = gt_ref[0]
        dgl_rows = []
        for h in range(NHEAD):
            sl = slice(h * HEAD, (h + 1) * HEAD)
            sh, dsp, doh = st_ref[0, h], ds_ref[h], do_ref[:, sl]
            wh, qdh, kdh, qkh = w_ref[:, sl], qd_ref[:, sl], kd_ref[:, sl], qk_ref[0, h]
            vn = u_ref[:, sl] - _dot(wh, sh)
            dvn = _dot(qkh, doh, "tn") + _dot(kdh, dsp)
            du_ref[:, sl] = dvn
            dw_ref[:, sl] = -_dot(dvn, sh, "nt")
            dqd_ref[:, sl] = _dot(doh, sh, "nt")
            dkd_ref[:, sl] = _dot(vn, dsp, "nt")
            dqk_ref[0, h] = _dot(doh, vn, "nt")
            dgl_rows.append(_colsum(sh * dsp))
            ds_ref[h] = dsp * _state_decay(gtv, h) + _dot(qdh, doh, "tn") - _dot(wh, dvn, "tn")
        dgl_ref[0] = jnp.concatenate(dgl_rows, axis=0)

    tok = pl.BlockSpec((CHUNK, w), lambda i: (n - 1 - i, 0))
    sq = pl.BlockSpec((1, NHEAD, CHUNK, CHUNK), lambda i: (n - 1 - i, 0, 0, 0))
    return pl.pallas_call(
        body, name="gdn_seq_bwd", grid=(n,),
        in_specs=[tok] * 5 + [sq, pl.BlockSpec((1, BA_W, CHUNK), lambda i: (n - 1 - i, 0, 0)),
                              pl.BlockSpec((1, NHEAD, HEAD, HEAD), lambda i: (n - 1 - i, 0, 0, 0))],
        out_specs=[tok] * 4 + [sq, pl.BlockSpec((1, NHEAD, HEAD), lambda i: (n - 1 - i, 0, 0))],
        out_shape=[jax.ShapeDtypeStruct((s, w), F32)] * 4 + [jax.ShapeDtypeStruct((n, NHEAD, CHUNK, CHUNK), F32),
                                                            jax.ShapeDtypeStruct((n, NHEAD, HEAD), F32)],
        scratch_shapes=[pltpu.VMEM((NHEAD, HEAD, HEAD), F32)],
        compiler_params=_cparams(("arbitrary",)),
    )(do, u, wd, qd, kd, qk, gt3, states)


def gdn_intra_bwd(q, k, v, bg, gcol, gt3, tinv, du, dw, dqd, dkd, dqk, dgl):
    s, w = q.shape
    n = s // CHUNK

    def body(q_ref, k_ref, v_ref, bg_ref, gcol_ref, gt_ref, t_ref, du_ref, dw_ref, dqd_ref, dkd_ref, dqk_ref, dgl_ref,
             dq_ref, dk_ref, dv_ref, dbg_ref):
        bgv, gcolv, gtv, dglv = bg_ref[...], gcol_ref[...], gt_ref[0], dgl_ref[0]
        ii, jj = _iota2((CHUNK, CHUNK), 0), _iota2((CHUNK, CHUNK), 1)
        triu = jnp.where(ii <= jj, 1.0, 0.0)
        ones = jnp.ones((CHUNK, BA_W), F32)
        lane = _iota2((CHUNK, BA_W), 1)
        row = _iota2((CHUNK, 1), 0)
        dbg = jnp.zeros((CHUNK, BA_W), F32)
        for h in range(NHEAD):
            sl = slice(h * HEAD, (h + 1) * HEAD)
            qh, kh, vh = q_ref[:, sl], k_ref[:, sl], v_ref[:, sl]
            beta, gcc, decay, strict, gl = _chunk_common(qh, kh, bgv, gcolv, gtv, h)
            t = t_ref[0, h]
            duh, dwh, dqdh, dkdh, dqkh = du_ref[:, sl], dw_ref[:, sl], dqd_ref[:, sl], dkd_ref[:, sl], dqk_ref[0, h]
            kb = kh * beta
            eg = jnp.exp(gcc)
            ekd = jnp.exp(gl - gcc)
            rv, rk = vh * beta, kb * eg
            m = _dot(kb, kh, "nt")
            p = _dot(qh, kh, "nt")
            drv = _dot(t, duh, "tn", prec=HI)
            drk = _dot(t, dwh, "tn", prec=HI)
            dt = _dot(duh, rv, "nt", prec=HI) + _dot(dwh, rk, "nt", prec=HI)
            da = jnp.where(strict, -_dot(_dot(t, dt, "tn", prec=HI), t, "nt", prec=HI), 0.0)
            dm = da * decay
            dpm = dqkh * decay
            dkb = _dot(dm, kh) + drk * eg
            dq = _dot(dpm, kh) + dqdh * eg
            dk = _dot(dm, kb, "tn") + _dot(dpm, qh, "tn") + dkdh * ekd + dkb * beta
            e = (da * m + dqkh * p) * decay
            sk = jnp.sum(dkdh * kh * ekd, axis=-1, keepdims=True)
            dgc = (jnp.sum(e, axis=-1, keepdims=True) - _dot(e, ones, "tn", prec=HI)[:, 0:1]
                   + jnp.sum(dqdh * qh * eg, axis=-1, keepdims=True) - sk + jnp.sum(drk * rk, axis=-1, keepdims=True))
            dglast = jnp.sum(sk, axis=0, keepdims=True) + jnp.sum(dglv[h:h + 1, :], axis=-1, keepdims=True) * jnp.exp(gl)
            dgc = dgc + jnp.where(row == CHUNK - 1, dglast, 0.0)
            dg = _dot(triu, dgc * ones, prec=HI)
            dbeta = jnp.sum(dkb * kh, axis=-1, keepdims=True) + jnp.sum(drv * vh, axis=-1, keepdims=True)
            dbg = dbg + jnp.where(lane == h, dbeta, 0.0) + jnp.where(lane == NHEAD + h, dg, 0.0)
            dq_ref[:, sl] = dq
            dk_ref[:, sl] = dk
            dv_ref[:, sl] = drv * beta
        dbg_ref[...] = dbg

    tok = pl.BlockSpec((CHUNK, w), lambda i: (i, 0))
    sm = pl.BlockSpec((CHUNK, BA_W), lambda i: (i, 0))
    sq = pl.BlockSpec((1, NHEAD, CHUNK, CHUNK), lambda i: (i, 0, 0, 0))
    return pl.pallas_call(
        body, name="gdn_intra_bwd", grid=(n,),
        in_specs=[tok, tok, tok, sm, sm, pl.BlockSpec((1, BA_W, CHUNK), lambda i: (i, 0, 0)), sq,
                  tok, tok, tok, tok, sq, pl.BlockSpec((1, NHEAD, HEAD), lambda i: (i, 0, 0))],
        out_specs=[tok] * 3 + [sm],
        out_shape=[jax.ShapeDtypeStruct((s, w), F32)] * 3 + [jax.ShapeDtypeStruct((s, BA_W), F32)],
        compiler_params=_cparams(("parallel",)),
    )(q, k, v, bg, gcol, gt3, tinv, du, dw, dqd, dkd, dqk, dgl)


def gdn_prep_bwd1(proj, conv_w, alog_pad, dt_pad, dq, dk, dv, dbg):
    s = proj.shape[0]
    tr = min(256, s)
    w = NHEAD * HEAD
    pad_w = BA_PAD

    def body(x_ref, halo_ref, ba_ref, w_ref, al_ref, dt_ref, dq_ref, dk_ref, dv_ref, dbg_ref,
             dc_ref, dba_ref, dw0_ref, dw1_ref, dw2_ref, dw3_ref, dal_ref, ddt_ref):
        i = pl.program_id(0)
        taps, c = _conv_silu_parts(x_ref[...], halo_ref[...], w_ref[...], i == 0)
        sv, dsv = _silu(c), _dsilu(c)
        for h in range(NHEAD):
            for base, d_ref, scale in ((0, dq_ref, HEAD ** -0.5), (w, dk_ref, 1.0)):
                sl = slice(base + h * HEAD, base + (h + 1) * HEAD)
                sh = sv[:, sl]
                dn = d_ref[:, h * HEAD:(h + 1) * HEAD]
                r = lax.rsqrt(jnp.sum(sh * sh, axis=-1, keepdims=True) + EPS)
                dsh = scale * (r * dn - sh * (r * r * r) * jnp.sum(dn * sh, axis=-1, keepdims=True))
                dc_ref[:, sl] = dsh * dsv[:, sl]
        dc_ref[:, 2 * w:] = dv_ref[...] * dsv[:, 2 * w:]
        dc = dc_ref[...]
        ba, dbgv = ba_ref[...], dbg_ref[...]
        lane = _iota2(ba.shape, 1)
        beta = _sigmoid(ba)
        ea = jnp.exp(al_ref[...])
        z = ba + dt_ref[...]
        g = -ea * _softplus(z)
        is_g = (lane >= NHEAD) & (lane < 2 * NHEAD)
        da_raw = jnp.where(is_g, dbgv * (-ea) * _sigmoid(z), 0.0)
        dba = jnp.where(lane < NHEAD, dbgv * beta * (1.0 - beta), da_raw)
        dba_ref[...] = jnp.concatenate([dba, jnp.zeros((tr, pad_w - BA_W), F32)], axis=1).astype(BF16)
        partial = [_colsum(dc * tp) for tp in taps] + [_colsum(jnp.where(is_g, dbgv * g, 0.0)), _colsum(da_raw)]
        red_refs = (dw0_ref, dw1_ref, dw2_ref, dw3_ref, dal_ref, ddt_ref)

        @pl.when(i == 0)
        def _():
            for r_, v_ in zip(red_refs, partial):
                r_[...] = v_

        @pl.when(i > 0)
        def _():
            for r_, v_ in zip(red_refs, partial):
                r_[...] += v_

    nb8 = tr // 8
    tok = pl.BlockSpec((tr, w), lambda i: (i, 0))
    one = lambda width: pl.BlockSpec((1, width), lambda i: (0, 0))
    return pl.pallas_call(
        body, name="gdn_prep_bwd1", grid=(s // tr,),
        in_specs=[pl.BlockSpec((tr, QKV_W), lambda i: (i, QKV_CB)),
                  pl.BlockSpec((8, QKV_W), lambda i: (jnp.maximum(i * nb8 - 1, 0), QKV_CB)),
                  pl.BlockSpec((tr, BA_W), lambda i: (i, BA_CB)),
                  pl.BlockSpec(conv_w.shape, lambda i: (0, 0)), one(BA_W), one(BA_W),
                  tok, tok, tok, pl.BlockSpec((tr, BA_W), lambda i: (i, 0))],
        out_specs=[pl.BlockSpec((tr, QKV_W), lambda i: (i, 0)), pl.BlockSpec((tr, pad_w), lambda i: (i, 0))]
        + [one(QKV_W)] * 4 + [one(BA_W)] * 2,
        out_shape=[jax.ShapeDtypeStruct((s, QKV_W), F32), jax.ShapeDtypeStruct((s, pad_w), BF16)]
        + [jax.ShapeDtypeStruct((1, QKV_W), F32)] * 4 + [jax.ShapeDtypeStruct((1, BA_W), F32)] * 2,
        compiler_params=_cparams(("arbitrary",)),
    )(proj, proj, proj, conv_w, alog_pad, dt_pad, dq, dk, dv, dbg)


def gdn_prep_bwd2(dc, conv_w):
    s = dc.shape[0]
    tr = min(256, s)
    nblk = s // tr
    nb8 = tr // 8

    def body(dc_ref, halo_ref, w_ref, o_ref):
        last = pl.program_id(0) == nblk - 1
        wv = w_ref[...]
        xs = jnp.concatenate([dc_ref[...], jnp.where(last, 0.0, halo_ref[...])], axis=0)
        acc = xs[:tr] * wv[3:4]
        for j in range(3):
            acc = acc + pltpu.roll(xs, tr + 8 - (3 - j), 0)[:tr] * wv[j:j + 1]
        o_ref[...] = acc.astype(BF16)

    return pl.pallas_call(
        body, name="gdn_prep_bwd2", grid=(nblk,),
        in_specs=[pl.BlockSpec((tr, QKV_W), lambda i: (i, 0)),
                  pl.BlockSpec((8, QKV_W), lambda i: (jnp.minimum((i + 1) * nb8, s // 8 - 1), 0)),
                  pl.BlockSpec(conv_w.shape, lambda i: (0, 0))],
        out_specs=pl.BlockSpec((tr, QKV_W), lambda i: (i, 0)),
        out_shape=jax.ShapeDtypeStruct((s, QKV_W), BF16),
        compiler_params=_cparams(("parallel",)),
    )(dc, dc, conv_w)


S5_W = S5_GROUPS * S5_STATE
S5_IN = S5_GROUPS * S5_GROUP
S5_TILES = 8
S5_TW, S5_TI = S5_W // S5_TILES, S5_IN // S5_TILES


def _s5_param_math(lr, li, ldt, br, bi):
    pr, pc = _iota2((S5_STATE, S5_STATE * S5_GROUP), 0), _iota2((S5_STATE, S5_STATE * S5_GROUP), 1)
    rep = jnp.where(pc // S5_GROUP == pr, 1.0, 0.0)
    dt = jnp.exp(ldt)
    mag = jnp.exp(lr * dt)
    ab_re, ab_im = mag * jnp.cos(li * dt), mag * jnp.sin(li * dt)
    den = lr * lr + li * li
    nr, ni = ab_re - 1.0, ab_im
    coef_re = (nr * lr + ni * li) / den
    coef_im = (ni * lr - nr * li) / den
    cr, ci = _dot(coef_re, rep, prec=HI), _dot(coef_im, rep, prec=HI)
    return ab_re, ab_im, cr * br - ci * bi, cr * bi + ci * br


def s5_param_fwd(lr, li, ldt, br, bi):
    def body(lr_ref, li_ref, ldt_ref, br_ref, bi_ref, ar_ref, ai_ref, bbr_ref, bbi_ref):
        res = _s5_param_math(lr_ref[...], li_ref[...], ldt_ref[...], br_ref[...], bi_ref[...])
        for r, v in zip((ar_ref, ai_ref, bbr_ref, bbi_ref), res):
            r[...] = v

    return pl.pallas_call(
        body, name="s5_param_fwd",
        out_shape=[jax.ShapeDtypeStruct(lr.shape, F32)] * 2 + [jax.ShapeDtypeStruct(br.shape, F32)] * 2,
        compiler_params=_cparams(),
    )(lr, li, ldt, br, bi)


def s5_param_bwd(lr, li, ldt, br, bi, dar, dai, dbbr, dbbi):
    def body(lr_ref, li_ref, ldt_ref, br_ref, bi_ref, dar_ref, dai_ref, dbbr_ref, dbbi_ref, *out_refs):
        _, vjp = jax.vjp(_s5_param_math, lr_ref[...], li_ref[...], ldt_ref[...], br_ref[...], bi_ref[...])
        for r, v in zip(out_refs, vjp((dar_ref[...], dai_ref[...], dbbr_ref[...], dbbi_ref[...]))):
            r[...] = v

    return pl.pallas_call(
        body, name="s5_param_bwd",
        out_shape=[jax.ShapeDtypeStruct(a.shape, F32) for a in (lr, li, ldt, br, bi)],
        compiler_params=_cparams(),
    )(lr, li, ldt, br, bi, dar, dai, dbbr, dbbi)


def _cmul(ar, ai, br, bi):
    return ar * br - ai * bi, ar * bi + ai * br


def _s5_power(ar, ai, steps):
    assert steps & (steps - 1) == 0
    for _ in range(steps.bit_length() - 1):
        ar, ai = _cmul(ar, ai, ar, ai)
    return ar, ai


def _s5_scan_rows(ar_ref, ai_ref, re_ref, im_ref, sr_ref, si_ref, tb, row0, reverse):
    quarter = S5_W // 4
    for qd in range(4):
        cs = slice(qd * quarter, (qd + 1) * quarter)
        are = jnp.broadcast_to(ar_ref[:, cs], (NSEG, quarter))
        aim = jnp.broadcast_to(ai_ref[:, cs], (NSEG, quarter))
        if reverse:
            aim = -aim

        def step(t, carry):
            h_r, h_i = carry
            tt = tb - 1 - t if reverse else t
            rows = pl.ds(pl.multiple_of(row0 + tt * NSEG, NSEG), NSEG)
            n_r = are * h_r - aim * h_i + re_ref[rows, cs]
            n_i = are * h_i + aim * h_r + im_ref[rows, cs]
            re_ref[rows, cs] = n_r
            im_ref[rows, cs] = n_i
            return n_r, n_i

        h_r, h_i = lax.fori_loop(0, tb, step, (sr_ref[:, cs], si_ref[:, cs]), unroll=4)
        sr_ref[:, cs] = h_r
        si_ref[:, cs] = h_i


def _s5_segment_carry(ar_ref, ai_ref, sr_ref, si_ref, steps, reverse):
    pr, pi = _s5_power(ar_ref[...], ai_ref[...], steps)
    if reverse:
        pi = -pi
    cur_r = jnp.zeros((1, S5_W), F32)
    cur_i = jnp.zeros((1, S5_W), F32)
    for s in (range(NSEG - 1, -1, -1) if reverse else range(NSEG)):
        e_r, e_i = sr_ref[s:s + 1, :], si_ref[s:s + 1, :]
        sr_ref[s:s + 1, :] = cur_r
        si_ref[s:s + 1, :] = cur_i
        nr, ni = _cmul(pr, pi, cur_r, cur_i)
        cur_r, cur_i = nr + e_r, ni + e_i


def _s5_blocks(s):
    steps = s // NSEG
    tb = min(32, steps)
    return steps, tb, NSEG * tb, steps // tb


def s5_scan_fwd(xp, a_re, a_im, bre, bim, cre, cim):
    s = xp.shape[0]
    steps, tb, rb, nb = _s5_blocks(s)

    def body(x_ref, ar_ref, ai_ref, bre_ref, bim_ref, cre_ref, cim_ref, y_ref, hsr_ref, hsi_ref,
             hr_ref, hi_ref, sr_ref, si_ref):
        ph, b = pl.program_id(0), pl.program_id(1)

        @pl.when((ph == 0) & (b == 0))
        def _():
            sr_ref[...] = jnp.zeros_like(sr_ref)
            si_ref[...] = jnp.zeros_like(si_ref)

        @pl.when((ph == 1) & (b == 0))
        def _():
            _s5_segment_carry(ar_ref, ai_ref, sr_ref, si_ref, steps, False)

        xv = x_ref[...].astype(BF16)
        for j in range(S5_TILES):
            xs = xv[:, j * S5_TI:(j + 1) * S5_TI]
            hr_ref[:, j * S5_TW:(j + 1) * S5_TW] = _dot(xs, bre_ref[j])
            hi_ref[:, j * S5_TW:(j + 1) * S5_TW] = _dot(xs, bim_ref[j])

        @pl.when(ph == 1)
        def _():
            hsr_ref[0] = sr_ref[...]
            hsi_ref[0] = si_ref[...]

        _s5_scan_rows(ar_ref, ai_ref, hr_ref, hi_ref, sr_ref, si_ref, tb, 0, False)

        @pl.when(ph == 1)
        def _():
            for j in range(S5_TILES):
                cs = slice(j * S5_TW, (j + 1) * S5_TW)
                y_ref[:, j * S5_TI:(j + 1) * S5_TI] = _dot(hr_ref[:, cs], cre_ref[j]) - _dot(hi_ref[:, cs], cim_ref[j])

    row = pl.BlockSpec((1, S5_W), lambda p, b: (0, 0))
    wb = pl.BlockSpec((S5_TILES, S5_TI, S5_TW), lambda p, b: (0, 0, 0))
    wc = pl.BlockSpec((S5_TILES, S5_TW, S5_TI), lambda p, b: (0, 0, 0))
    st = pl.BlockSpec((1, NSEG, S5_W), lambda p, b: (p * b, 0, 0))
    return pl.pallas_call(
        body, name="s5_scan_fwd", grid=(2, nb),
        in_specs=[pl.BlockSpec((rb, S5_IN), lambda p, b: (b, 0)), row, row, wb, wb, wc, wc],
        out_specs=[pl.BlockSpec((rb, S5_IN), lambda p, b: (p * b, 0)), st, st],
        out_shape=[jax.ShapeDtypeStruct((s, S5_IN), F32)] + [jax.ShapeDtypeStruct((nb, NSEG, S5_W), F32)] * 2,
        scratch_shapes=[pltpu.VMEM((rb, S5_W), F32)] * 2 + [pltpu.VMEM((NSEG, S5_W), F32)] * 2,
        compiler_params=_cparams(("arbitrary", "arbitrary")),
    )(xp, a_re, a_im, bre, bim, cre, cim)


def s5_scan_bwd(dyp, xp, a_re, a_im, bre, bim, cre_t, cim_t, hs_r, hs_i):
    s = xp.shape[0]
    steps, tb, rb, nb = _s5_blocks(s)

    def body(dy_ref, x_ref, ar_ref, ai_ref, bre_ref, bim_ref, crt_ref, cit_ref, hsr_ref, hsi_ref,
             dx_ref, dar_ref, dai_ref, dbr_ref, dbi_ref, dcr_ref, dci_ref,
             hr_ref, hi_ref, lr_ref, li_ref, sr_ref, si_ref, fr_ref, fi_ref, accr_ref, acci_ref):
        ph, b = pl.program_id(0), pl.program_id(1)

        @pl.when((ph == 0) & (b == 0))
        def _():
            sr_ref[...] = jnp.zeros_like(sr_ref)
            si_ref[...] = jnp.zeros_like(si_ref)

        @pl.when((ph == 1) & (b == 0))
        def _():
            _s5_segment_carry(ar_ref, ai_ref, sr_ref, si_ref, steps, True)
            for r in (accr_ref, acci_ref, dbr_ref, dbi_ref, dcr_ref, dci_ref):
                r[...] = jnp.zeros_like(r)

        dyv = dy_ref[...].astype(BF16)
        for j in range(S5_TILES):
            ds_ = dyv[:, j * S5_TI:(j + 1) * S5_TI]
            lr_ref[:, j * S5_TW:(j + 1) * S5_TW] = _dot(ds_, crt_ref[j])
            li_ref[:, j * S5_TW:(j + 1) * S5_TW] = -_dot(ds_, cit_ref[j])
        _s5_scan_rows(ar_ref, ai_ref, lr_ref, li_ref, sr_ref, si_ref, tb, 0, True)

        @pl.when(ph == 1)
        def _():
            xv = x_ref[...].astype(BF16)
            for j in range(S5_TILES):
                xs = xv[:, j * S5_TI:(j + 1) * S5_TI]
                hr_ref[NSEG:, j * S5_TW:(j + 1) * S5_TW] = _dot(xs, bre_ref[j])
                hi_ref[NSEG:, j * S5_TW:(j + 1) * S5_TW] = _dot(xs, bim_ref[j])
            hr_ref[0:NSEG, :] = hsr_ref[0]
            hi_ref[0:NSEG, :] = hsi_ref[0]
            fr_ref[...] = hsr_ref[0]
            fi_ref[...] = hsi_ref[0]
            _s5_scan_rows(ar_ref, ai_ref, hr_ref, hi_ref, fr_ref, fi_ref, tb, NSEG, False)
            lam_r, lam_i = lr_ref[...], li_ref[...]
            hp_r, hp_i = hr_ref[0:rb, :], hi_ref[0:rb, :]
            accr_ref[...] += jnp.sum((lam_r * hp_r + lam_i * hp_i).reshape(tb, NSEG, S5_W), axis=0)
            acci_ref[...] += jnp.sum((lam_i * hp_r - lam_r * hp_i).reshape(tb, NSEG, S5_W), axis=0)
            lam_rb, lam_ib = lam_r.astype(BF16), lam_i.astype(BF16)
            h_rb, h_ib = hr_ref[NSEG:, :].astype(BF16), hi_ref[NSEG:, :].astype(BF16)
            for j in range(S5_TILES):
                cs, ci = slice(j * S5_TW, (j + 1) * S5_TW), slice(j * S5_TI, (j + 1) * S5_TI)
                dbr_ref[j] += _dot(xv[:, ci], lam_rb[:, cs], "tn")
                dbi_ref[j] += _dot(xv[:, ci], lam_ib[:, cs], "tn")
                dx_ref[:, ci] = _dot(lam_rb[:, cs], bre_ref[j], "nt") + _dot(lam_ib[:, cs], bim_ref[j], "nt")
                dcr_ref[j] += _dot(h_rb[:, cs], dyv[:, ci], "tn")
                dci_ref[j] -= _dot(h_ib[:, cs], dyv[:, ci], "tn")

        @pl.when((ph == 1) & (b == nb - 1))
        def _():
            dar_ref[...] = jnp.sum(accr_ref[...], axis=0, keepdims=True)
            dai_ref[...] = jnp.sum(acci_ref[...], axis=0, keepdims=True)

    rev = lambda p, b: (nb - 1 - b, 0)
    row = pl.BlockSpec((1, S5_W), lambda p, b: (0, 0))
    wb = pl.BlockSpec((S5_TILES, S5_TI, S5_TW), lambda p, b: (0, 0, 0))
    wc = pl.BlockSpec((S5_TILES, S5_TW, S5_TI), lambda p, b: (0, 0, 0))
    st = pl.BlockSpec((1, NSEG, S5_W), lambda p, b: (nb - 1 - b, 0, 0))
    big = pltpu.VMEM((rb, S5_W), F32)
    big8 = pltpu.VMEM((rb + NSEG, S5_W), F32)
    small = pltpu.VMEM((NSEG, S5_W), F32)
    return pl.pallas_call(
        body, name="s5_scan_bwd", grid=(2, nb),
        in_specs=[pl.BlockSpec((rb, S5_IN), rev), pl.BlockSpec((rb, S5_IN), rev), row, row, wb, wb, wb, wb, st, st],
        out_specs=[pl.BlockSpec((rb, S5_IN), lambda p, b: (nb - 1 - p * b, 0)), row, row, wb, wb, wc, wc],
        out_shape=[jax.ShapeDtypeStruct((s, S5_IN), F32)] + [jax.ShapeDtypeStruct((1, S5_W), F32)] * 2
        + [jax.ShapeDtypeStruct((S5_TILES, S5_TI, S5_TW), F32)] * 2 + [jax.ShapeDtypeStruct((S5_TILES, S5_TW, S5_TI), F32)] * 2,
        scratch_shapes=[big8, big8, big, big, small, small, small, small, small, small],
        compiler_params=_cparams(("arbitrary", "arbitrary")),
    )(dyp, xp, a_re, a_im, bre, bim, cre_t, cim_t, hs_r, hs_i)


XA_DIM = 256
XA_W = XA_HEADS * XA_DIM


def _xa_probs(qh, kh):
    sc = _dot(qh, kh, "nt") * (XA_DIM ** -0.5)
    ex = jnp.exp(sc - jnp.max(sc, axis=-1, keepdims=True))
    return ex / jnp.sum(ex, axis=-1, keepdims=True)


def xa_fwd(proj, kv):
    s = proj.shape[0]
    tq = min(512, s)

    def body(q_ref, kv_ref, o_ref):
        for h in range(XA_HEADS):
            sl = slice(h * XA_DIM, (h + 1) * XA_DIM)
            p = _xa_probs(q_ref[:, sl], kv_ref[:, sl])
            o_ref[:, sl] = _dot(p, kv_ref[:, XA_W + h * XA_DIM:XA_W + (h + 1) * XA_DIM])

    return pl.pallas_call(
        body, name="xa_fwd", grid=(s // tq,),
        in_specs=[pl.BlockSpec((tq, XA_W), lambda i: (i, QC_CB)), pl.BlockSpec(kv.shape, lambda i: (0, 0))],
        out_specs=pl.BlockSpec((tq, XA_W), lambda i: (i, 0)),
        out_shape=jax.ShapeDtypeStruct((s, XA_W), F32),
        compiler_params=_cparams(("parallel",)),
    )(proj, kv)


def xa_bwd(do, proj, kv):
    s = proj.shape[0]
    tq = min(512, s)

    def body(do_ref, q_ref, kv_ref, dq_ref, dkv_ref):
        @pl.when(pl.program_id(0) == 0)
        def _():
            dkv_ref[...] = jnp.zeros_like(dkv_ref)

        for h in range(XA_HEADS):
            sl = slice(h * XA_DIM, (h + 1) * XA_DIM)
            sv = slice(XA_W + h * XA_DIM, XA_W + (h + 1) * XA_DIM)
            qh, kh, vh, doh = q_ref[:, sl], kv_ref[:, sl], kv_ref[:, sv], do_ref[:, sl]
            p = _xa_probs(qh, kh)
            dp = _dot(doh, vh, "nt")
            ds_ = p * (dp - jnp.sum(dp * p, axis=-1, keepdims=True)) * (XA_DIM ** -0.5)
            dq_ref[:, sl] = _dot(ds_, kh).astype(BF16)
            dkv_ref[:, sl] += _dot(ds_, qh, "tn")
            dkv_ref[:, sv] += _dot(p, doh, "tn")

    return pl.pallas_call(
        body, name="xa_bwd", grid=(s // tq,),
        in_specs=[pl.BlockSpec((tq, XA_W), lambda i: (i, 0)), pl.BlockSpec((tq, XA_W), lambda i: (i, QC_CB)),
                  pl.BlockSpec(kv.shape, lambda i: (0, 0))],
        out_specs=[pl.BlockSpec((tq, XA_W), lambda i: (i, 0)), pl.BlockSpec(kv.shape, lambda i: (0, 0))],
        out_shape=[jax.ShapeDtypeStruct((s, XA_W), BF16), jax.ShapeDtypeStruct(kv.shape, F32)],
        compiler_params=_cparams(("arbitrary",)),
    )(do, proj, kv)


def adamw(w, g, m, v, name):
    rows, cols = w.shape
    tr = rows
    while tr * cols * 4 * 7 * 2 > 36 * 2 ** 20 and tr % 16 == 0:
        tr //= 2

    def fn(wv, gv, mv, vv):
        m2 = ADAM_B1 * mv + (1.0 - ADAM_B1) * gv
        v2 = ADAM_B2 * vv + (1.0 - ADAM_B2) * (gv * gv)
        m_hat = m2 / (1.0 - ADAM_B1 ** ADAM_STEP)
        v_hat = v2 / (1.0 - ADAM_B2 ** ADAM_STEP)
        return -ADAM_LR * (m_hat / (jnp.sqrt(v_hat) + ADAM_EPS) + ADAM_WD * wv), m2, v2

    return rowwise(fn, [(a, cols, 0) for a in (w, g, m, v)], [(cols, F32)] * 3, rows=rows, tr=tr, name=name)


def _seg_perm(a):
    s, w = a.shape
    return a.reshape(NSEG, s // NSEG, w).transpose(1, 0, 2).reshape(s, w)


def _seg_unperm(a):
    s, w = a.shape
    return a.reshape(s // NSEG, NSEG, w).transpose(1, 0, 2).reshape(s, w)


def _block_diag(t):
    nt, _, r, c = t.shape
    eye = jnp.eye(8, dtype=bool)
    return jnp.where(eye[None, :, None, :, None], t[:, :, :, None, :], 0.0).reshape(nt, 8 * r, 8 * c)


def _block_diag_inv(d, r, c):
    d5 = d.reshape(d.shape[0], 8, r, 8, c)
    return jnp.diagonal(d5, axis1=1, axis2=3).transpose(0, 3, 1, 2)


def _s5_b_tiles(bb):
    return _block_diag(bb.reshape(S5_TILES, 8, S5_STATE, S5_GROUP).transpose(0, 1, 3, 2))


def _s5_b_untile(d):
    return _block_diag_inv(d, S5_GROUP, S5_STATE).transpose(0, 1, 3, 2).reshape(S5_GROUPS, S5_STATE * S5_GROUP)


def _s5_c_tiles(c):
    return _block_diag(c.reshape(S5_TILES, 8, S5_GROUP, S5_STATE).transpose(0, 1, 3, 2))


def _s5_c_untile(d):
    return _block_diag_inv(d, S5_STATE, S5_GROUP).transpose(0, 1, 3, 2).reshape(S5_GROUPS, S5_GROUP, S5_STATE)


def s5_ssm_fwd(xb, lam_re, lam_im, log_dt, b_re, b_im, c_re, c_im):
    br, bi = b_re.reshape(S5_GROUPS, -1), b_im.reshape(S5_GROUPS, -1)
    ldt = log_dt.reshape(S5_GROUPS, 1)
    ab_re, ab_im, bb_re, bb_im = s5_param_fwd(lam_re, lam_im, ldt, br, bi)
    a_re, a_im = ab_re.reshape(1, S5_W), ab_im.reshape(1, S5_W)
    bre, bim = _s5_b_tiles(bb_re).astype(BF16), _s5_b_tiles(bb_im).astype(BF16)
    cre, cim = _s5_c_tiles(c_re).astype(BF16), _s5_c_tiles(c_im).astype(BF16)
    xp = _seg_perm(xb)
    yp, hs_r, hs_i = s5_scan_fwd(xp, a_re, a_im, bre, bim, cre, cim)
    saved = (xp, a_re, a_im, bre, bim, cre, cim, hs_r, hs_i, (lam_re, lam_im, ldt, br, bi))
    return _seg_unperm(yp), saved


def s5_ssm_bwd(dy, saved):
    xp, a_re, a_im, bre, bim, cre, cim, hs_r, hs_i, params = saved
    cre_t, cim_t = cre.transpose(0, 2, 1), cim.transpose(0, 2, 1)
    dxp, dar, dai, dbr, dbi, dcr, dci = s5_scan_bwd(_seg_perm(dy), xp, a_re, a_im, bre, bim, cre_t, cim_t, hs_r, hs_i)
    dlr, dli, dldt, db_re, db_im = s5_param_bwd(*params, dar.reshape(S5_GROUPS, S5_STATE), dai.reshape(S5_GROUPS, S5_STATE),
                                                _s5_b_untile(dbr), _s5_b_untile(dbi))
    shape_b = (S5_GROUPS, S5_STATE, S5_GROUP)
    return (_seg_unperm(dxp), dlr, dli, dldt.reshape(S5_GROUPS), db_re.reshape(shape_b), db_im.reshape(shape_b),
            _s5_c_untile(dcr), _s5_c_untile(dci))


_MESH = pl.DeviceIdType.MESH
_HBM = pl.BlockSpec(memory_space=pltpu.HBM)
N_DEV = 8


def _position():
    return lax.axis_index("x"), lax.axis_index("y"), lax.axis_index("c")


def allgather_weights(wflat, convw):
    def body(w_ref, c_ref, wo_ref, co_ref, send_sems, recv_sems, local_sems):
        x, y, c = _position()
        mine = 2 * x + y
        pairs = ((w_ref, wo_ref), (c_ref, co_ref))
        local = [pltpu.make_async_copy(src, dst.at[mine], local_sems.at[t]) for t, (src, dst) in enumerate(pairs)]
        for cp in local:
            cp.start()
        peers = [(1 - x, y), (x, 1 - y), (1 - x, 1 - y)]

        def copy(k, t, block):
            src, dst = pairs[t]
            px, py = peers[k]
            return pltpu.make_async_remote_copy(src_ref=src, dst_ref=dst.at[block], send_sem=send_sems.at[2 * k + t],
                                                recv_sem=recv_sems.at[2 * k + t], device_id=(px, py, c), device_id_type=_MESH)

        sends = [copy(k, t, mine) for k in range(3) for t in range(2)]
        for cp in sends:
            cp.start()
        for k, (px, py) in enumerate(peers):
            for t in range(2):
                copy(k, t, 2 * px + py).wait_recv()
        for cp in sends:
            cp.wait_send()
        for cp in local:
            cp.wait()

    return pl.pallas_call(
        body, name="allgather_weights", in_specs=[_HBM, _HBM], out_specs=[_HBM, _HBM],
        out_shape=[jax.ShapeDtypeStruct((4,) + wflat.shape, wflat.dtype), jax.ShapeDtypeStruct((4,) + convw.shape, convw.dtype)],
        scratch_shapes=[pltpu.SemaphoreType.DMA((6,)), pltpu.SemaphoreType.DMA((6,)), pltpu.SemaphoreType.DMA((2,))],
    )(wflat, convw)


def exchange_grads(pieces, small):
    def body(g_ref, s_ref, go_ref, so_ref, send_sems, recv_sems, local_sems):
        x, y, c = _position()
        me = 4 * x + 2 * y + c
        local = [pltpu.make_async_copy(g_ref.at[me], go_ref.at[me], local_sems.at[0]),
                 pltpu.make_async_copy(s_ref, so_ref.at[me], local_sems.at[1])]
        for cp in local:
            cp.start()

        def peer(r):
            return (1 - x if r & 4 else x, 1 - y if r & 2 else y, 1 - c if r & 1 else c)

        def copy(r, t, sending):
            px, py, pc = peer(r)
            pid = 4 * px + 2 * py + pc
            if t == 0:
                src, dst = g_ref.at[pid], go_ref.at[me if sending else pid]
            else:
                src, dst = s_ref, so_ref.at[me if sending else pid]
            return pltpu.make_async_remote_copy(src_ref=src, dst_ref=dst, send_sem=send_sems.at[2 * (r - 1) + t],
                                                recv_sem=recv_sems.at[2 * (r - 1) + t], device_id=(px, py, pc), device_id_type=_MESH)

        sends = [copy(r, t, True) for r in range(1, N_DEV) for t in range(2)]
        for cp in sends:
            cp.start()
        for r in range(1, N_DEV):
            for t in range(2):
                copy(r, t, False).wait_recv()
        for cp in sends:
            cp.wait_send()
        for cp in local:
            cp.wait()

    n = 2 * (N_DEV - 1)
    return pl.pallas_call(
        body, name="exchange_grads", in_specs=[_HBM, _HBM], out_specs=[_HBM, _HBM],
        out_shape=[jax.ShapeDtypeStruct(pieces.shape, pieces.dtype), jax.ShapeDtypeStruct((N_DEV,) + small.shape, small.dtype)],
        scratch_shapes=[pltpu.SemaphoreType.DMA((n,)), pltpu.SemaphoreType.DMA((n,)), pltpu.SemaphoreType.DMA((2,))],
    )(pieces, small)


def sibling_exchange(half):
    def body(h_ref, o_ref, send_sem, recv_sem, local_sem):
        x, y, c = _position()
        local = pltpu.make_async_copy(h_ref, o_ref.at[c], local_sem)
        local.start()

        def copy(slot):
            return pltpu.make_async_remote_copy(src_ref=h_ref, dst_ref=o_ref.at[slot], send_sem=send_sem, recv_sem=recv_sem,
                                                device_id=(x, y, 1 - c), device_id_type=_MESH)

        send = copy(c)
        send.start()
        copy(1 - c).wait_recv()
        send.wait_send()
        local.wait()

    return pl.pallas_call(
        body, name="sibling_exchange", in_specs=[_HBM], out_specs=_HBM,
        out_shape=jax.ShapeDtypeStruct((2,) + half.shape, half.dtype),
        scratch_shapes=[pltpu.SemaphoreType.DMA, pltpu.SemaphoreType.DMA, pltpu.SemaphoreType.DMA],
    )(half)


def _row_tile(rows, unit, max_rows):
    best = unit
    for t in range(unit, min(rows, max_rows) + 1, unit):
        if rows % t == 0:
            best = t
    return best


def sum_pieces(pieces, name):
    n, rows, cols = pieces.shape
    tr = _row_tile(rows, 16, max(16, (6 * 2 ** 20) // (n * cols * pieces.dtype.itemsize)))

    def body(p_ref, o_ref):
        acc = p_ref[0].astype(F32)
        for i in range(1, n):
            acc = acc + p_ref[i].astype(F32)
        o_ref[...] = acc

    return pl.pallas_call(
        body, name=name, grid=(rows // tr,),
        in_specs=[pl.BlockSpec((n, tr, cols), lambda i: (0, i, 0))], out_specs=pl.BlockSpec((tr, cols), lambda i: (i, 0)),
        out_shape=jax.ShapeDtypeStruct((rows, cols), F32), compiler_params=_cparams(("parallel",)),
    )(pieces)


PACK_COLS = 512
BIG = ("w_in", "s5_w_glu", "w_kv_mem", "w_br_a", "w_br_b", "w_br_c", "w_out")
COL_SHARDED = ("w_in", "s5_w_glu", "w_br_a", "w_br_b", "w_br_c")
SMALL = ("norm_g", "gdn_a_log", "gdn_dt_bias", "gdn_norm_g", "s5_lambda_re", "s5_lambda_im", "s5_log_dt",
         "s5_b_re", "s5_b_im", "s5_c_re", "s5_c_im", "s5_d", "mem_norm_g", "final_g")
WEIGHTS = ("norm_g", "w_in", "conv_w", "gdn_a_log", "gdn_dt_bias", "gdn_norm_g", "s5_lambda_re", "s5_lambda_im",
           "s5_log_dt", "s5_b_re", "s5_b_im", "s5_c_re", "s5_c_im", "s5_d", "s5_w_glu", "mem_norm_g", "w_kv_mem",
           "w_br_a", "w_br_b", "w_br_c", "w_out", "final_g")
W_IN_SPLIT = 4096


def _pack_w_in(w):
    pad = jnp.zeros((w.shape[0], BA_PAD - 2 * NHEAD), w.dtype)
    return jnp.concatenate([w[:, :W_IN_SPLIT], w[:, W_IN_SPLIT + 2 * NHEAD:], w[:, W_IN_SPLIT:W_IN_SPLIT + 2 * NHEAD], pad], axis=1)


def _unpack_w_in(wp):
    n = PROJ_W - BA_PAD
    return jnp.concatenate([wp[:, :W_IN_SPLIT], wp[:, n:n + 2 * NHEAD], wp[:, W_IN_SPLIT:n]], axis=1)


def _pack_rows(arrs, rows_padded):
    flat = jnp.concatenate([a.reshape(-1, PACK_COLS) for a in arrs], axis=0)
    return jnp.pad(flat, ((0, rows_padded - flat.shape[0]), (0, 0)))


def _unpack_rows(flat, shapes):
    out, off = [], 0
    for shp in shapes:
        n = math.prod(shp) // PACK_COLS
        out.append(flat[off:off + n].reshape(shp))
        off += n
    return out


def _pack_small(arrs):
    parts = []
    for a in arrs:
        f = a.reshape(-1).astype(F32)
        parts.append(jnp.pad(f, (0, (-f.shape[0]) % 128)))
    flat = jnp.concatenate(parts)
    rows = flat.shape[0] // 128
    return jnp.pad(flat.reshape(rows, 128), ((0, (-rows) % 16), (0, 0)))


def _unpack_small(flat2d, shapes):
    f = flat2d.reshape(-1)
    out, off = [], 0
    for shp in shapes:
        n = math.prod(shp)
        out.append(f[off:off + n].reshape(shp))
        off += n + (-n) % 128
    return out


def kernel(x, mem, norm_g, w_in, conv_w, gdn_a_log, gdn_dt_bias, gdn_norm_g, s5_lambda_re, s5_lambda_im, s5_log_dt, s5_b_re, s5_b_im, s5_c_re, s5_c_im, s5_d, s5_w_glu, mem_norm_g, w_kv_mem, w_br_a, w_br_b, w_br_c, w_out, final_g, loss_target, m_norm_g, m_w_in, m_conv_w, m_gdn_a_log, m_gdn_dt_bias, m_gdn_norm_g, m_s5_lambda_re, m_s5_lambda_im, m_s5_log_dt, m_s5_b_re, m_s5_b_im, m_s5_c_re, m_s5_c_im, m_s5_d, m_s5_w_glu, m_mem_norm_g, m_w_kv_mem, m_w_br_a, m_w_br_b, m_w_br_c, m_w_out, m_final_g, v_norm_g, v_w_in, v_conv_w, v_gdn_a_log, v_gdn_dt_bias, v_gdn_norm_g, v_s5_lambda_re, v_s5_lambda_im, v_s5_log_dt, v_s5_b_re, v_s5_b_im, v_s5_c_re, v_s5_c_im, v_s5_d, v_s5_w_glu, v_mem_norm_g, v_w_kv_mem, v_w_br_a, v_w_br_b, v_w_br_c, v_w_out, v_final_g):
    wts = dict(norm_g=norm_g, w_in=w_in, conv_w=conv_w, gdn_a_log=gdn_a_log, gdn_dt_bias=gdn_dt_bias, gdn_norm_g=gdn_norm_g,
               s5_lambda_re=s5_lambda_re, s5_lambda_im=s5_lambda_im, s5_log_dt=s5_log_dt, s5_b_re=s5_b_re, s5_b_im=s5_b_im,
               s5_c_re=s5_c_re, s5_c_im=s5_c_im, s5_d=s5_d, s5_w_glu=s5_w_glu, mem_norm_g=mem_norm_g, w_kv_mem=w_kv_mem,
               w_br_a=w_br_a, w_br_b=w_br_b, w_br_c=w_br_c, w_out=w_out, final_g=final_g)
    mom = dict(norm_g=m_norm_g, w_in=m_w_in, conv_w=m_conv_w, gdn_a_log=m_gdn_a_log, gdn_dt_bias=m_gdn_dt_bias,
               gdn_norm_g=m_gdn_norm_g, s5_lambda_re=m_s5_lambda_re, s5_lambda_im=m_s5_lambda_im, s5_log_dt=m_s5_log_dt,
               s5_b_re=m_s5_b_re, s5_b_im=m_s5_b_im, s5_c_re=m_s5_c_re, s5_c_im=m_s5_c_im, s5_d=m_s5_d, s5_w_glu=m_s5_w_glu,
               mem_norm_g=m_mem_norm_g, w_kv_mem=m_w_kv_mem, w_br_a=m_w_br_a, w_br_b=m_w_br_b, w_br_c=m_w_br_c, w_out=m_w_out,
               final_g=m_final_g)
    vel = dict(norm_g=v_norm_g, w_in=v_w_in, conv_w=v_conv_w, gdn_a_log=v_gdn_a_log, gdn_dt_bias=v_gdn_dt_bias,
               gdn_norm_g=v_gdn_norm_g, s5_lambda_re=v_s5_lambda_re, s5_lambda_im=v_s5_lambda_im, s5_log_dt=v_s5_log_dt,
               s5_b_re=v_s5_b_re, s5_b_im=v_s5_b_im, s5_c_re=v_s5_c_re, s5_c_im=v_s5_c_im, s5_d=v_s5_d, s5_w_glu=v_s5_w_glu,
               mem_norm_g=v_mem_norm_g, w_kv_mem=v_w_kv_mem, w_br_a=v_w_br_a, w_br_b=v_w_br_b, w_br_c=v_w_br_c, w_out=v_w_out,
               final_g=v_final_g)
    x2, mem2, tgt = x[0], mem[0], loss_target[0]
    s, d = x2.shape
    n_chunks = s // CHUNK

    shard_shapes = [wts[n][0].shape for n in BIG]
    shard_rows = sum(math.prod(sh) // PACK_COLS for sh in shard_shapes)
    rows_padded = -(-shard_rows // 32) * 32
    wflat = _pack_rows([wts[n][0].astype(BF16) for n in BIG], rows_padded)
    wg, cg = allgather_weights(wflat, conv_w[0])
    full = {}
    for n, blocks in zip(BIG, zip(*[_unpack_rows(wg[j], shard_shapes) for j in range(4)])):
        full[n] = jnp.concatenate(blocks, axis=1 if n in COL_SHARDED else 0)
    wp = _pack_w_in(full["w_in"])
    conv_full = cg.transpose(1, 0, 2).reshape(conv_w.shape[1], -1)
    alog_pad = jnp.zeros((1, BA_W), F32).at[0, NHEAD:2 * NHEAD].set(gdn_a_log[0])
    dt_pad = jnp.zeros((1, BA_W), F32).at[0, NHEAD:2 * NHEAD].set(gdn_dt_bias[0])

    mm = functools.partial(matmul, tm=1024, tn=1024)
    u, r1 = rms_fwd(x2, norm_g, "rms_fwd_x")
    proj = mm(u, wp, mode="nn", out_dtype=F32, tk=2048, name="mm_proj")
    q, k, v, bg, gcol, gt = gdn_prep_fwd(proj, conv_full, alog_pad, dt_pad)
    gt3 = gt.reshape(BA_W, n_chunks, CHUNK).transpose(1, 0, 2)
    gu, gw, qd, kd, qk, tinv = gdn_intra_fwd(q, k, v, bg, gcol, gt3)
    o_raw, states = gdn_seq_fwd(gu, gw, qd, kd, qk, gt3)
    ga = gdn_out_fwd(o_raw, proj, ZA_CB, gdn_norm_g)

    xb = proj[:, XB_CB * S5_IN:(XB_CB + 1) * S5_IN]
    y_ssm, s5_saved = s5_ssm_fwd(xb, s5_lambda_re[0], s5_lambda_im[0], s5_log_dt[0], s5_b_re[0], s5_b_im[0],
                                 s5_c_re[0], s5_c_im[0])
    yb = s5_act_fwd(y_ssm, proj, XB_CB, s5_d)
    glu = mm(yb, full["s5_w_glu"], mode="nn", out_dtype=F32, tk=1024, name="mm_glu")
    gb = s5_glu_fwd(glu, proj, ZB_CB)

    mem_n, rm = rms_fwd(mem2, mem_norm_g, "rms_fwd_mem")
    kv = mm(mem_n, full["w_kv_mem"], mode="nn", out_dtype=BF16, tk=2048, name="mm_kv")
    o_c = xa_fwd(proj, kv)
    gcx = gate_fwd(o_c, proj, ZC_CB, "gate_fwd_c")

    pa = mm(ga, full["w_br_a"], mode="nn", out_dtype=F32, tk=1024, name="mm_pa")
    pb = mm(gb, full["w_br_b"], mode="nn", out_dtype=F32, tk=1024, name="mm_pb")
    pc = mm(gcx, full["w_br_c"], mode="nn", out_dtype=F32, tk=1024, name="mm_pc")
    merged = merge_fwd(pa, pb, pc, proj, GATE_CB)
    hres = mm(merged, full["w_out"], mode="nn", out_dtype=F32, tk=2048, name="mm_out")
    dh, dhb, loss_part, d_final_g = final_stage(x2, hres, tgt, final_g.reshape(1, d))

    gfull = {}
    dmerged = mm(dhb, full["w_out"], mode="nt", out_dtype=F32, tk=2048, name="mm_dmerged")
    gfull["w_out"] = mm(merged, dhb, mode="tn", out_dtype=BF16, tk=1024, name="mm_dw_out")
    dpa, dpb, dpc, dg0, dg1, dg2 = merge_bwd(dmerged, pa, pb, pc, proj, GATE_CB)
    dga = mm(dpa, full["w_br_a"], mode="nt", out_dtype=F32, tk=2048, name="mm_dga")
    dgb = mm(dpb, full["w_br_b"], mode="nt", out_dtype=F32, tk=2048, name="mm_dgb")
    dgc = mm(dpc, full["w_br_c"], mode="nt", out_dtype=F32, tk=2048, name="mm_dgc")
    gfull["w_br_a"] = mm(ga, dpa, mode="tn", out_dtype=BF16, tk=1024, name="mm_dw_a")
    gfull["w_br_b"] = mm(gb, dpb, mode="tn", out_dtype=BF16, tk=1024, name="mm_dw_b")
    gfull["w_br_c"] = mm(gcx, dpc, mode="tn", out_dtype=BF16, tk=1024, name="mm_dw_c")

    do_raw, dza, d_gdn_norm = gdn_out_bwd(dga, o_raw, proj, ZA_CB, gdn_norm_g)
    du_, dw_, dqd, dkd, dqk, dgl = gdn_seq_bwd(do_raw, gu, gw, qd, kd, qk, gt3, states)
    dq, dk, dv, dbg = gdn_intra_bwd(q, k, v, bg, gcol, gt3, tinv, du_, dw_, dqd, dkd, dqk, dgl)
    dc, dba, dcw0, dcw1, dcw2, dcw3, d_alog, d_dt = gdn_prep_bwd1(proj, conv_full, alog_pad, dt_pad, dq, dk, dv, dbg)
    dqkv = gdn_prep_bwd2(dc, conv_full)
    d_conv = jnp.concatenate([dcw0, dcw1, dcw2, dcw3], axis=0)

    dval, dgate, dzb = s5_glu_bwd(dgb, glu, proj, ZB_CB)
    dglu = jnp.concatenate([dval, dgate], axis=1)
    dyb = mm(dglu, full["s5_w_glu"], mode="nt", out_dtype=F32, tk=2048, name="mm_dyb")
    gfull["s5_w_glu"] = mm(yb, dglu, mode="tn", out_dtype=BF16, tk=1024, name="mm_dw_glu")
    dy_ssm, dxb_direct, d_s5_d = s5_act_bwd(dyb, y_ssm, proj, XB_CB, s5_d)
    dxb_scan, d_lre, d_lim, d_ldt, d_bre, d_bim, d_cre, d_cim = s5_ssm_bwd(dy_ssm, s5_saved)
    dxb = add_cast(dxb_direct, dxb_scan, "s5_dxb")

    do_c, dzc = gate_bwd(dgc, o_c, proj, ZC_CB, "gate_bwd_c")
    dqc, dkv = xa_bwd(do_c, proj, kv)
    gfull["w_kv_mem"] = mm(mem_n, dkv, mode="tn", out_dtype=BF16, tk=256, name="mm_dw_kv")
    dmem_n = mm(dkv, full["w_kv_mem"], mode="nt", out_dtype=F32, tk=2048, name="mm_dmem")
    d_mem_norm = rms_bwd_g(dmem_n, mem2, rm, "rms_bwd_mem")

    dproj = jnp.concatenate([dqkv, dza, dxb, dzb, dqc, dzc, dg0, dg1, dg2, dba], axis=1)
    dwp = matmul(u, dproj, mode="tn", out_dtype=BF16, tm=2048, tn=1024, tk=512, name="mm_dw_in")
    gfull["w_in"] = _unpack_w_in(dwp)
    du = mm(dproj, wp, mode="nt", out_dtype=F32, tk=1024, name="mm_du")
    grad_x, d_norm_g = rms_bwd_x(du, x2, r1, norm_g, dh)

    per_shard = []
    for j in range(4):
        blocks = []
        for n, shp in zip(BIG, shard_shapes):
            g = gfull[n]
            blocks.append(g[:, j * shp[1]:(j + 1) * shp[1]] if n in COL_SHARDED else g[j * shp[0]:(j + 1) * shp[0]])
        per_shard.append(_pack_rows(blocks, rows_padded))
    pieces = jnp.stack(per_shard).reshape(N_DEV, rows_padded // 2, PACK_COLS)
    small_g = dict(norm_g=d_norm_g, gdn_a_log=d_alog[:, NHEAD:2 * NHEAD], gdn_dt_bias=d_dt[:, NHEAD:2 * NHEAD],
                   gdn_norm_g=d_gdn_norm, s5_lambda_re=d_lre, s5_lambda_im=d_lim, s5_log_dt=d_ldt, s5_b_re=d_bre, s5_b_im=d_bim,
                   s5_c_re=d_cre, s5_c_im=d_cim, s5_d=d_s5_d, mem_norm_g=d_mem_norm, final_g=d_final_g)
    small_send = _pack_small([small_g[n] for n in SMALL] + [d_conv, loss_part])
    got_pieces, got_small = exchange_grads(pieces, small_send)
    half = sum_pieces(got_pieces, "sum_big")
    small_sum = sum_pieces(got_small, "sum_small")
    both = sibling_exchange(half).reshape(rows_padded, PACK_COLS)
    grads = dict(zip(BIG, _unpack_rows(both, shard_shapes)))
    small_shapes = [wts[n].shape for n in SMALL] + [d_conv.shape, (1, 1)]
    *small_list, conv_g_full, loss_sum = _unpack_small(small_sum, small_shapes)
    grads.update(zip(SMALL, small_list))
    cw = conv_w.shape[2]
    shard_idx = 2 * lax.axis_index("x") + lax.axis_index("y")
    grads["conv_w"] = lax.dynamic_slice(conv_g_full, (0, shard_idx * cw), (conv_w.shape[1], cw))

    delta, new_m, new_v = {}, {}, {}
    for n in BIG + ("conv_w",):
        shp = wts[n].shape
        w2 = wts[n].reshape(shp[-2], shp[-1])
        res = adamw(w2, grads[n].reshape(w2.shape), mom[n].reshape(w2.shape), vel[n].reshape(w2.shape), "adamw_" + n)
        delta[n], new_m[n], new_v[n] = (r.reshape(shp) for r in res)
        grads[n] = grads[n].reshape(shp)
    packed = [_pack_small([src[n] for n in SMALL]) for src in (wts, grads, mom, vel)]
    res = adamw(*packed, "adamw_small")
    shapes = [wts[n].shape for n in SMALL]
    for dst, flat in zip((delta, new_m, new_v), res):
        dst.update(zip(SMALL, _unpack_small(flat, shapes)))
    for n in SMALL:
        grads[n] = grads[n].reshape(wts[n].shape)

    return (loss_sum.reshape(()), grad_x.reshape(x.shape), *[grads[n] for n in WEIGHTS], *[delta[n] for n in WEIGHTS],
            *[new_m[n] for n in WEIGHTS], *[new_v[n] for n in WEIGHTS])
```

```python
import functools
import math

import jax
import jax.numpy as jnp
from jax import lax
from jax.experimental import pallas as pl
from jax.experimental.pallas import tpu as pltpu

F32 = jnp.float32
BF16 = jnp.bfloat16
HI = lax.Precision.HIGHEST

EPS = 1e-6
CHUNK = 64
HEAD = 128
NHEAD = 8
XA_HEADS = 4
S5_GROUPS = 64
S5_STATE = 64
S5_GROUP = 16
NSEG = 8
ADAM_LR, ADAM_B1, ADAM_B2, ADAM_EPS, ADAM_WD, ADAM_STEP = 0.001, 0.9, 0.999, 1e-08, 0.01, 10
VMEM_LIMIT = 56 * 2 ** 20


def _cparams(sem=None):
    return pltpu.CompilerParams(dimension_semantics=sem, vmem_limit_bytes=VMEM_LIMIT)


def _sigmoid(x):
    return 1.0 / (1.0 + jnp.exp(-x))


def _silu(x):
    return x * _sigmoid(x)


def _dsilu(x):
    s = _sigmoid(x)
    return s * (1.0 + x * (1.0 - s))


def _softplus(x):
    return jnp.maximum(x, 0.0) + jnp.log(1.0 + jnp.exp(-jnp.abs(x)))


_GELU_C = math.sqrt(2.0 / math.pi)


def _gelu(x):
    return 0.5 * x * (1.0 + jnp.tanh(_GELU_C * (x + 0.044715 * x * x * x)))


def _dgelu(x):
    t = jnp.tanh(_GELU_C * (x + 0.044715 * x * x * x))
    return 0.5 * (1.0 + t) + 0.5 * x * (1.0 - t * t) * _GELU_C * (1.0 + 3.0 * 0.044715 * x * x)


_DIMS = {"nn": (((1,), (0,)), ((), ())), "nt": (((1,), (1,)), ((), ())), "tn": (((0,), (0,)), ((), ()))}


def matmul(a, b, *, mode, out_dtype, tm, tn, tk, name):
    if mode == "nn":
        (m, k), n = a.shape, b.shape[1]
    elif mode == "nt":
        (m, k), n = a.shape, b.shape[0]
    else:
        (k, m), n = a.shape, b.shape[1]
    tm, tn, tk = min(tm, m), min(tn, n), min(tk, k)
    assert m % tm == 0 and n % tn == 0 and k % tk == 0, (name, m, n, k, tm, tn, tk)
    nk = k // tk
    dims = _DIMS[mode]

    def body(a_ref, b_ref, o_ref, *scratch):
        prod = lax.dot_general(a_ref[...].astype(BF16), b_ref[...].astype(BF16), dims, preferred_element_type=F32)
        if nk == 1:
            o_ref[...] = prod.astype(out_dtype)
            return
        acc_ref, = scratch
        kk = pl.program_id(2)

        @pl.when(kk == 0)
        def _():
            acc_ref[...] = prod

        @pl.when(kk > 0)
        def _():
            acc_ref[...] += prod

        @pl.when(kk == nk - 1)
        def _():
            o_ref[...] = acc_ref[...].astype(out_dtype)

    a_spec = pl.BlockSpec((tk, tm), lambda i, j, q: (q, i)) if mode == "tn" else pl.BlockSpec((tm, tk), lambda i, j, q: (i, q))
    b_spec = pl.BlockSpec((tn, tk), lambda i, j, q: (j, q)) if mode == "nt" else pl.BlockSpec((tk, tn), lambda i, j, q: (q, j))
    return pl.pallas_call(
        body, name=name, grid=(m // tm, n // tn, nk),
        in_specs=[a_spec, b_spec], out_specs=pl.BlockSpec((tm, tn), lambda i, j, q: (i, j)),
        out_shape=jax.ShapeDtypeStruct((m, n), out_dtype),
        scratch_shapes=[] if nk == 1 else [pltpu.VMEM((tm, tn), F32)],
        compiler_params=_cparams(("parallel", "parallel", "arbitrary")),
    )(a, b)


def rowwise(fn, ins, outs, *, rows, tr, name, consts=(), reds=()):
    tr = min(tr, rows)
    assert rows % tr == 0, (name, rows, tr)
    n_in, n_c, n_o = len(ins), len(consts), len(outs)

    def body(*refs):
        vals = [r[...] for r in refs[:n_in + n_c]]
        res = fn(*vals)
        o_refs = refs[n_in + n_c:]
        for r, v in zip(o_refs[:n_o], res[:n_o]):
            r[...] = v.astype(r.dtype)
        if reds:
            i = pl.program_id(0)

            @pl.when(i == 0)
            def _():
                for r, v in zip(o_refs[n_o:], res[n_o:]):
                    r[...] = v.astype(r.dtype)

            @pl.when(i > 0)
            def _():
                for r, v in zip(o_refs[n_o:], res[n_o:]):
                    r[...] += v.astype(r.dtype)

    in_specs = [pl.BlockSpec((tr, w), functools.partial(lambda i, cb: (i, cb), cb=cb)) for (_, w, cb) in ins]
    in_specs += [pl.BlockSpec(c.shape, lambda i: (0, 0)) for c in consts]
    out_specs = [pl.BlockSpec((tr, w), lambda i: (i, 0)) for (w, _) in outs]
    out_specs += [pl.BlockSpec(s, lambda i: (0, 0)) for (s, _) in reds]
    out_shape = [jax.ShapeDtypeStruct((rows, w), d) for (w, d) in outs]
    out_shape += [jax.ShapeDtypeStruct(s, d) for (s, d) in reds]
    res = pl.pallas_call(
        body, name=name, grid=(rows // tr,), in_specs=in_specs, out_specs=out_specs, out_shape=out_shape,
        compiler_params=_cparams(("arbitrary",) if reds else ("parallel",)),
    )(*[a for (a, _, _) in ins], *consts)
    return res


def _colsum(x):
    return jnp.sum(x, axis=0, keepdims=True)


def rms_fwd(x, g, name):
    s, d = x.shape

    def fn(xv, gv):
        r = lax.rsqrt(jnp.mean(xv * xv, axis=-1, keepdims=True) + EPS)
        return xv * r * gv, r

    return rowwise(fn, [(x, d, 0)], [(d, BF16), (1, F32)], rows=s, tr=256, name=name, consts=[g])


def rms_bwd_x(du, x, r, g, dh):
    s, d = x.shape

    def fn(duv, xv, rv, dhv, gv):
        dyg = duv * gv
        dx = rv * dyg - xv * (rv * rv * rv) * jnp.mean(dyg * xv, axis=-1, keepdims=True)
        return dhv + dx, _colsum(duv * xv * rv)

    return rowwise(fn, [(du, d, 0), (x, d, 0), (r, 1, 0), (dh, d, 0)], [(d, F32)], rows=s, tr=256,
                   name="rms_bwd_x", consts=[g], reds=[((1, d), F32)])


def rms_bwd_g(du, x, r, name):
    s, d = x.shape

    def fn(duv, xv, rv):
        return (_colsum(duv * xv * rv),)

    return rowwise(fn, [(du, d, 0), (x, d, 0), (r, 1, 0)], [], rows=s, tr=256, name=name, reds=[((1, d), F32)])[0]


def final_stage(x, hres, target, g):
    s, d = x.shape

    def fn(xv, hv, tv, gv):
        h = xv + hv
        r = lax.rsqrt(jnp.mean(h * h, axis=-1, keepdims=True) + EPS)
        y = h * r * gv
        e = y - tv
        loss = 0.5 * jnp.sum(jnp.sum(e * e, axis=-1, keepdims=True), axis=0, keepdims=True) / d
        dy = e / d
        dyg = dy * gv
        dh = r * dyg - h * (r * r * r) * jnp.mean(dyg * h, axis=-1, keepdims=True)
        return dh, dh, loss, _colsum(dy * h * r)

    return rowwise(fn, [(x, d, 0), (hres, d, 0), (target, d, 0)], [(d, F32), (d, BF16)], rows=s, tr=256,
                   name="final_stage", consts=[g], reds=[((1, 1), F32), ((1, d), F32)])


def merge_fwd(pa, pb, pc, proj, gate_cb):
    s, d = pa.shape

    def fn(a, b, c, g0, g1, g2):
        return (_sigmoid(g0) * a + _sigmoid(g1) * b + _sigmoid(g2) * c,)

    ins = [(pa, d, 0), (pb, d, 0), (pc, d, 0)] + [(proj, d, gate_cb + i) for i in range(3)]
    return rowwise(fn, ins, [(d, BF16)], rows=s, tr=256, name="merge_fwd")[0]


def merge_bwd(dm, pa, pb, pc, proj, gate_cb):
    s, d = pa.shape

    def fn(dmv, a, b, c, g0, g1, g2):
        s0, s1, s2 = _sigmoid(g0), _sigmoid(g1), _sigmoid(g2)
        return (dmv * s0, dmv * s1, dmv * s2,
                dmv * a * s0 * (1.0 - s0), dmv * b * s1 * (1.0 - s1), dmv * c * s2 * (1.0 - s2))

    ins = [(dm, d, 0), (pa, d, 0), (pb, d, 0), (pc, d, 0)] + [(proj, d, gate_cb + i) for i in range(3)]
    return rowwise(fn, ins, [(d, BF16)] * 6, rows=s, tr=128, name="merge_bwd")


def gate_fwd(o, proj, z_cb, name):
    s, w = o.shape

    def fn(ov, zv):
        return (ov * _silu(zv),)

    return rowwise(fn, [(o, w, 0), (proj, w, z_cb)], [(w, BF16)], rows=s, tr=512, name=name)[0]


def gate_bwd(dgo, o, proj, z_cb, name):
    s, w = o.shape

    def fn(dv, ov, zv):
        return dv * _silu(zv), dv * ov * _dsilu(zv)

    return rowwise(fn, [(dgo, w, 0), (o, w, 0), (proj, w, z_cb)], [(w, F32), (w, BF16)], rows=s, tr=512, name=name)


def gdn_out_fwd(o_raw, proj, z_cb, gn):
    s, w = o_raw.shape

    def fn(ov, zv, gv):
        outs = []
        for h in range(NHEAD):
            oh = ov[:, h * HEAD:(h + 1) * HEAD]
            r = lax.rsqrt(jnp.mean(oh * oh, axis=-1, keepdims=True) + EPS)
            outs.append(oh * r * gv)
        return (jnp.concatenate(outs, axis=1) * _silu(zv),)

    return rowwise(fn, [(o_raw, w, 0), (proj, w, z_cb)], [(w, BF16)], rows=s, tr=512, name="gdn_out_fwd", consts=[gn])[0]


def gdn_out_bwd(dga, o_raw, proj, z_cb, gn):
    s, w = o_raw.shape

    def fn(dv, ov, zv, gv):
        sz, dsz = _silu(zv), _dsilu(zv)
        do_l, dz_l = [], []
        dg = jnp.zeros((1, HEAD), F32)
        for h in range(NHEAD):
            sl = slice(h * HEAD, (h + 1) * HEAD)
            oh, dgh = ov[:, sl], dv[:, sl]
            r = lax.rsqrt(jnp.mean(oh * oh, axis=-1, keepdims=True) + EPS)
            on = oh * r * gv
            don = dgh * sz[:, sl]
            dz_l.append(dgh * on * dsz[:, sl])
            dg = dg + _colsum(don * oh * r)
            dyg = don * gv
            do_l.append(r * dyg - oh * (r * r * r) * jnp.mean(dyg * oh, axis=-1, keepdims=True))
        return jnp.concatenate(do_l, axis=1), jnp.concatenate(dz_l, axis=1), dg

    return rowwise(fn, [(dga, w, 0), (o_raw, w, 0), (proj, w, z_cb)], [(w, F32), (w, BF16)], rows=s, tr=512,
                   name="gdn_out_bwd", consts=[gn], reds=[((1, HEAD), F32)])


def s5_act_fwd(y_ssm, proj, xb_cb, dvec):
    s, w = y_ssm.shape

    def fn(yv, xv, dv):
        return (_gelu(yv + dv * xv),)

    return rowwise(fn, [(y_ssm, w, 0), (proj, w, xb_cb)], [(w, BF16)], rows=s, tr=512, name="s5_act_fwd", consts=[dvec])[0]


def s5_act_bwd(dyb, y_ssm, proj, xb_cb, dvec):
    s, w = y_ssm.shape

    def fn(dv_, yv, xv, dv):
        dpre = dv_ * _dgelu(yv + dv * xv)
        return dpre, dpre * dv, _colsum(dpre * xv)

    return rowwise(fn, [(dyb, w, 0), (y_ssm, w, 0), (proj, w, xb_cb)], [(w, F32), (w, F32)], rows=s, tr=512,
                   name="s5_act_bwd", consts=[dvec], reds=[((1, w), F32)])


def s5_glu_fwd(glu, proj, z_cb):
    s, w2 = glu.shape
    w = w2 // 2

    def fn(val, gate, zv):
        return (val * _sigmoid(gate) * _silu(zv),)

    return rowwise(fn, [(glu, w, 0), (glu, w, 1), (proj, w, z_cb)], [(w, BF16)], rows=s, tr=512, name="s5_glu_fwd")[0]


def s5_glu_bwd(dgb, glu, proj, z_cb):
    s, w2 = glu.shape
    w = w2 // 2

    def fn(dv, val, gate, zv):
        sg = _sigmoid(gate)
        ob = val * sg
        dob = dv * _silu(zv)
        return dob * sg, dob * val * sg * (1.0 - sg), dv * ob * _dsilu(zv)

    return rowwise(fn, [(dgb, w, 0), (glu, w, 0), (glu, w, 1), (proj, w, z_cb)], [(w, BF16)] * 3, rows=s, tr=512,
                   name="s5_glu_bwd")


def add_cast(a, b, name):
    s, w = a.shape

    def fn(av, bv):
        return (av + bv,)

    return rowwise(fn, [(a, w, 0), (b, w, 0)], [(w, BF16)], rows=s, tr=512, name=name)[0]


QKV_W, QKV_CB = 3072, 0
ZA_CB, XB_CB, ZB_CB, QC_CB, ZC_CB = 3, 4, 5, 6, 7
GATE_CB = 4
BA_CB, BA_W = 112, 128
BA_PAD = 1024
PROJ_W = 14336 + BA_PAD


def _dot(a, b, dims="nn", prec=None):
    if prec is None:
        a, b = a.astype(BF16), b.astype(BF16)
    return lax.dot_general(a, b, _DIMS[dims], preferred_element_type=F32, precision=prec)


def _split(a):
    hi = a.astype(BF16)
    return hi, (a - hi.astype(F32)).astype(BF16)


def _dot3(a, b, dims="nn"):
    (ah, al), (bh, bl) = _split(a), _split(b)
    d = functools.partial(lax.dot_general, dimension_numbers=_DIMS[dims], preferred_element_type=F32)
    return d(ah, bh) + (d(ah, bl) + d(al, bh))


def _iota2(shape, dim):
    return lax.broadcasted_iota(jnp.int32, shape, dim)


def _conv_taps(xs, tr, k):
    if k == 0:
        return xs[8:8 + tr]
    return pltpu.roll(xs, k, 0)[8:8 + tr]


def _conv_silu_parts(xv, halo, wv, first):
    tr = xv.shape[0]
    xs = jnp.concatenate([jnp.where(first, 0.0, halo), xv], axis=0)
    taps = [_conv_taps(xs, tr, 3 - j) for j in range(4)]
    c = taps[0] * wv[0:1] + taps[1] * wv[1:2] + taps[2] * wv[2:3] + taps[3] * wv[3:4]
    return taps, c


def gdn_prep_fwd(proj, conv_w, alog_pad, dt_pad):
    s = proj.shape[0]
    tr = min(256, s)
    w = NHEAD * HEAD

    def body(x_ref, halo_ref, ba_ref, w_ref, al_ref, dt_ref, q_ref, k_ref, v_ref, bg_ref, gcol_ref, gt_ref):
        first = pl.program_id(0) == 0
        _, c = _conv_silu_parts(x_ref[...], halo_ref[...], w_ref[...], first)
        sv = _silu(c)
        for h in range(NHEAD):
            sl = slice(h * HEAD, (h + 1) * HEAD)
            qh, kh = sv[:, h * HEAD:(h + 1) * HEAD], sv[:, w + h * HEAD:w + (h + 1) * HEAD]
            q_ref[:, sl] = qh * lax.rsqrt(jnp.sum(qh * qh, axis=-1, keepdims=True) + EPS) * (HEAD ** -0.5)
            k_ref[:, sl] = kh * lax.rsqrt(jnp.sum(kh * kh, axis=-1, keepdims=True) + EPS)
        v_ref[...] = sv[:, 2 * w:]
        ba = ba_ref[...]
        lane = _iota2(ba.shape, 1)
        beta = _sigmoid(ba)
        g = -jnp.exp(al_ref[...]) * _softplus(ba + dt_ref[...])
        bg = jnp.where(lane < NHEAD, beta, jnp.where(lane < 2 * NHEAD, g, 0.0))
        bg_ref[...] = bg
        er, ec = _iota2((BA_W, BA_W), 0), _iota2((BA_W, BA_W), 1)
        expand = jnp.where((er == NHEAD + ec // 8) & (ec < 8 * NHEAD), 1.0, 0.0)
        grep = _dot(bg, expand, prec=HI)
        lr, lc = _iota2((tr, tr), 0), _iota2((tr, tr), 1)
        tril = jnp.where((lr // CHUNK == lc // CHUNK) & (lr >= lc), 1.0, 0.0)
        gc = _dot(tril, grep, prec=HI)
        gcol_ref[...] = gc
        gt_ref[...] = gc.T

    nb8 = tr // 8
    return pl.pallas_call(
        body, name="gdn_prep_fwd", grid=(s // tr,),
        in_specs=[pl.BlockSpec((tr, QKV_W), lambda i: (i, QKV_CB)),
                  pl.BlockSpec((8, QKV_W), lambda i: (jnp.maximum(i * nb8 - 1, 0), QKV_CB)),
                  pl.BlockSpec((tr, BA_W), lambda i: (i, BA_CB)),
                  pl.BlockSpec(conv_w.shape, lambda i: (0, 0)),
                  pl.BlockSpec((1, BA_W), lambda i: (0, 0)), pl.BlockSpec((1, BA_W), lambda i: (0, 0))],
        out_specs=[pl.BlockSpec((tr, w), lambda i: (i, 0))] * 3 + [pl.BlockSpec((tr, BA_W), lambda i: (i, 0))] * 2
        + [pl.BlockSpec((BA_W, tr), lambda i: (0, i))],
        out_shape=[jax.ShapeDtypeStruct((s, w), F32)] * 3 + [jax.ShapeDtypeStruct((s, BA_W), F32)] * 2
        + [jax.ShapeDtypeStruct((BA_W, s), F32)],
        compiler_params=_cparams(("parallel",)),
    )(proj, proj, proj, conv_w, alog_pad, dt_pad)


def _chunk_common(qh, kh, bgv, gcolv, gtv, h):
    beta = bgv[:, h:h + 1]
    gcc = gcolv[:, 8 * h:8 * h + 1]
    gcr = jnp.concatenate([gtv[8 * h:8 * h + 8, :]] * (CHUNK // 8), axis=0)
    ii, jj = _iota2((CHUNK, CHUNK), 0), _iota2((CHUNK, CHUNK), 1)
    incl, strict = ii >= jj, ii > jj
    decay = jnp.where(incl, jnp.exp(jnp.where(incl, gcc - gcr, 0.0)), 0.0)
    gl = gcr[:, CHUNK - 1:CHUNK]
    return beta, gcc, decay, strict, gl


def gdn_intra_fwd(q, k, v, bg, gcol, gt3):
    s, w = q.shape
    n = s // CHUNK

    def body(q_ref, k_ref, v_ref, bg_ref, gcol_ref, gt_ref, u_ref, w_ref, qd_ref, kd_ref, qk_ref, t_ref):
        bgv, gcolv, gtv = bg_ref[...], gcol_ref[...], gt_ref[0]
        ii, jj = _iota2((CHUNK, CHUNK), 0), _iota2((CHUNK, CHUNK), 1)
        eye = jnp.where(ii == jj, 1.0, 0.0)
        ps, ts, rhs = [], [], []
        for h in range(NHEAD):
            sl = slice(h * HEAD, (h + 1) * HEAD)
            qh, kh, vh = q_ref[:, sl], k_ref[:, sl], v_ref[:, sl]
            beta, gcc, decay, strict, gl = _chunk_common(qh, kh, bgv, gcolv, gtv, h)
            kb = kh * beta
            eg = jnp.exp(gcc)
            p = -jnp.where(strict, _dot(kb, kh, "nt") * decay, 0.0)
            ps.append(p)
            ts.append(eye + p)
            rhs.append((vh * beta, kb * eg))
            qd_ref[:, sl] = qh * eg
            kd_ref[:, sl] = kh * jnp.exp(gl - gcc)
            qk_ref[0, h] = _dot(qh, kh, "nt") * decay
        for _ in range(5):
            ps = [_dot3(p, p) for p in ps]
            ts = [t + _dot3(t, p) for t, p in zip(ts, ps)]
        for h in range(NHEAD):
            sl = slice(h * HEAD, (h + 1) * HEAD)
            u_ref[:, sl] = _dot3(ts[h], rhs[h][0])
            w_ref[:, sl] = _dot3(ts[h], rhs[h][1])
            t_ref[0, h] = ts[h]

    tok = pl.BlockSpec((CHUNK, w), lambda i: (i, 0))
    sm = pl.BlockSpec((CHUNK, BA_W), lambda i: (i, 0))
    sq = pl.BlockSpec((1, NHEAD, CHUNK, CHUNK), lambda i: (i, 0, 0, 0))
    return pl.pallas_call(
        body, name="gdn_intra_fwd", grid=(n,),
        in_specs=[tok, tok, tok, sm, sm, pl.BlockSpec((1, BA_W, CHUNK), lambda i: (i, 0, 0))],
        out_specs=[tok] * 4 + [sq, sq],
        out_shape=[jax.ShapeDtypeStruct((s, w), F32)] * 4 + [jax.ShapeDtypeStruct((n, NHEAD, CHUNK, CHUNK), F32)] * 2,
        compiler_params=_cparams(("parallel",)),
    )(q, k, v, bg, gcol, gt3)


def _state_decay(gtv, h):
    g8 = gtv[8 * h:8 * h + 8, CHUNK - 1:CHUNK]
    return jnp.exp(jnp.concatenate([g8] * (HEAD // 8), axis=0))


def gdn_seq_fwd(u, wd, qd, kd, qk, gt3):
    s, w = u.shape
    n = s // CHUNK

    def body(u_ref, w_ref, qd_ref, kd_ref, qk_ref, gt_ref, o_ref, st_ref, s_ref):
        @pl.when(pl.program_id(0) == 0)
        def _():
            s_ref[...] = jnp.zeros_like(s_ref)

        gtv = gt_ref[0]
        for h in range(NHEAD):
            sl = slice(h * HEAD, (h + 1) * HEAD)
            sh = s_ref[h]
            st_ref[0, h] = sh
            vn = u_ref[:, sl] - _dot(w_ref[:, sl], sh)
            o_ref[:, sl] = _dot(qd_ref[:, sl], sh) + _dot(qk_ref[0, h], vn)
            s_ref[h] = sh * _state_decay(gtv, h) + _dot(kd_ref[:, sl], vn, "tn")

    tok = pl.BlockSpec((CHUNK, w), lambda i: (i, 0))
    return pl.pallas_call(
        body, name="gdn_seq_fwd", grid=(n,),
        in_specs=[tok] * 4 + [pl.BlockSpec((1, NHEAD, CHUNK, CHUNK), lambda i: (i, 0, 0, 0)),
                              pl.BlockSpec((1, BA_W, CHUNK), lambda i: (i, 0, 0))],
        out_specs=[tok, pl.BlockSpec((1, NHEAD, HEAD, HEAD), lambda i: (i, 0, 0, 0))],
        out_shape=[jax.ShapeDtypeStruct((s, w), F32), jax.ShapeDtypeStruct((n, NHEAD, HEAD, HEAD), F32)],
        scratch_shapes=[pltpu.VMEM((NHEAD, HEAD, HEAD), F32)],
        compiler_params=_cparams(("arbitrary",)),
    )(u, wd, qd, kd, qk, gt3)


def gdn_seq_bwd(do, u, wd, qd, kd, qk, gt3, states):
    s, w = u.shape
    n = s // CHUNK

    def body(do_ref, u_ref, w_ref, qd_ref, kd_ref, qk_ref, gt_ref, st_ref,
             du_ref, dw_ref, dqd_ref, dkd_ref, dqk_ref, dgl_ref, ds_ref):
        @pl.when(pl.program_id(0) == 0)
        def _():
            ds_ref[...] = jnp.zeros_like(ds_ref)

        gtv = gt_ref[0]
        dgl_rows = []
        for h in range(NHEAD):
            sl = slice(h * HEAD, (h + 1) * HEAD)
            sh, dsp, doh = st_ref[0, h], ds_ref[h], do_ref[:, sl]
            wh, qdh, kdh, qkh = w_ref[:, sl], qd_ref[:, sl], kd_ref[:, sl], qk_ref[0, h]
            vn = u_ref[:, sl] - _dot(wh, sh)
            dvn = _dot(qkh, doh, "tn") + _dot(kdh, dsp)
            du_ref[:, sl] = dvn
            dw_ref[:, sl] = -_dot(dvn, sh, "nt")
            dqd_ref[:, sl] = _dot(doh, sh, "nt")
            dkd_ref[:, sl] = _dot(vn, dsp, "nt")
            dqk_ref[0, h] = _dot(doh, vn, "nt")
            dgl_rows.append(_colsum(sh * dsp))
            ds_ref[h] = dsp * _state_decay(gtv, h) + _dot(qdh, doh, "tn") - _dot(wh, dvn, "tn")
        dgl_ref[0] = jnp.concatenate(dgl_rows, axis=0)

    tok = pl.BlockSpec((CHUNK, w), lambda i: (n - 1 - i, 0))
    sq = pl.BlockSpec((1, NHEAD, CHUNK, CHUNK), lambda i: (n - 1 - i, 0, 0, 0))
    return pl.pallas_call(
        body, name="gdn_seq_bwd", grid=(n,),
        in_specs=[tok] * 5 + [sq, pl.BlockSpec((1, BA_W, CHUNK), lambda i: (n - 1 - i, 0, 0)),
                              pl.BlockSpec((1, NHEAD, HEAD, HEAD), lambda i: (n - 1 - i, 0, 0, 0))],
        out_specs=[tok] * 4 + [sq, pl.BlockSpec((1, NHEAD, HEAD), lambda i: (n - 1 - i, 0, 0))],
        out_shape=[jax.ShapeDtypeStruct((s, w), F32)] * 4 + [jax.ShapeDtypeStruct((n, NHEAD, CHUNK, CHUNK), F32),
                                                            jax.ShapeDtypeStruct((n, NHEAD, HEAD), F32)],
        scratch_shapes=[pltpu.VMEM((NHEAD, HEAD, HEAD), F32)],
        compiler_params=_cparams(("arbitrary",)),
    )(do, u, wd, qd, kd, qk, gt3, states)


def gdn_intra_bwd(q, k, v, bg, gcol, gt3, tinv, du, dw, dqd, dkd, dqk, dgl):
    s, w = q.shape
    n = s // CHUNK

    def body(q_ref, k_ref, v_ref, bg_ref, gcol_ref, gt_ref, t_ref, du_ref, dw_ref, dqd_ref, dkd_ref, dqk_ref, dgl_ref,
             dq_ref, dk_ref, dv_ref, dbg_ref):
        bgv, gcolv, gtv, dglv = bg_ref[...], gcol_ref[...], gt_ref[0], dgl_ref[0]
        ii, jj = _iota2((CHUNK, CHUNK), 0), _iota2((CHUNK, CHUNK), 1)
        triu = jnp.where(ii <= jj, 1.0, 0.0)
        ones = jnp.ones((CHUNK, BA_W), F32)
        lane = _iota2((CHUNK, BA_W), 1)
        row = _iota2((CHUNK, 1), 0)
        dbg = jnp.zeros((CHUNK, BA_W), F32)
        for h in range(NHEAD):
            sl = slice(h * HEAD, (h + 1) * HEAD)
            qh, kh, vh = q_ref[:, sl], k_ref[:, sl], v_ref[:, sl]
            beta, gcc, decay, strict, gl = _chunk_common(qh, kh, bgv, gcolv, gtv, h)
            t = t_ref[0, h]
            duh, dwh, dqdh, dkdh, dqkh = du_ref[:, sl], dw_ref[:, sl], dqd_ref[:, sl], dkd_ref[:, sl], dqk_ref[0, h]
            kb = kh * beta
            eg = jnp.exp(gcc)
            ekd = jnp.exp(gl - gcc)
            rv, rk = vh * beta, kb * eg
            m = _dot(kb, kh, "nt")
            p = _dot(qh, kh, "nt")
            drv = _dot3(t, duh, "tn")
            drk = _dot3(t, dwh, "tn")
            dt = _dot3(duh, rv, "nt") + _dot3(dwh, rk, "nt")
            da = jnp.where(strict, -_dot3(_dot3(t, dt, "tn"), t, "nt"), 0.0)
            dm = da * decay
            dpm = dqkh * decay
            dkb = _dot(dm, kh) + drk * eg
            dq = _dot(dpm, kh) + dqdh * eg
            dk = _dot(dm, kb, "tn") + _dot(dpm, qh, "tn") + dkdh * ekd + dkb * beta
            e = (da * m + dqkh * p) * decay
            sk = jnp.sum(dkdh * kh * ekd, axis=-1, keepdims=True)
            dgc = (jnp.sum(e, axis=-1, keepdims=True) - _dot3(e, ones, "tn")[:, 0:1]
                   + jnp.sum(dqdh * qh * eg, axis=-1, keepdims=True) - sk + jnp.sum(drk * rk, axis=-1, keepdims=True))
            dglast = jnp.sum(sk, axis=0, keepdims=True) + jnp.sum(dglv[h:h + 1, :], axis=-1, keepdims=True) * jnp.exp(gl)
            dgc = dgc + jnp.where(row == CHUNK - 1, dglast, 0.0)
            dg = _dot3(triu, dgc * ones)
            dbeta = jnp.sum(dkb * kh, axis=-1, keepdims=True) + jnp.sum(drv * vh, axis=-1, keepdims=True)
            dbg = dbg + jnp.where(lane == h, dbeta, 0.0) + jnp.where(lane == NHEAD + h, dg, 0.0)
            dq_ref[:, sl] = dq
            dk_ref[:, sl] = dk
            dv_ref[:, sl] = drv * beta
        dbg_ref[...] = dbg

    tok = pl.BlockSpec((CHUNK, w), lambda i: (i, 0))
    sm = pl.BlockSpec((CHUNK, BA_W), lambda i: (i, 0))
    sq = pl.BlockSpec((1, NHEAD, CHUNK, CHUNK), lambda i: (i, 0, 0, 0))
    return pl.pallas_call(
        body, name="gdn_intra_bwd", grid=(n,),
        in_specs=[tok, tok, tok, sm, sm, pl.BlockSpec((1, BA_W, CHUNK), lambda i: (i, 0, 0)), sq,
                  tok, tok, tok, tok, sq, pl.BlockSpec((1, NHEAD, HEAD), lambda i: (i, 0, 0))],
        out_specs=[tok] * 3 + [sm],
        out_shape=[jax.ShapeDtypeStruct((s, w), F32)] * 3 + [jax.ShapeDtypeStruct((s, BA_W), F32)],
        compiler_params=_cparams(("parallel",)),
    )(q, k, v, bg, gcol, gt3, tinv, du, dw, dqd, dkd, dqk, dgl)


def gdn_prep_bwd1(proj, conv_w, alog_pad, dt_pad, dq, dk, dv, dbg):
    s = proj.shape[0]
    tr = min(256, s)
    w = NHEAD * HEAD
    pad_w = BA_PAD

    def body(x_ref, halo_ref, ba_ref, w_ref, al_ref, dt_ref, dq_ref, dk_ref, dv_ref, dbg_ref,
             dc_ref, dba_ref, dw0_ref, dw1_ref, dw2_ref, dw3_ref, dal_ref, ddt_ref):
        i = pl.program_id(0)
        taps, c = _conv_silu_parts(x_ref[...], halo_ref[...], w_ref[...], i == 0)
        sv, dsv = _silu(c), _dsilu(c)
        for h in range(NHEAD):
            for base, d_ref, scale in ((0, dq_ref, HEAD ** -0.5), (w, dk_ref, 1.0)):
                sl = slice(base + h * HEAD, base + (h + 1) * HEAD)
                sh = sv[:, sl]
                dn = d_ref[:, h * HEAD:(h + 1) * HEAD]
                r = lax.rsqrt(jnp.sum(sh * sh, axis=-1, keepdims=True) + EPS)
                dsh = scale * (r * dn - sh * (r * r * r) * jnp.sum(dn * sh, axis=-1, keepdims=True))
                dc_ref[:, sl] = dsh * dsv[:, sl]
        dc_ref[:, 2 * w:] = dv_ref[...] * dsv[:, 2 * w:]
        dc = dc_ref[...]
        ba, dbgv = ba_ref[...], dbg_ref[...]
        lane = _iota2(ba.shape, 1)
        beta = _sigmoid(ba)
        ea = jnp.exp(al_ref[...])
        z = ba + dt_ref[...]
        g = -ea * _softplus(z)
        is_g = (lane >= NHEAD) & (lane < 2 * NHEAD)
        da_raw = jnp.where(is_g, dbgv * (-ea) * _sigmoid(z), 0.0)
        dba = jnp.where(lane < NHEAD, dbgv * beta * (1.0 - beta), da_raw)
        dba_ref[...] = jnp.concatenate([dba, jnp.zeros((tr, pad_w - BA_W), F32)], axis=1).astype(BF16)
        partial = [_colsum(dc * tp) for tp in taps] + [_colsum(jnp.where(is_g, dbgv * g, 0.0)), _colsum(da_raw)]
        red_refs = (dw0_ref, dw1_ref, dw2_ref, dw3_ref, dal_ref, ddt_ref)

        @pl.when(i == 0)
        def _():
            for r_, v_ in zip(red_refs, partial):
                r_[...] = v_

        @pl.when(i > 0)
        def _():
            for r_, v_ in zip(red_refs, partial):
                r_[...] += v_

    nb8 = tr // 8
    tok = pl.BlockSpec((tr, w), lambda i: (i, 0))
    one = lambda width: pl.BlockSpec((1, width), lambda i: (0, 0))
    return pl.pallas_call(
        body, name="gdn_prep_bwd1", grid=(s // tr,),
        in_specs=[pl.BlockSpec((tr, QKV_W), lambda i: (i, QKV_CB)),
                  pl.BlockSpec((8, QKV_W), lambda i: (jnp.maximum(i * nb8 - 1, 0), QKV_CB)),
                  pl.BlockSpec((tr, BA_W), lambda i: (i, BA_CB)),
                  pl.BlockSpec(conv_w.shape, lambda i: (0, 0)), one(BA_W), one(BA_W),
                  tok, tok, tok, pl.BlockSpec((tr, BA_W), lambda i: (i, 0))],
        out_specs=[pl.BlockSpec((tr, QKV_W), lambda i: (i, 0)), pl.BlockSpec((tr, pad_w), lambda i: (i, 0))]
        + [one(QKV_W)] * 4 + [one(BA_W)] * 2,
        out_shape=[jax.ShapeDtypeStruct((s, QKV_W), F32), jax.ShapeDtypeStruct((s, pad_w), BF16)]
        + [jax.ShapeDtypeStruct((1, QKV_W), F32)] * 4 + [jax.ShapeDtypeStruct((1, BA_W), F32)] * 2,
        compiler_params=_cparams(("arbitrary",)),
    )(proj, proj, proj, conv_w, alog_pad, dt_pad, dq, dk, dv, dbg)


def gdn_prep_bwd2(dc, conv_w):
    s = dc.shape[0]
    tr = min(256, s)
    nblk = s // tr
    nb8 = tr // 8

    def body(dc_ref, halo_ref, w_ref, o_ref):
        last = pl.program_id(0) == nblk - 1
        wv = w_ref[...]
        xs = jnp.concatenate([dc_ref[...], jnp.where(last, 0.0, halo_ref[...])], axis=0)
        acc = xs[:tr] * wv[3:4]
        for j in range(3):
            acc = acc + pltpu.roll(xs, tr + 8 - (3 - j), 0)[:tr] * wv[j:j + 1]
        o_ref[...] = acc.astype(BF16)

    return pl.pallas_call(
        body, name="gdn_prep_bwd2", grid=(nblk,),
        in_specs=[pl.BlockSpec((tr, QKV_W), lambda i: (i, 0)),
                  pl.BlockSpec((8, QKV_W), lambda i: (jnp.minimum((i + 1) * nb8, s // 8 - 1), 0)),
                  pl.BlockSpec(conv_w.shape, lambda i: (0, 0))],
        out_specs=pl.BlockSpec((tr, QKV_W), lambda i: (i, 0)),
        out_shape=jax.ShapeDtypeStruct((s, QKV_W), BF16),
        compiler_params=_cparams(("parallel",)),
    )(dc, dc, conv_w)


S5_W = S5_GROUPS * S5_STATE
S5_IN = S5_GROUPS * S5_GROUP
S5_TILES = 8
S5_TW, S5_TI = S5_W // S5_TILES, S5_IN // S5_TILES


def _s5_param_math(lr, li, ldt, br, bi):
    pr, pc = _iota2((S5_STATE, S5_STATE * S5_GROUP), 0), _iota2((S5_STATE, S5_STATE * S5_GROUP), 1)
    rep = jnp.where(pc // S5_GROUP == pr, 1.0, 0.0)
    dt = jnp.exp(ldt)
    mag = jnp.exp(lr * dt)
    ab_re, ab_im = mag * jnp.cos(li * dt), mag * jnp.sin(li * dt)
    den = lr * lr + li * li
    nr, ni = ab_re - 1.0, ab_im
    coef_re = (nr * lr + ni * li) / den
    coef_im = (ni * lr - nr * li) / den
    cr, ci = _dot(coef_re, rep, prec=HI), _dot(coef_im, rep, prec=HI)
    return ab_re, ab_im, cr * br - ci * bi, cr * bi + ci * br


def s5_param_fwd(lr, li, ldt, br, bi):
    def body(lr_ref, li_ref, ldt_ref, br_ref, bi_ref, ar_ref, ai_ref, bbr_ref, bbi_ref):
        res = _s5_param_math(lr_ref[...], li_ref[...], ldt_ref[...], br_ref[...], bi_ref[...])
        for r, v in zip((ar_ref, ai_ref, bbr_ref, bbi_ref), res):
            r[...] = v

    return pl.pallas_call(
        body, name="s5_param_fwd",
        out_shape=[jax.ShapeDtypeStruct(lr.shape, F32)] * 2 + [jax.ShapeDtypeStruct(br.shape, F32)] * 2,
        compiler_params=_cparams(),
    )(lr, li, ldt, br, bi)


def s5_param_bwd(lr, li, ldt, br, bi, dar, dai, dbbr, dbbi):
    def body(lr_ref, li_ref, ldt_ref, br_ref, bi_ref, dar_ref, dai_ref, dbbr_ref, dbbi_ref, *out_refs):
        _, vjp = jax.vjp(_s5_param_math, lr_ref[...], li_ref[...], ldt_ref[...], br_ref[...], bi_ref[...])
        for r, v in zip(out_refs, vjp((dar_ref[...], dai_ref[...], dbbr_ref[...], dbbi_ref[...]))):
            r[...] = v

    return pl.pallas_call(
        body, name="s5_param_bwd",
        out_shape=[jax.ShapeDtypeStruct(a.shape, F32) for a in (lr, li, ldt, br, bi)],
        compiler_params=_cparams(),
    )(lr, li, ldt, br, bi, dar, dai, dbbr, dbbi)


def _cmul(ar, ai, br, bi):
    return ar * br - ai * bi, ar * bi + ai * br


def _s5_power(ar, ai, steps):
    assert steps & (steps - 1) == 0
    for _ in range(steps.bit_length() - 1):
        ar, ai = _cmul(ar, ai, ar, ai)
    return ar, ai


def _s5_scan_rows(ar_ref, ai_ref, re_ref, im_ref, sr_ref, si_ref, tb, row0, reverse):
    quarter = S5_W // 4
    for qd in range(4):
        cs = slice(qd * quarter, (qd + 1) * quarter)
        are = jnp.broadcast_to(ar_ref[:, cs], (NSEG, quarter))
        aim = jnp.broadcast_to(ai_ref[:, cs], (NSEG, quarter))
        if reverse:
            aim = -aim

        def step(t, carry):
            h_r, h_i = carry
            tt = tb - 1 - t if reverse else t
            rows = pl.ds(pl.multiple_of(row0 + tt * NSEG, NSEG), NSEG)
            n_r = are * h_r - aim * h_i + re_ref[rows, cs]
            n_i = are * h_i + aim * h_r + im_ref[rows, cs]
            re_ref[rows, cs] = n_r
            im_ref[rows, cs] = n_i
            return n_r, n_i

        h_r, h_i = lax.fori_loop(0, tb, step, (sr_ref[:, cs], si_ref[:, cs]), unroll=4)
        sr_ref[:, cs] = h_r
        si_ref[:, cs] = h_i


def _s5_segment_carry(ar_ref, ai_ref, sr_ref, si_ref, steps, reverse):
    pr, pi = _s5_power(ar_ref[...], ai_ref[...], steps)
    if reverse:
        pi = -pi
    cur_r = jnp.zeros((1, S5_W), F32)
    cur_i = jnp.zeros((1, S5_W), F32)
    for s in (range(NSEG - 1, -1, -1) if reverse else range(NSEG)):
        e_r, e_i = sr_ref[s:s + 1, :], si_ref[s:s + 1, :]
        sr_ref[s:s + 1, :] = cur_r
        si_ref[s:s + 1, :] = cur_i
        nr, ni = _cmul(pr, pi, cur_r, cur_i)
        cur_r, cur_i = nr + e_r, ni + e_i


def _s5_blocks(s):
    steps = s // NSEG
    tb = min(32, steps)
    return steps, tb, NSEG * tb, steps // tb


def s5_scan_fwd(xp, a_re, a_im, bre, bim, cre, cim):
    s = xp.shape[0]
    steps, tb, rb, nb = _s5_blocks(s)

    def body(x_ref, ar_ref, ai_ref, bre_ref, bim_ref, cre_ref, cim_ref, y_ref, hsr_ref, hsi_ref,
             hr_ref, hi_ref, sr_ref, si_ref):
        ph, b = pl.program_id(0), pl.program_id(1)

        @pl.when((ph == 0) & (b == 0))
        def _():
            sr_ref[...] = jnp.zeros_like(sr_ref)
            si_ref[...] = jnp.zeros_like(si_ref)

        @pl.when((ph == 1) & (b == 0))
        def _():
            _s5_segment_carry(ar_ref, ai_ref, sr_ref, si_ref, steps, False)

        xv = x_ref[...].astype(BF16)
        for j in range(S5_TILES):
            xs = xv[:, j * S5_TI:(j + 1) * S5_TI]
            hr_ref[:, j * S5_TW:(j + 1) * S5_TW] = _dot(xs, bre_ref[j])
            hi_ref[:, j * S5_TW:(j + 1) * S5_TW] = _dot(xs, bim_ref[j])

        @pl.when(ph == 1)
        def _():
            hsr_ref[0] = sr_ref[...]
            hsi_ref[0] = si_ref[...]

        _s5_scan_rows(ar_ref, ai_ref, hr_ref, hi_ref, sr_ref, si_ref, tb, 0, False)

        @pl.when(ph == 1)
        def _():
            for j in range(S5_TILES):
                cs = slice(j * S5_TW, (j + 1) * S5_TW)
                y_ref[:, j * S5_TI:(j + 1) * S5_TI] = _dot(hr_ref[:, cs], cre_ref[j]) - _dot(hi_ref[:, cs], cim_ref[j])

    row = pl.BlockSpec((1, S5_W), lambda p, b: (0, 0))
    wb = pl.BlockSpec((S5_TILES, S5_TI, S5_TW), lambda p, b: (0, 0, 0))
    wc = pl.BlockSpec((S5_TILES, S5_TW, S5_TI), lambda p, b: (0, 0, 0))
    st = pl.BlockSpec((1, NSEG, S5_W), lambda p, b: (p * b, 0, 0))
    return pl.pallas_call(
        body, name="s5_scan_fwd", grid=(2, nb),
        in_specs=[pl.BlockSpec((rb, S5_IN), lambda p, b: (b, 0)), row, row, wb, wb, wc, wc],
        out_specs=[pl.BlockSpec((rb, S5_IN), lambda p, b: (p * b, 0)), st, st],
        out_shape=[jax.ShapeDtypeStruct((s, S5_IN), F32)] + [jax.ShapeDtypeStruct((nb, NSEG, S5_W), F32)] * 2,
        scratch_shapes=[pltpu.VMEM((rb, S5_W), F32)] * 2 + [pltpu.VMEM((NSEG, S5_W), F32)] * 2,
        compiler_params=_cparams(("arbitrary", "arbitrary")),
    )(xp, a_re, a_im, bre, bim, cre, cim)


def s5_scan_bwd(dyp, xp, a_re, a_im, bre, bim, cre_t, cim_t, hs_r, hs_i):
    s = xp.shape[0]
    steps, tb, rb, nb = _s5_blocks(s)

    def body(dy_ref, x_ref, ar_ref, ai_ref, bre_ref, bim_ref, crt_ref, cit_ref, hsr_ref, hsi_ref,
             dx_ref, dar_ref, dai_ref, dbr_ref, dbi_ref, dcr_ref, dci_ref,
             hr_ref, hi_ref, lr_ref, li_ref, sr_ref, si_ref, fr_ref, fi_ref, accr_ref, acci_ref):
        ph, b = pl.program_id(0), pl.program_id(1)

        @pl.when((ph == 0) & (b == 0))
        def _():
            sr_ref[...] = jnp.zeros_like(sr_ref)
            si_ref[...] = jnp.zeros_like(si_ref)

        @pl.when((ph == 1) & (b == 0))
        def _():
            _s5_segment_carry(ar_ref, ai_ref, sr_ref, si_ref, steps, True)
            for r in (accr_ref, acci_ref, dbr_ref, dbi_ref, dcr_ref, dci_ref):
                r[...] = jnp.zeros_like(r)

        dyv = dy_ref[...].astype(BF16)
        for j in range(S5_TILES):
            ds_ = dyv[:, j * S5_TI:(j + 1) * S5_TI]
            lr_ref[:, j * S5_TW:(j + 1) * S5_TW] = _dot(ds_, crt_ref[j])
            li_ref[:, j * S5_TW:(j + 1) * S5_TW] = -_dot(ds_, cit_ref[j])
        _s5_scan_rows(ar_ref, ai_ref, lr_ref, li_ref, sr_ref, si_ref, tb, 0, True)

        @pl.when(ph == 1)
        def _():
            xv = x_ref[...].astype(BF16)
            for j in range(S5_TILES):
                xs = xv[:, j * S5_TI:(j + 1) * S5_TI]
                hr_ref[NSEG:, j * S5_TW:(j + 1) * S5_TW] = _dot(xs, bre_ref[j])
                hi_ref[NSEG:, j * S5_TW:(j + 1) * S5_TW] = _dot(xs, bim_ref[j])
            hr_ref[0:NSEG, :] = hsr_ref[0]
            hi_ref[0:NSEG, :] = hsi_ref[0]
            fr_ref[...] = hsr_ref[0]
            fi_ref[...] = hsi_ref[0]
            _s5_scan_rows(ar_ref, ai_ref, hr_ref, hi_ref, fr_ref, fi_ref, tb, NSEG, False)
            lam_r, lam_i = lr_ref[...], li_ref[...]
            hp_r, hp_i = hr_ref[0:rb, :], hi_ref[0:rb, :]
            accr_ref[...] += jnp.sum((lam_r * hp_r + lam_i * hp_i).reshape(tb, NSEG, S5_W), axis=0)
            acci_ref[...] += jnp.sum((lam_i * hp_r - lam_r * hp_i).reshape(tb, NSEG, S5_W), axis=0)
            lam_rb, lam_ib = lam_r.astype(BF16), lam_i.astype(BF16)
            h_rb, h_ib = hr_ref[NSEG:, :].astype(BF16), hi_ref[NSEG:, :].astype(BF16)
            for j in range(S5_TILES):
                cs, ci = slice(j * S5_TW, (j + 1) * S5_TW), slice(j * S5_TI, (j + 1) * S5_TI)
                dbr_ref[j] += _dot(xv[:, ci], lam_rb[:, cs], "tn")
                dbi_ref[j] += _dot(xv[:, ci], lam_ib[:, cs], "tn")
                dx_ref[:, ci] = _dot(lam_rb[:, cs], bre_ref[j], "nt") + _dot(lam_ib[:, cs], bim_ref[j], "nt")
                dcr_ref[j] += _dot(h_rb[:, cs], dyv[:, ci], "tn")
                dci_ref[j] -= _dot(h_ib[:, cs], dyv[:, ci], "tn")

        @pl.when((ph == 1) & (b == nb - 1))
        def _():
            dar_ref[...] = jnp.sum(accr_ref[...], axis=0, keepdims=True)
            dai_ref[...] = jnp.sum(acci_ref[...], axis=0, keepdims=True)

    rev = lambda p, b: (nb - 1 - b, 0)
    row = pl.BlockSpec((1, S5_W), lambda p, b: (0, 0))
    wb = pl.BlockSpec((S5_TILES, S5_TI, S5_TW), lambda p, b: (0, 0, 0))
    wc = pl.BlockSpec((S5_TILES, S5_TW, S5_TI), lambda p, b: (0, 0, 0))
    st = pl.BlockSpec((1, NSEG, S5_W), lambda p, b: (nb - 1 - b, 0, 0))
    big = pltpu.VMEM((rb, S5_W), F32)
    big8 = pltpu.VMEM((rb + NSEG, S5_W), F32)
    small = pltpu.VMEM((NSEG, S5_W), F32)
    return pl.pallas_call(
        body, name="s5_scan_bwd", grid=(2, nb),
        in_specs=[pl.BlockSpec((rb, S5_IN), rev), pl.BlockSpec((rb, S5_IN), rev), row, row, wb, wb, wb, wb, st, st],
        out_specs=[pl.BlockSpec((rb, S5_IN), lambda p, b: (nb - 1 - p * b, 0)), row, row, wb, wb, wc, wc],
        out_shape=[jax.ShapeDtypeStruct((s, S5_IN), F32)] + [jax.ShapeDtypeStruct((1, S5_W), F32)] * 2
        + [jax.ShapeDtypeStruct((S5_TILES, S5_TI, S5_TW), F32)] * 2 + [jax.ShapeDtypeStruct((S5_TILES, S5_TW, S5_TI), F32)] * 2,
        scratch_shapes=[big8, big8, big, big, small, small, small, small, small, small],
        compiler_params=_cparams(("arbitrary", "arbitrary")),
    )(dyp, xp, a_re, a_im, bre, bim, cre_t, cim_t, hs_r, hs_i)


XA_DIM = 256
XA_W = XA_HEADS * XA_DIM


def _xa_probs(qh, kh):
    sc = _dot(qh, kh, "nt") * (XA_DIM ** -0.5)
    ex = jnp.exp(sc - jnp.max(sc, axis=-1, keepdims=True))
    return ex / jnp.sum(ex, axis=-1, keepdims=True)


def xa_fwd(proj, kv):
    s = proj.shape[0]
    tq = min(512, s)

    def body(q_ref, kv_ref, o_ref):
        for h in range(XA_HEADS):
            sl = slice(h * XA_DIM, (h + 1) * XA_DIM)
            p = _xa_probs(q_ref[:, sl], kv_ref[:, sl])
            o_ref[:, sl] = _dot(p, kv_ref[:, XA_W + h * XA_DIM:XA_W + (h + 1) * XA_DIM])

    return pl.pallas_call(
        body, name="xa_fwd", grid=(s // tq,),
        in_specs=[pl.BlockSpec((tq, XA_W), lambda i: (i, QC_CB)), pl.BlockSpec(kv.shape, lambda i: (0, 0))],
        out_specs=pl.BlockSpec((tq, XA_W), lambda i: (i, 0)),
        out_shape=jax.ShapeDtypeStruct((s, XA_W), F32),
        compiler_params=_cparams(("parallel",)),
    )(proj, kv)


def xa_bwd(do, proj, kv):
    s = proj.shape[0]
    tq = min(512, s)

    def body(do_ref, q_ref, kv_ref, dq_ref, dkv_ref):
        @pl.when(pl.program_id(0) == 0)
        def _():
            dkv_ref[...] = jnp.zeros_like(dkv_ref)

        for h in range(XA_HEADS):
            sl = slice(h * XA_DIM, (h + 1) * XA_DIM)
            sv = slice(XA_W + h * XA_DIM, XA_W + (h + 1) * XA_DIM)
            qh, kh, vh, doh = q_ref[:, sl], kv_ref[:, sl], kv_ref[:, sv], do_ref[:, sl]
            p = _xa_probs(qh, kh)
            dp = _dot(doh, vh, "nt")
            ds_ = p * (dp - jnp.sum(dp * p, axis=-1, keepdims=True)) * (XA_DIM ** -0.5)
            dq_ref[:, sl] = _dot(ds_, kh).astype(BF16)
            dkv_ref[:, sl] += _dot(ds_, qh, "tn")
            dkv_ref[:, sv] += _dot(p, doh, "tn")

    return pl.pallas_call(
        body, name="xa_bwd", grid=(s // tq,),
        in_specs=[pl.BlockSpec((tq, XA_W), lambda i: (i, 0)), pl.BlockSpec((tq, XA_W), lambda i: (i, QC_CB)),
                  pl.BlockSpec(kv.shape, lambda i: (0, 0))],
        out_specs=[pl.BlockSpec((tq, XA_W), lambda i: (i, 0)), pl.BlockSpec(kv.shape, lambda i: (0, 0))],
        out_shape=[jax.ShapeDtypeStruct((s, XA_W), BF16), jax.ShapeDtypeStruct(kv.shape, F32)],
        compiler_params=_cparams(("arbitrary",)),
    )(do, proj, kv)


def adamw(w, g, m, v, name):
    lead = (0,) * (w.ndim - 2)
    rows, cols = w.shape[-2:]
    tr = rows
    while tr * cols * 4 * 7 * 2 > 36 * 2 ** 20 and tr % 16 == 0:
        tr //= 2

    def body(w_ref, g_ref, m_ref, v_ref, d_ref, m2_ref, v2_ref):
        gv = g_ref[...]
        m2 = ADAM_B1 * m_ref[...] + (1.0 - ADAM_B1) * gv
        v2 = ADAM_B2 * v_ref[...] + (1.0 - ADAM_B2) * (gv * gv)
        m_hat = m2 / (1.0 - ADAM_B1 ** ADAM_STEP)
        v_hat = v2 / (1.0 - ADAM_B2 ** ADAM_STEP)
        d_ref[...] = -ADAM_LR * (m_hat / (jnp.sqrt(v_hat) + ADAM_EPS) + ADAM_WD * w_ref[...])
        m2_ref[...] = m2
        v2_ref[...] = v2

    spec = pl.BlockSpec((1,) * len(lead) + (tr, cols), lambda i: lead + (i, 0))
    return pl.pallas_call(
        body, name=name, grid=(rows // tr,), in_specs=[spec] * 4, out_specs=[spec] * 3,
        out_shape=[jax.ShapeDtypeStruct(w.shape, F32)] * 3, compiler_params=_cparams(("parallel",)),
    )(w, g, m, v)


def _seg_perm(a):
    s, w = a.shape
    return a.reshape(NSEG, s // NSEG, w).transpose(1, 0, 2).reshape(s, w)


def _seg_unperm(a):
    s, w = a.shape
    return a.reshape(s // NSEG, NSEG, w).transpose(1, 0, 2).reshape(s, w)


def _block_diag(t):
    nt, _, r, c = t.shape
    eye = jnp.eye(8, dtype=bool)
    return jnp.where(eye[None, :, None, :, None], t[:, :, :, None, :], 0.0).reshape(nt, 8 * r, 8 * c)


def _block_diag_inv(d, r, c):
    d5 = d.reshape(d.shape[0], 8, r, 8, c)
    return jnp.diagonal(d5, axis1=1, axis2=3).transpose(0, 3, 1, 2)


def _s5_b_tiles(bb):
    return _block_diag(bb.reshape(S5_TILES, 8, S5_STATE, S5_GROUP).transpose(0, 1, 3, 2))


def _s5_b_untile(d):
    return _block_diag_inv(d, S5_GROUP, S5_STATE).transpose(0, 1, 3, 2).reshape(S5_GROUPS, S5_STATE * S5_GROUP)


def _s5_c_tiles(c):
    return _block_diag(c.reshape(S5_TILES, 8, S5_GROUP, S5_STATE).transpose(0, 1, 3, 2))


def _s5_c_untile(d):
    return _block_diag_inv(d, S5_STATE, S5_GROUP).transpose(0, 1, 3, 2).reshape(S5_GROUPS, S5_GROUP, S5_STATE)


def s5_ssm_fwd(xb, lam_re, lam_im, log_dt, b_re, b_im, c_re, c_im):
    br, bi = b_re.reshape(S5_GROUPS, -1), b_im.reshape(S5_GROUPS, -1)
    ldt = log_dt.reshape(S5_GROUPS, 1)
    ab_re, ab_im, bb_re, bb_im = s5_param_fwd(lam_re, lam_im, ldt, br, bi)
    a_re, a_im = ab_re.reshape(1, S5_W), ab_im.reshape(1, S5_W)
    bre, bim = _s5_b_tiles(bb_re).astype(BF16), _s5_b_tiles(bb_im).astype(BF16)
    cre, cim = _s5_c_tiles(c_re).astype(BF16), _s5_c_tiles(c_im).astype(BF16)
    xp = _seg_perm(xb)
    yp, hs_r, hs_i = s5_scan_fwd(xp, a_re, a_im, bre, bim, cre, cim)
    saved = (xp, a_re, a_im, bre, bim, cre, cim, hs_r, hs_i, (lam_re, lam_im, ldt, br, bi))
    return _seg_unperm(yp), saved


def s5_ssm_bwd(dy, saved):
    xp, a_re, a_im, bre, bim, cre, cim, hs_r, hs_i, params = saved
    cre_t, cim_t = cre.transpose(0, 2, 1), cim.transpose(0, 2, 1)
    dxp, dar, dai, dbr, dbi, dcr, dci = s5_scan_bwd(_seg_perm(dy), xp, a_re, a_im, bre, bim, cre_t, cim_t, hs_r, hs_i)
    dlr, dli, dldt, db_re, db_im = s5_param_bwd(*params, dar.reshape(S5_GROUPS, S5_STATE), dai.reshape(S5_GROUPS, S5_STATE),
                                                _s5_b_untile(dbr), _s5_b_untile(dbi))
    shape_b = (S5_GROUPS, S5_STATE, S5_GROUP)
    return (_seg_unperm(dxp), dlr, dli, dldt.reshape(S5_GROUPS), db_re.reshape(shape_b), db_im.reshape(shape_b),
            _s5_c_untile(dcr), _s5_c_untile(dci))


_MESH = pl.DeviceIdType.MESH
_HBM = pl.BlockSpec(memory_space=pltpu.HBM)
N_DEV = 8


def _position():
    return lax.axis_index("x"), lax.axis_index("y"), lax.axis_index("c")


def _half(core, rows):
    h = rows // 2
    return pl.ds(pl.multiple_of(core * h, 16), h)


def allgather_weights(ws, convw):
    n = len(ws)

    def body(*refs):
        w_refs, c_ref = refs[:n], refs[n]
        wo_refs, co_ref = refs[n + 1:2 * n + 1], refs[2 * n + 1]
        send_sems, recv_sems, local_sems = refs[2 * n + 2:]
        x, y, c = _position()
        mine = 2 * x + y
        peers = [(1 - x, y), (x, 1 - y), (1 - x, 1 - y)]
        blocks = [2 * px + py for px, py in peers]
        local = [pltpu.make_async_copy(w_refs[i], wo_refs[i].at[mine], local_sems.at[i]) for i in range(n)]
        local.append(pltpu.make_async_copy(c_ref, co_ref.at[mine], local_sems.at[n]))
        for cp in local:
            cp.start()

        def ici(i, k, block):
            rows = _half(c, w_refs[i].shape[0])
            sem = 2 * (3 * i + k)
            return pltpu.make_async_remote_copy(src_ref=w_refs[i].at[rows, :], dst_ref=wo_refs[i].at[block, rows, :],
                                                send_sem=send_sems.at[sem], recv_sem=recv_sems.at[sem],
                                                device_id=(*peers[k], c), device_id_type=_MESH)

        def d2d(i, k, core):
            region = wo_refs[i].at[blocks[k], _half(core, w_refs[i].shape[0]), :]
            sem = 2 * (3 * i + k) + 1
            return pltpu.make_async_remote_copy(src_ref=region, dst_ref=region, send_sem=send_sems.at[sem],
                                                recv_sem=recv_sems.at[sem], device_id=(x, y, 1 - c), device_id_type=_MESH)

        def conv(k, block):
            return pltpu.make_async_remote_copy(src_ref=c_ref, dst_ref=co_ref.at[block], send_sem=send_sems.at[6 * n + k],
                                                recv_sem=recv_sems.at[6 * n + k], device_id=(*peers[k], c), device_id_type=_MESH)

        sends = [ici(i, k, mine) for k in range(3) for i in range(n)] + [conv(k, mine) for k in range(3)]
        for cp in sends:
            cp.start()
        for k in range(3):
            for i in range(n):
                ici(i, k, blocks[k]).wait_recv()
                fwd = d2d(i, k, c)
                fwd.start()
                sends.append(fwd)
        for k in range(3):
            conv(k, blocks[k]).wait_recv()
            for i in range(n):
                d2d(i, k, 1 - c).wait_recv()
        for cp in sends:
            cp.wait_send()
        for cp in local:
            cp.wait()

    nsem = 6 * n + 3
    return pl.pallas_call(
        body, name="allgather_weights", in_specs=[_HBM] * (n + 1), out_specs=[_HBM] * (n + 1),
        out_shape=[jax.ShapeDtypeStruct((4,) + w.shape, w.dtype) for w in ws] + [jax.ShapeDtypeStruct((4,) + convw.shape, convw.dtype)],
        scratch_shapes=[pltpu.SemaphoreType.DMA((nsem,)), pltpu.SemaphoreType.DMA((nsem,)), pltpu.SemaphoreType.DMA((n + 1,))],
    )(*ws, convw)


def exchange_cores(gs, small):
    n = len(gs)

    def body(*refs):
        g_refs, s_ref = refs[:n], refs[n]
        kept_refs, got_refs, so_ref = refs[n + 1:2 * n + 1], refs[2 * n + 1:3 * n + 1], refs[3 * n + 1]
        send_sems, recv_sems, local_sems = refs[3 * n + 2:]
        x, y, c = _position()
        me = 4 * x + 2 * y + c
        local = [pltpu.make_async_copy(g_refs[i].at[:, _half(c, g_refs[i].shape[1]), :], kept_refs[i], local_sems.at[i])
                 for i in range(n)]
        local.append(pltpu.make_async_copy(s_ref, so_ref.at[me], local_sems.at[n]))
        for cp in local:
            cp.start()

        def big(i, j):
            sem = 4 * i + j
            return pltpu.make_async_remote_copy(src_ref=g_refs[i].at[j, _half(1 - c, g_refs[i].shape[1]), :],
                                                dst_ref=got_refs[i].at[j], send_sem=send_sems.at[sem], recv_sem=recv_sems.at[sem],
                                                device_id=(x, y, 1 - c), device_id_type=_MESH)

        def tiny(r, sending):
            px, py, pc = (1 - x if r & 4 else x, 1 - y if r & 2 else y, 1 - c if r & 1 else c)
            slot = me if sending else 4 * px + 2 * py + pc
            sem = 4 * n + r - 1
            return pltpu.make_async_remote_copy(src_ref=s_ref, dst_ref=so_ref.at[slot], send_sem=send_sems.at[sem],
                                                recv_sem=recv_sems.at[sem], device_id=(px, py, pc), device_id_type=_MESH)

        sends = [big(i, j) for i in range(n) for j in range(4)] + [tiny(r, True) for r in range(1, N_DEV)]
        for cp in sends:
            cp.start()
        for i in range(n):
            for j in range(4):
                big(i, j).wait_recv()
        for r in range(1, N_DEV):
            tiny(r, False).wait_recv()
        for cp in sends:
            cp.wait_send()
        for cp in local:
            cp.wait()

    nsem = 4 * n + N_DEV - 1
    halves = [jax.ShapeDtypeStruct((4, g.shape[1] // 2, g.shape[2]), g.dtype) for g in gs]
    res = pl.pallas_call(
        body, name="exchange_cores", in_specs=[_HBM] * (n + 1), out_specs=[_HBM] * (2 * n + 1),
        out_shape=halves + halves + [jax.ShapeDtypeStruct((N_DEV,) + small.shape, small.dtype)],
        scratch_shapes=[pltpu.SemaphoreType.DMA((nsem,)), pltpu.SemaphoreType.DMA((nsem,)), pltpu.SemaphoreType.DMA((n + 1,))],
    )(*gs, small)
    return res[:n], res[n:2 * n], res[2 * n]


def exchange_chips(cs):
    n = len(cs)

    def body(*refs):
        c_refs, o_refs = refs[:n], refs[n:2 * n]
        send_sems, recv_sems, local_sems = refs[2 * n:]
        x, y, c = _position()
        mine = 2 * x + y
        peers = [(1 - x, y), (x, 1 - y), (1 - x, 1 - y)]
        blocks = [2 * px + py for px, py in peers]
        local = [pltpu.make_async_copy(c_refs[i].at[mine], o_refs[i].at[mine], local_sems.at[i]) for i in range(n)]
        for cp in local:
            cp.start()

        def copy(i, k, sending):
            sem = 3 * i + k
            return pltpu.make_async_remote_copy(src_ref=c_refs[i].at[blocks[k]], dst_ref=o_refs[i].at[mine if sending else blocks[k]],
                                                send_sem=send_sems.at[sem], recv_sem=recv_sems.at[sem],
                                                device_id=(*peers[k], c), device_id_type=_MESH)

        sends = [copy(i, k, True) for k in range(3) for i in range(n)]
        for cp in sends:
            cp.start()
        for k in range(3):
            for i in range(n):
                copy(i, k, False).wait_recv()
        for cp in sends:
            cp.wait_send()
        for cp in local:
            cp.wait()

    return pl.pallas_call(
        body, name="exchange_chips", in_specs=[_HBM] * n, out_specs=[_HBM] * n,
        out_shape=[jax.ShapeDtypeStruct(a.shape, a.dtype) for a in cs],
        scratch_shapes=[pltpu.SemaphoreType.DMA((3 * n,)), pltpu.SemaphoreType.DMA((3 * n,)), pltpu.SemaphoreType.DMA((n,))],
    )(*cs)


def pair_sum(a, b, name):
    nb, rows, cols = a.shape
    tr = _row_tile(rows, 16, max(16, (2 * 2 ** 20) // (cols * a.dtype.itemsize)))

    def body(a_ref, b_ref, o_ref):
        o_ref[...] = (a_ref[...].astype(F32) + b_ref[...].astype(F32)).astype(o_ref.dtype)

    spec = pl.BlockSpec((1, tr, cols), lambda j, i: (j, i, 0))
    return pl.pallas_call(
        body, name=name, grid=(nb, rows // tr), in_specs=[spec, spec], out_specs=spec,
        out_shape=jax.ShapeDtypeStruct(a.shape, a.dtype), compiler_params=_cparams(("parallel", "parallel")),
    )(a, b)


D2D_CHUNK_BYTES = 2 * 2 ** 20


def sibling_exchange(halves):
    n = len(halves)
    chunks = []
    for hf in halves:
        rows = hf.shape[0]
        k = max(1, min(8, (rows * hf.shape[1] * hf.dtype.itemsize) // D2D_CHUNK_BYTES))
        while rows % (8 * k):
            k -= 1
        chunks.append(k)
    first = [sum(chunks[:i]) for i in range(n)]

    def body(*refs):
        h_refs, o_refs = refs[:n], refs[n:2 * n]
        send_sems, recv_sems, local_sems = refs[2 * n:]
        x, y, c = _position()
        local = [pltpu.make_async_copy(h_refs[i], o_refs[i].at[0, _half(c, 2 * h_refs[i].shape[0]), :], local_sems.at[i])
                 for i in range(n)]
        for cp in local:
            cp.start()

        def copy(i, q, core):
            rows = h_refs[i].shape[0]
            step = rows // chunks[i]
            dst_rows = pl.ds(pl.multiple_of(core * rows + q * step, 8), step)
            return pltpu.make_async_remote_copy(src_ref=h_refs[i].at[pl.ds(q * step, step), :], dst_ref=o_refs[i].at[0, dst_rows, :],
                                                send_sem=send_sems.at[first[i] + q], recv_sem=recv_sems.at[first[i] + q],
                                                device_id=(x, y, 1 - c), device_id_type=_MESH)

        sends = [copy(i, q, c) for i in range(n) for q in range(chunks[i])]
        for cp in sends:
            cp.start()
        for i in range(n):
            for q in range(chunks[i]):
                copy(i, q, 1 - c).wait_recv()
        for cp in sends:
            cp.wait_send()
        for cp in local:
            cp.wait()

    nsem = sum(chunks)
    return pl.pallas_call(
        body, name="sibling_exchange", in_specs=[_HBM] * n, out_specs=[_HBM] * n,
        out_shape=[jax.ShapeDtypeStruct((1, 2 * hf.shape[0], hf.shape[1]), hf.dtype) for hf in halves],
        scratch_shapes=[pltpu.SemaphoreType.DMA((nsem,)), pltpu.SemaphoreType.DMA((nsem,)), pltpu.SemaphoreType.DMA((n,))],
    )(*halves)


def _row_tile(rows, unit, max_rows):
    best = unit
    for t in range(unit, min(rows, max_rows) + 1, unit):
        if rows % t == 0:
            best = t
    return best


def sum_pieces(pieces, name):
    n, rows, cols = pieces.shape
    tr = _row_tile(rows, 16, max(16, (6 * 2 ** 20) // (n * cols * pieces.dtype.itemsize)))

    def body(p_ref, o_ref):
        acc = p_ref[0].astype(F32)
        for i in range(1, n):
            acc = acc + p_ref[i].astype(F32)
        o_ref[...] = acc

    return pl.pallas_call(
        body, name=name, grid=(rows // tr,),
        in_specs=[pl.BlockSpec((n, tr, cols), lambda i: (0, i, 0))], out_specs=pl.BlockSpec((tr, cols), lambda i: (i, 0)),
        out_shape=jax.ShapeDtypeStruct((rows, cols), F32), compiler_params=_cparams(("parallel",)),
    )(pieces)


BIG = ("w_in", "s5_w_glu", "w_kv_mem", "w_br_a", "w_br_b", "w_br_c", "w_out")
COL_SHARDED = ("w_in", "s5_w_glu", "w_br_a", "w_br_b", "w_br_c")
SMALL = ("norm_g", "gdn_a_log", "gdn_dt_bias", "gdn_norm_g", "s5_lambda_re", "s5_lambda_im", "s5_log_dt",
         "s5_b_re", "s5_b_im", "s5_c_re", "s5_c_im", "s5_d", "mem_norm_g", "final_g")
WEIGHTS = ("norm_g", "w_in", "conv_w", "gdn_a_log", "gdn_dt_bias", "gdn_norm_g", "s5_lambda_re", "s5_lambda_im",
           "s5_log_dt", "s5_b_re", "s5_b_im", "s5_c_re", "s5_c_im", "s5_d", "s5_w_glu", "mem_norm_g", "w_kv_mem",
           "w_br_a", "w_br_b", "w_br_c", "w_out", "final_g")
W_IN_SPLIT = 4096


W_IN_COLS = PROJ_W - BA_PAD + 2 * NHEAD


def _pack_w_in(shards):
    cs = shards.shape[2]
    parts = []
    for a, b in ((0, W_IN_SPLIT), (W_IN_SPLIT + 2 * NHEAD, W_IN_COLS), (W_IN_SPLIT, W_IN_SPLIT + 2 * NHEAD)):
        while a < b:
            j = a // cs
            hi = min(b, (j + 1) * cs)
            parts.append(shards[j, :, a - j * cs:hi - j * cs])
            a = hi
    parts.append(jnp.zeros((shards.shape[1], BA_PAD - 2 * NHEAD), shards.dtype))
    return jnp.concatenate(parts, axis=1)


def _unpack_w_in(wp):
    cs = W_IN_COLS // 4
    moves = ((0, W_IN_SPLIT, 0), (W_IN_SPLIT, W_IN_SPLIT + 2 * NHEAD, PROJ_W - BA_PAD - W_IN_SPLIT),
             (W_IN_SPLIT + 2 * NHEAD, W_IN_COLS, -2 * NHEAD))
    shards = []
    for j in range(4):
        parts = []
        for lo, hi, shift in moves:
            s, e = max(j * cs, lo), min((j + 1) * cs, hi)
            if s < e:
                parts.append(wp[:, s + shift:e + shift])
        shards.append(jnp.concatenate(parts, axis=1))
    return jnp.stack(shards)


def _pack_small(arrs):
    parts = []
    for a in arrs:
        f = a.reshape(-1).astype(F32)
        parts.append(jnp.pad(f, (0, (-f.shape[0]) % 128)))
    flat = jnp.concatenate(parts)
    rows = flat.shape[0] // 128
    return jnp.pad(flat.reshape(rows, 128), ((0, (-rows) % 16), (0, 0)))


def _unpack_small(flat2d, shapes):
    f = flat2d.reshape(-1)
    out, off = [], 0
    for shp in shapes:
        n = math.prod(shp)
        out.append(f[off:off + n].reshape(shp))
        off += n + (-n) % 128
    return out


def kernel(x, mem, norm_g, w_in, conv_w, gdn_a_log, gdn_dt_bias, gdn_norm_g, s5_lambda_re, s5_lambda_im, s5_log_dt, s5_b_re, s5_b_im, s5_c_re, s5_c_im, s5_d, s5_w_glu, mem_norm_g, w_kv_mem, w_br_a, w_br_b, w_br_c, w_out, final_g, loss_target, m_norm_g, m_w_in, m_conv_w, m_gdn_a_log, m_gdn_dt_bias, m_gdn_norm_g, m_s5_lambda_re, m_s5_lambda_im, m_s5_log_dt, m_s5_b_re, m_s5_b_im, m_s5_c_re, m_s5_c_im, m_s5_d, m_s5_w_glu, m_mem_norm_g, m_w_kv_mem, m_w_br_a, m_w_br_b, m_w_br_c, m_w_out, m_final_g, v_norm_g, v_w_in, v_conv_w, v_gdn_a_log, v_gdn_dt_bias, v_gdn_norm_g, v_s5_lambda_re, v_s5_lambda_im, v_s5_log_dt, v_s5_b_re, v_s5_b_im, v_s5_c_re, v_s5_c_im, v_s5_d, v_s5_w_glu, v_mem_norm_g, v_w_kv_mem, v_w_br_a, v_w_br_b, v_w_br_c, v_w_out, v_final_g):
    wts = dict(norm_g=norm_g, w_in=w_in, conv_w=conv_w, gdn_a_log=gdn_a_log, gdn_dt_bias=gdn_dt_bias, gdn_norm_g=gdn_norm_g,
               s5_lambda_re=s5_lambda_re, s5_lambda_im=s5_lambda_im, s5_log_dt=s5_log_dt, s5_b_re=s5_b_re, s5_b_im=s5_b_im,
               s5_c_re=s5_c_re, s5_c_im=s5_c_im, s5_d=s5_d, s5_w_glu=s5_w_glu, mem_norm_g=mem_norm_g, w_kv_mem=w_kv_mem,
               w_br_a=w_br_a, w_br_b=w_br_b, w_br_c=w_br_c, w_out=w_out, final_g=final_g)
    mom = dict(norm_g=m_norm_g, w_in=m_w_in, conv_w=m_conv_w, gdn_a_log=m_gdn_a_log, gdn_dt_bias=m_gdn_dt_bias,
               gdn_norm_g=m_gdn_norm_g, s5_lambda_re=m_s5_lambda_re, s5_lambda_im=m_s5_lambda_im, s5_log_dt=m_s5_log_dt,
               s5_b_re=m_s5_b_re, s5_b_im=m_s5_b_im, s5_c_re=m_s5_c_re, s5_c_im=m_s5_c_im, s5_d=m_s5_d, s5_w_glu=m_s5_w_glu,
               mem_norm_g=m_mem_norm_g, w_kv_mem=m_w_kv_mem, w_br_a=m_w_br_a, w_br_b=m_w_br_b, w_br_c=m_w_br_c, w_out=m_w_out,
               final_g=m_final_g)
    vel = dict(norm_g=v_norm_g, w_in=v_w_in, conv_w=v_conv_w, gdn_a_log=v_gdn_a_log, gdn_dt_bias=v_gdn_dt_bias,
               gdn_norm_g=v_gdn_norm_g, s5_lambda_re=v_s5_lambda_re, s5_lambda_im=v_s5_lambda_im, s5_log_dt=v_s5_log_dt,
               s5_b_re=v_s5_b_re, s5_b_im=v_s5_b_im, s5_c_re=v_s5_c_re, s5_c_im=v_s5_c_im, s5_d=v_s5_d, s5_w_glu=v_s5_w_glu,
               mem_norm_g=v_mem_norm_g, w_kv_mem=v_w_kv_mem, w_br_a=v_w_br_a, w_br_b=v_w_br_b, w_br_c=v_w_br_c, w_out=v_w_out,
               final_g=v_final_g)
    x2, mem2, tgt = x[0], mem[0], loss_target[0]
    s, d = x2.shape
    n_chunks = s // CHUNK

    *gathered, cg = allgather_weights([wts[n][0].astype(BF16) for n in BIG], conv_w[0])
    full = {}
    for n, wg in zip(BIG[1:], gathered[1:]):
        rows, cols = wg.shape[1:]
        full[n] = wg.transpose(1, 0, 2).reshape(rows, 4 * cols) if n in COL_SHARDED else wg.reshape(4 * rows, cols)
    wp = _pack_w_in(gathered[0])
    conv_full = cg.transpose(1, 0, 2).reshape(conv_w.shape[1], -1)
    alog_pad = jnp.zeros((1, BA_W), F32).at[0, NHEAD:2 * NHEAD].set(gdn_a_log[0])
    dt_pad = jnp.zeros((1, BA_W), F32).at[0, NHEAD:2 * NHEAD].set(gdn_dt_bias[0])

    mm = functools.partial(matmul, tm=1024, tn=1024)
    u, r1 = rms_fwd(x2, norm_g, "rms_fwd_x")
    proj = mm(u, wp, mode="nn", out_dtype=F32, tk=2048, name="mm_proj")
    q, k, v, bg, gcol, gt = gdn_prep_fwd(proj, conv_full, alog_pad, dt_pad)
    gt3 = gt.reshape(BA_W, n_chunks, CHUNK).transpose(1, 0, 2)
    gu, gw, qd, kd, qk, tinv = gdn_intra_fwd(q, k, v, bg, gcol, gt3)
    o_raw, states = gdn_seq_fwd(gu, gw, qd, kd, qk, gt3)
    ga = gdn_out_fwd(o_raw, proj, ZA_CB, gdn_norm_g)

    xb = proj[:, XB_CB * S5_IN:(XB_CB + 1) * S5_IN]
    y_ssm, s5_saved = s5_ssm_fwd(xb, s5_lambda_re[0], s5_lambda_im[0], s5_log_dt[0], s5_b_re[0], s5_b_im[0],
                                 s5_c_re[0], s5_c_im[0])
    yb = s5_act_fwd(y_ssm, proj, XB_CB, s5_d)
    glu = mm(yb, full["s5_w_glu"], mode="nn", out_dtype=F32, tk=1024, name="mm_glu")
    gb = s5_glu_fwd(glu, proj, ZB_CB)

    mem_n, rm = rms_fwd(mem2, mem_norm_g, "rms_fwd_mem")
    kv = mm(mem_n, full["w_kv_mem"], mode="nn", out_dtype=BF16, tk=2048, name="mm_kv")
    o_c = xa_fwd(proj, kv)
    gcx = gate_fwd(o_c, proj, ZC_CB, "gate_fwd_c")

    pa = mm(ga, full["w_br_a"], mode="nn", out_dtype=F32, tk=1024, name="mm_pa")
    pb = mm(gb, full["w_br_b"], mode="nn", out_dtype=F32, tk=1024, name="mm_pb")
    pc = mm(gcx, full["w_br_c"], mode="nn", out_dtype=F32, tk=1024, name="mm_pc")
    merged = merge_fwd(pa, pb, pc, proj, GATE_CB)
    hres = mm(merged, full["w_out"], mode="nn", out_dtype=F32, tk=2048, name="mm_out")
    dh, dhb, loss_part, d_final_g = final_stage(x2, hres, tgt, final_g.reshape(1, d))

    gfull = {}
    dmerged = mm(dhb, full["w_out"], mode="nt", out_dtype=F32, tk=2048, name="mm_dmerged")
    gfull["w_out"] = mm(merged, dhb, mode="tn", out_dtype=BF16, tk=1024, name="mm_dw_out")
    dpa, dpb, dpc, dg0, dg1, dg2 = merge_bwd(dmerged, pa, pb, pc, proj, GATE_CB)
    dga = mm(dpa, full["w_br_a"], mode="nt", out_dtype=F32, tk=2048, name="mm_dga")
    dgb = mm(dpb, full["w_br_b"], mode="nt", out_dtype=F32, tk=2048, name="mm_dgb")
    dgc = mm(dpc, full["w_br_c"], mode="nt", out_dtype=F32, tk=2048, name="mm_dgc")
    gfull["w_br_a"] = mm(ga, dpa, mode="tn", out_dtype=BF16, tk=1024, name="mm_dw_a")
    gfull["w_br_b"] = mm(gb, dpb, mode="tn", out_dtype=BF16, tk=1024, name="mm_dw_b")
    gfull["w_br_c"] = mm(gcx, dpc, mode="tn", out_dtype=BF16, tk=1024, name="mm_dw_c")

    do_raw, dza, d_gdn_norm = gdn_out_bwd(dga, o_raw, proj, ZA_CB, gdn_norm_g)
    du_, dw_, dqd, dkd, dqk, dgl = gdn_seq_bwd(do_raw, gu, gw, qd, kd, qk, gt3, states)
    dq, dk, dv, dbg = gdn_intra_bwd(q, k, v, bg, gcol, gt3, tinv, du_, dw_, dqd, dkd, dqk, dgl)
    dc, dba, dcw0, dcw1, dcw2, dcw3, d_alog, d_dt = gdn_prep_bwd1(proj, conv_full, alog_pad, dt_pad, dq, dk, dv, dbg)
    dqkv = gdn_prep_bwd2(dc, conv_full)
    d_conv = jnp.concatenate([dcw0, dcw1, dcw2, dcw3], axis=0)

    dval, dgate, dzb = s5_glu_bwd(dgb, glu, proj, ZB_CB)
    dglu = jnp.concatenate([dval, dgate], axis=1)
    dyb = mm(dglu, full["s5_w_glu"], mode="nt", out_dtype=F32, tk=2048, name="mm_dyb")
    gfull["s5_w_glu"] = mm(yb, dglu, mode="tn", out_dtype=BF16, tk=1024, name="mm_dw_glu")
    dy_ssm, dxb_direct, d_s5_d = s5_act_bwd(dyb, y_ssm, proj, XB_CB, s5_d)
    dxb_scan, d_lre, d_lim, d_ldt, d_bre, d_bim, d_cre, d_cim = s5_ssm_bwd(dy_ssm, s5_saved)
    dxb = add_cast(dxb_direct, dxb_scan, "s5_dxb")

    do_c, dzc = gate_bwd(dgc, o_c, proj, ZC_CB, "gate_bwd_c")
    dqc, dkv = xa_bwd(do_c, proj, kv)
    gfull["w_kv_mem"] = mm(mem_n, dkv, mode="tn", out_dtype=BF16, tk=256, name="mm_dw_kv")
    dmem_n = mm(dkv, full["w_kv_mem"], mode="nt", out_dtype=F32, tk=2048, name="mm_dmem")
    d_mem_norm = rms_bwd_g(dmem_n, mem2, rm, "rms_bwd_mem")

    dproj = jnp.concatenate([dqkv, dza, dxb, dzb, dqc, dzc, dg0, dg1, dg2, dba], axis=1)
    dwp = matmul(u, dproj, mode="tn", out_dtype=BF16, tm=2048, tn=1024, tk=512, name="mm_dw_in")
    du = mm(dproj, wp, mode="nt", out_dtype=F32, tk=1024, name="mm_du")
    grad_x, d_norm_g = rms_bwd_x(du, x2, r1, norm_g, dh)

    by_shard = [_unpack_w_in(dwp)]
    for n in BIG[1:]:
        rows, cols = wts[n].shape[1:]
        g = gfull[n]
        by_shard.append(g.reshape(rows, 4, cols).transpose(1, 0, 2) if n in COL_SHARDED else g.reshape(4, rows, cols))
    small_g = dict(norm_g=d_norm_g, gdn_a_log=d_alog[:, NHEAD:2 * NHEAD], gdn_dt_bias=d_dt[:, NHEAD:2 * NHEAD],
                   gdn_norm_g=d_gdn_norm, s5_lambda_re=d_lre, s5_lambda_im=d_lim, s5_log_dt=d_ldt, s5_b_re=d_bre, s5_b_im=d_bim,
                   s5_c_re=d_cre, s5_c_im=d_cim, s5_d=d_s5_d, mem_norm_g=d_mem_norm, final_g=d_final_g)
    small_send = _pack_small([small_g[n] for n in SMALL] + [d_conv, loss_part])
    kept, got, got_small = exchange_cores(by_shard, small_send)
    chip_sums = [pair_sum(a, b, "sum_cores_" + n) for n, a, b in zip(BIG, kept, got)]
    from_chips = exchange_chips(chip_sums)
    halves = [sum_pieces(a, "sum_chips_" + n) for n, a in zip(BIG, from_chips)]
    small_sum = sum_pieces(got_small, "sum_small")
    grads = dict(zip(BIG, sibling_exchange(halves)))
    small_shapes = [wts[n].shape for n in SMALL] + [d_conv.shape, (1, 1)]
    *small_list, conv_g_full, loss_sum = _unpack_small(small_sum, small_shapes)
    grads.update(zip(SMALL, small_list))
    cw = conv_w.shape[2]
    shard_idx = 2 * lax.axis_index("x") + lax.axis_index("y")
    grads["conv_w"] = lax.dynamic_slice(conv_g_full, (0, shard_idx * cw), (conv_w.shape[1], cw))[None]

    delta, new_m, new_v = {}, {}, {}
    for n in BIG + ("conv_w",):
        delta[n], new_m[n], new_v[n] = adamw(wts[n], grads[n], mom[n], vel[n], "adamw_" + n)
    packed = [_pack_small([src[n] for n in SMALL]) for src in (wts, grads, mom, vel)]
    res = adamw(*packed, "adamw_small")
    shapes = [wts[n].shape for n in SMALL]
    for dst, flat in zip((delta, new_m, new_v), res):
        dst.update(zip(SMALL, _unpack_small(flat, shapes)))
    for n in SMALL:
        grads[n] = grads[n].reshape(wts[n].shape)

    return (loss_sum.reshape(()), grad_x.reshape(x.shape), *[grads[n] for n in WEIGHTS], *[delta[n] for n in WEIGHTS],
            *[new_m[n] for n in WEIGHTS], *[new_v[n] for n in WEIGHTS])
```

```python
import functools
import math

import jax
import jax.numpy as jnp
from jax import lax
from jax.experimental import pallas as pl
from jax.experimental.pallas import tpu as pltpu

F32 = jnp.float32
BF16 = jnp.bfloat16
HI = lax.Precision.HIGHEST

EPS = 1e-6
CHUNK = 64
HEAD = 128
NHEAD = 8
XA_HEADS = 4
S5_GROUPS = 64
S5_STATE = 64
S5_GROUP = 16
NSEG = 8
ADAM_LR, ADAM_B1, ADAM_B2, ADAM_EPS, ADAM_WD, ADAM_STEP = 0.001, 0.9, 0.999, 1e-08, 0.01, 10
VMEM_LIMIT = 56 * 2 ** 20


def _cparams(sem=None):
    return pltpu.CompilerParams(dimension_semantics=sem, vmem_limit_bytes=VMEM_LIMIT)


def _sigmoid(x):
    return 1.0 / (1.0 + jnp.exp(-x))


def _silu(x):
    return x * _sigmoid(x)


def _dsilu(x):
    s = _sigmoid(x)
    return s * (1.0 + x * (1.0 - s))


def _softplus(x):
    return jnp.maximum(x, 0.0) + jnp.log(1.0 + jnp.exp(-jnp.abs(x)))


_GELU_C = math.sqrt(2.0 / math.pi)


def _gelu(x):
    return 0.5 * x * (1.0 + jnp.tanh(_GELU_C * (x + 0.044715 * x * x * x)))


def _dgelu(x):
    t = jnp.tanh(_GELU_C * (x + 0.044715 * x * x * x))
    return 0.5 * (1.0 + t) + 0.5 * x * (1.0 - t * t) * _GELU_C * (1.0 + 3.0 * 0.044715 * x * x)


_DIMS = {"nn": (((1,), (0,)), ((), ())), "nt": (((1,), (1,)), ((), ())), "tn": (((0,), (0,)), ((), ()))}


def matmul(a, b, *, mode, out_dtype, tm, tn, tk, name):
    if mode == "nn":
        (m, k), n = a.shape, b.shape[1]
    elif mode == "nt":
        (m, k), n = a.shape, b.shape[0]
    else:
        (k, m), n = a.shape, b.shape[1]
    tm, tn, tk = min(tm, m), min(tn, n), min(tk, k)
    assert m % tm == 0 and n % tn == 0 and k % tk == 0, (name, m, n, k, tm, tn, tk)
    nk = k // tk
    dims = _DIMS[mode]

    def body(a_ref, b_ref, o_ref, *scratch):
        prod = lax.dot_general(a_ref[...].astype(BF16), b_ref[...].astype(BF16), dims, preferred_element_type=F32)
        if nk == 1:
            o_ref[...] = prod.astype(out_dtype)
            return
        acc_ref, = scratch
        kk = pl.program_id(2)

        @pl.when(kk == 0)
        def _():
            acc_ref[...] = prod

        @pl.when(kk > 0)
        def _():
            acc_ref[...] += prod

        @pl.when(kk == nk - 1)
        def _():
            o_ref[...] = acc_ref[...].astype(out_dtype)

    a_spec = pl.BlockSpec((tk, tm), lambda i, j, q: (q, i)) if mode == "tn" else pl.BlockSpec((tm, tk), lambda i, j, q: (i, q))
    b_spec = pl.BlockSpec((tn, tk), lambda i, j, q: (j, q)) if mode == "nt" else pl.BlockSpec((tk, tn), lambda i, j, q: (q, j))
    return pl.pallas_call(
        body, name=name, grid=(m // tm, n // tn, nk),
        in_specs=[a_spec, b_spec], out_specs=pl.BlockSpec((tm, tn), lambda i, j, q: (i, j)),
        out_shape=jax.ShapeDtypeStruct((m, n), out_dtype),
        scratch_shapes=[] if nk == 1 else [pltpu.VMEM((tm, tn), F32)],
        compiler_params=_cparams(("parallel", "parallel", "arbitrary")),
    )(a, b)


def rowwise(fn, ins, outs, *, rows, tr, name, consts=(), reds=()):
    tr = min(tr, rows)
    assert rows % tr == 0, (name, rows, tr)
    n_in, n_c, n_o = len(ins), len(consts), len(outs)

    def body(*refs):
        vals = [r[...] for r in refs[:n_in + n_c]]
        res = fn(*vals)
        o_refs = refs[n_in + n_c:]
        for r, v in zip(o_refs[:n_o], res[:n_o]):
            r[...] = v.astype(r.dtype)
        if reds:
            i = pl.program_id(0)

            @pl.when(i == 0)
            def _():
                for r, v in zip(o_refs[n_o:], res[n_o:]):
                    r[...] = v.astype(r.dtype)

            @pl.when(i > 0)
            def _():
                for r, v in zip(o_refs[n_o:], res[n_o:]):
                    r[...] += v.astype(r.dtype)

    in_specs = [pl.BlockSpec((tr, w), functools.partial(lambda i, cb: (i, cb), cb=cb)) for (_, w, cb) in ins]
    in_specs += [pl.BlockSpec(c.shape, lambda i: (0, 0)) for c in consts]
    out_specs = [pl.BlockSpec((tr, w), lambda i: (i, 0)) for (w, _) in outs]
    out_specs += [pl.BlockSpec(s, lambda i: (0, 0)) for (s, _) in reds]
    out_shape = [jax.ShapeDtypeStruct((rows, w), d) for (w, d) in outs]
    out_shape += [jax.ShapeDtypeStruct(s, d) for (s, d) in reds]
    res = pl.pallas_call(
        body, name=name, grid=(rows // tr,), in_specs=in_specs, out_specs=out_specs, out_shape=out_shape,
        compiler_params=_cparams(("arbitrary",) if reds else ("parallel",)),
    )(*[a for (a, _, _) in ins], *consts)
    return res


def _colsum(x):
    return jnp.sum(x, axis=0, keepdims=True)


def rms_fwd(x, g, name):
    s, d = x.shape

    def fn(xv, gv):
        r = lax.rsqrt(jnp.mean(xv * xv, axis=-1, keepdims=True) + EPS)
        return xv * r * gv, r

    return rowwise(fn, [(x, d, 0)], [(d, BF16), (1, F32)], rows=s, tr=256, name=name, consts=[g])


def rms_bwd_x(du, x, r, g, dh):
    s, d = x.shape

    def fn(duv, xv, rv, dhv, gv):
        dyg = duv * gv
        dx = rv * dyg - xv * (rv * rv * rv) * jnp.mean(dyg * xv, axis=-1, keepdims=True)
        return dhv + dx, _colsum(duv * xv * rv)

    return rowwise(fn, [(du, d, 0), (x, d, 0), (r, 1, 0), (dh, d, 0)], [(d, F32)], rows=s, tr=256,
                   name="rms_bwd_x", consts=[g], reds=[((1, d), F32)])


def rms_bwd_g(du, x, r, name):
    s, d = x.shape

    def fn(duv, xv, rv):
        return (_colsum(duv * xv * rv),)

    return rowwise(fn, [(du, d, 0), (x, d, 0), (r, 1, 0)], [], rows=s, tr=256, name=name, reds=[((1, d), F32)])[0]


def final_stage(x, hres, target, g):
    s, d = x.shape

    def fn(xv, hv, tv, gv):
        h = xv + hv
        r = lax.rsqrt(jnp.mean(h * h, axis=-1, keepdims=True) + EPS)
        y = h * r * gv
        e = y - tv
        loss = 0.5 * jnp.sum(jnp.sum(e * e, axis=-1, keepdims=True), axis=0, keepdims=True) / d
        dy = e / d
        dyg = dy * gv
        dh = r * dyg - h * (r * r * r) * jnp.mean(dyg * h, axis=-1, keepdims=True)
        return dh, dh, loss, _colsum(dy * h * r)

    return rowwise(fn, [(x, d, 0), (hres, d, 0), (target, d, 0)], [(d, F32), (d, BF16)], rows=s, tr=256,
                   name="final_stage", consts=[g], reds=[((1, 1), F32), ((1, d), F32)])


def merge_fwd(pa, pb, pc, proj, gate_cb):
    s, d = pa.shape

    def fn(a, b, c, g0, g1, g2):
        return (_sigmoid(g0) * a + _sigmoid(g1) * b + _sigmoid(g2) * c,)

    ins = [(pa, d, 0), (pb, d, 0), (pc, d, 0)] + [(proj, d, gate_cb + i) for i in range(3)]
    return rowwise(fn, ins, [(d, BF16)], rows=s, tr=256, name="merge_fwd")[0]


def merge_bwd(dm, pa, pb, pc, proj, gate_cb):
    s, d = pa.shape

    def fn(dmv, a, b, c, g0, g1, g2):
        s0, s1, s2 = _sigmoid(g0), _sigmoid(g1), _sigmoid(g2)
        return (dmv * s0, dmv * s1, dmv * s2,
                dmv * a * s0 * (1.0 - s0), dmv * b * s1 * (1.0 - s1), dmv * c * s2 * (1.0 - s2))

    ins = [(dm, d, 0), (pa, d, 0), (pb, d, 0), (pc, d, 0)] + [(proj, d, gate_cb + i) for i in range(3)]
    return rowwise(fn, ins, [(d, BF16)] * 6, rows=s, tr=128, name="merge_bwd")


def gate_fwd(o, proj, z_cb, name):
    s, w = o.shape

    def fn(ov, zv):
        return (ov * _silu(zv),)

    return rowwise(fn, [(o, w, 0), (proj, w, z_cb)], [(w, BF16)], rows=s, tr=512, name=name)[0]


def gate_bwd(dgo, o, proj, z_cb, name):
    s, w = o.shape

    def fn(dv, ov, zv):
        return dv * _silu(zv), dv * ov * _dsilu(zv)

    return rowwise(fn, [(dgo, w, 0), (o, w, 0), (proj, w, z_cb)], [(w, F32), (w, BF16)], rows=s, tr=512, name=name)


def gdn_out_fwd(o_raw, proj, z_cb, gn):
    s, w = o_raw.shape

    def fn(ov, zv, gv):
        outs = []
        for h in range(NHEAD):
            oh = ov[:, h * HEAD:(h + 1) * HEAD]
            r = lax.rsqrt(jnp.mean(oh * oh, axis=-1, keepdims=True) + EPS)
            outs.append(oh * r * gv)
        return (jnp.concatenate(outs, axis=1) * _silu(zv),)

    return rowwise(fn, [(o_raw, w, 0), (proj, w, z_cb)], [(w, BF16)], rows=s, tr=512, name="gdn_out_fwd", consts=[gn])[0]


def gdn_out_bwd(dga, o_raw, proj, z_cb, gn):
    s, w = o_raw.shape

    def fn(dv, ov, zv, gv):
        sz, dsz = _silu(zv), _dsilu(zv)
        do_l, dz_l = [], []
        dg = jnp.zeros((1, HEAD), F32)
        for h in range(NHEAD):
            sl = slice(h * HEAD, (h + 1) * HEAD)
            oh, dgh = ov[:, sl], dv[:, sl]
            r = lax.rsqrt(jnp.mean(oh * oh, axis=-1, keepdims=True) + EPS)
            on = oh * r * gv
            don = dgh * sz[:, sl]
            dz_l.append(dgh * on * dsz[:, sl])
            dg = dg + _colsum(don * oh * r)
            dyg = don * gv
            do_l.append(r * dyg - oh * (r * r * r) * jnp.mean(dyg * oh, axis=-1, keepdims=True))
        return jnp.concatenate(do_l, axis=1), jnp.concatenate(dz_l, axis=1), dg

    return rowwise(fn, [(dga, w, 0), (o_raw, w, 0), (proj, w, z_cb)], [(w, F32), (w, BF16)], rows=s, tr=512,
                   name="gdn_out_bwd", consts=[gn], reds=[((1, HEAD), F32)])


def s5_act_fwd(y_ssm, proj, xb_cb, dvec):
    s, w = y_ssm.shape

    def fn(yv, xv, dv):
        return (_gelu(yv + dv * xv),)

    return rowwise(fn, [(y_ssm, w, 0), (proj, w, xb_cb)], [(w, BF16)], rows=s, tr=512, name="s5_act_fwd", consts=[dvec])[0]


def s5_act_bwd(dyb, y_ssm, proj, xb_cb, dvec):
    s, w = y_ssm.shape

    def fn(dv_, yv, xv, dv):
        dpre = dv_ * _dgelu(yv + dv * xv)
        return dpre, dpre * dv, _colsum(dpre * xv)

    return rowwise(fn, [(dyb, w, 0), (y_ssm, w, 0), (proj, w, xb_cb)], [(w, F32), (w, F32)], rows=s, tr=512,
                   name="s5_act_bwd", consts=[dvec], reds=[((1, w), F32)])


def s5_glu_fwd(glu, proj, z_cb):
    s, w2 = glu.shape
    w = w2 // 2

    def fn(val, gate, zv):
        return (val * _sigmoid(gate) * _silu(zv),)

    return rowwise(fn, [(glu, w, 0), (glu, w, 1), (proj, w, z_cb)], [(w, BF16)], rows=s, tr=512, name="s5_glu_fwd")[0]


def s5_glu_bwd(dgb, glu, proj, z_cb):
    s, w2 = glu.shape
    w = w2 // 2

    def fn(dv, val, gate, zv):
        sg = _sigmoid(gate)
        ob = val * sg
        dob = dv * _silu(zv)
        return dob * sg, dob * val * sg * (1.0 - sg), dv * ob * _dsilu(zv)

    return rowwise(fn, [(dgb, w, 0), (glu, w, 0), (glu, w, 1), (proj, w, z_cb)], [(w, BF16)] * 3, rows=s, tr=512,
                   name="s5_glu_bwd")


def add_cast(a, b, name):
    s, w = a.shape

    def fn(av, bv):
        return (av + bv,)

    return rowwise(fn, [(a, w, 0), (b, w, 0)], [(w, BF16)], rows=s, tr=512, name=name)[0]


QKV_W, QKV_CB = 3072, 0
ZA_CB, XB_CB, ZB_CB, QC_CB, ZC_CB = 3, 4, 5, 6, 7
GATE_CB = 4
BA_CB, BA_W = 112, 128
BA_PAD = 1024
PROJ_W = 14336 + BA_PAD


def _dot(a, b, dims="nn", prec=None):
    if prec is None:
        a, b = a.astype(BF16), b.astype(BF16)
    return lax.dot_general(a, b, _DIMS[dims], preferred_element_type=F32, precision=prec)


def _split(a):
    hi = a.astype(BF16)
    return hi, (a - hi.astype(F32)).astype(BF16)


def _dot3(a, b, dims="nn"):
    (ah, al), (bh, bl) = _split(a), _split(b)
    d = functools.partial(lax.dot_general, dimension_numbers=_DIMS[dims], preferred_element_type=F32)
    return d(ah, bh) + (d(ah, bl) + d(al, bh))


def _iota2(shape, dim):
    return lax.broadcasted_iota(jnp.int32, shape, dim)


def _conv_taps(xs, tr, k):
    if k == 0:
        return xs[8:8 + tr]
    return pltpu.roll(xs, k, 0)[8:8 + tr]


def _conv_silu_parts(xv, halo, wv, first):
    tr = xv.shape[0]
    xs = jnp.concatenate([jnp.where(first, 0.0, halo), xv], axis=0)
    taps = [_conv_taps(xs, tr, 3 - j) for j in range(4)]
    c = taps[0] * wv[0:1] + taps[1] * wv[1:2] + taps[2] * wv[2:3] + taps[3] * wv[3:4]
    return taps, c


def gdn_prep_fwd(proj, conv_w, alog_pad, dt_pad):
    s = proj.shape[0]
    tr = min(256, s)
    w = NHEAD * HEAD

    def body(x_ref, halo_ref, ba_ref, w_ref, al_ref, dt_ref, q_ref, k_ref, v_ref, bg_ref, gcol_ref, gt_ref):
        first = pl.program_id(0) == 0
        _, c = _conv_silu_parts(x_ref[...], halo_ref[...], w_ref[...], first)
        sv = _silu(c)
        for h in range(NHEAD):
            sl = slice(h * HEAD, (h + 1) * HEAD)
            qh, kh = sv[:, h * HEAD:(h + 1) * HEAD], sv[:, w + h * HEAD:w + (h + 1) * HEAD]
            q_ref[:, sl] = qh * lax.rsqrt(jnp.sum(qh * qh, axis=-1, keepdims=True) + EPS) * (HEAD ** -0.5)
            k_ref[:, sl] = kh * lax.rsqrt(jnp.sum(kh * kh, axis=-1, keepdims=True) + EPS)
        v_ref[...] = sv[:, 2 * w:]
        ba = ba_ref[...]
        lane = _iota2(ba.shape, 1)
        beta = _sigmoid(ba)
        g = -jnp.exp(al_ref[...]) * _softplus(ba + dt_ref[...])
        bg = jnp.where(lane < NHEAD, beta, jnp.where(lane < 2 * NHEAD, g, 0.0))
        bg_ref[...] = bg
        er, ec = _iota2((BA_W, BA_W), 0), _iota2((BA_W, BA_W), 1)
        expand = jnp.where((er == NHEAD + ec // 8) & (ec < 8 * NHEAD), 1.0, 0.0)
        grep = _dot(bg, expand, prec=HI)
        lr, lc = _iota2((tr, tr), 0), _iota2((tr, tr), 1)
        tril = jnp.where((lr // CHUNK == lc // CHUNK) & (lr >= lc), 1.0, 0.0)
        gc = _dot(tril, grep, prec=HI)
        gcol_ref[...] = gc
        gt_ref[...] = gc.T

    nb8 = tr // 8
    return pl.pallas_call(
        body, name="gdn_prep_fwd", grid=(s // tr,),
        in_specs=[pl.BlockSpec((tr, QKV_W), lambda i: (i, QKV_CB)),
                  pl.BlockSpec((8, QKV_W), lambda i: (jnp.maximum(i * nb8 - 1, 0), QKV_CB)),
                  pl.BlockSpec((tr, BA_W), lambda i: (i, BA_CB)),
                  pl.BlockSpec(conv_w.shape, lambda i: (0, 0)),
                  pl.BlockSpec((1, BA_W), lambda i: (0, 0)), pl.BlockSpec((1, BA_W), lambda i: (0, 0))],
        out_specs=[pl.BlockSpec((tr, w), lambda i: (i, 0))] * 3 + [pl.BlockSpec((tr, BA_W), lambda i: (i, 0))] * 2
        + [pl.BlockSpec((BA_W, tr), lambda i: (0, i))],
        out_shape=[jax.ShapeDtypeStruct((s, w), F32)] * 3 + [jax.ShapeDtypeStruct((s, BA_W), F32)] * 2
        + [jax.ShapeDtypeStruct((BA_W, s), F32)],
        compiler_params=_cparams(("parallel",)),
    )(proj, proj, proj, conv_w, alog_pad, dt_pad)


def _chunk_common(qh, kh, bgv, gcolv, gtv, h):
    beta = bgv[:, h:h + 1]
    gcc = gcolv[:, 8 * h:8 * h + 1]
    gcr = jnp.concatenate([gtv[8 * h:8 * h + 8, :]] * (CHUNK // 8), axis=0)
    ii, jj = _iota2((CHUNK, CHUNK), 0), _iota2((CHUNK, CHUNK), 1)
    incl, strict = ii >= jj, ii > jj
    decay = jnp.where(incl, jnp.exp(jnp.where(incl, gcc - gcr, 0.0)), 0.0)
    gl = gcr[:, CHUNK - 1:CHUNK]
    return beta, gcc, decay, strict, gl


def gdn_intra_fwd(q, k, v, bg, gcol, gt3):
    s, w = q.shape
    n = s // CHUNK

    def body(q_ref, k_ref, v_ref, bg_ref, gcol_ref, gt_ref, u_ref, w_ref, qd_ref, kd_ref, qk_ref, t_ref):
        bgv, gcolv, gtv = bg_ref[...], gcol_ref[...], gt_ref[0]
        ii, jj = _iota2((CHUNK, CHUNK), 0), _iota2((CHUNK, CHUNK), 1)
        eye = jnp.where(ii == jj, 1.0, 0.0)
        ps, ts, rhs = [], [], []
        for h in range(NHEAD):
            sl = slice(h * HEAD, (h + 1) * HEAD)
            qh, kh, vh = q_ref[:, sl], k_ref[:, sl], v_ref[:, sl]
            beta, gcc, decay, strict, gl = _chunk_common(qh, kh, bgv, gcolv, gtv, h)
            kb = kh * beta
            eg = jnp.exp(gcc)
            p = -jnp.where(strict, _dot(kb, kh, "nt") * decay, 0.0)
            ps.append(p)
            ts.append(eye + p)
            rhs.append((vh * beta, kb * eg))
            qd_ref[:, sl] = qh * eg
            kd_ref[:, sl] = kh * jnp.exp(gl - gcc)
            qk_ref[0, h] = _dot(qh, kh, "nt") * decay
        for _ in range(5):
            ps = [_dot3(p, p) for p in ps]
            ts = [t + _dot3(t, p) for t, p in zip(ts, ps)]
        for h in range(NHEAD):
            sl = slice(h * HEAD, (h + 1) * HEAD)
            u_ref[:, sl] = _dot3(ts[h], rhs[h][0])
            w_ref[:, sl] = _dot3(ts[h], rhs[h][1])
            t_ref[0, h] = ts[h]

    tok = pl.BlockSpec((CHUNK, w), lambda i: (i, 0))
    sm = pl.BlockSpec((CHUNK, BA_W), lambda i: (i, 0))
    sq = pl.BlockSpec((1, NHEAD, CHUNK, CHUNK), lambda i: (i, 0, 0, 0))
    return pl.pallas_call(
        body, name="gdn_intra_fwd", grid=(n,),
        in_specs=[tok, tok, tok, sm, sm, pl.BlockSpec((1, BA_W, CHUNK), lambda i: (i, 0, 0))],
        out_specs=[tok] * 4 + [sq, sq],
        out_shape=[jax.ShapeDtypeStruct((s, w), F32)] * 4 + [jax.ShapeDtypeStruct((n, NHEAD, CHUNK, CHUNK), F32)] * 2,
        compiler_params=_cparams(("parallel",)),
    )(q, k, v, bg, gcol, gt3)


def _state_decay(gtv, h):
    g8 = gtv[8 * h:8 * h + 8, CHUNK - 1:CHUNK]
    return jnp.exp(jnp.concatenate([g8] * (HEAD // 8), axis=0))


def gdn_seq_fwd(u, wd, qd, kd, qk, gt3):
    s, w = u.shape
    n = s // CHUNK

    def body(u_ref, w_ref, qd_ref, kd_ref, qk_ref, gt_ref, o_ref, st_ref, s_ref):
        @pl.when(pl.program_id(0) == 0)
        def _():
            s_ref[...] = jnp.zeros_like(s_ref)

        gtv = gt_ref[0]
        for h in range(NHEAD):
            sl = slice(h * HEAD, (h + 1) * HEAD)
            sh = s_ref[h]
            st_ref[0, h] = sh
            vn = u_ref[:, sl] - _dot(w_ref[:, sl], sh)
            o_ref[:, sl] = _dot(qd_ref[:, sl], sh) + _dot(qk_ref[0, h], vn)
            s_ref[h] = sh * _state_decay(gtv, h) + _dot(kd_ref[:, sl], vn, "tn")

    tok = pl.BlockSpec((CHUNK, w), lambda i: (i, 0))
    return pl.pallas_call(
        body, name="gdn_seq_fwd", grid=(n,),
        in_specs=[tok] * 4 + [pl.BlockSpec((1, NHEAD, CHUNK, CHUNK), lambda i: (i, 0, 0, 0)),
                              pl.BlockSpec((1, BA_W, CHUNK), lambda i: (i, 0, 0))],
        out_specs=[tok, pl.BlockSpec((1, NHEAD, HEAD, HEAD), lambda i: (i, 0, 0, 0))],
        out_shape=[jax.ShapeDtypeStruct((s, w), F32), jax.ShapeDtypeStruct((n, NHEAD, HEAD, HEAD), F32)],
        scratch_shapes=[pltpu.VMEM((NHEAD, HEAD, HEAD), F32)],
        compiler_params=_cparams(("arbitrary",)),
    )(u, wd, qd, kd, qk, gt3)


def gdn_seq_bwd(do, u, wd, qd, kd, qk, gt3, states):
    s, w = u.shape
    n = s // CHUNK

    def body(do_ref, u_ref, w_ref, qd_ref, kd_ref, qk_ref, gt_ref, st_ref,
             du_ref, dw_ref, dqd_ref, dkd_ref, dqk_ref, dgl_ref, ds_ref):
        @pl.when(pl.program_id(0) == 0)
        def _():
            ds_ref[...] = jnp.zeros_like(ds_ref)

        gtv = gt_ref[0]
        dgl_rows = []
        for h in range(NHEAD):
            sl = slice(h * HEAD, (h + 1) * HEAD)
            sh, dsp, doh = st_ref[0, h], ds_ref[h], do_ref[:, sl]
            wh, qdh, kdh, qkh = w_ref[:, sl], qd_ref[:, sl], kd_ref[:, sl], qk_ref[0, h]
            vn = u_ref[:, sl] - _dot(wh, sh)
            dvn = _dot(qkh, doh, "tn") + _dot(kdh, dsp)
            du_ref[:, sl] = dvn
            dw_ref[:, sl] = -_dot(dvn, sh, "nt")
            dqd_ref[:, sl] = _dot(doh, sh, "nt")
            dkd_ref[:, sl] = _dot(vn, dsp, "nt")
            dqk_ref[0, h] = _dot(doh, vn, "nt")
            dgl_rows.append(_colsum(sh * dsp))
            ds_ref[h] = dsp * _state_decay(gtv, h) + _dot(qdh, doh, "tn") - _dot(wh, dvn, "tn")
        dgl_ref[0] = jnp.concatenate(dgl_rows, axis=0)

    tok = pl.BlockSpec((CHUNK, w), lambda i: (n - 1 - i, 0))
    sq = pl.BlockSpec((1, NHEAD, CHUNK, CHUNK), lambda i: (n - 1 - i, 0, 0, 0))
    return pl.pallas_call(
        body, name="gdn_seq_bwd", grid=(n,),
        in_specs=[tok] * 5 + [sq, pl.BlockSpec((1, BA_W, CHUNK), lambda i: (n - 1 - i, 0, 0)),
                              pl.BlockSpec((1, NHEAD, HEAD, HEAD), lambda i: (n - 1 - i, 0, 0, 0))],
        out_specs=[tok] * 4 + [sq, pl.BlockSpec((1, NHEAD, HEAD), lambda i: (n - 1 - i, 0, 0))],
        out_shape=[jax.ShapeDtypeStruct((s, w), F32)] * 4 + [jax.ShapeDtypeStruct((n, NHEAD, CHUNK, CHUNK), F32),
                                                            jax.ShapeDtypeStruct((n, NHEAD, HEAD), F32)],
        scratch_shapes=[pltpu.VMEM((NHEAD, HEAD, HEAD), F32)],
        compiler_params=_cparams(("arbitrary",)),
    )(do, u, wd, qd, kd, qk, gt3, states)


def gdn_intra_bwd(q, k, v, bg, gcol, gt3, tinv, du, dw, dqd, dkd, dqk, dgl):
    s, w = q.shape
    n = s // CHUNK

    def body(q_ref, k_ref, v_ref, bg_ref, gcol_ref, gt_ref, t_ref, du_ref, dw_ref, dqd_ref, dkd_ref, dqk_ref, dgl_ref,
             dq_ref, dk_ref, dv_ref, dbg_ref):
        bgv, gcolv, gtv, dglv = bg_ref[...], gcol_ref[...], gt_ref[0], dgl_ref[0]
        ii, jj = _iota2((CHUNK, CHUNK), 0), _iota2((CHUNK, CHUNK), 1)
        triu = jnp.where(ii <= jj, 1.0, 0.0)
        ones = jnp.ones((CHUNK, BA_W), F32)
        lane = _iota2((CHUNK, BA_W), 1)
        row = _iota2((CHUNK, 1), 0)
        dbg = jnp.zeros((CHUNK, BA_W), F32)
        for h in range(NHEAD):
            sl = slice(h * HEAD, (h + 1) * HEAD)
            qh, kh, vh = q_ref[:, sl], k_ref[:, sl], v_ref[:, sl]
            beta, gcc, decay, strict, gl = _chunk_common(qh, kh, bgv, gcolv, gtv, h)
            t = t_ref[0, h]
            duh, dwh, dqdh, dkdh, dqkh = du_ref[:, sl], dw_ref[:, sl], dqd_ref[:, sl], dkd_ref[:, sl], dqk_ref[0, h]
            kb = kh * beta
            eg = jnp.exp(gcc)
            ekd = jnp.exp(gl - gcc)
            rv, rk = vh * beta, kb * eg
            m = _dot(kb, kh, "nt")
            p = _dot(qh, kh, "nt")
            drv = _dot3(t, duh, "tn")
            drk = _dot3(t, dwh, "tn")
            dt = _dot3(duh, rv, "nt") + _dot3(dwh, rk, "nt")
            da = jnp.where(strict, -_dot3(_dot3(t, dt, "tn"), t, "nt"), 0.0)
            dm = da * decay
            dpm = dqkh * decay
            dkb = _dot(dm, kh) + drk * eg
            dq = _dot(dpm, kh) + dqdh * eg
            dk = _dot(dm, kb, "tn") + _dot(dpm, qh, "tn") + dkdh * ekd + dkb * beta
            e = (da * m + dqkh * p) * decay
            sk = jnp.sum(dkdh * kh * ekd, axis=-1, keepdims=True)
            dgc = (jnp.sum(e, axis=-1, keepdims=True) - _dot3(e, ones, "tn")[:, 0:1]
                   + jnp.sum(dqdh * qh * eg, axis=-1, keepdims=True) - sk + jnp.sum(drk * rk, axis=-1, keepdims=True))
            dglast = jnp.sum(sk, axis=0, keepdims=True) + jnp.sum(dglv[h:h + 1, :], axis=-1, keepdims=True) * jnp.exp(gl)
            dgc = dgc + jnp.where(row == CHUNK - 1, dglast, 0.0)
            dg = _dot3(triu, dgc * ones)
            dbeta = jnp.sum(dkb * kh, axis=-1, keepdims=True) + jnp.sum(drv * vh, axis=-1, keepdims=True)
            dbg = dbg + jnp.where(lane == h, dbeta, 0.0) + jnp.where(lane == NHEAD + h, dg, 0.0)
            dq_ref[:, sl] = dq
            dk_ref[:, sl] = dk
            dv_ref[:, sl] = drv * beta
        dbg_ref[...] = dbg

    tok = pl.BlockSpec((CHUNK, w), lambda i: (i, 0))
    sm = pl.BlockSpec((CHUNK, BA_W), lambda i: (i, 0))
    sq = pl.BlockSpec((1, NHEAD, CHUNK, CHUNK), lambda i: (i, 0, 0, 0))
    return pl.pallas_call(
        body, name="gdn_intra_bwd", grid=(n,),
        in_specs=[tok, tok, tok, sm, sm, pl.BlockSpec((1, BA_W, CHUNK), lambda i: (i, 0, 0)), sq,
                  tok, tok, tok, tok, sq, pl.BlockSpec((1, NHEAD, HEAD), lambda i: (i, 0, 0))],
        out_specs=[tok] * 3 + [sm],
        out_shape=[jax.ShapeDtypeStruct((s, w), F32)] * 3 + [jax.ShapeDtypeStruct((s, BA_W), F32)],
        compiler_params=_cparams(("parallel",)),
    )(q, k, v, bg, gcol, gt3, tinv, du, dw, dqd, dkd, dqk, dgl)


def gdn_prep_bwd1(proj, conv_w, alog_pad, dt_pad, dq, dk, dv, dbg):
    s = proj.shape[0]
    tr = min(256, s)
    w = NHEAD * HEAD
    pad_w = BA_PAD

    def body(x_ref, halo_ref, ba_ref, w_ref, al_ref, dt_ref, dq_ref, dk_ref, dv_ref, dbg_ref,
             dc_ref, dba_ref, dw0_ref, dw1_ref, dw2_ref, dw3_ref, dal_ref, ddt_ref):
        i = pl.program_id(0)
        taps, c = _conv_silu_parts(x_ref[...], halo_ref[...], w_ref[...], i == 0)
        sv, dsv = _silu(c), _dsilu(c)
        for h in range(NHEAD):
            for base, d_ref, scale in ((0, dq_ref, HEAD ** -0.5), (w, dk_ref, 1.0)):
                sl = slice(base + h * HEAD, base + (h + 1) * HEAD)
                sh = sv[:, sl]
                dn = d_ref[:, h * HEAD:(h + 1) * HEAD]
                r = lax.rsqrt(jnp.sum(sh * sh, axis=-1, keepdims=True) + EPS)
                dsh = scale * (r * dn - sh * (r * r * r) * jnp.sum(dn * sh, axis=-1, keepdims=True))
                dc_ref[:, sl] = dsh * dsv[:, sl]
        dc_ref[:, 2 * w:] = dv_ref[...] * dsv[:, 2 * w:]
        dc = dc_ref[...]
        ba, dbgv = ba_ref[...], dbg_ref[...]
        lane = _iota2(ba.shape, 1)
        beta = _sigmoid(ba)
        ea = jnp.exp(al_ref[...])
        z = ba + dt_ref[...]
        g = -ea * _softplus(z)
        is_g = (lane >= NHEAD) & (lane < 2 * NHEAD)
        da_raw = jnp.where(is_g, dbgv * (-ea) * _sigmoid(z), 0.0)
        dba = jnp.where(lane < NHEAD, dbgv * beta * (1.0 - beta), da_raw)
        dba_ref[...] = jnp.concatenate([dba, jnp.zeros((tr, pad_w - BA_W), F32)], axis=1).astype(BF16)
        partial = [_colsum(dc * tp) for tp in taps] + [_colsum(jnp.where(is_g, dbgv * g, 0.0)), _colsum(da_raw)]
        red_refs = (dw0_ref, dw1_ref, dw2_ref, dw3_ref, dal_ref, ddt_ref)

        @pl.when(i == 0)
        def _():
            for r_, v_ in zip(red_refs, partial):
                r_[...] = v_

        @pl.when(i > 0)
        def _():
            for r_, v_ in zip(red_refs, partial):
                r_[...] += v_

    nb8 = tr // 8
    tok = pl.BlockSpec((tr, w), lambda i: (i, 0))
    one = lambda width: pl.BlockSpec((1, width), lambda i: (0, 0))
    return pl.pallas_call(
        body, name="gdn_prep_bwd1", grid=(s // tr,),
        in_specs=[pl.BlockSpec((tr, QKV_W), lambda i: (i, QKV_CB)),
                  pl.BlockSpec((8, QKV_W), lambda i: (jnp.maximum(i * nb8 - 1, 0), QKV_CB)),
                  pl.BlockSpec((tr, BA_W), lambda i: (i, BA_CB)),
                  pl.BlockSpec(conv_w.shape, lambda i: (0, 0)), one(BA_W), one(BA_W),
                  tok, tok, tok, pl.BlockSpec((tr, BA_W), lambda i: (i, 0))],
        out_specs=[pl.BlockSpec((tr, QKV_W), lambda i: (i, 0)), pl.BlockSpec((tr, pad_w), lambda i: (i, 0))]
        + [one(QKV_W)] * 4 + [one(BA_W)] * 2,
        out_shape=[jax.ShapeDtypeStruct((s, QKV_W), F32), jax.ShapeDtypeStruct((s, pad_w), BF16)]
        + [jax.ShapeDtypeStruct((1, QKV_W), F32)] * 4 + [jax.ShapeDtypeStruct((1, BA_W), F32)] * 2,
        compiler_params=_cparams(("arbitrary",)),
    )(proj, proj, proj, conv_w, alog_pad, dt_pad, dq, dk, dv, dbg)


def gdn_prep_bwd2(dc, conv_w):
    s = dc.shape[0]
    tr = min(256, s)
    nblk = s // tr
    nb8 = tr // 8

    def body(dc_ref, halo_ref, w_ref, o_ref):
        last = pl.program_id(0) == nblk - 1
        wv = w_ref[...]
        xs = jnp.concatenate([dc_ref[...], jnp.where(last, 0.0, halo_ref[...])], axis=0)
        acc = xs[:tr] * wv[3:4]
        for j in range(3):
            acc = acc + pltpu.roll(xs, tr + 8 - (3 - j), 0)[:tr] * wv[j:j + 1]
        o_ref[...] = acc.astype(BF16)

    return pl.pallas_call(
        body, name="gdn_prep_bwd2", grid=(nblk,),
        in_specs=[pl.BlockSpec((tr, QKV_W), lambda i: (i, 0)),
                  pl.BlockSpec((8, QKV_W), lambda i: (jnp.minimum((i + 1) * nb8, s // 8 - 1), 0)),
                  pl.BlockSpec(conv_w.shape, lambda i: (0, 0))],
        out_specs=pl.BlockSpec((tr, QKV_W), lambda i: (i, 0)),
        out_shape=jax.ShapeDtypeStruct((s, QKV_W), BF16),
        compiler_params=_cparams(("parallel",)),
    )(dc, dc, conv_w)


S5_W = S5_GROUPS * S5_STATE
S5_IN = S5_GROUPS * S5_GROUP
S5_TILES = 8
S5_TW, S5_TI = S5_W // S5_TILES, S5_IN // S5_TILES


def _s5_param_math(lr, li, ldt, br, bi):
    pr, pc = _iota2((S5_STATE, S5_STATE * S5_GROUP), 0), _iota2((S5_STATE, S5_STATE * S5_GROUP), 1)
    rep = jnp.where(pc // S5_GROUP == pr, 1.0, 0.0)
    dt = jnp.exp(ldt)
    mag = jnp.exp(lr * dt)
    ab_re, ab_im = mag * jnp.cos(li * dt), mag * jnp.sin(li * dt)
    den = lr * lr + li * li
    nr, ni = ab_re - 1.0, ab_im
    coef_re = (nr * lr + ni * li) / den
    coef_im = (ni * lr - nr * li) / den
    cr, ci = _dot(coef_re, rep, prec=HI), _dot(coef_im, rep, prec=HI)
    return ab_re, ab_im, cr * br - ci * bi, cr * bi + ci * br


def s5_param_fwd(lr, li, ldt, br, bi):
    def body(lr_ref, li_ref, ldt_ref, br_ref, bi_ref, ar_ref, ai_ref, bbr_ref, bbi_ref):
        res = _s5_param_math(lr_ref[...], li_ref[...], ldt_ref[...], br_ref[...], bi_ref[...])
        for r, v in zip((ar_ref, ai_ref, bbr_ref, bbi_ref), res):
            r[...] = v

    return pl.pallas_call(
        body, name="s5_param_fwd",
        out_shape=[jax.ShapeDtypeStruct(lr.shape, F32)] * 2 + [jax.ShapeDtypeStruct(br.shape, F32)] * 2,
        compiler_params=_cparams(),
    )(lr, li, ldt, br, bi)


def s5_param_bwd(lr, li, ldt, br, bi, dar, dai, dbbr, dbbi):
    def body(lr_ref, li_ref, ldt_ref, br_ref, bi_ref, dar_ref, dai_ref, dbbr_ref, dbbi_ref, *out_refs):
        _, vjp = jax.vjp(_s5_param_math, lr_ref[...], li_ref[...], ldt_ref[...], br_ref[...], bi_ref[...])
        for r, v in zip(out_refs, vjp((dar_ref[...], dai_ref[...], dbbr_ref[...], dbbi_ref[...]))):
            r[...] = v

    return pl.pallas_call(
        body, name="s5_param_bwd",
        out_shape=[jax.ShapeDtypeStruct(a.shape, F32) for a in (lr, li, ldt, br, bi)],
        compiler_params=_cparams(),
    )(lr, li, ldt, br, bi, dar, dai, dbbr, dbbi)


def _cmul(ar, ai, br, bi):
    return ar * br - ai * bi, ar * bi + ai * br


def _s5_power(ar, ai, steps):
    assert steps & (steps - 1) == 0
    for _ in range(steps.bit_length() - 1):
        ar, ai = _cmul(ar, ai, ar, ai)
    return ar, ai


def _s5_scan_rows(ar_ref, ai_ref, re_ref, im_ref, sr_ref, si_ref, tb, row0, reverse):
    quarter = S5_W // 4
    for qd in range(4):
        cs = slice(qd * quarter, (qd + 1) * quarter)
        are = jnp.broadcast_to(ar_ref[:, cs], (NSEG, quarter))
        aim = jnp.broadcast_to(ai_ref[:, cs], (NSEG, quarter))
        if reverse:
            aim = -aim

        def step(t, carry):
            h_r, h_i = carry
            tt = tb - 1 - t if reverse else t
            rows = pl.ds(pl.multiple_of(row0 + tt * NSEG, NSEG), NSEG)
            n_r = are * h_r - aim * h_i + re_ref[rows, cs]
            n_i = are * h_i + aim * h_r + im_ref[rows, cs]
            re_ref[rows, cs] = n_r
            im_ref[rows, cs] = n_i
            return n_r, n_i

        h_r, h_i = lax.fori_loop(0, tb, step, (sr_ref[:, cs], si_ref[:, cs]), unroll=4)
        sr_ref[:, cs] = h_r
        si_ref[:, cs] = h_i


def _s5_segment_carry(ar_ref, ai_ref, sr_ref, si_ref, steps, reverse):
    pr, pi = _s5_power(ar_ref[...], ai_ref[...], steps)
    if reverse:
        pi = -pi
    cur_r = jnp.zeros((1, S5_W), F32)
    cur_i = jnp.zeros((1, S5_W), F32)
    for s in (range(NSEG - 1, -1, -1) if reverse else range(NSEG)):
        e_r, e_i = sr_ref[s:s + 1, :], si_ref[s:s + 1, :]
        sr_ref[s:s + 1, :] = cur_r
        si_ref[s:s + 1, :] = cur_i
        nr, ni = _cmul(pr, pi, cur_r, cur_i)
        cur_r, cur_i = nr + e_r, ni + e_i


def _s5_blocks(s):
    steps = s // NSEG
    tb = min(32, steps)
    return steps, tb, NSEG * tb, steps // tb


def s5_scan_fwd(xp, a_re, a_im, bre, bim, cre, cim):
    s = xp.shape[0]
    steps, tb, rb, nb = _s5_blocks(s)

    def body(x_ref, ar_ref, ai_ref, bre_ref, bim_ref, cre_ref, cim_ref, y_ref, hsr_ref, hsi_ref,
             hr_ref, hi_ref, sr_ref, si_ref):
        ph, b = pl.program_id(0), pl.program_id(1)

        @pl.when((ph == 0) & (b == 0))
        def _():
            sr_ref[...] = jnp.zeros_like(sr_ref)
            si_ref[...] = jnp.zeros_like(si_ref)

        @pl.when((ph == 1) & (b == 0))
        def _():
            _s5_segment_carry(ar_ref, ai_ref, sr_ref, si_ref, steps, False)

        xv = x_ref[...].astype(BF16)
        for j in range(S5_TILES):
            xs = xv[:, j * S5_TI:(j + 1) * S5_TI]
            hr_ref[:, j * S5_TW:(j + 1) * S5_TW] = _dot(xs, bre_ref[j])
            hi_ref[:, j * S5_TW:(j + 1) * S5_TW] = _dot(xs, bim_ref[j])

        @pl.when(ph == 1)
        def _():
            hsr_ref[0] = sr_ref[...]
            hsi_ref[0] = si_ref[...]

        _s5_scan_rows(ar_ref, ai_ref, hr_ref, hi_ref, sr_ref, si_ref, tb, 0, False)

        @pl.when(ph == 1)
        def _():
            for j in range(S5_TILES):
                cs = slice(j * S5_TW, (j + 1) * S5_TW)
                y_ref[:, j * S5_TI:(j + 1) * S5_TI] = _dot(hr_ref[:, cs], cre_ref[j]) - _dot(hi_ref[:, cs], cim_ref[j])

    row = pl.BlockSpec((1, S5_W), lambda p, b: (0, 0))
    wb = pl.BlockSpec((S5_TILES, S5_TI, S5_TW), lambda p, b: (0, 0, 0))
    wc = pl.BlockSpec((S5_TILES, S5_TW, S5_TI), lambda p, b: (0, 0, 0))
    st = pl.BlockSpec((1, NSEG, S5_W), lambda p, b: (p * b, 0, 0))
    return pl.pallas_call(
        body, name="s5_scan_fwd", grid=(2, nb),
        in_specs=[pl.BlockSpec((rb, S5_IN), lambda p, b: (b, 0)), row, row, wb, wb, wc, wc],
        out_specs=[pl.BlockSpec((rb, S5_IN), lambda p, b: (p * b, 0)), st, st],
        out_shape=[jax.ShapeDtypeStruct((s, S5_IN), F32)] + [jax.ShapeDtypeStruct((nb, NSEG, S5_W), F32)] * 2,
        scratch_shapes=[pltpu.VMEM((rb, S5_W), F32)] * 2 + [pltpu.VMEM((NSEG, S5_W), F32)] * 2,
        compiler_params=_cparams(("arbitrary", "arbitrary")),
    )(xp, a_re, a_im, bre, bim, cre, cim)


def s5_scan_bwd(dyp, xp, a_re, a_im, bre, bim, cre_t, cim_t, hs_r, hs_i):
    s = xp.shape[0]
    steps, tb, rb, nb = _s5_blocks(s)

    def body(dy_ref, x_ref, ar_ref, ai_ref, bre_ref, bim_ref, crt_ref, cit_ref, hsr_ref, hsi_ref,
             dx_ref, dar_ref, dai_ref, dbr_ref, dbi_ref, dcr_ref, dci_ref,
             hr_ref, hi_ref, lr_ref, li_ref, sr_ref, si_ref, fr_ref, fi_ref, accr_ref, acci_ref):
        ph, b = pl.program_id(0), pl.program_id(1)

        @pl.when((ph == 0) & (b == 0))
        def _():
            sr_ref[...] = jnp.zeros_like(sr_ref)
            si_ref[...] = jnp.zeros_like(si_ref)

        @pl.when((ph == 1) & (b == 0))
        def _():
            _s5_segment_carry(ar_ref, ai_ref, sr_ref, si_ref, steps, True)
            for r in (accr_ref, acci_ref, dbr_ref, dbi_ref, dcr_ref, dci_ref):
                r[...] = jnp.zeros_like(r)

        dyv = dy_ref[...].astype(BF16)
        for j in range(S5_TILES):
            ds_ = dyv[:, j * S5_TI:(j + 1) * S5_TI]
            lr_ref[:, j * S5_TW:(j + 1) * S5_TW] = _dot(ds_, crt_ref[j])
            li_ref[:, j * S5_TW:(j + 1) * S5_TW] = -_dot(ds_, cit_ref[j])
        _s5_scan_rows(ar_ref, ai_ref, lr_ref, li_ref, sr_ref, si_ref, tb, 0, True)

        @pl.when(ph == 1)
        def _():
            xv = x_ref[...].astype(BF16)
            for j in range(S5_TILES):
                xs = xv[:, j * S5_TI:(j + 1) * S5_TI]
                hr_ref[NSEG:, j * S5_TW:(j + 1) * S5_TW] = _dot(xs, bre_ref[j])
                hi_ref[NSEG:, j * S5_TW:(j + 1) * S5_TW] = _dot(xs, bim_ref[j])
            hr_ref[0:NSEG, :] = hsr_ref[0]
            hi_ref[0:NSEG, :] = hsi_ref[0]
            fr_ref[...] = hsr_ref[0]
            fi_ref[...] = hsi_ref[0]
            _s5_scan_rows(ar_ref, ai_ref, hr_ref, hi_ref, fr_ref, fi_ref, tb, NSEG, False)
            lam_r, lam_i = lr_ref[...], li_ref[...]
            hp_r, hp_i = hr_ref[0:rb, :], hi_ref[0:rb, :]
            accr_ref[...] += jnp.sum((lam_r * hp_r + lam_i * hp_i).reshape(tb, NSEG, S5_W), axis=0)
            acci_ref[...] += jnp.sum((lam_i * hp_r - lam_r * hp_i).reshape(tb, NSEG, S5_W), axis=0)
            lam_rb, lam_ib = lam_r.astype(BF16), lam_i.astype(BF16)
            h_rb, h_ib = hr_ref[NSEG:, :].astype(BF16), hi_ref[NSEG:, :].astype(BF16)
            for j in range(S5_TILES):
                cs, ci = slice(j * S5_TW, (j + 1) * S5_TW), slice(j * S5_TI, (j + 1) * S5_TI)
                dbr_ref[j] += _dot(xv[:, ci], lam_rb[:, cs], "tn")
                dbi_ref[j] += _dot(xv[:, ci], lam_ib[:, cs], "tn")
                dx_ref[:, ci] = _dot(lam_rb[:, cs], bre_ref[j], "nt") + _dot(lam_ib[:, cs], bim_ref[j], "nt")
                dcr_ref[j] += _dot(h_rb[:, cs], dyv[:, ci], "tn")
                dci_ref[j] -= _dot(h_ib[:, cs], dyv[:, ci], "tn")

        @pl.when((ph == 1) & (b == nb - 1))
        def _():
            dar_ref[...] = jnp.sum(accr_ref[...], axis=0, keepdims=True)
            dai_ref[...] = jnp.sum(acci_ref[...], axis=0, keepdims=True)

    rev = lambda p, b: (nb - 1 - b, 0)
    row = pl.BlockSpec((1, S5_W), lambda p, b: (0, 0))
    wb = pl.BlockSpec((S5_TILES, S5_TI, S5_TW), lambda p, b: (0, 0, 0))
    wc = pl.BlockSpec((S5_TILES, S5_TW, S5_TI), lambda p, b: (0, 0, 0))
    st = pl.BlockSpec((1, NSEG, S5_W), lambda p, b: (nb - 1 - b, 0, 0))
    big = pltpu.VMEM((rb, S5_W), F32)
    big8 = pltpu.VMEM((rb + NSEG, S5_W), F32)
    small = pltpu.VMEM((NSEG, S5_W), F32)
    return pl.pallas_call(
        body, name="s5_scan_bwd", grid=(2, nb),
        in_specs=[pl.BlockSpec((rb, S5_IN), rev), pl.BlockSpec((rb, S5_IN), rev), row, row, wb, wb, wb, wb, st, st],
        out_specs=[pl.BlockSpec((rb, S5_IN), lambda p, b: (nb - 1 - p * b, 0)), row, row, wb, wb, wc, wc],
        out_shape=[jax.ShapeDtypeStruct((s, S5_IN), F32)] + [jax.ShapeDtypeStruct((1, S5_W), F32)] * 2
        + [jax.ShapeDtypeStruct((S5_TILES, S5_TI, S5_TW), F32)] * 2 + [jax.ShapeDtypeStruct((S5_TILES, S5_TW, S5_TI), F32)] * 2,
        scratch_shapes=[big8, big8, big, big, small, small, small, small, small, small],
        compiler_params=_cparams(("arbitrary", "arbitrary")),
    )(dyp, xp, a_re, a_im, bre, bim, cre_t, cim_t, hs_r, hs_i)


XA_DIM = 256
XA_W = XA_HEADS * XA_DIM


def _xa_probs(qh, kh):
    sc = _dot(qh, kh, "nt") * (XA_DIM ** -0.5)
    ex = jnp.exp(sc - jnp.max(sc, axis=-1, keepdims=True))
    return ex / jnp.sum(ex, axis=-1, keepdims=True)


def xa_fwd(proj, kv):
    s = proj.shape[0]
    tq = min(512, s)

    def body(q_ref, kv_ref, o_ref):
        for h in range(XA_HEADS):
            sl = slice(h * XA_DIM, (h + 1) * XA_DIM)
            p = _xa_probs(q_ref[:, sl], kv_ref[:, sl])
            o_ref[:, sl] = _dot(p, kv_ref[:, XA_W + h * XA_DIM:XA_W + (h + 1) * XA_DIM])

    return pl.pallas_call(
        body, name="xa_fwd", grid=(s // tq,),
        in_specs=[pl.BlockSpec((tq, XA_W), lambda i: (i, QC_CB)), pl.BlockSpec(kv.shape, lambda i: (0, 0))],
        out_specs=pl.BlockSpec((tq, XA_W), lambda i: (i, 0)),
        out_shape=jax.ShapeDtypeStruct((s, XA_W), F32),
        compiler_params=_cparams(("parallel",)),
    )(proj, kv)


def xa_bwd(do, proj, kv):
    s = proj.shape[0]
    tq = min(512, s)

    def body(do_ref, q_ref, kv_ref, dq_ref, dkv_ref):
        @pl.when(pl.program_id(0) == 0)
        def _():
            dkv_ref[...] = jnp.zeros_like(dkv_ref)

        for h in range(XA_HEADS):
            sl = slice(h * XA_DIM, (h + 1) * XA_DIM)
            sv = slice(XA_W + h * XA_DIM, XA_W + (h + 1) * XA_DIM)
            qh, kh, vh, doh = q_ref[:, sl], kv_ref[:, sl], kv_ref[:, sv], do_ref[:, sl]
            p = _xa_probs(qh, kh)
            dp = _dot(doh, vh, "nt")
            ds_ = p * (dp - jnp.sum(dp * p, axis=-1, keepdims=True)) * (XA_DIM ** -0.5)
            dq_ref[:, sl] = _dot(ds_, kh).astype(BF16)
            dkv_ref[:, sl] += _dot(ds_, qh, "tn")
            dkv_ref[:, sv] += _dot(p, doh, "tn")

    return pl.pallas_call(
        body, name="xa_bwd", grid=(s // tq,),
        in_specs=[pl.BlockSpec((tq, XA_W), lambda i: (i, 0)), pl.BlockSpec((tq, XA_W), lambda i: (i, QC_CB)),
                  pl.BlockSpec(kv.shape, lambda i: (0, 0))],
        out_specs=[pl.BlockSpec((tq, XA_W), lambda i: (i, 0)), pl.BlockSpec(kv.shape, lambda i: (0, 0))],
        out_shape=[jax.ShapeDtypeStruct((s, XA_W), BF16), jax.ShapeDtypeStruct(kv.shape, F32)],
        compiler_params=_cparams(("arbitrary",)),
    )(do, proj, kv)


def adamw(w, g, m, v, name):
    lead = (0,) * (w.ndim - 2)
    rows, cols = w.shape[-2:]
    tr = rows
    while tr * cols * 4 * 7 * 2 > 36 * 2 ** 20 and tr % 16 == 0:
        tr //= 2

    def body(w_ref, g_ref, m_ref, v_ref, d_ref, m2_ref, v2_ref):
        gv = g_ref[...]
        m2 = ADAM_B1 * m_ref[...] + (1.0 - ADAM_B1) * gv
        v2 = ADAM_B2 * v_ref[...] + (1.0 - ADAM_B2) * (gv * gv)
        m_hat = m2 / (1.0 - ADAM_B1 ** ADAM_STEP)
        v_hat = v2 / (1.0 - ADAM_B2 ** ADAM_STEP)
        d_ref[...] = -ADAM_LR * (m_hat / (jnp.sqrt(v_hat) + ADAM_EPS) + ADAM_WD * w_ref[...])
        m2_ref[...] = m2
        v2_ref[...] = v2

    spec = pl.BlockSpec((1,) * len(lead) + (tr, cols), lambda i: lead + (i, 0))
    return pl.pallas_call(
        body, name=name, grid=(rows // tr,), in_specs=[spec] * 4, out_specs=[spec] * 3,
        out_shape=[jax.ShapeDtypeStruct(w.shape, F32)] * 3, compiler_params=_cparams(("parallel",)),
    )(w, g, m, v)


def _seg_perm(a):
    s, w = a.shape
    return a.reshape(NSEG, s // NSEG, w).transpose(1, 0, 2).reshape(s, w)


def _seg_unperm(a):
    s, w = a.shape
    return a.reshape(s // NSEG, NSEG, w).transpose(1, 0, 2).reshape(s, w)


def _block_diag(t):
    nt, _, r, c = t.shape
    eye = jnp.eye(8, dtype=bool)
    return jnp.where(eye[None, :, None, :, None], t[:, :, :, None, :], 0.0).reshape(nt, 8 * r, 8 * c)


def _block_diag_inv(d, r, c):
    d5 = d.reshape(d.shape[0], 8, r, 8, c)
    return jnp.diagonal(d5, axis1=1, axis2=3).transpose(0, 3, 1, 2)


def _s5_b_tiles(bb):
    return _block_diag(bb.reshape(S5_TILES, 8, S5_STATE, S5_GROUP).transpose(0, 1, 3, 2))


def _s5_b_untile(d):
    return _block_diag_inv(d, S5_GROUP, S5_STATE).transpose(0, 1, 3, 2).reshape(S5_GROUPS, S5_STATE * S5_GROUP)


def _s5_c_tiles(c):
    return _block_diag(c.reshape(S5_TILES, 8, S5_GROUP, S5_STATE).transpose(0, 1, 3, 2))


def _s5_c_untile(d):
    return _block_diag_inv(d, S5_STATE, S5_GROUP).transpose(0, 1, 3, 2).reshape(S5_GROUPS, S5_GROUP, S5_STATE)


def s5_ssm_fwd(xb, lam_re, lam_im, log_dt, b_re, b_im, c_re, c_im):
    br, bi = b_re.reshape(S5_GROUPS, -1), b_im.reshape(S5_GROUPS, -1)
    ldt = log_dt.reshape(S5_GROUPS, 1)
    ab_re, ab_im, bb_re, bb_im = s5_param_fwd(lam_re, lam_im, ldt, br, bi)
    a_re, a_im = ab_re.reshape(1, S5_W), ab_im.reshape(1, S5_W)
    bre, bim = _s5_b_tiles(bb_re).astype(BF16), _s5_b_tiles(bb_im).astype(BF16)
    cre, cim = _s5_c_tiles(c_re).astype(BF16), _s5_c_tiles(c_im).astype(BF16)
    xp = _seg_perm(xb)
    yp, hs_r, hs_i = s5_scan_fwd(xp, a_re, a_im, bre, bim, cre, cim)
    saved = (xp, a_re, a_im, bre, bim, cre, cim, hs_r, hs_i, (lam_re, lam_im, ldt, br, bi))
    return _seg_unperm(yp), saved


def s5_ssm_bwd(dy, saved):
    xp, a_re, a_im, bre, bim, cre, cim, hs_r, hs_i, params = saved
    cre_t, cim_t = cre.transpose(0, 2, 1), cim.transpose(0, 2, 1)
    dxp, dar, dai, dbr, dbi, dcr, dci = s5_scan_bwd(_seg_perm(dy), xp, a_re, a_im, bre, bim, cre_t, cim_t, hs_r, hs_i)
    dlr, dli, dldt, db_re, db_im = s5_param_bwd(*params, dar.reshape(S5_GROUPS, S5_STATE), dai.reshape(S5_GROUPS, S5_STATE),
                                                _s5_b_untile(dbr), _s5_b_untile(dbi))
    shape_b = (S5_GROUPS, S5_STATE, S5_GROUP)
    return (_seg_unperm(dxp), dlr, dli, dldt.reshape(S5_GROUPS), db_re.reshape(shape_b), db_im.reshape(shape_b),
            _s5_c_untile(dcr), _s5_c_untile(dci))


_MESH = pl.DeviceIdType.MESH
_HBM = pl.BlockSpec(memory_space=pltpu.HBM)
N_DEV = 8


def _position():
    return lax.axis_index("x"), lax.axis_index("y"), lax.axis_index("c")


def _half(core, rows):
    h = rows // 2
    return pl.ds(pl.multiple_of(core * h, 16), h)


D2D_CHUNK_BYTES = 2 ** 20


def _chunk_rows(rows, cols, itemsize):
    return _row_tile(rows, 16, max(16, D2D_CHUNK_BYTES // (cols * itemsize)))


def _rows(start, size, unit=16):
    return pl.ds(pl.multiple_of(start, unit), size)


def _push_to_sibling(chunks, stages, recv_sems, store_sems, sibling, lag=2):
    in_slot, used, stores = {}, {}, []

    def push(q, slot):
        _, _, sid, land, _ = chunks[q]
        buf, send_sems, _ = stages[sid]
        return pltpu.make_async_remote_copy(src_ref=buf.at[slot], dst_ref=land, send_sem=send_sems.at[slot],
                                            recv_sem=recv_sems.at[q], device_id=sibling, device_id_type=_MESH)

    def receive(q):
        push(q, 0).wait_recv()
        st = pltpu.make_async_copy(chunks[q][3], chunks[q][4], store_sems.at[q])
        st.start()
        stores.append(st)

    for q, (pre, src, sid, _, _) in enumerate(chunks):
        if pre is not None:
            pre()
        slot = used.get(sid, 0) % 2
        used[sid] = used.get(sid, 0) + 1
        if (sid, slot) in in_slot:
            in_slot.pop((sid, slot)).wait_send()
        load = pltpu.make_async_copy(src, stages[sid][0].at[slot], stages[sid][2].at[slot])
        load.start()
        load.wait()
        cp = push(q, slot)
        cp.start()
        in_slot[(sid, slot)] = cp
        if q >= lag:
            receive(q - lag)
    for q in range(max(0, len(chunks) - lag), len(chunks)):
        receive(q)
    for cp in in_slot.values():
        cp.wait_send()
    for st in stores:
        st.wait()


def _stage_scratch(shapes_dtypes):
    out = []
    for shape, dtype in shapes_dtypes:
        out += [pltpu.VMEM((2,) + shape, dtype), pltpu.SemaphoreType.DMA((2,)), pltpu.SemaphoreType.DMA((2,))]
    return out


def allgather_weights(ws, convw, name):
    n = len(ws)
    extra = 0 if convw is None else 1
    halves = [w.shape[0] // 2 for w in ws]
    steps = [_chunk_rows(h, w.shape[1], w.dtype.itemsize) for h, w in zip(halves, ws)]
    per_peer = [h // s for h, s in zip(halves, steps)]
    nchunks = 3 * sum(per_peer)

    def body(*refs):
        w_refs = refs[:n]
        wo_refs = refs[n + extra:2 * n + extra]
        scratch = refs[2 * (n + extra):]
        send_sems, recv_sems, local_sems, fwd_recv_sems, store_sems = scratch[:5]
        lands = scratch[5:5 + n]
        stage_refs = scratch[5 + n:]
        stages = [tuple(stage_refs[3 * i:3 * i + 3]) for i in range(n)]
        x, y, c = _position()
        mine = 2 * x + y
        peers = [(1 - x, y), (x, 1 - y), (1 - x, 1 - y)]
        blocks = [2 * px + py for px, py in peers]
        local = [pltpu.make_async_copy(w_refs[i], wo_refs[i].at[mine], local_sems.at[i]) for i in range(n)]
        if extra:
            c_ref, co_ref = refs[n], refs[2 * n + 1]
            local.append(pltpu.make_async_copy(c_ref, co_ref.at[mine], local_sems.at[n]))
        for cp in local:
            cp.start()

        def ici(i, k, block):
            rows = _rows(c * halves[i], halves[i])
            return pltpu.make_async_remote_copy(src_ref=w_refs[i].at[rows, :], dst_ref=wo_refs[i].at[block, rows, :],
                                                send_sem=send_sems.at[3 * i + k], recv_sem=recv_sems.at[3 * i + k],
                                                device_id=(*peers[k], c), device_id_type=_MESH)

        def conv(k, block):
            return pltpu.make_async_remote_copy(src_ref=c_ref, dst_ref=co_ref.at[block], send_sem=send_sems.at[3 * n + k],
                                                recv_sem=recv_sems.at[3 * n + k], device_id=(*peers[k], c), device_id_type=_MESH)

        sends = [ici(i, k, mine) for k in range(3) for i in range(n)] + ([conv(k, mine) for k in range(3)] if extra else [])
        for cp in sends:
            cp.start()
        chunks = []
        for k in range(3):
            for i in range(n):
                for q in range(per_peer[i]):
                    pre = functools.partial(lambda i, k: ici(i, k, blocks[k]).wait_recv(), i, k) if q == 0 else None
                    src = wo_refs[i].at[blocks[k], _rows(c * halves[i] + q * steps[i], steps[i]), :]
                    out = wo_refs[i].at[blocks[k], _rows((1 - c) * halves[i] + q * steps[i], steps[i]), :]
                    chunks.append((pre, src, i, lands[i].at[k * per_peer[i] + q], out))
        _push_to_sibling(chunks, stages, fwd_recv_sems, store_sems, (x, y, 1 - c))
        if extra:
            for k in range(3):
                conv(k, blocks[k]).wait_recv()
        for cp in sends:
            cp.wait_send()
        for cp in local:
            cp.wait()

    nsem = 3 * (n + extra)
    scratch = [pltpu.SemaphoreType.DMA((nsem,)), pltpu.SemaphoreType.DMA((nsem,)), pltpu.SemaphoreType.DMA((n + extra,)),
               pltpu.SemaphoreType.DMA((nchunks,)), pltpu.SemaphoreType.DMA((nchunks,))]
    scratch += [pltpu.VMEM((3 * p, s, w.shape[1]), w.dtype) for p, s, w in zip(per_peer, steps, ws)]
    scratch += _stage_scratch([((s, w.shape[1]), w.dtype) for s, w in zip(steps, ws)])
    operands = list(ws) + ([convw] if extra else [])
    return pl.pallas_call(
        body, name=name, in_specs=[_HBM] * len(operands), out_specs=[_HBM] * len(operands),
        out_shape=[jax.ShapeDtypeStruct((4,) + w.shape, w.dtype) for w in operands],
        scratch_shapes=scratch, compiler_params=pltpu.CompilerParams(vmem_limit_bytes=VMEM_LIMIT),
    )(*operands)


def exchange_cores(gs, small, name):
    n = len(gs)
    extra = 0 if small is None else 1
    halves = [g.shape[1] // 2 for g in gs]
    steps = [_chunk_rows(h, g.shape[2], g.dtype.itemsize) for h, g in zip(halves, gs)]
    per_shard = [h // s for h, s in zip(halves, steps)]
    nchunks = 4 * sum(per_shard)

    def body(*refs):
        g_refs = refs[:n]
        got_refs = refs[n + extra:2 * n + extra]
        scratch = refs[2 * (n + extra):]
        recv_sems, store_sems = scratch[:2]
        lands = scratch[2:2 + n]
        stage_refs = scratch[2 + n:2 + 4 * n]
        stages = [tuple(stage_refs[3 * i:3 * i + 3]) for i in range(n)]
        x, y, c = _position()
        if extra:
            s_ref, so_ref = refs[n], refs[2 * n + 1]
            tiny_send, tiny_recv, tiny_local = scratch[2 + 4 * n:]
            me = 4 * x + 2 * y + c
            local = pltpu.make_async_copy(s_ref, so_ref.at[me], tiny_local)
            local.start()

            def tiny(r, sending):
                px, py, pc = (1 - x if r & 4 else x, 1 - y if r & 2 else y, 1 - c if r & 1 else c)
                slot = me if sending else 4 * px + 2 * py + pc
                return pltpu.make_async_remote_copy(src_ref=s_ref, dst_ref=so_ref.at[slot], send_sem=tiny_send.at[r - 1],
                                                    recv_sem=tiny_recv.at[r - 1], device_id=(px, py, pc), device_id_type=_MESH)

            sends = [tiny(r, True) for r in range(1, N_DEV)]
            for cp in sends:
                cp.start()
        chunks = []
        for i in range(n):
            for j in range(4):
                for q in range(per_shard[i]):
                    src = g_refs[i].at[j, _rows((1 - c) * halves[i] + q * steps[i], steps[i]), :]
                    out = got_refs[i].at[j, pl.ds(q * steps[i], steps[i]), :]
                    chunks.append((None, src, i, lands[i].at[j * per_shard[i] + q], out))
        _push_to_sibling(chunks, stages, recv_sems, store_sems, (x, y, 1 - c))
        if extra:
            for r in range(1, N_DEV):
                tiny(r, False).wait_recv()
            for cp in sends:
                cp.wait_send()
            local.wait()

    scratch = [pltpu.SemaphoreType.DMA((nchunks,)), pltpu.SemaphoreType.DMA((nchunks,))]
    scratch += [pltpu.VMEM((4 * p, s, g.shape[2]), g.dtype) for p, s, g in zip(per_shard, steps, gs)]
    scratch += _stage_scratch([((s, g.shape[2]), g.dtype) for s, g in zip(steps, gs)])
    out_shape = [jax.ShapeDtypeStruct((4, h, g.shape[2]), g.dtype) for h, g in zip(halves, gs)]
    if extra:
        scratch += [pltpu.SemaphoreType.DMA((N_DEV - 1,)), pltpu.SemaphoreType.DMA((N_DEV - 1,)), pltpu.SemaphoreType.DMA]
        out_shape.append(jax.ShapeDtypeStruct((N_DEV,) + small.shape, small.dtype))
    operands = list(gs) + ([small] if extra else [])
    return pl.pallas_call(
        body, name=name, in_specs=[_HBM] * len(operands), out_specs=[_HBM] * len(out_shape), out_shape=out_shape,
        scratch_shapes=scratch, compiler_params=pltpu.CompilerParams(vmem_limit_bytes=VMEM_LIMIT),
    )(*operands)


def exchange_chips(cs):
    n = len(cs)

    def body(*refs):
        c_refs, o_refs = refs[:n], refs[n:2 * n]
        send_sems, recv_sems, local_sems = refs[2 * n:]
        x, y, c = _position()
        mine = 2 * x + y
        peers = [(1 - x, y), (x, 1 - y), (1 - x, 1 - y)]
        blocks = [2 * px + py for px, py in peers]
        local = [pltpu.make_async_copy(c_refs[i].at[mine], o_refs[i].at[mine], local_sems.at[i]) for i in range(n)]
        for cp in local:
            cp.start()

        def copy(i, k, sending):
            sem = 3 * i + k
            return pltpu.make_async_remote_copy(src_ref=c_refs[i].at[blocks[k]], dst_ref=o_refs[i].at[mine if sending else blocks[k]],
                                                send_sem=send_sems.at[sem], recv_sem=recv_sems.at[sem],
                                                device_id=(*peers[k], c), device_id_type=_MESH)

        sends = [copy(i, k, True) for k in range(3) for i in range(n)]
        for cp in sends:
            cp.start()
        for k in range(3):
            for i in range(n):
                copy(i, k, False).wait_recv()
        for cp in sends:
            cp.wait_send()
        for cp in local:
            cp.wait()

    return pl.pallas_call(
        body, name="exchange_chips", in_specs=[_HBM] * n, out_specs=[_HBM] * n,
        out_shape=[jax.ShapeDtypeStruct(a.shape, a.dtype) for a in cs],
        scratch_shapes=[pltpu.SemaphoreType.DMA((3 * n,)), pltpu.SemaphoreType.DMA((3 * n,)), pltpu.SemaphoreType.DMA((n,))],
    )(*cs)


def pair_sum(core, g, got, name):
    nb, rows, cols = got.shape
    tr = _row_tile(rows, 16, max(16, (2 * 2 ** 20) // (cols * g.dtype.itemsize)))
    nblk = rows // tr

    def body(c_ref, a_ref, b_ref, o_ref):
        o_ref[...] = (a_ref[...].astype(F32) + b_ref[...].astype(F32)).astype(o_ref.dtype)

    spec = pl.BlockSpec((1, tr, cols), lambda j, i, c_ref: (j, i, 0))
    mine = pl.BlockSpec((1, tr, cols), lambda j, i, c_ref: (j, c_ref[0] * nblk + i, 0))
    return pl.pallas_call(
        body, name=name,
        grid_spec=pltpu.PrefetchScalarGridSpec(num_scalar_prefetch=1, grid=(nb, nblk), in_specs=[mine, spec], out_specs=spec),
        out_shape=jax.ShapeDtypeStruct(got.shape, g.dtype), compiler_params=_cparams(("parallel", "parallel")),
    )(core, g, got)


def sum_chips(core, pieces, name):
    nb, rows, cols = pieces.shape
    tr = _row_tile(rows, 16, max(16, (6 * 2 ** 20) // (nb * cols * pieces.dtype.itemsize)))
    nblk = rows // tr

    def body(c_ref, p_ref, o_ref):
        acc = p_ref[0].astype(F32)
        for i in range(1, nb):
            acc = acc + p_ref[i].astype(F32)
        o_ref[0] = acc

    return pl.pallas_call(
        body, name=name,
        grid_spec=pltpu.PrefetchScalarGridSpec(
            num_scalar_prefetch=1, grid=(nblk,),
            in_specs=[pl.BlockSpec((nb, tr, cols), lambda i, c_ref: (0, i, 0))],
            out_specs=pl.BlockSpec((1, tr, cols), lambda i, c_ref: (0, c_ref[0] * nblk + i, 0))),
        out_shape=jax.ShapeDtypeStruct((1, 2 * rows, cols), F32), compiler_params=_cparams(("parallel",)),
    )(core, pieces)


def sibling_exchange(fulls):
    n = len(fulls)
    halves = [f.shape[1] // 2 for f in fulls]
    steps = [_chunk_rows(h, f.shape[2], f.dtype.itemsize) for h, f in zip(halves, fulls)]
    counts = [h // s for h, s in zip(halves, steps)]
    nchunks = sum(counts)

    def body(*refs):
        f_refs, o_refs = refs[:n], refs[n:2 * n]
        scratch = refs[2 * n:]
        recv_sems, store_sems = scratch[:2]
        lands = scratch[2:2 + n]
        stages = [tuple(scratch[2 + n + 3 * i:2 + n + 3 * i + 3]) for i in range(n)]
        x, y, c = _position()
        chunks = []
        for i in range(n):
            for q in range(counts[i]):
                src = f_refs[i].at[0, _rows(c * halves[i] + q * steps[i], steps[i]), :]
                out = o_refs[i].at[0, _rows((1 - c) * halves[i] + q * steps[i], steps[i]), :]
                chunks.append((None, src, i, lands[i].at[q], out))
        _push_to_sibling(chunks, stages, recv_sems, store_sems, (x, y, 1 - c))

    scratch = [pltpu.SemaphoreType.DMA((nchunks,)), pltpu.SemaphoreType.DMA((nchunks,))]
    scratch += [pltpu.VMEM((k, s, f.shape[2]), f.dtype) for k, s, f in zip(counts, steps, fulls)]
    scratch += _stage_scratch([((s, f.shape[2]), f.dtype) for s, f in zip(steps, fulls)])
    return pl.pallas_call(
        body, name="sibling_exchange", in_specs=[_HBM] * n, out_specs=[_HBM] * n,
        out_shape=[jax.ShapeDtypeStruct(f.shape, f.dtype) for f in fulls],
        input_output_aliases={i: i for i in range(n)},
        scratch_shapes=scratch, compiler_params=pltpu.CompilerParams(vmem_limit_bytes=VMEM_LIMIT),
    )(*fulls)


def _row_tile(rows, unit, max_rows):
    best = unit
    for t in range(unit, min(rows, max_rows) + 1, unit):
        if rows % t == 0:
            best = t
    return best


def sum_pieces(pieces, name):
    n, rows, cols = pieces.shape
    tr = _row_tile(rows, 16, max(16, (6 * 2 ** 20) // (n * cols * pieces.dtype.itemsize)))

    def body(p_ref, o_ref):
        acc = p_ref[0].astype(F32)
        for i in range(1, n):
            acc = acc + p_ref[i].astype(F32)
        o_ref[...] = acc

    return pl.pallas_call(
        body, name=name, grid=(rows // tr,),
        in_specs=[pl.BlockSpec((n, tr, cols), lambda i: (0, i, 0))], out_specs=pl.BlockSpec((tr, cols), lambda i: (i, 0)),
        out_shape=jax.ShapeDtypeStruct((rows, cols), F32), compiler_params=_cparams(("parallel",)),
    )(pieces)


BIG = ("w_in", "s5_w_glu", "w_kv_mem", "w_br_a", "w_br_b", "w_br_c", "w_out")
COL_SHARDED = ("w_in", "s5_w_glu", "w_br_a", "w_br_b", "w_br_c")
SMALL = ("norm_g", "gdn_a_log", "gdn_dt_bias", "gdn_norm_g", "s5_lambda_re", "s5_lambda_im", "s5_log_dt",
         "s5_b_re", "s5_b_im", "s5_c_re", "s5_c_im", "s5_d", "mem_norm_g", "final_g")
WEIGHTS = ("norm_g", "w_in", "conv_w", "gdn_a_log", "gdn_dt_bias", "gdn_norm_g", "s5_lambda_re", "s5_lambda_im",
           "s5_log_dt", "s5_b_re", "s5_b_im", "s5_c_re", "s5_c_im", "s5_d", "s5_w_glu", "mem_norm_g", "w_kv_mem",
           "w_br_a", "w_br_b", "w_br_c", "w_out", "final_g")
W_IN_SPLIT = 4096


W_IN_COLS = PROJ_W - BA_PAD + 2 * NHEAD


def _pack_w_in(shards):
    cs = shards.shape[2]
    parts = []
    for a, b in ((0, W_IN_SPLIT), (W_IN_SPLIT + 2 * NHEAD, W_IN_COLS), (W_IN_SPLIT, W_IN_SPLIT + 2 * NHEAD)):
        while a < b:
            j = a // cs
            hi = min(b, (j + 1) * cs)
            parts.append(shards[j, :, a - j * cs:hi - j * cs])
            a = hi
    parts.append(jnp.zeros((shards.shape[1], BA_PAD - 2 * NHEAD), shards.dtype))
    return jnp.concatenate(parts, axis=1)


def _unpack_w_in(wp):
    cs = W_IN_COLS // 4
    moves = ((0, W_IN_SPLIT, 0), (W_IN_SPLIT, W_IN_SPLIT + 2 * NHEAD, PROJ_W - BA_PAD - W_IN_SPLIT),
             (W_IN_SPLIT + 2 * NHEAD, W_IN_COLS, -2 * NHEAD))
    shards = []
    for j in range(4):
        parts = []
        for lo, hi, shift in moves:
            s, e = max(j * cs, lo), min((j + 1) * cs, hi)
            if s < e:
                parts.append(wp[:, s + shift:e + shift])
        shards.append(jnp.concatenate(parts, axis=1))
    return jnp.stack(shards)


def _pack_small(arrs):
    parts = []
    for a in arrs:
        f = a.reshape(-1).astype(F32)
        parts.append(jnp.pad(f, (0, (-f.shape[0]) % 128)))
    flat = jnp.concatenate(parts)
    rows = flat.shape[0] // 128
    return jnp.pad(flat.reshape(rows, 128), ((0, (-rows) % 16), (0, 0)))


def _unpack_small(flat2d, shapes):
    f = flat2d.reshape(-1)
    out, off = [], 0
    for shp in shapes:
        n = math.prod(shp)
        out.append(f[off:off + n].reshape(shp))
        off += n + (-n) % 128
    return out


def kernel(x, mem, norm_g, w_in, conv_w, gdn_a_log, gdn_dt_bias, gdn_norm_g, s5_lambda_re, s5_lambda_im, s5_log_dt, s5_b_re, s5_b_im, s5_c_re, s5_c_im, s5_d, s5_w_glu, mem_norm_g, w_kv_mem, w_br_a, w_br_b, w_br_c, w_out, final_g, loss_target, m_norm_g, m_w_in, m_conv_w, m_gdn_a_log, m_gdn_dt_bias, m_gdn_norm_g, m_s5_lambda_re, m_s5_lambda_im, m_s5_log_dt, m_s5_b_re, m_s5_b_im, m_s5_c_re, m_s5_c_im, m_s5_d, m_s5_w_glu, m_mem_norm_g, m_w_kv_mem, m_w_br_a, m_w_br_b, m_w_br_c, m_w_out, m_final_g, v_norm_g, v_w_in, v_conv_w, v_gdn_a_log, v_gdn_dt_bias, v_gdn_norm_g, v_s5_lambda_re, v_s5_lambda_im, v_s5_log_dt, v_s5_b_re, v_s5_b_im, v_s5_c_re, v_s5_c_im, v_s5_d, v_s5_w_glu, v_mem_norm_g, v_w_kv_mem, v_w_br_a, v_w_br_b, v_w_br_c, v_w_out, v_final_g):
    wts = dict(norm_g=norm_g, w_in=w_in, conv_w=conv_w, gdn_a_log=gdn_a_log, gdn_dt_bias=gdn_dt_bias, gdn_norm_g=gdn_norm_g,
               s5_lambda_re=s5_lambda_re, s5_lambda_im=s5_lambda_im, s5_log_dt=s5_log_dt, s5_b_re=s5_b_re, s5_b_im=s5_b_im,
               s5_c_re=s5_c_re, s5_c_im=s5_c_im, s5_d=s5_d, s5_w_glu=s5_w_glu, mem_norm_g=mem_norm_g, w_kv_mem=w_kv_mem,
               w_br_a=w_br_a, w_br_b=w_br_b, w_br_c=w_br_c, w_out=w_out, final_g=final_g)
    mom = dict(norm_g=m_norm_g, w_in=m_w_in, conv_w=m_conv_w, gdn_a_log=m_gdn_a_log, gdn_dt_bias=m_gdn_dt_bias,
               gdn_norm_g=m_gdn_norm_g, s5_lambda_re=m_s5_lambda_re, s5_lambda_im=m_s5_lambda_im, s5_log_dt=m_s5_log_dt,
               s5_b_re=m_s5_b_re, s5_b_im=m_s5_b_im, s5_c_re=m_s5_c_re, s5_c_im=m_s5_c_im, s5_d=m_s5_d, s5_w_glu=m_s5_w_glu,
               mem_norm_g=m_mem_norm_g, w_kv_mem=m_w_kv_mem, w_br_a=m_w_br_a, w_br_b=m_w_br_b, w_br_c=m_w_br_c, w_out=m_w_out,
               final_g=m_final_g)
    vel = dict(norm_g=v_norm_g, w_in=v_w_in, conv_w=v_conv_w, gdn_a_log=v_gdn_a_log, gdn_dt_bias=v_gdn_dt_bias,
               gdn_norm_g=v_gdn_norm_g, s5_lambda_re=v_s5_lambda_re, s5_lambda_im=v_s5_lambda_im, s5_log_dt=v_s5_log_dt,
               s5_b_re=v_s5_b_re, s5_b_im=v_s5_b_im, s5_c_re=v_s5_c_re, s5_c_im=v_s5_c_im, s5_d=v_s5_d, s5_w_glu=v_s5_w_glu,
               mem_norm_g=v_mem_norm_g, w_kv_mem=v_w_kv_mem, w_br_a=v_w_br_a, w_br_b=v_w_br_b, w_br_c=v_w_br_c, w_out=v_w_out,
               final_g=v_final_g)
    x2, mem2, tgt = x[0], mem[0], loss_target[0]
    s, d = x2.shape
    n_chunks = s // CHUNK

    shards = [wts[n][0].astype(BF16) for n in BIG]
    gathered = allgather_weights(shards[:1], None, "allgather_w_in")
    *rest, cg = allgather_weights(shards[1:], conv_w[0], "allgather_rest")
    gathered = list(gathered) + rest
    full = {}
    for n, wg in zip(BIG[1:], gathered[1:]):
        rows, cols = wg.shape[1:]
        full[n] = wg.transpose(1, 0, 2).reshape(rows, 4 * cols) if n in COL_SHARDED else wg.reshape(4 * rows, cols)
    wp = _pack_w_in(gathered[0])
    conv_full = cg.transpose(1, 0, 2).reshape(conv_w.shape[1], -1)
    alog_pad = jnp.zeros((1, BA_W), F32).at[0, NHEAD:2 * NHEAD].set(gdn_a_log[0])
    dt_pad = jnp.zeros((1, BA_W), F32).at[0, NHEAD:2 * NHEAD].set(gdn_dt_bias[0])

    mm = functools.partial(matmul, tm=1024, tn=1024)
    u, r1 = rms_fwd(x2, norm_g, "rms_fwd_x")
    proj = mm(u, wp, mode="nn", out_dtype=F32, tk=2048, name="mm_proj")
    q, k, v, bg, gcol, gt = gdn_prep_fwd(proj, conv_full, alog_pad, dt_pad)
    gt3 = gt.reshape(BA_W, n_chunks, CHUNK).transpose(1, 0, 2)
    gu, gw, qd, kd, qk, tinv = gdn_intra_fwd(q, k, v, bg, gcol, gt3)
    o_raw, states = gdn_seq_fwd(gu, gw, qd, kd, qk, gt3)
    ga = gdn_out_fwd(o_raw, proj, ZA_CB, gdn_norm_g)

    xb = proj[:, XB_CB * S5_IN:(XB_CB + 1) * S5_IN]
    y_ssm, s5_saved = s5_ssm_fwd(xb, s5_lambda_re[0], s5_lambda_im[0], s5_log_dt[0], s5_b_re[0], s5_b_im[0],
                                 s5_c_re[0], s5_c_im[0])
    yb = s5_act_fwd(y_ssm, proj, XB_CB, s5_d)
    glu = mm(yb, full["s5_w_glu"], mode="nn", out_dtype=F32, tk=1024, name="mm_glu")
    gb = s5_glu_fwd(glu, proj, ZB_CB)

    mem_n, rm = rms_fwd(mem2, mem_norm_g, "rms_fwd_mem")
    kv = mm(mem_n, full["w_kv_mem"], mode="nn", out_dtype=BF16, tk=2048, name="mm_kv")
    o_c = xa_fwd(proj, kv)
    gcx = gate_fwd(o_c, proj, ZC_CB, "gate_fwd_c")

    pa = mm(ga, full["w_br_a"], mode="nn", out_dtype=F32, tk=1024, name="mm_pa")
    pb = mm(gb, full["w_br_b"], mode="nn", out_dtype=F32, tk=1024, name="mm_pb")
    pc = mm(gcx, full["w_br_c"], mode="nn", out_dtype=F32, tk=1024, name="mm_pc")
    merged = merge_fwd(pa, pb, pc, proj, GATE_CB)
    hres = mm(merged, full["w_out"], mode="nn", out_dtype=F32, tk=2048, name="mm_out")
    dh, dhb, loss_part, d_final_g = final_stage(x2, hres, tgt, final_g.reshape(1, d))

    gfull = {}
    dmerged = mm(dhb, full["w_out"], mode="nt", out_dtype=F32, tk=2048, name="mm_dmerged")
    gfull["w_out"] = mm(merged, dhb, mode="tn", out_dtype=BF16, tk=1024, name="mm_dw_out")
    dpa, dpb, dpc, dg0, dg1, dg2 = merge_bwd(dmerged, pa, pb, pc, proj, GATE_CB)
    dga = mm(dpa, full["w_br_a"], mode="nt", out_dtype=F32, tk=2048, name="mm_dga")
    dgb = mm(dpb, full["w_br_b"], mode="nt", out_dtype=F32, tk=2048, name="mm_dgb")
    dgc = mm(dpc, full["w_br_c"], mode="nt", out_dtype=F32, tk=2048, name="mm_dgc")
    gfull["w_br_a"] = mm(ga, dpa, mode="tn", out_dtype=BF16, tk=1024, name="mm_dw_a")
    gfull["w_br_b"] = mm(gb, dpb, mode="tn", out_dtype=BF16, tk=1024, name="mm_dw_b")
    gfull["w_br_c"] = mm(gcx, dpc, mode="tn", out_dtype=BF16, tk=1024, name="mm_dw_c")

    do_raw, dza, d_gdn_norm = gdn_out_bwd(dga, o_raw, proj, ZA_CB, gdn_norm_g)
    du_, dw_, dqd, dkd, dqk, dgl = gdn_seq_bwd(do_raw, gu, gw, qd, kd, qk, gt3, states)
    dq, dk, dv, dbg = gdn_intra_bwd(q, k, v, bg, gcol, gt3, tinv, du_, dw_, dqd, dkd, dqk, dgl)
    dc, dba, dcw0, dcw1, dcw2, dcw3, d_alog, d_dt = gdn_prep_bwd1(proj, conv_full, alog_pad, dt_pad, dq, dk, dv, dbg)
    dqkv = gdn_prep_bwd2(dc, conv_full)
    d_conv = jnp.concatenate([dcw0, dcw1, dcw2, dcw3], axis=0)

    dval, dgate, dzb = s5_glu_bwd(dgb, glu, proj, ZB_CB)
    dglu = jnp.concatenate([dval, dgate], axis=1)
    dyb = mm(dglu, full["s5_w_glu"], mode="nt", out_dtype=F32, tk=2048, name="mm_dyb")
    gfull["s5_w_glu"] = mm(yb, dglu, mode="tn", out_dtype=BF16, tk=1024, name="mm_dw_glu")
    dy_ssm, dxb_direct, d_s5_d = s5_act_bwd(dyb, y_ssm, proj, XB_CB, s5_d)
    dxb_scan, d_lre, d_lim, d_ldt, d_bre, d_bim, d_cre, d_cim = s5_ssm_bwd(dy_ssm, s5_saved)
    dxb = add_cast(dxb_direct, dxb_scan, "s5_dxb")

    do_c, dzc = gate_bwd(dgc, o_c, proj, ZC_CB, "gate_bwd_c")
    dqc, dkv = xa_bwd(do_c, proj, kv)
    gfull["w_kv_mem"] = mm(mem_n, dkv, mode="tn", out_dtype=BF16, tk=256, name="mm_dw_kv")
    dmem_n = mm(dkv, full["w_kv_mem"], mode="nt", out_dtype=F32, tk=2048, name="mm_dmem")
    d_mem_norm = rms_bwd_g(dmem_n, mem2, rm, "rms_bwd_mem")

    dproj = jnp.concatenate([dqkv, dza, dxb, dzb, dqc, dzc, dg0, dg1, dg2, dba], axis=1)
    dwp = matmul(u, dproj, mode="tn", out_dtype=BF16, tm=2048, tn=1024, tk=512, name="mm_dw_in")
    du = mm(dproj, wp, mode="nt", out_dtype=F32, tk=1024, name="mm_du")
    grad_x, d_norm_g = rms_bwd_x(du, x2, r1, norm_g, dh)

    by_shard = [_unpack_w_in(dwp)]
    for n in BIG[1:]:
        rows, cols = wts[n].shape[1:]
        g = gfull[n]
        by_shard.append(g.reshape(rows, 4, cols).transpose(1, 0, 2) if n in COL_SHARDED else g.reshape(4, rows, cols))
    small_g = dict(norm_g=d_norm_g, gdn_a_log=d_alog[:, NHEAD:2 * NHEAD], gdn_dt_bias=d_dt[:, NHEAD:2 * NHEAD],
                   gdn_norm_g=d_gdn_norm, s5_lambda_re=d_lre, s5_lambda_im=d_lim, s5_log_dt=d_ldt, s5_b_re=d_bre, s5_b_im=d_bim,
                   s5_c_re=d_cre, s5_c_im=d_cim, s5_d=d_s5_d, mem_norm_g=d_mem_norm, final_g=d_final_g)
    small_send = _pack_small([small_g[n] for n in SMALL] + [d_conv, loss_part])
    core = lax.axis_index("c").astype(jnp.int32).reshape(1)
    got = list(exchange_cores(by_shard[:1], None, "exchange_cores_w_in"))
    *got_rest, got_small = exchange_cores(by_shard[1:], small_send, "exchange_cores_rest")
    chip_sums = [pair_sum(core, g, r, "sum_cores_" + n) for n, g, r in zip(BIG, by_shard, got + got_rest)]
    from_chips = exchange_chips(chip_sums)
    fulls = [sum_chips(core, a, "sum_chips_" + n) for n, a in zip(BIG, from_chips)]
    small_sum = sum_pieces(got_small, "sum_small")
    grads = dict(zip(BIG, sibling_exchange(fulls)))
    small_shapes = [wts[n].shape for n in SMALL] + [d_conv.shape, (1, 1)]
    *small_list, conv_g_full, loss_sum = _unpack_small(small_sum, small_shapes)
    grads.update(zip(SMALL, small_list))
    cw = conv_w.shape[2]
    shard_idx = 2 * lax.axis_index("x") + lax.axis_index("y")
    grads["conv_w"] = lax.dynamic_slice(conv_g_full, (0, shard_idx * cw), (conv_w.shape[1], cw))[None]

    delta, new_m, new_v = {}, {}, {}
    for n in BIG + ("conv_w",):
        delta[n], new_m[n], new_v[n] = adamw(wts[n], grads[n], mom[n], vel[n], "adamw_" + n)
    packed = [_pack_small([src[n] for n in SMALL]) for src in (wts, grads, mom, vel)]
    res = adamw(*packed, "adamw_small")
    shapes = [wts[n].shape for n in SMALL]
    for dst, flat in zip((delta, new_m, new_v), res):
        dst.update(zip(SMALL, _unpack_small(flat, shapes)))
    for n in SMALL:
        grads[n] = grads[n].reshape(wts[n].shape)

    return (loss_sum.reshape(()), grad_x.reshape(x.shape), *[grads[n] for n in WEIGHTS], *[delta[n] for n in WEIGHTS],
            *[new_m[n] for n in WEIGHTS], *[new_v[n] for n in WEIGHTS])
```

```python
import functools
import math

import jax
import jax.numpy as jnp
from jax import lax
from jax.experimental import pallas as pl
from jax.experimental.pallas import tpu as pltpu

F32 = jnp.float32
BF16 = jnp.bfloat16
HI = lax.Precision.HIGHEST

EPS = 1e-6
CHUNK = 64
HEAD = 128
NHEAD = 8
XA_HEADS = 4
S5_GROUPS = 64
S5_STATE = 64
S5_GROUP = 16
NSEG = 8
ADAM_LR, ADAM_B1, ADAM_B2, ADAM_EPS, ADAM_WD, ADAM_STEP = 0.001, 0.9, 0.999, 1e-08, 0.01, 10
VMEM_LIMIT = 56 * 2 ** 20


def _cparams(sem=None):
    return pltpu.CompilerParams(dimension_semantics=sem, vmem_limit_bytes=VMEM_LIMIT)


def _sigmoid(x):
    return 1.0 / (1.0 + jnp.exp(-x))


def _silu(x):
    return x * _sigmoid(x)


def _dsilu(x):
    s = _sigmoid(x)
    return s * (1.0 + x * (1.0 - s))


def _softplus(x):
    return jnp.maximum(x, 0.0) + jnp.log(1.0 + jnp.exp(-jnp.abs(x)))


_GELU_C = math.sqrt(2.0 / math.pi)


def _gelu(x):
    return 0.5 * x * (1.0 + jnp.tanh(_GELU_C * (x + 0.044715 * x * x * x)))


def _dgelu(x):
    t = jnp.tanh(_GELU_C * (x + 0.044715 * x * x * x))
    return 0.5 * (1.0 + t) + 0.5 * x * (1.0 - t * t) * _GELU_C * (1.0 + 3.0 * 0.044715 * x * x)


_DIMS = {"nn": (((1,), (0,)), ((), ())), "nt": (((1,), (1,)), ((), ())), "tn": (((0,), (0,)), ((), ()))}


class Side:
    def __init__(self, operands, out_shapes, scratch, start, finish):
        self.operands, self.out_shapes, self.scratch, self.start, self.finish = operands, out_shapes, scratch, start, finish


def matmul(a, b, *, mode, out_dtype, tm, tn, tk, name, side=None):
    if mode == "nn":
        (m, k), n = a.shape, b.shape[1]
    elif mode == "nt":
        (m, k), n = a.shape, b.shape[0]
    else:
        (k, m), n = a.shape, b.shape[1]
    tm, tn, tk = min(tm, m), min(tn, n), min(tk, k)
    assert m % tm == 0 and n % tn == 0 and k % tk == 0, (name, m, n, k, tm, tn, tk)
    grid = (m // tm, n // tn, k // tk)
    nk = grid[2]
    dims = _DIMS[mode]
    n_in = 0 if side is None else len(side.operands)
    n_out = 0 if side is None else len(side.out_shapes)
    n_acc = 0 if nk == 1 else 1

    def body(*refs):
        a_ref, b_ref, o_ref = refs[0], refs[1], refs[2 + n_in]
        scratch = refs[3 + n_in + n_out:]
        side_refs = (refs[2:2 + n_in], refs[3 + n_in:3 + n_in + n_out], scratch[n_acc:])
        ids = [pl.program_id(d) for d in range(3)]
        if side is not None:
            @pl.when((ids[0] == 0) & (ids[1] == 0) & (ids[2] == 0))
            def _():
                side.start(*side_refs)

        prod = lax.dot_general(a_ref[...].astype(BF16), b_ref[...].astype(BF16), dims, preferred_element_type=F32)
        if nk == 1:
            o_ref[...] = prod.astype(out_dtype)
        else:
            acc_ref = scratch[0]

            @pl.when(ids[2] == 0)
            def _():
                acc_ref[...] = prod

            @pl.when(ids[2] > 0)
            def _():
                acc_ref[...] += prod

            @pl.when(ids[2] == nk - 1)
            def _():
                o_ref[...] = acc_ref[...].astype(out_dtype)

        if side is not None:
            @pl.when((ids[0] == grid[0] - 1) & (ids[1] == grid[1] - 1) & (ids[2] == nk - 1))
            def _():
                side.finish(*side_refs)

    a_spec = pl.BlockSpec((tk, tm), lambda i, j, q: (q, i)) if mode == "tn" else pl.BlockSpec((tm, tk), lambda i, j, q: (i, q))
    b_spec = pl.BlockSpec((tn, tk), lambda i, j, q: (j, q)) if mode == "nt" else pl.BlockSpec((tk, tn), lambda i, j, q: (q, j))
    o_spec = pl.BlockSpec((tm, tn), lambda i, j, q: (i, j))
    o_shape = jax.ShapeDtypeStruct((m, n), out_dtype)
    acc = [] if nk == 1 else [pltpu.VMEM((tm, tn), F32)]
    if side is None:
        return pl.pallas_call(
            body, name=name, grid=grid, in_specs=[a_spec, b_spec], out_specs=o_spec, out_shape=o_shape, scratch_shapes=acc,
            compiler_params=_cparams(("parallel", "parallel", "arbitrary")),
        )(a, b)
    hbm = pl.BlockSpec(memory_space=pltpu.HBM)
    return pl.pallas_call(
        body, name=name, grid=grid, in_specs=[a_spec, b_spec] + [hbm] * n_in, out_specs=[o_spec] + [hbm] * n_out,
        out_shape=[o_shape] + list(side.out_shapes), scratch_shapes=acc + list(side.scratch),
        compiler_params=_cparams(("arbitrary", "arbitrary", "arbitrary")),
    )(a, b, *side.operands)


def rowwise(fn, ins, outs, *, rows, tr, name, consts=(), reds=()):
    tr = min(tr, rows)
    assert rows % tr == 0, (name, rows, tr)
    n_in, n_c, n_o = len(ins), len(consts), len(outs)

    def body(*refs):
        vals = [r[...] for r in refs[:n_in + n_c]]
        res = fn(*vals)
        o_refs = refs[n_in + n_c:]
        for r, v in zip(o_refs[:n_o], res[:n_o]):
            r[...] = v.astype(r.dtype)
        if reds:
            i = pl.program_id(0)

            @pl.when(i == 0)
            def _():
                for r, v in zip(o_refs[n_o:], res[n_o:]):
                    r[...] = v.astype(r.dtype)

            @pl.when(i > 0)
            def _():
                for r, v in zip(o_refs[n_o:], res[n_o:]):
                    r[...] += v.astype(r.dtype)

    in_specs = [pl.BlockSpec((tr, w), functools.partial(lambda i, cb: (i, cb), cb=cb)) for (_, w, cb) in ins]
    in_specs += [pl.BlockSpec(c.shape, lambda i: (0, 0)) for c in consts]
    out_specs = [pl.BlockSpec((tr, w), lambda i: (i, 0)) for (w, _) in outs]
    out_specs += [pl.BlockSpec(s, lambda i: (0, 0)) for (s, _) in reds]
    out_shape = [jax.ShapeDtypeStruct((rows, w), d) for (w, d) in outs]
    out_shape += [jax.ShapeDtypeStruct(s, d) for (s, d) in reds]
    res = pl.pallas_call(
        body, name=name, grid=(rows // tr,), in_specs=in_specs, out_specs=out_specs, out_shape=out_shape,
        compiler_params=_cparams(("arbitrary",) if reds else ("parallel",)),
    )(*[a for (a, _, _) in ins], *consts)
    return res


def _colsum(x):
    return jnp.sum(x, axis=0, keepdims=True)


def rms_fwd(x, g, name):
    s, d = x.shape

    def fn(xv, gv):
        r = lax.rsqrt(jnp.mean(xv * xv, axis=-1, keepdims=True) + EPS)
        return xv * r * gv, r

    return rowwise(fn, [(x, d, 0)], [(d, BF16), (1, F32)], rows=s, tr=256, name=name, consts=[g])


def rms_bwd_x(du, x, r, g, dh):
    s, d = x.shape

    def fn(duv, xv, rv, dhv, gv):
        dyg = duv * gv
        dx = rv * dyg - xv * (rv * rv * rv) * jnp.mean(dyg * xv, axis=-1, keepdims=True)
        return dhv + dx, _colsum(duv * xv * rv)

    return rowwise(fn, [(du, d, 0), (x, d, 0), (r, 1, 0), (dh, d, 0)], [(d, F32)], rows=s, tr=256,
                   name="rms_bwd_x", consts=[g], reds=[((1, d), F32)])


def rms_bwd_g(du, x, r, name):
    s, d = x.shape

    def fn(duv, xv, rv):
        return (_colsum(duv * xv * rv),)

    return rowwise(fn, [(du, d, 0), (x, d, 0), (r, 1, 0)], [], rows=s, tr=256, name=name, reds=[((1, d), F32)])[0]


def final_stage(x, hres, target, g):
    s, d = x.shape

    def fn(xv, hv, tv, gv):
        h = xv + hv
        r = lax.rsqrt(jnp.mean(h * h, axis=-1, keepdims=True) + EPS)
        y = h * r * gv
        e = y - tv
        loss = 0.5 * jnp.sum(jnp.sum(e * e, axis=-1, keepdims=True), axis=0, keepdims=True) / d
        dy = e / d
        dyg = dy * gv
        dh = r * dyg - h * (r * r * r) * jnp.mean(dyg * h, axis=-1, keepdims=True)
        return dh, dh, loss, _colsum(dy * h * r)

    return rowwise(fn, [(x, d, 0), (hres, d, 0), (target, d, 0)], [(d, F32), (d, BF16)], rows=s, tr=256,
                   name="final_stage", consts=[g], reds=[((1, 1), F32), ((1, d), F32)])


def merge_fwd(pa, pb, pc, proj, gate_cb):
    s, d = pa.shape

    def fn(a, b, c, g0, g1, g2):
        return (_sigmoid(g0) * a + _sigmoid(g1) * b + _sigmoid(g2) * c,)

    ins = [(pa, d, 0), (pb, d, 0), (pc, d, 0)] + [(proj, d, gate_cb + i) for i in range(3)]
    return rowwise(fn, ins, [(d, BF16)], rows=s, tr=256, name="merge_fwd")[0]


def merge_bwd(dm, pa, pb, pc, proj, gate_cb):
    s, d = pa.shape

    def fn(dmv, a, b, c, g0, g1, g2):
        s0, s1, s2 = _sigmoid(g0), _sigmoid(g1), _sigmoid(g2)
        return (dmv * s0, dmv * s1, dmv * s2,
                dmv * a * s0 * (1.0 - s0), dmv * b * s1 * (1.0 - s1), dmv * c * s2 * (1.0 - s2))

    ins = [(dm, d, 0), (pa, d, 0), (pb, d, 0), (pc, d, 0)] + [(proj, d, gate_cb + i) for i in range(3)]
    return rowwise(fn, ins, [(d, BF16)] * 6, rows=s, tr=128, name="merge_bwd")


def gate_fwd(o, proj, z_cb, name):
    s, w = o.shape

    def fn(ov, zv):
        return (ov * _silu(zv),)

    return rowwise(fn, [(o, w, 0), (proj, w, z_cb)], [(w, BF16)], rows=s, tr=512, name=name)[0]


def gate_bwd(dgo, o, proj, z_cb, name):
    s, w = o.shape

    def fn(dv, ov, zv):
        return dv * _silu(zv), dv * ov * _dsilu(zv)

    return rowwise(fn, [(dgo, w, 0), (o, w, 0), (proj, w, z_cb)], [(w, F32), (w, BF16)], rows=s, tr=512, name=name)


def gdn_out_fwd(o_raw, proj, z_cb, gn):
    s, w = o_raw.shape

    def fn(ov, zv, gv):
        outs = []
        for h in range(NHEAD):
            oh = ov[:, h * HEAD:(h + 1) * HEAD]
            r = lax.rsqrt(jnp.mean(oh * oh, axis=-1, keepdims=True) + EPS)
            outs.append(oh * r * gv)
        return (jnp.concatenate(outs, axis=1) * _silu(zv),)

    return rowwise(fn, [(o_raw, w, 0), (proj, w, z_cb)], [(w, BF16)], rows=s, tr=512, name="gdn_out_fwd", consts=[gn])[0]


def gdn_out_bwd(dga, o_raw, proj, z_cb, gn):
    s, w = o_raw.shape

    def fn(dv, ov, zv, gv):
        sz, dsz = _silu(zv), _dsilu(zv)
        do_l, dz_l = [], []
        dg = jnp.zeros((1, HEAD), F32)
        for h in range(NHEAD):
            sl = slice(h * HEAD, (h + 1) * HEAD)
            oh, dgh = ov[:, sl], dv[:, sl]
            r = lax.rsqrt(jnp.mean(oh * oh, axis=-1, keepdims=True) + EPS)
            on = oh * r * gv
            don = dgh * sz[:, sl]
            dz_l.append(dgh * on * dsz[:, sl])
            dg = dg + _colsum(don * oh * r)
            dyg = don * gv
            do_l.append(r * dyg - oh * (r * r * r) * jnp.mean(dyg * oh, axis=-1, keepdims=True))
        return jnp.concatenate(do_l, axis=1), jnp.concatenate(dz_l, axis=1), dg

    return rowwise(fn, [(dga, w, 0), (o_raw, w, 0), (proj, w, z_cb)], [(w, F32), (w, BF16)], rows=s, tr=512,
                   name="gdn_out_bwd", consts=[gn], reds=[((1, HEAD), F32)])


def s5_act_fwd(y_ssm, proj, xb_cb, dvec):
    s, w = y_ssm.shape

    def fn(yv, xv, dv):
        return (_gelu(yv + dv * xv),)

    return rowwise(fn, [(y_ssm, w, 0), (proj, w, xb_cb)], [(w, BF16)], rows=s, tr=512, name="s5_act_fwd", consts=[dvec])[0]


def s5_act_bwd(dyb, y_ssm, proj, xb_cb, dvec):
    s, w = y_ssm.shape

    def fn(dv_, yv, xv, dv):
        dpre = dv_ * _dgelu(yv + dv * xv)
        return dpre, dpre * dv, _colsum(dpre * xv)

    return rowwise(fn, [(dyb, w, 0), (y_ssm, w, 0), (proj, w, xb_cb)], [(w, F32), (w, F32)], rows=s, tr=512,
                   name="s5_act_bwd", consts=[dvec], reds=[((1, w), F32)])


def s5_glu_fwd(glu, proj, z_cb):
    s, w2 = glu.shape
    w = w2 // 2

    def fn(val, gate, zv):
        return (val * _sigmoid(gate) * _silu(zv),)

    return rowwise(fn, [(glu, w, 0), (glu, w, 1), (proj, w, z_cb)], [(w, BF16)], rows=s, tr=512, name="s5_glu_fwd")[0]


def s5_glu_bwd(dgb, glu, proj, z_cb):
    s, w2 = glu.shape
    w = w2 // 2

    def fn(dv, val, gate, zv):
        sg = _sigmoid(gate)
        ob = val * sg
        dob = dv * _silu(zv)
        return dob * sg, dob * val * sg * (1.0 - sg), dv * ob * _dsilu(zv)

    return rowwise(fn, [(dgb, w, 0), (glu, w, 0), (glu, w, 1), (proj, w, z_cb)], [(w, BF16)] * 3, rows=s, tr=512,
                   name="s5_glu_bwd")


def add_cast(a, b, name):
    s, w = a.shape

    def fn(av, bv):
        return (av + bv,)

    return rowwise(fn, [(a, w, 0), (b, w, 0)], [(w, BF16)], rows=s, tr=512, name=name)[0]


QKV_W, QKV_CB = 3072, 0
ZA_CB, XB_CB, ZB_CB, QC_CB, ZC_CB = 3, 4, 5, 6, 7
GATE_CB = 4
BA_CB, BA_W = 112, 128
BA_PAD = 1024
PROJ_W = 14336 + BA_PAD


def _dot(a, b, dims="nn", prec=None):
    if prec is None:
        a, b = a.astype(BF16), b.astype(BF16)
    return lax.dot_general(a, b, _DIMS[dims], preferred_element_type=F32, precision=prec)


def _split(a):
    hi = a.astype(BF16)
    return hi, (a - hi.astype(F32)).astype(BF16)


def _dot3(a, b, dims="nn"):
    (ah, al), (bh, bl) = _split(a), _split(b)
    d = functools.partial(lax.dot_general, dimension_numbers=_DIMS[dims], preferred_element_type=F32)
    return d(ah, bh) + (d(ah, bl) + d(al, bh))


def _iota2(shape, dim):
    return lax.broadcasted_iota(jnp.int32, shape, dim)


def _conv_taps(xs, tr, k):
    if k == 0:
        return xs[8:8 + tr]
    return pltpu.roll(xs, k, 0)[8:8 + tr]


def _conv_silu_parts(xv, halo, wv, first):
    tr = xv.shape[0]
    xs = jnp.concatenate([jnp.where(first, 0.0, halo), xv], axis=0)
    taps = [_conv_taps(xs, tr, 3 - j) for j in range(4)]
    c = taps[0] * wv[0:1] + taps[1] * wv[1:2] + taps[2] * wv[2:3] + taps[3] * wv[3:4]
    return taps, c


def gdn_prep_fwd(proj, conv_w, alog_pad, dt_pad):
    s = proj.shape[0]
    tr = min(256, s)
    w = NHEAD * HEAD

    def body(x_ref, halo_ref, ba_ref, w_ref, al_ref, dt_ref, q_ref, k_ref, v_ref, bg_ref, gcol_ref, gt_ref):
        first = pl.program_id(0) == 0
        _, c = _conv_silu_parts(x_ref[...], halo_ref[...], w_ref[...], first)
        sv = _silu(c)
        for h in range(NHEAD):
            sl = slice(h * HEAD, (h + 1) * HEAD)
            qh, kh = sv[:, h * HEAD:(h + 1) * HEAD], sv[:, w + h * HEAD:w + (h + 1) * HEAD]
            q_ref[:, sl] = qh * lax.rsqrt(jnp.sum(qh * qh, axis=-1, keepdims=True) + EPS) * (HEAD ** -0.5)
            k_ref[:, sl] = kh * lax.rsqrt(jnp.sum(kh * kh, axis=-1, keepdims=True) + EPS)
        v_ref[...] = sv[:, 2 * w:]
        ba = ba_ref[...]
        lane = _iota2(ba.shape, 1)
        beta = _sigmoid(ba)
        g = -jnp.exp(al_ref[...]) * _softplus(ba + dt_ref[...])
        bg = jnp.where(lane < NHEAD, beta, jnp.where(lane < 2 * NHEAD, g, 0.0))
        bg_ref[...] = bg
        er, ec = _iota2((BA_W, BA_W), 0), _iota2((BA_W, BA_W), 1)
        expand = jnp.where((er == NHEAD + ec // 8) & (ec < 8 * NHEAD), 1.0, 0.0)
        grep = _dot(bg, expand, prec=HI)
        lr, lc = _iota2((tr, tr), 0), _iota2((tr, tr), 1)
        tril = jnp.where((lr // CHUNK == lc // CHUNK) & (lr >= lc), 1.0, 0.0)
        gc = _dot(tril, grep, prec=HI)
        gcol_ref[...] = gc
        gt_ref[...] = gc.T

    nb8 = tr // 8
    return pl.pallas_call(
        body, name="gdn_prep_fwd", grid=(s // tr,),
        in_specs=[pl.BlockSpec((tr, QKV_W), lambda i: (i, QKV_CB)),
                  pl.BlockSpec((8, QKV_W), lambda i: (jnp.maximum(i * nb8 - 1, 0), QKV_CB)),
                  pl.BlockSpec((tr, BA_W), lambda i: (i, BA_CB)),
                  pl.BlockSpec(conv_w.shape, lambda i: (0, 0)),
                  pl.BlockSpec((1, BA_W), lambda i: (0, 0)), pl.BlockSpec((1, BA_W), lambda i: (0, 0))],
        out_specs=[pl.BlockSpec((tr, w), lambda i: (i, 0))] * 3 + [pl.BlockSpec((tr, BA_W), lambda i: (i, 0))] * 2
        + [pl.BlockSpec((BA_W, tr), lambda i: (0, i))],
        out_shape=[jax.ShapeDtypeStruct((s, w), F32)] * 3 + [jax.ShapeDtypeStruct((s, BA_W), F32)] * 2
        + [jax.ShapeDtypeStruct((BA_W, s), F32)],
        compiler_params=_cparams(("parallel",)),
    )(proj, proj, proj, conv_w, alog_pad, dt_pad)


def _chunk_common(qh, kh, bgv, gcolv, gtv, h):
    beta = bgv[:, h:h + 1]
    gcc = gcolv[:, 8 * h:8 * h + 1]
    gcr = jnp.concatenate([gtv[8 * h:8 * h + 8, :]] * (CHUNK // 8), axis=0)
    ii, jj = _iota2((CHUNK, CHUNK), 0), _iota2((CHUNK, CHUNK), 1)
    incl, strict = ii >= jj, ii > jj
    decay = jnp.where(incl, jnp.exp(jnp.where(incl, gcc - gcr, 0.0)), 0.0)
    gl = gcr[:, CHUNK - 1:CHUNK]
    return beta, gcc, decay, strict, gl


def gdn_intra_fwd(q, k, v, bg, gcol, gt3):
    s, w = q.shape
    n = s // CHUNK

    def body(q_ref, k_ref, v_ref, bg_ref, gcol_ref, gt_ref, u_ref, w_ref, qd_ref, kd_ref, qk_ref, t_ref):
        bgv, gcolv, gtv = bg_ref[...], gcol_ref[...], gt_ref[0]
        ii, jj = _iota2((CHUNK, CHUNK), 0), _iota2((CHUNK, CHUNK), 1)
        eye = jnp.where(ii == jj, 1.0, 0.0)
        ps, ts, rhs = [], [], []
        for h in range(NHEAD):
            sl = slice(h * HEAD, (h + 1) * HEAD)
            qh, kh, vh = q_ref[:, sl], k_ref[:, sl], v_ref[:, sl]
            beta, gcc, decay, strict, gl = _chunk_common(qh, kh, bgv, gcolv, gtv, h)
            kb = kh * beta
            eg = jnp.exp(gcc)
            p = -jnp.where(strict, _dot(kb, kh, "nt") * decay, 0.0)
            ps.append(p)
            ts.append(eye + p)
            rhs.append((vh * beta, kb * eg))
            qd_ref[:, sl] = qh * eg
            kd_ref[:, sl] = kh * jnp.exp(gl - gcc)
            qk_ref[0, h] = _dot(qh, kh, "nt") * decay
        for _ in range(5):
            ps = [_dot3(p, p) for p in ps]
            ts = [t + _dot3(t, p) for t, p in zip(ts, ps)]
        for h in range(NHEAD):
            sl = slice(h * HEAD, (h + 1) * HEAD)
            u_ref[:, sl] = _dot3(ts[h], rhs[h][0])
            w_ref[:, sl] = _dot3(ts[h], rhs[h][1])
            t_ref[0, h] = ts[h]

    tok = pl.BlockSpec((CHUNK, w), lambda i: (i, 0))
    sm = pl.BlockSpec((CHUNK, BA_W), lambda i: (i, 0))
    sq = pl.BlockSpec((1, NHEAD, CHUNK, CHUNK), lambda i: (i, 0, 0, 0))
    return pl.pallas_call(
        body, name="gdn_intra_fwd", grid=(n,),
        in_specs=[tok, tok, tok, sm, sm, pl.BlockSpec((1, BA_W, CHUNK), lambda i: (i, 0, 0))],
        out_specs=[tok] * 4 + [sq, sq],
        out_shape=[jax.ShapeDtypeStruct((s, w), F32)] * 4 + [jax.ShapeDtypeStruct((n, NHEAD, CHUNK, CHUNK), F32)] * 2,
        compiler_params=_cparams(("parallel",)),
    )(q, k, v, bg, gcol, gt3)


def _state_decay(gtv, h):
    g8 = gtv[8 * h:8 * h + 8, CHUNK - 1:CHUNK]
    return jnp.exp(jnp.concatenate([g8] * (HEAD // 8), axis=0))


def gdn_seq_fwd(u, wd, qd, kd, qk, gt3):
    s, w = u.shape
    n = s // CHUNK

    def body(u_ref, w_ref, qd_ref, kd_ref, qk_ref, gt_ref, o_ref, st_ref, s_ref):
        @pl.when(pl.program_id(0) == 0)
        def _():
            s_ref[...] = jnp.zeros_like(s_ref)

        gtv = gt_ref[0]
        for h in range(NHEAD):
            sl = slice(h * HEAD, (h + 1) * HEAD)
            sh = s_ref[h]
            st_ref[0, h] = sh
            vn = u_ref[:, sl] - _dot(w_ref[:, sl], sh)
            o_ref[:, sl] = _dot(qd_ref[:, sl], sh) + _dot(qk_ref[0, h], vn)
            s_ref[h] = sh * _state_decay(gtv, h) + _dot(kd_ref[:, sl], vn, "tn")

    tok = pl.BlockSpec((CHUNK, w), lambda i: (i, 0))
    return pl.pallas_call(
        body, name="gdn_seq_fwd", grid=(n,),
        in_specs=[tok] * 4 + [pl.BlockSpec((1, NHEAD, CHUNK, CHUNK), lambda i: (i, 0, 0, 0)),
                              pl.BlockSpec((1, BA_W, CHUNK), lambda i: (i, 0, 0))],
        out_specs=[tok, pl.BlockSpec((1, NHEAD, HEAD, HEAD), lambda i: (i, 0, 0, 0))],
        out_shape=[jax.ShapeDtypeStruct((s, w), F32), jax.ShapeDtypeStruct((n, NHEAD, HEAD, HEAD), F32)],
        scratch_shapes=[pltpu.VMEM((NHEAD, HEAD, HEAD), F32)],
        compiler_params=_cparams(("arbitrary",)),
    )(u, wd, qd, kd, qk, gt3)


def gdn_seq_bwd(do, u, wd, qd, kd, qk, gt3, states):
    s, w = u.shape
    n = s // CHUNK

    def body(do_ref, u_ref, w_ref, qd_ref, kd_ref, qk_ref, gt_ref, st_ref,
             du_ref, dw_ref, dqd_ref, dkd_ref, dqk_ref, dgl_ref, ds_ref):
        @pl.when(pl.program_id(0) == 0)
        def _():
            ds_ref[...] = jnp.zeros_like(ds_ref)

        gtv = gt_ref[0]
        dgl_rows = []
        for h in range(NHEAD):
            sl = slice(h * HEAD, (h + 1) * HEAD)
            sh, dsp, doh = st_ref[0, h], ds_ref[h], do_ref[:, sl]
            wh, qdh, kdh, qkh = w_ref[:, sl], qd_ref[:, sl], kd_ref[:, sl], qk_ref[0, h]
            vn = u_ref[:, sl] - _dot(wh, sh)
            dvn = _dot(qkh, doh, "tn") + _dot(kdh, dsp)
            du_ref[:, sl] = dvn
            dw_ref[:, sl] = -_dot(dvn, sh, "nt")
            dqd_ref[:, sl] = _dot(doh, sh, "nt")
            dkd_ref[:, sl] = _dot(vn, dsp, "nt")
            dqk_ref[0, h] = _dot(doh, vn, "nt")
            dgl_rows.append(_colsum(sh * dsp))
            ds_ref[h] = dsp * _state_decay(gtv, h) + _dot(qdh, doh, "tn") - _dot(wh, dvn, "tn")
        dgl_ref[0] = jnp.concatenate(dgl_rows, axis=0)

    tok = pl.BlockSpec((CHUNK, w), lambda i: (n - 1 - i, 0))
    sq = pl.BlockSpec((1, NHEAD, CHUNK, CHUNK), lambda i: (n - 1 - i, 0, 0, 0))
    return pl.pallas_call(
        body, name="gdn_seq_bwd", grid=(n,),
        in_specs=[tok] * 5 + [sq, pl.BlockSpec((1, BA_W, CHUNK), lambda i: (n - 1 - i, 0, 0)),
                              pl.BlockSpec((1, NHEAD, HEAD, HEAD), lambda i: (n - 1 - i, 0, 0, 0))],
        out_specs=[tok] * 4 + [sq, pl.BlockSpec((1, NHEAD, HEAD), lambda i: (n - 1 - i, 0, 0))],
        out_shape=[jax.ShapeDtypeStruct((s, w), F32)] * 4 + [jax.ShapeDtypeStruct((n, NHEAD, CHUNK, CHUNK), F32),
                                                            jax.ShapeDtypeStruct((n, NHEAD, HEAD), F32)],
        scratch_shapes=[pltpu.VMEM((NHEAD, HEAD, HEAD), F32)],
        compiler_params=_cparams(("arbitrary",)),
    )(do, u, wd, qd, kd, qk, gt3, states)


def gdn_intra_bwd(q, k, v, bg, gcol, gt3, tinv, du, dw, dqd, dkd, dqk, dgl):
    s, w = q.shape
    n = s // CHUNK

    def body(q_ref, k_ref, v_ref, bg_ref, gcol_ref, gt_ref, t_ref, du_ref, dw_ref, dqd_ref, dkd_ref, dqk_ref, dgl_ref,
             dq_ref, dk_ref, dv_ref, dbg_ref):
        bgv, gcolv, gtv, dglv = bg_ref[...], gcol_ref[...], gt_ref[0], dgl_ref[0]
        ii, jj = _iota2((CHUNK, CHUNK), 0), _iota2((CHUNK, CHUNK), 1)
        triu = jnp.where(ii <= jj, 1.0, 0.0)
        ones = jnp.ones((CHUNK, BA_W), F32)
        lane = _iota2((CHUNK, BA_W), 1)
        row = _iota2((CHUNK, 1), 0)
        dbg = jnp.zeros((CHUNK, BA_W), F32)
        first = []
        for h in range(NHEAD):
            sl = slice(h * HEAD, (h + 1) * HEAD)
            qh, kh, vh = q_ref[:, sl], k_ref[:, sl], v_ref[:, sl]
            beta, gcc, decay, strict, gl = _chunk_common(qh, kh, bgv, gcolv, gtv, h)
            kb = kh * beta
            eg = jnp.exp(gcc)
            rv, rk = vh * beta, kb * eg
            t, duh, dwh = t_ref[0, h], du_ref[:, sl], dw_ref[:, sl]
            first.append((_dot3(duh, rv, "nt") + _dot3(dwh, rk, "nt"), _dot3(t, duh, "tn"), _dot3(t, dwh, "tn"),
                          _dot(kb, kh, "nt"), _dot(qh, kh, "nt")))
        second = [_dot3(t_ref[0, h], first[h][0], "tn") for h in range(NHEAD)]
        third = [_dot3(second[h], t_ref[0, h], "nt") for h in range(NHEAD)]
        for h in range(NHEAD):
            sl = slice(h * HEAD, (h + 1) * HEAD)
            qh, kh, vh = q_ref[:, sl], k_ref[:, sl], v_ref[:, sl]
            beta, gcc, decay, strict, gl = _chunk_common(qh, kh, bgv, gcolv, gtv, h)
            dqdh, dkdh, dqkh = dqd_ref[:, sl], dkd_ref[:, sl], dqk_ref[0, h]
            kb = kh * beta
            eg = jnp.exp(gcc)
            ekd = jnp.exp(gl - gcc)
            rk = kb * eg
            _, drv, drk, m, p = first[h]
            da = jnp.where(strict, -third[h], 0.0)
            dm = da * decay
            dpm = dqkh * decay
            dkb = _dot(dm, kh) + drk * eg
            dq = _dot(dpm, kh) + dqdh * eg
            dk = _dot(dm, kb, "tn") + _dot(dpm, qh, "tn") + dkdh * ekd + dkb * beta
            e = (da * m + dqkh * p) * decay
            sk = jnp.sum(dkdh * kh * ekd, axis=-1, keepdims=True)
            dgc = (jnp.sum(e, axis=-1, keepdims=True) - _dot3(e, ones, "tn")[:, 0:1]
                   + jnp.sum(dqdh * qh * eg, axis=-1, keepdims=True) - sk + jnp.sum(drk * rk, axis=-1, keepdims=True))
            dglast = jnp.sum(sk, axis=0, keepdims=True) + jnp.sum(dglv[h:h + 1, :], axis=-1, keepdims=True) * jnp.exp(gl)
            dgc = dgc + jnp.where(row == CHUNK - 1, dglast, 0.0)
            dg = _dot3(triu, dgc * ones)
            dbeta = jnp.sum(dkb * kh, axis=-1, keepdims=True) + jnp.sum(drv * vh, axis=-1, keepdims=True)
            dbg = dbg + jnp.where(lane == h, dbeta, 0.0) + jnp.where(lane == NHEAD + h, dg, 0.0)
            dq_ref[:, sl] = dq
            dk_ref[:, sl] = dk
            dv_ref[:, sl] = drv * beta
        dbg_ref[...] = dbg

    tok = pl.BlockSpec((CHUNK, w), lambda i: (i, 0))
    sm = pl.BlockSpec((CHUNK, BA_W), lambda i: (i, 0))
    sq = pl.BlockSpec((1, NHEAD, CHUNK, CHUNK), lambda i: (i, 0, 0, 0))
    return pl.pallas_call(
        body, name="gdn_intra_bwd", grid=(n,),
        in_specs=[tok, tok, tok, sm, sm, pl.BlockSpec((1, BA_W, CHUNK), lambda i: (i, 0, 0)), sq,
                  tok, tok, tok, tok, sq, pl.BlockSpec((1, NHEAD, HEAD), lambda i: (i, 0, 0))],
        out_specs=[tok] * 3 + [sm],
        out_shape=[jax.ShapeDtypeStruct((s, w), F32)] * 3 + [jax.ShapeDtypeStruct((s, BA_W), F32)],
        compiler_params=_cparams(("parallel",)),
    )(q, k, v, bg, gcol, gt3, tinv, du, dw, dqd, dkd, dqk, dgl)


def gdn_prep_bwd1(proj, conv_w, alog_pad, dt_pad, dq, dk, dv, dbg):
    s = proj.shape[0]
    tr = min(256, s)
    w = NHEAD * HEAD
    pad_w = BA_PAD

    def body(x_ref, halo_ref, ba_ref, w_ref, al_ref, dt_ref, dq_ref, dk_ref, dv_ref, dbg_ref,
             dc_ref, dba_ref, dw0_ref, dw1_ref, dw2_ref, dw3_ref, dal_ref, ddt_ref):
        i = pl.program_id(0)
        taps, c = _conv_silu_parts(x_ref[...], halo_ref[...], w_ref[...], i == 0)
        sv, dsv = _silu(c), _dsilu(c)
        for h in range(NHEAD):
            for base, d_ref, scale in ((0, dq_ref, HEAD ** -0.5), (w, dk_ref, 1.0)):
                sl = slice(base + h * HEAD, base + (h + 1) * HEAD)
                sh = sv[:, sl]
                dn = d_ref[:, h * HEAD:(h + 1) * HEAD]
                r = lax.rsqrt(jnp.sum(sh * sh, axis=-1, keepdims=True) + EPS)
                dsh = scale * (r * dn - sh * (r * r * r) * jnp.sum(dn * sh, axis=-1, keepdims=True))
                dc_ref[:, sl] = dsh * dsv[:, sl]
        dc_ref[:, 2 * w:] = dv_ref[...] * dsv[:, 2 * w:]
        dc = dc_ref[...]
        ba, dbgv = ba_ref[...], dbg_ref[...]
        lane = _iota2(ba.shape, 1)
        beta = _sigmoid(ba)
        ea = jnp.exp(al_ref[...])
        z = ba + dt_ref[...]
        g = -ea * _softplus(z)
        is_g = (lane >= NHEAD) & (lane < 2 * NHEAD)
        da_raw = jnp.where(is_g, dbgv * (-ea) * _sigmoid(z), 0.0)
        dba = jnp.where(lane < NHEAD, dbgv * beta * (1.0 - beta), da_raw)
        dba_ref[...] = jnp.concatenate([dba, jnp.zeros((tr, pad_w - BA_W), F32)], axis=1).astype(BF16)
        partial = [_colsum(dc * tp) for tp in taps] + [_colsum(jnp.where(is_g, dbgv * g, 0.0)), _colsum(da_raw)]
        red_refs = (dw0_ref, dw1_ref, dw2_ref, dw3_ref, dal_ref, ddt_ref)

        @pl.when(i == 0)
        def _():
            for r_, v_ in zip(red_refs, partial):
                r_[...] = v_

        @pl.when(i > 0)
        def _():
            for r_, v_ in zip(red_refs, partial):
                r_[...] += v_

    nb8 = tr // 8
    tok = pl.BlockSpec((tr, w), lambda i: (i, 0))
    one = lambda width: pl.BlockSpec((1, width), lambda i: (0, 0))
    return pl.pallas_call(
        body, name="gdn_prep_bwd1", grid=(s // tr,),
        in_specs=[pl.BlockSpec((tr, QKV_W), lambda i: (i, QKV_CB)),
                  pl.BlockSpec((8, QKV_W), lambda i: (jnp.maximum(i * nb8 - 1, 0), QKV_CB)),
                  pl.BlockSpec((tr, BA_W), lambda i: (i, BA_CB)),
                  pl.BlockSpec(conv_w.shape, lambda i: (0, 0)), one(BA_W), one(BA_W),
                  tok, tok, tok, pl.BlockSpec((tr, BA_W), lambda i: (i, 0))],
        out_specs=[pl.BlockSpec((tr, QKV_W), lambda i: (i, 0)), pl.BlockSpec((tr, pad_w), lambda i: (i, 0))]
        + [one(QKV_W)] * 4 + [one(BA_W)] * 2,
        out_shape=[jax.ShapeDtypeStruct((s, QKV_W), F32), jax.ShapeDtypeStruct((s, pad_w), BF16)]
        + [jax.ShapeDtypeStruct((1, QKV_W), F32)] * 4 + [jax.ShapeDtypeStruct((1, BA_W), F32)] * 2,
        compiler_params=_cparams(("arbitrary",)),
    )(proj, proj, proj, conv_w, alog_pad, dt_pad, dq, dk, dv, dbg)


def gdn_prep_bwd2(dc, conv_w):
    s = dc.shape[0]
    tr = min(256, s)
    nblk = s // tr
    nb8 = tr // 8

    def body(dc_ref, halo_ref, w_ref, o_ref):
        last = pl.program_id(0) == nblk - 1
        wv = w_ref[...]
        xs = jnp.concatenate([dc_ref[...], jnp.where(last, 0.0, halo_ref[...])], axis=0)
        acc = xs[:tr] * wv[3:4]
        for j in range(3):
            acc = acc + pltpu.roll(xs, tr + 8 - (3 - j), 0)[:tr] * wv[j:j + 1]
        o_ref[...] = acc.astype(BF16)

    return pl.pallas_call(
        body, name="gdn_prep_bwd2", grid=(nblk,),
        in_specs=[pl.BlockSpec((tr, QKV_W), lambda i: (i, 0)),
                  pl.BlockSpec((8, QKV_W), lambda i: (jnp.minimum((i + 1) * nb8, s // 8 - 1), 0)),
                  pl.BlockSpec(conv_w.shape, lambda i: (0, 0))],
        out_specs=pl.BlockSpec((tr, QKV_W), lambda i: (i, 0)),
        out_shape=jax.ShapeDtypeStruct((s, QKV_W), BF16),
        compiler_params=_cparams(("parallel",)),
    )(dc, dc, conv_w)


S5_W = S5_GROUPS * S5_STATE
S5_IN = S5_GROUPS * S5_GROUP
S5_TILES = 8
S5_TW, S5_TI = S5_W // S5_TILES, S5_IN // S5_TILES


def _s5_param_math(lr, li, ldt, br, bi):
    pr, pc = _iota2((S5_STATE, S5_STATE * S5_GROUP), 0), _iota2((S5_STATE, S5_STATE * S5_GROUP), 1)
    rep = jnp.where(pc // S5_GROUP == pr, 1.0, 0.0)
    dt = jnp.exp(ldt)
    mag = jnp.exp(lr * dt)
    ab_re, ab_im = mag * jnp.cos(li * dt), mag * jnp.sin(li * dt)
    den = lr * lr + li * li
    nr, ni = ab_re - 1.0, ab_im
    coef_re = (nr * lr + ni * li) / den
    coef_im = (ni * lr - nr * li) / den
    cr, ci = _dot(coef_re, rep, prec=HI), _dot(coef_im, rep, prec=HI)
    return ab_re, ab_im, cr * br - ci * bi, cr * bi + ci * br


def s5_param_fwd(lr, li, ldt, br, bi):
    def body(lr_ref, li_ref, ldt_ref, br_ref, bi_ref, ar_ref, ai_ref, bbr_ref, bbi_ref):
        res = _s5_param_math(lr_ref[...], li_ref[...], ldt_ref[...], br_ref[...], bi_ref[...])
        for r, v in zip((ar_ref, ai_ref, bbr_ref, bbi_ref), res):
            r[...] = v

    return pl.pallas_call(
        body, name="s5_param_fwd",
        out_shape=[jax.ShapeDtypeStruct(lr.shape, F32)] * 2 + [jax.ShapeDtypeStruct(br.shape, F32)] * 2,
        compiler_params=_cparams(),
    )(lr, li, ldt, br, bi)


def s5_param_bwd(lr, li, ldt, br, bi, dar, dai, dbbr, dbbi):
    def body(lr_ref, li_ref, ldt_ref, br_ref, bi_ref, dar_ref, dai_ref, dbbr_ref, dbbi_ref, *out_refs):
        _, vjp = jax.vjp(_s5_param_math, lr_ref[...], li_ref[...], ldt_ref[...], br_ref[...], bi_ref[...])
        for r, v in zip(out_refs, vjp((dar_ref[...], dai_ref[...], dbbr_ref[...], dbbi_ref[...]))):
            r[...] = v

    return pl.pallas_call(
        body, name="s5_param_bwd",
        out_shape=[jax.ShapeDtypeStruct(a.shape, F32) for a in (lr, li, ldt, br, bi)],
        compiler_params=_cparams(),
    )(lr, li, ldt, br, bi, dar, dai, dbbr, dbbi)


def _cmul(ar, ai, br, bi):
    return ar * br - ai * bi, ar * bi + ai * br


def _s5_power(ar, ai, steps):
    assert steps & (steps - 1) == 0
    for _ in range(steps.bit_length() - 1):
        ar, ai = _cmul(ar, ai, ar, ai)
    return ar, ai


def _s5_scan_rows(ar_ref, ai_ref, re_ref, im_ref, sr_ref, si_ref, tb, row0, reverse):
    quarter = S5_W // 4
    for qd in range(4):
        cs = slice(qd * quarter, (qd + 1) * quarter)
        are = jnp.broadcast_to(ar_ref[:, cs], (NSEG, quarter))
        aim = jnp.broadcast_to(ai_ref[:, cs], (NSEG, quarter))
        if reverse:
            aim = -aim

        def step(t, carry):
            h_r, h_i = carry
            tt = tb - 1 - t if reverse else t
            rows = pl.ds(pl.multiple_of(row0 + tt * NSEG, NSEG), NSEG)
            n_r = are * h_r - aim * h_i + re_ref[rows, cs]
            n_i = are * h_i + aim * h_r + im_ref[rows, cs]
            re_ref[rows, cs] = n_r
            im_ref[rows, cs] = n_i
            return n_r, n_i

        h_r, h_i = lax.fori_loop(0, tb, step, (sr_ref[:, cs], si_ref[:, cs]), unroll=4)
        sr_ref[:, cs] = h_r
        si_ref[:, cs] = h_i


def _s5_segment_carry(ar_ref, ai_ref, sr_ref, si_ref, steps, reverse):
    pr, pi = _s5_power(ar_ref[...], ai_ref[...], steps)
    if reverse:
        pi = -pi
    cur_r = jnp.zeros((1, S5_W), F32)
    cur_i = jnp.zeros((1, S5_W), F32)
    for s in (range(NSEG - 1, -1, -1) if reverse else range(NSEG)):
        e_r, e_i = sr_ref[s:s + 1, :], si_ref[s:s + 1, :]
        sr_ref[s:s + 1, :] = cur_r
        si_ref[s:s + 1, :] = cur_i
        nr, ni = _cmul(pr, pi, cur_r, cur_i)
        cur_r, cur_i = nr + e_r, ni + e_i


def _s5_blocks(s):
    steps = s // NSEG
    tb = min(32, steps)
    return steps, tb, NSEG * tb, steps // tb


def s5_scan_fwd(xp, a_re, a_im, bre, bim, cre, cim):
    s = xp.shape[0]
    steps, tb, rb, nb = _s5_blocks(s)

    def body(x_ref, ar_ref, ai_ref, bre_ref, bim_ref, cre_ref, cim_ref, y_ref, hsr_ref, hsi_ref,
             hr_ref, hi_ref, sr_ref, si_ref):
        ph, b = pl.program_id(0), pl.program_id(1)

        @pl.when((ph == 0) & (b == 0))
        def _():
            sr_ref[...] = jnp.zeros_like(sr_ref)
            si_ref[...] = jnp.zeros_like(si_ref)

        @pl.when((ph == 1) & (b == 0))
        def _():
            _s5_segment_carry(ar_ref, ai_ref, sr_ref, si_ref, steps, False)

        xv = x_ref[...].astype(BF16)
        for j in range(S5_TILES):
            xs = xv[:, j * S5_TI:(j + 1) * S5_TI]
            hr_ref[:, j * S5_TW:(j + 1) * S5_TW] = _dot(xs, bre_ref[j])
            hi_ref[:, j * S5_TW:(j + 1) * S5_TW] = _dot(xs, bim_ref[j])

        @pl.when(ph == 1)
        def _():
            hsr_ref[0] = sr_ref[...]
            hsi_ref[0] = si_ref[...]

        _s5_scan_rows(ar_ref, ai_ref, hr_ref, hi_ref, sr_ref, si_ref, tb, 0, False)

        @pl.when(ph == 1)
        def _():
            for j in range(S5_TILES):
                cs = slice(j * S5_TW, (j + 1) * S5_TW)
                y_ref[:, j * S5_TI:(j + 1) * S5_TI] = _dot(hr_ref[:, cs], cre_ref[j]) - _dot(hi_ref[:, cs], cim_ref[j])

    row = pl.BlockSpec((1, S5_W), lambda p, b: (0, 0))
    wb = pl.BlockSpec((S5_TILES, S5_TI, S5_TW), lambda p, b: (0, 0, 0))
    wc = pl.BlockSpec((S5_TILES, S5_TW, S5_TI), lambda p, b: (0, 0, 0))
    st = pl.BlockSpec((1, NSEG, S5_W), lambda p, b: (p * b, 0, 0))
    return pl.pallas_call(
        body, name="s5_scan_fwd", grid=(2, nb),
        in_specs=[pl.BlockSpec((rb, S5_IN), lambda p, b: (b, 0)), row, row, wb, wb, wc, wc],
        out_specs=[pl.BlockSpec((rb, S5_IN), lambda p, b: (p * b, 0)), st, st],
        out_shape=[jax.ShapeDtypeStruct((s, S5_IN), F32)] + [jax.ShapeDtypeStruct((nb, NSEG, S5_W), F32)] * 2,
        scratch_shapes=[pltpu.VMEM((rb, S5_W), F32)] * 2 + [pltpu.VMEM((NSEG, S5_W), F32)] * 2,
        compiler_params=_cparams(("arbitrary", "arbitrary")),
    )(xp, a_re, a_im, bre, bim, cre, cim)


def s5_scan_bwd(dyp, xp, a_re, a_im, bre, bim, cre_t, cim_t, hs_r, hs_i):
    s = xp.shape[0]
    steps, tb, rb, nb = _s5_blocks(s)

    def body(dy_ref, x_ref, ar_ref, ai_ref, bre_ref, bim_ref, crt_ref, cit_ref, hsr_ref, hsi_ref,
             dx_ref, dar_ref, dai_ref, dbr_ref, dbi_ref, dcr_ref, dci_ref,
             hr_ref, hi_ref, lr_ref, li_ref, sr_ref, si_ref, fr_ref, fi_ref, accr_ref, acci_ref):
        ph, b = pl.program_id(0), pl.program_id(1)

        @pl.when((ph == 0) & (b == 0))
        def _():
            sr_ref[...] = jnp.zeros_like(sr_ref)
            si_ref[...] = jnp.zeros_like(si_ref)

        @pl.when((ph == 1) & (b == 0))
        def _():
            _s5_segment_carry(ar_ref, ai_ref, sr_ref, si_ref, steps, True)
            for r in (accr_ref, acci_ref, dbr_ref, dbi_ref, dcr_ref, dci_ref):
                r[...] = jnp.zeros_like(r)

        dyv = dy_ref[...].astype(BF16)
        for j in range(S5_TILES):
            ds_ = dyv[:, j * S5_TI:(j + 1) * S5_TI]
            lr_ref[:, j * S5_TW:(j + 1) * S5_TW] = _dot(ds_, crt_ref[j])
            li_ref[:, j * S5_TW:(j + 1) * S5_TW] = -_dot(ds_, cit_ref[j])
        _s5_scan_rows(ar_ref, ai_ref, lr_ref, li_ref, sr_ref, si_ref, tb, 0, True)

        @pl.when(ph == 1)
        def _():
            xv = x_ref[...].astype(BF16)
            for j in range(S5_TILES):
                xs = xv[:, j * S5_TI:(j + 1) * S5_TI]
                hr_ref[NSEG:, j * S5_TW:(j + 1) * S5_TW] = _dot(xs, bre_ref[j])
                hi_ref[NSEG:, j * S5_TW:(j + 1) * S5_TW] = _dot(xs, bim_ref[j])
            hr_ref[0:NSEG, :] = hsr_ref[0]
            hi_ref[0:NSEG, :] = hsi_ref[0]
            fr_ref[...] = hsr_ref[0]
            fi_ref[...] = hsi_ref[0]
            _s5_scan_rows(ar_ref, ai_ref, hr_ref, hi_ref, fr_ref, fi_ref, tb, NSEG, False)
            lam_r, lam_i = lr_ref[...], li_ref[...]
            hp_r, hp_i = hr_ref[0:rb, :], hi_ref[0:rb, :]
            accr_ref[...] += jnp.sum((lam_r * hp_r + lam_i * hp_i).reshape(tb, NSEG, S5_W), axis=0)
            acci_ref[...] += jnp.sum((lam_i * hp_r - lam_r * hp_i).reshape(tb, NSEG, S5_W), axis=0)
            lam_rb, lam_ib = lam_r.astype(BF16), lam_i.astype(BF16)
            h_rb, h_ib = hr_ref[NSEG:, :].astype(BF16), hi_ref[NSEG:, :].astype(BF16)
            for j in range(S5_TILES):
                cs, ci = slice(j * S5_TW, (j + 1) * S5_TW), slice(j * S5_TI, (j + 1) * S5_TI)
                dbr_ref[j] += _dot(xv[:, ci], lam_rb[:, cs], "tn")
                dbi_ref[j] += _dot(xv[:, ci], lam_ib[:, cs], "tn")
                dx_ref[:, ci] = _dot(lam_rb[:, cs], bre_ref[j], "nt") + _dot(lam_ib[:, cs], bim_ref[j], "nt")
                dcr_ref[j] += _dot(h_rb[:, cs], dyv[:, ci], "tn")
                dci_ref[j] -= _dot(h_ib[:, cs], dyv[:, ci], "tn")

        @pl.when((ph == 1) & (b == nb - 1))
        def _():
            dar_ref[...] = jnp.sum(accr_ref[...], axis=0, keepdims=True)
            dai_ref[...] = jnp.sum(acci_ref[...], axis=0, keepdims=True)

    rev = lambda p, b: (nb - 1 - b, 0)
    row = pl.BlockSpec((1, S5_W), lambda p, b: (0, 0))
    wb = pl.BlockSpec((S5_TILES, S5_TI, S5_TW), lambda p, b: (0, 0, 0))
    wc = pl.BlockSpec((S5_TILES, S5_TW, S5_TI), lambda p, b: (0, 0, 0))
    st = pl.BlockSpec((1, NSEG, S5_W), lambda p, b: (nb - 1 - b, 0, 0))
    big = pltpu.VMEM((rb, S5_W), F32)
    big8 = pltpu.VMEM((rb + NSEG, S5_W), F32)
    small = pltpu.VMEM((NSEG, S5_W), F32)
    return pl.pallas_call(
        body, name="s5_scan_bwd", grid=(2, nb),
        in_specs=[pl.BlockSpec((rb, S5_IN), rev), pl.BlockSpec((rb, S5_IN), rev), row, row, wb, wb, wb, wb, st, st],
        out_specs=[pl.BlockSpec((rb, S5_IN), lambda p, b: (nb - 1 - p * b, 0)), row, row, wb, wb, wc, wc],
        out_shape=[jax.ShapeDtypeStruct((s, S5_IN), F32)] + [jax.ShapeDtypeStruct((1, S5_W), F32)] * 2
        + [jax.ShapeDtypeStruct((S5_TILES, S5_TI, S5_TW), F32)] * 2 + [jax.ShapeDtypeStruct((S5_TILES, S5_TW, S5_TI), F32)] * 2,
        scratch_shapes=[big8, big8, big, big, small, small, small, small, small, small],
        compiler_params=_cparams(("arbitrary", "arbitrary")),
    )(dyp, xp, a_re, a_im, bre, bim, cre_t, cim_t, hs_r, hs_i)


XA_DIM = 256
XA_W = XA_HEADS * XA_DIM


def _xa_probs(qh, kh):
    sc = _dot(qh, kh, "nt") * (XA_DIM ** -0.5)
    ex = jnp.exp(sc - jnp.max(sc, axis=-1, keepdims=True))
    return ex / jnp.sum(ex, axis=-1, keepdims=True)


def xa_fwd(proj, kv):
    s = proj.shape[0]
    tq = min(512, s)

    def body(q_ref, kv_ref, o_ref):
        for h in range(XA_HEADS):
            sl = slice(h * XA_DIM, (h + 1) * XA_DIM)
            p = _xa_probs(q_ref[:, sl], kv_ref[:, sl])
            o_ref[:, sl] = _dot(p, kv_ref[:, XA_W + h * XA_DIM:XA_W + (h + 1) * XA_DIM])

    return pl.pallas_call(
        body, name="xa_fwd", grid=(s // tq,),
        in_specs=[pl.BlockSpec((tq, XA_W), lambda i: (i, QC_CB)), pl.BlockSpec(kv.shape, lambda i: (0, 0))],
        out_specs=pl.BlockSpec((tq, XA_W), lambda i: (i, 0)),
        out_shape=jax.ShapeDtypeStruct((s, XA_W), F32),
        compiler_params=_cparams(("parallel",)),
    )(proj, kv)


def xa_bwd(do, proj, kv):
    s = proj.shape[0]
    tq = min(512, s)

    def body(do_ref, q_ref, kv_ref, dq_ref, dkv_ref):
        @pl.when(pl.program_id(0) == 0)
        def _():
            dkv_ref[...] = jnp.zeros_like(dkv_ref)

        for h in range(XA_HEADS):
            sl = slice(h * XA_DIM, (h + 1) * XA_DIM)
            sv = slice(XA_W + h * XA_DIM, XA_W + (h + 1) * XA_DIM)
            qh, kh, vh, doh = q_ref[:, sl], kv_ref[:, sl], kv_ref[:, sv], do_ref[:, sl]
            p = _xa_probs(qh, kh)
            dp = _dot(doh, vh, "nt")
            ds_ = p * (dp - jnp.sum(dp * p, axis=-1, keepdims=True)) * (XA_DIM ** -0.5)
            dq_ref[:, sl] = _dot(ds_, kh).astype(BF16)
            dkv_ref[:, sl] += _dot(ds_, qh, "tn")
            dkv_ref[:, sv] += _dot(p, doh, "tn")

    return pl.pallas_call(
        body, name="xa_bwd", grid=(s // tq,),
        in_specs=[pl.BlockSpec((tq, XA_W), lambda i: (i, 0)), pl.BlockSpec((tq, XA_W), lambda i: (i, QC_CB)),
                  pl.BlockSpec(kv.shape, lambda i: (0, 0))],
        out_specs=[pl.BlockSpec((tq, XA_W), lambda i: (i, 0)), pl.BlockSpec(kv.shape, lambda i: (0, 0))],
        out_shape=[jax.ShapeDtypeStruct((s, XA_W), BF16), jax.ShapeDtypeStruct(kv.shape, F32)],
        compiler_params=_cparams(("arbitrary",)),
    )(do, proj, kv)


def adamw(w, g, m, v, name):
    lead = (0,) * (w.ndim - 2)
    rows, cols = w.shape[-2:]
    tr = rows
    while tr * cols * 4 * 7 * 2 > 36 * 2 ** 20 and tr % 16 == 0:
        tr //= 2

    def body(w_ref, g_ref, m_ref, v_ref, d_ref, m2_ref, v2_ref):
        gv = g_ref[...]
        m2 = ADAM_B1 * m_ref[...] + (1.0 - ADAM_B1) * gv
        v2 = ADAM_B2 * v_ref[...] + (1.0 - ADAM_B2) * (gv * gv)
        m_hat = m2 / (1.0 - ADAM_B1 ** ADAM_STEP)
        v_hat = v2 / (1.0 - ADAM_B2 ** ADAM_STEP)
        d_ref[...] = -ADAM_LR * (m_hat / (jnp.sqrt(v_hat) + ADAM_EPS) + ADAM_WD * w_ref[...])
        m2_ref[...] = m2
        v2_ref[...] = v2

    spec = pl.BlockSpec((1,) * len(lead) + (tr, cols), lambda i: lead + (i, 0))
    return pl.pallas_call(
        body, name=name, grid=(rows // tr,), in_specs=[spec] * 4, out_specs=[spec] * 3,
        out_shape=[jax.ShapeDtypeStruct(w.shape, F32)] * 3, compiler_params=_cparams(("parallel",)),
    )(w, g, m, v)


def _seg_perm(a):
    s, w = a.shape
    return a.reshape(NSEG, s // NSEG, w).transpose(1, 0, 2).reshape(s, w)


def _seg_unperm(a):
    s, w = a.shape
    return a.reshape(s // NSEG, NSEG, w).transpose(1, 0, 2).reshape(s, w)


def _block_diag(t):
    nt, _, r, c = t.shape
    eye = jnp.eye(8, dtype=bool)
    return jnp.where(eye[None, :, None, :, None], t[:, :, :, None, :], 0.0).reshape(nt, 8 * r, 8 * c)


def _block_diag_inv(d, r, c):
    d5 = d.reshape(d.shape[0], 8, r, 8, c)
    return jnp.diagonal(d5, axis1=1, axis2=3).transpose(0, 3, 1, 2)


def _s5_b_tiles(bb):
    return _block_diag(bb.reshape(S5_TILES, 8, S5_STATE, S5_GROUP).transpose(0, 1, 3, 2))


def _s5_b_untile(d):
    return _block_diag_inv(d, S5_GROUP, S5_STATE).transpose(0, 1, 3, 2).reshape(S5_GROUPS, S5_STATE * S5_GROUP)


def _s5_c_tiles(c):
    return _block_diag(c.reshape(S5_TILES, 8, S5_GROUP, S5_STATE).transpose(0, 1, 3, 2))


def _s5_c_untile(d):
    return _block_diag_inv(d, S5_STATE, S5_GROUP).transpose(0, 1, 3, 2).reshape(S5_GROUPS, S5_GROUP, S5_STATE)


def s5_ssm_fwd(xb, lam_re, lam_im, log_dt, b_re, b_im, c_re, c_im):
    br, bi = b_re.reshape(S5_GROUPS, -1), b_im.reshape(S5_GROUPS, -1)
    ldt = log_dt.reshape(S5_GROUPS, 1)
    ab_re, ab_im, bb_re, bb_im = s5_param_fwd(lam_re, lam_im, ldt, br, bi)
    a_re, a_im = ab_re.reshape(1, S5_W), ab_im.reshape(1, S5_W)
    bre, bim = _s5_b_tiles(bb_re).astype(BF16), _s5_b_tiles(bb_im).astype(BF16)
    cre, cim = _s5_c_tiles(c_re).astype(BF16), _s5_c_tiles(c_im).astype(BF16)
    xp = _seg_perm(xb)
    yp, hs_r, hs_i = s5_scan_fwd(xp, a_re, a_im, bre, bim, cre, cim)
    saved = (xp, a_re, a_im, bre, bim, cre, cim, hs_r, hs_i, (lam_re, lam_im, ldt, br, bi))
    return _seg_unperm(yp), saved


def s5_ssm_bwd(dy, saved):
    xp, a_re, a_im, bre, bim, cre, cim, hs_r, hs_i, params = saved
    cre_t, cim_t = cre.transpose(0, 2, 1), cim.transpose(0, 2, 1)
    dxp, dar, dai, dbr, dbi, dcr, dci = s5_scan_bwd(_seg_perm(dy), xp, a_re, a_im, bre, bim, cre_t, cim_t, hs_r, hs_i)
    dlr, dli, dldt, db_re, db_im = s5_param_bwd(*params, dar.reshape(S5_GROUPS, S5_STATE), dai.reshape(S5_GROUPS, S5_STATE),
                                                _s5_b_untile(dbr), _s5_b_untile(dbi))
    shape_b = (S5_GROUPS, S5_STATE, S5_GROUP)
    return (_seg_unperm(dxp), dlr, dli, dldt.reshape(S5_GROUPS), db_re.reshape(shape_b), db_im.reshape(shape_b),
            _s5_c_untile(dcr), _s5_c_untile(dci))


_MESH = pl.DeviceIdType.MESH
_HBM = pl.BlockSpec(memory_space=pltpu.HBM)
N_DEV = 8


def _position():
    return lax.axis_index("x"), lax.axis_index("y"), lax.axis_index("c")


D2D_CHUNK_BYTES = 2 ** 20


def _chunk_rows(rows, cols, itemsize):
    return _row_tile(rows, 16, max(16, D2D_CHUNK_BYTES // (cols * itemsize)))


def _rows(start, size, unit=16):
    return pl.ds(pl.multiple_of(start, unit), size)


def _push_to_sibling(chunks, stages, recv_sems, store_sems, sibling, lag=2):
    in_slot, used, stores = {}, {}, []

    def push(q, slot):
        _, _, sid, land, _ = chunks[q]
        buf, send_sems, _ = stages[sid]
        return pltpu.make_async_remote_copy(src_ref=buf.at[slot], dst_ref=land, send_sem=send_sems.at[slot],
                                            recv_sem=recv_sems.at[q], device_id=sibling, device_id_type=_MESH)

    def receive(q):
        push(q, 0).wait_recv()
        st = pltpu.make_async_copy(chunks[q][3], chunks[q][4], store_sems.at[q])
        st.start()
        stores.append(st)

    for q, (pre, src, sid, _, _) in enumerate(chunks):
        if pre is not None:
            pre()
        slot = used.get(sid, 0) % 2
        used[sid] = used.get(sid, 0) + 1
        if (sid, slot) in in_slot:
            in_slot.pop((sid, slot)).wait_send()
        load = pltpu.make_async_copy(src, stages[sid][0].at[slot], stages[sid][2].at[slot])
        load.start()
        load.wait()
        cp = push(q, slot)
        cp.start()
        in_slot[(sid, slot)] = cp
        if q >= lag:
            receive(q - lag)
    for q in range(max(0, len(chunks) - lag), len(chunks)):
        receive(q)
    for cp in in_slot.values():
        cp.wait_send()
    for st in stores:
        st.wait()


def _stage_scratch(shapes_dtypes):
    out = []
    for shape, dtype in shapes_dtypes:
        out += [pltpu.VMEM((2,) + shape, dtype), pltpu.SemaphoreType.DMA((2,)), pltpu.SemaphoreType.DMA((2,))]
    return out


def allgather_weights(ws, convw, name):
    n = len(ws)
    extra = 0 if convw is None else 1
    halves = [w.shape[0] // 2 for w in ws]
    steps = [_chunk_rows(h, w.shape[1], w.dtype.itemsize) for h, w in zip(halves, ws)]
    per_peer = [h // s for h, s in zip(halves, steps)]
    nchunks = 3 * sum(per_peer)

    def body(*refs):
        w_refs = refs[:n]
        wo_refs = refs[n + extra:2 * n + extra]
        scratch = refs[2 * (n + extra):]
        send_sems, recv_sems, local_sems, fwd_recv_sems, store_sems = scratch[:5]
        lands = scratch[5:5 + n]
        stage_refs = scratch[5 + n:]
        stages = [tuple(stage_refs[3 * i:3 * i + 3]) for i in range(n)]
        x, y, c = _position()
        mine = 2 * x + y
        peers = [(1 - x, y), (x, 1 - y), (1 - x, 1 - y)]
        blocks = [2 * px + py for px, py in peers]
        local = [pltpu.make_async_copy(w_refs[i], wo_refs[i].at[mine], local_sems.at[i]) for i in range(n)]
        if extra:
            c_ref, co_ref = refs[n], refs[2 * n + 1]
            local.append(pltpu.make_async_copy(c_ref, co_ref.at[mine], local_sems.at[n]))
        for cp in local:
            cp.start()

        def ici(i, k, block):
            rows = _rows(c * halves[i], halves[i])
            return pltpu.make_async_remote_copy(src_ref=w_refs[i].at[rows, :], dst_ref=wo_refs[i].at[block, rows, :],
                                                send_sem=send_sems.at[3 * i + k], recv_sem=recv_sems.at[3 * i + k],
                                                device_id=(*peers[k], c), device_id_type=_MESH)

        def conv(k, block):
            return pltpu.make_async_remote_copy(src_ref=c_ref, dst_ref=co_ref.at[block], send_sem=send_sems.at[3 * n + k],
                                                recv_sem=recv_sems.at[3 * n + k], device_id=(*peers[k], c), device_id_type=_MESH)

        sends = [ici(i, k, mine) for k in range(3) for i in range(n)] + ([conv(k, mine) for k in range(3)] if extra else [])
        for cp in sends:
            cp.start()
        chunks = []
        for k in range(3):
            for i in range(n):
                for q in range(per_peer[i]):
                    pre = functools.partial(lambda i, k: ici(i, k, blocks[k]).wait_recv(), i, k) if q == 0 else None
                    src = wo_refs[i].at[blocks[k], _rows(c * halves[i] + q * steps[i], steps[i]), :]
                    out = wo_refs[i].at[blocks[k], _rows((1 - c) * halves[i] + q * steps[i], steps[i]), :]
                    chunks.append((pre, src, i, lands[i].at[k * per_peer[i] + q], out))
        _push_to_sibling(chunks, stages, fwd_recv_sems, store_sems, (x, y, 1 - c))
        if extra:
            for k in range(3):
                conv(k, blocks[k]).wait_recv()
        for cp in sends:
            cp.wait_send()
        for cp in local:
            cp.wait()

    nsem = 3 * (n + extra)
    scratch = [pltpu.SemaphoreType.DMA((nsem,)), pltpu.SemaphoreType.DMA((nsem,)), pltpu.SemaphoreType.DMA((n + extra,)),
               pltpu.SemaphoreType.DMA((nchunks,)), pltpu.SemaphoreType.DMA((nchunks,))]
    scratch += [pltpu.VMEM((3 * p, s, w.shape[1]), w.dtype) for p, s, w in zip(per_peer, steps, ws)]
    scratch += _stage_scratch([((s, w.shape[1]), w.dtype) for s, w in zip(steps, ws)])
    operands = list(ws) + ([convw] if extra else [])
    return pl.pallas_call(
        body, name=name, in_specs=[_HBM] * len(operands), out_specs=[_HBM] * len(operands),
        out_shape=[jax.ShapeDtypeStruct((4,) + w.shape, w.dtype) for w in operands],
        scratch_shapes=scratch, compiler_params=pltpu.CompilerParams(vmem_limit_bytes=VMEM_LIMIT),
    )(*operands)


def exchange_cores(gs, small, name):
    n = len(gs)
    extra = 0 if small is None else 1
    halves = [g.shape[1] // 2 for g in gs]
    steps = [_chunk_rows(h, g.shape[2], g.dtype.itemsize) for h, g in zip(halves, gs)]
    per_shard = [h // s for h, s in zip(halves, steps)]
    nchunks = 4 * sum(per_shard)

    def body(*refs):
        g_refs = refs[:n]
        got_refs = refs[n + extra:2 * n + extra]
        scratch = refs[2 * (n + extra):]
        recv_sems, store_sems = scratch[:2]
        lands = scratch[2:2 + n]
        stage_refs = scratch[2 + n:2 + 4 * n]
        stages = [tuple(stage_refs[3 * i:3 * i + 3]) for i in range(n)]
        x, y, c = _position()
        if extra:
            s_ref, so_ref = refs[n], refs[2 * n + 1]
            tiny_send, tiny_recv, tiny_local = scratch[2 + 4 * n:]
            me = 4 * x + 2 * y + c
            local = pltpu.make_async_copy(s_ref, so_ref.at[me], tiny_local)
            local.start()

            def tiny(r, sending):
                px, py, pc = (1 - x if r & 4 else x, 1 - y if r & 2 else y, 1 - c if r & 1 else c)
                slot = me if sending else 4 * px + 2 * py + pc
                return pltpu.make_async_remote_copy(src_ref=s_ref, dst_ref=so_ref.at[slot], send_sem=tiny_send.at[r - 1],
                                                    recv_sem=tiny_recv.at[r - 1], device_id=(px, py, pc), device_id_type=_MESH)

            sends = [tiny(r, True) for r in range(1, N_DEV)]
            for cp in sends:
                cp.start()
        chunks = []
        for i in range(n):
            for j in range(4):
                for q in range(per_shard[i]):
                    src = g_refs[i].at[j, _rows((1 - c) * halves[i] + q * steps[i], steps[i]), :]
                    out = got_refs[i].at[j, pl.ds(q * steps[i], steps[i]), :]
                    chunks.append((None, src, i, lands[i].at[j * per_shard[i] + q], out))
        _push_to_sibling(chunks, stages, recv_sems, store_sems, (x, y, 1 - c))
        if extra:
            for r in range(1, N_DEV):
                tiny(r, False).wait_recv()
            for cp in sends:
                cp.wait_send()
            local.wait()

    scratch = [pltpu.SemaphoreType.DMA((nchunks,)), pltpu.SemaphoreType.DMA((nchunks,))]
    scratch += [pltpu.VMEM((4 * p, s, g.shape[2]), g.dtype) for p, s, g in zip(per_shard, steps, gs)]
    scratch += _stage_scratch([((s, g.shape[2]), g.dtype) for s, g in zip(steps, gs)])
    out_shape = [jax.ShapeDtypeStruct((4, h, g.shape[2]), g.dtype) for h, g in zip(halves, gs)]
    if extra:
        scratch += [pltpu.SemaphoreType.DMA((N_DEV - 1,)), pltpu.SemaphoreType.DMA((N_DEV - 1,)), pltpu.SemaphoreType.DMA]
        out_shape.append(jax.ShapeDtypeStruct((N_DEV,) + small.shape, small.dtype))
    operands = list(gs) + ([small] if extra else [])
    return pl.pallas_call(
        body, name=name, in_specs=[_HBM] * len(operands), out_specs=[_HBM] * len(out_shape), out_shape=out_shape,
        scratch_shapes=scratch, compiler_params=pltpu.CompilerParams(vmem_limit_bytes=VMEM_LIMIT),
    )(*operands)


def chips_side(cs):
    n = len(cs)

    def copies(c_refs, o_refs, sems):
        send_sems, recv_sems, local_sems = sems
        x, y, c = _position()
        mine = 2 * x + y
        peers = [(1 - x, y), (x, 1 - y), (1 - x, 1 - y)]
        blocks = [2 * px + py for px, py in peers]
        local = [pltpu.make_async_copy(c_refs[i].at[mine], o_refs[i].at[mine], local_sems.at[i]) for i in range(n)]

        def copy(i, k, sending):
            return pltpu.make_async_remote_copy(src_ref=c_refs[i].at[blocks[k]], dst_ref=o_refs[i].at[mine if sending else blocks[k]],
                                                send_sem=send_sems.at[3 * i + k], recv_sem=recv_sems.at[3 * i + k],
                                                device_id=(*peers[k], c), device_id_type=_MESH)

        sends = [copy(i, k, True) for k in range(3) for i in range(n)]
        return local, sends, lambda: [copy(i, k, False) for k in range(3) for i in range(n)]

    def start(*refs):
        local, sends, _ = copies(*refs)
        for cp in local + sends:
            cp.start()

    def finish(*refs):
        local, sends, arrivals = copies(*refs)
        for cp in arrivals():
            cp.wait_recv()
        for cp in sends:
            cp.wait_send()
        for cp in local:
            cp.wait()

    scratch = [pltpu.SemaphoreType.DMA((3 * n,)), pltpu.SemaphoreType.DMA((3 * n,)), pltpu.SemaphoreType.DMA((n,))]
    return Side(list(cs), [jax.ShapeDtypeStruct(a.shape, a.dtype) for a in cs], scratch, start, finish)


def gather_side(ws, convw):
    n = len(ws)
    halves = [w.shape[0] // 2 for w in ws]

    def copies(in_refs, out_refs, sems):
        w_refs, c_ref, wo_refs, co_ref = in_refs[:n], in_refs[n], out_refs[:n], out_refs[n]
        send_sems, recv_sems, local_sems = sems
        x, y, c = _position()
        mine = 2 * x + y
        peers = [(1 - x, y), (x, 1 - y), (1 - x, 1 - y)]
        blocks = [2 * px + py for px, py in peers]
        local = [pltpu.make_async_copy(w_refs[i], wo_refs[i].at[mine], local_sems.at[i]) for i in range(n)]
        local.append(pltpu.make_async_copy(c_ref, co_ref.at[mine], local_sems.at[n]))

        def ici(i, k, block):
            rows = _rows(c * halves[i], halves[i])
            return pltpu.make_async_remote_copy(src_ref=w_refs[i].at[rows, :], dst_ref=wo_refs[i].at[block, rows, :],
                                                send_sem=send_sems.at[3 * i + k], recv_sem=recv_sems.at[3 * i + k],
                                                device_id=(*peers[k], c), device_id_type=_MESH)

        def conv(k, block):
            return pltpu.make_async_remote_copy(src_ref=c_ref, dst_ref=co_ref.at[block], send_sem=send_sems.at[3 * n + k],
                                                recv_sem=recv_sems.at[3 * n + k], device_id=(*peers[k], c), device_id_type=_MESH)

        sends = [ici(i, k, mine) for k in range(3) for i in range(n)] + [conv(k, mine) for k in range(3)]
        return local, sends, lambda: ([ici(i, k, blocks[k]) for k in range(3) for i in range(n)]
                                      + [conv(k, blocks[k]) for k in range(3)])

    def start(*refs):
        local, sends, _ = copies(*refs)
        for cp in local + sends:
            cp.start()

    def finish(*refs):
        local, sends, arrivals = copies(*refs)
        for cp in arrivals():
            cp.wait_recv()
        for cp in sends:
            cp.wait_send()
        for cp in local:
            cp.wait()

    nsem = 3 * n + 3
    scratch = [pltpu.SemaphoreType.DMA((nsem,)), pltpu.SemaphoreType.DMA((nsem,)), pltpu.SemaphoreType.DMA((n + 1,))]
    operands = list(ws) + [convw]
    return Side(operands, [jax.ShapeDtypeStruct((4,) + w.shape, w.dtype) for w in operands], scratch, start, finish)


def forward_halves(stacked):
    n = len(stacked)
    halves = [w.shape[1] // 2 for w in stacked]
    steps = [_chunk_rows(h, w.shape[2], w.dtype.itemsize) for h, w in zip(halves, stacked)]
    per_peer = [h // s for h, s in zip(halves, steps)]
    nchunks = 3 * sum(per_peer)

    def body(*refs):
        w_refs, o_refs = refs[:n], refs[n:2 * n]
        scratch = refs[2 * n:]
        recv_sems, store_sems = scratch[:2]
        lands = scratch[2:2 + n]
        stages = [tuple(scratch[2 + n + 3 * i:2 + n + 3 * i + 3]) for i in range(n)]
        x, y, c = _position()
        blocks = [2 * px + py for px, py in ((1 - x, y), (x, 1 - y), (1 - x, 1 - y))]
        chunks = []
        for k in range(3):
            for i in range(n):
                for q in range(per_peer[i]):
                    src = w_refs[i].at[blocks[k], _rows(c * halves[i] + q * steps[i], steps[i]), :]
                    out = o_refs[i].at[blocks[k], _rows((1 - c) * halves[i] + q * steps[i], steps[i]), :]
                    chunks.append((None, src, i, lands[i].at[k * per_peer[i] + q], out))
        _push_to_sibling(chunks, stages, recv_sems, store_sems, (x, y, 1 - c))

    scratch = [pltpu.SemaphoreType.DMA((nchunks,)), pltpu.SemaphoreType.DMA((nchunks,))]
    scratch += [pltpu.VMEM((3 * p, s, w.shape[2]), w.dtype) for p, s, w in zip(per_peer, steps, stacked)]
    scratch += _stage_scratch([((s, w.shape[2]), w.dtype) for s, w in zip(steps, stacked)])
    return pl.pallas_call(
        body, name="forward_halves", in_specs=[_HBM] * n, out_specs=[_HBM] * n,
        out_shape=[jax.ShapeDtypeStruct(w.shape, w.dtype) for w in stacked], input_output_aliases={i: i for i in range(n)},
        scratch_shapes=scratch, compiler_params=pltpu.CompilerParams(vmem_limit_bytes=VMEM_LIMIT),
    )(*stacked)


def exchange_small(small):
    def body(s_ref, so_ref, send_sems, recv_sems, local_sem):
        x, y, c = _position()
        me = 4 * x + 2 * y + c
        local = pltpu.make_async_copy(s_ref, so_ref.at[me], local_sem)
        local.start()

        def copy(r, sending):
            px, py, pc = (1 - x if r & 4 else x, 1 - y if r & 2 else y, 1 - c if r & 1 else c)
            slot = me if sending else 4 * px + 2 * py + pc
            return pltpu.make_async_remote_copy(src_ref=s_ref, dst_ref=so_ref.at[slot], send_sem=send_sems.at[r - 1],
                                                recv_sem=recv_sems.at[r - 1], device_id=(px, py, pc), device_id_type=_MESH)

        sends = [copy(r, True) for r in range(1, N_DEV)]
        for cp in sends:
            cp.start()
        for r in range(1, N_DEV):
            copy(r, False).wait_recv()
        for cp in sends:
            cp.wait_send()
        local.wait()

    return pl.pallas_call(
        body, name="exchange_small", in_specs=[_HBM], out_specs=_HBM,
        out_shape=jax.ShapeDtypeStruct((N_DEV,) + small.shape, small.dtype),
        scratch_shapes=[pltpu.SemaphoreType.DMA((N_DEV - 1,)), pltpu.SemaphoreType.DMA((N_DEV - 1,)), pltpu.SemaphoreType.DMA],
    )(small)


def pair_sum(core, g, got, name):
    nb, rows, cols = got.shape
    tr = _row_tile(rows, 16, max(16, (2 * 2 ** 20) // (cols * g.dtype.itemsize)))
    nblk = rows // tr

    def body(c_ref, a_ref, b_ref, o_ref):
        o_ref[...] = (a_ref[...].astype(F32) + b_ref[...].astype(F32)).astype(o_ref.dtype)

    spec = pl.BlockSpec((1, tr, cols), lambda j, i, c_ref: (j, i, 0))
    mine = pl.BlockSpec((1, tr, cols), lambda j, i, c_ref: (j, c_ref[0] * nblk + i, 0))
    return pl.pallas_call(
        body, name=name,
        grid_spec=pltpu.PrefetchScalarGridSpec(num_scalar_prefetch=1, grid=(nb, nblk), in_specs=[mine, spec], out_specs=spec),
        out_shape=jax.ShapeDtypeStruct(got.shape, g.dtype), compiler_params=_cparams(("parallel", "parallel")),
    )(core, g, got)


def sum_chips(core, pieces, name):
    nb, rows, cols = pieces.shape
    tr = _row_tile(rows, 16, max(16, (6 * 2 ** 20) // (nb * cols * pieces.dtype.itemsize)))
    nblk = rows // tr

    def body(c_ref, p_ref, o_ref):
        acc = p_ref[0].astype(F32)
        for i in range(1, nb):
            acc = acc + p_ref[i].astype(F32)
        o_ref[0] = acc

    return pl.pallas_call(
        body, name=name,
        grid_spec=pltpu.PrefetchScalarGridSpec(
            num_scalar_prefetch=1, grid=(nblk,),
            in_specs=[pl.BlockSpec((nb, tr, cols), lambda i, c_ref: (0, i, 0))],
            out_specs=pl.BlockSpec((1, tr, cols), lambda i, c_ref: (0, c_ref[0] * nblk + i, 0))),
        out_shape=jax.ShapeDtypeStruct((1, 2 * rows, cols), F32), compiler_params=_cparams(("parallel",)),
    )(core, pieces)


def sibling_exchange(fulls):
    n = len(fulls)
    halves = [f.shape[1] // 2 for f in fulls]
    steps = [_chunk_rows(h, f.shape[2], f.dtype.itemsize) for h, f in zip(halves, fulls)]
    counts = [h // s for h, s in zip(halves, steps)]
    nchunks = sum(counts)

    def body(*refs):
        f_refs, o_refs = refs[:n], refs[n:2 * n]
        scratch = refs[2 * n:]
        recv_sems, store_sems = scratch[:2]
        lands = scratch[2:2 + n]
        stages = [tuple(scratch[2 + n + 3 * i:2 + n + 3 * i + 3]) for i in range(n)]
        x, y, c = _position()
        chunks = []
        for i in range(n):
            for q in range(counts[i]):
                src = f_refs[i].at[0, _rows(c * halves[i] + q * steps[i], steps[i]), :]
                out = o_refs[i].at[0, _rows((1 - c) * halves[i] + q * steps[i], steps[i]), :]
                chunks.append((None, src, i, lands[i].at[q], out))
        _push_to_sibling(chunks, stages, recv_sems, store_sems, (x, y, 1 - c))

    scratch = [pltpu.SemaphoreType.DMA((nchunks,)), pltpu.SemaphoreType.DMA((nchunks,))]
    scratch += [pltpu.VMEM((k, s, f.shape[2]), f.dtype) for k, s, f in zip(counts, steps, fulls)]
    scratch += _stage_scratch([((s, f.shape[2]), f.dtype) for s, f in zip(steps, fulls)])
    return pl.pallas_call(
        body, name="sibling_exchange", in_specs=[_HBM] * n, out_specs=[_HBM] * n,
        out_shape=[jax.ShapeDtypeStruct(f.shape, f.dtype) for f in fulls],
        input_output_aliases={i: i for i in range(n)},
        scratch_shapes=scratch, compiler_params=pltpu.CompilerParams(vmem_limit_bytes=VMEM_LIMIT),
    )(*fulls)


def _row_tile(rows, unit, max_rows):
    best = unit
    for t in range(unit, min(rows, max_rows) + 1, unit):
        if rows % t == 0:
            best = t
    return best


def sum_pieces(pieces, name):
    n, rows, cols = pieces.shape
    tr = _row_tile(rows, 16, max(16, (6 * 2 ** 20) // (n * cols * pieces.dtype.itemsize)))

    def body(p_ref, o_ref):
        acc = p_ref[0].astype(F32)
        for i in range(1, n):
            acc = acc + p_ref[i].astype(F32)
        o_ref[...] = acc

    return pl.pallas_call(
        body, name=name, grid=(rows // tr,),
        in_specs=[pl.BlockSpec((n, tr, cols), lambda i: (0, i, 0))], out_specs=pl.BlockSpec((tr, cols), lambda i: (i, 0)),
        out_shape=jax.ShapeDtypeStruct((rows, cols), F32), compiler_params=_cparams(("parallel",)),
    )(pieces)


BIG = ("w_in", "s5_w_glu", "w_kv_mem", "w_br_a", "w_br_b", "w_br_c", "w_out")
COL_SHARDED = ("w_in", "s5_w_glu", "w_br_a", "w_br_b", "w_br_c")
SMALL = ("norm_g", "gdn_a_log", "gdn_dt_bias", "gdn_norm_g", "s5_lambda_re", "s5_lambda_im", "s5_log_dt",
         "s5_b_re", "s5_b_im", "s5_c_re", "s5_c_im", "s5_d", "mem_norm_g", "final_g")
WEIGHTS = ("norm_g", "w_in", "conv_w", "gdn_a_log", "gdn_dt_bias", "gdn_norm_g", "s5_lambda_re", "s5_lambda_im",
           "s5_log_dt", "s5_b_re", "s5_b_im", "s5_c_re", "s5_c_im", "s5_d", "s5_w_glu", "mem_norm_g", "w_kv_mem",
           "w_br_a", "w_br_b", "w_br_c", "w_out", "final_g")
W_IN_SPLIT = 4096


W_IN_COLS = PROJ_W - BA_PAD + 2 * NHEAD


def _pack_w_in(shards):
    cs = shards.shape[2]
    parts = []
    for a, b in ((0, W_IN_SPLIT), (W_IN_SPLIT + 2 * NHEAD, W_IN_COLS), (W_IN_SPLIT, W_IN_SPLIT + 2 * NHEAD)):
        while a < b:
            j = a // cs
            hi = min(b, (j + 1) * cs)
            parts.append(shards[j, :, a - j * cs:hi - j * cs])
            a = hi
    parts.append(jnp.zeros((shards.shape[1], BA_PAD - 2 * NHEAD), shards.dtype))
    return jnp.concatenate(parts, axis=1)


def _unpack_w_in(wp):
    cs = W_IN_COLS // 4
    moves = ((0, W_IN_SPLIT, 0), (W_IN_SPLIT, W_IN_SPLIT + 2 * NHEAD, PROJ_W - BA_PAD - W_IN_SPLIT),
             (W_IN_SPLIT + 2 * NHEAD, W_IN_COLS, -2 * NHEAD))
    shards = []
    for j in range(4):
        parts = []
        for lo, hi, shift in moves:
            s, e = max(j * cs, lo), min((j + 1) * cs, hi)
            if s < e:
                parts.append(wp[:, s + shift:e + shift])
        shards.append(jnp.concatenate(parts, axis=1))
    return jnp.stack(shards)


def _pack_small(arrs):
    parts = []
    for a in arrs:
        f = a.reshape(-1).astype(F32)
        parts.append(jnp.pad(f, (0, (-f.shape[0]) % 128)))
    flat = jnp.concatenate(parts)
    rows = flat.shape[0] // 128
    return jnp.pad(flat.reshape(rows, 128), ((0, (-rows) % 16), (0, 0)))


def _unpack_small(flat2d, shapes):
    f = flat2d.reshape(-1)
    out, off = [], 0
    for shp in shapes:
        n = math.prod(shp)
        out.append(f[off:off + n].reshape(shp))
        off += n + (-n) % 128
    return out


def kernel(x, mem, norm_g, w_in, conv_w, gdn_a_log, gdn_dt_bias, gdn_norm_g, s5_lambda_re, s5_lambda_im, s5_log_dt, s5_b_re, s5_b_im, s5_c_re, s5_c_im, s5_d, s5_w_glu, mem_norm_g, w_kv_mem, w_br_a, w_br_b, w_br_c, w_out, final_g, loss_target, m_norm_g, m_w_in, m_conv_w, m_gdn_a_log, m_gdn_dt_bias, m_gdn_norm_g, m_s5_lambda_re, m_s5_lambda_im, m_s5_log_dt, m_s5_b_re, m_s5_b_im, m_s5_c_re, m_s5_c_im, m_s5_d, m_s5_w_glu, m_mem_norm_g, m_w_kv_mem, m_w_br_a, m_w_br_b, m_w_br_c, m_w_out, m_final_g, v_norm_g, v_w_in, v_conv_w, v_gdn_a_log, v_gdn_dt_bias, v_gdn_norm_g, v_s5_lambda_re, v_s5_lambda_im, v_s5_log_dt, v_s5_b_re, v_s5_b_im, v_s5_c_re, v_s5_c_im, v_s5_d, v_s5_w_glu, v_mem_norm_g, v_w_kv_mem, v_w_br_a, v_w_br_b, v_w_br_c, v_w_out, v_final_g):
    wts = dict(norm_g=norm_g, w_in=w_in, conv_w=conv_w, gdn_a_log=gdn_a_log, gdn_dt_bias=gdn_dt_bias, gdn_norm_g=gdn_norm_g,
               s5_lambda_re=s5_lambda_re, s5_lambda_im=s5_lambda_im, s5_log_dt=s5_log_dt, s5_b_re=s5_b_re, s5_b_im=s5_b_im,
               s5_c_re=s5_c_re, s5_c_im=s5_c_im, s5_d=s5_d, s5_w_glu=s5_w_glu, mem_norm_g=mem_norm_g, w_kv_mem=w_kv_mem,
               w_br_a=w_br_a, w_br_b=w_br_b, w_br_c=w_br_c, w_out=w_out, final_g=final_g)
    mom = dict(norm_g=m_norm_g, w_in=m_w_in, conv_w=m_conv_w, gdn_a_log=m_gdn_a_log, gdn_dt_bias=m_gdn_dt_bias,
               gdn_norm_g=m_gdn_norm_g, s5_lambda_re=m_s5_lambda_re, s5_lambda_im=m_s5_lambda_im, s5_log_dt=m_s5_log_dt,
               s5_b_re=m_s5_b_re, s5_b_im=m_s5_b_im, s5_c_re=m_s5_c_re, s5_c_im=m_s5_c_im, s5_d=m_s5_d, s5_w_glu=m_s5_w_glu,
               mem_norm_g=m_mem_norm_g, w_kv_mem=m_w_kv_mem, w_br_a=m_w_br_a, w_br_b=m_w_br_b, w_br_c=m_w_br_c, w_out=m_w_out,
               final_g=m_final_g)
    vel = dict(norm_g=v_norm_g, w_in=v_w_in, conv_w=v_conv_w, gdn_a_log=v_gdn_a_log, gdn_dt_bias=v_gdn_dt_bias,
               gdn_norm_g=v_gdn_norm_g, s5_lambda_re=v_s5_lambda_re, s5_lambda_im=v_s5_lambda_im, s5_log_dt=v_s5_log_dt,
               s5_b_re=v_s5_b_re, s5_b_im=v_s5_b_im, s5_c_re=v_s5_c_re, s5_c_im=v_s5_c_im, s5_d=v_s5_d, s5_w_glu=v_s5_w_glu,
               mem_norm_g=v_mem_norm_g, w_kv_mem=v_w_kv_mem, w_br_a=v_w_br_a, w_br_b=v_w_br_b, w_br_c=v_w_br_c, w_out=v_w_out,
               final_g=v_final_g)
    x2, mem2, tgt = x[0], mem[0], loss_target[0]
    s, d = x2.shape
    n_chunks = s // CHUNK

    shards = [wts[n][0].astype(BF16) for n in BIG]
    wp = _pack_w_in(allgather_weights(shards[:1], None, "allgather_w_in")[0])
    mm = functools.partial(matmul, tm=1024, tn=1024)
    u, r1 = rms_fwd(x2, norm_g, "rms_fwd_x")
    proj, *rest, cg = mm(u, wp, mode="nn", out_dtype=F32, tk=2048, name="mm_proj", side=gather_side(shards[1:], conv_w[0]))
    full = {}
    for n, wg in zip(BIG[1:], forward_halves(rest)):
        rows, cols = wg.shape[1:]
        full[n] = wg.transpose(1, 0, 2).reshape(rows, 4 * cols) if n in COL_SHARDED else wg.reshape(4 * rows, cols)
    conv_full = cg.transpose(1, 0, 2).reshape(conv_w.shape[1], -1)
    alog_pad = jnp.zeros((1, BA_W), F32).at[0, NHEAD:2 * NHEAD].set(gdn_a_log[0])
    dt_pad = jnp.zeros((1, BA_W), F32).at[0, NHEAD:2 * NHEAD].set(gdn_dt_bias[0])

    q, k, v, bg, gcol, gt = gdn_prep_fwd(proj, conv_full, alog_pad, dt_pad)
    gt3 = gt.reshape(BA_W, n_chunks, CHUNK).transpose(1, 0, 2)
    gu, gw, qd, kd, qk, tinv = gdn_intra_fwd(q, k, v, bg, gcol, gt3)
    o_raw, states = gdn_seq_fwd(gu, gw, qd, kd, qk, gt3)
    ga = gdn_out_fwd(o_raw, proj, ZA_CB, gdn_norm_g)

    xb = proj[:, XB_CB * S5_IN:(XB_CB + 1) * S5_IN]
    y_ssm, s5_saved = s5_ssm_fwd(xb, s5_lambda_re[0], s5_lambda_im[0], s5_log_dt[0], s5_b_re[0], s5_b_im[0],
                                 s5_c_re[0], s5_c_im[0])
    yb = s5_act_fwd(y_ssm, proj, XB_CB, s5_d)
    glu = mm(yb, full["s5_w_glu"], mode="nn", out_dtype=F32, tk=1024, name="mm_glu")
    gb = s5_glu_fwd(glu, proj, ZB_CB)

    mem_n, rm = rms_fwd(mem2, mem_norm_g, "rms_fwd_mem")
    kv = mm(mem_n, full["w_kv_mem"], mode="nn", out_dtype=BF16, tk=2048, name="mm_kv")
    o_c = xa_fwd(proj, kv)
    gcx = gate_fwd(o_c, proj, ZC_CB, "gate_fwd_c")

    pa = mm(ga, full["w_br_a"], mode="nn", out_dtype=F32, tk=1024, name="mm_pa")
    pb = mm(gb, full["w_br_b"], mode="nn", out_dtype=F32, tk=1024, name="mm_pb")
    pc = mm(gcx, full["w_br_c"], mode="nn", out_dtype=F32, tk=1024, name="mm_pc")
    merged = merge_fwd(pa, pb, pc, proj, GATE_CB)
    hres = mm(merged, full["w_out"], mode="nn", out_dtype=F32, tk=2048, name="mm_out")
    dh, dhb, loss_part, d_final_g = final_stage(x2, hres, tgt, final_g.reshape(1, d))

    gfull = {}
    dmerged = mm(dhb, full["w_out"], mode="nt", out_dtype=F32, tk=2048, name="mm_dmerged")
    gfull["w_out"] = mm(merged, dhb, mode="tn", out_dtype=BF16, tk=1024, name="mm_dw_out")
    dpa, dpb, dpc, dg0, dg1, dg2 = merge_bwd(dmerged, pa, pb, pc, proj, GATE_CB)
    dga = mm(dpa, full["w_br_a"], mode="nt", out_dtype=F32, tk=2048, name="mm_dga")
    dgb = mm(dpb, full["w_br_b"], mode="nt", out_dtype=F32, tk=2048, name="mm_dgb")
    dgc = mm(dpc, full["w_br_c"], mode="nt", out_dtype=F32, tk=2048, name="mm_dgc")
    gfull["w_br_a"] = mm(ga, dpa, mode="tn", out_dtype=BF16, tk=1024, name="mm_dw_a")
    gfull["w_br_b"] = mm(gb, dpb, mode="tn", out_dtype=BF16, tk=1024, name="mm_dw_b")
    gfull["w_br_c"] = mm(gcx, dpc, mode="tn", out_dtype=BF16, tk=1024, name="mm_dw_c")

    do_raw, dza, d_gdn_norm = gdn_out_bwd(dga, o_raw, proj, ZA_CB, gdn_norm_g)
    du_, dw_, dqd, dkd, dqk, dgl = gdn_seq_bwd(do_raw, gu, gw, qd, kd, qk, gt3, states)
    dq, dk, dv, dbg = gdn_intra_bwd(q, k, v, bg, gcol, gt3, tinv, du_, dw_, dqd, dkd, dqk, dgl)
    dc, dba, dcw0, dcw1, dcw2, dcw3, d_alog, d_dt = gdn_prep_bwd1(proj, conv_full, alog_pad, dt_pad, dq, dk, dv, dbg)
    dqkv = gdn_prep_bwd2(dc, conv_full)
    d_conv = jnp.concatenate([dcw0, dcw1, dcw2, dcw3], axis=0)

    dval, dgate, dzb = s5_glu_bwd(dgb, glu, proj, ZB_CB)
    dglu = jnp.concatenate([dval, dgate], axis=1)
    dyb = mm(dglu, full["s5_w_glu"], mode="nt", out_dtype=F32, tk=2048, name="mm_dyb")
    gfull["s5_w_glu"] = mm(yb, dglu, mode="tn", out_dtype=BF16, tk=1024, name="mm_dw_glu")
    dy_ssm, dxb_direct, d_s5_d = s5_act_bwd(dyb, y_ssm, proj, XB_CB, s5_d)
    dxb_scan, d_lre, d_lim, d_ldt, d_bre, d_bim, d_cre, d_cim = s5_ssm_bwd(dy_ssm, s5_saved)
    dxb = add_cast(dxb_direct, dxb_scan, "s5_dxb")

    do_c, dzc = gate_bwd(dgc, o_c, proj, ZC_CB, "gate_bwd_c")
    dqc, dkv = xa_bwd(do_c, proj, kv)
    gfull["w_kv_mem"] = mm(mem_n, dkv, mode="tn", out_dtype=BF16, tk=256, name="mm_dw_kv")
    dmem_n = mm(dkv, full["w_kv_mem"], mode="nt", out_dtype=F32, tk=2048, name="mm_dmem")
    d_mem_norm = rms_bwd_g(dmem_n, mem2, rm, "rms_bwd_mem")

    core = lax.axis_index("c").astype(jnp.int32).reshape(1)
    by_shard = []
    for n in BIG[1:]:
        rows, cols = wts[n].shape[1:]
        g = gfull[n]
        by_shard.append(g.reshape(rows, 4, cols).transpose(1, 0, 2) if n in COL_SHARDED else g.reshape(4, rows, cols))
    got_rest = exchange_cores(by_shard, None, "exchange_cores_rest")
    chip_rest = [pair_sum(core, g, r, "sum_cores_" + n) for n, g, r in zip(BIG[1:], by_shard, got_rest)]

    dproj = jnp.concatenate([dqkv, dza, dxb, dzb, dqc, dzc, dg0, dg1, dg2, dba], axis=1)
    dwp, *from_chips_rest = matmul(u.T, dproj, mode="nn", out_dtype=BF16, tm=2048, tn=1024, tk=1024, name="mm_dw_in",
                                   side=chips_side(chip_rest))
    w_in_shards = _unpack_w_in(dwp)
    got_in, = exchange_cores([w_in_shards], None, "exchange_cores_w_in")
    chip_in = pair_sum(core, w_in_shards, got_in, "sum_cores_w_in")
    du, from_chips_in = matmul(dproj, wp, mode="nt", out_dtype=F32, tm=2048, tn=1024, tk=512, name="mm_du",
                               side=chips_side([chip_in]))
    grad_x, d_norm_g = rms_bwd_x(du, x2, r1, norm_g, dh)
    from_chips = [from_chips_in] + from_chips_rest
    small_g = dict(norm_g=d_norm_g, gdn_a_log=d_alog[:, NHEAD:2 * NHEAD], gdn_dt_bias=d_dt[:, NHEAD:2 * NHEAD],
                   gdn_norm_g=d_gdn_norm, s5_lambda_re=d_lre, s5_lambda_im=d_lim, s5_log_dt=d_ldt, s5_b_re=d_bre, s5_b_im=d_bim,
                   s5_c_re=d_cre, s5_c_im=d_cim, s5_d=d_s5_d, mem_norm_g=d_mem_norm, final_g=d_final_g)
    small_send = _pack_small([small_g[n] for n in SMALL] + [d_conv, loss_part])
    fulls = [sum_chips(core, a, "sum_chips_" + n) for n, a in zip(BIG, from_chips)]
    small_sum = sum_pieces(exchange_small(small_send), "sum_small")
    grads = dict(zip(BIG, sibling_exchange(fulls)))
    small_shapes = [wts[n].shape for n in SMALL] + [d_conv.shape, (1, 1)]
    *small_list, conv_g_full, loss_sum = _unpack_small(small_sum, small_shapes)
    grads.update(zip(SMALL, small_list))
    cw = conv_w.shape[2]
    shard_idx = 2 * lax.axis_index("x") + lax.axis_index("y")
    grads["conv_w"] = lax.dynamic_slice(conv_g_full, (0, shard_idx * cw), (conv_w.shape[1], cw))[None]

    delta, new_m, new_v = {}, {}, {}
    for n in BIG + ("conv_w",):
        delta[n], new_m[n], new_v[n] = adamw(wts[n], grads[n], mom[n], vel[n], "adamw_" + n)
    packed = [_pack_small([src[n] for n in SMALL]) for src in (wts, grads, mom, vel)]
    res = adamw(*packed, "adamw_small")
    shapes = [wts[n].shape for n in SMALL]
    for dst, flat in zip((delta, new_m, new_v), res):
        dst.update(zip(SMALL, _unpack_small(flat, shapes)))
    for n in SMALL:
        grads[n] = grads[n].reshape(wts[n].shape)

    return (loss_sum.reshape(()), grad_x.reshape(x.shape), *[grads[n] for n in WEIGHTS], *[delta[n] for n in WEIGHTS],
            *[new_m[n] for n in WEIGHTS], *[new_v[n] for n in WEIGHTS])
```

```python
import functools
import math

import jax
import jax.numpy as jnp
from jax import lax
from jax.experimental import pallas as pl
from jax.experimental.pallas import tpu as pltpu

F32 = jnp.float32
BF16 = jnp.bfloat16
HI = lax.Precision.HIGHEST

EPS = 1e-6
CHUNK = 64
HEAD = 128
NHEAD = 8
XA_HEADS = 4
S5_GROUPS = 64
S5_STATE = 64
S5_GROUP = 16
NSEG = 8
ADAM_LR, ADAM_B1, ADAM_B2, ADAM_EPS, ADAM_WD, ADAM_STEP = 0.001, 0.9, 0.999, 1e-08, 0.01, 10
VMEM_LIMIT = 56 * 2 ** 20


def _cparams(sem=None):
    return pltpu.CompilerParams(dimension_semantics=sem, vmem_limit_bytes=VMEM_LIMIT)


def _sigmoid(x):
    return 1.0 / (1.0 + jnp.exp(-x))


def _silu(x):
    return x * _sigmoid(x)


def _dsilu(x):
    s = _sigmoid(x)
    return s * (1.0 + x * (1.0 - s))


def _softplus(x):
    return jnp.maximum(x, 0.0) + jnp.log(1.0 + jnp.exp(-jnp.abs(x)))


_GELU_C = math.sqrt(2.0 / math.pi)


def _gelu(x):
    return 0.5 * x * (1.0 + jnp.tanh(_GELU_C * (x + 0.044715 * x * x * x)))


def _dgelu(x):
    t = jnp.tanh(_GELU_C * (x + 0.044715 * x * x * x))
    return 0.5 * (1.0 + t) + 0.5 * x * (1.0 - t * t) * _GELU_C * (1.0 + 3.0 * 0.044715 * x * x)


_DIMS = {"nn": (((1,), (0,)), ((), ())), "nt": (((1,), (1,)), ((), ())), "tn": (((0,), (0,)), ((), ()))}


class Side:
    def __init__(self, operands, out_shapes, scratch, start, finish):
        self.operands, self.out_shapes, self.scratch, self.start, self.finish = operands, out_shapes, scratch, start, finish


def matmul(a, b, *, mode, out_dtype, tm, tn, tk, name, side=None):
    if mode == "nn":
        (m, k), n = a.shape, b.shape[1]
    elif mode == "nt":
        (m, k), n = a.shape, b.shape[0]
    else:
        (k, m), n = a.shape, b.shape[1]
    tm, tn, tk = min(tm, m), min(tn, n), min(tk, k)
    assert m % tm == 0 and n % tn == 0 and k % tk == 0, (name, m, n, k, tm, tn, tk)
    grid = (m // tm, n // tn, k // tk)
    nk = grid[2]
    dims = _DIMS[mode]
    n_in = 0 if side is None else len(side.operands)
    n_out = 0 if side is None else len(side.out_shapes)
    n_acc = 0 if nk == 1 else 1

    def body(*refs):
        a_ref, b_ref, o_ref = refs[0], refs[1], refs[2 + n_in]
        scratch = refs[3 + n_in + n_out:]
        side_refs = (refs[2:2 + n_in], refs[3 + n_in:3 + n_in + n_out], scratch[n_acc:])
        ids = [pl.program_id(d) for d in range(3)]
        if side is not None:
            @pl.when((ids[0] == 0) & (ids[1] == 0) & (ids[2] == 0))
            def _():
                side.start(*side_refs)

        prod = lax.dot_general(a_ref[...].astype(BF16), b_ref[...].astype(BF16), dims, preferred_element_type=F32)
        if nk == 1:
            o_ref[...] = prod.astype(out_dtype)
        else:
            acc_ref = scratch[0]

            @pl.when(ids[2] == 0)
            def _():
                acc_ref[...] = prod

            @pl.when(ids[2] > 0)
            def _():
                acc_ref[...] += prod

            @pl.when(ids[2] == nk - 1)
            def _():
                o_ref[...] = acc_ref[...].astype(out_dtype)

        if side is not None:
            @pl.when((ids[0] == grid[0] - 1) & (ids[1] == grid[1] - 1) & (ids[2] == nk - 1))
            def _():
                side.finish(*side_refs)

    a_spec = pl.BlockSpec((tk, tm), lambda i, j, q: (q, i)) if mode == "tn" else pl.BlockSpec((tm, tk), lambda i, j, q: (i, q))
    b_spec = pl.BlockSpec((tn, tk), lambda i, j, q: (j, q)) if mode == "nt" else pl.BlockSpec((tk, tn), lambda i, j, q: (q, j))
    o_spec = pl.BlockSpec((tm, tn), lambda i, j, q: (i, j))
    o_shape = jax.ShapeDtypeStruct((m, n), out_dtype)
    acc = [] if nk == 1 else [pltpu.VMEM((tm, tn), F32)]
    if side is None:
        return pl.pallas_call(
            body, name=name, grid=grid, in_specs=[a_spec, b_spec], out_specs=o_spec, out_shape=o_shape, scratch_shapes=acc,
            compiler_params=_cparams(("parallel", "parallel", "arbitrary")),
        )(a, b)
    hbm = pl.BlockSpec(memory_space=pltpu.HBM)
    return pl.pallas_call(
        body, name=name, grid=grid, in_specs=[a_spec, b_spec] + [hbm] * n_in, out_specs=[o_spec] + [hbm] * n_out,
        out_shape=[o_shape] + list(side.out_shapes), scratch_shapes=acc + list(side.scratch),
        compiler_params=_cparams(("arbitrary", "arbitrary", "arbitrary")),
    )(a, b, *side.operands)


def rowwise(fn, ins, outs, *, rows, tr, name, consts=(), reds=(), into=None):
    tr = min(tr, rows)
    assert rows % tr == 0, (name, rows, tr)
    n_in, n_c, n_o = len(ins), len(consts), len(outs)
    n_buf = 0 if into is None else 1

    def body(*refs):
        vals = [r[...] for r in refs[:n_in + n_c]]
        res = fn(*vals)
        o_refs = refs[n_in + n_c + n_buf:]
        for r, v in zip(o_refs[:n_o], res[:n_o]):
            r[...] = v.astype(r.dtype)
        if reds:
            i = pl.program_id(0)

            @pl.when(i == 0)
            def _():
                for r, v in zip(o_refs[n_o:], res[n_o:]):
                    r[...] = v.astype(r.dtype)

            @pl.when(i > 0)
            def _():
                for r, v in zip(o_refs[n_o:], res[n_o:]):
                    r[...] += v.astype(r.dtype)

    in_specs = [pl.BlockSpec((tr, w), functools.partial(lambda i, cb: (i, cb), cb=cb)) for (_, w, cb) in ins]
    in_specs += [pl.BlockSpec(c.shape, lambda i: (0, 0)) for c in consts]
    out_specs = [pl.BlockSpec((tr, w), lambda i: (i, 0)) for (w, _) in outs]
    out_specs += [pl.BlockSpec(s, lambda i: (0, 0)) for (s, _) in reds]
    out_shape = [jax.ShapeDtypeStruct((rows, w), d) for (w, d) in outs]
    out_shape += [jax.ShapeDtypeStruct(s, d) for (s, d) in reds]
    operands = [a for (a, _, _) in ins] + list(consts)
    aliases = {}
    if into is not None:
        buf, pos, cb = into
        assert buf.dtype == outs[pos][1] and buf.shape[0] == rows, (name, buf.shape, buf.dtype)
        in_specs.append(pl.BlockSpec(memory_space=pl.ANY))
        operands.append(buf)
        out_specs[pos] = pl.BlockSpec((tr, outs[pos][0]), functools.partial(lambda i, cb: (i, cb), cb=cb))
        out_shape[pos] = jax.ShapeDtypeStruct(buf.shape, buf.dtype)
        aliases = {len(operands) - 1: pos}
    return pl.pallas_call(
        body, name=name, grid=(rows // tr,), in_specs=in_specs, out_specs=out_specs, out_shape=out_shape,
        input_output_aliases=aliases, compiler_params=_cparams(("arbitrary",) if reds else ("parallel",)),
    )(*operands)


def _colsum(x):
    return jnp.sum(x, axis=0, keepdims=True)


def rms_fwd(x, g, name):
    s, d = x.shape

    def fn(xv, gv):
        r = lax.rsqrt(jnp.mean(xv * xv, axis=-1, keepdims=True) + EPS)
        return xv * r * gv, r

    return rowwise(fn, [(x, d, 0)], [(d, BF16), (1, F32)], rows=s, tr=256, name=name, consts=[g])


def rms_bwd_x(du, x, r, g, dh):
    s, d = x.shape

    def fn(duv, xv, rv, dhv, gv):
        dyg = duv * gv
        dx = rv * dyg - xv * (rv * rv * rv) * jnp.mean(dyg * xv, axis=-1, keepdims=True)
        return dhv + dx, _colsum(duv * xv * rv)

    return rowwise(fn, [(du, d, 0), (x, d, 0), (r, 1, 0), (dh, d, 0)], [(d, F32)], rows=s, tr=256,
                   name="rms_bwd_x", consts=[g], reds=[((1, d), F32)])


def rms_bwd_g(du, x, r, name):
    s, d = x.shape

    def fn(duv, xv, rv):
        return (_colsum(duv * xv * rv),)

    return rowwise(fn, [(du, d, 0), (x, d, 0), (r, 1, 0)], [], rows=s, tr=256, name=name, reds=[((1, d), F32)])[0]


def final_stage(x, hres, target, g):
    s, d = x.shape

    def fn(xv, hv, tv, gv):
        h = xv + hv
        r = lax.rsqrt(jnp.mean(h * h, axis=-1, keepdims=True) + EPS)
        y = h * r * gv
        e = y - tv
        loss = 0.5 * jnp.sum(jnp.sum(e * e, axis=-1, keepdims=True), axis=0, keepdims=True) / d
        dy = e / d
        dyg = dy * gv
        dh = r * dyg - h * (r * r * r) * jnp.mean(dyg * h, axis=-1, keepdims=True)
        return dh, dh, loss, _colsum(dy * h * r)

    return rowwise(fn, [(x, d, 0), (hres, d, 0), (target, d, 0)], [(d, F32), (d, BF16)], rows=s, tr=256,
                   name="final_stage", consts=[g], reds=[((1, 1), F32), ((1, d), F32)])


def merge_fwd(pa, pb, pc, proj, gate_cb):
    s, d = pa.shape

    def fn(a, b, c, g0, g1, g2):
        return (_sigmoid(g0) * a + _sigmoid(g1) * b + _sigmoid(g2) * c,)

    ins = [(pa, d, 0), (pb, d, 0), (pc, d, 0)] + [(proj, d, gate_cb + i) for i in range(3)]
    return rowwise(fn, ins, [(d, BF16)], rows=s, tr=256, name="merge_fwd")[0]


def merge_bwd(dm, pa, pb, pc, proj, gate_cb, dproj):
    s, d = pa.shape

    def fn(dmv, a, b, c, g0, g1, g2):
        s0, s1, s2 = _sigmoid(g0), _sigmoid(g1), _sigmoid(g2)
        dgates = [dmv * a * s0 * (1.0 - s0), dmv * b * s1 * (1.0 - s1), dmv * c * s2 * (1.0 - s2)]
        return dmv * s0, dmv * s1, dmv * s2, jnp.concatenate(dgates, axis=1)

    ins = [(dm, d, 0), (pa, d, 0), (pb, d, 0), (pc, d, 0)] + [(proj, d, gate_cb + i) for i in range(3)]
    return rowwise(fn, ins, [(d, BF16)] * 3 + [(3 * d, BF16)], rows=s, tr=128, name="merge_bwd", into=(dproj, 3, gate_cb // 3))


def gate_fwd(o, proj, z_cb, name):
    s, w = o.shape

    def fn(ov, zv):
        return (ov * _silu(zv),)

    return rowwise(fn, [(o, w, 0), (proj, w, z_cb)], [(w, BF16)], rows=s, tr=512, name=name)[0]


def gate_bwd(dgo, o, proj, z_cb, name, dproj):
    s, w = o.shape

    def fn(dv, ov, zv):
        return dv * _silu(zv), dv * ov * _dsilu(zv)

    return rowwise(fn, [(dgo, w, 0), (o, w, 0), (proj, w, z_cb)], [(w, F32), (w, BF16)], rows=s, tr=512, name=name,
                   into=(dproj, 1, z_cb))


def gdn_out_fwd(o_raw, proj, z_cb, gn):
    s, w = o_raw.shape

    def fn(ov, zv, gv):
        outs = []
        for h in range(NHEAD):
            oh = ov[:, h * HEAD:(h + 1) * HEAD]
            r = lax.rsqrt(jnp.mean(oh * oh, axis=-1, keepdims=True) + EPS)
            outs.append(oh * r * gv)
        return (jnp.concatenate(outs, axis=1) * _silu(zv),)

    return rowwise(fn, [(o_raw, w, 0), (proj, w, z_cb)], [(w, BF16)], rows=s, tr=512, name="gdn_out_fwd", consts=[gn])[0]


def gdn_out_bwd(dga, o_raw, proj, z_cb, gn, dproj):
    s, w = o_raw.shape

    def fn(dv, ov, zv, gv):
        sz, dsz = _silu(zv), _dsilu(zv)
        do_l, dz_l = [], []
        dg = jnp.zeros((1, HEAD), F32)
        for h in range(NHEAD):
            sl = slice(h * HEAD, (h + 1) * HEAD)
            oh, dgh = ov[:, sl], dv[:, sl]
            r = lax.rsqrt(jnp.mean(oh * oh, axis=-1, keepdims=True) + EPS)
            on = oh * r * gv
            don = dgh * sz[:, sl]
            dz_l.append(dgh * on * dsz[:, sl])
            dg = dg + _colsum(don * oh * r)
            dyg = don * gv
            do_l.append(r * dyg - oh * (r * r * r) * jnp.mean(dyg * oh, axis=-1, keepdims=True))
        return jnp.concatenate(do_l, axis=1), jnp.concatenate(dz_l, axis=1), dg

    return rowwise(fn, [(dga, w, 0), (o_raw, w, 0), (proj, w, z_cb)], [(w, F32), (w, BF16)], rows=s, tr=512,
                   name="gdn_out_bwd", consts=[gn], reds=[((1, HEAD), F32)], into=(dproj, 1, z_cb))


def s5_act_fwd(y_ssm, proj, xb_cb, dvec):
    s, w = y_ssm.shape

    def fn(yv, xv, dv):
        return (_gelu(yv + dv * xv),)

    return rowwise(fn, [(y_ssm, w, 0), (proj, w, xb_cb)], [(w, BF16)], rows=s, tr=512, name="s5_act_fwd", consts=[dvec])[0]


def s5_act_bwd(dyb, y_ssm, proj, xb_cb, dvec):
    s, w = y_ssm.shape

    def fn(dv_, yv, xv, dv):
        dpre = dv_ * _dgelu(yv + dv * xv)
        return dpre, dpre * dv, _colsum(dpre * xv)

    return rowwise(fn, [(dyb, w, 0), (y_ssm, w, 0), (proj, w, xb_cb)], [(w, F32), (w, F32)], rows=s, tr=512,
                   name="s5_act_bwd", consts=[dvec], reds=[((1, w), F32)])


def s5_glu_fwd(glu, proj, z_cb):
    s, w2 = glu.shape
    w = w2 // 2

    def fn(val, gate, zv):
        return (val * _sigmoid(gate) * _silu(zv),)

    return rowwise(fn, [(glu, w, 0), (glu, w, 1), (proj, w, z_cb)], [(w, BF16)], rows=s, tr=512, name="s5_glu_fwd")[0]


def s5_glu_bwd(dgb, glu, proj, z_cb, dproj):
    s, w2 = glu.shape
    w = w2 // 2

    def fn(dv, val, gate, zv):
        sg = _sigmoid(gate)
        ob = val * sg
        dob = dv * _silu(zv)
        return dob * sg, dob * val * sg * (1.0 - sg), dv * ob * _dsilu(zv)

    return rowwise(fn, [(dgb, w, 0), (glu, w, 0), (glu, w, 1), (proj, w, z_cb)], [(w, BF16)] * 3, rows=s, tr=512,
                   name="s5_glu_bwd", into=(dproj, 2, z_cb))


def add_into(a, b, name, dproj, cb):
    s, w = a.shape

    def fn(av, bv):
        return (av + bv,)

    return rowwise(fn, [(a, w, 0), (b, w, 0)], [(w, BF16)], rows=s, tr=512, name=name, into=(dproj, 0, cb))[0]


GATE_W, GATE_CB = 6144, 0
QKV_W, QKV_CB = 3072, 2
ZA_CB, XB_CB, ZB_CB, QC_CB, ZC_CB = 9, 10, 11, 12, 13
BA_CB, BA_W = 112, 128
BA_PAD, BA_PAD_CB = 1024, 14
PROJ_W = 14336 + BA_PAD


def _dot(a, b, dims="nn", prec=None):
    if prec is None:
        a, b = a.astype(BF16), b.astype(BF16)
    return lax.dot_general(a, b, _DIMS[dims], preferred_element_type=F32, precision=prec)


def _split(a):
    hi = a.astype(BF16)
    return hi, (a - hi.astype(F32)).astype(BF16)


def _dot3(a, b, dims="nn"):
    (ah, al), (bh, bl) = _split(a), _split(b)
    d = functools.partial(lax.dot_general, dimension_numbers=_DIMS[dims], preferred_element_type=F32)
    return d(ah, bh) + (d(ah, bl) + d(al, bh))


def _iota2(shape, dim):
    return lax.broadcasted_iota(jnp.int32, shape, dim)


def _conv_taps(xs, tr, k):
    if k == 0:
        return xs[8:8 + tr]
    return pltpu.roll(xs, k, 0)[8:8 + tr]


def _conv_silu_parts(xv, halo, wv, first):
    tr = xv.shape[0]
    xs = jnp.concatenate([jnp.where(first, 0.0, halo), xv], axis=0)
    taps = [_conv_taps(xs, tr, 3 - j) for j in range(4)]
    c = taps[0] * wv[0:1] + taps[1] * wv[1:2] + taps[2] * wv[2:3] + taps[3] * wv[3:4]
    return taps, c


def gdn_prep_fwd(proj, conv_w, alog_pad, dt_pad):
    s = proj.shape[0]
    tr = min(256, s)
    w = NHEAD * HEAD

    def body(x_ref, halo_ref, ba_ref, w_ref, al_ref, dt_ref, q_ref, k_ref, v_ref, bg_ref, gcol_ref, gt_ref):
        first = pl.program_id(0) == 0
        _, c = _conv_silu_parts(x_ref[...], halo_ref[...], w_ref[...], first)
        sv = _silu(c)
        for h in range(NHEAD):
            sl = slice(h * HEAD, (h + 1) * HEAD)
            qh, kh = sv[:, h * HEAD:(h + 1) * HEAD], sv[:, w + h * HEAD:w + (h + 1) * HEAD]
            q_ref[:, sl] = qh * lax.rsqrt(jnp.sum(qh * qh, axis=-1, keepdims=True) + EPS) * (HEAD ** -0.5)
            k_ref[:, sl] = kh * lax.rsqrt(jnp.sum(kh * kh, axis=-1, keepdims=True) + EPS)
        v_ref[...] = sv[:, 2 * w:]
        ba = ba_ref[...]
        lane = _iota2(ba.shape, 1)
        beta = _sigmoid(ba)
        g = -jnp.exp(al_ref[...]) * _softplus(ba + dt_ref[...])
        bg = jnp.where(lane < NHEAD, beta, jnp.where(lane < 2 * NHEAD, g, 0.0))
        bg_ref[...] = bg
        er, ec = _iota2((BA_W, BA_W), 0), _iota2((BA_W, BA_W), 1)
        expand = jnp.where((er == NHEAD + ec // 8) & (ec < 8 * NHEAD), 1.0, 0.0)
        grep = _dot(bg, expand, prec=HI)
        lr, lc = _iota2((tr, tr), 0), _iota2((tr, tr), 1)
        tril = jnp.where((lr // CHUNK == lc // CHUNK) & (lr >= lc), 1.0, 0.0)
        gc = _dot(tril, grep, prec=HI)
        gcol_ref[...] = gc
        gt_ref[...] = gc.T

    nb8 = tr // 8
    return pl.pallas_call(
        body, name="gdn_prep_fwd", grid=(s // tr,),
        in_specs=[pl.BlockSpec((tr, QKV_W), lambda i: (i, QKV_CB)),
                  pl.BlockSpec((8, QKV_W), lambda i: (jnp.maximum(i * nb8 - 1, 0), QKV_CB)),
                  pl.BlockSpec((tr, BA_W), lambda i: (i, BA_CB)),
                  pl.BlockSpec(conv_w.shape, lambda i: (0, 0)),
                  pl.BlockSpec((1, BA_W), lambda i: (0, 0)), pl.BlockSpec((1, BA_W), lambda i: (0, 0))],
        out_specs=[pl.BlockSpec((tr, w), lambda i: (i, 0))] * 3 + [pl.BlockSpec((tr, BA_W), lambda i: (i, 0))] * 2
        + [pl.BlockSpec((BA_W, tr), lambda i: (0, i))],
        out_shape=[jax.ShapeDtypeStruct((s, w), F32)] * 3 + [jax.ShapeDtypeStruct((s, BA_W), F32)] * 2
        + [jax.ShapeDtypeStruct((BA_W, s), F32)],
        compiler_params=_cparams(("parallel",)),
    )(proj, proj, proj, conv_w, alog_pad, dt_pad)


def _chunk_common(qh, kh, bgv, gcolv, gtv, h):
    beta = bgv[:, h:h + 1]
    gcc = gcolv[:, 8 * h:8 * h + 1]
    gcr = jnp.concatenate([gtv[8 * h:8 * h + 8, :]] * (CHUNK // 8), axis=0)
    ii, jj = _iota2((CHUNK, CHUNK), 0), _iota2((CHUNK, CHUNK), 1)
    incl, strict = ii >= jj, ii > jj
    decay = jnp.where(incl, jnp.exp(jnp.where(incl, gcc - gcr, 0.0)), 0.0)
    gl = gcr[:, CHUNK - 1:CHUNK]
    return beta, gcc, decay, strict, gl


def gdn_intra_fwd(q, k, v, bg, gcol, gt3):
    s, w = q.shape
    n = s // CHUNK

    def body(q_ref, k_ref, v_ref, bg_ref, gcol_ref, gt_ref, u_ref, w_ref, qd_ref, kd_ref, qk_ref, t_ref):
        bgv, gcolv, gtv = bg_ref[...], gcol_ref[...], gt_ref[0]
        ii, jj = _iota2((CHUNK, CHUNK), 0), _iota2((CHUNK, CHUNK), 1)
        eye = jnp.where(ii == jj, 1.0, 0.0)
        ps, ts, rhs = [], [], []
        for h in range(NHEAD):
            sl = slice(h * HEAD, (h + 1) * HEAD)
            qh, kh, vh = q_ref[:, sl], k_ref[:, sl], v_ref[:, sl]
            beta, gcc, decay, strict, gl = _chunk_common(qh, kh, bgv, gcolv, gtv, h)
            kb = kh * beta
            eg = jnp.exp(gcc)
            p = -jnp.where(strict, _dot(kb, kh, "nt") * decay, 0.0)
            ps.append(p)
            ts.append(eye + p)
            rhs.append((vh * beta, kb * eg))
            qd_ref[:, sl] = qh * eg
            kd_ref[:, sl] = kh * jnp.exp(gl - gcc)
            qk_ref[0, h] = _dot(qh, kh, "nt") * decay
        for _ in range(5):
            ps = [_dot3(p, p) for p in ps]
            ts = [t + _dot3(t, p) for t, p in zip(ts, ps)]
        for h in range(NHEAD):
            sl = slice(h * HEAD, (h + 1) * HEAD)
            u_ref[:, sl] = _dot3(ts[h], rhs[h][0])
            w_ref[:, sl] = _dot3(ts[h], rhs[h][1])
            t_ref[0, h] = ts[h]

    tok = pl.BlockSpec((CHUNK, w), lambda i: (i, 0))
    sm = pl.BlockSpec((CHUNK, BA_W), lambda i: (i, 0))
    sq = pl.BlockSpec((1, NHEAD, CHUNK, CHUNK), lambda i: (i, 0, 0, 0))
    return pl.pallas_call(
        body, name="gdn_intra_fwd", grid=(n,),
        in_specs=[tok, tok, tok, sm, sm, pl.BlockSpec((1, BA_W, CHUNK), lambda i: (i, 0, 0))],
        out_specs=[tok] * 4 + [sq, sq],
        out_shape=[jax.ShapeDtypeStruct((s, w), F32)] * 4 + [jax.ShapeDtypeStruct((n, NHEAD, CHUNK, CHUNK), F32)] * 2,
        compiler_params=_cparams(("parallel",)),
    )(q, k, v, bg, gcol, gt3)


def _state_decay(gtv, h):
    g8 = gtv[8 * h:8 * h + 8, CHUNK - 1:CHUNK]
    return jnp.exp(jnp.concatenate([g8] * (HEAD // 8), axis=0))


def gdn_seq_fwd(u, wd, qd, kd, qk, gt3):
    s, w = u.shape
    n = s // CHUNK

    def body(u_ref, w_ref, qd_ref, kd_ref, qk_ref, gt_ref, o_ref, st_ref, s_ref):
        @pl.when(pl.program_id(0) == 0)
        def _():
            s_ref[...] = jnp.zeros_like(s_ref)

        gtv = gt_ref[0]
        for h in range(NHEAD):
            sl = slice(h * HEAD, (h + 1) * HEAD)
            sh = s_ref[h]
            st_ref[0, h] = sh
            vn = u_ref[:, sl] - _dot(w_ref[:, sl], sh)
            o_ref[:, sl] = _dot(qd_ref[:, sl], sh) + _dot(qk_ref[0, h], vn)
            s_ref[h] = sh * _state_decay(gtv, h) + _dot(kd_ref[:, sl], vn, "tn")

    tok = pl.BlockSpec((CHUNK, w), lambda i: (i, 0))
    return pl.pallas_call(
        body, name="gdn_seq_fwd", grid=(n,),
        in_specs=[tok] * 4 + [pl.BlockSpec((1, NHEAD, CHUNK, CHUNK), lambda i: (i, 0, 0, 0)),
                              pl.BlockSpec((1, BA_W, CHUNK), lambda i: (i, 0, 0))],
        out_specs=[tok, pl.BlockSpec((1, NHEAD, HEAD, HEAD), lambda i: (i, 0, 0, 0))],
        out_shape=[jax.ShapeDtypeStruct((s, w), F32), jax.ShapeDtypeStruct((n, NHEAD, HEAD, HEAD), F32)],
        scratch_shapes=[pltpu.VMEM((NHEAD, HEAD, HEAD), F32)],
        compiler_params=_cparams(("arbitrary",)),
    )(u, wd, qd, kd, qk, gt3)


def gdn_seq_bwd(do, u, wd, qd, kd, qk, gt3, states):
    s, w = u.shape
    n = s // CHUNK

    def body(do_ref, u_ref, w_ref, qd_ref, kd_ref, qk_ref, gt_ref, st_ref,
             du_ref, dw_ref, dqd_ref, dkd_ref, dqk_ref, dgl_ref, ds_ref):
        @pl.when(pl.program_id(0) == 0)
        def _():
            ds_ref[...] = jnp.zeros_like(ds_ref)

        gtv = gt_ref[0]
        dgl_rows = []
        for h in range(NHEAD):
            sl = slice(h * HEAD, (h + 1) * HEAD)
            sh, dsp, doh = st_ref[0, h], ds_ref[h], do_ref[:, sl]
            wh, qdh, kdh, qkh = w_ref[:, sl], qd_ref[:, sl], kd_ref[:, sl], qk_ref[0, h]
            vn = u_ref[:, sl] - _dot(wh, sh)
            dvn = _dot(qkh, doh, "tn") + _dot(kdh, dsp)
            du_ref[:, sl] = dvn
            dw_ref[:, sl] = -_dot(dvn, sh, "nt")
            dqd_ref[:, sl] = _dot(doh, sh, "nt")
            dkd_ref[:, sl] = _dot(vn, dsp, "nt")
            dqk_ref[0, h] = _dot(doh, vn, "nt")
            dgl_rows.append(_colsum(sh * dsp))
            ds_ref[h] = dsp * _state_decay(gtv, h) + _dot(qdh, doh, "tn") - _dot(wh, dvn, "tn")
        dgl_ref[0] = jnp.concatenate(dgl_rows, axis=0)

    tok = pl.BlockSpec((CHUNK, w), lambda i: (n - 1 - i, 0))
    sq = pl.BlockSpec((1, NHEAD, CHUNK, CHUNK), lambda i: (n - 1 - i, 0, 0, 0))
    return pl.pallas_call(
        body, name="gdn_seq_bwd", grid=(n,),
        in_specs=[tok] * 5 + [sq, pl.BlockSpec((1, BA_W, CHUNK), lambda i: (n - 1 - i, 0, 0)),
                              pl.BlockSpec((1, NHEAD, HEAD, HEAD), lambda i: (n - 1 - i, 0, 0, 0))],
        out_specs=[tok] * 4 + [sq, pl.BlockSpec((1, NHEAD, HEAD), lambda i: (n - 1 - i, 0, 0))],
        out_shape=[jax.ShapeDtypeStruct((s, w), F32)] * 4 + [jax.ShapeDtypeStruct((n, NHEAD, CHUNK, CHUNK), F32),
                                                            jax.ShapeDtypeStruct((n, NHEAD, HEAD), F32)],
        scratch_shapes=[pltpu.VMEM((NHEAD, HEAD, HEAD), F32)],
        compiler_params=_cparams(("arbitrary",)),
    )(do, u, wd, qd, kd, qk, gt3, states)


def gdn_intra_bwd(q, k, v, bg, gcol, gt3, tinv, du, dw, dqd, dkd, dqk, dgl):
    s, w = q.shape
    n = s // CHUNK

    def body(q_ref, k_ref, v_ref, bg_ref, gcol_ref, gt_ref, t_ref, du_ref, dw_ref, dqd_ref, dkd_ref, dqk_ref, dgl_ref,
             dq_ref, dk_ref, dv_ref, dbg_ref):
        bgv, gcolv, gtv, dglv = bg_ref[...], gcol_ref[...], gt_ref[0], dgl_ref[0]
        ii, jj = _iota2((CHUNK, CHUNK), 0), _iota2((CHUNK, CHUNK), 1)
        triu = jnp.where(ii <= jj, 1.0, 0.0)
        ones = jnp.ones((CHUNK, BA_W), F32)
        lane = _iota2((CHUNK, BA_W), 1)
        row = _iota2((CHUNK, 1), 0)
        dbg = jnp.zeros((CHUNK, BA_W), F32)
        first = []
        for h in range(NHEAD):
            sl = slice(h * HEAD, (h + 1) * HEAD)
            qh, kh, vh = q_ref[:, sl], k_ref[:, sl], v_ref[:, sl]
            beta, gcc, decay, strict, gl = _chunk_common(qh, kh, bgv, gcolv, gtv, h)
            kb = kh * beta
            eg = jnp.exp(gcc)
            rv, rk = vh * beta, kb * eg
            t, duh, dwh = t_ref[0, h], du_ref[:, sl], dw_ref[:, sl]
            first.append((_dot3(duh, rv, "nt") + _dot3(dwh, rk, "nt"), _dot3(t, duh, "tn"), _dot3(t, dwh, "tn"),
                          _dot(kb, kh, "nt"), _dot(qh, kh, "nt")))
        second = [_dot3(t_ref[0, h], first[h][0], "tn") for h in range(NHEAD)]
        third = [_dot3(second[h], t_ref[0, h], "nt") for h in range(NHEAD)]
        for h in range(NHEAD):
            sl = slice(h * HEAD, (h + 1) * HEAD)
            qh, kh, vh = q_ref[:, sl], k_ref[:, sl], v_ref[:, sl]
            beta, gcc, decay, strict, gl = _chunk_common(qh, kh, bgv, gcolv, gtv, h)
            dqdh, dkdh, dqkh = dqd_ref[:, sl], dkd_ref[:, sl], dqk_ref[0, h]
            kb = kh * beta
            eg = jnp.exp(gcc)
            ekd = jnp.exp(gl - gcc)
            rk = kb * eg
            _, drv, drk, m, p = first[h]
            da = jnp.where(strict, -third[h], 0.0)
            dm = da * decay
            dpm = dqkh * decay
            dkb = _dot(dm, kh) + drk * eg
            dq = _dot(dpm, kh) + dqdh * eg
            dk = _dot(dm, kb, "tn") + _dot(dpm, qh, "tn") + dkdh * ekd + dkb * beta
            e = (da * m + dqkh * p) * decay
            sk = jnp.sum(dkdh * kh * ekd, axis=-1, keepdims=True)
            dgc = (jnp.sum(e, axis=-1, keepdims=True) - _dot3(e, ones, "tn")[:, 0:1]
                   + jnp.sum(dqdh * qh * eg, axis=-1, keepdims=True) - sk + jnp.sum(drk * rk, axis=-1, keepdims=True))
            dglast = jnp.sum(sk, axis=0, keepdims=True) + jnp.sum(dglv[h:h + 1, :], axis=-1, keepdims=True) * jnp.exp(gl)
            dgc = dgc + jnp.where(row == CHUNK - 1, dglast, 0.0)
            dg = _dot3(triu, dgc * ones)
            dbeta = jnp.sum(dkb * kh, axis=-1, keepdims=True) + jnp.sum(drv * vh, axis=-1, keepdims=True)
            dbg = dbg + jnp.where(lane == h, dbeta, 0.0) + jnp.where(lane == NHEAD + h, dg, 0.0)
            dq_ref[:, sl] = dq
            dk_ref[:, sl] = dk
            dv_ref[:, sl] = drv * beta
        dbg_ref[...] = dbg

    tok = pl.BlockSpec((CHUNK, w), lambda i: (i, 0))
    sm = pl.BlockSpec((CHUNK, BA_W), lambda i: (i, 0))
    sq = pl.BlockSpec((1, NHEAD, CHUNK, CHUNK), lambda i: (i, 0, 0, 0))
    return pl.pallas_call(
        body, name="gdn_intra_bwd", grid=(n,),
        in_specs=[tok, tok, tok, sm, sm, pl.BlockSpec((1, BA_W, CHUNK), lambda i: (i, 0, 0)), sq,
                  tok, tok, tok, tok, sq, pl.BlockSpec((1, NHEAD, HEAD), lambda i: (i, 0, 0))],
        out_specs=[tok] * 3 + [sm],
        out_shape=[jax.ShapeDtypeStruct((s, w), F32)] * 3 + [jax.ShapeDtypeStruct((s, BA_W), F32)],
        compiler_params=_cparams(("parallel",)),
    )(q, k, v, bg, gcol, gt3, tinv, du, dw, dqd, dkd, dqk, dgl)


def gdn_prep_bwd1(proj, conv_w, alog_pad, dt_pad, dq, dk, dv, dbg, dproj):
    s = proj.shape[0]
    tr = min(256, s)
    w = NHEAD * HEAD
    pad_w = BA_PAD

    def body(x_ref, halo_ref, ba_ref, w_ref, al_ref, dt_ref, dq_ref, dk_ref, dv_ref, dbg_ref, buf_ref,
             dc_ref, dba_ref, dw0_ref, dw1_ref, dw2_ref, dw3_ref, dal_ref, ddt_ref):
        i = pl.program_id(0)
        taps, c = _conv_silu_parts(x_ref[...], halo_ref[...], w_ref[...], i == 0)
        sv, dsv = _silu(c), _dsilu(c)
        for h in range(NHEAD):
            for base, d_ref, scale in ((0, dq_ref, HEAD ** -0.5), (w, dk_ref, 1.0)):
                sl = slice(base + h * HEAD, base + (h + 1) * HEAD)
                sh = sv[:, sl]
                dn = d_ref[:, h * HEAD:(h + 1) * HEAD]
                r = lax.rsqrt(jnp.sum(sh * sh, axis=-1, keepdims=True) + EPS)
                dsh = scale * (r * dn - sh * (r * r * r) * jnp.sum(dn * sh, axis=-1, keepdims=True))
                dc_ref[:, sl] = dsh * dsv[:, sl]
        dc_ref[:, 2 * w:] = dv_ref[...] * dsv[:, 2 * w:]
        dc = dc_ref[...]
        ba, dbgv = ba_ref[...], dbg_ref[...]
        lane = _iota2(ba.shape, 1)
        beta = _sigmoid(ba)
        ea = jnp.exp(al_ref[...])
        z = ba + dt_ref[...]
        g = -ea * _softplus(z)
        is_g = (lane >= NHEAD) & (lane < 2 * NHEAD)
        da_raw = jnp.where(is_g, dbgv * (-ea) * _sigmoid(z), 0.0)
        dba = jnp.where(lane < NHEAD, dbgv * beta * (1.0 - beta), da_raw)
        dba_ref[...] = jnp.concatenate([dba, jnp.zeros((tr, pad_w - BA_W), F32)], axis=1).astype(BF16)
        partial = [_colsum(dc * tp) for tp in taps] + [_colsum(jnp.where(is_g, dbgv * g, 0.0)), _colsum(da_raw)]
        red_refs = (dw0_ref, dw1_ref, dw2_ref, dw3_ref, dal_ref, ddt_ref)

        @pl.when(i == 0)
        def _():
            for r_, v_ in zip(red_refs, partial):
                r_[...] = v_

        @pl.when(i > 0)
        def _():
            for r_, v_ in zip(red_refs, partial):
                r_[...] += v_

    nb8 = tr // 8
    tok = pl.BlockSpec((tr, w), lambda i: (i, 0))
    one = lambda width: pl.BlockSpec((1, width), lambda i: (0, 0))
    return pl.pallas_call(
        body, name="gdn_prep_bwd1", grid=(s // tr,),
        in_specs=[pl.BlockSpec((tr, QKV_W), lambda i: (i, QKV_CB)),
                  pl.BlockSpec((8, QKV_W), lambda i: (jnp.maximum(i * nb8 - 1, 0), QKV_CB)),
                  pl.BlockSpec((tr, BA_W), lambda i: (i, BA_CB)),
                  pl.BlockSpec(conv_w.shape, lambda i: (0, 0)), one(BA_W), one(BA_W),
                  tok, tok, tok, pl.BlockSpec((tr, BA_W), lambda i: (i, 0)), pl.BlockSpec(memory_space=pl.ANY)],
        out_specs=[pl.BlockSpec((tr, QKV_W), lambda i: (i, 0)), pl.BlockSpec((tr, pad_w), lambda i: (i, BA_PAD_CB))]
        + [one(QKV_W)] * 4 + [one(BA_W)] * 2,
        out_shape=[jax.ShapeDtypeStruct((s, QKV_W), F32), jax.ShapeDtypeStruct(dproj.shape, dproj.dtype)]
        + [jax.ShapeDtypeStruct((1, QKV_W), F32)] * 4 + [jax.ShapeDtypeStruct((1, BA_W), F32)] * 2,
        input_output_aliases={10: 1}, compiler_params=_cparams(("arbitrary",)),
    )(proj, proj, proj, conv_w, alog_pad, dt_pad, dq, dk, dv, dbg, dproj)


def gdn_prep_bwd2(dc, conv_w, dproj):
    s = dc.shape[0]
    tr = min(256, s)
    nblk = s // tr
    nb8 = tr // 8

    def body(dc_ref, halo_ref, w_ref, buf_ref, o_ref):
        last = pl.program_id(0) == nblk - 1
        wv = w_ref[...]
        xs = jnp.concatenate([dc_ref[...], jnp.where(last, 0.0, halo_ref[...])], axis=0)
        acc = xs[:tr] * wv[3:4]
        for j in range(3):
            acc = acc + pltpu.roll(xs, tr + 8 - (3 - j), 0)[:tr] * wv[j:j + 1]
        o_ref[...] = acc.astype(BF16)

    return pl.pallas_call(
        body, name="gdn_prep_bwd2", grid=(nblk,),
        in_specs=[pl.BlockSpec((tr, QKV_W), lambda i: (i, 0)),
                  pl.BlockSpec((8, QKV_W), lambda i: (jnp.minimum((i + 1) * nb8, s // 8 - 1), 0)),
                  pl.BlockSpec(conv_w.shape, lambda i: (0, 0)), pl.BlockSpec(memory_space=pl.ANY)],
        out_specs=pl.BlockSpec((tr, QKV_W), lambda i: (i, QKV_CB)),
        out_shape=jax.ShapeDtypeStruct(dproj.shape, dproj.dtype), input_output_aliases={3: 0},
        compiler_params=_cparams(("parallel",)),
    )(dc, dc, conv_w, dproj)


S5_W = S5_GROUPS * S5_STATE
S5_IN = S5_GROUPS * S5_GROUP
S5_TILES = 8
S5_TW, S5_TI = S5_W // S5_TILES, S5_IN // S5_TILES


def _s5_param_math(lr, li, ldt, br, bi):
    pr, pc = _iota2((S5_STATE, S5_STATE * S5_GROUP), 0), _iota2((S5_STATE, S5_STATE * S5_GROUP), 1)
    rep = jnp.where(pc // S5_GROUP == pr, 1.0, 0.0)
    dt = jnp.exp(ldt)
    mag = jnp.exp(lr * dt)
    ab_re, ab_im = mag * jnp.cos(li * dt), mag * jnp.sin(li * dt)
    den = lr * lr + li * li
    nr, ni = ab_re - 1.0, ab_im
    coef_re = (nr * lr + ni * li) / den
    coef_im = (ni * lr - nr * li) / den
    cr, ci = _dot(coef_re, rep, prec=HI), _dot(coef_im, rep, prec=HI)
    return ab_re, ab_im, cr * br - ci * bi, cr * bi + ci * br


def s5_param_fwd(lr, li, ldt, br, bi):
    def body(lr_ref, li_ref, ldt_ref, br_ref, bi_ref, ar_ref, ai_ref, bbr_ref, bbi_ref):
        res = _s5_param_math(lr_ref[...], li_ref[...], ldt_ref[...], br_ref[...], bi_ref[...])
        for r, v in zip((ar_ref, ai_ref, bbr_ref, bbi_ref), res):
            r[...] = v

    return pl.pallas_call(
        body, name="s5_param_fwd",
        out_shape=[jax.ShapeDtypeStruct(lr.shape, F32)] * 2 + [jax.ShapeDtypeStruct(br.shape, F32)] * 2,
        compiler_params=_cparams(),
    )(lr, li, ldt, br, bi)


def s5_param_bwd(lr, li, ldt, br, bi, dar, dai, dbbr, dbbi):
    def body(lr_ref, li_ref, ldt_ref, br_ref, bi_ref, dar_ref, dai_ref, dbbr_ref, dbbi_ref, *out_refs):
        _, vjp = jax.vjp(_s5_param_math, lr_ref[...], li_ref[...], ldt_ref[...], br_ref[...], bi_ref[...])
        for r, v in zip(out_refs, vjp((dar_ref[...], dai_ref[...], dbbr_ref[...], dbbi_ref[...]))):
            r[...] = v

    return pl.pallas_call(
        body, name="s5_param_bwd",
        out_shape=[jax.ShapeDtypeStruct(a.shape, F32) for a in (lr, li, ldt, br, bi)],
        compiler_params=_cparams(),
    )(lr, li, ldt, br, bi, dar, dai, dbbr, dbbi)


def _cmul(ar, ai, br, bi):
    return ar * br - ai * bi, ar * bi + ai * br


def _s5_power(ar, ai, steps):
    assert steps & (steps - 1) == 0
    for _ in range(steps.bit_length() - 1):
        ar, ai = _cmul(ar, ai, ar, ai)
    return ar, ai


def _s5_scan_rows(ar_ref, ai_ref, re_ref, im_ref, sr_ref, si_ref, tb, row0, reverse):
    quarter = S5_W // 4
    for qd in range(4):
        cs = slice(qd * quarter, (qd + 1) * quarter)
        are = jnp.broadcast_to(ar_ref[:, cs], (NSEG, quarter))
        aim = jnp.broadcast_to(ai_ref[:, cs], (NSEG, quarter))
        if reverse:
            aim = -aim

        def step(t, carry):
            h_r, h_i = carry
            tt = tb - 1 - t if reverse else t
            rows = pl.ds(pl.multiple_of(row0 + tt * NSEG, NSEG), NSEG)
            n_r = are * h_r - aim * h_i + re_ref[rows, cs]
            n_i = are * h_i + aim * h_r + im_ref[rows, cs]
            re_ref[rows, cs] = n_r
            im_ref[rows, cs] = n_i
            return n_r, n_i

        h_r, h_i = lax.fori_loop(0, tb, step, (sr_ref[:, cs], si_ref[:, cs]), unroll=4)
        sr_ref[:, cs] = h_r
        si_ref[:, cs] = h_i


def _s5_segment_carry(ar_ref, ai_ref, sr_ref, si_ref, steps, reverse):
    pr, pi = _s5_power(ar_ref[...], ai_ref[...], steps)
    if reverse:
        pi = -pi
    cur_r = jnp.zeros((1, S5_W), F32)
    cur_i = jnp.zeros((1, S5_W), F32)
    for s in (range(NSEG - 1, -1, -1) if reverse else range(NSEG)):
        e_r, e_i = sr_ref[s:s + 1, :], si_ref[s:s + 1, :]
        sr_ref[s:s + 1, :] = cur_r
        si_ref[s:s + 1, :] = cur_i
        nr, ni = _cmul(pr, pi, cur_r, cur_i)
        cur_r, cur_i = nr + e_r, ni + e_i


def _s5_blocks(s):
    steps = s // NSEG
    tb = min(32, steps)
    return steps, tb, NSEG * tb, steps // tb


def s5_scan_fwd(xp, a_re, a_im, bre, bim, cre, cim):
    s = xp.shape[0]
    steps, tb, rb, nb = _s5_blocks(s)

    def body(x_ref, ar_ref, ai_ref, bre_ref, bim_ref, cre_ref, cim_ref, y_ref, hsr_ref, hsi_ref,
             hr_ref, hi_ref, sr_ref, si_ref):
        ph, b = pl.program_id(0), pl.program_id(1)

        @pl.when((ph == 0) & (b == 0))
        def _():
            sr_ref[...] = jnp.zeros_like(sr_ref)
            si_ref[...] = jnp.zeros_like(si_ref)

        @pl.when((ph == 1) & (b == 0))
        def _():
            _s5_segment_carry(ar_ref, ai_ref, sr_ref, si_ref, steps, False)

        xv = x_ref[...].astype(BF16)
        for j in range(S5_TILES):
            xs = xv[:, j * S5_TI:(j + 1) * S5_TI]
            hr_ref[:, j * S5_TW:(j + 1) * S5_TW] = _dot(xs, bre_ref[j])
            hi_ref[:, j * S5_TW:(j + 1) * S5_TW] = _dot(xs, bim_ref[j])

        @pl.when(ph == 1)
        def _():
            hsr_ref[0] = sr_ref[...]
            hsi_ref[0] = si_ref[...]

        _s5_scan_rows(ar_ref, ai_ref, hr_ref, hi_ref, sr_ref, si_ref, tb, 0, False)

        @pl.when(ph == 1)
        def _():
            for j in range(S5_TILES):
                cs = slice(j * S5_TW, (j + 1) * S5_TW)
                y_ref[:, j * S5_TI:(j + 1) * S5_TI] = _dot(hr_ref[:, cs], cre_ref[j]) - _dot(hi_ref[:, cs], cim_ref[j])

    row = pl.BlockSpec((1, S5_W), lambda p, b: (0, 0))
    wb = pl.BlockSpec((S5_TILES, S5_TI, S5_TW), lambda p, b: (0, 0, 0))
    wc = pl.BlockSpec((S5_TILES, S5_TW, S5_TI), lambda p, b: (0, 0, 0))
    st = pl.BlockSpec((1, NSEG, S5_W), lambda p, b: (p * b, 0, 0))
    return pl.pallas_call(
        body, name="s5_scan_fwd", grid=(2, nb),
        in_specs=[pl.BlockSpec((rb, S5_IN), lambda p, b: (b, 0)), row, row, wb, wb, wc, wc],
        out_specs=[pl.BlockSpec((rb, S5_IN), lambda p, b: (p * b, 0)), st, st],
        out_shape=[jax.ShapeDtypeStruct((s, S5_IN), F32)] + [jax.ShapeDtypeStruct((nb, NSEG, S5_W), F32)] * 2,
        scratch_shapes=[pltpu.VMEM((rb, S5_W), F32)] * 2 + [pltpu.VMEM((NSEG, S5_W), F32)] * 2,
        compiler_params=_cparams(("arbitrary", "arbitrary")),
    )(xp, a_re, a_im, bre, bim, cre, cim)


def s5_scan_bwd(dyp, xp, a_re, a_im, bre, bim, cre_t, cim_t, hs_r, hs_i):
    s = xp.shape[0]
    steps, tb, rb, nb = _s5_blocks(s)

    def body(dy_ref, x_ref, ar_ref, ai_ref, bre_ref, bim_ref, crt_ref, cit_ref, hsr_ref, hsi_ref,
             dx_ref, dar_ref, dai_ref, dbr_ref, dbi_ref, dcr_ref, dci_ref,
             hr_ref, hi_ref, lr_ref, li_ref, sr_ref, si_ref, fr_ref, fi_ref, accr_ref, acci_ref):
        ph, b = pl.program_id(0), pl.program_id(1)

        @pl.when((ph == 0) & (b == 0))
        def _():
            sr_ref[...] = jnp.zeros_like(sr_ref)
            si_ref[...] = jnp.zeros_like(si_ref)

        @pl.when((ph == 1) & (b == 0))
        def _():
            _s5_segment_carry(ar_ref, ai_ref, sr_ref, si_ref, steps, True)
            for r in (accr_ref, acci_ref, dbr_ref, dbi_ref, dcr_ref, dci_ref):
                r[...] = jnp.zeros_like(r)

        dyv = dy_ref[...].astype(BF16)
        for j in range(S5_TILES):
            ds_ = dyv[:, j * S5_TI:(j + 1) * S5_TI]
            lr_ref[:, j * S5_TW:(j + 1) * S5_TW] = _dot(ds_, crt_ref[j])
            li_ref[:, j * S5_TW:(j + 1) * S5_TW] = -_dot(ds_, cit_ref[j])
        _s5_scan_rows(ar_ref, ai_ref, lr_ref, li_ref, sr_ref, si_ref, tb, 0, True)

        @pl.when(ph == 1)
        def _():
            xv = x_ref[...].astype(BF16)
            for j in range(S5_TILES):
                xs = xv[:, j * S5_TI:(j + 1) * S5_TI]
                hr_ref[NSEG:, j * S5_TW:(j + 1) * S5_TW] = _dot(xs, bre_ref[j])
                hi_ref[NSEG:, j * S5_TW:(j + 1) * S5_TW] = _dot(xs, bim_ref[j])
            hr_ref[0:NSEG, :] = hsr_ref[0]
            hi_ref[0:NSEG, :] = hsi_ref[0]
            fr_ref[...] = hsr_ref[0]
            fi_ref[...] = hsi_ref[0]
            _s5_scan_rows(ar_ref, ai_ref, hr_ref, hi_ref, fr_ref, fi_ref, tb, NSEG, False)
            lam_r, lam_i = lr_ref[...], li_ref[...]
            hp_r, hp_i = hr_ref[0:rb, :], hi_ref[0:rb, :]
            accr_ref[...] += jnp.sum((lam_r * hp_r + lam_i * hp_i).reshape(tb, NSEG, S5_W), axis=0)
            acci_ref[...] += jnp.sum((lam_i * hp_r - lam_r * hp_i).reshape(tb, NSEG, S5_W), axis=0)
            lam_rb, lam_ib = lam_r.astype(BF16), lam_i.astype(BF16)
            h_rb, h_ib = hr_ref[NSEG:, :].astype(BF16), hi_ref[NSEG:, :].astype(BF16)
            for j in range(S5_TILES):
                cs, ci = slice(j * S5_TW, (j + 1) * S5_TW), slice(j * S5_TI, (j + 1) * S5_TI)
                dbr_ref[j] += _dot(xv[:, ci], lam_rb[:, cs], "tn")
                dbi_ref[j] += _dot(xv[:, ci], lam_ib[:, cs], "tn")
                dx_ref[:, ci] = _dot(lam_rb[:, cs], bre_ref[j], "nt") + _dot(lam_ib[:, cs], bim_ref[j], "nt")
                dcr_ref[j] += _dot(h_rb[:, cs], dyv[:, ci], "tn")
                dci_ref[j] -= _dot(h_ib[:, cs], dyv[:, ci], "tn")

        @pl.when((ph == 1) & (b == nb - 1))
        def _():
            dar_ref[...] = jnp.sum(accr_ref[...], axis=0, keepdims=True)
            dai_ref[...] = jnp.sum(acci_ref[...], axis=0, keepdims=True)

    rev = lambda p, b: (nb - 1 - b, 0)
    row = pl.BlockSpec((1, S5_W), lambda p, b: (0, 0))
    wb = pl.BlockSpec((S5_TILES, S5_TI, S5_TW), lambda p, b: (0, 0, 0))
    wc = pl.BlockSpec((S5_TILES, S5_TW, S5_TI), lambda p, b: (0, 0, 0))
    st = pl.BlockSpec((1, NSEG, S5_W), lambda p, b: (nb - 1 - b, 0, 0))
    big = pltpu.VMEM((rb, S5_W), F32)
    big8 = pltpu.VMEM((rb + NSEG, S5_W), F32)
    small = pltpu.VMEM((NSEG, S5_W), F32)
    return pl.pallas_call(
        body, name="s5_scan_bwd", grid=(2, nb),
        in_specs=[pl.BlockSpec((rb, S5_IN), rev), pl.BlockSpec((rb, S5_IN), rev), row, row, wb, wb, wb, wb, st, st],
        out_specs=[pl.BlockSpec((rb, S5_IN), lambda p, b: (nb - 1 - p * b, 0)), row, row, wb, wb, wc, wc],
        out_shape=[jax.ShapeDtypeStruct((s, S5_IN), F32)] + [jax.ShapeDtypeStruct((1, S5_W), F32)] * 2
        + [jax.ShapeDtypeStruct((S5_TILES, S5_TI, S5_TW), F32)] * 2 + [jax.ShapeDtypeStruct((S5_TILES, S5_TW, S5_TI), F32)] * 2,
        scratch_shapes=[big8, big8, big, big, small, small, small, small, small, small],
        compiler_params=_cparams(("arbitrary", "arbitrary")),
    )(dyp, xp, a_re, a_im, bre, bim, cre_t, cim_t, hs_r, hs_i)


XA_DIM = 256
XA_W = XA_HEADS * XA_DIM


def _xa_probs(qh, kh):
    sc = _dot(qh, kh, "nt") * (XA_DIM ** -0.5)
    ex = jnp.exp(sc - jnp.max(sc, axis=-1, keepdims=True))
    return ex / jnp.sum(ex, axis=-1, keepdims=True)


def xa_fwd(proj, kv):
    s = proj.shape[0]
    tq = min(512, s)

    def body(q_ref, kv_ref, o_ref):
        for h in range(XA_HEADS):
            sl = slice(h * XA_DIM, (h + 1) * XA_DIM)
            p = _xa_probs(q_ref[:, sl], kv_ref[:, sl])
            o_ref[:, sl] = _dot(p, kv_ref[:, XA_W + h * XA_DIM:XA_W + (h + 1) * XA_DIM])

    return pl.pallas_call(
        body, name="xa_fwd", grid=(s // tq,),
        in_specs=[pl.BlockSpec((tq, XA_W), lambda i: (i, QC_CB)), pl.BlockSpec(kv.shape, lambda i: (0, 0))],
        out_specs=pl.BlockSpec((tq, XA_W), lambda i: (i, 0)),
        out_shape=jax.ShapeDtypeStruct((s, XA_W), F32),
        compiler_params=_cparams(("parallel",)),
    )(proj, kv)


def xa_bwd(do, proj, kv, dproj):
    s = proj.shape[0]
    tq = min(512, s)

    def body(do_ref, q_ref, kv_ref, buf_ref, dq_ref, dkv_ref):
        @pl.when(pl.program_id(0) == 0)
        def _():
            dkv_ref[...] = jnp.zeros_like(dkv_ref)

        for h in range(XA_HEADS):
            sl = slice(h * XA_DIM, (h + 1) * XA_DIM)
            sv = slice(XA_W + h * XA_DIM, XA_W + (h + 1) * XA_DIM)
            qh, kh, vh, doh = q_ref[:, sl], kv_ref[:, sl], kv_ref[:, sv], do_ref[:, sl]
            p = _xa_probs(qh, kh)
            dp = _dot(doh, vh, "nt")
            ds_ = p * (dp - jnp.sum(dp * p, axis=-1, keepdims=True)) * (XA_DIM ** -0.5)
            dq_ref[:, sl] = _dot(ds_, kh).astype(BF16)
            dkv_ref[:, sl] += _dot(ds_, qh, "tn")
            dkv_ref[:, sv] += _dot(p, doh, "tn")

    return pl.pallas_call(
        body, name="xa_bwd", grid=(s // tq,),
        in_specs=[pl.BlockSpec((tq, XA_W), lambda i: (i, 0)), pl.BlockSpec((tq, XA_W), lambda i: (i, QC_CB)),
                  pl.BlockSpec(kv.shape, lambda i: (0, 0)), pl.BlockSpec(memory_space=pl.ANY)],
        out_specs=[pl.BlockSpec((tq, XA_W), lambda i: (i, QC_CB)), pl.BlockSpec(kv.shape, lambda i: (0, 0))],
        out_shape=[jax.ShapeDtypeStruct(dproj.shape, dproj.dtype), jax.ShapeDtypeStruct(kv.shape, F32)],
        input_output_aliases={3: 0}, compiler_params=_cparams(("arbitrary",)),
    )(do, proj, kv, dproj)


def adamw(w, g, m, v, name):
    lead = (0,) * (w.ndim - 2)
    rows, cols = w.shape[-2:]
    tr = rows
    while tr * cols * 4 * 7 * 2 > 36 * 2 ** 20 and tr % 16 == 0:
        tr //= 2

    def body(w_ref, g_ref, m_ref, v_ref, d_ref, m2_ref, v2_ref):
        gv = g_ref[...]
        m2 = ADAM_B1 * m_ref[...] + (1.0 - ADAM_B1) * gv
        v2 = ADAM_B2 * v_ref[...] + (1.0 - ADAM_B2) * (gv * gv)
        m_hat = m2 / (1.0 - ADAM_B1 ** ADAM_STEP)
        v_hat = v2 / (1.0 - ADAM_B2 ** ADAM_STEP)
        d_ref[...] = -ADAM_LR * (m_hat / (jnp.sqrt(v_hat) + ADAM_EPS) + ADAM_WD * w_ref[...])
        m2_ref[...] = m2
        v2_ref[...] = v2

    spec = pl.BlockSpec((1,) * len(lead) + (tr, cols), lambda i: lead + (i, 0))
    return pl.pallas_call(
        body, name=name, grid=(rows // tr,), in_specs=[spec] * 4, out_specs=[spec] * 3,
        out_shape=[jax.ShapeDtypeStruct(w.shape, F32)] * 3, compiler_params=_cparams(("parallel",)),
    )(w, g, m, v)


def _seg_perm(a):
    s, w = a.shape
    return a.reshape(NSEG, s // NSEG, w).transpose(1, 0, 2).reshape(s, w)


def _seg_unperm(a):
    s, w = a.shape
    return a.reshape(s // NSEG, NSEG, w).transpose(1, 0, 2).reshape(s, w)


def _block_diag(t):
    nt, _, r, c = t.shape
    eye = jnp.eye(8, dtype=bool)
    return jnp.where(eye[None, :, None, :, None], t[:, :, :, None, :], 0.0).reshape(nt, 8 * r, 8 * c)


def _block_diag_inv(d, r, c):
    d5 = d.reshape(d.shape[0], 8, r, 8, c)
    return jnp.diagonal(d5, axis1=1, axis2=3).transpose(0, 3, 1, 2)


def _s5_b_tiles(bb):
    return _block_diag(bb.reshape(S5_TILES, 8, S5_STATE, S5_GROUP).transpose(0, 1, 3, 2))


def _s5_b_untile(d):
    return _block_diag_inv(d, S5_GROUP, S5_STATE).transpose(0, 1, 3, 2).reshape(S5_GROUPS, S5_STATE * S5_GROUP)


def _s5_c_tiles(c):
    return _block_diag(c.reshape(S5_TILES, 8, S5_GROUP, S5_STATE).transpose(0, 1, 3, 2))


def _s5_c_untile(d):
    return _block_diag_inv(d, S5_STATE, S5_GROUP).transpose(0, 1, 3, 2).reshape(S5_GROUPS, S5_GROUP, S5_STATE)


def s5_ssm_fwd(xb, lam_re, lam_im, log_dt, b_re, b_im, c_re, c_im):
    br, bi = b_re.reshape(S5_GROUPS, -1), b_im.reshape(S5_GROUPS, -1)
    ldt = log_dt.reshape(S5_GROUPS, 1)
    ab_re, ab_im, bb_re, bb_im = s5_param_fwd(lam_re, lam_im, ldt, br, bi)
    a_re, a_im = ab_re.reshape(1, S5_W), ab_im.reshape(1, S5_W)
    bre, bim = _s5_b_tiles(bb_re).astype(BF16), _s5_b_tiles(bb_im).astype(BF16)
    cre, cim = _s5_c_tiles(c_re).astype(BF16), _s5_c_tiles(c_im).astype(BF16)
    xp = _seg_perm(xb)
    yp, hs_r, hs_i = s5_scan_fwd(xp, a_re, a_im, bre, bim, cre, cim)
    saved = (xp, a_re, a_im, bre, bim, cre, cim, hs_r, hs_i, (lam_re, lam_im, ldt, br, bi))
    return _seg_unperm(yp), saved


def s5_ssm_bwd(dy, saved):
    xp, a_re, a_im, bre, bim, cre, cim, hs_r, hs_i, params = saved
    cre_t, cim_t = cre.transpose(0, 2, 1), cim.transpose(0, 2, 1)
    dxp, dar, dai, dbr, dbi, dcr, dci = s5_scan_bwd(_seg_perm(dy), xp, a_re, a_im, bre, bim, cre_t, cim_t, hs_r, hs_i)
    dlr, dli, dldt, db_re, db_im = s5_param_bwd(*params, dar.reshape(S5_GROUPS, S5_STATE), dai.reshape(S5_GROUPS, S5_STATE),
                                                _s5_b_untile(dbr), _s5_b_untile(dbi))
    shape_b = (S5_GROUPS, S5_STATE, S5_GROUP)
    return (_seg_unperm(dxp), dlr, dli, dldt.reshape(S5_GROUPS), db_re.reshape(shape_b), db_im.reshape(shape_b),
            _s5_c_untile(dcr), _s5_c_untile(dci))


_MESH = pl.DeviceIdType.MESH
_HBM = pl.BlockSpec(memory_space=pltpu.HBM)
N_DEV = 8


def _position():
    return lax.axis_index("x"), lax.axis_index("y"), lax.axis_index("c")


D2D_CHUNK_BYTES = 2 ** 20


def _chunk_rows(rows, cols, itemsize):
    return _row_tile(rows, 16, max(16, D2D_CHUNK_BYTES // (cols * itemsize)))


def _rows(start, size, unit=16):
    return pl.ds(pl.multiple_of(start, unit), size)


def _push_to_sibling(chunks, stages, recv_sems, store_sems, sibling, lag=2):
    in_slot, used, stores = {}, {}, []

    def push(q, slot):
        _, _, sid, land, _ = chunks[q]
        buf, send_sems, _ = stages[sid]
        return pltpu.make_async_remote_copy(src_ref=buf.at[slot], dst_ref=land, send_sem=send_sems.at[slot],
                                            recv_sem=recv_sems.at[q], device_id=sibling, device_id_type=_MESH)

    def receive(q):
        push(q, 0).wait_recv()
        st = pltpu.make_async_copy(chunks[q][3], chunks[q][4], store_sems.at[q])
        st.start()
        stores.append(st)

    for q, (pre, src, sid, _, _) in enumerate(chunks):
        if pre is not None:
            pre()
        slot = used.get(sid, 0) % 2
        used[sid] = used.get(sid, 0) + 1
        if (sid, slot) in in_slot:
            in_slot.pop((sid, slot)).wait_send()
        load = pltpu.make_async_copy(src, stages[sid][0].at[slot], stages[sid][2].at[slot])
        load.start()
        load.wait()
        cp = push(q, slot)
        cp.start()
        in_slot[(sid, slot)] = cp
        if q >= lag:
            receive(q - lag)
    for q in range(max(0, len(chunks) - lag), len(chunks)):
        receive(q)
    for cp in in_slot.values():
        cp.wait_send()
    for st in stores:
        st.wait()


def _stage_scratch(shapes_dtypes):
    out = []
    for shape, dtype in shapes_dtypes:
        out += [pltpu.VMEM((2,) + shape, dtype), pltpu.SemaphoreType.DMA((2,)), pltpu.SemaphoreType.DMA((2,))]
    return out


def allgather_weights(ws, convw, name):
    n = len(ws)
    extra = 0 if convw is None else 1
    halves = [w.shape[0] // 2 for w in ws]
    steps = [_chunk_rows(h, w.shape[1], w.dtype.itemsize) for h, w in zip(halves, ws)]
    per_peer = [h // s for h, s in zip(halves, steps)]
    nchunks = 3 * sum(per_peer)

    def body(*refs):
        w_refs = refs[:n]
        wo_refs = refs[n + extra:2 * n + extra]
        scratch = refs[2 * (n + extra):]
        send_sems, recv_sems, local_sems, fwd_recv_sems, store_sems = scratch[:5]
        lands = scratch[5:5 + n]
        stage_refs = scratch[5 + n:]
        stages = [tuple(stage_refs[3 * i:3 * i + 3]) for i in range(n)]
        x, y, c = _position()
        mine = 2 * x + y
        peers = [(1 - x, y), (x, 1 - y), (1 - x, 1 - y)]
        blocks = [2 * px + py for px, py in peers]
        local = [pltpu.make_async_copy(w_refs[i], wo_refs[i].at[mine], local_sems.at[i]) for i in range(n)]
        if extra:
            c_ref, co_ref = refs[n], refs[2 * n + 1]
            local.append(pltpu.make_async_copy(c_ref, co_ref.at[mine], local_sems.at[n]))
        for cp in local:
            cp.start()

        def ici(i, k, block):
            rows = _rows(c * halves[i], halves[i])
            return pltpu.make_async_remote_copy(src_ref=w_refs[i].at[rows, :], dst_ref=wo_refs[i].at[block, rows, :],
                                                send_sem=send_sems.at[3 * i + k], recv_sem=recv_sems.at[3 * i + k],
                                                device_id=(*peers[k], c), device_id_type=_MESH)

        def conv(k, block):
            return pltpu.make_async_remote_copy(src_ref=c_ref, dst_ref=co_ref.at[block], send_sem=send_sems.at[3 * n + k],
                                                recv_sem=recv_sems.at[3 * n + k], device_id=(*peers[k], c), device_id_type=_MESH)

        sends = [ici(i, k, mine) for k in range(3) for i in range(n)] + ([conv(k, mine) for k in range(3)] if extra else [])
        for cp in sends:
            cp.start()
        chunks = []
        for k in range(3):
            for i in range(n):
                for q in range(per_peer[i]):
                    pre = functools.partial(lambda i, k: ici(i, k, blocks[k]).wait_recv(), i, k) if q == 0 else None
                    src = wo_refs[i].at[blocks[k], _rows(c * halves[i] + q * steps[i], steps[i]), :]
                    out = wo_refs[i].at[blocks[k], _rows((1 - c) * halves[i] + q * steps[i], steps[i]), :]
                    chunks.append((pre, src, i, lands[i].at[k * per_peer[i] + q], out))
        _push_to_sibling(chunks, stages, fwd_recv_sems, store_sems, (x, y, 1 - c))
        if extra:
            for k in range(3):
                conv(k, blocks[k]).wait_recv()
        for cp in sends:
            cp.wait_send()
        for cp in local:
            cp.wait()

    nsem = 3 * (n + extra)
    scratch = [pltpu.SemaphoreType.DMA((nsem,)), pltpu.SemaphoreType.DMA((nsem,)), pltpu.SemaphoreType.DMA((n + extra,)),
               pltpu.SemaphoreType.DMA((nchunks,)), pltpu.SemaphoreType.DMA((nchunks,))]
    scratch += [pltpu.VMEM((3 * p, s, w.shape[1]), w.dtype) for p, s, w in zip(per_peer, steps, ws)]
    scratch += _stage_scratch([((s, w.shape[1]), w.dtype) for s, w in zip(steps, ws)])
    operands = list(ws) + ([convw] if extra else [])
    return pl.pallas_call(
        body, name=name, in_specs=[_HBM] * len(operands), out_specs=[_HBM] * len(operands),
        out_shape=[jax.ShapeDtypeStruct((4,) + w.shape, w.dtype) for w in operands],
        scratch_shapes=scratch, compiler_params=pltpu.CompilerParams(vmem_limit_bytes=VMEM_LIMIT),
    )(*operands)


def exchange_cores(gs, small, name):
    n = len(gs)
    extra = 0 if small is None else 1
    halves = [g.shape[1] // 2 for g in gs]
    steps = [_chunk_rows(h, g.shape[2], g.dtype.itemsize) for h, g in zip(halves, gs)]
    per_shard = [h // s for h, s in zip(halves, steps)]
    nchunks = 4 * sum(per_shard)

    def body(*refs):
        g_refs = refs[:n]
        got_refs = refs[n + extra:2 * n + extra]
        scratch = refs[2 * (n + extra):]
        recv_sems, store_sems = scratch[:2]
        lands = scratch[2:2 + n]
        stage_refs = scratch[2 + n:2 + 4 * n]
        stages = [tuple(stage_refs[3 * i:3 * i + 3]) for i in range(n)]
        x, y, c = _position()
        if extra:
            s_ref, so_ref = refs[n], refs[2 * n + 1]
            tiny_send, tiny_recv, tiny_local = scratch[2 + 4 * n:]
            me = 4 * x + 2 * y + c
            local = pltpu.make_async_copy(s_ref, so_ref.at[me], tiny_local)
            local.start()

            def tiny(r, sending):
                px, py, pc = (1 - x if r & 4 else x, 1 - y if r & 2 else y, 1 - c if r & 1 else c)
                slot = me if sending else 4 * px + 2 * py + pc
                return pltpu.make_async_remote_copy(src_ref=s_ref, dst_ref=so_ref.at[slot], send_sem=tiny_send.at[r - 1],
                                                    recv_sem=tiny_recv.at[r - 1], device_id=(px, py, pc), device_id_type=_MESH)

            sends = [tiny(r, True) for r in range(1, N_DEV)]
            for cp in sends:
                cp.start()
        chunks = []
        for i in range(n):
            for j in range(4):
                for q in range(per_shard[i]):
                    src = g_refs[i].at[j, _rows((1 - c) * halves[i] + q * steps[i], steps[i]), :]
                    out = got_refs[i].at[j, pl.ds(q * steps[i], steps[i]), :]
                    chunks.append((None, src, i, lands[i].at[j * per_shard[i] + q], out))
        _push_to_sibling(chunks, stages, recv_sems, store_sems, (x, y, 1 - c))
        if extra:
            for r in range(1, N_DEV):
                tiny(r, False).wait_recv()
            for cp in sends:
                cp.wait_send()
            local.wait()

    scratch = [pltpu.SemaphoreType.DMA((nchunks,)), pltpu.SemaphoreType.DMA((nchunks,))]
    scratch += [pltpu.VMEM((4 * p, s, g.shape[2]), g.dtype) for p, s, g in zip(per_shard, steps, gs)]
    scratch += _stage_scratch([((s, g.shape[2]), g.dtype) for s, g in zip(steps, gs)])
    out_shape = [jax.ShapeDtypeStruct((4, h, g.shape[2]), g.dtype) for h, g in zip(halves, gs)]
    if extra:
        scratch += [pltpu.SemaphoreType.DMA((N_DEV - 1,)), pltpu.SemaphoreType.DMA((N_DEV - 1,)), pltpu.SemaphoreType.DMA]
        out_shape.append(jax.ShapeDtypeStruct((N_DEV,) + small.shape, small.dtype))
    operands = list(gs) + ([small] if extra else [])
    return pl.pallas_call(
        body, name=name, in_specs=[_HBM] * len(operands), out_specs=[_HBM] * len(out_shape), out_shape=out_shape,
        scratch_shapes=scratch, compiler_params=pltpu.CompilerParams(vmem_limit_bytes=VMEM_LIMIT),
    )(*operands)


def chips_side(cs):
    n = len(cs)

    def copies(c_refs, o_refs, sems):
        send_sems, recv_sems, local_sems = sems
        x, y, c = _position()
        mine = 2 * x + y
        peers = [(1 - x, y), (x, 1 - y), (1 - x, 1 - y)]
        blocks = [2 * px + py for px, py in peers]
        local = [pltpu.make_async_copy(c_refs[i].at[mine], o_refs[i].at[mine], local_sems.at[i]) for i in range(n)]

        def copy(i, k, sending):
            return pltpu.make_async_remote_copy(src_ref=c_refs[i].at[blocks[k]], dst_ref=o_refs[i].at[mine if sending else blocks[k]],
                                                send_sem=send_sems.at[3 * i + k], recv_sem=recv_sems.at[3 * i + k],
                                                device_id=(*peers[k], c), device_id_type=_MESH)

        sends = [copy(i, k, True) for k in range(3) for i in range(n)]
        return local, sends, lambda: [copy(i, k, False) for k in range(3) for i in range(n)]

    def start(*refs):
        local, sends, _ = copies(*refs)
        for cp in local + sends:
            cp.start()

    def finish(*refs):
        local, sends, arrivals = copies(*refs)
        for cp in arrivals():
            cp.wait_recv()
        for cp in sends:
            cp.wait_send()
        for cp in local:
            cp.wait()

    scratch = [pltpu.SemaphoreType.DMA((3 * n,)), pltpu.SemaphoreType.DMA((3 * n,)), pltpu.SemaphoreType.DMA((n,))]
    return Side(list(cs), [jax.ShapeDtypeStruct(a.shape, a.dtype) for a in cs], scratch, start, finish)


def gather_side(ws, convw):
    n = len(ws)
    halves = [w.shape[0] // 2 for w in ws]

    def copies(in_refs, out_refs, sems):
        w_refs, c_ref, wo_refs, co_ref = in_refs[:n], in_refs[n], out_refs[:n], out_refs[n]
        send_sems, recv_sems, local_sems = sems
        x, y, c = _position()
        mine = 2 * x + y
        peers = [(1 - x, y), (x, 1 - y), (1 - x, 1 - y)]
        blocks = [2 * px + py for px, py in peers]
        local = [pltpu.make_async_copy(w_refs[i], wo_refs[i].at[mine], local_sems.at[i]) for i in range(n)]
        local.append(pltpu.make_async_copy(c_ref, co_ref.at[mine], local_sems.at[n]))

        def ici(i, k, block):
            rows = _rows(c * halves[i], halves[i])
            return pltpu.make_async_remote_copy(src_ref=w_refs[i].at[rows, :], dst_ref=wo_refs[i].at[block, rows, :],
                                                send_sem=send_sems.at[3 * i + k], recv_sem=recv_sems.at[3 * i + k],
                                                device_id=(*peers[k], c), device_id_type=_MESH)

        def conv(k, block):
            return pltpu.make_async_remote_copy(src_ref=c_ref, dst_ref=co_ref.at[block], send_sem=send_sems.at[3 * n + k],
                                                recv_sem=recv_sems.at[3 * n + k], device_id=(*peers[k], c), device_id_type=_MESH)

        sends = [ici(i, k, mine) for k in range(3) for i in range(n)] + [conv(k, mine) for k in range(3)]
        return local, sends, lambda: ([ici(i, k, blocks[k]) for k in range(3) for i in range(n)]
                                      + [conv(k, blocks[k]) for k in range(3)])

    def start(*refs):
        local, sends, _ = copies(*refs)
        for cp in local + sends:
            cp.start()

    def finish(*refs):
        local, sends, arrivals = copies(*refs)
        for cp in arrivals():
            cp.wait_recv()
        for cp in sends:
            cp.wait_send()
        for cp in local:
            cp.wait()

    nsem = 3 * n + 3
    scratch = [pltpu.SemaphoreType.DMA((nsem,)), pltpu.SemaphoreType.DMA((nsem,)), pltpu.SemaphoreType.DMA((n + 1,))]
    operands = list(ws) + [convw]
    return Side(operands, [jax.ShapeDtypeStruct((4,) + w.shape, w.dtype) for w in operands], scratch, start, finish)


def forward_halves(stacked):
    n = len(stacked)
    halves = [w.shape[1] // 2 for w in stacked]
    steps = [_chunk_rows(h, w.shape[2], w.dtype.itemsize) for h, w in zip(halves, stacked)]
    per_peer = [h // s for h, s in zip(halves, steps)]
    nchunks = 3 * sum(per_peer)

    def body(*refs):
        w_refs, o_refs = refs[:n], refs[n:2 * n]
        scratch = refs[2 * n:]
        recv_sems, store_sems = scratch[:2]
        lands = scratch[2:2 + n]
        stages = [tuple(scratch[2 + n + 3 * i:2 + n + 3 * i + 3]) for i in range(n)]
        x, y, c = _position()
        blocks = [2 * px + py for px, py in ((1 - x, y), (x, 1 - y), (1 - x, 1 - y))]
        chunks = []
        for k in range(3):
            for i in range(n):
                for q in range(per_peer[i]):
                    src = w_refs[i].at[blocks[k], _rows(c * halves[i] + q * steps[i], steps[i]), :]
                    out = o_refs[i].at[blocks[k], _rows((1 - c) * halves[i] + q * steps[i], steps[i]), :]
                    chunks.append((None, src, i, lands[i].at[k * per_peer[i] + q], out))
        _push_to_sibling(chunks, stages, recv_sems, store_sems, (x, y, 1 - c))

    scratch = [pltpu.SemaphoreType.DMA((nchunks,)), pltpu.SemaphoreType.DMA((nchunks,))]
    scratch += [pltpu.VMEM((3 * p, s, w.shape[2]), w.dtype) for p, s, w in zip(per_peer, steps, stacked)]
    scratch += _stage_scratch([((s, w.shape[2]), w.dtype) for s, w in zip(steps, stacked)])
    return pl.pallas_call(
        body, name="forward_halves", in_specs=[_HBM] * n, out_specs=[_HBM] * n,
        out_shape=[jax.ShapeDtypeStruct(w.shape, w.dtype) for w in stacked], input_output_aliases={i: i for i in range(n)},
        scratch_shapes=scratch, compiler_params=pltpu.CompilerParams(vmem_limit_bytes=VMEM_LIMIT),
    )(*stacked)


def exchange_small(small):
    def body(s_ref, so_ref, send_sems, recv_sems, local_sem):
        x, y, c = _position()
        me = 4 * x + 2 * y + c
        local = pltpu.make_async_copy(s_ref, so_ref.at[me], local_sem)
        local.start()

        def copy(r, sending):
            px, py, pc = (1 - x if r & 4 else x, 1 - y if r & 2 else y, 1 - c if r & 1 else c)
            slot = me if sending else 4 * px + 2 * py + pc
            return pltpu.make_async_remote_copy(src_ref=s_ref, dst_ref=so_ref.at[slot], send_sem=send_sems.at[r - 1],
                                                recv_sem=recv_sems.at[r - 1], device_id=(px, py, pc), device_id_type=_MESH)

        sends = [copy(r, True) for r in range(1, N_DEV)]
        for cp in sends:
            cp.start()
        for r in range(1, N_DEV):
            copy(r, False).wait_recv()
        for cp in sends:
            cp.wait_send()
        local.wait()

    return pl.pallas_call(
        body, name="exchange_small", in_specs=[_HBM], out_specs=_HBM,
        out_shape=jax.ShapeDtypeStruct((N_DEV,) + small.shape, small.dtype),
        scratch_shapes=[pltpu.SemaphoreType.DMA((N_DEV - 1,)), pltpu.SemaphoreType.DMA((N_DEV - 1,)), pltpu.SemaphoreType.DMA],
    )(small)


def pair_sum(core, g, got, name):
    nb, rows, cols = got.shape
    tr = _row_tile(rows, 16, max(16, (2 * 2 ** 20) // (cols * g.dtype.itemsize)))
    nblk = rows // tr

    def body(c_ref, a_ref, b_ref, o_ref):
        o_ref[...] = (a_ref[...].astype(F32) + b_ref[...].astype(F32)).astype(o_ref.dtype)

    spec = pl.BlockSpec((1, tr, cols), lambda j, i, c_ref: (j, i, 0))
    mine = pl.BlockSpec((1, tr, cols), lambda j, i, c_ref: (j, c_ref[0] * nblk + i, 0))
    return pl.pallas_call(
        body, name=name,
        grid_spec=pltpu.PrefetchScalarGridSpec(num_scalar_prefetch=1, grid=(nb, nblk), in_specs=[mine, spec], out_specs=spec),
        out_shape=jax.ShapeDtypeStruct(got.shape, g.dtype), compiler_params=_cparams(("parallel", "parallel")),
    )(core, g, got)


def sum_chips(core, pieces, name):
    nb, rows, cols = pieces.shape
    tr = _row_tile(rows, 16, max(16, (6 * 2 ** 20) // (nb * cols * pieces.dtype.itemsize)))
    nblk = rows // tr

    def body(c_ref, p_ref, o_ref):
        acc = p_ref[0].astype(F32)
        for i in range(1, nb):
            acc = acc + p_ref[i].astype(F32)
        o_ref[0] = acc

    return pl.pallas_call(
        body, name=name,
        grid_spec=pltpu.PrefetchScalarGridSpec(
            num_scalar_prefetch=1, grid=(nblk,),
            in_specs=[pl.BlockSpec((nb, tr, cols), lambda i, c_ref: (0, i, 0))],
            out_specs=pl.BlockSpec((1, tr, cols), lambda i, c_ref: (0, c_ref[0] * nblk + i, 0))),
        out_shape=jax.ShapeDtypeStruct((1, 2 * rows, cols), F32), compiler_params=_cparams(("parallel",)),
    )(core, pieces)


def sibling_exchange(fulls):
    n = len(fulls)
    halves = [f.shape[1] // 2 for f in fulls]
    steps = [_chunk_rows(h, f.shape[2], f.dtype.itemsize) for h, f in zip(halves, fulls)]
    counts = [h // s for h, s in zip(halves, steps)]
    nchunks = sum(counts)

    def body(*refs):
        f_refs, o_refs = refs[:n], refs[n:2 * n]
        scratch = refs[2 * n:]
        recv_sems, store_sems = scratch[:2]
        lands = scratch[2:2 + n]
        stages = [tuple(scratch[2 + n + 3 * i:2 + n + 3 * i + 3]) for i in range(n)]
        x, y, c = _position()
        chunks = []
        for i in range(n):
            for q in range(counts[i]):
                src = f_refs[i].at[0, _rows(c * halves[i] + q * steps[i], steps[i]), :]
                out = o_refs[i].at[0, _rows((1 - c) * halves[i] + q * steps[i], steps[i]), :]
                chunks.append((None, src, i, lands[i].at[q], out))
        _push_to_sibling(chunks, stages, recv_sems, store_sems, (x, y, 1 - c))

    scratch = [pltpu.SemaphoreType.DMA((nchunks,)), pltpu.SemaphoreType.DMA((nchunks,))]
    scratch += [pltpu.VMEM((k, s, f.shape[2]), f.dtype) for k, s, f in zip(counts, steps, fulls)]
    scratch += _stage_scratch([((s, f.shape[2]), f.dtype) for s, f in zip(steps, fulls)])
    return pl.pallas_call(
        body, name="sibling_exchange", in_specs=[_HBM] * n, out_specs=[_HBM] * n,
        out_shape=[jax.ShapeDtypeStruct(f.shape, f.dtype) for f in fulls],
        input_output_aliases={i: i for i in range(n)},
        scratch_shapes=scratch, compiler_params=pltpu.CompilerParams(vmem_limit_bytes=VMEM_LIMIT),
    )(*fulls)


def _row_tile(rows, unit, max_rows):
    best = unit
    for t in range(unit, min(rows, max_rows) + 1, unit):
        if rows % t == 0:
            best = t
    return best


def sum_pieces(pieces, name):
    n, rows, cols = pieces.shape
    tr = _row_tile(rows, 16, max(16, (6 * 2 ** 20) // (n * cols * pieces.dtype.itemsize)))

    def body(p_ref, o_ref):
        acc = p_ref[0].astype(F32)
        for i in range(1, n):
            acc = acc + p_ref[i].astype(F32)
        o_ref[...] = acc

    return pl.pallas_call(
        body, name=name, grid=(rows // tr,),
        in_specs=[pl.BlockSpec((n, tr, cols), lambda i: (0, i, 0))], out_specs=pl.BlockSpec((tr, cols), lambda i: (i, 0)),
        out_shape=jax.ShapeDtypeStruct((rows, cols), F32), compiler_params=_cparams(("parallel",)),
    )(pieces)


BIG = ("w_in", "s5_w_glu", "w_kv_mem", "w_br_a", "w_br_b", "w_br_c", "w_out")
COL_SHARDED = ("w_in", "s5_w_glu", "w_br_a", "w_br_b", "w_br_c")
SMALL = ("norm_g", "gdn_a_log", "gdn_dt_bias", "gdn_norm_g", "s5_lambda_re", "s5_lambda_im", "s5_log_dt",
         "s5_b_re", "s5_b_im", "s5_c_re", "s5_c_im", "s5_d", "mem_norm_g", "final_g")
WEIGHTS = ("norm_g", "w_in", "conv_w", "gdn_a_log", "gdn_dt_bias", "gdn_norm_g", "s5_lambda_re", "s5_lambda_im",
           "s5_log_dt", "s5_b_re", "s5_b_im", "s5_c_re", "s5_c_im", "s5_d", "s5_w_glu", "mem_norm_g", "w_kv_mem",
           "w_br_a", "w_br_b", "w_br_c", "w_out", "final_g")
W_IN_SPLIT = 4096


W_IN_COLS = PROJ_W - BA_PAD + 2 * NHEAD
W_IN_GATES = W_IN_COLS - GATE_W
W_IN_MOVES = ((0, W_IN_SPLIT, GATE_W), (W_IN_SPLIT, W_IN_SPLIT + 2 * NHEAD, PROJ_W - BA_PAD - W_IN_SPLIT),
              (W_IN_SPLIT + 2 * NHEAD, W_IN_GATES, GATE_W - 2 * NHEAD), (W_IN_GATES, W_IN_COLS, -W_IN_GATES))


def _pack_w_in(shards):
    cs = shards.shape[2]
    parts = []
    for a, b, _ in sorted(W_IN_MOVES, key=lambda move: move[0] + move[2]):
        while a < b:
            j = a // cs
            hi = min(b, (j + 1) * cs)
            parts.append(shards[j, :, a - j * cs:hi - j * cs])
            a = hi
    parts.append(jnp.zeros((shards.shape[1], BA_PAD - 2 * NHEAD), shards.dtype))
    return jnp.concatenate(parts, axis=1)


def _unpack_w_in(wp):
    cs = W_IN_COLS // 4
    shards = []
    for j in range(4):
        parts = []
        for lo, hi, shift in W_IN_MOVES:
            s, e = max(j * cs, lo), min((j + 1) * cs, hi)
            if s < e:
                parts.append(wp[:, s + shift:e + shift])
        shards.append(jnp.concatenate(parts, axis=1))
    return jnp.stack(shards)


def _pack_small(arrs):
    parts = []
    for a in arrs:
        f = a.reshape(-1).astype(F32)
        parts.append(jnp.pad(f, (0, (-f.shape[0]) % 128)))
    flat = jnp.concatenate(parts)
    rows = flat.shape[0] // 128
    return jnp.pad(flat.reshape(rows, 128), ((0, (-rows) % 16), (0, 0)))


def _unpack_small(flat2d, shapes):
    f = flat2d.reshape(-1)
    out, off = [], 0
    for shp in shapes:
        n = math.prod(shp)
        out.append(f[off:off + n].reshape(shp))
        off += n + (-n) % 128
    return out


def kernel(x, mem, norm_g, w_in, conv_w, gdn_a_log, gdn_dt_bias, gdn_norm_g, s5_lambda_re, s5_lambda_im, s5_log_dt, s5_b_re, s5_b_im, s5_c_re, s5_c_im, s5_d, s5_w_glu, mem_norm_g, w_kv_mem, w_br_a, w_br_b, w_br_c, w_out, final_g, loss_target, m_norm_g, m_w_in, m_conv_w, m_gdn_a_log, m_gdn_dt_bias, m_gdn_norm_g, m_s5_lambda_re, m_s5_lambda_im, m_s5_log_dt, m_s5_b_re, m_s5_b_im, m_s5_c_re, m_s5_c_im, m_s5_d, m_s5_w_glu, m_mem_norm_g, m_w_kv_mem, m_w_br_a, m_w_br_b, m_w_br_c, m_w_out, m_final_g, v_norm_g, v_w_in, v_conv_w, v_gdn_a_log, v_gdn_dt_bias, v_gdn_norm_g, v_s5_lambda_re, v_s5_lambda_im, v_s5_log_dt, v_s5_b_re, v_s5_b_im, v_s5_c_re, v_s5_c_im, v_s5_d, v_s5_w_glu, v_mem_norm_g, v_w_kv_mem, v_w_br_a, v_w_br_b, v_w_br_c, v_w_out, v_final_g):
    wts = dict(norm_g=norm_g, w_in=w_in, conv_w=conv_w, gdn_a_log=gdn_a_log, gdn_dt_bias=gdn_dt_bias, gdn_norm_g=gdn_norm_g,
               s5_lambda_re=s5_lambda_re, s5_lambda_im=s5_lambda_im, s5_log_dt=s5_log_dt, s5_b_re=s5_b_re, s5_b_im=s5_b_im,
               s5_c_re=s5_c_re, s5_c_im=s5_c_im, s5_d=s5_d, s5_w_glu=s5_w_glu, mem_norm_g=mem_norm_g, w_kv_mem=w_kv_mem,
               w_br_a=w_br_a, w_br_b=w_br_b, w_br_c=w_br_c, w_out=w_out, final_g=final_g)
    mom = dict(norm_g=m_norm_g, w_in=m_w_in, conv_w=m_conv_w, gdn_a_log=m_gdn_a_log, gdn_dt_bias=m_gdn_dt_bias,
               gdn_norm_g=m_gdn_norm_g, s5_lambda_re=m_s5_lambda_re, s5_lambda_im=m_s5_lambda_im, s5_log_dt=m_s5_log_dt,
               s5_b_re=m_s5_b_re, s5_b_im=m_s5_b_im, s5_c_re=m_s5_c_re, s5_c_im=m_s5_c_im, s5_d=m_s5_d, s5_w_glu=m_s5_w_glu,
               mem_norm_g=m_mem_norm_g, w_kv_mem=m_w_kv_mem, w_br_a=m_w_br_a, w_br_b=m_w_br_b, w_br_c=m_w_br_c, w_out=m_w_out,
               final_g=m_final_g)
    vel = dict(norm_g=v_norm_g, w_in=v_w_in, conv_w=v_conv_w, gdn_a_log=v_gdn_a_log, gdn_dt_bias=v_gdn_dt_bias,
               gdn_norm_g=v_gdn_norm_g, s5_lambda_re=v_s5_lambda_re, s5_lambda_im=v_s5_lambda_im, s5_log_dt=v_s5_log_dt,
               s5_b_re=v_s5_b_re, s5_b_im=v_s5_b_im, s5_c_re=v_s5_c_re, s5_c_im=v_s5_c_im, s5_d=v_s5_d, s5_w_glu=v_s5_w_glu,
               mem_norm_g=v_mem_norm_g, w_kv_mem=v_w_kv_mem, w_br_a=v_w_br_a, w_br_b=v_w_br_b, w_br_c=v_w_br_c, w_out=v_w_out,
               final_g=v_final_g)
    x2, mem2, tgt = x[0], mem[0], loss_target[0]
    s, d = x2.shape
    n_chunks = s // CHUNK

    shards = [wts[n][0].astype(BF16) for n in BIG]
    wp = _pack_w_in(allgather_weights(shards[:1], None, "allgather_w_in")[0])
    mm = functools.partial(matmul, tm=1024, tn=1024)
    u, r1 = rms_fwd(x2, norm_g, "rms_fwd_x")
    proj, *rest, cg = mm(u, wp, mode="nn", out_dtype=F32, tk=2048, name="mm_proj", side=gather_side(shards[1:], conv_w[0]))
    full = {}
    for n, wg in zip(BIG[1:], forward_halves(rest)):
        rows, cols = wg.shape[1:]
        full[n] = wg.transpose(1, 0, 2).reshape(rows, 4 * cols) if n in COL_SHARDED else wg.reshape(4 * rows, cols)
    conv_full = cg.transpose(1, 0, 2).reshape(conv_w.shape[1], -1)
    alog_pad = jnp.pad(gdn_a_log, ((0, 0), (NHEAD, BA_W - 2 * NHEAD)))
    dt_pad = jnp.pad(gdn_dt_bias, ((0, 0), (NHEAD, BA_W - 2 * NHEAD)))

    q, k, v, bg, gcol, gt = gdn_prep_fwd(proj, conv_full, alog_pad, dt_pad)
    gt3 = gt.reshape(BA_W, n_chunks, CHUNK).transpose(1, 0, 2)
    gu, gw, qd, kd, qk, tinv = gdn_intra_fwd(q, k, v, bg, gcol, gt3)
    o_raw, states = gdn_seq_fwd(gu, gw, qd, kd, qk, gt3)
    ga = gdn_out_fwd(o_raw, proj, ZA_CB, gdn_norm_g)

    xb = proj[:, XB_CB * S5_IN:(XB_CB + 1) * S5_IN]
    y_ssm, s5_saved = s5_ssm_fwd(xb, s5_lambda_re[0], s5_lambda_im[0], s5_log_dt[0], s5_b_re[0], s5_b_im[0],
                                 s5_c_re[0], s5_c_im[0])
    yb = s5_act_fwd(y_ssm, proj, XB_CB, s5_d)
    glu = mm(yb, full["s5_w_glu"], mode="nn", out_dtype=F32, tk=1024, name="mm_glu")
    gb = s5_glu_fwd(glu, proj, ZB_CB)

    mem_n, rm = rms_fwd(mem2, mem_norm_g, "rms_fwd_mem")
    kv = mm(mem_n, full["w_kv_mem"], mode="nn", out_dtype=BF16, tk=2048, name="mm_kv")
    o_c = xa_fwd(proj, kv)
    gcx = gate_fwd(o_c, proj, ZC_CB, "gate_fwd_c")

    pa = mm(ga, full["w_br_a"], mode="nn", out_dtype=F32, tk=1024, name="mm_pa")
    pb = mm(gb, full["w_br_b"], mode="nn", out_dtype=F32, tk=1024, name="mm_pb")
    pc = mm(gcx, full["w_br_c"], mode="nn", out_dtype=F32, tk=1024, name="mm_pc")
    merged = merge_fwd(pa, pb, pc, proj, GATE_CB)
    hres = mm(merged, full["w_out"], mode="nn", out_dtype=F32, tk=2048, name="mm_out")
    dh, dhb, loss_part, d_final_g = final_stage(x2, hres, tgt, final_g.reshape(1, d))

    gfull = {}
    dmerged = mm(dhb, full["w_out"], mode="nt", out_dtype=F32, tk=2048, name="mm_dmerged")
    gfull["w_out"] = mm(merged, dhb, mode="tn", out_dtype=BF16, tk=1024, name="mm_dw_out")
    dproj = lax.empty((s, PROJ_W), BF16)
    dpa, dpb, dpc, dproj = merge_bwd(dmerged, pa, pb, pc, proj, GATE_CB, dproj)
    dga = mm(dpa, full["w_br_a"], mode="nt", out_dtype=F32, tk=2048, name="mm_dga")
    dgb = mm(dpb, full["w_br_b"], mode="nt", out_dtype=F32, tk=2048, name="mm_dgb")
    dgc = mm(dpc, full["w_br_c"], mode="nt", out_dtype=F32, tk=2048, name="mm_dgc")
    gfull["w_br_a"] = mm(ga, dpa, mode="tn", out_dtype=BF16, tk=1024, name="mm_dw_a")
    gfull["w_br_b"] = mm(gb, dpb, mode="tn", out_dtype=BF16, tk=1024, name="mm_dw_b")
    gfull["w_br_c"] = mm(gcx, dpc, mode="tn", out_dtype=BF16, tk=1024, name="mm_dw_c")

    do_raw, dproj, d_gdn_norm = gdn_out_bwd(dga, o_raw, proj, ZA_CB, gdn_norm_g, dproj)
    du_, dw_, dqd, dkd, dqk, dgl = gdn_seq_bwd(do_raw, gu, gw, qd, kd, qk, gt3, states)
    dq, dk, dv, dbg = gdn_intra_bwd(q, k, v, bg, gcol, gt3, tinv, du_, dw_, dqd, dkd, dqk, dgl)
    dc, dproj, dcw0, dcw1, dcw2, dcw3, d_alog, d_dt = gdn_prep_bwd1(proj, conv_full, alog_pad, dt_pad, dq, dk, dv, dbg, dproj)
    dproj = gdn_prep_bwd2(dc, conv_full, dproj)
    d_conv = jnp.concatenate([dcw0, dcw1, dcw2, dcw3], axis=0)

    dval, dgate, dproj = s5_glu_bwd(dgb, glu, proj, ZB_CB, dproj)
    dglu = jnp.concatenate([dval, dgate], axis=1)
    dyb = mm(dglu, full["s5_w_glu"], mode="nt", out_dtype=F32, tk=2048, name="mm_dyb")
    gfull["s5_w_glu"] = mm(yb, dglu, mode="tn", out_dtype=BF16, tk=1024, name="mm_dw_glu")
    dy_ssm, dxb_direct, d_s5_d = s5_act_bwd(dyb, y_ssm, proj, XB_CB, s5_d)
    dxb_scan, d_lre, d_lim, d_ldt, d_bre, d_bim, d_cre, d_cim = s5_ssm_bwd(dy_ssm, s5_saved)
    dproj = add_into(dxb_direct, dxb_scan, "s5_dxb", dproj, XB_CB)

    do_c, dproj = gate_bwd(dgc, o_c, proj, ZC_CB, "gate_bwd_c", dproj)
    dproj, dkv = xa_bwd(do_c, proj, kv, dproj)
    gfull["w_kv_mem"] = mm(mem_n, dkv, mode="tn", out_dtype=BF16, tk=256, name="mm_dw_kv")
    dmem_n = mm(dkv, full["w_kv_mem"], mode="nt", out_dtype=F32, tk=2048, name="mm_dmem")
    d_mem_norm = rms_bwd_g(dmem_n, mem2, rm, "rms_bwd_mem")

    core = lax.axis_index("c").astype(jnp.int32).reshape(1)
    by_shard = []
    for n in BIG[1:]:
        rows, cols = wts[n].shape[1:]
        g = gfull[n]
        by_shard.append(g.reshape(rows, 4, cols).transpose(1, 0, 2) if n in COL_SHARDED else g.reshape(4, rows, cols))
    got_rest = exchange_cores(by_shard, None, "exchange_cores_rest")
    chip_rest = [pair_sum(core, g, r, "sum_cores_" + n) for n, g, r in zip(BIG[1:], by_shard, got_rest)]

    dwp, *from_chips_rest = matmul(u.T, dproj, mode="nn", out_dtype=BF16, tm=2048, tn=1024, tk=1024, name="mm_dw_in",
                                   side=chips_side(chip_rest))
    w_in_shards = _unpack_w_in(dwp)
    got_in, = exchange_cores([w_in_shards], None, "exchange_cores_w_in")
    chip_in = pair_sum(core, w_in_shards, got_in, "sum_cores_w_in")
    du, from_chips_in = matmul(dproj, wp, mode="nt", out_dtype=F32, tm=2048, tn=1024, tk=512, name="mm_du",
                               side=chips_side([chip_in]))
    grad_x, d_norm_g = rms_bwd_x(du, x2, r1, norm_g, dh)
    from_chips = [from_chips_in] + from_chips_rest
    small_g = dict(norm_g=d_norm_g, gdn_a_log=d_alog[:, NHEAD:2 * NHEAD], gdn_dt_bias=d_dt[:, NHEAD:2 * NHEAD],
                   gdn_norm_g=d_gdn_norm, s5_lambda_re=d_lre, s5_lambda_im=d_lim, s5_log_dt=d_ldt, s5_b_re=d_bre, s5_b_im=d_bim,
                   s5_c_re=d_cre, s5_c_im=d_cim, s5_d=d_s5_d, mem_norm_g=d_mem_norm, final_g=d_final_g)
    small_send = _pack_small([small_g[n] for n in SMALL] + [d_conv, loss_part])
    fulls = [sum_chips(core, a, "sum_chips_" + n) for n, a in zip(BIG, from_chips)]
    small_sum = sum_pieces(exchange_small(small_send), "sum_small")
    grads = dict(zip(BIG, sibling_exchange(fulls)))
    small_shapes = [wts[n].shape for n in SMALL] + [d_conv.shape, (1, 1)]
    *small_list, conv_g_full, loss_sum = _unpack_small(small_sum, small_shapes)
    grads.update(zip(SMALL, small_list))
    cw = conv_w.shape[2]
    shard_idx = 2 * lax.axis_index("x") + lax.axis_index("y")
    grads["conv_w"] = lax.dynamic_slice(conv_g_full, (0, shard_idx * cw), (conv_w.shape[1], cw))[None]

    delta, new_m, new_v = {}, {}, {}
    for n in BIG + ("conv_w",):
        delta[n], new_m[n], new_v[n] = adamw(wts[n], grads[n], mom[n], vel[n], "adamw_" + n)
    packed = [_pack_small([src[n] for n in SMALL]) for src in (wts, grads, mom, vel)]
    res = adamw(*packed, "adamw_small")
    shapes = [wts[n].shape for n in SMALL]
    for dst, flat in zip((delta, new_m, new_v), res):
        dst.update(zip(SMALL, _unpack_small(flat, shapes)))
    for n in SMALL:
        grads[n] = grads[n].reshape(wts[n].shape)

    return (loss_sum.reshape(()), grad_x.reshape(x.shape), *[grads[n] for n in WEIGHTS], *[delta[n] for n in WEIGHTS],
            *[new_m[n] for n in WEIGHTS], *[new_v[n] for n in WEIGHTS])
```

```python
import functools
import math

import jax
import jax.numpy as jnp
from jax import lax
from jax.experimental import pallas as pl
from jax.experimental.pallas import tpu as pltpu

F32 = jnp.float32
BF16 = jnp.bfloat16
HI = lax.Precision.HIGHEST

EPS = 1e-6
CHUNK = 64
HEAD = 128
NHEAD = 8
XA_HEADS = 4
S5_GROUPS = 64
S5_STATE = 64
S5_GROUP = 16
NSEG = 8
ADAM_LR, ADAM_B1, ADAM_B2, ADAM_EPS, ADAM_WD, ADAM_STEP = 0.001, 0.9, 0.999, 1e-08, 0.01, 10
VMEM_LIMIT = 56 * 2 ** 20


def _cparams(sem=None):
    return pltpu.CompilerParams(dimension_semantics=sem, vmem_limit_bytes=VMEM_LIMIT)


def _sigmoid(x):
    return 1.0 / (1.0 + jnp.exp(-x))


def _silu(x):
    return x * _sigmoid(x)


def _dsilu(x):
    s = _sigmoid(x)
    return s * (1.0 + x * (1.0 - s))


def _softplus(x):
    return jnp.maximum(x, 0.0) + jnp.log(1.0 + jnp.exp(-jnp.abs(x)))


_GELU_C = math.sqrt(2.0 / math.pi)


def _gelu(x):
    return 0.5 * x * (1.0 + jnp.tanh(_GELU_C * (x + 0.044715 * x * x * x)))


def _dgelu(x):
    t = jnp.tanh(_GELU_C * (x + 0.044715 * x * x * x))
    return 0.5 * (1.0 + t) + 0.5 * x * (1.0 - t * t) * _GELU_C * (1.0 + 3.0 * 0.044715 * x * x)


_DIMS = {"nn": (((1,), (0,)), ((), ())), "nt": (((1,), (1,)), ((), ())), "tn": (((0,), (0,)), ((), ()))}


class Side:
    def __init__(self, operands, out_shapes, scratch, start, finish):
        self.operands, self.out_shapes, self.scratch, self.start, self.finish = operands, out_shapes, scratch, start, finish


def matmul(a, b, *, mode, out_dtype, tm, tn, tk, name, side=None):
    if mode == "nn":
        (m, k), n = a.shape, b.shape[1]
    elif mode == "nt":
        (m, k), n = a.shape, b.shape[0]
    else:
        (k, m), n = a.shape, b.shape[1]
    tm, tn, tk = min(tm, m), min(tn, n), min(tk, k)
    assert m % tm == 0 and n % tn == 0 and k % tk == 0, (name, m, n, k, tm, tn, tk)
    grid = (m // tm, n // tn, k // tk)
    nk = grid[2]
    dims = _DIMS[mode]
    n_in = 0 if side is None else len(side.operands)
    n_out = 0 if side is None else len(side.out_shapes)
    n_acc = 0 if nk == 1 else 1

    def body(*refs):
        a_ref, b_ref, o_ref = refs[0], refs[1], refs[2 + n_in]
        scratch = refs[3 + n_in + n_out:]
        side_refs = (refs[2:2 + n_in], refs[3 + n_in:3 + n_in + n_out], scratch[n_acc:])
        ids = [pl.program_id(d) for d in range(3)]
        if side is not None:
            @pl.when((ids[0] == 0) & (ids[1] == 0) & (ids[2] == 0))
            def _():
                side.start(*side_refs)

        prod = lax.dot_general(a_ref[...].astype(BF16), b_ref[...].astype(BF16), dims, preferred_element_type=F32)
        if nk == 1:
            o_ref[...] = prod.astype(out_dtype)
        else:
            acc_ref = scratch[0]

            @pl.when(ids[2] == 0)
            def _():
                acc_ref[...] = prod

            @pl.when(ids[2] > 0)
            def _():
                acc_ref[...] += prod

            @pl.when(ids[2] == nk - 1)
            def _():
                o_ref[...] = acc_ref[...].astype(out_dtype)

        if side is not None:
            @pl.when((ids[0] == grid[0] - 1) & (ids[1] == grid[1] - 1) & (ids[2] == nk - 1))
            def _():
                side.finish(*side_refs)

    a_spec = pl.BlockSpec((tk, tm), lambda i, j, q: (q, i)) if mode == "tn" else pl.BlockSpec((tm, tk), lambda i, j, q: (i, q))
    b_spec = pl.BlockSpec((tn, tk), lambda i, j, q: (j, q)) if mode == "nt" else pl.BlockSpec((tk, tn), lambda i, j, q: (q, j))
    o_spec = pl.BlockSpec((tm, tn), lambda i, j, q: (i, j))
    o_shape = jax.ShapeDtypeStruct((m, n), out_dtype)
    acc = [] if nk == 1 else [pltpu.VMEM((tm, tn), F32)]
    if side is None:
        return pl.pallas_call(
            body, name=name, grid=grid, in_specs=[a_spec, b_spec], out_specs=o_spec, out_shape=o_shape, scratch_shapes=acc,
            compiler_params=_cparams(("parallel", "parallel", "arbitrary")),
        )(a, b)
    hbm = pl.BlockSpec(memory_space=pltpu.HBM)
    return pl.pallas_call(
        body, name=name, grid=grid, in_specs=[a_spec, b_spec] + [hbm] * n_in, out_specs=[o_spec] + [hbm] * n_out,
        out_shape=[o_shape] + list(side.out_shapes), scratch_shapes=acc + list(side.scratch),
        compiler_params=_cparams(("arbitrary", "arbitrary", "arbitrary")),
    )(a, b, *side.operands)


def rowwise(fn, ins, outs, *, rows, tr, name, consts=(), reds=(), into=None):
    tr = min(tr, rows)
    assert rows % tr == 0, (name, rows, tr)
    n_in, n_c, n_o = len(ins), len(consts), len(outs)
    n_buf = 0 if into is None else 1

    def body(*refs):
        vals = [r[...] for r in refs[:n_in + n_c]]
        res = fn(*vals)
        o_refs = refs[n_in + n_c + n_buf:]
        for r, v in zip(o_refs[:n_o], res[:n_o]):
            r[...] = v.astype(r.dtype)
        if reds:
            i = pl.program_id(0)

            @pl.when(i == 0)
            def _():
                for r, v in zip(o_refs[n_o:], res[n_o:]):
                    r[...] = v.astype(r.dtype)

            @pl.when(i > 0)
            def _():
                for r, v in zip(o_refs[n_o:], res[n_o:]):
                    r[...] += v.astype(r.dtype)

    in_specs = [pl.BlockSpec((tr, w), functools.partial(lambda i, cb: (i, cb), cb=cb)) for (_, w, cb) in ins]
    in_specs += [pl.BlockSpec(c.shape, lambda i: (0, 0)) for c in consts]
    out_specs = [pl.BlockSpec((tr, w), lambda i: (i, 0)) for (w, _) in outs]
    out_specs += [pl.BlockSpec(s, lambda i: (0, 0)) for (s, _) in reds]
    out_shape = [jax.ShapeDtypeStruct((rows, w), d) for (w, d) in outs]
    out_shape += [jax.ShapeDtypeStruct(s, d) for (s, d) in reds]
    operands = [a for (a, _, _) in ins] + list(consts)
    aliases = {}
    if into is not None:
        buf, pos, cb = into
        assert buf.dtype == outs[pos][1] and buf.shape[0] == rows, (name, buf.shape, buf.dtype)
        in_specs.append(pl.BlockSpec(memory_space=pl.ANY))
        operands.append(buf)
        out_specs[pos] = pl.BlockSpec((tr, outs[pos][0]), functools.partial(lambda i, cb: (i, cb), cb=cb))
        out_shape[pos] = jax.ShapeDtypeStruct(buf.shape, buf.dtype)
        aliases = {len(operands) - 1: pos}
    return pl.pallas_call(
        body, name=name, grid=(rows // tr,), in_specs=in_specs, out_specs=out_specs, out_shape=out_shape,
        input_output_aliases=aliases, compiler_params=_cparams(("arbitrary",) if reds else ("parallel",)),
    )(*operands)


def _colsum(x):
    return jnp.sum(x, axis=0, keepdims=True)


def rms_fwd(x, g, name):
    s, d = x.shape

    def fn(xv, gv):
        r = lax.rsqrt(jnp.mean(xv * xv, axis=-1, keepdims=True) + EPS)
        return xv * r * gv, r

    return rowwise(fn, [(x, d, 0)], [(d, BF16), (1, F32)], rows=s, tr=256, name=name, consts=[g])


def rms_bwd_x(du, x, r, g, dh):
    s, d = x.shape

    def fn(duv, xv, rv, dhv, gv):
        dyg = duv * gv
        dx = rv * dyg - xv * (rv * rv * rv) * jnp.mean(dyg * xv, axis=-1, keepdims=True)
        return dhv + dx, _colsum(duv * xv * rv)

    return rowwise(fn, [(du, d, 0), (x, d, 0), (r, 1, 0), (dh, d, 0)], [(d, F32)], rows=s, tr=256,
                   name="rms_bwd_x", consts=[g], reds=[((1, d), F32)])


def rms_bwd_g(du, x, r, name):
    s, d = x.shape

    def fn(duv, xv, rv):
        return (_colsum(duv * xv * rv),)

    return rowwise(fn, [(du, d, 0), (x, d, 0), (r, 1, 0)], [], rows=s, tr=256, name=name, reds=[((1, d), F32)])[0]


def final_stage(x, hres, target, g):
    s, d = x.shape

    def fn(xv, hv, tv, gv):
        h = xv + hv
        r = lax.rsqrt(jnp.mean(h * h, axis=-1, keepdims=True) + EPS)
        y = h * r * gv
        e = y - tv
        loss = 0.5 * jnp.sum(jnp.sum(e * e, axis=-1, keepdims=True), axis=0, keepdims=True) / d
        dy = e / d
        dyg = dy * gv
        dh = r * dyg - h * (r * r * r) * jnp.mean(dyg * h, axis=-1, keepdims=True)
        return dh, dh, loss, _colsum(dy * h * r)

    return rowwise(fn, [(x, d, 0), (hres, d, 0), (target, d, 0)], [(d, F32), (d, BF16)], rows=s, tr=256,
                   name="final_stage", consts=[g], reds=[((1, 1), F32), ((1, d), F32)])


def merge_fwd(pa, pb, pc, proj, gate_cb):
    s, d = pa.shape

    def fn(a, b, c, g0, g1, g2):
        return (_sigmoid(g0) * a + _sigmoid(g1) * b + _sigmoid(g2) * c,)

    ins = [(pa, d, 0), (pb, d, 0), (pc, d, 0)] + [(proj, d, gate_cb + i) for i in range(3)]
    return rowwise(fn, ins, [(d, BF16)], rows=s, tr=256, name="merge_fwd")[0]


def merge_bwd(dm, pa, pb, pc, proj, gate_cb, dproj):
    s, d = pa.shape

    def fn(dmv, a, b, c, g0, g1, g2):
        s0, s1, s2 = _sigmoid(g0), _sigmoid(g1), _sigmoid(g2)
        dgates = [dmv * a * s0 * (1.0 - s0), dmv * b * s1 * (1.0 - s1), dmv * c * s2 * (1.0 - s2)]
        return dmv * s0, dmv * s1, dmv * s2, jnp.concatenate(dgates, axis=1)

    ins = [(dm, d, 0), (pa, d, 0), (pb, d, 0), (pc, d, 0)] + [(proj, d, gate_cb + i) for i in range(3)]
    return rowwise(fn, ins, [(d, BF16)] * 3 + [(3 * d, BF16)], rows=s, tr=128, name="merge_bwd", into=(dproj, 3, gate_cb // 3))


def gate_fwd(o, proj, z_cb, name):
    s, w = o.shape

    def fn(ov, zv):
        return (ov * _silu(zv),)

    return rowwise(fn, [(o, w, 0), (proj, w, z_cb)], [(w, BF16)], rows=s, tr=512, name=name)[0]


def gate_bwd(dgo, o, proj, z_cb, name, dproj):
    s, w = o.shape

    def fn(dv, ov, zv):
        return dv * _silu(zv), dv * ov * _dsilu(zv)

    return rowwise(fn, [(dgo, w, 0), (o, w, 0), (proj, w, z_cb)], [(w, F32), (w, BF16)], rows=s, tr=512, name=name,
                   into=(dproj, 1, z_cb))


def gdn_out_fwd(o_raw, proj, z_cb, gn):
    s, w = o_raw.shape

    def fn(ov, zv, gv):
        outs = []
        for h in range(NHEAD):
            oh = ov[:, h * HEAD:(h + 1) * HEAD]
            r = lax.rsqrt(jnp.mean(oh * oh, axis=-1, keepdims=True) + EPS)
            outs.append(oh * r * gv)
        return (jnp.concatenate(outs, axis=1) * _silu(zv),)

    return rowwise(fn, [(o_raw, w, 0), (proj, w, z_cb)], [(w, BF16)], rows=s, tr=512, name="gdn_out_fwd", consts=[gn])[0]


def gdn_out_bwd(dga, o_raw, proj, z_cb, gn, dproj):
    s, w = o_raw.shape

    def fn(dv, ov, zv, gv):
        sz, dsz = _silu(zv), _dsilu(zv)
        do_l, dz_l = [], []
        dg = jnp.zeros((1, HEAD), F32)
        for h in range(NHEAD):
            sl = slice(h * HEAD, (h + 1) * HEAD)
            oh, dgh = ov[:, sl], dv[:, sl]
            r = lax.rsqrt(jnp.mean(oh * oh, axis=-1, keepdims=True) + EPS)
            on = oh * r * gv
            don = dgh * sz[:, sl]
            dz_l.append(dgh * on * dsz[:, sl])
            dg = dg + _colsum(don * oh * r)
            dyg = don * gv
            do_l.append(r * dyg - oh * (r * r * r) * jnp.mean(dyg * oh, axis=-1, keepdims=True))
        return jnp.concatenate(do_l, axis=1), jnp.concatenate(dz_l, axis=1), dg

    return rowwise(fn, [(dga, w, 0), (o_raw, w, 0), (proj, w, z_cb)], [(w, F32), (w, BF16)], rows=s, tr=512,
                   name="gdn_out_bwd", consts=[gn], reds=[((1, HEAD), F32)], into=(dproj, 1, z_cb))


def s5_act_fwd(y_ssm, proj, xb_cb, dvec):
    s, w = y_ssm.shape

    def fn(yv, xv, dv):
        return (_gelu(yv + dv * xv),)

    return rowwise(fn, [(y_ssm, w, 0), (proj, w, xb_cb)], [(w, BF16)], rows=s, tr=512, name="s5_act_fwd", consts=[dvec])[0]


def s5_act_bwd(dyb, y_ssm, proj, xb_cb, dvec):
    s, w = y_ssm.shape

    def fn(dv_, yv, xv, dv):
        dpre = dv_ * _dgelu(yv + dv * xv)
        return dpre, dpre * dv, _colsum(dpre * xv)

    return rowwise(fn, [(dyb, w, 0), (y_ssm, w, 0), (proj, w, xb_cb)], [(w, F32), (w, F32)], rows=s, tr=512,
                   name="s5_act_bwd", consts=[dvec], reds=[((1, w), F32)])


def s5_glu_fwd(glu, proj, z_cb):
    s, w2 = glu.shape
    w = w2 // 2

    def fn(val, gate, zv):
        return (val * _sigmoid(gate) * _silu(zv),)

    return rowwise(fn, [(glu, w, 0), (glu, w, 1), (proj, w, z_cb)], [(w, BF16)], rows=s, tr=512, name="s5_glu_fwd")[0]


def s5_glu_bwd(dgb, glu, proj, z_cb, dproj):
    s, w2 = glu.shape
    w = w2 // 2

    def fn(dv, val, gate, zv):
        sg = _sigmoid(gate)
        ob = val * sg
        dob = dv * _silu(zv)
        return dob * sg, dob * val * sg * (1.0 - sg), dv * ob * _dsilu(zv)

    return rowwise(fn, [(dgb, w, 0), (glu, w, 0), (glu, w, 1), (proj, w, z_cb)], [(w, BF16)] * 3, rows=s, tr=512,
                   name="s5_glu_bwd", into=(dproj, 2, z_cb))


def add_into(a, b, name, dproj, cb):
    s, w = a.shape

    def fn(av, bv):
        return (av + bv,)

    return rowwise(fn, [(a, w, 0), (b, w, 0)], [(w, BF16)], rows=s, tr=512, name=name, into=(dproj, 0, cb))[0]


GATE_W, GATE_CB = 6144, 0
QKV_W, QKV_CB = 3072, 2
ZA_CB, XB_CB, ZB_CB, QC_CB, ZC_CB = 9, 10, 11, 12, 13
BA_CB, BA_W = 112, 128
BA_PAD, BA_PAD_CB = 1024, 14
PROJ_W = 14336 + BA_PAD


def _dot(a, b, dims="nn", prec=None):
    if prec is None:
        a, b = a.astype(BF16), b.astype(BF16)
    return lax.dot_general(a, b, _DIMS[dims], preferred_element_type=F32, precision=prec)


def _split(a):
    hi = a.astype(BF16)
    return hi, (a - hi.astype(F32)).astype(BF16)


def _dot3(a, b, dims="nn"):
    (ah, al), (bh, bl) = _split(a), _split(b)
    d = functools.partial(lax.dot_general, dimension_numbers=_DIMS[dims], preferred_element_type=F32)
    return d(ah, bh) + (d(ah, bl) + d(al, bh))


def _iota2(shape, dim):
    return lax.broadcasted_iota(jnp.int32, shape, dim)


def _conv_taps(xs, tr, k):
    if k == 0:
        return xs[8:8 + tr]
    return pltpu.roll(xs, k, 0)[8:8 + tr]


def _conv_silu_parts(xv, halo, wv, first):
    tr = xv.shape[0]
    xs = jnp.concatenate([jnp.where(first, 0.0, halo), xv], axis=0)
    taps = [_conv_taps(xs, tr, 3 - j) for j in range(4)]
    c = taps[0] * wv[0:1] + taps[1] * wv[1:2] + taps[2] * wv[2:3] + taps[3] * wv[3:4]
    return taps, c


def gdn_prep_fwd(proj, conv_w, alog_pad, dt_pad):
    s = proj.shape[0]
    tr = min(256, s)
    w = NHEAD * HEAD

    def body(x_ref, halo_ref, ba_ref, w_ref, al_ref, dt_ref, q_ref, k_ref, v_ref, bg_ref, gcol_ref, gt_ref):
        first = pl.program_id(0) == 0
        _, c = _conv_silu_parts(x_ref[...], halo_ref[...], w_ref[...], first)
        sv = _silu(c)
        for h in range(NHEAD):
            sl = slice(h * HEAD, (h + 1) * HEAD)
            qh, kh = sv[:, h * HEAD:(h + 1) * HEAD], sv[:, w + h * HEAD:w + (h + 1) * HEAD]
            q_ref[:, sl] = qh * lax.rsqrt(jnp.sum(qh * qh, axis=-1, keepdims=True) + EPS) * (HEAD ** -0.5)
            k_ref[:, sl] = kh * lax.rsqrt(jnp.sum(kh * kh, axis=-1, keepdims=True) + EPS)
        v_ref[...] = sv[:, 2 * w:]
        ba = ba_ref[...]
        lane = _iota2(ba.shape, 1)
        beta = _sigmoid(ba)
        g = -jnp.exp(al_ref[...]) * _softplus(ba + dt_ref[...])
        bg = jnp.where(lane < NHEAD, beta, jnp.where(lane < 2 * NHEAD, g, 0.0))
        bg_ref[...] = bg
        er, ec = _iota2((BA_W, BA_W), 0), _iota2((BA_W, BA_W), 1)
        expand = jnp.where((er == NHEAD + ec // 8) & (ec < 8 * NHEAD), 1.0, 0.0)
        grep = _dot(bg, expand, prec=HI)
        lr, lc = _iota2((tr, tr), 0), _iota2((tr, tr), 1)
        tril = jnp.where((lr // CHUNK == lc // CHUNK) & (lr >= lc), 1.0, 0.0)
        gc = _dot(tril, grep, prec=HI)
        gcol_ref[...] = gc
        gt_ref[...] = gc.T

    nb8 = tr // 8
    return pl.pallas_call(
        body, name="gdn_prep_fwd", grid=(s // tr,),
        in_specs=[pl.BlockSpec((tr, QKV_W), lambda i: (i, QKV_CB)),
                  pl.BlockSpec((8, QKV_W), lambda i: (jnp.maximum(i * nb8 - 1, 0), QKV_CB)),
                  pl.BlockSpec((tr, BA_W), lambda i: (i, BA_CB)),
                  pl.BlockSpec(conv_w.shape, lambda i: (0, 0)),
                  pl.BlockSpec((1, BA_W), lambda i: (0, 0)), pl.BlockSpec((1, BA_W), lambda i: (0, 0))],
        out_specs=[pl.BlockSpec((tr, w), lambda i: (i, 0))] * 3 + [pl.BlockSpec((tr, BA_W), lambda i: (i, 0))] * 2
        + [pl.BlockSpec((BA_W, tr), lambda i: (0, i))],
        out_shape=[jax.ShapeDtypeStruct((s, w), F32)] * 3 + [jax.ShapeDtypeStruct((s, BA_W), F32)] * 2
        + [jax.ShapeDtypeStruct((BA_W, s), F32)],
        compiler_params=_cparams(("parallel",)),
    )(proj, proj, proj, conv_w, alog_pad, dt_pad)


def _chunk_common(qh, kh, bgv, gcolv, gtv, h):
    beta = bgv[:, h:h + 1]
    gcc = gcolv[:, 8 * h:8 * h + 1]
    gcr = jnp.concatenate([gtv[8 * h:8 * h + 8, :]] * (CHUNK // 8), axis=0)
    ii, jj = _iota2((CHUNK, CHUNK), 0), _iota2((CHUNK, CHUNK), 1)
    incl, strict = ii >= jj, ii > jj
    decay = jnp.where(incl, jnp.exp(jnp.where(incl, gcc - gcr, 0.0)), 0.0)
    gl = gcr[:, CHUNK - 1:CHUNK]
    return beta, gcc, decay, strict, gl


def gdn_intra_fwd(q, k, v, bg, gcol, gt3):
    s, w = q.shape
    n = s // CHUNK

    def body(q_ref, k_ref, v_ref, bg_ref, gcol_ref, gt_ref, u_ref, w_ref, qd_ref, kd_ref, qk_ref, t_ref):
        bgv, gcolv, gtv = bg_ref[...], gcol_ref[...], gt_ref[0]
        ii, jj = _iota2((CHUNK, CHUNK), 0), _iota2((CHUNK, CHUNK), 1)
        eye = jnp.where(ii == jj, 1.0, 0.0)
        ps, ts, rhs = [], [], []
        for h in range(NHEAD):
            sl = slice(h * HEAD, (h + 1) * HEAD)
            qh, kh, vh = q_ref[:, sl], k_ref[:, sl], v_ref[:, sl]
            beta, gcc, decay, strict, gl = _chunk_common(qh, kh, bgv, gcolv, gtv, h)
            kb = kh * beta
            eg = jnp.exp(gcc)
            p = -jnp.where(strict, _dot(kb, kh, "nt") * decay, 0.0)
            ps.append(p)
            ts.append(eye + p)
            rhs.append((vh * beta, kb * eg))
            qd_ref[:, sl] = qh * eg
            kd_ref[:, sl] = kh * jnp.exp(gl - gcc)
            qk_ref[0, h] = _dot(qh, kh, "nt") * decay
        for _ in range(5):
            ps = [_dot3(p, p) for p in ps]
            ts = [t + _dot3(t, p) for t, p in zip(ts, ps)]
        for h in range(NHEAD):
            sl = slice(h * HEAD, (h + 1) * HEAD)
            u_ref[:, sl] = _dot3(ts[h], rhs[h][0])
            w_ref[:, sl] = _dot3(ts[h], rhs[h][1])
            t_ref[0, h] = ts[h]

    tok = pl.BlockSpec((CHUNK, w), lambda i: (i, 0))
    sm = pl.BlockSpec((CHUNK, BA_W), lambda i: (i, 0))
    sq = pl.BlockSpec((1, NHEAD, CHUNK, CHUNK), lambda i: (i, 0, 0, 0))
    return pl.pallas_call(
        body, name="gdn_intra_fwd", grid=(n,),
        in_specs=[tok, tok, tok, sm, sm, pl.BlockSpec((1, BA_W, CHUNK), lambda i: (i, 0, 0))],
        out_specs=[tok] * 4 + [sq, sq],
        out_shape=[jax.ShapeDtypeStruct((s, w), F32)] * 4 + [jax.ShapeDtypeStruct((n, NHEAD, CHUNK, CHUNK), F32)] * 2,
        compiler_params=_cparams(("parallel",)),
    )(q, k, v, bg, gcol, gt3)


def _state_decay(gtv, h):
    g8 = gtv[8 * h:8 * h + 8, CHUNK - 1:CHUNK]
    return jnp.exp(jnp.concatenate([g8] * (HEAD // 8), axis=0))


def gdn_seq_fwd(u, wd, qd, kd, qk, gt3):
    s, w = u.shape
    n = s // CHUNK

    def body(u_ref, w_ref, qd_ref, kd_ref, qk_ref, gt_ref, o_ref, st_ref, s_ref):
        @pl.when(pl.program_id(0) == 0)
        def _():
            s_ref[...] = jnp.zeros_like(s_ref)

        gtv = gt_ref[0]
        cols = [slice(h * HEAD, (h + 1) * HEAD) for h in range(NHEAD)]
        states = [s_ref[h] for h in range(NHEAD)]
        for h in range(NHEAD):
            st_ref[0, h] = states[h]
        vns = [u_ref[:, cols[h]] - _dot(w_ref[:, cols[h]], states[h]) for h in range(NHEAD)]
        from_state = [_dot(qd_ref[:, cols[h]], states[h]) for h in range(NHEAD)]
        for h in range(NHEAD):
            o_ref[:, cols[h]] = from_state[h] + _dot(qk_ref[0, h], vns[h])
        for h in range(NHEAD):
            s_ref[h] = states[h] * _state_decay(gtv, h) + _dot(kd_ref[:, cols[h]], vns[h], "tn")

    tok = pl.BlockSpec((CHUNK, w), lambda i: (i, 0))
    return pl.pallas_call(
        body, name="gdn_seq_fwd", grid=(n,),
        in_specs=[tok] * 4 + [pl.BlockSpec((1, NHEAD, CHUNK, CHUNK), lambda i: (i, 0, 0, 0)),
                              pl.BlockSpec((1, BA_W, CHUNK), lambda i: (i, 0, 0))],
        out_specs=[tok, pl.BlockSpec((1, NHEAD, HEAD, HEAD), lambda i: (i, 0, 0, 0))],
        out_shape=[jax.ShapeDtypeStruct((s, w), F32), jax.ShapeDtypeStruct((n, NHEAD, HEAD, HEAD), F32)],
        scratch_shapes=[pltpu.VMEM((NHEAD, HEAD, HEAD), F32)],
        compiler_params=_cparams(("arbitrary",)),
    )(u, wd, qd, kd, qk, gt3)


def gdn_seq_bwd(do, u, wd, qd, kd, qk, gt3, states):
    s, w = u.shape
    n = s // CHUNK

    def body(do_ref, u_ref, w_ref, qd_ref, kd_ref, qk_ref, gt_ref, st_ref,
             du_ref, dw_ref, dqd_ref, dkd_ref, dqk_ref, dgl_ref, ds_ref):
        @pl.when(pl.program_id(0) == 0)
        def _():
            ds_ref[...] = jnp.zeros_like(ds_ref)

        gtv = gt_ref[0]
        heads = range(NHEAD)
        cols = [slice(h * HEAD, (h + 1) * HEAD) for h in heads]
        sts = [st_ref[0, h] for h in heads]
        dsps = [ds_ref[h] for h in heads]
        vns = [u_ref[:, cols[h]] - _dot(w_ref[:, cols[h]], sts[h]) for h in heads]
        dvns = [_dot(qk_ref[0, h], do_ref[:, cols[h]], "tn") + _dot(kd_ref[:, cols[h]], dsps[h]) for h in heads]
        for h in heads:
            du_ref[:, cols[h]] = dvns[h]
            dqd_ref[:, cols[h]] = _dot(do_ref[:, cols[h]], sts[h], "nt")
        for h in heads:
            dw_ref[:, cols[h]] = -_dot(dvns[h], sts[h], "nt")
            dkd_ref[:, cols[h]] = _dot(vns[h], dsps[h], "nt")
            dqk_ref[0, h] = _dot(do_ref[:, cols[h]], vns[h], "nt")
        for h in heads:
            ds_ref[h] = (dsps[h] * _state_decay(gtv, h) + _dot(qd_ref[:, cols[h]], do_ref[:, cols[h]], "tn")
                         - _dot(w_ref[:, cols[h]], dvns[h], "tn"))
        dgl_ref[0] = jnp.concatenate([_colsum(sts[h] * dsps[h]) for h in heads], axis=0)

    tok = pl.BlockSpec((CHUNK, w), lambda i: (n - 1 - i, 0))
    sq = pl.BlockSpec((1, NHEAD, CHUNK, CHUNK), lambda i: (n - 1 - i, 0, 0, 0))
    return pl.pallas_call(
        body, name="gdn_seq_bwd", grid=(n,),
        in_specs=[tok] * 5 + [sq, pl.BlockSpec((1, BA_W, CHUNK), lambda i: (n - 1 - i, 0, 0)),
                              pl.BlockSpec((1, NHEAD, HEAD, HEAD), lambda i: (n - 1 - i, 0, 0, 0))],
        out_specs=[tok] * 4 + [sq, pl.BlockSpec((1, NHEAD, HEAD), lambda i: (n - 1 - i, 0, 0))],
        out_shape=[jax.ShapeDtypeStruct((s, w), F32)] * 4 + [jax.ShapeDtypeStruct((n, NHEAD, CHUNK, CHUNK), F32),
                                                            jax.ShapeDtypeStruct((n, NHEAD, HEAD), F32)],
        scratch_shapes=[pltpu.VMEM((NHEAD, HEAD, HEAD), F32)],
        compiler_params=_cparams(("arbitrary",)),
    )(do, u, wd, qd, kd, qk, gt3, states)


def gdn_intra_bwd(q, k, v, bg, gcol, gt3, tinv, du, dw, dqd, dkd, dqk, dgl):
    s, w = q.shape
    n = s // CHUNK

    def body(q_ref, k_ref, v_ref, bg_ref, gcol_ref, gt_ref, t_ref, du_ref, dw_ref, dqd_ref, dkd_ref, dqk_ref, dgl_ref,
             dq_ref, dk_ref, dv_ref, dbg_ref):
        bgv, gcolv, gtv, dglv = bg_ref[...], gcol_ref[...], gt_ref[0], dgl_ref[0]
        ii, jj = _iota2((CHUNK, CHUNK), 0), _iota2((CHUNK, CHUNK), 1)
        triu = jnp.where(ii <= jj, 1.0, 0.0)
        ones = jnp.ones((CHUNK, BA_W), F32)
        lane = _iota2((CHUNK, BA_W), 1)
        row = _iota2((CHUNK, 1), 0)
        dbg = jnp.zeros((CHUNK, BA_W), F32)
        first = []
        for h in range(NHEAD):
            sl = slice(h * HEAD, (h + 1) * HEAD)
            qh, kh, vh = q_ref[:, sl], k_ref[:, sl], v_ref[:, sl]
            beta, gcc, decay, strict, gl = _chunk_common(qh, kh, bgv, gcolv, gtv, h)
            kb = kh * beta
            eg = jnp.exp(gcc)
            rv, rk = vh * beta, kb * eg
            t, duh, dwh = t_ref[0, h], du_ref[:, sl], dw_ref[:, sl]
            first.append((_dot3(duh, rv, "nt") + _dot3(dwh, rk, "nt"), _dot3(t, duh, "tn"), _dot3(t, dwh, "tn"),
                          _dot(kb, kh, "nt"), _dot(qh, kh, "nt")))
        second = [_dot3(t_ref[0, h], first[h][0], "tn") for h in range(NHEAD)]
        third = [_dot3(second[h], t_ref[0, h], "nt") for h in range(NHEAD)]
        for h in range(NHEAD):
            sl = slice(h * HEAD, (h + 1) * HEAD)
            qh, kh, vh = q_ref[:, sl], k_ref[:, sl], v_ref[:, sl]
            beta, gcc, decay, strict, gl = _chunk_common(qh, kh, bgv, gcolv, gtv, h)
            dqdh, dkdh, dqkh = dqd_ref[:, sl], dkd_ref[:, sl], dqk_ref[0, h]
            kb = kh * beta
            eg = jnp.exp(gcc)
            ekd = jnp.exp(gl - gcc)
            rk = kb * eg
            _, drv, drk, m, p = first[h]
            da = jnp.where(strict, -third[h], 0.0)
            dm = da * decay
            dpm = dqkh * decay
            dkb = _dot(dm, kh) + drk * eg
            dq = _dot(dpm, kh) + dqdh * eg
            dk = _dot(dm, kb, "tn") + _dot(dpm, qh, "tn") + dkdh * ekd + dkb * beta
            e = (da * m + dqkh * p) * decay
            sk = jnp.sum(dkdh * kh * ekd, axis=-1, keepdims=True)
            dgc = (jnp.sum(e, axis=-1, keepdims=True) - _dot3(e, ones, "tn")[:, 0:1]
                   + jnp.sum(dqdh * qh * eg, axis=-1, keepdims=True) - sk + jnp.sum(drk * rk, axis=-1, keepdims=True))
            dglast = jnp.sum(sk, axis=0, keepdims=True) + jnp.sum(dglv[h:h + 1, :], axis=-1, keepdims=True) * jnp.exp(gl)
            dgc = dgc + jnp.where(row == CHUNK - 1, dglast, 0.0)
            dg = _dot3(triu, dgc * ones)
            dbeta = jnp.sum(dkb * kh, axis=-1, keepdims=True) + jnp.sum(drv * vh, axis=-1, keepdims=True)
            dbg = dbg + jnp.where(lane == h, dbeta, 0.0) + jnp.where(lane == NHEAD + h, dg, 0.0)
            dq_ref[:, sl] = dq
            dk_ref[:, sl] = dk
            dv_ref[:, sl] = drv * beta
        dbg_ref[...] = dbg

    tok = pl.BlockSpec((CHUNK, w), lambda i: (i, 0))
    sm = pl.BlockSpec((CHUNK, BA_W), lambda i: (i, 0))
    sq = pl.BlockSpec((1, NHEAD, CHUNK, CHUNK), lambda i: (i, 0, 0, 0))
    return pl.pallas_call(
        body, name="gdn_intra_bwd", grid=(n,),
        in_specs=[tok, tok, tok, sm, sm, pl.BlockSpec((1, BA_W, CHUNK), lambda i: (i, 0, 0)), sq,
                  tok, tok, tok, tok, sq, pl.BlockSpec((1, NHEAD, HEAD), lambda i: (i, 0, 0))],
        out_specs=[tok] * 3 + [sm],
        out_shape=[jax.ShapeDtypeStruct((s, w), F32)] * 3 + [jax.ShapeDtypeStruct((s, BA_W), F32)],
        compiler_params=_cparams(("parallel",)),
    )(q, k, v, bg, gcol, gt3, tinv, du, dw, dqd, dkd, dqk, dgl)


def gdn_prep_bwd1(proj, conv_w, alog_pad, dt_pad, dq, dk, dv, dbg, dproj):
    s = proj.shape[0]
    tr = min(256, s)
    w = NHEAD * HEAD
    pad_w = BA_PAD

    def body(x_ref, halo_ref, ba_ref, w_ref, al_ref, dt_ref, dq_ref, dk_ref, dv_ref, dbg_ref, buf_ref,
             dc_ref, dba_ref, dw0_ref, dw1_ref, dw2_ref, dw3_ref, dal_ref, ddt_ref):
        i = pl.program_id(0)
        taps, c = _conv_silu_parts(x_ref[...], halo_ref[...], w_ref[...], i == 0)
        sv, dsv = _silu(c), _dsilu(c)
        for h in range(NHEAD):
            for base, d_ref, scale in ((0, dq_ref, HEAD ** -0.5), (w, dk_ref, 1.0)):
                sl = slice(base + h * HEAD, base + (h + 1) * HEAD)
                sh = sv[:, sl]
                dn = d_ref[:, h * HEAD:(h + 1) * HEAD]
                r = lax.rsqrt(jnp.sum(sh * sh, axis=-1, keepdims=True) + EPS)
                dsh = scale * (r * dn - sh * (r * r * r) * jnp.sum(dn * sh, axis=-1, keepdims=True))
                dc_ref[:, sl] = dsh * dsv[:, sl]
        dc_ref[:, 2 * w:] = dv_ref[...] * dsv[:, 2 * w:]
        dc = dc_ref[...]
        ba, dbgv = ba_ref[...], dbg_ref[...]
        lane = _iota2(ba.shape, 1)
        beta = _sigmoid(ba)
        ea = jnp.exp(al_ref[...])
        z = ba + dt_ref[...]
        g = -ea * _softplus(z)
        is_g = (lane >= NHEAD) & (lane < 2 * NHEAD)
        da_raw = jnp.where(is_g, dbgv * (-ea) * _sigmoid(z), 0.0)
        dba = jnp.where(lane < NHEAD, dbgv * beta * (1.0 - beta), da_raw)
        dba_ref[...] = jnp.concatenate([dba, jnp.zeros((tr, pad_w - BA_W), F32)], axis=1).astype(BF16)
        partial = [_colsum(dc * tp) for tp in taps] + [_colsum(jnp.where(is_g, dbgv * g, 0.0)), _colsum(da_raw)]
        red_refs = (dw0_ref, dw1_ref, dw2_ref, dw3_ref, dal_ref, ddt_ref)

        @pl.when(i == 0)
        def _():
            for r_, v_ in zip(red_refs, partial):
                r_[...] = v_

        @pl.when(i > 0)
        def _():
            for r_, v_ in zip(red_refs, partial):
                r_[...] += v_

    nb8 = tr // 8
    tok = pl.BlockSpec((tr, w), lambda i: (i, 0))
    one = lambda width: pl.BlockSpec((1, width), lambda i: (0, 0))
    return pl.pallas_call(
        body, name="gdn_prep_bwd1", grid=(s // tr,),
        in_specs=[pl.BlockSpec((tr, QKV_W), lambda i: (i, QKV_CB)),
                  pl.BlockSpec((8, QKV_W), lambda i: (jnp.maximum(i * nb8 - 1, 0), QKV_CB)),
                  pl.BlockSpec((tr, BA_W), lambda i: (i, BA_CB)),
                  pl.BlockSpec(conv_w.shape, lambda i: (0, 0)), one(BA_W), one(BA_W),
                  tok, tok, tok, pl.BlockSpec((tr, BA_W), lambda i: (i, 0)), pl.BlockSpec(memory_space=pl.ANY)],
        out_specs=[pl.BlockSpec((tr, QKV_W), lambda i: (i, 0)), pl.BlockSpec((tr, pad_w), lambda i: (i, BA_PAD_CB))]
        + [one(QKV_W)] * 4 + [one(BA_W)] * 2,
        out_shape=[jax.ShapeDtypeStruct((s, QKV_W), F32), jax.ShapeDtypeStruct(dproj.shape, dproj.dtype)]
        + [jax.ShapeDtypeStruct((1, QKV_W), F32)] * 4 + [jax.ShapeDtypeStruct((1, BA_W), F32)] * 2,
        input_output_aliases={10: 1}, compiler_params=_cparams(("arbitrary",)),
    )(proj, proj, proj, conv_w, alog_pad, dt_pad, dq, dk, dv, dbg, dproj)


def gdn_prep_bwd2(dc, conv_w, dproj):
    s = dc.shape[0]
    tr = min(256, s)
    nblk = s // tr
    nb8 = tr // 8

    def body(dc_ref, halo_ref, w_ref, buf_ref, o_ref):
        last = pl.program_id(0) == nblk - 1
        wv = w_ref[...]
        xs = jnp.concatenate([dc_ref[...], jnp.where(last, 0.0, halo_ref[...])], axis=0)
        acc = xs[:tr] * wv[3:4]
        for j in range(3):
            acc = acc + pltpu.roll(xs, tr + 8 - (3 - j), 0)[:tr] * wv[j:j + 1]
        o_ref[...] = acc.astype(BF16)

    return pl.pallas_call(
        body, name="gdn_prep_bwd2", grid=(nblk,),
        in_specs=[pl.BlockSpec((tr, QKV_W), lambda i: (i, 0)),
                  pl.BlockSpec((8, QKV_W), lambda i: (jnp.minimum((i + 1) * nb8, s // 8 - 1), 0)),
                  pl.BlockSpec(conv_w.shape, lambda i: (0, 0)), pl.BlockSpec(memory_space=pl.ANY)],
        out_specs=pl.BlockSpec((tr, QKV_W), lambda i: (i, QKV_CB)),
        out_shape=jax.ShapeDtypeStruct(dproj.shape, dproj.dtype), input_output_aliases={3: 0},
        compiler_params=_cparams(("parallel",)),
    )(dc, dc, conv_w, dproj)


S5_W = S5_GROUPS * S5_STATE
S5_IN = S5_GROUPS * S5_GROUP
S5_TILES = 8
S5_TW, S5_TI = S5_W // S5_TILES, S5_IN // S5_TILES


def _s5_param_math(lr, li, ldt, br, bi):
    pr, pc = _iota2((S5_STATE, S5_STATE * S5_GROUP), 0), _iota2((S5_STATE, S5_STATE * S5_GROUP), 1)
    rep = jnp.where(pc // S5_GROUP == pr, 1.0, 0.0)
    dt = jnp.exp(ldt)
    mag = jnp.exp(lr * dt)
    ab_re, ab_im = mag * jnp.cos(li * dt), mag * jnp.sin(li * dt)
    den = lr * lr + li * li
    nr, ni = ab_re - 1.0, ab_im
    coef_re = (nr * lr + ni * li) / den
    coef_im = (ni * lr - nr * li) / den
    cr, ci = _dot(coef_re, rep, prec=HI), _dot(coef_im, rep, prec=HI)
    return ab_re, ab_im, cr * br - ci * bi, cr * bi + ci * br


def s5_param_fwd(lr, li, ldt, br, bi):
    def body(lr_ref, li_ref, ldt_ref, br_ref, bi_ref, ar_ref, ai_ref, bbr_ref, bbi_ref):
        res = _s5_param_math(lr_ref[...], li_ref[...], ldt_ref[...], br_ref[...], bi_ref[...])
        for r, v in zip((ar_ref, ai_ref, bbr_ref, bbi_ref), res):
            r[...] = v

    return pl.pallas_call(
        body, name="s5_param_fwd",
        out_shape=[jax.ShapeDtypeStruct(lr.shape, F32)] * 2 + [jax.ShapeDtypeStruct(br.shape, F32)] * 2,
        compiler_params=_cparams(),
    )(lr, li, ldt, br, bi)


def s5_param_bwd(lr, li, ldt, br, bi, dar, dai, dbbr, dbbi):
    def body(lr_ref, li_ref, ldt_ref, br_ref, bi_ref, dar_ref, dai_ref, dbbr_ref, dbbi_ref, *out_refs):
        _, vjp = jax.vjp(_s5_param_math, lr_ref[...], li_ref[...], ldt_ref[...], br_ref[...], bi_ref[...])
        for r, v in zip(out_refs, vjp((dar_ref[...], dai_ref[...], dbbr_ref[...], dbbi_ref[...]))):
            r[...] = v

    return pl.pallas_call(
        body, name="s5_param_bwd",
        out_shape=[jax.ShapeDtypeStruct(a.shape, F32) for a in (lr, li, ldt, br, bi)],
        compiler_params=_cparams(),
    )(lr, li, ldt, br, bi, dar, dai, dbbr, dbbi)


def _cmul(ar, ai, br, bi):
    return ar * br - ai * bi, ar * bi + ai * br


def _s5_power(ar, ai, steps):
    assert steps & (steps - 1) == 0
    for _ in range(steps.bit_length() - 1):
        ar, ai = _cmul(ar, ai, ar, ai)
    return ar, ai


def _s5_scan_rows(ar_ref, ai_ref, re_ref, im_ref, sr_ref, si_ref, tb, row0, reverse):
    quarter = S5_W // 4
    for qd in range(4):
        cs = slice(qd * quarter, (qd + 1) * quarter)
        are = jnp.broadcast_to(ar_ref[:, cs], (NSEG, quarter))
        aim = jnp.broadcast_to(ai_ref[:, cs], (NSEG, quarter))
        if reverse:
            aim = -aim

        def step(t, carry):
            h_r, h_i = carry
            tt = tb - 1 - t if reverse else t
            rows = pl.ds(pl.multiple_of(row0 + tt * NSEG, NSEG), NSEG)
            n_r = are * h_r - aim * h_i + re_ref[rows, cs]
            n_i = are * h_i + aim * h_r + im_ref[rows, cs]
            re_ref[rows, cs] = n_r
            im_ref[rows, cs] = n_i
            return n_r, n_i

        h_r, h_i = lax.fori_loop(0, tb, step, (sr_ref[:, cs], si_ref[:, cs]), unroll=8)
        sr_ref[:, cs] = h_r
        si_ref[:, cs] = h_i


def _s5_segment_carry(ar_ref, ai_ref, sr_ref, si_ref, steps, reverse):
    pr, pi = _s5_power(ar_ref[...], ai_ref[...], steps)
    if reverse:
        pi = -pi
    cur_r = jnp.zeros((1, S5_W), F32)
    cur_i = jnp.zeros((1, S5_W), F32)
    for s in (range(NSEG - 1, -1, -1) if reverse else range(NSEG)):
        e_r, e_i = sr_ref[s:s + 1, :], si_ref[s:s + 1, :]
        sr_ref[s:s + 1, :] = cur_r
        si_ref[s:s + 1, :] = cur_i
        nr, ni = _cmul(pr, pi, cur_r, cur_i)
        cur_r, cur_i = nr + e_r, ni + e_i


def _s5_blocks(s):
    steps = s // NSEG
    tb = min(32, steps)
    return steps, tb, NSEG * tb, steps // tb


def s5_scan_fwd(xp, a_re, a_im, bre, bim, cre, cim):
    s = xp.shape[0]
    steps, tb, rb, nb = _s5_blocks(s)

    def body(x_ref, ar_ref, ai_ref, bre_ref, bim_ref, cre_ref, cim_ref, y_ref, hsr_ref, hsi_ref,
             hr_ref, hi_ref, sr_ref, si_ref):
        ph, b = pl.program_id(0), pl.program_id(1)

        @pl.when((ph == 0) & (b == 0))
        def _():
            sr_ref[...] = jnp.zeros_like(sr_ref)
            si_ref[...] = jnp.zeros_like(si_ref)

        @pl.when((ph == 1) & (b == 0))
        def _():
            _s5_segment_carry(ar_ref, ai_ref, sr_ref, si_ref, steps, False)

        xv = x_ref[...].astype(BF16)
        for j in range(S5_TILES):
            xs = xv[:, j * S5_TI:(j + 1) * S5_TI]
            hr_ref[:, j * S5_TW:(j + 1) * S5_TW] = _dot(xs, bre_ref[j])
            hi_ref[:, j * S5_TW:(j + 1) * S5_TW] = _dot(xs, bim_ref[j])

        @pl.when(ph == 1)
        def _():
            hsr_ref[0] = sr_ref[...]
            hsi_ref[0] = si_ref[...]

        _s5_scan_rows(ar_ref, ai_ref, hr_ref, hi_ref, sr_ref, si_ref, tb, 0, False)

        @pl.when(ph == 1)
        def _():
            for j in range(S5_TILES):
                cs = slice(j * S5_TW, (j + 1) * S5_TW)
                y_ref[:, j * S5_TI:(j + 1) * S5_TI] = _dot(hr_ref[:, cs], cre_ref[j]) - _dot(hi_ref[:, cs], cim_ref[j])

    row = pl.BlockSpec((1, S5_W), lambda p, b: (0, 0))
    wb = pl.BlockSpec((S5_TILES, S5_TI, S5_TW), lambda p, b: (0, 0, 0))
    wc = pl.BlockSpec((S5_TILES, S5_TW, S5_TI), lambda p, b: (0, 0, 0))
    st = pl.BlockSpec((1, NSEG, S5_W), lambda p, b: (p * b, 0, 0))
    return pl.pallas_call(
        body, name="s5_scan_fwd", grid=(2, nb),
        in_specs=[pl.BlockSpec((rb, S5_IN), lambda p, b: (b, 0)), row, row, wb, wb, wc, wc],
        out_specs=[pl.BlockSpec((rb, S5_IN), lambda p, b: (p * b, 0)), st, st],
        out_shape=[jax.ShapeDtypeStruct((s, S5_IN), F32)] + [jax.ShapeDtypeStruct((nb, NSEG, S5_W), F32)] * 2,
        scratch_shapes=[pltpu.VMEM((rb, S5_W), F32)] * 2 + [pltpu.VMEM((NSEG, S5_W), F32)] * 2,
        compiler_params=_cparams(("arbitrary", "arbitrary")),
    )(xp, a_re, a_im, bre, bim, cre, cim)


def s5_scan_bwd(dyp, xp, a_re, a_im, bre, bim, cre_t, cim_t, hs_r, hs_i):
    s = xp.shape[0]
    steps, tb, rb, nb = _s5_blocks(s)

    def body(dy_ref, x_ref, ar_ref, ai_ref, bre_ref, bim_ref, crt_ref, cit_ref, hsr_ref, hsi_ref,
             dx_ref, dar_ref, dai_ref, dbr_ref, dbi_ref, dcr_ref, dci_ref,
             hr_ref, hi_ref, lr_ref, li_ref, sr_ref, si_ref, fr_ref, fi_ref, accr_ref, acci_ref):
        ph, b = pl.program_id(0), pl.program_id(1)

        @pl.when((ph == 0) & (b == 0))
        def _():
            sr_ref[...] = jnp.zeros_like(sr_ref)
            si_ref[...] = jnp.zeros_like(si_ref)

        @pl.when((ph == 1) & (b == 0))
        def _():
            _s5_segment_carry(ar_ref, ai_ref, sr_ref, si_ref, steps, True)
            for r in (accr_ref, acci_ref, dbr_ref, dbi_ref, dcr_ref, dci_ref):
                r[...] = jnp.zeros_like(r)

        dyv = dy_ref[...].astype(BF16)
        for j in range(S5_TILES):
            ds_ = dyv[:, j * S5_TI:(j + 1) * S5_TI]
            lr_ref[:, j * S5_TW:(j + 1) * S5_TW] = _dot(ds_, crt_ref[j])
            li_ref[:, j * S5_TW:(j + 1) * S5_TW] = -_dot(ds_, cit_ref[j])
        _s5_scan_rows(ar_ref, ai_ref, lr_ref, li_ref, sr_ref, si_ref, tb, 0, True)

        @pl.when(ph == 1)
        def _():
            xv = x_ref[...].astype(BF16)
            for j in range(S5_TILES):
                xs = xv[:, j * S5_TI:(j + 1) * S5_TI]
                hr_ref[NSEG:, j * S5_TW:(j + 1) * S5_TW] = _dot(xs, bre_ref[j])
                hi_ref[NSEG:, j * S5_TW:(j + 1) * S5_TW] = _dot(xs, bim_ref[j])
            hr_ref[0:NSEG, :] = hsr_ref[0]
            hi_ref[0:NSEG, :] = hsi_ref[0]
            fr_ref[...] = hsr_ref[0]
            fi_ref[...] = hsi_ref[0]
            _s5_scan_rows(ar_ref, ai_ref, hr_ref, hi_ref, fr_ref, fi_ref, tb, NSEG, False)
            lam_r, lam_i = lr_ref[...], li_ref[...]
            hp_r, hp_i = hr_ref[0:rb, :], hi_ref[0:rb, :]
            accr_ref[...] += jnp.sum((lam_r * hp_r + lam_i * hp_i).reshape(tb, NSEG, S5_W), axis=0)
            acci_ref[...] += jnp.sum((lam_i * hp_r - lam_r * hp_i).reshape(tb, NSEG, S5_W), axis=0)
            lam_rb, lam_ib = lam_r.astype(BF16), lam_i.astype(BF16)
            h_rb, h_ib = hr_ref[NSEG:, :].astype(BF16), hi_ref[NSEG:, :].astype(BF16)
            for j in range(S5_TILES):
                cs, ci = slice(j * S5_TW, (j + 1) * S5_TW), slice(j * S5_TI, (j + 1) * S5_TI)
                dbr_ref[j] += _dot(xv[:, ci], lam_rb[:, cs], "tn")
                dbi_ref[j] += _dot(xv[:, ci], lam_ib[:, cs], "tn")
                dx_ref[:, ci] = _dot(lam_rb[:, cs], bre_ref[j], "nt") + _dot(lam_ib[:, cs], bim_ref[j], "nt")
                dcr_ref[j] += _dot(h_rb[:, cs], dyv[:, ci], "tn")
                dci_ref[j] -= _dot(h_ib[:, cs], dyv[:, ci], "tn")

        @pl.when((ph == 1) & (b == nb - 1))
        def _():
            dar_ref[...] = jnp.sum(accr_ref[...], axis=0, keepdims=True)
            dai_ref[...] = jnp.sum(acci_ref[...], axis=0, keepdims=True)

    rev = lambda p, b: (nb - 1 - b, 0)
    row = pl.BlockSpec((1, S5_W), lambda p, b: (0, 0))
    wb = pl.BlockSpec((S5_TILES, S5_TI, S5_TW), lambda p, b: (0, 0, 0))
    wc = pl.BlockSpec((S5_TILES, S5_TW, S5_TI), lambda p, b: (0, 0, 0))
    st = pl.BlockSpec((1, NSEG, S5_W), lambda p, b: (nb - 1 - b, 0, 0))
    big = pltpu.VMEM((rb, S5_W), F32)
    big8 = pltpu.VMEM((rb + NSEG, S5_W), F32)
    small = pltpu.VMEM((NSEG, S5_W), F32)
    return pl.pallas_call(
        body, name="s5_scan_bwd", grid=(2, nb),
        in_specs=[pl.BlockSpec((rb, S5_IN), rev), pl.BlockSpec((rb, S5_IN), rev), row, row, wb, wb, wb, wb, st, st],
        out_specs=[pl.BlockSpec((rb, S5_IN), lambda p, b: (nb - 1 - p * b, 0)), row, row, wb, wb, wc, wc],
        out_shape=[jax.ShapeDtypeStruct((s, S5_IN), F32)] + [jax.ShapeDtypeStruct((1, S5_W), F32)] * 2
        + [jax.ShapeDtypeStruct((S5_TILES, S5_TI, S5_TW), F32)] * 2 + [jax.ShapeDtypeStruct((S5_TILES, S5_TW, S5_TI), F32)] * 2,
        scratch_shapes=[big8, big8, big, big, small, small, small, small, small, small],
        compiler_params=_cparams(("arbitrary", "arbitrary")),
    )(dyp, xp, a_re, a_im, bre, bim, cre_t, cim_t, hs_r, hs_i)


XA_DIM = 256
XA_W = XA_HEADS * XA_DIM


def _xa_probs(qh, kh):
    sc = _dot(qh, kh, "nt") * (XA_DIM ** -0.5)
    ex = jnp.exp(sc - jnp.max(sc, axis=-1, keepdims=True))
    return ex / jnp.sum(ex, axis=-1, keepdims=True)


def xa_fwd(proj, kv):
    s = proj.shape[0]
    tq = min(512, s)

    def body(q_ref, kv_ref, o_ref):
        for h in range(XA_HEADS):
            sl = slice(h * XA_DIM, (h + 1) * XA_DIM)
            p = _xa_probs(q_ref[:, sl], kv_ref[:, sl])
            o_ref[:, sl] = _dot(p, kv_ref[:, XA_W + h * XA_DIM:XA_W + (h + 1) * XA_DIM])

    return pl.pallas_call(
        body, name="xa_fwd", grid=(s // tq,),
        in_specs=[pl.BlockSpec((tq, XA_W), lambda i: (i, QC_CB)), pl.BlockSpec(kv.shape, lambda i: (0, 0))],
        out_specs=pl.BlockSpec((tq, XA_W), lambda i: (i, 0)),
        out_shape=jax.ShapeDtypeStruct((s, XA_W), F32),
        compiler_params=_cparams(("parallel",)),
    )(proj, kv)


def xa_bwd(do, proj, kv, dproj):
    s = proj.shape[0]
    tq = min(512, s)

    def body(do_ref, q_ref, kv_ref, buf_ref, dq_ref, dkv_ref):
        @pl.when(pl.program_id(0) == 0)
        def _():
            dkv_ref[...] = jnp.zeros_like(dkv_ref)

        for h in range(XA_HEADS):
            sl = slice(h * XA_DIM, (h + 1) * XA_DIM)
            sv = slice(XA_W + h * XA_DIM, XA_W + (h + 1) * XA_DIM)
            qh, kh, vh, doh = q_ref[:, sl], kv_ref[:, sl], kv_ref[:, sv], do_ref[:, sl]
            p = _xa_probs(qh, kh)
            dp = _dot(doh, vh, "nt")
            ds_ = p * (dp - jnp.sum(dp * p, axis=-1, keepdims=True)) * (XA_DIM ** -0.5)
            dq_ref[:, sl] = _dot(ds_, kh).astype(BF16)
            dkv_ref[:, sl] += _dot(ds_, qh, "tn")
            dkv_ref[:, sv] += _dot(p, doh, "tn")

    return pl.pallas_call(
        body, name="xa_bwd", grid=(s // tq,),
        in_specs=[pl.BlockSpec((tq, XA_W), lambda i: (i, 0)), pl.BlockSpec((tq, XA_W), lambda i: (i, QC_CB)),
                  pl.BlockSpec(kv.shape, lambda i: (0, 0)), pl.BlockSpec(memory_space=pl.ANY)],
        out_specs=[pl.BlockSpec((tq, XA_W), lambda i: (i, QC_CB)), pl.BlockSpec(kv.shape, lambda i: (0, 0))],
        out_shape=[jax.ShapeDtypeStruct(dproj.shape, dproj.dtype), jax.ShapeDtypeStruct(kv.shape, F32)],
        input_output_aliases={3: 0}, compiler_params=_cparams(("arbitrary",)),
    )(do, proj, kv, dproj)


def adamw(w, g, m, v, name):
    lead = (0,) * (w.ndim - 2)
    rows, cols = w.shape[-2:]
    tr = rows
    while tr * cols * 4 * 7 * 2 > 36 * 2 ** 20 and tr % 16 == 0:
        tr //= 2

    def body(w_ref, g_ref, m_ref, v_ref, d_ref, m2_ref, v2_ref):
        gv = g_ref[...]
        m2 = ADAM_B1 * m_ref[...] + (1.0 - ADAM_B1) * gv
        v2 = ADAM_B2 * v_ref[...] + (1.0 - ADAM_B2) * (gv * gv)
        m_hat = m2 / (1.0 - ADAM_B1 ** ADAM_STEP)
        v_hat = v2 / (1.0 - ADAM_B2 ** ADAM_STEP)
        d_ref[...] = -ADAM_LR * (m_hat / (jnp.sqrt(v_hat) + ADAM_EPS) + ADAM_WD * w_ref[...])
        m2_ref[...] = m2
        v2_ref[...] = v2

    spec = pl.BlockSpec((1,) * len(lead) + (tr, cols), lambda i: lead + (i, 0))
    return pl.pallas_call(
        body, name=name, grid=(rows // tr,), in_specs=[spec] * 4, out_specs=[spec] * 3,
        out_shape=[jax.ShapeDtypeStruct(w.shape, F32)] * 3, compiler_params=_cparams(("parallel",)),
    )(w, g, m, v)


def _seg_perm(a):
    s, w = a.shape
    return a.reshape(NSEG, s // NSEG, w).transpose(1, 0, 2).reshape(s, w)


def _seg_unperm(a):
    s, w = a.shape
    return a.reshape(s // NSEG, NSEG, w).transpose(1, 0, 2).reshape(s, w)


def _block_diag(t):
    nt, _, r, c = t.shape
    eye = jnp.eye(8, dtype=bool)
    return jnp.where(eye[None, :, None, :, None], t[:, :, :, None, :], 0.0).reshape(nt, 8 * r, 8 * c)


def _block_diag_inv(d, r, c):
    d5 = d.reshape(d.shape[0], 8, r, 8, c)
    return jnp.diagonal(d5, axis1=1, axis2=3).transpose(0, 3, 1, 2)


def _s5_b_tiles(bb):
    return _block_diag(bb.reshape(S5_TILES, 8, S5_STATE, S5_GROUP).transpose(0, 1, 3, 2))


def _s5_b_untile(d):
    return _block_diag_inv(d, S5_GROUP, S5_STATE).transpose(0, 1, 3, 2).reshape(S5_GROUPS, S5_STATE * S5_GROUP)


def _s5_c_tiles(c):
    return _block_diag(c.reshape(S5_TILES, 8, S5_GROUP, S5_STATE).transpose(0, 1, 3, 2))


def _s5_c_untile(d):
    return _block_diag_inv(d, S5_STATE, S5_GROUP).transpose(0, 1, 3, 2).reshape(S5_GROUPS, S5_GROUP, S5_STATE)


def s5_ssm_fwd(xb, lam_re, lam_im, log_dt, b_re, b_im, c_re, c_im):
    br, bi = b_re.reshape(S5_GROUPS, -1), b_im.reshape(S5_GROUPS, -1)
    ldt = log_dt.reshape(S5_GROUPS, 1)
    ab_re, ab_im, bb_re, bb_im = s5_param_fwd(lam_re, lam_im, ldt, br, bi)
    a_re, a_im = ab_re.reshape(1, S5_W), ab_im.reshape(1, S5_W)
    bre, bim = _s5_b_tiles(bb_re).astype(BF16), _s5_b_tiles(bb_im).astype(BF16)
    cre, cim = _s5_c_tiles(c_re).astype(BF16), _s5_c_tiles(c_im).astype(BF16)
    xp = _seg_perm(xb)
    yp, hs_r, hs_i = s5_scan_fwd(xp, a_re, a_im, bre, bim, cre, cim)
    saved = (xp, a_re, a_im, bre, bim, cre, cim, hs_r, hs_i, (lam_re, lam_im, ldt, br, bi))
    return _seg_unperm(yp), saved


def s5_ssm_bwd(dy, saved):
    xp, a_re, a_im, bre, bim, cre, cim, hs_r, hs_i, params = saved
    cre_t, cim_t = cre.transpose(0, 2, 1), cim.transpose(0, 2, 1)
    dxp, dar, dai, dbr, dbi, dcr, dci = s5_scan_bwd(_seg_perm(dy), xp, a_re, a_im, bre, bim, cre_t, cim_t, hs_r, hs_i)
    dlr, dli, dldt, db_re, db_im = s5_param_bwd(*params, dar.reshape(S5_GROUPS, S5_STATE), dai.reshape(S5_GROUPS, S5_STATE),
                                                _s5_b_untile(dbr), _s5_b_untile(dbi))
    shape_b = (S5_GROUPS, S5_STATE, S5_GROUP)
    return (_seg_unperm(dxp), dlr, dli, dldt.reshape(S5_GROUPS), db_re.reshape(shape_b), db_im.reshape(shape_b),
            _s5_c_untile(dcr), _s5_c_untile(dci))


_MESH = pl.DeviceIdType.MESH
_HBM = pl.BlockSpec(memory_space=pltpu.HBM)
N_DEV = 8


def _position():
    return lax.axis_index("x"), lax.axis_index("y"), lax.axis_index("c")


D2D_CHUNK_BYTES = 2 ** 20


def _chunk_rows(rows, cols, itemsize):
    return _row_tile(rows, 16, max(16, D2D_CHUNK_BYTES // (cols * itemsize)))


def _rows(start, size, unit=16):
    return pl.ds(pl.multiple_of(start, unit), size)


def _push_to_sibling(chunks, stages, recv_sems, store_sems, sibling, lag=2):
    in_slot, used, stores = {}, {}, []

    def push(q, slot):
        _, _, sid, land, _ = chunks[q]
        buf, send_sems, _ = stages[sid]
        return pltpu.make_async_remote_copy(src_ref=buf.at[slot], dst_ref=land, send_sem=send_sems.at[slot],
                                            recv_sem=recv_sems.at[q], device_id=sibling, device_id_type=_MESH)

    def receive(q):
        push(q, 0).wait_recv()
        st = pltpu.make_async_copy(chunks[q][3], chunks[q][4], store_sems.at[q])
        st.start()
        stores.append(st)

    for q, (pre, src, sid, _, _) in enumerate(chunks):
        if pre is not None:
            pre()
        slot = used.get(sid, 0) % 2
        used[sid] = used.get(sid, 0) + 1
        if (sid, slot) in in_slot:
            in_slot.pop((sid, slot)).wait_send()
        load = pltpu.make_async_copy(src, stages[sid][0].at[slot], stages[sid][2].at[slot])
        load.start()
        load.wait()
        cp = push(q, slot)
        cp.start()
        in_slot[(sid, slot)] = cp
        if q >= lag:
            receive(q - lag)
    for q in range(max(0, len(chunks) - lag), len(chunks)):
        receive(q)
    for cp in in_slot.values():
        cp.wait_send()
    for st in stores:
        st.wait()


def _stage_scratch(shapes_dtypes):
    out = []
    for shape, dtype in shapes_dtypes:
        out += [pltpu.VMEM((2,) + shape, dtype), pltpu.SemaphoreType.DMA((2,)), pltpu.SemaphoreType.DMA((2,))]
    return out


def allgather_weights(ws, convw, name):
    n = len(ws)
    extra = 0 if convw is None else 1
    halves = [w.shape[0] // 2 for w in ws]
    steps = [_chunk_rows(h, w.shape[1], w.dtype.itemsize) for h, w in zip(halves, ws)]
    per_peer = [h // s for h, s in zip(halves, steps)]
    nchunks = 3 * sum(per_peer)

    def body(*refs):
        w_refs = refs[:n]
        wo_refs = refs[n + extra:2 * n + extra]
        scratch = refs[2 * (n + extra):]
        send_sems, recv_sems, local_sems, fwd_recv_sems, store_sems = scratch[:5]
        lands = scratch[5:5 + n]
        stage_refs = scratch[5 + n:]
        stages = [tuple(stage_refs[3 * i:3 * i + 3]) for i in range(n)]
        x, y, c = _position()
        mine = 2 * x + y
        peers = [(1 - x, y), (x, 1 - y), (1 - x, 1 - y)]
        blocks = [2 * px + py for px, py in peers]
        local = [pltpu.make_async_copy(w_refs[i], wo_refs[i].at[mine], local_sems.at[i]) for i in range(n)]
        if extra:
            c_ref, co_ref = refs[n], refs[2 * n + 1]
            local.append(pltpu.make_async_copy(c_ref, co_ref.at[mine], local_sems.at[n]))
        for cp in local:
            cp.start()

        def ici(i, k, block):
            rows = _rows(c * halves[i], halves[i])
            return pltpu.make_async_remote_copy(src_ref=w_refs[i].at[rows, :], dst_ref=wo_refs[i].at[block, rows, :],
                                                send_sem=send_sems.at[3 * i + k], recv_sem=recv_sems.at[3 * i + k],
                                                device_id=(*peers[k], c), device_id_type=_MESH)

        def conv(k, block):
            return pltpu.make_async_remote_copy(src_ref=c_ref, dst_ref=co_ref.at[block], send_sem=send_sems.at[3 * n + k],
                                                recv_sem=recv_sems.at[3 * n + k], device_id=(*peers[k], c), device_id_type=_MESH)

        sends = [ici(i, k, mine) for k in range(3) for i in range(n)] + ([conv(k, mine) for k in range(3)] if extra else [])
        for cp in sends:
            cp.start()
        chunks = []
        for k in range(3):
            for i in range(n):
                for q in range(per_peer[i]):
                    pre = functools.partial(lambda i, k: ici(i, k, blocks[k]).wait_recv(), i, k) if q == 0 else None
                    src = wo_refs[i].at[blocks[k], _rows(c * halves[i] + q * steps[i], steps[i]), :]
                    out = wo_refs[i].at[blocks[k], _rows((1 - c) * halves[i] + q * steps[i], steps[i]), :]
                    chunks.append((pre, src, i, lands[i].at[k * per_peer[i] + q], out))
        _push_to_sibling(chunks, stages, fwd_recv_sems, store_sems, (x, y, 1 - c))
        if extra:
            for k in range(3):
                conv(k, blocks[k]).wait_recv()
        for cp in sends:
            cp.wait_send()
        for cp in local:
            cp.wait()

    nsem = 3 * (n + extra)
    scratch = [pltpu.SemaphoreType.DMA((nsem,)), pltpu.SemaphoreType.DMA((nsem,)), pltpu.SemaphoreType.DMA((n + extra,)),
               pltpu.SemaphoreType.DMA((nchunks,)), pltpu.SemaphoreType.DMA((nchunks,))]
    scratch += [pltpu.VMEM((3 * p, s, w.shape[1]), w.dtype) for p, s, w in zip(per_peer, steps, ws)]
    scratch += _stage_scratch([((s, w.shape[1]), w.dtype) for s, w in zip(steps, ws)])
    operands = list(ws) + ([convw] if extra else [])
    return pl.pallas_call(
        body, name=name, in_specs=[_HBM] * len(operands), out_specs=[_HBM] * len(operands),
        out_shape=[jax.ShapeDtypeStruct((4,) + w.shape, w.dtype) for w in operands],
        scratch_shapes=scratch, compiler_params=pltpu.CompilerParams(vmem_limit_bytes=VMEM_LIMIT),
    )(*operands)


def exchange_cores(gs, name):
    n = len(gs)
    halves = [g.shape[1] // 2 for g in gs]
    steps = [_chunk_rows(h, g.shape[2], g.dtype.itemsize) for h, g in zip(halves, gs)]
    per_shard = [h // s for h, s in zip(halves, steps)]
    nchunks = 4 * sum(per_shard)

    def body(*refs):
        g_refs, got_refs, scratch = refs[:n], refs[n:2 * n], refs[2 * n:]
        recv_sems, store_sems = scratch[:2]
        lands = scratch[2:2 + n]
        stage_refs = scratch[2 + n:2 + 4 * n]
        stages = [tuple(stage_refs[3 * i:3 * i + 3]) for i in range(n)]
        x, y, c = _position()
        chunks = []
        for i in range(n):
            for j in range(4):
                for q in range(per_shard[i]):
                    src = g_refs[i].at[j, _rows((1 - c) * halves[i] + q * steps[i], steps[i]), :]
                    out = got_refs[i].at[j, pl.ds(q * steps[i], steps[i]), :]
                    chunks.append((None, src, i, lands[i].at[j * per_shard[i] + q], out))
        _push_to_sibling(chunks, stages, recv_sems, store_sems, (x, y, 1 - c))

    scratch = [pltpu.SemaphoreType.DMA((nchunks,)), pltpu.SemaphoreType.DMA((nchunks,))]
    scratch += [pltpu.VMEM((4 * p, s, g.shape[2]), g.dtype) for p, s, g in zip(per_shard, steps, gs)]
    scratch += _stage_scratch([((s, g.shape[2]), g.dtype) for s, g in zip(steps, gs)])
    return pl.pallas_call(
        body, name=name, in_specs=[_HBM] * n, out_specs=[_HBM] * n,
        out_shape=[jax.ShapeDtypeStruct((4, h, g.shape[2]), g.dtype) for h, g in zip(halves, gs)],
        scratch_shapes=scratch, compiler_params=pltpu.CompilerParams(vmem_limit_bytes=VMEM_LIMIT),
    )(*gs)


def chips_side(cs):
    n = len(cs)

    def copies(c_refs, o_refs, sems):
        send_sems, recv_sems, local_sems = sems
        x, y, c = _position()
        mine = 2 * x + y
        peers = [(1 - x, y), (x, 1 - y), (1 - x, 1 - y)]
        blocks = [2 * px + py for px, py in peers]
        local = [pltpu.make_async_copy(c_refs[i].at[mine], o_refs[i].at[mine], local_sems.at[i]) for i in range(n)]

        def copy(i, k, sending):
            return pltpu.make_async_remote_copy(src_ref=c_refs[i].at[blocks[k]], dst_ref=o_refs[i].at[mine if sending else blocks[k]],
                                                send_sem=send_sems.at[3 * i + k], recv_sem=recv_sems.at[3 * i + k],
                                                device_id=(*peers[k], c), device_id_type=_MESH)

        sends = [copy(i, k, True) for k in range(3) for i in range(n)]
        return local, sends, lambda: [copy(i, k, False) for k in range(3) for i in range(n)]

    def start(*refs):
        local, sends, _ = copies(*refs)
        for cp in local + sends:
            cp.start()

    def finish(*refs):
        local, sends, arrivals = copies(*refs)
        for cp in arrivals():
            cp.wait_recv()
        for cp in sends:
            cp.wait_send()
        for cp in local:
            cp.wait()

    scratch = [pltpu.SemaphoreType.DMA((3 * n,)), pltpu.SemaphoreType.DMA((3 * n,)), pltpu.SemaphoreType.DMA((n,))]
    return Side(list(cs), [jax.ShapeDtypeStruct(a.shape, a.dtype) for a in cs], scratch, start, finish)


def gather_side(ws, convw):
    n = len(ws)
    halves = [w.shape[0] // 2 for w in ws]

    def copies(in_refs, out_refs, sems):
        w_refs, c_ref, wo_refs, co_ref = in_refs[:n], in_refs[n], out_refs[:n], out_refs[n]
        send_sems, recv_sems, local_sems = sems
        x, y, c = _position()
        mine = 2 * x + y
        peers = [(1 - x, y), (x, 1 - y), (1 - x, 1 - y)]
        blocks = [2 * px + py for px, py in peers]
        local = [pltpu.make_async_copy(w_refs[i], wo_refs[i].at[mine], local_sems.at[i]) for i in range(n)]
        local.append(pltpu.make_async_copy(c_ref, co_ref.at[mine], local_sems.at[n]))

        def ici(i, k, block):
            rows = _rows(c * halves[i], halves[i])
            return pltpu.make_async_remote_copy(src_ref=w_refs[i].at[rows, :], dst_ref=wo_refs[i].at[block, rows, :],
                                                send_sem=send_sems.at[3 * i + k], recv_sem=recv_sems.at[3 * i + k],
                                                device_id=(*peers[k], c), device_id_type=_MESH)

        def conv(k, block):
            return pltpu.make_async_remote_copy(src_ref=c_ref, dst_ref=co_ref.at[block], send_sem=send_sems.at[3 * n + k],
                                                recv_sem=recv_sems.at[3 * n + k], device_id=(*peers[k], c), device_id_type=_MESH)

        sends = [ici(i, k, mine) for k in range(3) for i in range(n)] + [conv(k, mine) for k in range(3)]
        return local, sends, lambda: ([ici(i, k, blocks[k]) for k in range(3) for i in range(n)]
                                      + [conv(k, blocks[k]) for k in range(3)])

    def start(*refs):
        local, sends, _ = copies(*refs)
        for cp in local + sends:
            cp.start()

    def finish(*refs):
        local, sends, arrivals = copies(*refs)
        for cp in arrivals():
            cp.wait_recv()
        for cp in sends:
            cp.wait_send()
        for cp in local:
            cp.wait()

    nsem = 3 * n + 3
    scratch = [pltpu.SemaphoreType.DMA((nsem,)), pltpu.SemaphoreType.DMA((nsem,)), pltpu.SemaphoreType.DMA((n + 1,))]
    operands = list(ws) + [convw]
    return Side(operands, [jax.ShapeDtypeStruct((4,) + w.shape, w.dtype) for w in operands], scratch, start, finish)


def forward_halves(stacked):
    n = len(stacked)
    halves = [w.shape[1] // 2 for w in stacked]
    steps = [_chunk_rows(h, w.shape[2], w.dtype.itemsize) for h, w in zip(halves, stacked)]
    per_peer = [h // s for h, s in zip(halves, steps)]
    nchunks = 3 * sum(per_peer)

    def body(*refs):
        w_refs, o_refs = refs[:n], refs[n:2 * n]
        scratch = refs[2 * n:]
        recv_sems, store_sems = scratch[:2]
        lands = scratch[2:2 + n]
        stages = [tuple(scratch[2 + n + 3 * i:2 + n + 3 * i + 3]) for i in range(n)]
        x, y, c = _position()
        blocks = [2 * px + py for px, py in ((1 - x, y), (x, 1 - y), (1 - x, 1 - y))]
        chunks = []
        for k in range(3):
            for i in range(n):
                for q in range(per_peer[i]):
                    src = w_refs[i].at[blocks[k], _rows(c * halves[i] + q * steps[i], steps[i]), :]
                    out = o_refs[i].at[blocks[k], _rows((1 - c) * halves[i] + q * steps[i], steps[i]), :]
                    chunks.append((None, src, i, lands[i].at[k * per_peer[i] + q], out))
        _push_to_sibling(chunks, stages, recv_sems, store_sems, (x, y, 1 - c))

    scratch = [pltpu.SemaphoreType.DMA((nchunks,)), pltpu.SemaphoreType.DMA((nchunks,))]
    scratch += [pltpu.VMEM((3 * p, s, w.shape[2]), w.dtype) for p, s, w in zip(per_peer, steps, stacked)]
    scratch += _stage_scratch([((s, w.shape[2]), w.dtype) for s, w in zip(steps, stacked)])
    return pl.pallas_call(
        body, name="forward_halves", in_specs=[_HBM] * n, out_specs=[_HBM] * n,
        out_shape=[jax.ShapeDtypeStruct(w.shape, w.dtype) for w in stacked], input_output_aliases={i: i for i in range(n)},
        scratch_shapes=scratch, compiler_params=pltpu.CompilerParams(vmem_limit_bytes=VMEM_LIMIT),
    )(*stacked)


def exchange_small(small):
    def body(s_ref, so_ref, send_sems, recv_sems, local_sem):
        x, y, c = _position()
        me = 4 * x + 2 * y + c
        local = pltpu.make_async_copy(s_ref, so_ref.at[me], local_sem)
        local.start()

        def copy(r, sending):
            px, py, pc = (1 - x if r & 4 else x, 1 - y if r & 2 else y, 1 - c if r & 1 else c)
            slot = me if sending else 4 * px + 2 * py + pc
            return pltpu.make_async_remote_copy(src_ref=s_ref, dst_ref=so_ref.at[slot], send_sem=send_sems.at[r - 1],
                                                recv_sem=recv_sems.at[r - 1], device_id=(px, py, pc), device_id_type=_MESH)

        sends = [copy(r, True) for r in range(1, N_DEV)]
        for cp in sends:
            cp.start()
        for r in range(1, N_DEV):
            copy(r, False).wait_recv()
        for cp in sends:
            cp.wait_send()
        local.wait()

    return pl.pallas_call(
        body, name="exchange_small", in_specs=[_HBM], out_specs=_HBM,
        out_shape=jax.ShapeDtypeStruct((N_DEV,) + small.shape, small.dtype),
        scratch_shapes=[pltpu.SemaphoreType.DMA((N_DEV - 1,)), pltpu.SemaphoreType.DMA((N_DEV - 1,)), pltpu.SemaphoreType.DMA],
    )(small)


def pair_sum(core, g, got, name):
    nb, rows, cols = got.shape
    tr = _row_tile(rows, 16, max(16, (2 * 2 ** 20) // (cols * g.dtype.itemsize)))
    nblk = rows // tr

    def body(c_ref, a_ref, b_ref, o_ref):
        o_ref[...] = (a_ref[...].astype(F32) + b_ref[...].astype(F32)).astype(o_ref.dtype)

    spec = pl.BlockSpec((1, tr, cols), lambda j, i, c_ref: (j, i, 0))
    mine = pl.BlockSpec((1, tr, cols), lambda j, i, c_ref: (j, c_ref[0] * nblk + i, 0))
    return pl.pallas_call(
        body, name=name,
        grid_spec=pltpu.PrefetchScalarGridSpec(num_scalar_prefetch=1, grid=(nb, nblk), in_specs=[mine, spec], out_specs=spec),
        out_shape=jax.ShapeDtypeStruct(got.shape, g.dtype), compiler_params=_cparams(("parallel", "parallel")),
    )(core, g, got)


def sum_chips(core, pieces, name):
    nb, rows, cols = pieces.shape
    tr = _row_tile(rows, 16, max(16, (6 * 2 ** 20) // (nb * cols * pieces.dtype.itemsize)))
    nblk = rows // tr

    def body(c_ref, p_ref, o_ref):
        acc = p_ref[0].astype(F32)
        for i in range(1, nb):
            acc = acc + p_ref[i].astype(F32)
        o_ref[0] = acc

    return pl.pallas_call(
        body, name=name,
        grid_spec=pltpu.PrefetchScalarGridSpec(
            num_scalar_prefetch=1, grid=(nblk,),
            in_specs=[pl.BlockSpec((nb, tr, cols), lambda i, c_ref: (0, i, 0))],
            out_specs=pl.BlockSpec((1, tr, cols), lambda i, c_ref: (0, c_ref[0] * nblk + i, 0))),
        out_shape=jax.ShapeDtypeStruct((1, 2 * rows, cols), F32), compiler_params=_cparams(("parallel",)),
    )(core, pieces)


def sibling_exchange(fulls):
    n = len(fulls)
    halves = [f.shape[1] // 2 for f in fulls]
    steps = [_chunk_rows(h, f.shape[2], f.dtype.itemsize) for h, f in zip(halves, fulls)]
    counts = [h // s for h, s in zip(halves, steps)]
    nchunks = sum(counts)

    def body(*refs):
        f_refs, o_refs = refs[:n], refs[n:2 * n]
        scratch = refs[2 * n:]
        recv_sems, store_sems = scratch[:2]
        lands = scratch[2:2 + n]
        stages = [tuple(scratch[2 + n + 3 * i:2 + n + 3 * i + 3]) for i in range(n)]
        x, y, c = _position()
        chunks = []
        for i in range(n):
            for q in range(counts[i]):
                src = f_refs[i].at[0, _rows(c * halves[i] + q * steps[i], steps[i]), :]
                out = o_refs[i].at[0, _rows((1 - c) * halves[i] + q * steps[i], steps[i]), :]
                chunks.append((None, src, i, lands[i].at[q], out))
        _push_to_sibling(chunks, stages, recv_sems, store_sems, (x, y, 1 - c))

    scratch = [pltpu.SemaphoreType.DMA((nchunks,)), pltpu.SemaphoreType.DMA((nchunks,))]
    scratch += [pltpu.VMEM((k, s, f.shape[2]), f.dtype) for k, s, f in zip(counts, steps, fulls)]
    scratch += _stage_scratch([((s, f.shape[2]), f.dtype) for s, f in zip(steps, fulls)])
    return pl.pallas_call(
        body, name="sibling_exchange", in_specs=[_HBM] * n, out_specs=[_HBM] * n,
        out_shape=[jax.ShapeDtypeStruct(f.shape, f.dtype) for f in fulls],
        input_output_aliases={i: i for i in range(n)},
        scratch_shapes=scratch, compiler_params=pltpu.CompilerParams(vmem_limit_bytes=VMEM_LIMIT),
    )(*fulls)


def _row_tile(rows, unit, max_rows):
    best = unit
    for t in range(unit, min(rows, max_rows) + 1, unit):
        if rows % t == 0:
            best = t
    return best


def sum_pieces(pieces, name):
    n, rows, cols = pieces.shape
    tr = _row_tile(rows, 16, max(16, (6 * 2 ** 20) // (n * cols * pieces.dtype.itemsize)))

    def body(p_ref, o_ref):
        acc = p_ref[0].astype(F32)
        for i in range(1, n):
            acc = acc + p_ref[i].astype(F32)
        o_ref[...] = acc

    return pl.pallas_call(
        body, name=name, grid=(rows // tr,),
        in_specs=[pl.BlockSpec((n, tr, cols), lambda i: (0, i, 0))], out_specs=pl.BlockSpec((tr, cols), lambda i: (i, 0)),
        out_shape=jax.ShapeDtypeStruct((rows, cols), F32), compiler_params=_cparams(("parallel",)),
    )(pieces)


BIG = ("w_in", "s5_w_glu", "w_kv_mem", "w_br_a", "w_br_b", "w_br_c", "w_out")
COL_SHARDED = ("w_in", "s5_w_glu", "w_br_a", "w_br_b", "w_br_c")
SMALL = ("norm_g", "gdn_a_log", "gdn_dt_bias", "gdn_norm_g", "s5_lambda_re", "s5_lambda_im", "s5_log_dt",
         "s5_b_re", "s5_b_im", "s5_c_re", "s5_c_im", "s5_d", "mem_norm_g", "final_g")
WEIGHTS = ("norm_g", "w_in", "conv_w", "gdn_a_log", "gdn_dt_bias", "gdn_norm_g", "s5_lambda_re", "s5_lambda_im",
           "s5_log_dt", "s5_b_re", "s5_b_im", "s5_c_re", "s5_c_im", "s5_d", "s5_w_glu", "mem_norm_g", "w_kv_mem",
           "w_br_a", "w_br_b", "w_br_c", "w_out", "final_g")
W_IN_SPLIT = 4096


W_IN_COLS = PROJ_W - BA_PAD + 2 * NHEAD
W_IN_GATES = W_IN_COLS - GATE_W
W_IN_MOVES = ((0, W_IN_SPLIT, GATE_W), (W_IN_SPLIT, W_IN_SPLIT + 2 * NHEAD, PROJ_W - BA_PAD - W_IN_SPLIT),
              (W_IN_SPLIT + 2 * NHEAD, W_IN_GATES, GATE_W - 2 * NHEAD), (W_IN_GATES, W_IN_COLS, -W_IN_GATES))


def _pack_w_in(shards):
    cs = shards.shape[2]
    parts = []
    for a, b, _ in sorted(W_IN_MOVES, key=lambda move: move[0] + move[2]):
        while a < b:
            j = a // cs
            hi = min(b, (j + 1) * cs)
            parts.append(shards[j, :, a - j * cs:hi - j * cs])
            a = hi
    parts.append(jnp.zeros((shards.shape[1], BA_PAD - 2 * NHEAD), shards.dtype))
    return jnp.concatenate(parts, axis=1)


def _unpack_w_in(wp):
    cs = W_IN_COLS // 4
    shards = []
    for j in range(4):
        parts = []
        for lo, hi, shift in W_IN_MOVES:
            s, e = max(j * cs, lo), min((j + 1) * cs, hi)
            if s < e:
                parts.append(wp[:, s + shift:e + shift])
        shards.append(jnp.concatenate(parts, axis=1))
    return jnp.stack(shards)


def _pack_small(arrs):
    parts = []
    for a in arrs:
        f = a.reshape(-1).astype(F32)
        parts.append(jnp.pad(f, (0, (-f.shape[0]) % 128)))
    flat = jnp.concatenate(parts)
    rows = flat.shape[0] // 128
    return jnp.pad(flat.reshape(rows, 128), ((0, (-rows) % 16), (0, 0)))


def _unpack_small(flat2d, shapes):
    f = flat2d.reshape(-1)
    out, off = [], 0
    for shp in shapes:
        n = math.prod(shp)
        out.append(f[off:off + n].reshape(shp))
        off += n + (-n) % 128
    return out


def kernel(x, mem, norm_g, w_in, conv_w, gdn_a_log, gdn_dt_bias, gdn_norm_g, s5_lambda_re, s5_lambda_im, s5_log_dt, s5_b_re, s5_b_im, s5_c_re, s5_c_im, s5_d, s5_w_glu, mem_norm_g, w_kv_mem, w_br_a, w_br_b, w_br_c, w_out, final_g, loss_target, m_norm_g, m_w_in, m_conv_w, m_gdn_a_log, m_gdn_dt_bias, m_gdn_norm_g, m_s5_lambda_re, m_s5_lambda_im, m_s5_log_dt, m_s5_b_re, m_s5_b_im, m_s5_c_re, m_s5_c_im, m_s5_d, m_s5_w_glu, m_mem_norm_g, m_w_kv_mem, m_w_br_a, m_w_br_b, m_w_br_c, m_w_out, m_final_g, v_norm_g, v_w_in, v_conv_w, v_gdn_a_log, v_gdn_dt_bias, v_gdn_norm_g, v_s5_lambda_re, v_s5_lambda_im, v_s5_log_dt, v_s5_b_re, v_s5_b_im, v_s5_c_re, v_s5_c_im, v_s5_d, v_s5_w_glu, v_mem_norm_g, v_w_kv_mem, v_w_br_a, v_w_br_b, v_w_br_c, v_w_out, v_final_g):
    wts = dict(norm_g=norm_g, w_in=w_in, conv_w=conv_w, gdn_a_log=gdn_a_log, gdn_dt_bias=gdn_dt_bias, gdn_norm_g=gdn_norm_g,
               s5_lambda_re=s5_lambda_re, s5_lambda_im=s5_lambda_im, s5_log_dt=s5_log_dt, s5_b_re=s5_b_re, s5_b_im=s5_b_im,
               s5_c_re=s5_c_re, s5_c_im=s5_c_im, s5_d=s5_d, s5_w_glu=s5_w_glu, mem_norm_g=mem_norm_g, w_kv_mem=w_kv_mem,
               w_br_a=w_br_a, w_br_b=w_br_b, w_br_c=w_br_c, w_out=w_out, final_g=final_g)
    mom = dict(norm_g=m_norm_g, w_in=m_w_in, conv_w=m_conv_w, gdn_a_log=m_gdn_a_log, gdn_dt_bias=m_gdn_dt_bias,
               gdn_norm_g=m_gdn_norm_g, s5_lambda_re=m_s5_lambda_re, s5_lambda_im=m_s5_lambda_im, s5_log_dt=m_s5_log_dt,
               s5_b_re=m_s5_b_re, s5_b_im=m_s5_b_im, s5_c_re=m_s5_c_re, s5_c_im=m_s5_c_im, s5_d=m_s5_d, s5_w_glu=m_s5_w_glu,
               mem_norm_g=m_mem_norm_g, w_kv_mem=m_w_kv_mem, w_br_a=m_w_br_a, w_br_b=m_w_br_b, w_br_c=m_w_br_c, w_out=m_w_out,
               final_g=m_final_g)
    vel = dict(norm_g=v_norm_g, w_in=v_w_in, conv_w=v_conv_w, gdn_a_log=v_gdn_a_log, gdn_dt_bias=v_gdn_dt_bias,
               gdn_norm_g=v_gdn_norm_g, s5_lambda_re=v_s5_lambda_re, s5_lambda_im=v_s5_lambda_im, s5_log_dt=v_s5_log_dt,
               s5_b_re=v_s5_b_re, s5_b_im=v_s5_b_im, s5_c_re=v_s5_c_re, s5_c_im=v_s5_c_im, s5_d=v_s5_d, s5_w_glu=v_s5_w_glu,
               mem_norm_g=v_mem_norm_g, w_kv_mem=v_w_kv_mem, w_br_a=v_w_br_a, w_br_b=v_w_br_b, w_br_c=v_w_br_c, w_out=v_w_out,
               final_g=v_final_g)
    x2, mem2, tgt = x[0], mem[0], loss_target[0]
    s, d = x2.shape
    n_chunks = s // CHUNK

    shards = [wts[n][0].astype(BF16) for n in BIG]
    wp = _pack_w_in(allgather_weights(shards[:1], None, "allgather_w_in")[0])
    mm = functools.partial(matmul, tm=1024, tn=1024)
    u, r1 = rms_fwd(x2, norm_g, "rms_fwd_x")
    proj, *rest, cg = mm(u, wp, mode="nn", out_dtype=F32, tk=2048, name="mm_proj", side=gather_side(shards[1:], conv_w[0]))
    full = {}
    for n, wg in zip(BIG[1:], forward_halves(rest)):
        rows, cols = wg.shape[1:]
        full[n] = wg.transpose(1, 0, 2).reshape(rows, 4 * cols) if n in COL_SHARDED else wg.reshape(4 * rows, cols)
    conv_full = cg.transpose(1, 0, 2).reshape(conv_w.shape[1], -1)
    alog_pad = jnp.pad(gdn_a_log, ((0, 0), (NHEAD, BA_W - 2 * NHEAD)))
    dt_pad = jnp.pad(gdn_dt_bias, ((0, 0), (NHEAD, BA_W - 2 * NHEAD)))

    q, k, v, bg, gcol, gt = gdn_prep_fwd(proj, conv_full, alog_pad, dt_pad)
    gt3 = gt.reshape(BA_W, n_chunks, CHUNK).transpose(1, 0, 2)
    gu, gw, qd, kd, qk, tinv = gdn_intra_fwd(q, k, v, bg, gcol, gt3)
    o_raw, states = gdn_seq_fwd(gu, gw, qd, kd, qk, gt3)
    ga = gdn_out_fwd(o_raw, proj, ZA_CB, gdn_norm_g)

    xb = proj[:, XB_CB * S5_IN:(XB_CB + 1) * S5_IN]
    y_ssm, s5_saved = s5_ssm_fwd(xb, s5_lambda_re[0], s5_lambda_im[0], s5_log_dt[0], s5_b_re[0], s5_b_im[0],
                                 s5_c_re[0], s5_c_im[0])
    yb = s5_act_fwd(y_ssm, proj, XB_CB, s5_d)
    glu = mm(yb, full["s5_w_glu"], mode="nn", out_dtype=F32, tk=1024, name="mm_glu")
    gb = s5_glu_fwd(glu, proj, ZB_CB)

    mem_n, rm = rms_fwd(mem2, mem_norm_g, "rms_fwd_mem")
    kv = mm(mem_n, full["w_kv_mem"], mode="nn", out_dtype=BF16, tk=2048, name="mm_kv")
    o_c = xa_fwd(proj, kv)
    gcx = gate_fwd(o_c, proj, ZC_CB, "gate_fwd_c")

    pa = mm(ga, full["w_br_a"], mode="nn", out_dtype=F32, tk=1024, name="mm_pa")
    pb = mm(gb, full["w_br_b"], mode="nn", out_dtype=F32, tk=1024, name="mm_pb")
    pc = mm(gcx, full["w_br_c"], mode="nn", out_dtype=F32, tk=1024, name="mm_pc")
    merged = merge_fwd(pa, pb, pc, proj, GATE_CB)
    hres = mm(merged, full["w_out"], mode="nn", out_dtype=F32, tk=2048, name="mm_out")
    dh, dhb, loss_part, d_final_g = final_stage(x2, hres, tgt, final_g.reshape(1, d))

    gfull = {}
    dmerged = mm(dhb, full["w_out"], mode="nt", out_dtype=F32, tk=2048, name="mm_dmerged")
    gfull["w_out"] = mm(merged, dhb, mode="tn", out_dtype=BF16, tk=1024, name="mm_dw_out")
    dproj = lax.empty((s, PROJ_W), BF16)
    dpa, dpb, dpc, dproj = merge_bwd(dmerged, pa, pb, pc, proj, GATE_CB, dproj)
    dga = mm(dpa, full["w_br_a"], mode="nt", out_dtype=F32, tk=2048, name="mm_dga")
    dgb = mm(dpb, full["w_br_b"], mode="nt", out_dtype=F32, tk=2048, name="mm_dgb")
    dgc = mm(dpc, full["w_br_c"], mode="nt", out_dtype=F32, tk=2048, name="mm_dgc")
    gfull["w_br_a"] = mm(ga, dpa, mode="tn", out_dtype=BF16, tk=1024, name="mm_dw_a")
    gfull["w_br_b"] = mm(gb, dpb, mode="tn", out_dtype=BF16, tk=1024, name="mm_dw_b")
    gfull["w_br_c"] = mm(gcx, dpc, mode="tn", out_dtype=BF16, tk=1024, name="mm_dw_c")

    do_raw, dproj, d_gdn_norm = gdn_out_bwd(dga, o_raw, proj, ZA_CB, gdn_norm_g, dproj)
    du_, dw_, dqd, dkd, dqk, dgl = gdn_seq_bwd(do_raw, gu, gw, qd, kd, qk, gt3, states)
    dq, dk, dv, dbg = gdn_intra_bwd(q, k, v, bg, gcol, gt3, tinv, du_, dw_, dqd, dkd, dqk, dgl)
    dc, dproj, dcw0, dcw1, dcw2, dcw3, d_alog, d_dt = gdn_prep_bwd1(proj, conv_full, alog_pad, dt_pad, dq, dk, dv, dbg, dproj)
    dproj = gdn_prep_bwd2(dc, conv_full, dproj)
    d_conv = jnp.concatenate([dcw0, dcw1, dcw2, dcw3], axis=0)

    dval, dgate, dproj = s5_glu_bwd(dgb, glu, proj, ZB_CB, dproj)
    dglu = jnp.concatenate([dval, dgate], axis=1)
    dyb = mm(dglu, full["s5_w_glu"], mode="nt", out_dtype=F32, tk=2048, name="mm_dyb")
    gfull["s5_w_glu"] = mm(yb, dglu, mode="tn", out_dtype=BF16, tk=1024, name="mm_dw_glu")
    dy_ssm, dxb_direct, d_s5_d = s5_act_bwd(dyb, y_ssm, proj, XB_CB, s5_d)
    dxb_scan, d_lre, d_lim, d_ldt, d_bre, d_bim, d_cre, d_cim = s5_ssm_bwd(dy_ssm, s5_saved)
    dproj = add_into(dxb_direct, dxb_scan, "s5_dxb", dproj, XB_CB)

    do_c, dproj = gate_bwd(dgc, o_c, proj, ZC_CB, "gate_bwd_c", dproj)
    dproj, dkv = xa_bwd(do_c, proj, kv, dproj)
    gfull["w_kv_mem"] = mm(mem_n, dkv, mode="tn", out_dtype=BF16, tk=256, name="mm_dw_kv")
    dmem_n = mm(dkv, full["w_kv_mem"], mode="nt", out_dtype=F32, tk=2048, name="mm_dmem")
    d_mem_norm = rms_bwd_g(dmem_n, mem2, rm, "rms_bwd_mem")

    core = lax.axis_index("c").astype(jnp.int32).reshape(1)
    by_shard = []
    for n in BIG[1:]:
        rows, cols = wts[n].shape[1:]
        g = gfull[n]
        by_shard.append(g.reshape(rows, 4, cols).transpose(1, 0, 2) if n in COL_SHARDED else g.reshape(4, rows, cols))
    got_rest = exchange_cores(by_shard, "exchange_cores_rest")
    chip_rest = [pair_sum(core, g, r, "sum_cores_" + n) for n, g, r in zip(BIG[1:], by_shard, got_rest)]

    dwp, *from_chips_rest = matmul(u.T, dproj, mode="nn", out_dtype=BF16, tm=2048, tn=1024, tk=1024, name="mm_dw_in",
                                   side=chips_side(chip_rest))
    w_in_shards = _unpack_w_in(dwp)
    got_in, = exchange_cores([w_in_shards], "exchange_cores_w_in")
    chip_in = pair_sum(core, w_in_shards, got_in, "sum_cores_w_in")
    du, from_chips_in = matmul(dproj, wp, mode="nt", out_dtype=F32, tm=2048, tn=1024, tk=512, name="mm_du",
                               side=chips_side([chip_in]))
    grad_x, d_norm_g = rms_bwd_x(du, x2, r1, norm_g, dh)
    from_chips = [from_chips_in] + from_chips_rest
    small_g = dict(norm_g=d_norm_g, gdn_a_log=d_alog[:, NHEAD:2 * NHEAD], gdn_dt_bias=d_dt[:, NHEAD:2 * NHEAD],
                   gdn_norm_g=d_gdn_norm, s5_lambda_re=d_lre, s5_lambda_im=d_lim, s5_log_dt=d_ldt, s5_b_re=d_bre, s5_b_im=d_bim,
                   s5_c_re=d_cre, s5_c_im=d_cim, s5_d=d_s5_d, mem_norm_g=d_mem_norm, final_g=d_final_g)
    small_send = _pack_small([small_g[n] for n in SMALL] + [d_conv, loss_part])
    fulls = [sum_chips(core, a, "sum_chips_" + n) for n, a in zip(BIG, from_chips)]
    small_sum = sum_pieces(exchange_small(small_send), "sum_small")
    grads = dict(zip(BIG, sibling_exchange(fulls)))
    small_shapes = [wts[n].shape for n in SMALL] + [d_conv.shape, (1, 1)]
    *small_list, conv_g_full, loss_sum = _unpack_small(small_sum, small_shapes)
    grads.update(zip(SMALL, small_list))
    cw = conv_w.shape[2]
    shard_idx = 2 * lax.axis_index("x") + lax.axis_index("y")
    grads["conv_w"] = lax.dynamic_slice(conv_g_full, (0, shard_idx * cw), (conv_w.shape[1], cw))[None]

    delta, new_m, new_v = {}, {}, {}
    for n in BIG + ("conv_w",):
        delta[n], new_m[n], new_v[n] = adamw(wts[n], grads[n], mom[n], vel[n], "adamw_" + n)
    packed_w, packed_m, packed_v = (_pack_small([src[n] for n in SMALL]) for src in (wts, mom, vel))
    res = adamw(packed_w, small_sum[:packed_w.shape[0]], packed_m, packed_v, "adamw_small")
    shapes = [wts[n].shape for n in SMALL]
    for dst, flat in zip((delta, new_m, new_v), res):
        dst.update(zip(SMALL, _unpack_small(flat, shapes)))
    for n in SMALL:
        grads[n] = grads[n].reshape(wts[n].shape)

    return (loss_sum.reshape(()), grad_x.reshape(x.shape), *[grads[n] for n in WEIGHTS], *[delta[n] for n in WEIGHTS],
            *[new_m[n] for n in WEIGHTS], *[new_v[n] for n in WEIGHTS])
```

```python
import functools
import math

import jax
import jax.numpy as jnp
from jax import lax
from jax.experimental import pallas as pl
from jax.experimental.pallas import tpu as pltpu

F32 = jnp.float32
BF16 = jnp.bfloat16
HI = lax.Precision.HIGHEST

EPS = 1e-6
CHUNK = 64
HEAD = 128
NHEAD = 8
XA_HEADS = 4
S5_GROUPS = 64
S5_STATE = 64
S5_GROUP = 16
NSEG = 8
ADAM_LR, ADAM_B1, ADAM_B2, ADAM_EPS, ADAM_WD, ADAM_STEP = 0.001, 0.9, 0.999, 1e-08, 0.01, 10
VMEM_LIMIT = 56 * 2 ** 20


def _cparams(sem=None):
    return pltpu.CompilerParams(dimension_semantics=sem, vmem_limit_bytes=VMEM_LIMIT)


def _sigmoid(x):
    return 1.0 / (1.0 + jnp.exp(-x))


def _silu(x):
    return x * _sigmoid(x)


def _dsilu(x):
    s = _sigmoid(x)
    return s * (1.0 + x * (1.0 - s))


def _softplus(x):
    return jnp.maximum(x, 0.0) + jnp.log(1.0 + jnp.exp(-jnp.abs(x)))


_GELU_C = math.sqrt(2.0 / math.pi)


def _gelu(x):
    return 0.5 * x * (1.0 + jnp.tanh(_GELU_C * (x + 0.044715 * x * x * x)))


def _dgelu(x):
    t = jnp.tanh(_GELU_C * (x + 0.044715 * x * x * x))
    return 0.5 * (1.0 + t) + 0.5 * x * (1.0 - t * t) * _GELU_C * (1.0 + 3.0 * 0.044715 * x * x)


_DIMS = {"nn": (((1,), (0,)), ((), ())), "nt": (((1,), (1,)), ((), ())), "tn": (((0,), (0,)), ((), ()))}


class Side:
    def __init__(self, operands, out_shapes, scratch, start, finish):
        self.operands, self.out_shapes, self.scratch, self.start, self.finish = operands, out_shapes, scratch, start, finish


def matmul(a, b, *, mode, out_dtype, tm, tn, tk, name, side=None):
    if mode == "nn":
        (m, k), n = a.shape, b.shape[1]
    elif mode == "nt":
        (m, k), n = a.shape, b.shape[0]
    else:
        (k, m), n = a.shape, b.shape[1]
    tm, tn, tk = min(tm, m), min(tn, n), min(tk, k)
    assert m % tm == 0 and n % tn == 0 and k % tk == 0, (name, m, n, k, tm, tn, tk)
    grid = (m // tm, n // tn, k // tk)
    nk = grid[2]
    dims = _DIMS[mode]
    n_in = 0 if side is None else len(side.operands)
    n_out = 0 if side is None else len(side.out_shapes)
    n_acc = 0 if nk == 1 else 1

    def body(*refs):
        a_ref, b_ref, o_ref = refs[0], refs[1], refs[2 + n_in]
        scratch = refs[3 + n_in + n_out:]
        side_refs = (refs[2:2 + n_in], refs[3 + n_in:3 + n_in + n_out], scratch[n_acc:])
        ids = [pl.program_id(d) for d in range(3)]
        if side is not None:
            @pl.when((ids[0] == 0) & (ids[1] == 0) & (ids[2] == 0))
            def _():
                side.start(*side_refs)

        prod = lax.dot_general(a_ref[...].astype(BF16), b_ref[...].astype(BF16), dims, preferred_element_type=F32)
        if nk == 1:
            o_ref[...] = prod.astype(out_dtype)
        else:
            acc_ref = scratch[0]

            @pl.when(ids[2] == 0)
            def _():
                acc_ref[...] = prod

            @pl.when(ids[2] > 0)
            def _():
                acc_ref[...] += prod

            @pl.when(ids[2] == nk - 1)
            def _():
                o_ref[...] = acc_ref[...].astype(out_dtype)

        if side is not None:
            @pl.when((ids[0] == grid[0] - 1) & (ids[1] == grid[1] - 1) & (ids[2] == nk - 1))
            def _():
                side.finish(*side_refs)

    a_spec = pl.BlockSpec((tk, tm), lambda i, j, q: (q, i)) if mode == "tn" else pl.BlockSpec((tm, tk), lambda i, j, q: (i, q))
    b_spec = pl.BlockSpec((tn, tk), lambda i, j, q: (j, q)) if mode == "nt" else pl.BlockSpec((tk, tn), lambda i, j, q: (q, j))
    o_spec = pl.BlockSpec((tm, tn), lambda i, j, q: (i, j))
    o_shape = jax.ShapeDtypeStruct((m, n), out_dtype)
    acc = [] if nk == 1 else [pltpu.VMEM((tm, tn), F32)]
    if side is None:
        return pl.pallas_call(
            body, name=name, grid=grid, in_specs=[a_spec, b_spec], out_specs=o_spec, out_shape=o_shape, scratch_shapes=acc,
            compiler_params=_cparams(("parallel", "parallel", "arbitrary")),
        )(a, b)
    hbm = pl.BlockSpec(memory_space=pltpu.HBM)
    return pl.pallas_call(
        body, name=name, grid=grid, in_specs=[a_spec, b_spec] + [hbm] * n_in, out_specs=[o_spec] + [hbm] * n_out,
        out_shape=[o_shape] + list(side.out_shapes), scratch_shapes=acc + list(side.scratch),
        compiler_params=_cparams(("arbitrary", "arbitrary", "arbitrary")),
    )(a, b, *side.operands)


def rowwise(fn, ins, outs, *, rows, tr, name, consts=(), reds=(), into=None):
    tr = min(tr, rows)
    assert rows % tr == 0, (name, rows, tr)
    n_in, n_c, n_o = len(ins), len(consts), len(outs)
    n_buf = 0 if into is None else 1

    def body(*refs):
        vals = [r[...].astype(F32) for r in refs[:n_in + n_c]]
        res = fn(*vals)
        o_refs = refs[n_in + n_c + n_buf:]
        for r, v in zip(o_refs[:n_o], res[:n_o]):
            r[...] = v.astype(r.dtype)
        if reds:
            i = pl.program_id(0)

            @pl.when(i == 0)
            def _():
                for r, v in zip(o_refs[n_o:], res[n_o:]):
                    r[...] = v.astype(r.dtype)

            @pl.when(i > 0)
            def _():
                for r, v in zip(o_refs[n_o:], res[n_o:]):
                    r[...] += v.astype(r.dtype)

    in_specs = [pl.BlockSpec((tr, w), functools.partial(lambda i, cb: (i, cb), cb=cb)) for (_, w, cb) in ins]
    in_specs += [pl.BlockSpec(c.shape, lambda i: (0, 0)) for c in consts]
    out_specs = [pl.BlockSpec((tr, w), lambda i: (i, 0)) for (w, _) in outs]
    out_specs += [pl.BlockSpec(s, lambda i: (0, 0)) for (s, _) in reds]
    out_shape = [jax.ShapeDtypeStruct((rows, w), d) for (w, d) in outs]
    out_shape += [jax.ShapeDtypeStruct(s, d) for (s, d) in reds]
    operands = [a for (a, _, _) in ins] + list(consts)
    aliases = {}
    if into is not None:
        buf, pos, cb = into
        assert buf.dtype == outs[pos][1] and buf.shape[0] == rows, (name, buf.shape, buf.dtype)
        in_specs.append(pl.BlockSpec(memory_space=pl.ANY))
        operands.append(buf)
        out_specs[pos] = pl.BlockSpec((tr, outs[pos][0]), functools.partial(lambda i, cb: (i, cb), cb=cb))
        out_shape[pos] = jax.ShapeDtypeStruct(buf.shape, buf.dtype)
        aliases = {len(operands) - 1: pos}
    return pl.pallas_call(
        body, name=name, grid=(rows // tr,), in_specs=in_specs, out_specs=out_specs, out_shape=out_shape,
        input_output_aliases=aliases, compiler_params=_cparams(("arbitrary",) if reds else ("parallel",)),
    )(*operands)


def _colsum(x):
    return jnp.sum(x, axis=0, keepdims=True)


def rms_fwd(x, g, name):
    s, d = x.shape

    def fn(xv, gv):
        r = lax.rsqrt(jnp.mean(xv * xv, axis=-1, keepdims=True) + EPS)
        return xv * r * gv, r

    return rowwise(fn, [(x, d, 0)], [(d, BF16), (1, F32)], rows=s, tr=256, name=name, consts=[g])


def rms_bwd_x(du, x, r, g, dh):
    s, d = x.shape

    def fn(duv, xv, rv, dhv, gv):
        dyg = duv * gv
        dx = rv * dyg - xv * (rv * rv * rv) * jnp.mean(dyg * xv, axis=-1, keepdims=True)
        return dhv + dx, _colsum(duv * xv * rv)

    return rowwise(fn, [(du, d, 0), (x, d, 0), (r, 1, 0), (dh, d, 0)], [(d, F32)], rows=s, tr=256,
                   name="rms_bwd_x", consts=[g], reds=[((1, d), F32)])


def rms_bwd_g(du, x, r, name):
    s, d = x.shape

    def fn(duv, xv, rv):
        return (_colsum(duv * xv * rv),)

    return rowwise(fn, [(du, d, 0), (x, d, 0), (r, 1, 0)], [], rows=s, tr=256, name=name, reds=[((1, d), F32)])[0]


def final_stage(x, hres, target, g):
    s, d = x.shape

    def fn(xv, hv, tv, gv):
        h = xv + hv
        r = lax.rsqrt(jnp.mean(h * h, axis=-1, keepdims=True) + EPS)
        y = h * r * gv
        e = y - tv
        loss = 0.5 * jnp.sum(jnp.sum(e * e, axis=-1, keepdims=True), axis=0, keepdims=True) / d
        dy = e / d
        dyg = dy * gv
        dh = r * dyg - h * (r * r * r) * jnp.mean(dyg * h, axis=-1, keepdims=True)
        return dh, dh, loss, _colsum(dy * h * r)

    return rowwise(fn, [(x, d, 0), (hres, d, 0), (target, d, 0)], [(d, F32), (d, BF16)], rows=s, tr=256,
                   name="final_stage", consts=[g], reds=[((1, 1), F32), ((1, d), F32)])


def merge_fwd(pa, pb, pc, proj, gate_cb):
    s, d = pa.shape

    def fn(a, b, c, g0, g1, g2):
        return (_sigmoid(g0) * a + _sigmoid(g1) * b + _sigmoid(g2) * c,)

    ins = [(pa, d, 0), (pb, d, 0), (pc, d, 0)] + [(proj, d, gate_cb + i) for i in range(3)]
    return rowwise(fn, ins, [(d, BF16)], rows=s, tr=256, name="merge_fwd")[0]


def merge_bwd(dm, pa, pb, pc, proj, gate_cb, dproj):
    s, d = pa.shape

    def fn(dmv, a, b, c, g0, g1, g2):
        s0, s1, s2 = _sigmoid(g0), _sigmoid(g1), _sigmoid(g2)
        dgates = [dmv * a * s0 * (1.0 - s0), dmv * b * s1 * (1.0 - s1), dmv * c * s2 * (1.0 - s2)]
        return dmv * s0, dmv * s1, dmv * s2, jnp.concatenate(dgates, axis=1)

    ins = [(dm, d, 0), (pa, d, 0), (pb, d, 0), (pc, d, 0)] + [(proj, d, gate_cb + i) for i in range(3)]
    return rowwise(fn, ins, [(d, BF16)] * 3 + [(3 * d, BF16)], rows=s, tr=128, name="merge_bwd", into=(dproj, 3, gate_cb // 3))


def gate_fwd(o, proj, z_cb, name):
    s, w = o.shape

    def fn(ov, zv):
        return (ov * _silu(zv),)

    return rowwise(fn, [(o, w, 0), (proj, w, z_cb)], [(w, BF16)], rows=s, tr=512, name=name)[0]


def gate_bwd(dgo, o, proj, z_cb, name, dproj):
    s, w = o.shape

    def fn(dv, ov, zv):
        return dv * _silu(zv), dv * ov * _dsilu(zv)

    return rowwise(fn, [(dgo, w, 0), (o, w, 0), (proj, w, z_cb)], [(w, F32), (w, BF16)], rows=s, tr=512, name=name,
                   into=(dproj, 1, z_cb))


def gdn_out_fwd(o_raw, proj, z_cb, gn):
    s, w = o_raw.shape

    def fn(ov, zv, gv):
        outs = []
        for h in range(NHEAD):
            oh = ov[:, h * HEAD:(h + 1) * HEAD]
            r = lax.rsqrt(jnp.mean(oh * oh, axis=-1, keepdims=True) + EPS)
            outs.append(oh * r * gv)
        return (jnp.concatenate(outs, axis=1) * _silu(zv),)

    return rowwise(fn, [(o_raw, w, 0), (proj, w, z_cb)], [(w, BF16)], rows=s, tr=512, name="gdn_out_fwd", consts=[gn])[0]


def gdn_out_bwd(dga, o_raw, proj, z_cb, gn, dproj):
    s, w = o_raw.shape

    def fn(dv, ov, zv, gv):
        sz, dsz = _silu(zv), _dsilu(zv)
        do_l, dz_l = [], []
        dg = jnp.zeros((1, HEAD), F32)
        for h in range(NHEAD):
            sl = slice(h * HEAD, (h + 1) * HEAD)
            oh, dgh = ov[:, sl], dv[:, sl]
            r = lax.rsqrt(jnp.mean(oh * oh, axis=-1, keepdims=True) + EPS)
            on = oh * r * gv
            don = dgh * sz[:, sl]
            dz_l.append(dgh * on * dsz[:, sl])
            dg = dg + _colsum(don * oh * r)
            dyg = don * gv
            do_l.append(r * dyg - oh * (r * r * r) * jnp.mean(dyg * oh, axis=-1, keepdims=True))
        return jnp.concatenate(do_l, axis=1), jnp.concatenate(dz_l, axis=1), dg

    return rowwise(fn, [(dga, w, 0), (o_raw, w, 0), (proj, w, z_cb)], [(w, F32), (w, BF16)], rows=s, tr=512,
                   name="gdn_out_bwd", consts=[gn], reds=[((1, HEAD), F32)], into=(dproj, 1, z_cb))


def s5_act_fwd(y_ssm, proj, xb_cb, dvec):
    s, w = y_ssm.shape

    def fn(yv, xv, dv):
        return (_gelu(yv + dv * xv),)

    return rowwise(fn, [(y_ssm, w, 0), (proj, w, xb_cb)], [(w, BF16)], rows=s, tr=512, name="s5_act_fwd", consts=[dvec])[0]


def s5_act_bwd(dyb, y_ssm, proj, xb_cb, dvec):
    s, w = y_ssm.shape

    def fn(dv_, yv, xv, dv):
        dpre = dv_ * _dgelu(yv + dv * xv)
        return dpre, dpre * dv, _colsum(dpre * xv)

    return rowwise(fn, [(dyb, w, 0), (y_ssm, w, 0), (proj, w, xb_cb)], [(w, F32), (w, F32)], rows=s, tr=512,
                   name="s5_act_bwd", consts=[dvec], reds=[((1, w), F32)])


def s5_glu_fwd(glu, proj, z_cb):
    s, w2 = glu.shape
    w = w2 // 2

    def fn(val, gate, zv):
        return (val * _sigmoid(gate) * _silu(zv),)

    return rowwise(fn, [(glu, w, 0), (glu, w, 1), (proj, w, z_cb)], [(w, BF16)], rows=s, tr=512, name="s5_glu_fwd")[0]


def s5_glu_bwd(dgb, glu, proj, z_cb, dproj):
    s, w2 = glu.shape
    w = w2 // 2

    def fn(dv, val, gate, zv):
        sg = _sigmoid(gate)
        ob = val * sg
        dob = dv * _silu(zv)
        return dob * sg, dob * val * sg * (1.0 - sg), dv * ob * _dsilu(zv)

    return rowwise(fn, [(dgb, w, 0), (glu, w, 0), (glu, w, 1), (proj, w, z_cb)], [(w, BF16)] * 3, rows=s, tr=512,
                   name="s5_glu_bwd", into=(dproj, 2, z_cb))


def add_into(a, b, name, dproj, cb):
    s, w = a.shape

    def fn(av, bv):
        return (av + bv,)

    return rowwise(fn, [(a, w, 0), (b, w, 0)], [(w, BF16)], rows=s, tr=512, name=name, into=(dproj, 0, cb))[0]


GATE_W, GATE_CB = 6144, 0
QKV_W, QKV_CB = 3072, 2
ZA_CB, XB_CB, ZB_CB, QC_CB, ZC_CB = 9, 10, 11, 12, 13
BA_CB, BA_W = 112, 128
BA_PAD, BA_PAD_CB = 1024, 14
PROJ_W = 14336 + BA_PAD


def _dot(a, b, dims="nn", prec=None):
    if prec is None:
        a, b = a.astype(BF16), b.astype(BF16)
    return lax.dot_general(a, b, _DIMS[dims], preferred_element_type=F32, precision=prec)


def _split(a):
    hi = a.astype(BF16)
    return hi, (a - hi.astype(F32)).astype(BF16)


def _dot3(a, b, dims="nn"):
    (ah, al), (bh, bl) = _split(a), _split(b)
    d = functools.partial(lax.dot_general, dimension_numbers=_DIMS[dims], preferred_element_type=F32)
    return d(ah, bh) + (d(ah, bl) + d(al, bh))


def _iota2(shape, dim):
    return lax.broadcasted_iota(jnp.int32, shape, dim)


def _conv_taps(xs, tr, k):
    if k == 0:
        return xs[8:8 + tr]
    return pltpu.roll(xs, k, 0)[8:8 + tr]


def _conv_silu_parts(xv, halo, wv, first):
    tr = xv.shape[0]
    xs = jnp.concatenate([jnp.where(first, 0.0, halo), xv], axis=0)
    taps = [_conv_taps(xs, tr, 3 - j) for j in range(4)]
    c = taps[0] * wv[0:1] + taps[1] * wv[1:2] + taps[2] * wv[2:3] + taps[3] * wv[3:4]
    return taps, c


def gdn_prep_fwd(proj, conv_w, alog_pad, dt_pad):
    s = proj.shape[0]
    tr = min(256, s)
    w = NHEAD * HEAD

    def body(x_ref, halo_ref, ba_ref, w_ref, al_ref, dt_ref, q_ref, k_ref, v_ref, bg_ref, gcol_ref, gt_ref):
        first = pl.program_id(0) == 0
        _, c = _conv_silu_parts(x_ref[...], halo_ref[...], w_ref[...], first)
        sv = _silu(c)
        for h in range(NHEAD):
            sl = slice(h * HEAD, (h + 1) * HEAD)
            qh, kh = sv[:, h * HEAD:(h + 1) * HEAD], sv[:, w + h * HEAD:w + (h + 1) * HEAD]
            q_ref[:, sl] = qh * lax.rsqrt(jnp.sum(qh * qh, axis=-1, keepdims=True) + EPS) * (HEAD ** -0.5)
            k_ref[:, sl] = kh * lax.rsqrt(jnp.sum(kh * kh, axis=-1, keepdims=True) + EPS)
        v_ref[...] = sv[:, 2 * w:]
        ba = ba_ref[...]
        lane = _iota2(ba.shape, 1)
        beta = _sigmoid(ba)
        g = -jnp.exp(al_ref[...]) * _softplus(ba + dt_ref[...])
        bg = jnp.where(lane < NHEAD, beta, jnp.where(lane < 2 * NHEAD, g, 0.0))
        bg_ref[...] = bg
        er, ec = _iota2((BA_W, BA_W), 0), _iota2((BA_W, BA_W), 1)
        expand = jnp.where((er == NHEAD + ec // 8) & (ec < 8 * NHEAD), 1.0, 0.0)
        grep = _dot(bg, expand, prec=HI)
        lr, lc = _iota2((tr, tr), 0), _iota2((tr, tr), 1)
        tril = jnp.where((lr // CHUNK == lc // CHUNK) & (lr >= lc), 1.0, 0.0)
        gc = _dot(tril, grep, prec=HI)
        gcol_ref[...] = gc
        gt_ref[...] = gc.T

    nb8 = tr // 8
    return pl.pallas_call(
        body, name="gdn_prep_fwd", grid=(s // tr,),
        in_specs=[pl.BlockSpec((tr, QKV_W), lambda i: (i, QKV_CB)),
                  pl.BlockSpec((8, QKV_W), lambda i: (jnp.maximum(i * nb8 - 1, 0), QKV_CB)),
                  pl.BlockSpec((tr, BA_W), lambda i: (i, BA_CB)),
                  pl.BlockSpec(conv_w.shape, lambda i: (0, 0)),
                  pl.BlockSpec((1, BA_W), lambda i: (0, 0)), pl.BlockSpec((1, BA_W), lambda i: (0, 0))],
        out_specs=[pl.BlockSpec((tr, w), lambda i: (i, 0))] * 3 + [pl.BlockSpec((tr, BA_W), lambda i: (i, 0))] * 2
        + [pl.BlockSpec((BA_W, tr), lambda i: (0, i))],
        out_shape=[jax.ShapeDtypeStruct((s, w), F32)] * 3 + [jax.ShapeDtypeStruct((s, BA_W), F32)] * 2
        + [jax.ShapeDtypeStruct((BA_W, s), F32)],
        compiler_params=_cparams(("parallel",)),
    )(proj, proj, proj, conv_w, alog_pad, dt_pad)


def _chunk_common(qh, kh, bgv, gcolv, gtv, h):
    beta = bgv[:, h:h + 1]
    gcc = gcolv[:, 8 * h:8 * h + 1]
    gcr = jnp.concatenate([gtv[8 * h:8 * h + 8, :]] * (CHUNK // 8), axis=0)
    ii, jj = _iota2((CHUNK, CHUNK), 0), _iota2((CHUNK, CHUNK), 1)
    incl, strict = ii >= jj, ii > jj
    decay = jnp.where(incl, jnp.exp(jnp.where(incl, gcc - gcr, 0.0)), 0.0)
    gl = gcr[:, CHUNK - 1:CHUNK]
    return beta, gcc, decay, strict, gl


def gdn_intra_fwd(q, k, v, bg, gcol, gt3):
    s, w = q.shape
    n = s // CHUNK

    def body(q_ref, k_ref, v_ref, bg_ref, gcol_ref, gt_ref, u_ref, w_ref, qd_ref, kd_ref, qk_ref, t_ref):
        bgv, gcolv, gtv = bg_ref[...], gcol_ref[...], gt_ref[0]
        ii, jj = _iota2((CHUNK, CHUNK), 0), _iota2((CHUNK, CHUNK), 1)
        eye = jnp.where(ii == jj, 1.0, 0.0)
        ps, ts, rhs = [], [], []
        for h in range(NHEAD):
            sl = slice(h * HEAD, (h + 1) * HEAD)
            qh, kh, vh = q_ref[:, sl], k_ref[:, sl], v_ref[:, sl]
            beta, gcc, decay, strict, gl = _chunk_common(qh, kh, bgv, gcolv, gtv, h)
            kb = kh * beta
            eg = jnp.exp(gcc)
            p = -jnp.where(strict, _dot(kb, kh, "nt") * decay, 0.0)
            ps.append(p)
            ts.append(eye + p)
            rhs.append((vh * beta, kb * eg))
            qd_ref[:, sl] = qh * eg
            kd_ref[:, sl] = kh * jnp.exp(gl - gcc)
            qk_ref[0, h] = _dot(qh, kh, "nt") * decay
        for _ in range(5):
            ps = [_dot3(p, p) for p in ps]
            ts = [t + _dot3(t, p) for t, p in zip(ts, ps)]
        for h in range(NHEAD):
            sl = slice(h * HEAD, (h + 1) * HEAD)
            u_ref[:, sl] = _dot3(ts[h], rhs[h][0])
            w_ref[:, sl] = _dot3(ts[h], rhs[h][1])
            t_ref[0, h] = ts[h]

    tok = pl.BlockSpec((CHUNK, w), lambda i: (i, 0))
    sm = pl.BlockSpec((CHUNK, BA_W), lambda i: (i, 0))
    sq = pl.BlockSpec((1, NHEAD, CHUNK, CHUNK), lambda i: (i, 0, 0, 0))
    return pl.pallas_call(
        body, name="gdn_intra_fwd", grid=(n,),
        in_specs=[tok, tok, tok, sm, sm, pl.BlockSpec((1, BA_W, CHUNK), lambda i: (i, 0, 0))],
        out_specs=[tok] * 4 + [sq, sq],
        out_shape=[jax.ShapeDtypeStruct((s, w), F32)] * 4 + [jax.ShapeDtypeStruct((n, NHEAD, CHUNK, CHUNK), F32)] * 2,
        compiler_params=_cparams(("parallel",)),
    )(q, k, v, bg, gcol, gt3)


def _state_decay(gtv, h):
    g8 = gtv[8 * h:8 * h + 8, CHUNK - 1:CHUNK]
    return jnp.exp(jnp.concatenate([g8] * (HEAD // 8), axis=0))


def gdn_seq_fwd(u, wd, qd, kd, qk, gt3):
    s, w = u.shape
    n = s // CHUNK

    def body(u_ref, w_ref, qd_ref, kd_ref, qk_ref, gt_ref, o_ref, st_ref, s_ref):
        @pl.when(pl.program_id(0) == 0)
        def _():
            s_ref[...] = jnp.zeros_like(s_ref)

        gtv = gt_ref[0]
        cols = [slice(h * HEAD, (h + 1) * HEAD) for h in range(NHEAD)]
        states = [s_ref[h] for h in range(NHEAD)]
        for h in range(NHEAD):
            st_ref[0, h] = states[h]
        vns = [u_ref[:, cols[h]] - _dot(w_ref[:, cols[h]], states[h]) for h in range(NHEAD)]
        from_state = [_dot(qd_ref[:, cols[h]], states[h]) for h in range(NHEAD)]
        for h in range(NHEAD):
            o_ref[:, cols[h]] = from_state[h] + _dot(qk_ref[0, h], vns[h])
        for h in range(NHEAD):
            s_ref[h] = states[h] * _state_decay(gtv, h) + _dot(kd_ref[:, cols[h]], vns[h], "tn")

    tok = pl.BlockSpec((CHUNK, w), lambda i: (i, 0))
    return pl.pallas_call(
        body, name="gdn_seq_fwd", grid=(n,),
        in_specs=[tok] * 4 + [pl.BlockSpec((1, NHEAD, CHUNK, CHUNK), lambda i: (i, 0, 0, 0)),
                              pl.BlockSpec((1, BA_W, CHUNK), lambda i: (i, 0, 0))],
        out_specs=[tok, pl.BlockSpec((1, NHEAD, HEAD, HEAD), lambda i: (i, 0, 0, 0))],
        out_shape=[jax.ShapeDtypeStruct((s, w), F32), jax.ShapeDtypeStruct((n, NHEAD, HEAD, HEAD), F32)],
        scratch_shapes=[pltpu.VMEM((NHEAD, HEAD, HEAD), F32)],
        compiler_params=_cparams(("arbitrary",)),
    )(u, wd, qd, kd, qk, gt3)


def gdn_seq_bwd(do, u, wd, qd, kd, qk, gt3, states):
    s, w = u.shape
    n = s // CHUNK

    def body(do_ref, u_ref, w_ref, qd_ref, kd_ref, qk_ref, gt_ref, st_ref,
             du_ref, dw_ref, dqd_ref, dkd_ref, dqk_ref, dgl_ref, ds_ref):
        @pl.when(pl.program_id(0) == 0)
        def _():
            ds_ref[...] = jnp.zeros_like(ds_ref)

        gtv = gt_ref[0]
        heads = range(NHEAD)
        cols = [slice(h * HEAD, (h + 1) * HEAD) for h in heads]
        sts = [st_ref[0, h] for h in heads]
        dsps = [ds_ref[h] for h in heads]
        vns = [u_ref[:, cols[h]] - _dot(w_ref[:, cols[h]], sts[h]) for h in heads]
        dvns = [_dot(qk_ref[0, h], do_ref[:, cols[h]], "tn") + _dot(kd_ref[:, cols[h]], dsps[h]) for h in heads]
        for h in heads:
            du_ref[:, cols[h]] = dvns[h]
            dqd_ref[:, cols[h]] = _dot(do_ref[:, cols[h]], sts[h], "nt")
        for h in heads:
            dw_ref[:, cols[h]] = -_dot(dvns[h], sts[h], "nt")
            dkd_ref[:, cols[h]] = _dot(vns[h], dsps[h], "nt")
            dqk_ref[0, h] = _dot(do_ref[:, cols[h]], vns[h], "nt")
        for h in heads:
            ds_ref[h] = (dsps[h] * _state_decay(gtv, h) + _dot(qd_ref[:, cols[h]], do_ref[:, cols[h]], "tn")
                         - _dot(w_ref[:, cols[h]], dvns[h], "tn"))
        dgl_ref[0] = jnp.concatenate([_colsum(sts[h] * dsps[h]) for h in heads], axis=0)

    tok = pl.BlockSpec((CHUNK, w), lambda i: (n - 1 - i, 0))
    sq = pl.BlockSpec((1, NHEAD, CHUNK, CHUNK), lambda i: (n - 1 - i, 0, 0, 0))
    return pl.pallas_call(
        body, name="gdn_seq_bwd", grid=(n,),
        in_specs=[tok] * 5 + [sq, pl.BlockSpec((1, BA_W, CHUNK), lambda i: (n - 1 - i, 0, 0)),
                              pl.BlockSpec((1, NHEAD, HEAD, HEAD), lambda i: (n - 1 - i, 0, 0, 0))],
        out_specs=[tok] * 4 + [sq, pl.BlockSpec((1, NHEAD, HEAD), lambda i: (n - 1 - i, 0, 0))],
        out_shape=[jax.ShapeDtypeStruct((s, w), F32)] * 4 + [jax.ShapeDtypeStruct((n, NHEAD, CHUNK, CHUNK), F32),
                                                            jax.ShapeDtypeStruct((n, NHEAD, HEAD), F32)],
        scratch_shapes=[pltpu.VMEM((NHEAD, HEAD, HEAD), F32)],
        compiler_params=_cparams(("arbitrary",)),
    )(do, u, wd, qd, kd, qk, gt3, states)


def gdn_intra_bwd(q, k, v, bg, gcol, gt3, tinv, du, dw, dqd, dkd, dqk, dgl):
    s, w = q.shape
    n = s // CHUNK

    def body(q_ref, k_ref, v_ref, bg_ref, gcol_ref, gt_ref, t_ref, du_ref, dw_ref, dqd_ref, dkd_ref, dqk_ref, dgl_ref,
             dq_ref, dk_ref, dv_ref, dbg_ref):
        bgv, gcolv, gtv, dglv = bg_ref[...], gcol_ref[...], gt_ref[0], dgl_ref[0]
        ii, jj = _iota2((CHUNK, CHUNK), 0), _iota2((CHUNK, CHUNK), 1)
        triu = jnp.where(ii <= jj, 1.0, 0.0)
        ones = jnp.ones((CHUNK, BA_W), F32)
        lane = _iota2((CHUNK, BA_W), 1)
        row = _iota2((CHUNK, 1), 0)
        dbg = jnp.zeros((CHUNK, BA_W), F32)
        first = []
        for h in range(NHEAD):
            sl = slice(h * HEAD, (h + 1) * HEAD)
            qh, kh, vh = q_ref[:, sl], k_ref[:, sl], v_ref[:, sl]
            beta, gcc, decay, strict, gl = _chunk_common(qh, kh, bgv, gcolv, gtv, h)
            kb = kh * beta
            eg = jnp.exp(gcc)
            rv, rk = vh * beta, kb * eg
            t, duh, dwh = t_ref[0, h], du_ref[:, sl], dw_ref[:, sl]
            first.append((_dot3(duh, rv, "nt") + _dot3(dwh, rk, "nt"), _dot3(t, duh, "tn"), _dot3(t, dwh, "tn"),
                          _dot(kb, kh, "nt"), _dot(qh, kh, "nt")))
        second = [_dot3(t_ref[0, h], first[h][0], "tn") for h in range(NHEAD)]
        third = [_dot3(second[h], t_ref[0, h], "nt") for h in range(NHEAD)]
        for h in range(NHEAD):
            sl = slice(h * HEAD, (h + 1) * HEAD)
            qh, kh, vh = q_ref[:, sl], k_ref[:, sl], v_ref[:, sl]
            beta, gcc, decay, strict, gl = _chunk_common(qh, kh, bgv, gcolv, gtv, h)
            dqdh, dkdh, dqkh = dqd_ref[:, sl], dkd_ref[:, sl], dqk_ref[0, h]
            kb = kh * beta
            eg = jnp.exp(gcc)
            ekd = jnp.exp(gl - gcc)
            rk = kb * eg
            _, drv, drk, m, p = first[h]
            da = jnp.where(strict, -third[h], 0.0)
            dm = da * decay
            dpm = dqkh * decay
            dkb = _dot(dm, kh) + drk * eg
            dq = _dot(dpm, kh) + dqdh * eg
            dk = _dot(dm, kb, "tn") + _dot(dpm, qh, "tn") + dkdh * ekd + dkb * beta
            e = (da * m + dqkh * p) * decay
            sk = jnp.sum(dkdh * kh * ekd, axis=-1, keepdims=True)
            dgc = (jnp.sum(e, axis=-1, keepdims=True) - _dot3(e, ones, "tn")[:, 0:1]
                   + jnp.sum(dqdh * qh * eg, axis=-1, keepdims=True) - sk + jnp.sum(drk * rk, axis=-1, keepdims=True))
            dglast = jnp.sum(sk, axis=0, keepdims=True) + jnp.sum(dglv[h:h + 1, :], axis=-1, keepdims=True) * jnp.exp(gl)
            dgc = dgc + jnp.where(row == CHUNK - 1, dglast, 0.0)
            dg = _dot3(triu, dgc * ones)
            dbeta = jnp.sum(dkb * kh, axis=-1, keepdims=True) + jnp.sum(drv * vh, axis=-1, keepdims=True)
            dbg = dbg + jnp.where(lane == h, dbeta, 0.0) + jnp.where(lane == NHEAD + h, dg, 0.0)
            dq_ref[:, sl] = dq
            dk_ref[:, sl] = dk
            dv_ref[:, sl] = drv * beta
        dbg_ref[...] = dbg

    tok = pl.BlockSpec((CHUNK, w), lambda i: (i, 0))
    sm = pl.BlockSpec((CHUNK, BA_W), lambda i: (i, 0))
    sq = pl.BlockSpec((1, NHEAD, CHUNK, CHUNK), lambda i: (i, 0, 0, 0))
    return pl.pallas_call(
        body, name="gdn_intra_bwd", grid=(n,),
        in_specs=[tok, tok, tok, sm, sm, pl.BlockSpec((1, BA_W, CHUNK), lambda i: (i, 0, 0)), sq,
                  tok, tok, tok, tok, sq, pl.BlockSpec((1, NHEAD, HEAD), lambda i: (i, 0, 0))],
        out_specs=[tok] * 3 + [sm],
        out_shape=[jax.ShapeDtypeStruct((s, w), F32)] * 3 + [jax.ShapeDtypeStruct((s, BA_W), F32)],
        compiler_params=_cparams(("parallel",)),
    )(q, k, v, bg, gcol, gt3, tinv, du, dw, dqd, dkd, dqk, dgl)


def gdn_prep_bwd1(proj, conv_w, alog_pad, dt_pad, dq, dk, dv, dbg, dproj):
    s = proj.shape[0]
    tr = min(256, s)
    w = NHEAD * HEAD
    pad_w = BA_PAD

    def body(x_ref, halo_ref, ba_ref, w_ref, al_ref, dt_ref, dq_ref, dk_ref, dv_ref, dbg_ref, buf_ref,
             dc_ref, dba_ref, dw0_ref, dw1_ref, dw2_ref, dw3_ref, dal_ref, ddt_ref):
        i = pl.program_id(0)
        taps, c = _conv_silu_parts(x_ref[...], halo_ref[...], w_ref[...], i == 0)
        sv, dsv = _silu(c), _dsilu(c)
        for h in range(NHEAD):
            for base, d_ref, scale in ((0, dq_ref, HEAD ** -0.5), (w, dk_ref, 1.0)):
                sl = slice(base + h * HEAD, base + (h + 1) * HEAD)
                sh = sv[:, sl]
                dn = d_ref[:, h * HEAD:(h + 1) * HEAD]
                r = lax.rsqrt(jnp.sum(sh * sh, axis=-1, keepdims=True) + EPS)
                dsh = scale * (r * dn - sh * (r * r * r) * jnp.sum(dn * sh, axis=-1, keepdims=True))
                dc_ref[:, sl] = dsh * dsv[:, sl]
        dc_ref[:, 2 * w:] = dv_ref[...] * dsv[:, 2 * w:]
        dc = dc_ref[...]
        ba, dbgv = ba_ref[...], dbg_ref[...]
        lane = _iota2(ba.shape, 1)
        beta = _sigmoid(ba)
        ea = jnp.exp(al_ref[...])
        z = ba + dt_ref[...]
        g = -ea * _softplus(z)
        is_g = (lane >= NHEAD) & (lane < 2 * NHEAD)
        da_raw = jnp.where(is_g, dbgv * (-ea) * _sigmoid(z), 0.0)
        dba = jnp.where(lane < NHEAD, dbgv * beta * (1.0 - beta), da_raw)
        dba_ref[...] = jnp.concatenate([dba, jnp.zeros((tr, pad_w - BA_W), F32)], axis=1).astype(BF16)
        partial = [_colsum(dc * tp) for tp in taps] + [_colsum(jnp.where(is_g, dbgv * g, 0.0)), _colsum(da_raw)]
        red_refs = (dw0_ref, dw1_ref, dw2_ref, dw3_ref, dal_ref, ddt_ref)

        @pl.when(i == 0)
        def _():
            for r_, v_ in zip(red_refs, partial):
                r_[...] = v_

        @pl.when(i > 0)
        def _():
            for r_, v_ in zip(red_refs, partial):
                r_[...] += v_

    nb8 = tr // 8
    tok = pl.BlockSpec((tr, w), lambda i: (i, 0))
    one = lambda width: pl.BlockSpec((1, width), lambda i: (0, 0))
    return pl.pallas_call(
        body, name="gdn_prep_bwd1", grid=(s // tr,),
        in_specs=[pl.BlockSpec((tr, QKV_W), lambda i: (i, QKV_CB)),
                  pl.BlockSpec((8, QKV_W), lambda i: (jnp.maximum(i * nb8 - 1, 0), QKV_CB)),
                  pl.BlockSpec((tr, BA_W), lambda i: (i, BA_CB)),
                  pl.BlockSpec(conv_w.shape, lambda i: (0, 0)), one(BA_W), one(BA_W),
                  tok, tok, tok, pl.BlockSpec((tr, BA_W), lambda i: (i, 0)), pl.BlockSpec(memory_space=pl.ANY)],
        out_specs=[pl.BlockSpec((tr, QKV_W), lambda i: (i, 0)), pl.BlockSpec((tr, pad_w), lambda i: (i, BA_PAD_CB))]
        + [one(QKV_W)] * 4 + [one(BA_W)] * 2,
        out_shape=[jax.ShapeDtypeStruct((s, QKV_W), F32), jax.ShapeDtypeStruct(dproj.shape, dproj.dtype)]
        + [jax.ShapeDtypeStruct((1, QKV_W), F32)] * 4 + [jax.ShapeDtypeStruct((1, BA_W), F32)] * 2,
        input_output_aliases={10: 1}, compiler_params=_cparams(("arbitrary",)),
    )(proj, proj, proj, conv_w, alog_pad, dt_pad, dq, dk, dv, dbg, dproj)


def gdn_prep_bwd2(dc, conv_w, dproj):
    s = dc.shape[0]
    tr = min(256, s)
    nblk = s // tr
    nb8 = tr // 8

    def body(dc_ref, halo_ref, w_ref, buf_ref, o_ref):
        last = pl.program_id(0) == nblk - 1
        wv = w_ref[...]
        xs = jnp.concatenate([dc_ref[...], jnp.where(last, 0.0, halo_ref[...])], axis=0)
        acc = xs[:tr] * wv[3:4]
        for j in range(3):
            acc = acc + pltpu.roll(xs, tr + 8 - (3 - j), 0)[:tr] * wv[j:j + 1]
        o_ref[...] = acc.astype(BF16)

    return pl.pallas_call(
        body, name="gdn_prep_bwd2", grid=(nblk,),
        in_specs=[pl.BlockSpec((tr, QKV_W), lambda i: (i, 0)),
                  pl.BlockSpec((8, QKV_W), lambda i: (jnp.minimum((i + 1) * nb8, s // 8 - 1), 0)),
                  pl.BlockSpec(conv_w.shape, lambda i: (0, 0)), pl.BlockSpec(memory_space=pl.ANY)],
        out_specs=pl.BlockSpec((tr, QKV_W), lambda i: (i, QKV_CB)),
        out_shape=jax.ShapeDtypeStruct(dproj.shape, dproj.dtype), input_output_aliases={3: 0},
        compiler_params=_cparams(("parallel",)),
    )(dc, dc, conv_w, dproj)


S5_W = S5_GROUPS * S5_STATE
S5_IN = S5_GROUPS * S5_GROUP
S5_TILES = 8
S5_TW, S5_TI = S5_W // S5_TILES, S5_IN // S5_TILES


def _s5_param_math(lr, li, ldt, br, bi):
    pr, pc = _iota2((S5_STATE, S5_STATE * S5_GROUP), 0), _iota2((S5_STATE, S5_STATE * S5_GROUP), 1)
    rep = jnp.where(pc // S5_GROUP == pr, 1.0, 0.0)
    dt = jnp.exp(ldt)
    mag = jnp.exp(lr * dt)
    ab_re, ab_im = mag * jnp.cos(li * dt), mag * jnp.sin(li * dt)
    den = lr * lr + li * li
    nr, ni = ab_re - 1.0, ab_im
    coef_re = (nr * lr + ni * li) / den
    coef_im = (ni * lr - nr * li) / den
    cr, ci = _dot(coef_re, rep, prec=HI), _dot(coef_im, rep, prec=HI)
    return ab_re, ab_im, cr * br - ci * bi, cr * bi + ci * br


def s5_param_fwd(lr, li, ldt, br, bi):
    def body(lr_ref, li_ref, ldt_ref, br_ref, bi_ref, ar_ref, ai_ref, bbr_ref, bbi_ref):
        res = _s5_param_math(lr_ref[...], li_ref[...], ldt_ref[...], br_ref[...], bi_ref[...])
        for r, v in zip((ar_ref, ai_ref, bbr_ref, bbi_ref), res):
            r[...] = v

    return pl.pallas_call(
        body, name="s5_param_fwd",
        out_shape=[jax.ShapeDtypeStruct(lr.shape, F32)] * 2 + [jax.ShapeDtypeStruct(br.shape, F32)] * 2,
        compiler_params=_cparams(),
    )(lr, li, ldt, br, bi)


def s5_param_bwd(lr, li, ldt, br, bi, dar, dai, dbbr, dbbi):
    def body(lr_ref, li_ref, ldt_ref, br_ref, bi_ref, dar_ref, dai_ref, dbbr_ref, dbbi_ref, *out_refs):
        _, vjp = jax.vjp(_s5_param_math, lr_ref[...], li_ref[...], ldt_ref[...], br_ref[...], bi_ref[...])
        for r, v in zip(out_refs, vjp((dar_ref[...], dai_ref[...], dbbr_ref[...], dbbi_ref[...]))):
            r[...] = v

    return pl.pallas_call(
        body, name="s5_param_bwd",
        out_shape=[jax.ShapeDtypeStruct(a.shape, F32) for a in (lr, li, ldt, br, bi)],
        compiler_params=_cparams(),
    )(lr, li, ldt, br, bi, dar, dai, dbbr, dbbi)


def _cmul(ar, ai, br, bi):
    return ar * br - ai * bi, ar * bi + ai * br


def _s5_power(ar, ai, steps):
    assert steps & (steps - 1) == 0
    for _ in range(steps.bit_length() - 1):
        ar, ai = _cmul(ar, ai, ar, ai)
    return ar, ai


def _s5_scan_rows(ar_ref, ai_ref, re_ref, im_ref, sr_ref, si_ref, tb, row0, reverse):
    quarter = S5_W // 4
    for qd in range(4):
        cs = slice(qd * quarter, (qd + 1) * quarter)
        are = jnp.broadcast_to(ar_ref[:, cs], (NSEG, quarter))
        aim = jnp.broadcast_to(ai_ref[:, cs], (NSEG, quarter))
        if reverse:
            aim = -aim

        def step(t, carry):
            h_r, h_i = carry
            tt = tb - 1 - t if reverse else t
            rows = pl.ds(pl.multiple_of(row0 + tt * NSEG, NSEG), NSEG)
            n_r = are * h_r - aim * h_i + re_ref[rows, cs]
            n_i = are * h_i + aim * h_r + im_ref[rows, cs]
            re_ref[rows, cs] = n_r
            im_ref[rows, cs] = n_i
            return n_r, n_i

        h_r, h_i = lax.fori_loop(0, tb, step, (sr_ref[:, cs], si_ref[:, cs]), unroll=8)
        sr_ref[:, cs] = h_r
        si_ref[:, cs] = h_i


def _s5_segment_carry(ar_ref, ai_ref, sr_ref, si_ref, steps, reverse):
    pr, pi = _s5_power(ar_ref[...], ai_ref[...], steps)
    if reverse:
        pi = -pi
    cur_r = jnp.zeros((1, S5_W), F32)
    cur_i = jnp.zeros((1, S5_W), F32)
    for s in (range(NSEG - 1, -1, -1) if reverse else range(NSEG)):
        e_r, e_i = sr_ref[s:s + 1, :], si_ref[s:s + 1, :]
        sr_ref[s:s + 1, :] = cur_r
        si_ref[s:s + 1, :] = cur_i
        nr, ni = _cmul(pr, pi, cur_r, cur_i)
        cur_r, cur_i = nr + e_r, ni + e_i


def _s5_blocks(s):
    steps = s // NSEG
    tb = min(32, steps)
    return steps, tb, NSEG * tb, steps // tb


def s5_scan_fwd(xp, a_re, a_im, bre, bim, cre, cim):
    s = xp.shape[0]
    steps, tb, rb, nb = _s5_blocks(s)

    def body(x_ref, ar_ref, ai_ref, bre_ref, bim_ref, cre_ref, cim_ref, y_ref, hsr_ref, hsi_ref,
             hr_ref, hi_ref, sr_ref, si_ref):
        ph, b = pl.program_id(0), pl.program_id(1)

        @pl.when((ph == 0) & (b == 0))
        def _():
            sr_ref[...] = jnp.zeros_like(sr_ref)
            si_ref[...] = jnp.zeros_like(si_ref)

        @pl.when((ph == 1) & (b == 0))
        def _():
            _s5_segment_carry(ar_ref, ai_ref, sr_ref, si_ref, steps, False)

        xv = x_ref[...].astype(BF16)
        for j in range(S5_TILES):
            xs = xv[:, j * S5_TI:(j + 1) * S5_TI]
            hr_ref[:, j * S5_TW:(j + 1) * S5_TW] = _dot(xs, bre_ref[j])
            hi_ref[:, j * S5_TW:(j + 1) * S5_TW] = _dot(xs, bim_ref[j])

        @pl.when(ph == 1)
        def _():
            hsr_ref[0] = sr_ref[...]
            hsi_ref[0] = si_ref[...]

        _s5_scan_rows(ar_ref, ai_ref, hr_ref, hi_ref, sr_ref, si_ref, tb, 0, False)

        @pl.when(ph == 1)
        def _():
            for j in range(S5_TILES):
                cs = slice(j * S5_TW, (j + 1) * S5_TW)
                y_ref[:, j * S5_TI:(j + 1) * S5_TI] = _dot(hr_ref[:, cs], cre_ref[j]) - _dot(hi_ref[:, cs], cim_ref[j])

    row = pl.BlockSpec((1, S5_W), lambda p, b: (0, 0))
    wb = pl.BlockSpec((S5_TILES, S5_TI, S5_TW), lambda p, b: (0, 0, 0))
    wc = pl.BlockSpec((S5_TILES, S5_TW, S5_TI), lambda p, b: (0, 0, 0))
    st = pl.BlockSpec((1, NSEG, S5_W), lambda p, b: (p * b, 0, 0))
    return pl.pallas_call(
        body, name="s5_scan_fwd", grid=(2, nb),
        in_specs=[pl.BlockSpec((rb, S5_IN), lambda p, b: (b, 0)), row, row, wb, wb, wc, wc],
        out_specs=[pl.BlockSpec((rb, S5_IN), lambda p, b: (p * b, 0)), st, st],
        out_shape=[jax.ShapeDtypeStruct((s, S5_IN), F32)] + [jax.ShapeDtypeStruct((nb, NSEG, S5_W), F32)] * 2,
        scratch_shapes=[pltpu.VMEM((rb, S5_W), F32)] * 2 + [pltpu.VMEM((NSEG, S5_W), F32)] * 2,
        compiler_params=_cparams(("arbitrary", "arbitrary")),
    )(xp, a_re, a_im, bre, bim, cre, cim)


def s5_scan_bwd(dyp, xp, a_re, a_im, bre, bim, cre_t, cim_t, hs_r, hs_i):
    s = xp.shape[0]
    steps, tb, rb, nb = _s5_blocks(s)

    def body(dy_ref, x_ref, ar_ref, ai_ref, bre_ref, bim_ref, crt_ref, cit_ref, hsr_ref, hsi_ref,
             dx_ref, dar_ref, dai_ref, dbr_ref, dbi_ref, dcr_ref, dci_ref,
             hr_ref, hi_ref, lr_ref, li_ref, sr_ref, si_ref, fr_ref, fi_ref, accr_ref, acci_ref):
        ph, b = pl.program_id(0), pl.program_id(1)

        @pl.when((ph == 0) & (b == 0))
        def _():
            sr_ref[...] = jnp.zeros_like(sr_ref)
            si_ref[...] = jnp.zeros_like(si_ref)

        @pl.when((ph == 1) & (b == 0))
        def _():
            _s5_segment_carry(ar_ref, ai_ref, sr_ref, si_ref, steps, True)
            for r in (accr_ref, acci_ref, dbr_ref, dbi_ref, dcr_ref, dci_ref):
                r[...] = jnp.zeros_like(r)

        dyv = dy_ref[...].astype(BF16)
        for j in range(S5_TILES):
            ds_ = dyv[:, j * S5_TI:(j + 1) * S5_TI]
            lr_ref[:, j * S5_TW:(j + 1) * S5_TW] = _dot(ds_, crt_ref[j])
            li_ref[:, j * S5_TW:(j + 1) * S5_TW] = -_dot(ds_, cit_ref[j])
        _s5_scan_rows(ar_ref, ai_ref, lr_ref, li_ref, sr_ref, si_ref, tb, 0, True)

        @pl.when(ph == 1)
        def _():
            xv = x_ref[...].astype(BF16)
            for j in range(S5_TILES):
                xs = xv[:, j * S5_TI:(j + 1) * S5_TI]
                hr_ref[NSEG:, j * S5_TW:(j + 1) * S5_TW] = _dot(xs, bre_ref[j])
                hi_ref[NSEG:, j * S5_TW:(j + 1) * S5_TW] = _dot(xs, bim_ref[j])
            hr_ref[0:NSEG, :] = hsr_ref[0]
            hi_ref[0:NSEG, :] = hsi_ref[0]
            fr_ref[...] = hsr_ref[0]
            fi_ref[...] = hsi_ref[0]
            _s5_scan_rows(ar_ref, ai_ref, hr_ref, hi_ref, fr_ref, fi_ref, tb, NSEG, False)
            lam_r, lam_i = lr_ref[...], li_ref[...]
            hp_r, hp_i = hr_ref[0:rb, :], hi_ref[0:rb, :]
            accr_ref[...] += jnp.sum((lam_r * hp_r + lam_i * hp_i).reshape(tb, NSEG, S5_W), axis=0)
            acci_ref[...] += jnp.sum((lam_i * hp_r - lam_r * hp_i).reshape(tb, NSEG, S5_W), axis=0)
            lam_rb, lam_ib = lam_r.astype(BF16), lam_i.astype(BF16)
            h_rb, h_ib = hr_ref[NSEG:, :].astype(BF16), hi_ref[NSEG:, :].astype(BF16)
            for j in range(S5_TILES):
                cs, ci = slice(j * S5_TW, (j + 1) * S5_TW), slice(j * S5_TI, (j + 1) * S5_TI)
                dbr_ref[j] += _dot(xv[:, ci], lam_rb[:, cs], "tn")
                dbi_ref[j] += _dot(xv[:, ci], lam_ib[:, cs], "tn")
                dx_ref[:, ci] = _dot(lam_rb[:, cs], bre_ref[j], "nt") + _dot(lam_ib[:, cs], bim_ref[j], "nt")
                dcr_ref[j] += _dot(h_rb[:, cs], dyv[:, ci], "tn")
                dci_ref[j] -= _dot(h_ib[:, cs], dyv[:, ci], "tn")

        @pl.when((ph == 1) & (b == nb - 1))
        def _():
            dar_ref[...] = jnp.sum(accr_ref[...], axis=0, keepdims=True)
            dai_ref[...] = jnp.sum(acci_ref[...], axis=0, keepdims=True)

    rev = lambda p, b: (nb - 1 - b, 0)
    row = pl.BlockSpec((1, S5_W), lambda p, b: (0, 0))
    wb = pl.BlockSpec((S5_TILES, S5_TI, S5_TW), lambda p, b: (0, 0, 0))
    wc = pl.BlockSpec((S5_TILES, S5_TW, S5_TI), lambda p, b: (0, 0, 0))
    st = pl.BlockSpec((1, NSEG, S5_W), lambda p, b: (nb - 1 - b, 0, 0))
    big = pltpu.VMEM((rb, S5_W), F32)
    big8 = pltpu.VMEM((rb + NSEG, S5_W), F32)
    small = pltpu.VMEM((NSEG, S5_W), F32)
    return pl.pallas_call(
        body, name="s5_scan_bwd", grid=(2, nb),
        in_specs=[pl.BlockSpec((rb, S5_IN), rev), pl.BlockSpec((rb, S5_IN), rev), row, row, wb, wb, wb, wb, st, st],
        out_specs=[pl.BlockSpec((rb, S5_IN), lambda p, b: (nb - 1 - p * b, 0)), row, row, wb, wb, wc, wc],
        out_shape=[jax.ShapeDtypeStruct((s, S5_IN), F32)] + [jax.ShapeDtypeStruct((1, S5_W), F32)] * 2
        + [jax.ShapeDtypeStruct((S5_TILES, S5_TI, S5_TW), F32)] * 2 + [jax.ShapeDtypeStruct((S5_TILES, S5_TW, S5_TI), F32)] * 2,
        scratch_shapes=[big8, big8, big, big, small, small, small, small, small, small],
        compiler_params=_cparams(("arbitrary", "arbitrary")),
    )(dyp, xp, a_re, a_im, bre, bim, cre_t, cim_t, hs_r, hs_i)


XA_DIM = 256
XA_W = XA_HEADS * XA_DIM


def _xa_probs(qh, kh):
    sc = _dot(qh, kh, "nt") * (XA_DIM ** -0.5)
    ex = jnp.exp(sc - jnp.max(sc, axis=-1, keepdims=True))
    return ex / jnp.sum(ex, axis=-1, keepdims=True)


def xa_fwd(proj, kv):
    s = proj.shape[0]
    tq = min(512, s)

    def body(q_ref, kv_ref, o_ref):
        for h in range(XA_HEADS):
            sl = slice(h * XA_DIM, (h + 1) * XA_DIM)
            p = _xa_probs(q_ref[:, sl], kv_ref[:, sl])
            o_ref[:, sl] = _dot(p, kv_ref[:, XA_W + h * XA_DIM:XA_W + (h + 1) * XA_DIM])

    return pl.pallas_call(
        body, name="xa_fwd", grid=(s // tq,),
        in_specs=[pl.BlockSpec((tq, XA_W), lambda i: (i, QC_CB)), pl.BlockSpec(kv.shape, lambda i: (0, 0))],
        out_specs=pl.BlockSpec((tq, XA_W), lambda i: (i, 0)),
        out_shape=jax.ShapeDtypeStruct((s, XA_W), F32),
        compiler_params=_cparams(("parallel",)),
    )(proj, kv)


def xa_bwd(do, proj, kv, dproj):
    s = proj.shape[0]
    tq = min(512, s)

    def body(do_ref, q_ref, kv_ref, buf_ref, dq_ref, dkv_ref):
        @pl.when(pl.program_id(0) == 0)
        def _():
            dkv_ref[...] = jnp.zeros_like(dkv_ref)

        for h in range(XA_HEADS):
            sl = slice(h * XA_DIM, (h + 1) * XA_DIM)
            sv = slice(XA_W + h * XA_DIM, XA_W + (h + 1) * XA_DIM)
            qh, kh, vh, doh = q_ref[:, sl], kv_ref[:, sl], kv_ref[:, sv], do_ref[:, sl]
            p = _xa_probs(qh, kh)
            dp = _dot(doh, vh, "nt")
            ds_ = p * (dp - jnp.sum(dp * p, axis=-1, keepdims=True)) * (XA_DIM ** -0.5)
            dq_ref[:, sl] = _dot(ds_, kh).astype(BF16)
            dkv_ref[:, sl] += _dot(ds_, qh, "tn")
            dkv_ref[:, sv] += _dot(p, doh, "tn")

    return pl.pallas_call(
        body, name="xa_bwd", grid=(s // tq,),
        in_specs=[pl.BlockSpec((tq, XA_W), lambda i: (i, 0)), pl.BlockSpec((tq, XA_W), lambda i: (i, QC_CB)),
                  pl.BlockSpec(kv.shape, lambda i: (0, 0)), pl.BlockSpec(memory_space=pl.ANY)],
        out_specs=[pl.BlockSpec((tq, XA_W), lambda i: (i, QC_CB)), pl.BlockSpec(kv.shape, lambda i: (0, 0))],
        out_shape=[jax.ShapeDtypeStruct(dproj.shape, dproj.dtype), jax.ShapeDtypeStruct(kv.shape, F32)],
        input_output_aliases={3: 0}, compiler_params=_cparams(("arbitrary",)),
    )(do, proj, kv, dproj)


def _adamw_math(wv, gv, mv, vv):
    m2 = ADAM_B1 * mv + (1.0 - ADAM_B1) * gv
    v2 = ADAM_B2 * vv + (1.0 - ADAM_B2) * (gv * gv)
    m_hat = m2 / (1.0 - ADAM_B1 ** ADAM_STEP)
    v_hat = v2 / (1.0 - ADAM_B2 ** ADAM_STEP)
    return -ADAM_LR * (m_hat / (jnp.sqrt(v_hat) + ADAM_EPS) + ADAM_WD * wv), m2, v2


def adamw(w, g, m, v, name):
    lead = (0,) * (w.ndim - 2)
    rows, cols = w.shape[-2:]
    tr = rows
    while tr * cols * 4 * 7 * 2 > 36 * 2 ** 20 and tr % 16 == 0:
        tr //= 2

    def body(w_ref, g_ref, m_ref, v_ref, d_ref, m2_ref, v2_ref):
        d_ref[...], m2_ref[...], v2_ref[...] = _adamw_math(w_ref[...], g_ref[...], m_ref[...], v_ref[...])

    spec = pl.BlockSpec((1,) * len(lead) + (tr, cols), lambda i: lead + (i, 0))
    return pl.pallas_call(
        body, name=name, grid=(rows // tr,), in_specs=[spec] * 4, out_specs=[spec] * 3,
        out_shape=[jax.ShapeDtypeStruct(w.shape, F32)] * 3, compiler_params=_cparams(("parallel",)),
    )(w, g, m, v)


def _seg_perm(a):
    s, w = a.shape
    return a.reshape(NSEG, s // NSEG, w).transpose(1, 0, 2).reshape(s, w)


def _seg_unperm(a):
    s, w = a.shape
    return a.reshape(s // NSEG, NSEG, w).transpose(1, 0, 2).reshape(s, w)


def _block_diag(t):
    nt, _, r, c = t.shape
    eye = jnp.eye(8, dtype=bool)
    return jnp.where(eye[None, :, None, :, None], t[:, :, :, None, :], 0.0).reshape(nt, 8 * r, 8 * c)


def _block_diag_inv(d, r, c):
    d5 = d.reshape(d.shape[0], 8, r, 8, c)
    return jnp.diagonal(d5, axis1=1, axis2=3).transpose(0, 3, 1, 2)


def _s5_b_tiles(bb):
    return _block_diag(bb.reshape(S5_TILES, 8, S5_STATE, S5_GROUP).transpose(0, 1, 3, 2))


def _s5_b_untile(d):
    return _block_diag_inv(d, S5_GROUP, S5_STATE).transpose(0, 1, 3, 2).reshape(S5_GROUPS, S5_STATE * S5_GROUP)


def _s5_c_tiles(c):
    return _block_diag(c.reshape(S5_TILES, 8, S5_GROUP, S5_STATE).transpose(0, 1, 3, 2))


def _s5_c_untile(d):
    return _block_diag_inv(d, S5_STATE, S5_GROUP).transpose(0, 1, 3, 2).reshape(S5_GROUPS, S5_GROUP, S5_STATE)


def s5_ssm_fwd(xb, lam_re, lam_im, log_dt, b_re, b_im, c_re, c_im):
    br, bi = b_re.reshape(S5_GROUPS, -1), b_im.reshape(S5_GROUPS, -1)
    ldt = log_dt.reshape(S5_GROUPS, 1)
    ab_re, ab_im, bb_re, bb_im = s5_param_fwd(lam_re, lam_im, ldt, br, bi)
    a_re, a_im = ab_re.reshape(1, S5_W), ab_im.reshape(1, S5_W)
    bre, bim = _s5_b_tiles(bb_re).astype(BF16), _s5_b_tiles(bb_im).astype(BF16)
    cre, cim = _s5_c_tiles(c_re).astype(BF16), _s5_c_tiles(c_im).astype(BF16)
    xp = _seg_perm(xb)
    yp, hs_r, hs_i = s5_scan_fwd(xp, a_re, a_im, bre, bim, cre, cim)
    saved = (xp, a_re, a_im, bre, bim, cre, cim, hs_r, hs_i, (lam_re, lam_im, ldt, br, bi))
    return _seg_unperm(yp), saved


def s5_ssm_bwd(dy, saved):
    xp, a_re, a_im, bre, bim, cre, cim, hs_r, hs_i, params = saved
    cre_t, cim_t = cre.transpose(0, 2, 1), cim.transpose(0, 2, 1)
    dxp, dar, dai, dbr, dbi, dcr, dci = s5_scan_bwd(_seg_perm(dy), xp, a_re, a_im, bre, bim, cre_t, cim_t, hs_r, hs_i)
    dlr, dli, dldt, db_re, db_im = s5_param_bwd(*params, dar.reshape(S5_GROUPS, S5_STATE), dai.reshape(S5_GROUPS, S5_STATE),
                                                _s5_b_untile(dbr), _s5_b_untile(dbi))
    shape_b = (S5_GROUPS, S5_STATE, S5_GROUP)
    return (_seg_unperm(dxp), dlr, dli, dldt.reshape(S5_GROUPS), db_re.reshape(shape_b), db_im.reshape(shape_b),
            _s5_c_untile(dcr), _s5_c_untile(dci))


_MESH = pl.DeviceIdType.MESH
_HBM = pl.BlockSpec(memory_space=pltpu.HBM)
N_DEV = 8


def _position():
    return lax.axis_index("x"), lax.axis_index("y"), lax.axis_index("c")


D2D_CHUNK_BYTES = 2 ** 20


def _chunk_rows(rows, cols, itemsize):
    return _row_tile(rows, 16, max(16, D2D_CHUNK_BYTES // (cols * itemsize)))


def _rows(start, size, unit=16):
    return pl.ds(pl.multiple_of(start, unit), size)


def _push_to_sibling(chunks, stages, recv_sems, store_sems, sibling, lag=2):
    in_slot, used, stores = {}, {}, []

    def push(q, slot):
        _, _, sid, land, _ = chunks[q]
        buf, send_sems, _ = stages[sid]
        return pltpu.make_async_remote_copy(src_ref=buf.at[slot], dst_ref=land, send_sem=send_sems.at[slot],
                                            recv_sem=recv_sems.at[q], device_id=sibling, device_id_type=_MESH)

    def receive(q):
        push(q, 0).wait_recv()
        st = pltpu.make_async_copy(chunks[q][3], chunks[q][4], store_sems.at[q])
        st.start()
        stores.append(st)

    for q, (pre, src, sid, _, _) in enumerate(chunks):
        if pre is not None:
            pre()
        slot = used.get(sid, 0) % 2
        used[sid] = used.get(sid, 0) + 1
        if (sid, slot) in in_slot:
            in_slot.pop((sid, slot)).wait_send()
        load = pltpu.make_async_copy(src, stages[sid][0].at[slot], stages[sid][2].at[slot])
        load.start()
        load.wait()
        cp = push(q, slot)
        cp.start()
        in_slot[(sid, slot)] = cp
        if q >= lag:
            receive(q - lag)
    for q in range(max(0, len(chunks) - lag), len(chunks)):
        receive(q)
    for cp in in_slot.values():
        cp.wait_send()
    for st in stores:
        st.wait()


def _stage_scratch(shapes_dtypes):
    out = []
    for shape, dtype in shapes_dtypes:
        out += [pltpu.VMEM((2,) + shape, dtype), pltpu.SemaphoreType.DMA((2,)), pltpu.SemaphoreType.DMA((2,))]
    return out


def allgather_weights(ws, convw, name):
    n = len(ws)
    extra = 0 if convw is None else 1
    halves = [w.shape[0] // 2 for w in ws]
    steps = [_chunk_rows(h, w.shape[1], w.dtype.itemsize) for h, w in zip(halves, ws)]
    per_peer = [h // s for h, s in zip(halves, steps)]
    nchunks = 3 * sum(per_peer)

    def body(*refs):
        w_refs = refs[:n]
        wo_refs = refs[n + extra:2 * n + extra]
        scratch = refs[2 * (n + extra):]
        send_sems, recv_sems, local_sems, fwd_recv_sems, store_sems = scratch[:5]
        lands = scratch[5:5 + n]
        stage_refs = scratch[5 + n:]
        stages = [tuple(stage_refs[3 * i:3 * i + 3]) for i in range(n)]
        x, y, c = _position()
        mine = 2 * x + y
        peers = [(1 - x, y), (x, 1 - y), (1 - x, 1 - y)]
        blocks = [2 * px + py for px, py in peers]
        local = [pltpu.make_async_copy(w_refs[i], wo_refs[i].at[mine], local_sems.at[i]) for i in range(n)]
        if extra:
            c_ref, co_ref = refs[n], refs[2 * n + 1]
            local.append(pltpu.make_async_copy(c_ref, co_ref.at[mine], local_sems.at[n]))
        for cp in local:
            cp.start()

        def ici(i, k, block):
            rows = _rows(c * halves[i], halves[i])
            return pltpu.make_async_remote_copy(src_ref=w_refs[i].at[rows, :], dst_ref=wo_refs[i].at[block, rows, :],
                                                send_sem=send_sems.at[3 * i + k], recv_sem=recv_sems.at[3 * i + k],
                                                device_id=(*peers[k], c), device_id_type=_MESH)

        def conv(k, block):
            return pltpu.make_async_remote_copy(src_ref=c_ref, dst_ref=co_ref.at[block], send_sem=send_sems.at[3 * n + k],
                                                recv_sem=recv_sems.at[3 * n + k], device_id=(*peers[k], c), device_id_type=_MESH)

        sends = [ici(i, k, mine) for k in range(3) for i in range(n)] + ([conv(k, mine) for k in range(3)] if extra else [])
        for cp in sends:
            cp.start()
        chunks = []
        for k in range(3):
            for i in range(n):
                for q in range(per_peer[i]):
                    pre = functools.partial(lambda i, k: ici(i, k, blocks[k]).wait_recv(), i, k) if q == 0 else None
                    src = wo_refs[i].at[blocks[k], _rows(c * halves[i] + q * steps[i], steps[i]), :]
                    out = wo_refs[i].at[blocks[k], _rows((1 - c) * halves[i] + q * steps[i], steps[i]), :]
                    chunks.append((pre, src, i, lands[i].at[k * per_peer[i] + q], out))
        _push_to_sibling(chunks, stages, fwd_recv_sems, store_sems, (x, y, 1 - c))
        if extra:
            for k in range(3):
                conv(k, blocks[k]).wait_recv()
        for cp in sends:
            cp.wait_send()
        for cp in local:
            cp.wait()

    nsem = 3 * (n + extra)
    scratch = [pltpu.SemaphoreType.DMA((nsem,)), pltpu.SemaphoreType.DMA((nsem,)), pltpu.SemaphoreType.DMA((n + extra,)),
               pltpu.SemaphoreType.DMA((nchunks,)), pltpu.SemaphoreType.DMA((nchunks,))]
    scratch += [pltpu.VMEM((3 * p, s, w.shape[1]), w.dtype) for p, s, w in zip(per_peer, steps, ws)]
    scratch += _stage_scratch([((s, w.shape[1]), w.dtype) for s, w in zip(steps, ws)])
    operands = list(ws) + ([convw] if extra else [])
    return pl.pallas_call(
        body, name=name, in_specs=[_HBM] * len(operands), out_specs=[_HBM] * len(operands),
        out_shape=[jax.ShapeDtypeStruct((4,) + w.shape, w.dtype) for w in operands],
        scratch_shapes=scratch, compiler_params=pltpu.CompilerParams(vmem_limit_bytes=VMEM_LIMIT),
    )(*operands)


def exchange_cores(gs, name):
    n = len(gs)
    halves = [g.shape[1] // 2 for g in gs]
    steps = [_chunk_rows(h, g.shape[2], g.dtype.itemsize) for h, g in zip(halves, gs)]
    per_shard = [h // s for h, s in zip(halves, steps)]
    nchunks = 4 * sum(per_shard)

    def body(*refs):
        g_refs, got_refs, scratch = refs[:n], refs[n:2 * n], refs[2 * n:]
        recv_sems, store_sems = scratch[:2]
        lands = scratch[2:2 + n]
        stage_refs = scratch[2 + n:2 + 4 * n]
        stages = [tuple(stage_refs[3 * i:3 * i + 3]) for i in range(n)]
        x, y, c = _position()
        chunks = []
        for i in range(n):
            for j in range(4):
                for q in range(per_shard[i]):
                    src = g_refs[i].at[j, _rows((1 - c) * halves[i] + q * steps[i], steps[i]), :]
                    out = got_refs[i].at[j, pl.ds(q * steps[i], steps[i]), :]
                    chunks.append((None, src, i, lands[i].at[j * per_shard[i] + q], out))
        _push_to_sibling(chunks, stages, recv_sems, store_sems, (x, y, 1 - c))

    scratch = [pltpu.SemaphoreType.DMA((nchunks,)), pltpu.SemaphoreType.DMA((nchunks,))]
    scratch += [pltpu.VMEM((4 * p, s, g.shape[2]), g.dtype) for p, s, g in zip(per_shard, steps, gs)]
    scratch += _stage_scratch([((s, g.shape[2]), g.dtype) for s, g in zip(steps, gs)])
    return pl.pallas_call(
        body, name=name, in_specs=[_HBM] * n, out_specs=[_HBM] * n,
        out_shape=[jax.ShapeDtypeStruct((4, h, g.shape[2]), g.dtype) for h, g in zip(halves, gs)],
        scratch_shapes=scratch, compiler_params=pltpu.CompilerParams(vmem_limit_bytes=VMEM_LIMIT),
    )(*gs)


def chips_side(cs):
    n = len(cs)

    def copies(c_refs, o_refs, sems):
        send_sems, recv_sems, local_sems = sems
        x, y, c = _position()
        mine = 2 * x + y
        peers = [(1 - x, y), (x, 1 - y), (1 - x, 1 - y)]
        blocks = [2 * px + py for px, py in peers]
        local = [pltpu.make_async_copy(c_refs[i].at[mine], o_refs[i].at[mine], local_sems.at[i]) for i in range(n)]

        def copy(i, k, sending):
            return pltpu.make_async_remote_copy(src_ref=c_refs[i].at[blocks[k]], dst_ref=o_refs[i].at[mine if sending else blocks[k]],
                                                send_sem=send_sems.at[3 * i + k], recv_sem=recv_sems.at[3 * i + k],
                                                device_id=(*peers[k], c), device_id_type=_MESH)

        sends = [copy(i, k, True) for k in range(3) for i in range(n)]
        return local, sends, lambda: [copy(i, k, False) for k in range(3) for i in range(n)]

    def start(*refs):
        local, sends, _ = copies(*refs)
        for cp in local + sends:
            cp.start()

    def finish(*refs):
        local, sends, arrivals = copies(*refs)
        for cp in arrivals():
            cp.wait_recv()
        for cp in sends:
            cp.wait_send()
        for cp in local:
            cp.wait()

    scratch = [pltpu.SemaphoreType.DMA((3 * n,)), pltpu.SemaphoreType.DMA((3 * n,)), pltpu.SemaphoreType.DMA((n,))]
    return Side(list(cs), [jax.ShapeDtypeStruct(a.shape, a.dtype) for a in cs], scratch, start, finish)


def gather_side(ws, convw):
    n = len(ws)
    halves = [w.shape[0] // 2 for w in ws]

    def copies(in_refs, out_refs, sems):
        w_refs, c_ref, wo_refs, co_ref = in_refs[:n], in_refs[n], out_refs[:n], out_refs[n]
        send_sems, recv_sems, local_sems = sems
        x, y, c = _position()
        mine = 2 * x + y
        peers = [(1 - x, y), (x, 1 - y), (1 - x, 1 - y)]
        blocks = [2 * px + py for px, py in peers]
        local = [pltpu.make_async_copy(w_refs[i], wo_refs[i].at[mine], local_sems.at[i]) for i in range(n)]
        local.append(pltpu.make_async_copy(c_ref, co_ref.at[mine], local_sems.at[n]))

        def ici(i, k, block):
            rows = _rows(c * halves[i], halves[i])
            return pltpu.make_async_remote_copy(src_ref=w_refs[i].at[rows, :], dst_ref=wo_refs[i].at[block, rows, :],
                                                send_sem=send_sems.at[3 * i + k], recv_sem=recv_sems.at[3 * i + k],
                                                device_id=(*peers[k], c), device_id_type=_MESH)

        def conv(k, block):
            return pltpu.make_async_remote_copy(src_ref=c_ref, dst_ref=co_ref.at[block], send_sem=send_sems.at[3 * n + k],
                                                recv_sem=recv_sems.at[3 * n + k], device_id=(*peers[k], c), device_id_type=_MESH)

        sends = [ici(i, k, mine) for k in range(3) for i in range(n)] + [conv(k, mine) for k in range(3)]
        return local, sends, lambda: ([ici(i, k, blocks[k]) for k in range(3) for i in range(n)]
                                      + [conv(k, blocks[k]) for k in range(3)])

    def start(*refs):
        local, sends, _ = copies(*refs)
        for cp in local + sends:
            cp.start()

    def finish(*refs):
        local, sends, arrivals = copies(*refs)
        for cp in arrivals():
            cp.wait_recv()
        for cp in sends:
            cp.wait_send()
        for cp in local:
            cp.wait()

    nsem = 3 * n + 3
    scratch = [pltpu.SemaphoreType.DMA((nsem,)), pltpu.SemaphoreType.DMA((nsem,)), pltpu.SemaphoreType.DMA((n + 1,))]
    operands = list(ws) + [convw]
    return Side(operands, [jax.ShapeDtypeStruct((4,) + w.shape, w.dtype) for w in operands], scratch, start, finish)


def forward_halves(stacked):
    n = len(stacked)
    halves = [w.shape[1] // 2 for w in stacked]
    steps = [_chunk_rows(h, w.shape[2], w.dtype.itemsize) for h, w in zip(halves, stacked)]
    per_peer = [h // s for h, s in zip(halves, steps)]
    nchunks = 3 * sum(per_peer)

    def body(*refs):
        w_refs, o_refs = refs[:n], refs[n:2 * n]
        scratch = refs[2 * n:]
        recv_sems, store_sems = scratch[:2]
        lands = scratch[2:2 + n]
        stages = [tuple(scratch[2 + n + 3 * i:2 + n + 3 * i + 3]) for i in range(n)]
        x, y, c = _position()
        blocks = [2 * px + py for px, py in ((1 - x, y), (x, 1 - y), (1 - x, 1 - y))]
        chunks = []
        for k in range(3):
            for i in range(n):
                for q in range(per_peer[i]):
                    src = w_refs[i].at[blocks[k], _rows(c * halves[i] + q * steps[i], steps[i]), :]
                    out = o_refs[i].at[blocks[k], _rows((1 - c) * halves[i] + q * steps[i], steps[i]), :]
                    chunks.append((None, src, i, lands[i].at[k * per_peer[i] + q], out))
        _push_to_sibling(chunks, stages, recv_sems, store_sems, (x, y, 1 - c))

    scratch = [pltpu.SemaphoreType.DMA((nchunks,)), pltpu.SemaphoreType.DMA((nchunks,))]
    scratch += [pltpu.VMEM((3 * p, s, w.shape[2]), w.dtype) for p, s, w in zip(per_peer, steps, stacked)]
    scratch += _stage_scratch([((s, w.shape[2]), w.dtype) for s, w in zip(steps, stacked)])
    return pl.pallas_call(
        body, name="forward_halves", in_specs=[_HBM] * n, out_specs=[_HBM] * n,
        out_shape=[jax.ShapeDtypeStruct(w.shape, w.dtype) for w in stacked], input_output_aliases={i: i for i in range(n)},
        scratch_shapes=scratch, compiler_params=pltpu.CompilerParams(vmem_limit_bytes=VMEM_LIMIT),
    )(*stacked)


def exchange_small(small):
    def body(s_ref, so_ref, send_sems, recv_sems, local_sem):
        x, y, c = _position()
        me = 4 * x + 2 * y + c
        local = pltpu.make_async_copy(s_ref, so_ref.at[me], local_sem)
        local.start()

        def copy(r, sending):
            px, py, pc = (1 - x if r & 4 else x, 1 - y if r & 2 else y, 1 - c if r & 1 else c)
            slot = me if sending else 4 * px + 2 * py + pc
            return pltpu.make_async_remote_copy(src_ref=s_ref, dst_ref=so_ref.at[slot], send_sem=send_sems.at[r - 1],
                                                recv_sem=recv_sems.at[r - 1], device_id=(px, py, pc), device_id_type=_MESH)

        sends = [copy(r, True) for r in range(1, N_DEV)]
        for cp in sends:
            cp.start()
        for r in range(1, N_DEV):
            copy(r, False).wait_recv()
        for cp in sends:
            cp.wait_send()
        local.wait()

    return pl.pallas_call(
        body, name="exchange_small", in_specs=[_HBM], out_specs=_HBM,
        out_shape=jax.ShapeDtypeStruct((N_DEV,) + small.shape, small.dtype),
        scratch_shapes=[pltpu.SemaphoreType.DMA((N_DEV - 1,)), pltpu.SemaphoreType.DMA((N_DEV - 1,)), pltpu.SemaphoreType.DMA],
    )(small)


def pair_sum(core, g, got, name):
    nb, rows, cols = got.shape
    tr = _row_tile(rows, 16, max(16, (2 * 2 ** 20) // (cols * g.dtype.itemsize)))
    nblk = rows // tr

    def body(c_ref, a_ref, b_ref, o_ref):
        o_ref[...] = (a_ref[...].astype(F32) + b_ref[...].astype(F32)).astype(o_ref.dtype)

    spec = pl.BlockSpec((1, tr, cols), lambda j, i, c_ref: (j, i, 0))
    mine = pl.BlockSpec((1, tr, cols), lambda j, i, c_ref: (j, c_ref[0] * nblk + i, 0))
    return pl.pallas_call(
        body, name=name,
        grid_spec=pltpu.PrefetchScalarGridSpec(num_scalar_prefetch=1, grid=(nb, nblk), in_specs=[mine, spec], out_specs=spec),
        out_shape=jax.ShapeDtypeStruct(got.shape, g.dtype), compiler_params=_cparams(("parallel", "parallel")),
    )(core, g, got)


def sum_chips(core, pieces, name):
    nb, rows, cols = pieces.shape
    tr = _row_tile(rows, 16, max(16, (6 * 2 ** 20) // (nb * cols * pieces.dtype.itemsize)))
    nblk = rows // tr

    def body(c_ref, p_ref, o_ref):
        acc = p_ref[0].astype(F32)
        for i in range(1, nb):
            acc = acc + p_ref[i].astype(F32)
        o_ref[0] = acc

    return pl.pallas_call(
        body, name=name,
        grid_spec=pltpu.PrefetchScalarGridSpec(
            num_scalar_prefetch=1, grid=(nblk,),
            in_specs=[pl.BlockSpec((nb, tr, cols), lambda i, c_ref: (0, i, 0))],
            out_specs=pl.BlockSpec((1, tr, cols), lambda i, c_ref: (0, c_ref[0] * nblk + i, 0))),
        out_shape=jax.ShapeDtypeStruct((1, 2 * rows, cols), F32), compiler_params=_cparams(("parallel",)),
    )(core, pieces)


def sibling_exchange(fulls):
    n = len(fulls)
    halves = [f.shape[1] // 2 for f in fulls]
    steps = [_chunk_rows(h, f.shape[2], f.dtype.itemsize) for h, f in zip(halves, fulls)]
    counts = [h // s for h, s in zip(halves, steps)]
    nchunks = sum(counts)

    def body(*refs):
        f_refs, o_refs = refs[:n], refs[n:2 * n]
        scratch = refs[2 * n:]
        recv_sems, store_sems = scratch[:2]
        lands = scratch[2:2 + n]
        stages = [tuple(scratch[2 + n + 3 * i:2 + n + 3 * i + 3]) for i in range(n)]
        x, y, c = _position()
        chunks = []
        for i in range(n):
            for q in range(counts[i]):
                src = f_refs[i].at[0, _rows(c * halves[i] + q * steps[i], steps[i]), :]
                out = o_refs[i].at[0, _rows((1 - c) * halves[i] + q * steps[i], steps[i]), :]
                chunks.append((None, src, i, lands[i].at[q], out))
        _push_to_sibling(chunks, stages, recv_sems, store_sems, (x, y, 1 - c))

    scratch = [pltpu.SemaphoreType.DMA((nchunks,)), pltpu.SemaphoreType.DMA((nchunks,))]
    scratch += [pltpu.VMEM((k, s, f.shape[2]), f.dtype) for k, s, f in zip(counts, steps, fulls)]
    scratch += _stage_scratch([((s, f.shape[2]), f.dtype) for s, f in zip(steps, fulls)])
    return pl.pallas_call(
        body, name="sibling_exchange", in_specs=[_HBM] * n, out_specs=[_HBM] * n,
        out_shape=[jax.ShapeDtypeStruct(f.shape, f.dtype) for f in fulls],
        input_output_aliases={i: i for i in range(n)},
        scratch_shapes=scratch, compiler_params=pltpu.CompilerParams(vmem_limit_bytes=VMEM_LIMIT),
    )(*fulls)


def _row_tile(rows, unit, max_rows):
    best = unit
    for t in range(unit, min(rows, max_rows) + 1, unit):
        if rows % t == 0:
            best = t
    return best


def sum_pieces(pieces, name):
    n, rows, cols = pieces.shape
    tr = _row_tile(rows, 16, max(16, (6 * 2 ** 20) // (n * cols * pieces.dtype.itemsize)))

    def body(p_ref, o_ref):
        acc = p_ref[0].astype(F32)
        for i in range(1, n):
            acc = acc + p_ref[i].astype(F32)
        o_ref[...] = acc

    return pl.pallas_call(
        body, name=name, grid=(rows // tr,),
        in_specs=[pl.BlockSpec((n, tr, cols), lambda i: (0, i, 0))], out_specs=pl.BlockSpec((tr, cols), lambda i: (i, 0)),
        out_shape=jax.ShapeDtypeStruct((rows, cols), F32), compiler_params=_cparams(("parallel",)),
    )(pieces)


BIG = ("w_in", "s5_w_glu", "w_kv_mem", "w_br_a", "w_br_b", "w_br_c", "w_out")
COL_SHARDED = ("w_in", "s5_w_glu", "w_br_a", "w_br_b", "w_br_c")
SMALL = ("norm_g", "gdn_a_log", "gdn_dt_bias", "gdn_norm_g", "s5_lambda_re", "s5_lambda_im", "s5_log_dt",
         "s5_b_re", "s5_b_im", "s5_c_re", "s5_c_im", "s5_d", "mem_norm_g", "final_g")
WEIGHTS = ("norm_g", "w_in", "conv_w", "gdn_a_log", "gdn_dt_bias", "gdn_norm_g", "s5_lambda_re", "s5_lambda_im",
           "s5_log_dt", "s5_b_re", "s5_b_im", "s5_c_re", "s5_c_im", "s5_d", "s5_w_glu", "mem_norm_g", "w_kv_mem",
           "w_br_a", "w_br_b", "w_br_c", "w_out", "final_g")
W_IN_SPLIT = 4096


W_IN_COLS = PROJ_W - BA_PAD + 2 * NHEAD
W_IN_GATES = W_IN_COLS - GATE_W
W_IN_MOVES = ((0, W_IN_SPLIT, GATE_W), (W_IN_SPLIT, W_IN_SPLIT + 2 * NHEAD, PROJ_W - BA_PAD - W_IN_SPLIT),
              (W_IN_SPLIT + 2 * NHEAD, W_IN_GATES, GATE_W - 2 * NHEAD), (W_IN_GATES, W_IN_COLS, -W_IN_GATES))


def _pack_w_in(shards):
    cs = shards.shape[2]
    parts = []
    for a, b, _ in sorted(W_IN_MOVES, key=lambda move: move[0] + move[2]):
        while a < b:
            j = a // cs
            hi = min(b, (j + 1) * cs)
            parts.append(shards[j, :, a - j * cs:hi - j * cs])
            a = hi
    parts.append(jnp.zeros((shards.shape[1], BA_PAD - 2 * NHEAD), shards.dtype))
    return jnp.concatenate(parts, axis=1)


def _unpack_w_in(wp):
    cs = W_IN_COLS // 4
    shards = []
    for j in range(4):
        parts = []
        for lo, hi, shift in W_IN_MOVES:
            s, e = max(j * cs, lo), min((j + 1) * cs, hi)
            if s < e:
                parts.append(wp[:, s + shift:e + shift])
        shards.append(jnp.concatenate(parts, axis=1))
    return jnp.stack(shards)


def _pack_small(arrs):
    parts = []
    for a in arrs:
        f = a.reshape(-1).astype(F32)
        parts.append(jnp.pad(f, (0, (-f.shape[0]) % 128)))
    flat = jnp.concatenate(parts)
    rows = flat.shape[0] // 128
    return jnp.pad(flat.reshape(rows, 128), ((0, (-rows) % 16), (0, 0)))


def _unpack_small(flat2d, shapes):
    f = flat2d.reshape(-1)
    out, off = [], 0
    for shp in shapes:
        n = math.prod(shp)
        out.append(f[off:off + n].reshape(shp))
        off += n + (-n) % 128
    return out


def kernel(x, mem, norm_g, w_in, conv_w, gdn_a_log, gdn_dt_bias, gdn_norm_g, s5_lambda_re, s5_lambda_im, s5_log_dt, s5_b_re, s5_b_im, s5_c_re, s5_c_im, s5_d, s5_w_glu, mem_norm_g, w_kv_mem, w_br_a, w_br_b, w_br_c, w_out, final_g, loss_target, m_norm_g, m_w_in, m_conv_w, m_gdn_a_log, m_gdn_dt_bias, m_gdn_norm_g, m_s5_lambda_re, m_s5_lambda_im, m_s5_log_dt, m_s5_b_re, m_s5_b_im, m_s5_c_re, m_s5_c_im, m_s5_d, m_s5_w_glu, m_mem_norm_g, m_w_kv_mem, m_w_br_a, m_w_br_b, m_w_br_c, m_w_out, m_final_g, v_norm_g, v_w_in, v_conv_w, v_gdn_a_log, v_gdn_dt_bias, v_gdn_norm_g, v_s5_lambda_re, v_s5_lambda_im, v_s5_log_dt, v_s5_b_re, v_s5_b_im, v_s5_c_re, v_s5_c_im, v_s5_d, v_s5_w_glu, v_mem_norm_g, v_w_kv_mem, v_w_br_a, v_w_br_b, v_w_br_c, v_w_out, v_final_g):
    wts = dict(norm_g=norm_g, w_in=w_in, conv_w=conv_w, gdn_a_log=gdn_a_log, gdn_dt_bias=gdn_dt_bias, gdn_norm_g=gdn_norm_g,
               s5_lambda_re=s5_lambda_re, s5_lambda_im=s5_lambda_im, s5_log_dt=s5_log_dt, s5_b_re=s5_b_re, s5_b_im=s5_b_im,
               s5_c_re=s5_c_re, s5_c_im=s5_c_im, s5_d=s5_d, s5_w_glu=s5_w_glu, mem_norm_g=mem_norm_g, w_kv_mem=w_kv_mem,
               w_br_a=w_br_a, w_br_b=w_br_b, w_br_c=w_br_c, w_out=w_out, final_g=final_g)
    mom = dict(norm_g=m_norm_g, w_in=m_w_in, conv_w=m_conv_w, gdn_a_log=m_gdn_a_log, gdn_dt_bias=m_gdn_dt_bias,
               gdn_norm_g=m_gdn_norm_g, s5_lambda_re=m_s5_lambda_re, s5_lambda_im=m_s5_lambda_im, s5_log_dt=m_s5_log_dt,
               s5_b_re=m_s5_b_re, s5_b_im=m_s5_b_im, s5_c_re=m_s5_c_re, s5_c_im=m_s5_c_im, s5_d=m_s5_d, s5_w_glu=m_s5_w_glu,
               mem_norm_g=m_mem_norm_g, w_kv_mem=m_w_kv_mem, w_br_a=m_w_br_a, w_br_b=m_w_br_b, w_br_c=m_w_br_c, w_out=m_w_out,
               final_g=m_final_g)
    vel = dict(norm_g=v_norm_g, w_in=v_w_in, conv_w=v_conv_w, gdn_a_log=v_gdn_a_log, gdn_dt_bias=v_gdn_dt_bias,
               gdn_norm_g=v_gdn_norm_g, s5_lambda_re=v_s5_lambda_re, s5_lambda_im=v_s5_lambda_im, s5_log_dt=v_s5_log_dt,
               s5_b_re=v_s5_b_re, s5_b_im=v_s5_b_im, s5_c_re=v_s5_c_re, s5_c_im=v_s5_c_im, s5_d=v_s5_d, s5_w_glu=v_s5_w_glu,
               mem_norm_g=v_mem_norm_g, w_kv_mem=v_w_kv_mem, w_br_a=v_w_br_a, w_br_b=v_w_br_b, w_br_c=v_w_br_c, w_out=v_w_out,
               final_g=v_final_g)
    x2, mem2, tgt = x[0], mem[0], loss_target[0]
    s, d = x2.shape
    n_chunks = s // CHUNK

    shards = [wts[n][0].astype(BF16) for n in BIG]
    wp = _pack_w_in(allgather_weights(shards[:1], None, "allgather_w_in")[0])
    mm = functools.partial(matmul, tm=1024, tn=1024)
    u, r1 = rms_fwd(x2, norm_g, "rms_fwd_x")
    proj, *rest, cg = mm(u, wp, mode="nn", out_dtype=F32, tk=2048, name="mm_proj", side=gather_side(shards[1:], conv_w[0]))
    full = {}
    for n, wg in zip(BIG[1:], forward_halves(rest)):
        rows, cols = wg.shape[1:]
        full[n] = wg.transpose(1, 0, 2).reshape(rows, 4 * cols) if n in COL_SHARDED else wg.reshape(4 * rows, cols)
    conv_full = cg.transpose(1, 0, 2).reshape(conv_w.shape[1], -1)
    alog_pad = jnp.pad(gdn_a_log, ((0, 0), (NHEAD, BA_W - 2 * NHEAD)))
    dt_pad = jnp.pad(gdn_dt_bias, ((0, 0), (NHEAD, BA_W - 2 * NHEAD)))

    q, k, v, bg, gcol, gt = gdn_prep_fwd(proj, conv_full, alog_pad, dt_pad)
    gt3 = gt.reshape(BA_W, n_chunks, CHUNK).transpose(1, 0, 2)
    gu, gw, qd, kd, qk, tinv = gdn_intra_fwd(q, k, v, bg, gcol, gt3)
    o_raw, states = gdn_seq_fwd(gu, gw, qd, kd, qk, gt3)
    ga = gdn_out_fwd(o_raw, proj, ZA_CB, gdn_norm_g)

    xb = proj[:, XB_CB * S5_IN:(XB_CB + 1) * S5_IN]
    y_ssm, s5_saved = s5_ssm_fwd(xb, s5_lambda_re[0], s5_lambda_im[0], s5_log_dt[0], s5_b_re[0], s5_b_im[0],
                                 s5_c_re[0], s5_c_im[0])
    yb = s5_act_fwd(y_ssm, proj, XB_CB, s5_d)
    glu = mm(yb, full["s5_w_glu"], mode="nn", out_dtype=BF16, tk=1024, name="mm_glu")
    gb = s5_glu_fwd(glu, proj, ZB_CB)

    mem_n, rm = rms_fwd(mem2, mem_norm_g, "rms_fwd_mem")
    kv = mm(mem_n, full["w_kv_mem"], mode="nn", out_dtype=BF16, tk=2048, name="mm_kv")
    o_c = xa_fwd(proj, kv)
    gcx = gate_fwd(o_c, proj, ZC_CB, "gate_fwd_c")

    pa = mm(ga, full["w_br_a"], mode="nn", out_dtype=BF16, tk=1024, name="mm_pa")
    pb = mm(gb, full["w_br_b"], mode="nn", out_dtype=BF16, tk=1024, name="mm_pb")
    pc = mm(gcx, full["w_br_c"], mode="nn", out_dtype=BF16, tk=1024, name="mm_pc")
    merged = merge_fwd(pa, pb, pc, proj, GATE_CB)
    hres = mm(merged, full["w_out"], mode="nn", out_dtype=F32, tk=2048, name="mm_out")
    dh, dhb, loss_part, d_final_g = final_stage(x2, hres, tgt, final_g.reshape(1, d))

    gfull = {}
    dmerged = mm(dhb, full["w_out"], mode="nt", out_dtype=BF16, tk=2048, name="mm_dmerged")
    gfull["w_out"] = mm(merged, dhb, mode="tn", out_dtype=BF16, tk=1024, name="mm_dw_out")
    dproj = lax.empty((s, PROJ_W), BF16)
    dpa, dpb, dpc, dproj = merge_bwd(dmerged, pa, pb, pc, proj, GATE_CB, dproj)
    dga = mm(dpa, full["w_br_a"], mode="nt", out_dtype=BF16, tk=2048, name="mm_dga")
    dgb = mm(dpb, full["w_br_b"], mode="nt", out_dtype=BF16, tk=2048, name="mm_dgb")
    dgc = mm(dpc, full["w_br_c"], mode="nt", out_dtype=BF16, tk=2048, name="mm_dgc")
    gfull["w_br_a"] = mm(ga, dpa, mode="tn", out_dtype=BF16, tk=1024, name="mm_dw_a")
    gfull["w_br_b"] = mm(gb, dpb, mode="tn", out_dtype=BF16, tk=1024, name="mm_dw_b")
    gfull["w_br_c"] = mm(gcx, dpc, mode="tn", out_dtype=BF16, tk=1024, name="mm_dw_c")

    do_raw, dproj, d_gdn_norm = gdn_out_bwd(dga, o_raw, proj, ZA_CB, gdn_norm_g, dproj)
    du_, dw_, dqd, dkd, dqk, dgl = gdn_seq_bwd(do_raw, gu, gw, qd, kd, qk, gt3, states)
    dq, dk, dv, dbg = gdn_intra_bwd(q, k, v, bg, gcol, gt3, tinv, du_, dw_, dqd, dkd, dqk, dgl)
    dc, dproj, dcw0, dcw1, dcw2, dcw3, d_alog, d_dt = gdn_prep_bwd1(proj, conv_full, alog_pad, dt_pad, dq, dk, dv, dbg, dproj)
    dproj = gdn_prep_bwd2(dc, conv_full, dproj)
    d_conv = jnp.concatenate([dcw0, dcw1, dcw2, dcw3], axis=0)

    dval, dgate, dproj = s5_glu_bwd(dgb, glu, proj, ZB_CB, dproj)
    dglu = jnp.concatenate([dval, dgate], axis=1)
    dyb = mm(dglu, full["s5_w_glu"], mode="nt", out_dtype=BF16, tk=2048, name="mm_dyb")
    gfull["s5_w_glu"] = mm(yb, dglu, mode="tn", out_dtype=BF16, tk=1024, name="mm_dw_glu")
    dy_ssm, dxb_direct, d_s5_d = s5_act_bwd(dyb, y_ssm, proj, XB_CB, s5_d)
    dxb_scan, d_lre, d_lim, d_ldt, d_bre, d_bim, d_cre, d_cim = s5_ssm_bwd(dy_ssm, s5_saved)
    dproj = add_into(dxb_direct, dxb_scan, "s5_dxb", dproj, XB_CB)

    do_c, dproj = gate_bwd(dgc, o_c, proj, ZC_CB, "gate_bwd_c", dproj)
    dproj, dkv = xa_bwd(do_c, proj, kv, dproj)
    gfull["w_kv_mem"] = mm(mem_n, dkv, mode="tn", out_dtype=BF16, tk=256, name="mm_dw_kv")
    dmem_n = mm(dkv, full["w_kv_mem"], mode="nt", out_dtype=F32, tk=2048, name="mm_dmem")
    d_mem_norm = rms_bwd_g(dmem_n, mem2, rm, "rms_bwd_mem")

    core = lax.axis_index("c").astype(jnp.int32).reshape(1)
    by_shard = []
    for n in BIG[1:]:
        rows, cols = wts[n].shape[1:]
        g = gfull[n]
        by_shard.append(g.reshape(rows, 4, cols).transpose(1, 0, 2) if n in COL_SHARDED else g.reshape(4, rows, cols))
    got_rest = exchange_cores(by_shard, "exchange_cores_rest")
    chip_rest = [pair_sum(core, g, r, "sum_cores_" + n) for n, g, r in zip(BIG[1:], by_shard, got_rest)]

    dwp, *from_chips_rest = matmul(u.T, dproj, mode="nn", out_dtype=BF16, tm=2048, tn=1024, tk=1024, name="mm_dw_in",
                                   side=chips_side(chip_rest))
    w_in_shards = _unpack_w_in(dwp)
    got_in, = exchange_cores([w_in_shards], "exchange_cores_w_in")
    chip_in = pair_sum(core, w_in_shards, got_in, "sum_cores_w_in")
    du, from_chips_in = matmul(dproj, wp, mode="nt", out_dtype=F32, tm=2048, tn=1024, tk=512, name="mm_du",
                               side=chips_side([chip_in]))
    grad_x, d_norm_g = rms_bwd_x(du, x2, r1, norm_g, dh)
    from_chips = [from_chips_in] + from_chips_rest
    small_g = dict(norm_g=d_norm_g, gdn_a_log=d_alog[:, NHEAD:2 * NHEAD], gdn_dt_bias=d_dt[:, NHEAD:2 * NHEAD],
                   gdn_norm_g=d_gdn_norm, s5_lambda_re=d_lre, s5_lambda_im=d_lim, s5_log_dt=d_ldt, s5_b_re=d_bre, s5_b_im=d_bim,
                   s5_c_re=d_cre, s5_c_im=d_cim, s5_d=d_s5_d, mem_norm_g=d_mem_norm, final_g=d_final_g)
    small_send = _pack_small([small_g[n] for n in SMALL] + [d_conv, loss_part])
    fulls = [sum_chips(core, a, "sum_chips_" + n) for n, a in zip(BIG, from_chips)]
    small_sum = sum_pieces(exchange_small(small_send), "sum_small")
    grads = dict(zip(BIG, sibling_exchange(fulls)))
    small_shapes = [wts[n].shape for n in SMALL] + [d_conv.shape, (1, 1)]
    *small_list, conv_g_full, loss_sum = _unpack_small(small_sum, small_shapes)
    grads.update(zip(SMALL, small_list))
    cw = conv_w.shape[2]
    shard_idx = 2 * lax.axis_index("x") + lax.axis_index("y")
    grads["conv_w"] = lax.dynamic_slice(conv_g_full, (0, shard_idx * cw), (conv_w.shape[1], cw))[None]

    delta, new_m, new_v = {}, {}, {}
    for n in BIG + ("conv_w",):
        delta[n], new_m[n], new_v[n] = adamw(wts[n], grads[n], mom[n], vel[n], "adamw_" + n)
    packed_w, packed_m, packed_v = (_pack_small([src[n] for n in SMALL]) for src in (wts, mom, vel))
    res = adamw(packed_w, small_sum[:packed_w.shape[0]], packed_m, packed_v, "adamw_small")
    shapes = [wts[n].shape for n in SMALL]
    for dst, flat in zip((delta, new_m, new_v), res):
        dst.update(zip(SMALL, _unpack_small(flat, shapes)))
    for n in SMALL:
        grads[n] = grads[n].reshape(wts[n].shape)

    return (loss_sum.reshape(()), grad_x.reshape(x.shape), *[grads[n] for n in WEIGHTS], *[delta[n] for n in WEIGHTS],
            *[new_m[n] for n in WEIGHTS], *[new_v[n] for n in WEIGHTS])
```

```python
import functools
import math

import jax
import jax.numpy as jnp
from jax import lax
from jax.experimental import pallas as pl
from jax.experimental.pallas import tpu as pltpu

F32 = jnp.float32
BF16 = jnp.bfloat16
HI = lax.Precision.HIGHEST

EPS = 1e-6
CHUNK = 64
HEAD = 128
NHEAD = 8
XA_HEADS = 4
S5_GROUPS = 64
S5_STATE = 64
S5_GROUP = 16
NSEG = 8
ADAM_LR, ADAM_B1, ADAM_B2, ADAM_EPS, ADAM_WD, ADAM_STEP = 0.001, 0.9, 0.999, 1e-08, 0.01, 10
VMEM_LIMIT = 56 * 2 ** 20


def _cparams(sem=None):
    return pltpu.CompilerParams(dimension_semantics=sem, vmem_limit_bytes=VMEM_LIMIT)


def _sigmoid(x):
    return 1.0 / (1.0 + jnp.exp(-x))


def _silu(x):
    return x * _sigmoid(x)


def _dsilu(x):
    s = _sigmoid(x)
    return s * (1.0 + x * (1.0 - s))


def _softplus(x):
    return jnp.maximum(x, 0.0) + jnp.log(1.0 + jnp.exp(-jnp.abs(x)))


_GELU_C = math.sqrt(2.0 / math.pi)


def _gelu(x):
    return 0.5 * x * (1.0 + jnp.tanh(_GELU_C * (x + 0.044715 * x * x * x)))


def _dgelu(x):
    t = jnp.tanh(_GELU_C * (x + 0.044715 * x * x * x))
    return 0.5 * (1.0 + t) + 0.5 * x * (1.0 - t * t) * _GELU_C * (1.0 + 3.0 * 0.044715 * x * x)


_DIMS = {"nn": (((1,), (0,)), ((), ())), "nt": (((1,), (1,)), ((), ())), "tn": (((0,), (0,)), ((), ()))}


class Side:
    def __init__(self, operands, out_shapes, scratch, start, finish):
        self.operands, self.out_shapes, self.scratch, self.start, self.finish = operands, out_shapes, scratch, start, finish


def matmul(a, b, *, mode, out_dtype, tm, tn, tk, name, side=None):
    if mode == "nn":
        (m, k), n = a.shape, b.shape[1]
    elif mode == "nt":
        (m, k), n = a.shape, b.shape[0]
    else:
        (k, m), n = a.shape, b.shape[1]
    tm, tn, tk = min(tm, m), min(tn, n), min(tk, k)
    assert m % tm == 0 and n % tn == 0 and k % tk == 0, (name, m, n, k, tm, tn, tk)
    grid = (m // tm, n // tn, k // tk)
    nk = grid[2]
    dims = _DIMS[mode]
    n_in = 0 if side is None else len(side.operands)
    n_out = 0 if side is None else len(side.out_shapes)
    n_acc = 0 if nk == 1 else 1

    def body(*refs):
        a_ref, b_ref, o_ref = refs[0], refs[1], refs[2 + n_in]
        scratch = refs[3 + n_in + n_out:]
        side_refs = (refs[2:2 + n_in], refs[3 + n_in:3 + n_in + n_out], scratch[n_acc:])
        ids = [pl.program_id(d) for d in range(3)]
        if side is not None:
            @pl.when((ids[0] == 0) & (ids[1] == 0) & (ids[2] == 0))
            def _():
                side.start(*side_refs)

        prod = lax.dot_general(a_ref[...].astype(BF16), b_ref[...].astype(BF16), dims, preferred_element_type=F32)
        if nk == 1:
            o_ref[...] = prod.astype(out_dtype)
        else:
            acc_ref = scratch[0]

            @pl.when(ids[2] == 0)
            def _():
                acc_ref[...] = prod

            @pl.when(ids[2] > 0)
            def _():
                acc_ref[...] += prod

            @pl.when(ids[2] == nk - 1)
            def _():
                o_ref[...] = acc_ref[...].astype(out_dtype)

        if side is not None:
            @pl.when((ids[0] == grid[0] - 1) & (ids[1] == grid[1] - 1) & (ids[2] == nk - 1))
            def _():
                side.finish(*side_refs)

    a_spec = pl.BlockSpec((tk, tm), lambda i, j, q: (q, i)) if mode == "tn" else pl.BlockSpec((tm, tk), lambda i, j, q: (i, q))
    b_spec = pl.BlockSpec((tn, tk), lambda i, j, q: (j, q)) if mode == "nt" else pl.BlockSpec((tk, tn), lambda i, j, q: (q, j))
    o_spec = pl.BlockSpec((tm, tn), lambda i, j, q: (i, j))
    o_shape = jax.ShapeDtypeStruct((m, n), out_dtype)
    acc = [] if nk == 1 else [pltpu.VMEM((tm, tn), F32)]
    if side is None:
        return pl.pallas_call(
            body, name=name, grid=grid, in_specs=[a_spec, b_spec], out_specs=o_spec, out_shape=o_shape, scratch_shapes=acc,
            compiler_params=_cparams(("parallel", "parallel", "arbitrary")),
        )(a, b)
    hbm = pl.BlockSpec(memory_space=pltpu.HBM)
    return pl.pallas_call(
        body, name=name, grid=grid, in_specs=[a_spec, b_spec] + [hbm] * n_in, out_specs=[o_spec] + [hbm] * n_out,
        out_shape=[o_shape] + list(side.out_shapes), scratch_shapes=acc + list(side.scratch),
        compiler_params=_cparams(("arbitrary", "arbitrary", "arbitrary")),
    )(a, b, *side.operands)


def rowwise(fn, ins, outs, *, rows, tr, name, consts=(), reds=(), into=None):
    tr = min(tr, rows)
    assert rows % tr == 0, (name, rows, tr)
    n_in, n_c, n_o = len(ins), len(consts), len(outs)
    n_buf = 0 if into is None else 1

    def body(*refs):
        vals = [r[...].astype(F32) for r in refs[:n_in + n_c]]
        res = fn(*vals)
        o_refs = refs[n_in + n_c + n_buf:]
        for r, v in zip(o_refs[:n_o], res[:n_o]):
            r[...] = v.astype(r.dtype)
        if reds:
            i = pl.program_id(0)

            @pl.when(i == 0)
            def _():
                for r, v in zip(o_refs[n_o:], res[n_o:]):
                    r[...] = v.astype(r.dtype)

            @pl.when(i > 0)
            def _():
                for r, v in zip(o_refs[n_o:], res[n_o:]):
                    r[...] += v.astype(r.dtype)

    in_specs = [pl.BlockSpec((tr, w), functools.partial(lambda i, cb: (i, cb), cb=cb)) for (_, w, cb) in ins]
    in_specs += [pl.BlockSpec(c.shape, lambda i: (0, 0)) for c in consts]
    out_specs = [pl.BlockSpec((tr, w), lambda i: (i, 0)) for (w, _) in outs]
    out_specs += [pl.BlockSpec(s, lambda i: (0, 0)) for (s, _) in reds]
    out_shape = [jax.ShapeDtypeStruct((rows, w), d) for (w, d) in outs]
    out_shape += [jax.ShapeDtypeStruct(s, d) for (s, d) in reds]
    operands = [a for (a, _, _) in ins] + list(consts)
    aliases = {}
    if into is not None:
        buf, pos, cb = into
        assert buf.dtype == outs[pos][1] and buf.shape[0] == rows, (name, buf.shape, buf.dtype)
        in_specs.append(pl.BlockSpec(memory_space=pl.ANY))
        operands.append(buf)
        out_specs[pos] = pl.BlockSpec((tr, outs[pos][0]), functools.partial(lambda i, cb: (i, cb), cb=cb))
        out_shape[pos] = jax.ShapeDtypeStruct(buf.shape, buf.dtype)
        aliases = {len(operands) - 1: pos}
    return pl.pallas_call(
        body, name=name, grid=(rows // tr,), in_specs=in_specs, out_specs=out_specs, out_shape=out_shape,
        input_output_aliases=aliases, compiler_params=_cparams(("arbitrary",) if reds else ("parallel",)),
    )(*operands)


def _colsum(x):
    return jnp.sum(x, axis=0, keepdims=True)


def rms_fwd(x, g, name):
    s, d = x.shape

    def fn(xv, gv):
        r = lax.rsqrt(jnp.mean(xv * xv, axis=-1, keepdims=True) + EPS)
        return xv * r * gv, r

    return rowwise(fn, [(x, d, 0)], [(d, BF16), (1, F32)], rows=s, tr=256, name=name, consts=[g])


def rms_bwd_x(du, x, r, g, dh):
    s, d = x.shape

    def fn(duv, xv, rv, dhv, gv):
        dyg = duv * gv
        dx = rv * dyg - xv * (rv * rv * rv) * jnp.mean(dyg * xv, axis=-1, keepdims=True)
        return dhv + dx, _colsum(duv * xv * rv)

    return rowwise(fn, [(du, d, 0), (x, d, 0), (r, 1, 0), (dh, d, 0)], [(d, F32)], rows=s, tr=256,
                   name="rms_bwd_x", consts=[g], reds=[((1, d), F32)])


def rms_bwd_g(du, x, r, name):
    s, d = x.shape

    def fn(duv, xv, rv):
        return (_colsum(duv * xv * rv),)

    return rowwise(fn, [(du, d, 0), (x, d, 0), (r, 1, 0)], [], rows=s, tr=256, name=name, reds=[((1, d), F32)])[0]


def final_stage(x, hres, target, g):
    s, d = x.shape

    def fn(xv, hv, tv, gv):
        h = xv + hv
        r = lax.rsqrt(jnp.mean(h * h, axis=-1, keepdims=True) + EPS)
        y = h * r * gv
        e = y - tv
        loss = 0.5 * jnp.sum(jnp.sum(e * e, axis=-1, keepdims=True), axis=0, keepdims=True) / d
        dy = e / d
        dyg = dy * gv
        dh = r * dyg - h * (r * r * r) * jnp.mean(dyg * h, axis=-1, keepdims=True)
        return dh, dh, loss, _colsum(dy * h * r)

    return rowwise(fn, [(x, d, 0), (hres, d, 0), (target, d, 0)], [(d, F32), (d, BF16)], rows=s, tr=256,
                   name="final_stage", consts=[g], reds=[((1, 1), F32), ((1, d), F32)])


def merge_fwd(pa, pb, pc, proj, gate_cb):
    s, d = pa.shape

    def fn(a, b, c, g0, g1, g2):
        return (_sigmoid(g0) * a + _sigmoid(g1) * b + _sigmoid(g2) * c,)

    ins = [(pa, d, 0), (pb, d, 0), (pc, d, 0)] + [(proj, d, gate_cb + i) for i in range(3)]
    return rowwise(fn, ins, [(d, BF16)], rows=s, tr=256, name="merge_fwd")[0]


def merge_bwd(dm, pa, pb, pc, proj, gate_cb, dproj):
    s, d = pa.shape

    def fn(dmv, a, b, c, g0, g1, g2):
        s0, s1, s2 = _sigmoid(g0), _sigmoid(g1), _sigmoid(g2)
        dgates = [dmv * a * s0 * (1.0 - s0), dmv * b * s1 * (1.0 - s1), dmv * c * s2 * (1.0 - s2)]
        return dmv * s0, dmv * s1, dmv * s2, jnp.concatenate(dgates, axis=1)

    ins = [(dm, d, 0), (pa, d, 0), (pb, d, 0), (pc, d, 0)] + [(proj, d, gate_cb + i) for i in range(3)]
    return rowwise(fn, ins, [(d, BF16)] * 3 + [(3 * d, BF16)], rows=s, tr=128, name="merge_bwd", into=(dproj, 3, gate_cb // 3))


def gate_fwd(o, proj, z_cb, name):
    s, w = o.shape

    def fn(ov, zv):
        return (ov * _silu(zv),)

    return rowwise(fn, [(o, w, 0), (proj, w, z_cb)], [(w, BF16)], rows=s, tr=512, name=name)[0]


def gate_bwd(dgo, o, proj, z_cb, name, dproj):
    s, w = o.shape

    def fn(dv, ov, zv):
        return dv * _silu(zv), dv * ov * _dsilu(zv)

    return rowwise(fn, [(dgo, w, 0), (o, w, 0), (proj, w, z_cb)], [(w, F32), (w, BF16)], rows=s, tr=512, name=name,
                   into=(dproj, 1, z_cb))


def gdn_out_fwd(o_raw, proj, z_cb, gn):
    s, w = o_raw.shape

    def fn(ov, zv, gv):
        outs = []
        for h in range(NHEAD):
            oh = ov[:, h * HEAD:(h + 1) * HEAD]
            r = lax.rsqrt(jnp.mean(oh * oh, axis=-1, keepdims=True) + EPS)
            outs.append(oh * r * gv)
        return (jnp.concatenate(outs, axis=1) * _silu(zv),)

    return rowwise(fn, [(o_raw, w, 0), (proj, w, z_cb)], [(w, BF16)], rows=s, tr=512, name="gdn_out_fwd", consts=[gn])[0]


def gdn_out_bwd(dga, o_raw, proj, z_cb, gn, dproj):
    s, w = o_raw.shape

    def fn(dv, ov, zv, gv):
        sz, dsz = _silu(zv), _dsilu(zv)
        do_l, dz_l = [], []
        dg = jnp.zeros((1, HEAD), F32)
        for h in range(NHEAD):
            sl = slice(h * HEAD, (h + 1) * HEAD)
            oh, dgh = ov[:, sl], dv[:, sl]
            r = lax.rsqrt(jnp.mean(oh * oh, axis=-1, keepdims=True) + EPS)
            on = oh * r * gv
            don = dgh * sz[:, sl]
            dz_l.append(dgh * on * dsz[:, sl])
            dg = dg + _colsum(don * oh * r)
            dyg = don * gv
            do_l.append(r * dyg - oh * (r * r * r) * jnp.mean(dyg * oh, axis=-1, keepdims=True))
        return jnp.concatenate(do_l, axis=1), jnp.concatenate(dz_l, axis=1), dg

    return rowwise(fn, [(dga, w, 0), (o_raw, w, 0), (proj, w, z_cb)], [(w, F32), (w, BF16)], rows=s, tr=512,
                   name="gdn_out_bwd", consts=[gn], reds=[((1, HEAD), F32)], into=(dproj, 1, z_cb))


def s5_act_fwd(y_ssm, proj, xb_cb, dvec):
    s, w = y_ssm.shape

    def fn(yv, xv, dv):
        return (_gelu(yv + dv * xv),)

    return rowwise(fn, [(y_ssm, w, 0), (proj, w, xb_cb)], [(w, BF16)], rows=s, tr=512, name="s5_act_fwd", consts=[dvec])[0]


def s5_act_bwd(dyb, y_ssm, proj, xb_cb, dvec):
    s, w = y_ssm.shape

    def fn(dv_, yv, xv, dv):
        dpre = dv_ * _dgelu(yv + dv * xv)
        return dpre, dpre * dv, _colsum(dpre * xv)

    return rowwise(fn, [(dyb, w, 0), (y_ssm, w, 0), (proj, w, xb_cb)], [(w, F32), (w, F32)], rows=s, tr=512,
                   name="s5_act_bwd", consts=[dvec], reds=[((1, w), F32)])


def s5_glu_fwd(glu, proj, z_cb):
    s, w2 = glu.shape
    w = w2 // 2

    def fn(val, gate, zv):
        return (val * _sigmoid(gate) * _silu(zv),)

    return rowwise(fn, [(glu, w, 0), (glu, w, 1), (proj, w, z_cb)], [(w, BF16)], rows=s, tr=512, name="s5_glu_fwd")[0]


def s5_glu_bwd(dgb, glu, proj, z_cb, dproj):
    s, w2 = glu.shape
    w = w2 // 2

    def fn(dv, val, gate, zv):
        sg = _sigmoid(gate)
        ob = val * sg
        dob = dv * _silu(zv)
        return dob * sg, dob * val * sg * (1.0 - sg), dv * ob * _dsilu(zv)

    return rowwise(fn, [(dgb, w, 0), (glu, w, 0), (glu, w, 1), (proj, w, z_cb)], [(w, BF16)] * 3, rows=s, tr=512,
                   name="s5_glu_bwd", into=(dproj, 2, z_cb))


def add_into(a, b, name, dproj, cb):
    s, w = a.shape

    def fn(av, bv):
        return (av + bv,)

    return rowwise(fn, [(a, w, 0), (b, w, 0)], [(w, BF16)], rows=s, tr=512, name=name, into=(dproj, 0, cb))[0]


GATE_W, GATE_CB = 6144, 0
QKV_W, QKV_CB = 3072, 2
ZA_CB, XB_CB, ZB_CB, QC_CB, ZC_CB = 9, 10, 11, 12, 13
BA_CB, BA_W = 112, 128
BA_PAD, BA_PAD_CB = 1024, 14
PROJ_W = 14336 + BA_PAD


def _dot(a, b, dims="nn", prec=None):
    if prec is None:
        a, b = a.astype(BF16), b.astype(BF16)
    return lax.dot_general(a, b, _DIMS[dims], preferred_element_type=F32, precision=prec)


def _split(a):
    hi = a.astype(BF16)
    return hi, (a - hi.astype(F32)).astype(BF16)


def _dot3(a, b, dims="nn"):
    (ah, al), (bh, bl) = _split(a), _split(b)
    d = functools.partial(lax.dot_general, dimension_numbers=_DIMS[dims], preferred_element_type=F32)
    return d(ah, bh) + (d(ah, bl) + d(al, bh))


def _iota2(shape, dim):
    return lax.broadcasted_iota(jnp.int32, shape, dim)


def _conv_taps(xs, tr, k):
    if k == 0:
        return xs[8:8 + tr]
    return pltpu.roll(xs, k, 0)[8:8 + tr]


def _conv_silu_parts(xv, halo, wv, first):
    tr = xv.shape[0]
    xs = jnp.concatenate([jnp.where(first, 0.0, halo), xv], axis=0)
    taps = [_conv_taps(xs, tr, 3 - j) for j in range(4)]
    c = taps[0] * wv[0:1] + taps[1] * wv[1:2] + taps[2] * wv[2:3] + taps[3] * wv[3:4]
    return taps, c


def gdn_prep_fwd(proj, conv_w, alog_pad, dt_pad):
    s = proj.shape[0]
    tr = min(256, s)
    w = NHEAD * HEAD

    def body(x_ref, halo_ref, ba_ref, w_ref, al_ref, dt_ref, q_ref, k_ref, v_ref, bg_ref, gcol_ref, gt_ref):
        first = pl.program_id(0) == 0
        _, c = _conv_silu_parts(x_ref[...], halo_ref[...], w_ref[...], first)
        sv = _silu(c)
        for h in range(NHEAD):
            sl = slice(h * HEAD, (h + 1) * HEAD)
            qh, kh = sv[:, h * HEAD:(h + 1) * HEAD], sv[:, w + h * HEAD:w + (h + 1) * HEAD]
            q_ref[:, sl] = qh * lax.rsqrt(jnp.sum(qh * qh, axis=-1, keepdims=True) + EPS) * (HEAD ** -0.5)
            k_ref[:, sl] = kh * lax.rsqrt(jnp.sum(kh * kh, axis=-1, keepdims=True) + EPS)
        v_ref[...] = sv[:, 2 * w:]
        ba = ba_ref[...]
        lane = _iota2(ba.shape, 1)
        beta = _sigmoid(ba)
        g = -jnp.exp(al_ref[...]) * _softplus(ba + dt_ref[...])
        bg = jnp.where(lane < NHEAD, beta, jnp.where(lane < 2 * NHEAD, g, 0.0))
        bg_ref[...] = bg
        er, ec = _iota2((BA_W, BA_W), 0), _iota2((BA_W, BA_W), 1)
        expand = jnp.where((er == NHEAD + ec // 8) & (ec < 8 * NHEAD), 1.0, 0.0)
        grep = _dot(bg, expand, prec=HI)
        lr, lc = _iota2((tr, tr), 0), _iota2((tr, tr), 1)
        tril = jnp.where((lr // CHUNK == lc // CHUNK) & (lr >= lc), 1.0, 0.0)
        gc = _dot(tril, grep, prec=HI)
        gcol_ref[...] = gc
        gt_ref[...] = gc.T

    nb8 = tr // 8
    return pl.pallas_call(
        body, name="gdn_prep_fwd", grid=(s // tr,),
        in_specs=[pl.BlockSpec((tr, QKV_W), lambda i: (i, QKV_CB)),
                  pl.BlockSpec((8, QKV_W), lambda i: (jnp.maximum(i * nb8 - 1, 0), QKV_CB)),
                  pl.BlockSpec((tr, BA_W), lambda i: (i, BA_CB)),
                  pl.BlockSpec(conv_w.shape, lambda i: (0, 0)),
                  pl.BlockSpec((1, BA_W), lambda i: (0, 0)), pl.BlockSpec((1, BA_W), lambda i: (0, 0))],
        out_specs=[pl.BlockSpec((tr, w), lambda i: (i, 0))] * 3 + [pl.BlockSpec((tr, BA_W), lambda i: (i, 0))] * 2
        + [pl.BlockSpec((BA_W, tr), lambda i: (0, i))],
        out_shape=[jax.ShapeDtypeStruct((s, w), F32)] * 3 + [jax.ShapeDtypeStruct((s, BA_W), F32)] * 2
        + [jax.ShapeDtypeStruct((BA_W, s), F32)],
        compiler_params=_cparams(("parallel",)),
    )(proj, proj, proj, conv_w, alog_pad, dt_pad)


def _chunk_common(qh, kh, bgv, gcolv, gtv, h):
    beta = bgv[:, h:h + 1]
    gcc = gcolv[:, 8 * h:8 * h + 1]
    gcr = jnp.concatenate([gtv[8 * h:8 * h + 8, :]] * (CHUNK // 8), axis=0)
    ii, jj = _iota2((CHUNK, CHUNK), 0), _iota2((CHUNK, CHUNK), 1)
    incl, strict = ii >= jj, ii > jj
    decay = jnp.where(incl, jnp.exp(jnp.where(incl, gcc - gcr, 0.0)), 0.0)
    gl = gcr[:, CHUNK - 1:CHUNK]
    return beta, gcc, decay, strict, gl


def gdn_intra_fwd(q, k, v, bg, gcol, gt3):
    s, w = q.shape
    n = s // CHUNK

    def body(q_ref, k_ref, v_ref, bg_ref, gcol_ref, gt_ref, u_ref, w_ref, qd_ref, kd_ref, qk_ref, t_ref):
        bgv, gcolv, gtv = bg_ref[...], gcol_ref[...], gt_ref[0]
        ii, jj = _iota2((CHUNK, CHUNK), 0), _iota2((CHUNK, CHUNK), 1)
        eye = jnp.where(ii == jj, 1.0, 0.0)
        ps, ts, rhs = [], [], []
        for h in range(NHEAD):
            sl = slice(h * HEAD, (h + 1) * HEAD)
            qh, kh, vh = q_ref[:, sl], k_ref[:, sl], v_ref[:, sl]
            beta, gcc, decay, strict, gl = _chunk_common(qh, kh, bgv, gcolv, gtv, h)
            kb = kh * beta
            eg = jnp.exp(gcc)
            p = -jnp.where(strict, _dot(kb, kh, "nt") * decay, 0.0)
            ps.append(p)
            ts.append(eye + p)
            rhs.append((vh * beta, kb * eg))
            qd_ref[:, sl] = qh * eg
            kd_ref[:, sl] = kh * jnp.exp(gl - gcc)
            qk_ref[0, h] = _dot(qh, kh, "nt") * decay
        for _ in range(5):
            ps = [_dot3(p, p) for p in ps]
            ts = [t + _dot3(t, p) for t, p in zip(ts, ps)]
        for h in range(NHEAD):
            sl = slice(h * HEAD, (h + 1) * HEAD)
            u_ref[:, sl] = _dot3(ts[h], rhs[h][0])
            w_ref[:, sl] = _dot3(ts[h], rhs[h][1])
            t_ref[0, h] = ts[h]

    tok = pl.BlockSpec((CHUNK, w), lambda i: (i, 0))
    sm = pl.BlockSpec((CHUNK, BA_W), lambda i: (i, 0))
    sq = pl.BlockSpec((1, NHEAD, CHUNK, CHUNK), lambda i: (i, 0, 0, 0))
    return pl.pallas_call(
        body, name="gdn_intra_fwd", grid=(n,),
        in_specs=[tok, tok, tok, sm, sm, pl.BlockSpec((1, BA_W, CHUNK), lambda i: (i, 0, 0))],
        out_specs=[tok] * 4 + [sq, sq],
        out_shape=[jax.ShapeDtypeStruct((s, w), F32)] * 4 + [jax.ShapeDtypeStruct((n, NHEAD, CHUNK, CHUNK), F32)] * 2,
        compiler_params=_cparams(("parallel",)),
    )(q, k, v, bg, gcol, gt3)


def _state_decay(gtv, h):
    g8 = gtv[8 * h:8 * h + 8, CHUNK - 1:CHUNK]
    return jnp.exp(jnp.concatenate([g8] * (HEAD // 8), axis=0))


def gdn_seq_fwd(u, wd, qd, kd, qk, gt3):
    s, w = u.shape
    n = s // CHUNK

    def body(u_ref, w_ref, qd_ref, kd_ref, qk_ref, gt_ref, o_ref, st_ref, s_ref):
        @pl.when(pl.program_id(0) == 0)
        def _():
            s_ref[...] = jnp.zeros_like(s_ref)

        gtv = gt_ref[0]
        cols = [slice(h * HEAD, (h + 1) * HEAD) for h in range(NHEAD)]
        states = [s_ref[h] for h in range(NHEAD)]
        for h in range(NHEAD):
            st_ref[0, h] = states[h]
        vns = [u_ref[:, cols[h]] - _dot(w_ref[:, cols[h]], states[h]) for h in range(NHEAD)]
        from_state = [_dot(qd_ref[:, cols[h]], states[h]) for h in range(NHEAD)]
        for h in range(NHEAD):
            o_ref[:, cols[h]] = from_state[h] + _dot(qk_ref[0, h], vns[h])
        for h in range(NHEAD):
            s_ref[h] = states[h] * _state_decay(gtv, h) + _dot(kd_ref[:, cols[h]], vns[h], "tn")

    tok = pl.BlockSpec((CHUNK, w), lambda i: (i, 0))
    return pl.pallas_call(
        body, name="gdn_seq_fwd", grid=(n,),
        in_specs=[tok] * 4 + [pl.BlockSpec((1, NHEAD, CHUNK, CHUNK), lambda i: (i, 0, 0, 0)),
                              pl.BlockSpec((1, BA_W, CHUNK), lambda i: (i, 0, 0))],
        out_specs=[tok, pl.BlockSpec((1, NHEAD, HEAD, HEAD), lambda i: (i, 0, 0, 0))],
        out_shape=[jax.ShapeDtypeStruct((s, w), F32), jax.ShapeDtypeStruct((n, NHEAD, HEAD, HEAD), F32)],
        scratch_shapes=[pltpu.VMEM((NHEAD, HEAD, HEAD), F32)],
        compiler_params=_cparams(("arbitrary",)),
    )(u, wd, qd, kd, qk, gt3)


def gdn_seq_bwd(do, u, wd, qd, kd, qk, gt3, states):
    s, w = u.shape
    n = s // CHUNK

    def body(do_ref, u_ref, w_ref, qd_ref, kd_ref, qk_ref, gt_ref, st_ref,
             du_ref, dw_ref, dqd_ref, dkd_ref, dqk_ref, dgl_ref, ds_ref):
        @pl.when(pl.program_id(0) == 0)
        def _():
            ds_ref[...] = jnp.zeros_like(ds_ref)

        gtv = gt_ref[0]
        heads = range(NHEAD)
        cols = [slice(h * HEAD, (h + 1) * HEAD) for h in heads]
        sts = [st_ref[0, h] for h in heads]
        dsps = [ds_ref[h] for h in heads]
        vns = [u_ref[:, cols[h]] - _dot(w_ref[:, cols[h]], sts[h]) for h in heads]
        dvns = [_dot(qk_ref[0, h], do_ref[:, cols[h]], "tn") + _dot(kd_ref[:, cols[h]], dsps[h]) for h in heads]
        for h in heads:
            du_ref[:, cols[h]] = dvns[h]
            dqd_ref[:, cols[h]] = _dot(do_ref[:, cols[h]], sts[h], "nt")
        for h in heads:
            dw_ref[:, cols[h]] = -_dot(dvns[h], sts[h], "nt")
            dkd_ref[:, cols[h]] = _dot(vns[h], dsps[h], "nt")
            dqk_ref[0, h] = _dot(do_ref[:, cols[h]], vns[h], "nt")
        for h in heads:
            ds_ref[h] = (dsps[h] * _state_decay(gtv, h) + _dot(qd_ref[:, cols[h]], do_ref[:, cols[h]], "tn")
                         - _dot(w_ref[:, cols[h]], dvns[h], "tn"))
        dgl_ref[0] = jnp.concatenate([_colsum(sts[h] * dsps[h]) for h in heads], axis=0)

    tok = pl.BlockSpec((CHUNK, w), lambda i: (n - 1 - i, 0))
    sq = pl.BlockSpec((1, NHEAD, CHUNK, CHUNK), lambda i: (n - 1 - i, 0, 0, 0))
    return pl.pallas_call(
        body, name="gdn_seq_bwd", grid=(n,),
        in_specs=[tok] * 5 + [sq, pl.BlockSpec((1, BA_W, CHUNK), lambda i: (n - 1 - i, 0, 0)),
                              pl.BlockSpec((1, NHEAD, HEAD, HEAD), lambda i: (n - 1 - i, 0, 0, 0))],
        out_specs=[tok] * 4 + [sq, pl.BlockSpec((1, NHEAD, HEAD), lambda i: (n - 1 - i, 0, 0))],
        out_shape=[jax.ShapeDtypeStruct((s, w), F32)] * 4 + [jax.ShapeDtypeStruct((n, NHEAD, CHUNK, CHUNK), F32),
                                                            jax.ShapeDtypeStruct((n, NHEAD, HEAD), F32)],
        scratch_shapes=[pltpu.VMEM((NHEAD, HEAD, HEAD), F32)],
        compiler_params=_cparams(("arbitrary",)),
    )(do, u, wd, qd, kd, qk, gt3, states)


def gdn_intra_bwd(q, k, v, bg, gcol, gt3, tinv, du, dw, dqd, dkd, dqk, dgl):
    s, w = q.shape
    n = s // CHUNK

    def body(q_ref, k_ref, v_ref, bg_ref, gcol_ref, gt_ref, t_ref, du_ref, dw_ref, dqd_ref, dkd_ref, dqk_ref, dgl_ref,
             dq_ref, dk_ref, dv_ref, dbg_ref):
        bgv, gcolv, gtv, dglv = bg_ref[...], gcol_ref[...], gt_ref[0], dgl_ref[0]
        ii, jj = _iota2((CHUNK, CHUNK), 0), _iota2((CHUNK, CHUNK), 1)
        triu = jnp.where(ii <= jj, 1.0, 0.0)
        ones = jnp.ones((CHUNK, BA_W), F32)
        lane = _iota2((CHUNK, BA_W), 1)
        row = _iota2((CHUNK, 1), 0)
        dbg = jnp.zeros((CHUNK, BA_W), F32)
        first = []
        for h in range(NHEAD):
            sl = slice(h * HEAD, (h + 1) * HEAD)
            qh, kh, vh = q_ref[:, sl], k_ref[:, sl], v_ref[:, sl]
            beta, gcc, decay, strict, gl = _chunk_common(qh, kh, bgv, gcolv, gtv, h)
            kb = kh * beta
            eg = jnp.exp(gcc)
            rv, rk = vh * beta, kb * eg
            t, duh, dwh = t_ref[0, h], du_ref[:, sl], dw_ref[:, sl]
            first.append((_dot3(duh, rv, "nt") + _dot3(dwh, rk, "nt"), _dot3(t, duh, "tn"), _dot3(t, dwh, "tn"),
                          _dot(kb, kh, "nt"), _dot(qh, kh, "nt")))
        second = [_dot3(t_ref[0, h], first[h][0], "tn") for h in range(NHEAD)]
        third = [_dot3(second[h], t_ref[0, h], "nt") for h in range(NHEAD)]
        for h in range(NHEAD):
            sl = slice(h * HEAD, (h + 1) * HEAD)
            qh, kh, vh = q_ref[:, sl], k_ref[:, sl], v_ref[:, sl]
            beta, gcc, decay, strict, gl = _chunk_common(qh, kh, bgv, gcolv, gtv, h)
            dqdh, dkdh, dqkh = dqd_ref[:, sl], dkd_ref[:, sl], dqk_ref[0, h]
            kb = kh * beta
            eg = jnp.exp(gcc)
            ekd = jnp.exp(gl - gcc)
            rk = kb * eg
            _, drv, drk, m, p = first[h]
            da = jnp.where(strict, -third[h], 0.0)
            dm = da * decay
            dpm = dqkh * decay
            dkb = _dot(dm, kh) + drk * eg
            dq = _dot(dpm, kh) + dqdh * eg
            dk = _dot(dm, kb, "tn") + _dot(dpm, qh, "tn") + dkdh * ekd + dkb * beta
            e = (da * m + dqkh * p) * decay
            sk = jnp.sum(dkdh * kh * ekd, axis=-1, keepdims=True)
            dgc = (jnp.sum(e, axis=-1, keepdims=True) - _dot3(e, ones, "tn")[:, 0:1]
                   + jnp.sum(dqdh * qh * eg, axis=-1, keepdims=True) - sk + jnp.sum(drk * rk, axis=-1, keepdims=True))
            dglast = jnp.sum(sk, axis=0, keepdims=True) + jnp.sum(dglv[h:h + 1, :], axis=-1, keepdims=True) * jnp.exp(gl)
            dgc = dgc + jnp.where(row == CHUNK - 1, dglast, 0.0)
            dg = _dot3(triu, dgc * ones)
            dbeta = jnp.sum(dkb * kh, axis=-1, keepdims=True) + jnp.sum(drv * vh, axis=-1, keepdims=True)
            dbg = dbg + jnp.where(lane == h, dbeta, 0.0) + jnp.where(lane == NHEAD + h, dg, 0.0)
            dq_ref[:, sl] = dq
            dk_ref[:, sl] = dk
            dv_ref[:, sl] = drv * beta
        dbg_ref[...] = dbg

    tok = pl.BlockSpec((CHUNK, w), lambda i: (i, 0))
    sm = pl.BlockSpec((CHUNK, BA_W), lambda i: (i, 0))
    sq = pl.BlockSpec((1, NHEAD, CHUNK, CHUNK), lambda i: (i, 0, 0, 0))
    return pl.pallas_call(
        body, name="gdn_intra_bwd", grid=(n,),
        in_specs=[tok, tok, tok, sm, sm, pl.BlockSpec((1, BA_W, CHUNK), lambda i: (i, 0, 0)), sq,
                  tok, tok, tok, tok, sq, pl.BlockSpec((1, NHEAD, HEAD), lambda i: (i, 0, 0))],
        out_specs=[tok] * 3 + [sm],
        out_shape=[jax.ShapeDtypeStruct((s, w), F32)] * 3 + [jax.ShapeDtypeStruct((s, BA_W), F32)],
        compiler_params=_cparams(("parallel",)),
    )(q, k, v, bg, gcol, gt3, tinv, du, dw, dqd, dkd, dqk, dgl)


def gdn_prep_bwd1(proj, conv_w, alog_pad, dt_pad, dq, dk, dv, dbg, dproj):
    s = proj.shape[0]
    tr = min(256, s)
    w = NHEAD * HEAD
    pad_w = BA_PAD

    def body(x_ref, halo_ref, ba_ref, w_ref, al_ref, dt_ref, dq_ref, dk_ref, dv_ref, dbg_ref, buf_ref,
             dc_ref, dba_ref, dw0_ref, dw1_ref, dw2_ref, dw3_ref, dal_ref, ddt_ref):
        i = pl.program_id(0)
        taps, c = _conv_silu_parts(x_ref[...], halo_ref[...], w_ref[...], i == 0)
        sv, dsv = _silu(c), _dsilu(c)
        for h in range(NHEAD):
            for base, d_ref, scale in ((0, dq_ref, HEAD ** -0.5), (w, dk_ref, 1.0)):
                sl = slice(base + h * HEAD, base + (h + 1) * HEAD)
                sh = sv[:, sl]
                dn = d_ref[:, h * HEAD:(h + 1) * HEAD]
                r = lax.rsqrt(jnp.sum(sh * sh, axis=-1, keepdims=True) + EPS)
                dsh = scale * (r * dn - sh * (r * r * r) * jnp.sum(dn * sh, axis=-1, keepdims=True))
                dc_ref[:, sl] = dsh * dsv[:, sl]
        dc_ref[:, 2 * w:] = dv_ref[...] * dsv[:, 2 * w:]
        dc = dc_ref[...]
        ba, dbgv = ba_ref[...], dbg_ref[...]
        lane = _iota2(ba.shape, 1)
        beta = _sigmoid(ba)
        ea = jnp.exp(al_ref[...])
        z = ba + dt_ref[...]
        g = -ea * _softplus(z)
        is_g = (lane >= NHEAD) & (lane < 2 * NHEAD)
        da_raw = jnp.where(is_g, dbgv * (-ea) * _sigmoid(z), 0.0)
        dba = jnp.where(lane < NHEAD, dbgv * beta * (1.0 - beta), da_raw)
        dba_ref[...] = jnp.concatenate([dba, jnp.zeros((tr, pad_w - BA_W), F32)], axis=1).astype(BF16)
        partial = [_colsum(dc * tp) for tp in taps] + [_colsum(jnp.where(is_g, dbgv * g, 0.0)), _colsum(da_raw)]
        red_refs = (dw0_ref, dw1_ref, dw2_ref, dw3_ref, dal_ref, ddt_ref)

        @pl.when(i == 0)
        def _():
            for r_, v_ in zip(red_refs, partial):
                r_[...] = v_

        @pl.when(i > 0)
        def _():
            for r_, v_ in zip(red_refs, partial):
                r_[...] += v_

    nb8 = tr // 8
    tok = pl.BlockSpec((tr, w), lambda i: (i, 0))
    one = lambda width: pl.BlockSpec((1, width), lambda i: (0, 0))
    return pl.pallas_call(
        body, name="gdn_prep_bwd1", grid=(s // tr,),
        in_specs=[pl.BlockSpec((tr, QKV_W), lambda i: (i, QKV_CB)),
                  pl.BlockSpec((8, QKV_W), lambda i: (jnp.maximum(i * nb8 - 1, 0), QKV_CB)),
                  pl.BlockSpec((tr, BA_W), lambda i: (i, BA_CB)),
                  pl.BlockSpec(conv_w.shape, lambda i: (0, 0)), one(BA_W), one(BA_W),
                  tok, tok, tok, pl.BlockSpec((tr, BA_W), lambda i: (i, 0)), pl.BlockSpec(memory_space=pl.ANY)],
        out_specs=[pl.BlockSpec((tr, QKV_W), lambda i: (i, 0)), pl.BlockSpec((tr, pad_w), lambda i: (i, BA_PAD_CB))]
        + [one(QKV_W)] * 4 + [one(BA_W)] * 2,
        out_shape=[jax.ShapeDtypeStruct((s, QKV_W), F32), jax.ShapeDtypeStruct(dproj.shape, dproj.dtype)]
        + [jax.ShapeDtypeStruct((1, QKV_W), F32)] * 4 + [jax.ShapeDtypeStruct((1, BA_W), F32)] * 2,
        input_output_aliases={10: 1}, compiler_params=_cparams(("arbitrary",)),
    )(proj, proj, proj, conv_w, alog_pad, dt_pad, dq, dk, dv, dbg, dproj)


def gdn_prep_bwd2(dc, conv_w, dproj):
    s = dc.shape[0]
    tr = min(256, s)
    nblk = s // tr
    nb8 = tr // 8

    def body(dc_ref, halo_ref, w_ref, buf_ref, o_ref):
        last = pl.program_id(0) == nblk - 1
        wv = w_ref[...]
        xs = jnp.concatenate([dc_ref[...], jnp.where(last, 0.0, halo_ref[...])], axis=0)
        acc = xs[:tr] * wv[3:4]
        for j in range(3):
            acc = acc + pltpu.roll(xs, tr + 8 - (3 - j), 0)[:tr] * wv[j:j + 1]
        o_ref[...] = acc.astype(BF16)

    return pl.pallas_call(
        body, name="gdn_prep_bwd2", grid=(nblk,),
        in_specs=[pl.BlockSpec((tr, QKV_W), lambda i: (i, 0)),
                  pl.BlockSpec((8, QKV_W), lambda i: (jnp.minimum((i + 1) * nb8, s // 8 - 1), 0)),
                  pl.BlockSpec(conv_w.shape, lambda i: (0, 0)), pl.BlockSpec(memory_space=pl.ANY)],
        out_specs=pl.BlockSpec((tr, QKV_W), lambda i: (i, QKV_CB)),
        out_shape=jax.ShapeDtypeStruct(dproj.shape, dproj.dtype), input_output_aliases={3: 0},
        compiler_params=_cparams(("parallel",)),
    )(dc, dc, conv_w, dproj)


S5_W = S5_GROUPS * S5_STATE
S5_IN = S5_GROUPS * S5_GROUP
S5_TILES = 8
S5_TW, S5_TI = S5_W // S5_TILES, S5_IN // S5_TILES


def _s5_param_math(lr, li, ldt, br, bi):
    pr, pc = _iota2((S5_STATE, S5_STATE * S5_GROUP), 0), _iota2((S5_STATE, S5_STATE * S5_GROUP), 1)
    rep = jnp.where(pc // S5_GROUP == pr, 1.0, 0.0)
    dt = jnp.exp(ldt)
    mag = jnp.exp(lr * dt)
    ab_re, ab_im = mag * jnp.cos(li * dt), mag * jnp.sin(li * dt)
    den = lr * lr + li * li
    nr, ni = ab_re - 1.0, ab_im
    coef_re = (nr * lr + ni * li) / den
    coef_im = (ni * lr - nr * li) / den
    cr, ci = _dot(coef_re, rep, prec=HI), _dot(coef_im, rep, prec=HI)
    return ab_re, ab_im, cr * br - ci * bi, cr * bi + ci * br


def s5_param_fwd(lr, li, ldt, br, bi):
    def body(lr_ref, li_ref, ldt_ref, br_ref, bi_ref, ar_ref, ai_ref, bbr_ref, bbi_ref):
        res = _s5_param_math(lr_ref[...], li_ref[...], ldt_ref[...], br_ref[...], bi_ref[...])
        for r, v in zip((ar_ref, ai_ref, bbr_ref, bbi_ref), res):
            r[...] = v

    return pl.pallas_call(
        body, name="s5_param_fwd",
        out_shape=[jax.ShapeDtypeStruct(lr.shape, F32)] * 2 + [jax.ShapeDtypeStruct(br.shape, F32)] * 2,
        compiler_params=_cparams(),
    )(lr, li, ldt, br, bi)


def s5_param_bwd(lr, li, ldt, br, bi, dar, dai, dbbr, dbbi):
    def body(lr_ref, li_ref, ldt_ref, br_ref, bi_ref, dar_ref, dai_ref, dbbr_ref, dbbi_ref, *out_refs):
        _, vjp = jax.vjp(_s5_param_math, lr_ref[...], li_ref[...], ldt_ref[...], br_ref[...], bi_ref[...])
        for r, v in zip(out_refs, vjp((dar_ref[...], dai_ref[...], dbbr_ref[...], dbbi_ref[...]))):
            r[...] = v

    return pl.pallas_call(
        body, name="s5_param_bwd",
        out_shape=[jax.ShapeDtypeStruct(a.shape, F32) for a in (lr, li, ldt, br, bi)],
        compiler_params=_cparams(),
    )(lr, li, ldt, br, bi, dar, dai, dbbr, dbbi)


def _cmul(ar, ai, br, bi):
    return ar * br - ai * bi, ar * bi + ai * br


def _s5_power(ar, ai, steps):
    assert steps & (steps - 1) == 0
    for _ in range(steps.bit_length() - 1):
        ar, ai = _cmul(ar, ai, ar, ai)
    return ar, ai


def _s5_scan_rows(ar_ref, ai_ref, re_ref, im_ref, sr_ref, si_ref, tb, row0, reverse):
    quarter = S5_W // 4
    for qd in range(4):
        cs = slice(qd * quarter, (qd + 1) * quarter)
        are = jnp.broadcast_to(ar_ref[:, cs], (NSEG, quarter))
        aim = jnp.broadcast_to(ai_ref[:, cs], (NSEG, quarter))
        if reverse:
            aim = -aim

        def step(t, carry):
            h_r, h_i = carry
            tt = tb - 1 - t if reverse else t
            rows = pl.ds(pl.multiple_of(row0 + tt * NSEG, NSEG), NSEG)
            n_r = are * h_r - aim * h_i + re_ref[rows, cs]
            n_i = are * h_i + aim * h_r + im_ref[rows, cs]
            re_ref[rows, cs] = n_r
            im_ref[rows, cs] = n_i
            return n_r, n_i

        h_r, h_i = lax.fori_loop(0, tb, step, (sr_ref[:, cs], si_ref[:, cs]), unroll=8)
        sr_ref[:, cs] = h_r
        si_ref[:, cs] = h_i


def _s5_segment_carry(ar_ref, ai_ref, sr_ref, si_ref, steps, reverse):
    pr, pi = _s5_power(ar_ref[...], ai_ref[...], steps)
    if reverse:
        pi = -pi
    cur_r = jnp.zeros((1, S5_W), F32)
    cur_i = jnp.zeros((1, S5_W), F32)
    for s in (range(NSEG - 1, -1, -1) if reverse else range(NSEG)):
        e_r, e_i = sr_ref[s:s + 1, :], si_ref[s:s + 1, :]
        sr_ref[s:s + 1, :] = cur_r
        si_ref[s:s + 1, :] = cur_i
        nr, ni = _cmul(pr, pi, cur_r, cur_i)
        cur_r, cur_i = nr + e_r, ni + e_i


def _s5_blocks(s):
    steps = s // NSEG
    tb = min(32, steps)
    return steps, tb, NSEG * tb, steps // tb


def s5_scan_fwd(xp, a_re, a_im, bre, bim, cre, cim):
    s = xp.shape[0]
    steps, tb, rb, nb = _s5_blocks(s)

    def body(x_ref, ar_ref, ai_ref, bre_ref, bim_ref, cre_ref, cim_ref, y_ref, hsr_ref, hsi_ref,
             hr_ref, hi_ref, sr_ref, si_ref):
        ph, b = pl.program_id(0), pl.program_id(1)

        @pl.when((ph == 0) & (b == 0))
        def _():
            sr_ref[...] = jnp.zeros_like(sr_ref)
            si_ref[...] = jnp.zeros_like(si_ref)

        @pl.when((ph == 1) & (b == 0))
        def _():
            _s5_segment_carry(ar_ref, ai_ref, sr_ref, si_ref, steps, False)

        xv = x_ref[...].astype(BF16)
        for j in range(S5_TILES):
            xs = xv[:, j * S5_TI:(j + 1) * S5_TI]
            hr_ref[:, j * S5_TW:(j + 1) * S5_TW] = _dot(xs, bre_ref[j])
            hi_ref[:, j * S5_TW:(j + 1) * S5_TW] = _dot(xs, bim_ref[j])

        @pl.when(ph == 1)
        def _():
            hsr_ref[0] = sr_ref[...]
            hsi_ref[0] = si_ref[...]

        _s5_scan_rows(ar_ref, ai_ref, hr_ref, hi_ref, sr_ref, si_ref, tb, 0, False)

        @pl.when(ph == 1)
        def _():
            for j in range(S5_TILES):
                cs = slice(j * S5_TW, (j + 1) * S5_TW)
                y_ref[:, j * S5_TI:(j + 1) * S5_TI] = _dot(hr_ref[:, cs], cre_ref[j]) - _dot(hi_ref[:, cs], cim_ref[j])

    row = pl.BlockSpec((1, S5_W), lambda p, b: (0, 0))
    wb = pl.BlockSpec((S5_TILES, S5_TI, S5_TW), lambda p, b: (0, 0, 0))
    wc = pl.BlockSpec((S5_TILES, S5_TW, S5_TI), lambda p, b: (0, 0, 0))
    st = pl.BlockSpec((1, NSEG, S5_W), lambda p, b: (p * b, 0, 0))
    return pl.pallas_call(
        body, name="s5_scan_fwd", grid=(2, nb),
        in_specs=[pl.BlockSpec((rb, S5_IN), lambda p, b: (b, 0)), row, row, wb, wb, wc, wc],
        out_specs=[pl.BlockSpec((rb, S5_IN), lambda p, b: (p * b, 0)), st, st],
        out_shape=[jax.ShapeDtypeStruct((s, S5_IN), F32)] + [jax.ShapeDtypeStruct((nb, NSEG, S5_W), F32)] * 2,
        scratch_shapes=[pltpu.VMEM((rb, S5_W), F32)] * 2 + [pltpu.VMEM((NSEG, S5_W), F32)] * 2,
        compiler_params=_cparams(("arbitrary", "arbitrary")),
    )(xp, a_re, a_im, bre, bim, cre, cim)


def s5_scan_bwd(dyp, xp, a_re, a_im, bre, bim, cre_t, cim_t, hs_r, hs_i):
    s = xp.shape[0]
    steps, tb, rb, nb = _s5_blocks(s)

    def body(dy_ref, x_ref, ar_ref, ai_ref, bre_ref, bim_ref, crt_ref, cit_ref, hsr_ref, hsi_ref,
             dx_ref, dar_ref, dai_ref, dbr_ref, dbi_ref, dcr_ref, dci_ref,
             hr_ref, hi_ref, lr_ref, li_ref, sr_ref, si_ref, fr_ref, fi_ref, accr_ref, acci_ref):
        ph, b = pl.program_id(0), pl.program_id(1)

        @pl.when((ph == 0) & (b == 0))
        def _():
            sr_ref[...] = jnp.zeros_like(sr_ref)
            si_ref[...] = jnp.zeros_like(si_ref)

        @pl.when((ph == 1) & (b == 0))
        def _():
            _s5_segment_carry(ar_ref, ai_ref, sr_ref, si_ref, steps, True)
            for r in (accr_ref, acci_ref, dbr_ref, dbi_ref, dcr_ref, dci_ref):
                r[...] = jnp.zeros_like(r)

        dyv = dy_ref[...].astype(BF16)
        for j in range(S5_TILES):
            ds_ = dyv[:, j * S5_TI:(j + 1) * S5_TI]
            lr_ref[:, j * S5_TW:(j + 1) * S5_TW] = _dot(ds_, crt_ref[j])
            li_ref[:, j * S5_TW:(j + 1) * S5_TW] = -_dot(ds_, cit_ref[j])
        _s5_scan_rows(ar_ref, ai_ref, lr_ref, li_ref, sr_ref, si_ref, tb, 0, True)

        @pl.when(ph == 1)
        def _():
            xv = x_ref[...].astype(BF16)
            for j in range(S5_TILES):
                xs = xv[:, j * S5_TI:(j + 1) * S5_TI]
                hr_ref[NSEG:, j * S5_TW:(j + 1) * S5_TW] = _dot(xs, bre_ref[j])
                hi_ref[NSEG:, j * S5_TW:(j + 1) * S5_TW] = _dot(xs, bim_ref[j])
            hr_ref[0:NSEG, :] = hsr_ref[0]
            hi_ref[0:NSEG, :] = hsi_ref[0]
            fr_ref[...] = hsr_ref[0]
            fi_ref[...] = hsi_ref[0]
            _s5_scan_rows(ar_ref, ai_ref, hr_ref, hi_ref, fr_ref, fi_ref, tb, NSEG, False)
            lam_r, lam_i = lr_ref[...], li_ref[...]
            hp_r, hp_i = hr_ref[0:rb, :], hi_ref[0:rb, :]
            accr_ref[...] += jnp.sum((lam_r * hp_r + lam_i * hp_i).reshape(tb, NSEG, S5_W), axis=0)
            acci_ref[...] += jnp.sum((lam_i * hp_r - lam_r * hp_i).reshape(tb, NSEG, S5_W), axis=0)
            lam_rb, lam_ib = lam_r.astype(BF16), lam_i.astype(BF16)
            h_rb, h_ib = hr_ref[NSEG:, :].astype(BF16), hi_ref[NSEG:, :].astype(BF16)
            for j in range(S5_TILES):
                cs, ci = slice(j * S5_TW, (j + 1) * S5_TW), slice(j * S5_TI, (j + 1) * S5_TI)
                dbr_ref[j] += _dot(xv[:, ci], lam_rb[:, cs], "tn")
                dbi_ref[j] += _dot(xv[:, ci], lam_ib[:, cs], "tn")
                dx_ref[:, ci] = _dot(lam_rb[:, cs], bre_ref[j], "nt") + _dot(lam_ib[:, cs], bim_ref[j], "nt")
                dcr_ref[j] += _dot(h_rb[:, cs], dyv[:, ci], "tn")
                dci_ref[j] -= _dot(h_ib[:, cs], dyv[:, ci], "tn")

        @pl.when((ph == 1) & (b == nb - 1))
        def _():
            dar_ref[...] = jnp.sum(accr_ref[...], axis=0, keepdims=True)
            dai_ref[...] = jnp.sum(acci_ref[...], axis=0, keepdims=True)

    rev = lambda p, b: (nb - 1 - b, 0)
    row = pl.BlockSpec((1, S5_W), lambda p, b: (0, 0))
    wb = pl.BlockSpec((S5_TILES, S5_TI, S5_TW), lambda p, b: (0, 0, 0))
    wc = pl.BlockSpec((S5_TILES, S5_TW, S5_TI), lambda p, b: (0, 0, 0))
    st = pl.BlockSpec((1, NSEG, S5_W), lambda p, b: (nb - 1 - b, 0, 0))
    big = pltpu.VMEM((rb, S5_W), F32)
    big8 = pltpu.VMEM((rb + NSEG, S5_W), F32)
    small = pltpu.VMEM((NSEG, S5_W), F32)
    return pl.pallas_call(
        body, name="s5_scan_bwd", grid=(2, nb),
        in_specs=[pl.BlockSpec((rb, S5_IN), rev), pl.BlockSpec((rb, S5_IN), rev), row, row, wb, wb, wb, wb, st, st],
        out_specs=[pl.BlockSpec((rb, S5_IN), lambda p, b: (nb - 1 - p * b, 0)), row, row, wb, wb, wc, wc],
        out_shape=[jax.ShapeDtypeStruct((s, S5_IN), F32)] + [jax.ShapeDtypeStruct((1, S5_W), F32)] * 2
        + [jax.ShapeDtypeStruct((S5_TILES, S5_TI, S5_TW), F32)] * 2 + [jax.ShapeDtypeStruct((S5_TILES, S5_TW, S5_TI), F32)] * 2,
        scratch_shapes=[big8, big8, big, big, small, small, small, small, small, small],
        compiler_params=_cparams(("arbitrary", "arbitrary")),
    )(dyp, xp, a_re, a_im, bre, bim, cre_t, cim_t, hs_r, hs_i)


XA_DIM = 256
XA_W = XA_HEADS * XA_DIM


def _xa_probs(qh, kh):
    sc = _dot(qh, kh, "nt") * (XA_DIM ** -0.5)
    ex = jnp.exp(sc - jnp.max(sc, axis=-1, keepdims=True))
    return ex / jnp.sum(ex, axis=-1, keepdims=True)


def xa_fwd(proj, kv):
    s = proj.shape[0]
    tq = min(512, s)

    def body(q_ref, kv_ref, o_ref):
        for h in range(XA_HEADS):
            sl = slice(h * XA_DIM, (h + 1) * XA_DIM)
            p = _xa_probs(q_ref[:, sl], kv_ref[:, sl])
            o_ref[:, sl] = _dot(p, kv_ref[:, XA_W + h * XA_DIM:XA_W + (h + 1) * XA_DIM])

    return pl.pallas_call(
        body, name="xa_fwd", grid=(s // tq,),
        in_specs=[pl.BlockSpec((tq, XA_W), lambda i: (i, QC_CB)), pl.BlockSpec(kv.shape, lambda i: (0, 0))],
        out_specs=pl.BlockSpec((tq, XA_W), lambda i: (i, 0)),
        out_shape=jax.ShapeDtypeStruct((s, XA_W), F32),
        compiler_params=_cparams(("parallel",)),
    )(proj, kv)


def xa_bwd(do, proj, kv, dproj):
    s = proj.shape[0]
    tq = min(512, s)

    def body(do_ref, q_ref, kv_ref, buf_ref, dq_ref, dkv_ref):
        @pl.when(pl.program_id(0) == 0)
        def _():
            dkv_ref[...] = jnp.zeros_like(dkv_ref)

        for h in range(XA_HEADS):
            sl = slice(h * XA_DIM, (h + 1) * XA_DIM)
            sv = slice(XA_W + h * XA_DIM, XA_W + (h + 1) * XA_DIM)
            qh, kh, vh, doh = q_ref[:, sl], kv_ref[:, sl], kv_ref[:, sv], do_ref[:, sl]
            p = _xa_probs(qh, kh)
            dp = _dot(doh, vh, "nt")
            ds_ = p * (dp - jnp.sum(dp * p, axis=-1, keepdims=True)) * (XA_DIM ** -0.5)
            dq_ref[:, sl] = _dot(ds_, kh).astype(BF16)
            dkv_ref[:, sl] += _dot(ds_, qh, "tn")
            dkv_ref[:, sv] += _dot(p, doh, "tn")

    return pl.pallas_call(
        body, name="xa_bwd", grid=(s // tq,),
        in_specs=[pl.BlockSpec((tq, XA_W), lambda i: (i, 0)), pl.BlockSpec((tq, XA_W), lambda i: (i, QC_CB)),
                  pl.BlockSpec(kv.shape, lambda i: (0, 0)), pl.BlockSpec(memory_space=pl.ANY)],
        out_specs=[pl.BlockSpec((tq, XA_W), lambda i: (i, QC_CB)), pl.BlockSpec(kv.shape, lambda i: (0, 0))],
        out_shape=[jax.ShapeDtypeStruct(dproj.shape, dproj.dtype), jax.ShapeDtypeStruct(kv.shape, F32)],
        input_output_aliases={3: 0}, compiler_params=_cparams(("arbitrary",)),
    )(do, proj, kv, dproj)


def _adamw_math(wv, gv, mv, vv):
    m2 = ADAM_B1 * mv + (1.0 - ADAM_B1) * gv
    v2 = ADAM_B2 * vv + (1.0 - ADAM_B2) * (gv * gv)
    m_hat = m2 / (1.0 - ADAM_B1 ** ADAM_STEP)
    v_hat = v2 / (1.0 - ADAM_B2 ** ADAM_STEP)
    return -ADAM_LR * (m_hat / (jnp.sqrt(v_hat) + ADAM_EPS) + ADAM_WD * wv), m2, v2


def adamw(w, g, m, v, name):
    lead = (0,) * (w.ndim - 2)
    rows, cols = w.shape[-2:]
    tr = rows
    while tr * cols * 4 * 7 * 2 > 36 * 2 ** 20 and tr % 16 == 0:
        tr //= 2

    def body(w_ref, g_ref, m_ref, v_ref, d_ref, m2_ref, v2_ref):
        d_ref[...], m2_ref[...], v2_ref[...] = _adamw_math(w_ref[...], g_ref[...], m_ref[...], v_ref[...])

    spec = pl.BlockSpec((1,) * len(lead) + (tr, cols), lambda i: lead + (i, 0))
    return pl.pallas_call(
        body, name=name, grid=(rows // tr,), in_specs=[spec] * 4, out_specs=[spec] * 3,
        out_shape=[jax.ShapeDtypeStruct(w.shape, F32)] * 3, compiler_params=_cparams(("parallel",)),
    )(w, g, m, v)


def _seg_perm(a):
    s, w = a.shape
    return a.reshape(NSEG, s // NSEG, w).transpose(1, 0, 2).reshape(s, w)


def _seg_unperm(a):
    s, w = a.shape
    return a.reshape(s // NSEG, NSEG, w).transpose(1, 0, 2).reshape(s, w)


def _block_diag(t):
    nt, _, r, c = t.shape
    eye = jnp.eye(8, dtype=bool)
    return jnp.where(eye[None, :, None, :, None], t[:, :, :, None, :], 0.0).reshape(nt, 8 * r, 8 * c)


def _block_diag_inv(d, r, c):
    d5 = d.reshape(d.shape[0], 8, r, 8, c)
    return jnp.diagonal(d5, axis1=1, axis2=3).transpose(0, 3, 1, 2)


def _s5_b_tiles(bb):
    return _block_diag(bb.reshape(S5_TILES, 8, S5_STATE, S5_GROUP).transpose(0, 1, 3, 2))


def _s5_b_untile(d):
    return _block_diag_inv(d, S5_GROUP, S5_STATE).transpose(0, 1, 3, 2).reshape(S5_GROUPS, S5_STATE * S5_GROUP)


def _s5_c_tiles(c):
    return _block_diag(c.reshape(S5_TILES, 8, S5_GROUP, S5_STATE).transpose(0, 1, 3, 2))


def _s5_c_untile(d):
    return _block_diag_inv(d, S5_STATE, S5_GROUP).transpose(0, 1, 3, 2).reshape(S5_GROUPS, S5_GROUP, S5_STATE)


def s5_ssm_fwd(xb, lam_re, lam_im, log_dt, b_re, b_im, c_re, c_im):
    br, bi = b_re.reshape(S5_GROUPS, -1), b_im.reshape(S5_GROUPS, -1)
    ldt = log_dt.reshape(S5_GROUPS, 1)
    ab_re, ab_im, bb_re, bb_im = s5_param_fwd(lam_re, lam_im, ldt, br, bi)
    a_re, a_im = ab_re.reshape(1, S5_W), ab_im.reshape(1, S5_W)
    bre, bim = _s5_b_tiles(bb_re).astype(BF16), _s5_b_tiles(bb_im).astype(BF16)
    cre, cim = _s5_c_tiles(c_re).astype(BF16), _s5_c_tiles(c_im).astype(BF16)
    xp = _seg_perm(xb)
    yp, hs_r, hs_i = s5_scan_fwd(xp, a_re, a_im, bre, bim, cre, cim)
    saved = (xp, a_re, a_im, bre, bim, cre, cim, hs_r, hs_i, (lam_re, lam_im, ldt, br, bi))
    return _seg_unperm(yp), saved


def s5_ssm_bwd(dy, saved):
    xp, a_re, a_im, bre, bim, cre, cim, hs_r, hs_i, params = saved
    cre_t, cim_t = cre.transpose(0, 2, 1), cim.transpose(0, 2, 1)
    dxp, dar, dai, dbr, dbi, dcr, dci = s5_scan_bwd(_seg_perm(dy), xp, a_re, a_im, bre, bim, cre_t, cim_t, hs_r, hs_i)
    dlr, dli, dldt, db_re, db_im = s5_param_bwd(*params, dar.reshape(S5_GROUPS, S5_STATE), dai.reshape(S5_GROUPS, S5_STATE),
                                                _s5_b_untile(dbr), _s5_b_untile(dbi))
    shape_b = (S5_GROUPS, S5_STATE, S5_GROUP)
    return (_seg_unperm(dxp), dlr, dli, dldt.reshape(S5_GROUPS), db_re.reshape(shape_b), db_im.reshape(shape_b),
            _s5_c_untile(dcr), _s5_c_untile(dci))


_MESH = pl.DeviceIdType.MESH
_HBM = pl.BlockSpec(memory_space=pltpu.HBM)
N_DEV = 8


def _position():
    return lax.axis_index("x"), lax.axis_index("y"), lax.axis_index("c")


D2D_CHUNK_BYTES = 2 ** 20


def _chunk_rows(rows, cols, itemsize):
    return _row_tile(rows, 16, max(16, D2D_CHUNK_BYTES // (cols * itemsize)))


def _rows(start, size, unit=16):
    return pl.ds(pl.multiple_of(start, unit), size)


def _push_to_sibling(chunks, stages, recv_sems, store_sems, sibling, lag=2):
    in_slot, used, stores = {}, {}, []

    def push(q, slot):
        _, _, sid, land, _ = chunks[q]
        buf, send_sems, _ = stages[sid]
        return pltpu.make_async_remote_copy(src_ref=buf.at[slot], dst_ref=land, send_sem=send_sems.at[slot],
                                            recv_sem=recv_sems.at[q], device_id=sibling, device_id_type=_MESH)

    def receive(q):
        push(q, 0).wait_recv()
        st = pltpu.make_async_copy(chunks[q][3], chunks[q][4], store_sems.at[q])
        st.start()
        stores.append(st)

    for q, (pre, src, sid, _, _) in enumerate(chunks):
        if pre is not None:
            pre()
        slot = used.get(sid, 0) % 2
        used[sid] = used.get(sid, 0) + 1
        if (sid, slot) in in_slot:
            in_slot.pop((sid, slot)).wait_send()
        load = pltpu.make_async_copy(src, stages[sid][0].at[slot], stages[sid][2].at[slot])
        load.start()
        load.wait()
        cp = push(q, slot)
        cp.start()
        in_slot[(sid, slot)] = cp
        if q >= lag:
            receive(q - lag)
    for q in range(max(0, len(chunks) - lag), len(chunks)):
        receive(q)
    for cp in in_slot.values():
        cp.wait_send()
    for st in stores:
        st.wait()


def _stage_scratch(shapes_dtypes):
    out = []
    for shape, dtype in shapes_dtypes:
        out += [pltpu.VMEM((2,) + shape, dtype), pltpu.SemaphoreType.DMA((2,)), pltpu.SemaphoreType.DMA((2,))]
    return out


def allgather_weights(ws, convw, name):
    n = len(ws)
    extra = 0 if convw is None else 1
    halves = [w.shape[0] // 2 for w in ws]
    steps = [_chunk_rows(h, w.shape[1], w.dtype.itemsize) for h, w in zip(halves, ws)]
    per_peer = [h // s for h, s in zip(halves, steps)]
    nchunks = 3 * sum(per_peer)

    def body(*refs):
        w_refs = refs[:n]
        wo_refs = refs[n + extra:2 * n + extra]
        scratch = refs[2 * (n + extra):]
        send_sems, recv_sems, local_sems, fwd_recv_sems, store_sems = scratch[:5]
        lands = scratch[5:5 + n]
        stage_refs = scratch[5 + n:]
        stages = [tuple(stage_refs[3 * i:3 * i + 3]) for i in range(n)]
        x, y, c = _position()
        mine = 2 * x + y
        peers = [(1 - x, y), (x, 1 - y), (1 - x, 1 - y)]
        blocks = [2 * px + py for px, py in peers]
        local = [pltpu.make_async_copy(w_refs[i], wo_refs[i].at[mine], local_sems.at[i]) for i in range(n)]
        if extra:
            c_ref, co_ref = refs[n], refs[2 * n + 1]
            local.append(pltpu.make_async_copy(c_ref, co_ref.at[mine], local_sems.at[n]))
        for cp in local:
            cp.start()

        def ici(i, k, block):
            rows = _rows(c * halves[i], halves[i])
            return pltpu.make_async_remote_copy(src_ref=w_refs[i].at[rows, :], dst_ref=wo_refs[i].at[block, rows, :],
                                                send_sem=send_sems.at[3 * i + k], recv_sem=recv_sems.at[3 * i + k],
                                                device_id=(*peers[k], c), device_id_type=_MESH)

        def conv(k, block):
            return pltpu.make_async_remote_copy(src_ref=c_ref, dst_ref=co_ref.at[block], send_sem=send_sems.at[3 * n + k],
                                                recv_sem=recv_sems.at[3 * n + k], device_id=(*peers[k], c), device_id_type=_MESH)

        sends = [ici(i, k, mine) for k in range(3) for i in range(n)] + ([conv(k, mine) for k in range(3)] if extra else [])
        for cp in sends:
            cp.start()
        chunks = []
        for k in range(3):
            for i in range(n):
                for q in range(per_peer[i]):
                    pre = functools.partial(lambda i, k: ici(i, k, blocks[k]).wait_recv(), i, k) if q == 0 else None
                    src = wo_refs[i].at[blocks[k], _rows(c * halves[i] + q * steps[i], steps[i]), :]
                    out = wo_refs[i].at[blocks[k], _rows((1 - c) * halves[i] + q * steps[i], steps[i]), :]
                    chunks.append((pre, src, i, lands[i].at[k * per_peer[i] + q], out))
        _push_to_sibling(chunks, stages, fwd_recv_sems, store_sems, (x, y, 1 - c))
        if extra:
            for k in range(3):
                conv(k, blocks[k]).wait_recv()
        for cp in sends:
            cp.wait_send()
        for cp in local:
            cp.wait()

    nsem = 3 * (n + extra)
    scratch = [pltpu.SemaphoreType.DMA((nsem,)), pltpu.SemaphoreType.DMA((nsem,)), pltpu.SemaphoreType.DMA((n + extra,)),
               pltpu.SemaphoreType.DMA((nchunks,)), pltpu.SemaphoreType.DMA((nchunks,))]
    scratch += [pltpu.VMEM((3 * p, s, w.shape[1]), w.dtype) for p, s, w in zip(per_peer, steps, ws)]
    scratch += _stage_scratch([((s, w.shape[1]), w.dtype) for s, w in zip(steps, ws)])
    operands = list(ws) + ([convw] if extra else [])
    return pl.pallas_call(
        body, name=name, in_specs=[_HBM] * len(operands), out_specs=[_HBM] * len(operands),
        out_shape=[jax.ShapeDtypeStruct((4,) + w.shape, w.dtype) for w in operands],
        scratch_shapes=scratch, compiler_params=pltpu.CompilerParams(vmem_limit_bytes=VMEM_LIMIT),
    )(*operands)


def exchange_cores(gs, name):
    n = len(gs)
    halves = [g.shape[1] // 2 for g in gs]
    steps = [_chunk_rows(h, g.shape[2], g.dtype.itemsize) for h, g in zip(halves, gs)]
    per_shard = [h // s for h, s in zip(halves, steps)]
    nchunks = 4 * sum(per_shard)

    def body(*refs):
        g_refs, got_refs, scratch = refs[:n], refs[n:2 * n], refs[2 * n:]
        recv_sems, store_sems = scratch[:2]
        lands = scratch[2:2 + n]
        stage_refs = scratch[2 + n:2 + 4 * n]
        stages = [tuple(stage_refs[3 * i:3 * i + 3]) for i in range(n)]
        x, y, c = _position()
        chunks = []
        for i in range(n):
            for j in range(4):
                for q in range(per_shard[i]):
                    src = g_refs[i].at[j, _rows((1 - c) * halves[i] + q * steps[i], steps[i]), :]
                    out = got_refs[i].at[j, pl.ds(q * steps[i], steps[i]), :]
                    chunks.append((None, src, i, lands[i].at[j * per_shard[i] + q], out))
        _push_to_sibling(chunks, stages, recv_sems, store_sems, (x, y, 1 - c))

    scratch = [pltpu.SemaphoreType.DMA((nchunks,)), pltpu.SemaphoreType.DMA((nchunks,))]
    scratch += [pltpu.VMEM((4 * p, s, g.shape[2]), g.dtype) for p, s, g in zip(per_shard, steps, gs)]
    scratch += _stage_scratch([((s, g.shape[2]), g.dtype) for s, g in zip(steps, gs)])
    return pl.pallas_call(
        body, name=name, in_specs=[_HBM] * n, out_specs=[_HBM] * n,
        out_shape=[jax.ShapeDtypeStruct((4, h, g.shape[2]), g.dtype) for h, g in zip(halves, gs)],
        scratch_shapes=scratch, compiler_params=pltpu.CompilerParams(vmem_limit_bytes=VMEM_LIMIT),
    )(*gs)


def chips_side(cs):
    n = len(cs)

    def copies(c_refs, o_refs, sems):
        send_sems, recv_sems, local_sems = sems
        x, y, c = _position()
        mine = 2 * x + y
        peers = [(1 - x, y), (x, 1 - y), (1 - x, 1 - y)]
        blocks = [2 * px + py for px, py in peers]
        local = [pltpu.make_async_copy(c_refs[i].at[mine], o_refs[i].at[mine], local_sems.at[i]) for i in range(n)]

        def copy(i, k, sending):
            return pltpu.make_async_remote_copy(src_ref=c_refs[i].at[blocks[k]], dst_ref=o_refs[i].at[mine if sending else blocks[k]],
                                                send_sem=send_sems.at[3 * i + k], recv_sem=recv_sems.at[3 * i + k],
                                                device_id=(*peers[k], c), device_id_type=_MESH)

        sends = [copy(i, k, True) for k in range(3) for i in range(n)]
        return local, sends, lambda: [copy(i, k, False) for k in range(3) for i in range(n)]

    def start(*refs):
        local, sends, _ = copies(*refs)
        for cp in local + sends:
            cp.start()

    def finish(*refs):
        local, sends, arrivals = copies(*refs)
        for cp in arrivals():
            cp.wait_recv()
        for cp in sends:
            cp.wait_send()
        for cp in local:
            cp.wait()

    scratch = [pltpu.SemaphoreType.DMA((3 * n,)), pltpu.SemaphoreType.DMA((3 * n,)), pltpu.SemaphoreType.DMA((n,))]
    return Side(list(cs), [jax.ShapeDtypeStruct(a.shape, a.dtype) for a in cs], scratch, start, finish)


def gather_side(ws, convw):
    n = len(ws)
    halves = [w.shape[0] // 2 for w in ws]

    def copies(in_refs, out_refs, sems):
        w_refs, c_ref, wo_refs, co_ref = in_refs[:n], in_refs[n], out_refs[:n], out_refs[n]
        send_sems, recv_sems, local_sems = sems
        x, y, c = _position()
        mine = 2 * x + y
        peers = [(1 - x, y), (x, 1 - y), (1 - x, 1 - y)]
        blocks = [2 * px + py for px, py in peers]
        local = [pltpu.make_async_copy(w_refs[i], wo_refs[i].at[mine], local_sems.at[i]) for i in range(n)]
        local.append(pltpu.make_async_copy(c_ref, co_ref.at[mine], local_sems.at[n]))

        def ici(i, k, block):
            rows = _rows(c * halves[i], halves[i])
            return pltpu.make_async_remote_copy(src_ref=w_refs[i].at[rows, :], dst_ref=wo_refs[i].at[block, rows, :],
                                                send_sem=send_sems.at[3 * i + k], recv_sem=recv_sems.at[3 * i + k],
                                                device_id=(*peers[k], c), device_id_type=_MESH)

        def conv(k, block):
            return pltpu.make_async_remote_copy(src_ref=c_ref, dst_ref=co_ref.at[block], send_sem=send_sems.at[3 * n + k],
                                                recv_sem=recv_sems.at[3 * n + k], device_id=(*peers[k], c), device_id_type=_MESH)

        sends = [ici(i, k, mine) for k in range(3) for i in range(n)] + [conv(k, mine) for k in range(3)]
        return local, sends, lambda: ([ici(i, k, blocks[k]) for k in range(3) for i in range(n)]
                                      + [conv(k, blocks[k]) for k in range(3)])

    def start(*refs):
        local, sends, _ = copies(*refs)
        for cp in local + sends:
            cp.start()

    def finish(*refs):
        local, sends, arrivals = copies(*refs)
        for cp in arrivals():
            cp.wait_recv()
        for cp in sends:
            cp.wait_send()
        for cp in local:
            cp.wait()

    nsem = 3 * n + 3
    scratch = [pltpu.SemaphoreType.DMA((nsem,)), pltpu.SemaphoreType.DMA((nsem,)), pltpu.SemaphoreType.DMA((n + 1,))]
    operands = list(ws) + [convw]
    return Side(operands, [jax.ShapeDtypeStruct((4,) + w.shape, w.dtype) for w in operands], scratch, start, finish)


def forward_halves(stacked):
    n = len(stacked)
    halves = [w.shape[1] // 2 for w in stacked]
    steps = [_chunk_rows(h, w.shape[2], w.dtype.itemsize) for h, w in zip(halves, stacked)]
    per_peer = [h // s for h, s in zip(halves, steps)]
    nchunks = 3 * sum(per_peer)

    def body(*refs):
        w_refs, o_refs = refs[:n], refs[n:2 * n]
        scratch = refs[2 * n:]
        recv_sems, store_sems = scratch[:2]
        lands = scratch[2:2 + n]
        stages = [tuple(scratch[2 + n + 3 * i:2 + n + 3 * i + 3]) for i in range(n)]
        x, y, c = _position()
        blocks = [2 * px + py for px, py in ((1 - x, y), (x, 1 - y), (1 - x, 1 - y))]
        chunks = []
        for k in range(3):
            for i in range(n):
                for q in range(per_peer[i]):
                    src = w_refs[i].at[blocks[k], _rows(c * halves[i] + q * steps[i], steps[i]), :]
                    out = o_refs[i].at[blocks[k], _rows((1 - c) * halves[i] + q * steps[i], steps[i]), :]
                    chunks.append((None, src, i, lands[i].at[k * per_peer[i] + q], out))
        _push_to_sibling(chunks, stages, recv_sems, store_sems, (x, y, 1 - c))

    scratch = [pltpu.SemaphoreType.DMA((nchunks,)), pltpu.SemaphoreType.DMA((nchunks,))]
    scratch += [pltpu.VMEM((3 * p, s, w.shape[2]), w.dtype) for p, s, w in zip(per_peer, steps, stacked)]
    scratch += _stage_scratch([((s, w.shape[2]), w.dtype) for s, w in zip(steps, stacked)])
    return pl.pallas_call(
        body, name="forward_halves", in_specs=[_HBM] * n, out_specs=[_HBM] * n,
        out_shape=[jax.ShapeDtypeStruct(w.shape, w.dtype) for w in stacked], input_output_aliases={i: i for i in range(n)},
        scratch_shapes=scratch, compiler_params=pltpu.CompilerParams(vmem_limit_bytes=VMEM_LIMIT),
    )(*stacked)


def small_side(small):
    def copies(in_refs, out_refs, sems):
        (s_ref,), (so_ref,), (send_sems, recv_sems, local_sem) = in_refs, out_refs, sems
        x, y, c = _position()
        me = 4 * x + 2 * y + c
        local = pltpu.make_async_copy(s_ref, so_ref.at[me], local_sem)

        def copy(r, sending):
            px, py, pc = (1 - x if r & 4 else x, 1 - y if r & 2 else y, 1 - c if r & 1 else c)
            slot = me if sending else 4 * px + 2 * py + pc
            return pltpu.make_async_remote_copy(src_ref=s_ref, dst_ref=so_ref.at[slot], send_sem=send_sems.at[r - 1],
                                                recv_sem=recv_sems.at[r - 1], device_id=(px, py, pc), device_id_type=_MESH)

        return local, [copy(r, True) for r in range(1, N_DEV)], lambda: [copy(r, False) for r in range(1, N_DEV)]

    def start(*refs):
        local, sends, _ = copies(*refs)
        for cp in [local] + sends:
            cp.start()

    def finish(*refs):
        local, sends, arrivals = copies(*refs)
        for cp in arrivals():
            cp.wait_recv()
        for cp in sends:
            cp.wait_send()
        local.wait()

    scratch = [pltpu.SemaphoreType.DMA((N_DEV - 1,)), pltpu.SemaphoreType.DMA((N_DEV - 1,)), pltpu.SemaphoreType.DMA]
    return Side([small], [jax.ShapeDtypeStruct((N_DEV,) + small.shape, small.dtype)], scratch, start, finish)


def combine_sides(a, b):
    na, oa, sa = len(a.operands), len(a.out_shapes), len(a.scratch)

    def split(ins, outs, scr):
        return (ins[:na], outs[:oa], scr[:sa]), (ins[na:], outs[oa:], scr[sa:])

    def start(*refs):
        ra, rb = split(*refs)
        a.start(*ra)
        b.start(*rb)

    def finish(*refs):
        ra, rb = split(*refs)
        a.finish(*ra)
        b.finish(*rb)

    return Side(a.operands + b.operands, a.out_shapes + b.out_shapes, a.scratch + b.scratch, start, finish)


def exchange_small(small):
    side = small_side(small)

    def body(s_ref, so_ref, *sems):
        side.start((s_ref,), (so_ref,), sems)
        side.finish((s_ref,), (so_ref,), sems)

    return pl.pallas_call(
        body, name="exchange_small", in_specs=[_HBM], out_specs=_HBM, out_shape=side.out_shapes[0], scratch_shapes=side.scratch,
    )(small)


def pair_sum(core, g, got, name):
    nb, rows, cols = got.shape
    tr = _row_tile(rows, 16, max(16, (2 * 2 ** 20) // (cols * g.dtype.itemsize)))
    nblk = rows // tr

    def body(c_ref, a_ref, b_ref, o_ref):
        o_ref[...] = (a_ref[...].astype(F32) + b_ref[...].astype(F32)).astype(o_ref.dtype)

    spec = pl.BlockSpec((1, tr, cols), lambda j, i, c_ref: (j, i, 0))
    mine = pl.BlockSpec((1, tr, cols), lambda j, i, c_ref: (j, c_ref[0] * nblk + i, 0))
    return pl.pallas_call(
        body, name=name,
        grid_spec=pltpu.PrefetchScalarGridSpec(num_scalar_prefetch=1, grid=(nb, nblk), in_specs=[mine, spec], out_specs=spec),
        out_shape=jax.ShapeDtypeStruct(got.shape, g.dtype), compiler_params=_cparams(("parallel", "parallel")),
    )(core, g, got)


def sum_chips(core, pieces, name):
    nb, rows, cols = pieces.shape
    tr = _row_tile(rows, 16, max(16, (6 * 2 ** 20) // (nb * cols * pieces.dtype.itemsize)))
    nblk = rows // tr

    def body(c_ref, p_ref, o_ref):
        acc = p_ref[0].astype(F32)
        for i in range(1, nb):
            acc = acc + p_ref[i].astype(F32)
        o_ref[0] = acc

    return pl.pallas_call(
        body, name=name,
        grid_spec=pltpu.PrefetchScalarGridSpec(
            num_scalar_prefetch=1, grid=(nblk,),
            in_specs=[pl.BlockSpec((nb, tr, cols), lambda i, c_ref: (0, i, 0))],
            out_specs=pl.BlockSpec((1, tr, cols), lambda i, c_ref: (0, c_ref[0] * nblk + i, 0))),
        out_shape=jax.ShapeDtypeStruct((1, 2 * rows, cols), F32), compiler_params=_cparams(("parallel",)),
    )(core, pieces)


def sibling_exchange(fulls):
    n = len(fulls)
    halves = [f.shape[1] // 2 for f in fulls]
    steps = [_chunk_rows(h, f.shape[2], f.dtype.itemsize) for h, f in zip(halves, fulls)]
    counts = [h // s for h, s in zip(halves, steps)]
    nchunks = sum(counts)

    def body(*refs):
        f_refs, o_refs = refs[:n], refs[n:2 * n]
        scratch = refs[2 * n:]
        recv_sems, store_sems = scratch[:2]
        lands = scratch[2:2 + n]
        stages = [tuple(scratch[2 + n + 3 * i:2 + n + 3 * i + 3]) for i in range(n)]
        x, y, c = _position()
        chunks = []
        for i in range(n):
            for q in range(counts[i]):
                src = f_refs[i].at[0, _rows(c * halves[i] + q * steps[i], steps[i]), :]
                out = o_refs[i].at[0, _rows((1 - c) * halves[i] + q * steps[i], steps[i]), :]
                chunks.append((None, src, i, lands[i].at[q], out))
        _push_to_sibling(chunks, stages, recv_sems, store_sems, (x, y, 1 - c))

    scratch = [pltpu.SemaphoreType.DMA((nchunks,)), pltpu.SemaphoreType.DMA((nchunks,))]
    scratch += [pltpu.VMEM((k, s, f.shape[2]), f.dtype) for k, s, f in zip(counts, steps, fulls)]
    scratch += _stage_scratch([((s, f.shape[2]), f.dtype) for s, f in zip(steps, fulls)])
    return pl.pallas_call(
        body, name="sibling_exchange", in_specs=[_HBM] * n, out_specs=[_HBM] * n,
        out_shape=[jax.ShapeDtypeStruct(f.shape, f.dtype) for f in fulls],
        input_output_aliases={i: i for i in range(n)},
        scratch_shapes=scratch, compiler_params=pltpu.CompilerParams(vmem_limit_bytes=VMEM_LIMIT),
    )(*fulls)


def _row_tile(rows, unit, max_rows):
    best = unit
    for t in range(unit, min(rows, max_rows) + 1, unit):
        if rows % t == 0:
            best = t
    return best


def sum_pieces(pieces, name):
    n, rows, cols = pieces.shape
    tr = _row_tile(rows, 16, max(16, (6 * 2 ** 20) // (n * cols * pieces.dtype.itemsize)))

    def body(p_ref, o_ref):
        acc = p_ref[0].astype(F32)
        for i in range(1, n):
            acc = acc + p_ref[i].astype(F32)
        o_ref[...] = acc

    return pl.pallas_call(
        body, name=name, grid=(rows // tr,),
        in_specs=[pl.BlockSpec((n, tr, cols), lambda i: (0, i, 0))], out_specs=pl.BlockSpec((tr, cols), lambda i: (i, 0)),
        out_shape=jax.ShapeDtypeStruct((rows, cols), F32), compiler_params=_cparams(("parallel",)),
    )(pieces)


BIG = ("w_in", "s5_w_glu", "w_kv_mem", "w_br_a", "w_br_b", "w_br_c", "w_out")
COL_SHARDED = ("w_in", "s5_w_glu", "w_br_a", "w_br_b", "w_br_c")
SMALL = ("gdn_a_log", "gdn_dt_bias", "gdn_norm_g", "s5_lambda_re", "s5_lambda_im", "s5_log_dt",
         "s5_b_re", "s5_b_im", "s5_c_re", "s5_c_im", "s5_d", "mem_norm_g", "final_g", "norm_g")
WEIGHTS = ("norm_g", "w_in", "conv_w", "gdn_a_log", "gdn_dt_bias", "gdn_norm_g", "s5_lambda_re", "s5_lambda_im",
           "s5_log_dt", "s5_b_re", "s5_b_im", "s5_c_re", "s5_c_im", "s5_d", "s5_w_glu", "mem_norm_g", "w_kv_mem",
           "w_br_a", "w_br_b", "w_br_c", "w_out", "final_g")
W_IN_SPLIT = 4096


W_IN_COLS = PROJ_W - BA_PAD + 2 * NHEAD
W_IN_GATES = W_IN_COLS - GATE_W
W_IN_MOVES = ((0, W_IN_SPLIT, GATE_W), (W_IN_SPLIT, W_IN_SPLIT + 2 * NHEAD, PROJ_W - BA_PAD - W_IN_SPLIT),
              (W_IN_SPLIT + 2 * NHEAD, W_IN_GATES, GATE_W - 2 * NHEAD), (W_IN_GATES, W_IN_COLS, -W_IN_GATES))


def _pack_w_in(shards):
    cs = shards.shape[2]
    parts = []
    for a, b, _ in sorted(W_IN_MOVES, key=lambda move: move[0] + move[2]):
        while a < b:
            j = a // cs
            hi = min(b, (j + 1) * cs)
            parts.append(shards[j, :, a - j * cs:hi - j * cs])
            a = hi
    parts.append(jnp.zeros((shards.shape[1], BA_PAD - 2 * NHEAD), shards.dtype))
    return jnp.concatenate(parts, axis=1)


def _unpack_w_in(wp):
    cs = W_IN_COLS // 4
    shards = []
    for j in range(4):
        parts = []
        for lo, hi, shift in W_IN_MOVES:
            s, e = max(j * cs, lo), min((j + 1) * cs, hi)
            if s < e:
                parts.append(wp[:, s + shift:e + shift])
        shards.append(jnp.concatenate(parts, axis=1))
    return jnp.stack(shards)


def _pack_small(arrs):
    parts = []
    for a in arrs:
        f = a.reshape(-1).astype(F32)
        parts.append(jnp.pad(f, (0, (-f.shape[0]) % 128)))
    flat = jnp.concatenate(parts)
    rows = flat.shape[0] // 128
    return jnp.pad(flat.reshape(rows, 128), ((0, (-rows) % 16), (0, 0)))


def _unpack_small(flat2d, shapes):
    f = flat2d.reshape(-1)
    out, off = [], 0
    for shp in shapes:
        n = math.prod(shp)
        out.append(f[off:off + n].reshape(shp))
        off += n + (-n) % 128
    return out


def kernel(x, mem, norm_g, w_in, conv_w, gdn_a_log, gdn_dt_bias, gdn_norm_g, s5_lambda_re, s5_lambda_im, s5_log_dt, s5_b_re, s5_b_im, s5_c_re, s5_c_im, s5_d, s5_w_glu, mem_norm_g, w_kv_mem, w_br_a, w_br_b, w_br_c, w_out, final_g, loss_target, m_norm_g, m_w_in, m_conv_w, m_gdn_a_log, m_gdn_dt_bias, m_gdn_norm_g, m_s5_lambda_re, m_s5_lambda_im, m_s5_log_dt, m_s5_b_re, m_s5_b_im, m_s5_c_re, m_s5_c_im, m_s5_d, m_s5_w_glu, m_mem_norm_g, m_w_kv_mem, m_w_br_a, m_w_br_b, m_w_br_c, m_w_out, m_final_g, v_norm_g, v_w_in, v_conv_w, v_gdn_a_log, v_gdn_dt_bias, v_gdn_norm_g, v_s5_lambda_re, v_s5_lambda_im, v_s5_log_dt, v_s5_b_re, v_s5_b_im, v_s5_c_re, v_s5_c_im, v_s5_d, v_s5_w_glu, v_mem_norm_g, v_w_kv_mem, v_w_br_a, v_w_br_b, v_w_br_c, v_w_out, v_final_g):
    wts = dict(norm_g=norm_g, w_in=w_in, conv_w=conv_w, gdn_a_log=gdn_a_log, gdn_dt_bias=gdn_dt_bias, gdn_norm_g=gdn_norm_g,
               s5_lambda_re=s5_lambda_re, s5_lambda_im=s5_lambda_im, s5_log_dt=s5_log_dt, s5_b_re=s5_b_re, s5_b_im=s5_b_im,
               s5_c_re=s5_c_re, s5_c_im=s5_c_im, s5_d=s5_d, s5_w_glu=s5_w_glu, mem_norm_g=mem_norm_g, w_kv_mem=w_kv_mem,
               w_br_a=w_br_a, w_br_b=w_br_b, w_br_c=w_br_c, w_out=w_out, final_g=final_g)
    mom = dict(norm_g=m_norm_g, w_in=m_w_in, conv_w=m_conv_w, gdn_a_log=m_gdn_a_log, gdn_dt_bias=m_gdn_dt_bias,
               gdn_norm_g=m_gdn_norm_g, s5_lambda_re=m_s5_lambda_re, s5_lambda_im=m_s5_lambda_im, s5_log_dt=m_s5_log_dt,
               s5_b_re=m_s5_b_re, s5_b_im=m_s5_b_im, s5_c_re=m_s5_c_re, s5_c_im=m_s5_c_im, s5_d=m_s5_d, s5_w_glu=m_s5_w_glu,
               mem_norm_g=m_mem_norm_g, w_kv_mem=m_w_kv_mem, w_br_a=m_w_br_a, w_br_b=m_w_br_b, w_br_c=m_w_br_c, w_out=m_w_out,
               final_g=m_final_g)
    vel = dict(norm_g=v_norm_g, w_in=v_w_in, conv_w=v_conv_w, gdn_a_log=v_gdn_a_log, gdn_dt_bias=v_gdn_dt_bias,
               gdn_norm_g=v_gdn_norm_g, s5_lambda_re=v_s5_lambda_re, s5_lambda_im=v_s5_lambda_im, s5_log_dt=v_s5_log_dt,
               s5_b_re=v_s5_b_re, s5_b_im=v_s5_b_im, s5_c_re=v_s5_c_re, s5_c_im=v_s5_c_im, s5_d=v_s5_d, s5_w_glu=v_s5_w_glu,
               mem_norm_g=v_mem_norm_g, w_kv_mem=v_w_kv_mem, w_br_a=v_w_br_a, w_br_b=v_w_br_b, w_br_c=v_w_br_c, w_out=v_w_out,
               final_g=v_final_g)
    x2, mem2, tgt = x[0], mem[0], loss_target[0]
    s, d = x2.shape
    n_chunks = s // CHUNK

    shards = [wts[n][0].astype(BF16) for n in BIG]
    wp = _pack_w_in(allgather_weights(shards[:1], None, "allgather_w_in")[0])
    mm = functools.partial(matmul, tm=1024, tn=1024)
    u, r1 = rms_fwd(x2, norm_g, "rms_fwd_x")
    proj, *rest, cg = mm(u, wp, mode="nn", out_dtype=F32, tk=2048, name="mm_proj", side=gather_side(shards[1:], conv_w[0]))
    full = {}
    for n, wg in zip(BIG[1:], forward_halves(rest)):
        rows, cols = wg.shape[1:]
        full[n] = wg.transpose(1, 0, 2).reshape(rows, 4 * cols) if n in COL_SHARDED else wg.reshape(4 * rows, cols)
    conv_full = cg.transpose(1, 0, 2).reshape(conv_w.shape[1], -1)
    alog_pad = jnp.pad(gdn_a_log, ((0, 0), (NHEAD, BA_W - 2 * NHEAD)))
    dt_pad = jnp.pad(gdn_dt_bias, ((0, 0), (NHEAD, BA_W - 2 * NHEAD)))

    q, k, v, bg, gcol, gt = gdn_prep_fwd(proj, conv_full, alog_pad, dt_pad)
    gt3 = gt.reshape(BA_W, n_chunks, CHUNK).transpose(1, 0, 2)
    gu, gw, qd, kd, qk, tinv = gdn_intra_fwd(q, k, v, bg, gcol, gt3)
    o_raw, states = gdn_seq_fwd(gu, gw, qd, kd, qk, gt3)
    ga = gdn_out_fwd(o_raw, proj, ZA_CB, gdn_norm_g)

    xb = proj[:, XB_CB * S5_IN:(XB_CB + 1) * S5_IN]
    y_ssm, s5_saved = s5_ssm_fwd(xb, s5_lambda_re[0], s5_lambda_im[0], s5_log_dt[0], s5_b_re[0], s5_b_im[0],
                                 s5_c_re[0], s5_c_im[0])
    yb = s5_act_fwd(y_ssm, proj, XB_CB, s5_d)
    glu = mm(yb, full["s5_w_glu"], mode="nn", out_dtype=BF16, tk=1024, name="mm_glu")
    gb = s5_glu_fwd(glu, proj, ZB_CB)

    mem_n, rm = rms_fwd(mem2, mem_norm_g, "rms_fwd_mem")
    kv = mm(mem_n, full["w_kv_mem"], mode="nn", out_dtype=BF16, tk=2048, name="mm_kv")
    o_c = xa_fwd(proj, kv)
    gcx = gate_fwd(o_c, proj, ZC_CB, "gate_fwd_c")

    pa = mm(ga, full["w_br_a"], mode="nn", out_dtype=BF16, tk=1024, name="mm_pa")
    pb = mm(gb, full["w_br_b"], mode="nn", out_dtype=BF16, tk=1024, name="mm_pb")
    pc = mm(gcx, full["w_br_c"], mode="nn", out_dtype=BF16, tk=1024, name="mm_pc")
    merged = merge_fwd(pa, pb, pc, proj, GATE_CB)
    hres = mm(merged, full["w_out"], mode="nn", out_dtype=F32, tk=2048, name="mm_out")
    dh, dhb, loss_part, d_final_g = final_stage(x2, hres, tgt, final_g.reshape(1, d))

    gfull = {}
    dmerged = mm(dhb, full["w_out"], mode="nt", out_dtype=BF16, tk=2048, name="mm_dmerged")
    gfull["w_out"] = mm(merged, dhb, mode="tn", out_dtype=BF16, tk=1024, name="mm_dw_out")
    dproj = lax.empty((s, PROJ_W), BF16)
    dpa, dpb, dpc, dproj = merge_bwd(dmerged, pa, pb, pc, proj, GATE_CB, dproj)
    dga = mm(dpa, full["w_br_a"], mode="nt", out_dtype=BF16, tk=2048, name="mm_dga")
    dgb = mm(dpb, full["w_br_b"], mode="nt", out_dtype=BF16, tk=2048, name="mm_dgb")
    dgc = mm(dpc, full["w_br_c"], mode="nt", out_dtype=BF16, tk=2048, name="mm_dgc")
    gfull["w_br_a"] = mm(ga, dpa, mode="tn", out_dtype=BF16, tk=1024, name="mm_dw_a")
    gfull["w_br_b"] = mm(gb, dpb, mode="tn", out_dtype=BF16, tk=1024, name="mm_dw_b")
    gfull["w_br_c"] = mm(gcx, dpc, mode="tn", out_dtype=BF16, tk=1024, name="mm_dw_c")

    do_raw, dproj, d_gdn_norm = gdn_out_bwd(dga, o_raw, proj, ZA_CB, gdn_norm_g, dproj)
    du_, dw_, dqd, dkd, dqk, dgl = gdn_seq_bwd(do_raw, gu, gw, qd, kd, qk, gt3, states)
    dq, dk, dv, dbg = gdn_intra_bwd(q, k, v, bg, gcol, gt3, tinv, du_, dw_, dqd, dkd, dqk, dgl)
    dc, dproj, dcw0, dcw1, dcw2, dcw3, d_alog, d_dt = gdn_prep_bwd1(proj, conv_full, alog_pad, dt_pad, dq, dk, dv, dbg, dproj)
    dproj = gdn_prep_bwd2(dc, conv_full, dproj)
    d_conv = jnp.concatenate([dcw0, dcw1, dcw2, dcw3], axis=0)

    dval, dgate, dproj = s5_glu_bwd(dgb, glu, proj, ZB_CB, dproj)
    dglu = jnp.concatenate([dval, dgate], axis=1)
    dyb = mm(dglu, full["s5_w_glu"], mode="nt", out_dtype=BF16, tk=2048, name="mm_dyb")
    gfull["s5_w_glu"] = mm(yb, dglu, mode="tn", out_dtype=BF16, tk=1024, name="mm_dw_glu")
    dy_ssm, dxb_direct, d_s5_d = s5_act_bwd(dyb, y_ssm, proj, XB_CB, s5_d)
    dxb_scan, d_lre, d_lim, d_ldt, d_bre, d_bim, d_cre, d_cim = s5_ssm_bwd(dy_ssm, s5_saved)
    dproj = add_into(dxb_direct, dxb_scan, "s5_dxb", dproj, XB_CB)

    do_c, dproj = gate_bwd(dgc, o_c, proj, ZC_CB, "gate_bwd_c", dproj)
    dproj, dkv = xa_bwd(do_c, proj, kv, dproj)
    gfull["w_kv_mem"] = mm(mem_n, dkv, mode="tn", out_dtype=BF16, tk=256, name="mm_dw_kv")
    dmem_n = mm(dkv, full["w_kv_mem"], mode="nt", out_dtype=F32, tk=2048, name="mm_dmem")
    d_mem_norm = rms_bwd_g(dmem_n, mem2, rm, "rms_bwd_mem")

    core = lax.axis_index("c").astype(jnp.int32).reshape(1)
    by_shard = []
    for n in BIG[1:]:
        rows, cols = wts[n].shape[1:]
        g = gfull[n]
        by_shard.append(g.reshape(rows, 4, cols).transpose(1, 0, 2) if n in COL_SHARDED else g.reshape(4, rows, cols))
    got_rest = exchange_cores(by_shard, "exchange_cores_rest")
    chip_rest = [pair_sum(core, g, r, "sum_cores_" + n) for n, g, r in zip(BIG[1:], by_shard, got_rest)]

    small_g = dict(gdn_a_log=d_alog[:, NHEAD:2 * NHEAD], gdn_dt_bias=d_dt[:, NHEAD:2 * NHEAD], gdn_norm_g=d_gdn_norm,
                   s5_lambda_re=d_lre, s5_lambda_im=d_lim, s5_log_dt=d_ldt, s5_b_re=d_bre, s5_b_im=d_bim,
                   s5_c_re=d_cre, s5_c_im=d_cim, s5_d=d_s5_d, mem_norm_g=d_mem_norm, final_g=d_final_g)
    early = SMALL[:-1]
    small_send = _pack_small([small_g[n] for n in early] + [d_conv, loss_part])
    dwp, *from_chips_rest, got_small = matmul(u.T, dproj, mode="nn", out_dtype=BF16, tm=2048, tn=1024, tk=1024, name="mm_dw_in",
                                              side=combine_sides(chips_side(chip_rest), small_side(small_send)))
    w_in_shards = _unpack_w_in(dwp)
    got_in, = exchange_cores([w_in_shards], "exchange_cores_w_in")
    chip_in = pair_sum(core, w_in_shards, got_in, "sum_cores_w_in")
    du, from_chips_in = matmul(dproj, wp, mode="nt", out_dtype=F32, tm=2048, tn=1024, tk=512, name="mm_du",
                               side=chips_side([chip_in]))
    grad_x, d_norm_g = rms_bwd_x(du, x2, r1, norm_g, dh)
    from_chips = [from_chips_in] + from_chips_rest
    fulls = [sum_chips(core, a, "sum_chips_" + n) for n, a in zip(BIG, from_chips)]
    small_sum = sum_pieces(got_small, "sum_small")
    norm_sum = sum_pieces(exchange_small(_pack_small([d_norm_g])), "sum_norm_g")
    grads = dict(zip(BIG, sibling_exchange(fulls)))
    small_shapes = [wts[n].shape for n in early] + [d_conv.shape, (1, 1)]
    *small_list, conv_g_full, loss_sum = _unpack_small(small_sum, small_shapes)
    grads.update(zip(early, small_list))
    grads["norm_g"], = _unpack_small(norm_sum, [norm_g.shape])
    cw = conv_w.shape[2]
    shard_idx = 2 * lax.axis_index("x") + lax.axis_index("y")
    grads["conv_w"] = lax.dynamic_slice(conv_g_full, (0, shard_idx * cw), (conv_w.shape[1], cw))[None]

    delta, new_m, new_v = {}, {}, {}
    for n in BIG + ("conv_w",):
        delta[n], new_m[n], new_v[n] = adamw(wts[n], grads[n], mom[n], vel[n], "adamw_" + n)
    res = adamw(*[_pack_small([src[n] for n in SMALL]) for src in (wts, grads, mom, vel)], "adamw_small")
    shapes = [wts[n].shape for n in SMALL]
    for dst, flat in zip((delta, new_m, new_v), res):
        dst.update(zip(SMALL, _unpack_small(flat, shapes)))
    for n in SMALL:
        grads[n] = grads[n].reshape(wts[n].shape)

    return (loss_sum.reshape(()), grad_x.reshape(x.shape), *[grads[n] for n in WEIGHTS], *[delta[n] for n in WEIGHTS],
            *[new_m[n] for n in WEIGHTS], *[new_v[n] for n in WEIGHTS])
```

```python
import functools
import math

import jax
import jax.numpy as jnp
from jax import lax
from jax.experimental import pallas as pl
from jax.experimental.pallas import tpu as pltpu

F32 = jnp.float32
BF16 = jnp.bfloat16
HI = lax.Precision.HIGHEST

EPS = 1e-6
CHUNK = 64
HEAD = 128
NHEAD = 8
XA_HEADS = 4
S5_GROUPS = 64
S5_STATE = 64
S5_GROUP = 16
NSEG = 8
ADAM_LR, ADAM_B1, ADAM_B2, ADAM_EPS, ADAM_WD, ADAM_STEP = 0.001, 0.9, 0.999, 1e-08, 0.01, 10
VMEM_LIMIT = 56 * 2 ** 20


def _cparams(sem=None):
    return pltpu.CompilerParams(dimension_semantics=sem, vmem_limit_bytes=VMEM_LIMIT)


def _sigmoid(x):
    return 1.0 / (1.0 + jnp.exp(-x))


def _silu(x):
    return x * _sigmoid(x)


def _dsilu(x):
    s = _sigmoid(x)
    return s * (1.0 + x * (1.0 - s))


def _softplus(x):
    return jnp.maximum(x, 0.0) + jnp.log(1.0 + jnp.exp(-jnp.abs(x)))


_GELU_C = math.sqrt(2.0 / math.pi)


def _gelu(x):
    return 0.5 * x * (1.0 + jnp.tanh(_GELU_C * (x + 0.044715 * x * x * x)))


def _dgelu(x):
    t = jnp.tanh(_GELU_C * (x + 0.044715 * x * x * x))
    return 0.5 * (1.0 + t) + 0.5 * x * (1.0 - t * t) * _GELU_C * (1.0 + 3.0 * 0.044715 * x * x)


_DIMS = {"nn": (((1,), (0,)), ((), ())), "nt": (((1,), (1,)), ((), ())), "tn": (((0,), (0,)), ((), ()))}


class Side:
    def __init__(self, operands, out_shapes, scratch, start, finish):
        self.operands, self.out_shapes, self.scratch, self.start, self.finish = operands, out_shapes, scratch, start, finish


def matmul(a, b, *, mode, out_dtype, tm, tn, tk, name, side=None):
    if mode == "nn":
        (m, k), n = a.shape, b.shape[1]
    elif mode == "nt":
        (m, k), n = a.shape, b.shape[0]
    else:
        (k, m), n = a.shape, b.shape[1]
    tm, tn, tk = min(tm, m), min(tn, n), min(tk, k)
    assert m % tm == 0 and n % tn == 0 and k % tk == 0, (name, m, n, k, tm, tn, tk)
    grid = (m // tm, n // tn, k // tk)
    nk = grid[2]
    dims = _DIMS[mode]
    n_in = 0 if side is None else len(side.operands)
    n_out = 0 if side is None else len(side.out_shapes)
    n_acc = 0 if nk == 1 else 1

    def body(*refs):
        a_ref, b_ref, o_ref = refs[0], refs[1], refs[2 + n_in]
        scratch = refs[3 + n_in + n_out:]
        side_refs = (refs[2:2 + n_in], refs[3 + n_in:3 + n_in + n_out], scratch[n_acc:])
        ids = [pl.program_id(d) for d in range(3)]
        if side is not None:
            @pl.when((ids[0] == 0) & (ids[1] == 0) & (ids[2] == 0))
            def _():
                side.start(*side_refs)

        prod = lax.dot_general(a_ref[...].astype(BF16), b_ref[...].astype(BF16), dims, preferred_element_type=F32)
        if nk == 1:
            o_ref[...] = prod.astype(out_dtype)
        else:
            acc_ref = scratch[0]

            @pl.when(ids[2] == 0)
            def _():
                acc_ref[...] = prod

            @pl.when(ids[2] > 0)
            def _():
                acc_ref[...] += prod

            @pl.when(ids[2] == nk - 1)
            def _():
                o_ref[...] = acc_ref[...].astype(out_dtype)

        if side is not None:
            @pl.when((ids[0] == grid[0] - 1) & (ids[1] == grid[1] - 1) & (ids[2] == nk - 1))
            def _():
                side.finish(*side_refs)

    a_spec = pl.BlockSpec((tk, tm), lambda i, j, q: (q, i)) if mode == "tn" else pl.BlockSpec((tm, tk), lambda i, j, q: (i, q))
    b_spec = pl.BlockSpec((tn, tk), lambda i, j, q: (j, q)) if mode == "nt" else pl.BlockSpec((tk, tn), lambda i, j, q: (q, j))
    o_spec = pl.BlockSpec((tm, tn), lambda i, j, q: (i, j))
    o_shape = jax.ShapeDtypeStruct((m, n), out_dtype)
    acc = [] if nk == 1 else [pltpu.VMEM((tm, tn), F32)]
    if side is None:
        return pl.pallas_call(
            body, name=name, grid=grid, in_specs=[a_spec, b_spec], out_specs=o_spec, out_shape=o_shape, scratch_shapes=acc,
            compiler_params=_cparams(("parallel", "parallel", "arbitrary")),
        )(a, b)
    hbm = pl.BlockSpec(memory_space=pltpu.HBM)
    return pl.pallas_call(
        body, name=name, grid=grid, in_specs=[a_spec, b_spec] + [hbm] * n_in, out_specs=[o_spec] + [hbm] * n_out,
        out_shape=[o_shape] + list(side.out_shapes), scratch_shapes=acc + list(side.scratch),
        compiler_params=_cparams(("arbitrary", "arbitrary", "arbitrary")),
    )(a, b, *side.operands)


def rowwise(fn, ins, outs, *, rows, tr, name, consts=(), reds=(), into=None):
    tr = min(tr, rows)
    assert rows % tr == 0, (name, rows, tr)
    n_in, n_c, n_o = len(ins), len(consts), len(outs)
    n_buf = 0 if into is None else 1

    def body(*refs):
        vals = [r[...].astype(F32) for r in refs[:n_in + n_c]]
        res = fn(*vals)
        o_refs = refs[n_in + n_c + n_buf:]
        for r, v in zip(o_refs[:n_o], res[:n_o]):
            r[...] = v.astype(r.dtype)
        if reds:
            i = pl.program_id(0)

            @pl.when(i == 0)
            def _():
                for r, v in zip(o_refs[n_o:], res[n_o:]):
                    r[...] = v.astype(r.dtype)

            @pl.when(i > 0)
            def _():
                for r, v in zip(o_refs[n_o:], res[n_o:]):
                    r[...] += v.astype(r.dtype)

    in_specs = [pl.BlockSpec((tr, w), functools.partial(lambda i, cb: (i, cb), cb=cb)) for (_, w, cb) in ins]
    in_specs += [pl.BlockSpec(c.shape, lambda i: (0, 0)) for c in consts]
    out_specs = [pl.BlockSpec((tr, w), lambda i: (i, 0)) for (w, _) in outs]
    out_specs += [pl.BlockSpec(s, lambda i: (0, 0)) for (s, _) in reds]
    out_shape = [jax.ShapeDtypeStruct((rows, w), d) for (w, d) in outs]
    out_shape += [jax.ShapeDtypeStruct(s, d) for (s, d) in reds]
    operands = [a for (a, _, _) in ins] + list(consts)
    aliases = {}
    if into is not None:
        buf, pos, cb = into
        assert buf.dtype == outs[pos][1] and buf.shape[0] == rows, (name, buf.shape, buf.dtype)
        in_specs.append(pl.BlockSpec(memory_space=pl.ANY))
        operands.append(buf)
        out_specs[pos] = pl.BlockSpec((tr, outs[pos][0]), functools.partial(lambda i, cb: (i, cb), cb=cb))
        out_shape[pos] = jax.ShapeDtypeStruct(buf.shape, buf.dtype)
        aliases = {len(operands) - 1: pos}
    return pl.pallas_call(
        body, name=name, grid=(rows // tr,), in_specs=in_specs, out_specs=out_specs, out_shape=out_shape,
        input_output_aliases=aliases, compiler_params=_cparams(("arbitrary",) if reds else ("parallel",)),
    )(*operands)


def _colsum(x):
    return jnp.sum(x, axis=0, keepdims=True)


def rms_fwd(x, g, name):
    s, d = x.shape

    def fn(xv, gv):
        r = lax.rsqrt(jnp.mean(xv * xv, axis=-1, keepdims=True) + EPS)
        return xv * r * gv, r

    return rowwise(fn, [(x, d, 0)], [(d, BF16), (1, F32)], rows=s, tr=256, name=name, consts=[g])


def rms_bwd_x(du, x, r, g, dh):
    s, d = x.shape

    def fn(duv, xv, rv, dhv, gv):
        dyg = duv * gv
        dx = rv * dyg - xv * (rv * rv * rv) * jnp.mean(dyg * xv, axis=-1, keepdims=True)
        return dhv + dx, _colsum(duv * xv * rv)

    return rowwise(fn, [(du, d, 0), (x, d, 0), (r, 1, 0), (dh, d, 0)], [(d, F32)], rows=s, tr=256,
                   name="rms_bwd_x", consts=[g], reds=[((1, d), F32)])


def rms_bwd_g(du, x, r, name):
    s, d = x.shape

    def fn(duv, xv, rv):
        return (_colsum(duv * xv * rv),)

    return rowwise(fn, [(du, d, 0), (x, d, 0), (r, 1, 0)], [], rows=s, tr=256, name=name, reds=[((1, d), F32)])[0]


def final_stage(x, hres, target, g):
    s, d = x.shape

    def fn(xv, hv, tv, gv):
        h = xv + hv
        r = lax.rsqrt(jnp.mean(h * h, axis=-1, keepdims=True) + EPS)
        y = h * r * gv
        e = y - tv
        loss = 0.5 * jnp.sum(jnp.sum(e * e, axis=-1, keepdims=True), axis=0, keepdims=True) / d
        dy = e / d
        dyg = dy * gv
        dh = r * dyg - h * (r * r * r) * jnp.mean(dyg * h, axis=-1, keepdims=True)
        return dh, dh, loss, _colsum(dy * h * r)

    return rowwise(fn, [(x, d, 0), (hres, d, 0), (target, d, 0)], [(d, F32), (d, BF16)], rows=s, tr=256,
                   name="final_stage", consts=[g], reds=[((1, 1), F32), ((1, d), F32)])


def merge_fwd(pa, pb, pc, proj, gate_cb):
    s, d = pa.shape

    def fn(a, b, c, g0, g1, g2):
        return (_sigmoid(g0) * a + _sigmoid(g1) * b + _sigmoid(g2) * c,)

    ins = [(pa, d, 0), (pb, d, 0), (pc, d, 0)] + [(proj, d, gate_cb + i) for i in range(3)]
    return rowwise(fn, ins, [(d, BF16)], rows=s, tr=256, name="merge_fwd")[0]


def merge_bwd(dm, pa, pb, pc, proj, gate_cb, dproj):
    s, d = pa.shape

    def fn(dmv, a, b, c, g0, g1, g2):
        s0, s1, s2 = _sigmoid(g0), _sigmoid(g1), _sigmoid(g2)
        dgates = [dmv * a * s0 * (1.0 - s0), dmv * b * s1 * (1.0 - s1), dmv * c * s2 * (1.0 - s2)]
        return dmv * s0, dmv * s1, dmv * s2, jnp.concatenate(dgates, axis=1)

    ins = [(dm, d, 0), (pa, d, 0), (pb, d, 0), (pc, d, 0)] + [(proj, d, gate_cb + i) for i in range(3)]
    return rowwise(fn, ins, [(d, BF16)] * 3 + [(3 * d, BF16)], rows=s, tr=128, name="merge_bwd", into=(dproj, 3, gate_cb // 3))


def gate_fwd(o, proj, z_cb, name):
    s, w = o.shape

    def fn(ov, zv):
        return (ov * _silu(zv),)

    return rowwise(fn, [(o, w, 0), (proj, w, z_cb)], [(w, BF16)], rows=s, tr=512, name=name)[0]


def gate_bwd(dgo, o, proj, z_cb, name, dproj):
    s, w = o.shape

    def fn(dv, ov, zv):
        return dv * _silu(zv), dv * ov * _dsilu(zv)

    return rowwise(fn, [(dgo, w, 0), (o, w, 0), (proj, w, z_cb)], [(w, F32), (w, BF16)], rows=s, tr=512, name=name,
                   into=(dproj, 1, z_cb))


def gdn_out_fwd(o_raw, proj, z_cb, gn):
    s, w = o_raw.shape

    def fn(ov, zv, gv):
        outs = []
        for h in range(NHEAD):
            oh = ov[:, h * HEAD:(h + 1) * HEAD]
            r = lax.rsqrt(jnp.mean(oh * oh, axis=-1, keepdims=True) + EPS)
            outs.append(oh * r * gv)
        return (jnp.concatenate(outs, axis=1) * _silu(zv),)

    return rowwise(fn, [(o_raw, w, 0), (proj, w, z_cb)], [(w, BF16)], rows=s, tr=512, name="gdn_out_fwd", consts=[gn])[0]


def gdn_out_bwd(dga, o_raw, proj, z_cb, gn, dproj):
    s, w = o_raw.shape

    def fn(dv, ov, zv, gv):
        sz, dsz = _silu(zv), _dsilu(zv)
        do_l, dz_l = [], []
        dg = jnp.zeros((1, HEAD), F32)
        for h in range(NHEAD):
            sl = slice(h * HEAD, (h + 1) * HEAD)
            oh, dgh = ov[:, sl], dv[:, sl]
            r = lax.rsqrt(jnp.mean(oh * oh, axis=-1, keepdims=True) + EPS)
            on = oh * r * gv
            don = dgh * sz[:, sl]
            dz_l.append(dgh * on * dsz[:, sl])
            dg = dg + _colsum(don * oh * r)
            dyg = don * gv
            do_l.append(r * dyg - oh * (r * r * r) * jnp.mean(dyg * oh, axis=-1, keepdims=True))
        return jnp.concatenate(do_l, axis=1), jnp.concatenate(dz_l, axis=1), dg

    return rowwise(fn, [(dga, w, 0), (o_raw, w, 0), (proj, w, z_cb)], [(w, F32), (w, BF16)], rows=s, tr=512,
                   name="gdn_out_bwd", consts=[gn], reds=[((1, HEAD), F32)], into=(dproj, 1, z_cb))


def s5_act_fwd(y_ssm, proj, xb_cb, dvec):
    s, w = y_ssm.shape

    def fn(yv, xv, dv):
        return (_gelu(yv + dv * xv),)

    return rowwise(fn, [(y_ssm, w, 0), (proj, w, xb_cb)], [(w, BF16)], rows=s, tr=512, name="s5_act_fwd", consts=[dvec])[0]


def s5_act_bwd(dyb, y_ssm, proj, xb_cb, dvec):
    s, w = y_ssm.shape

    def fn(dv_, yv, xv, dv):
        dpre = dv_ * _dgelu(yv + dv * xv)
        return dpre, dpre * dv, _colsum(dpre * xv)

    return rowwise(fn, [(dyb, w, 0), (y_ssm, w, 0), (proj, w, xb_cb)], [(w, F32), (w, F32)], rows=s, tr=512,
                   name="s5_act_bwd", consts=[dvec], reds=[((1, w), F32)])


def s5_glu_fwd(glu, proj, z_cb):
    s, w2 = glu.shape
    w = w2 // 2

    def fn(val, gate, zv):
        return (val * _sigmoid(gate) * _silu(zv),)

    return rowwise(fn, [(glu, w, 0), (glu, w, 1), (proj, w, z_cb)], [(w, BF16)], rows=s, tr=512, name="s5_glu_fwd")[0]


def s5_glu_bwd(dgb, glu, proj, z_cb, dproj):
    s, w2 = glu.shape
    w = w2 // 2

    def fn(dv, val, gate, zv):
        sg = _sigmoid(gate)
        ob = val * sg
        dob = dv * _silu(zv)
        return dob * sg, dob * val * sg * (1.0 - sg), dv * ob * _dsilu(zv)

    return rowwise(fn, [(dgb, w, 0), (glu, w, 0), (glu, w, 1), (proj, w, z_cb)], [(w, BF16)] * 3, rows=s, tr=512,
                   name="s5_glu_bwd", into=(dproj, 2, z_cb))


def add_into(a, b, name, dproj, cb):
    s, w = a.shape

    def fn(av, bv):
        return (av + bv,)

    return rowwise(fn, [(a, w, 0), (b, w, 0)], [(w, BF16)], rows=s, tr=512, name=name, into=(dproj, 0, cb))[0]


GATE_W, GATE_CB = 6144, 0
QKV_W, QKV_CB = 3072, 2
ZA_CB, XB_CB, ZB_CB, QC_CB, ZC_CB = 9, 10, 11, 12, 13
BA_CB, BA_W = 112, 128
BA_PAD, BA_PAD_CB = 1024, 14
PROJ_W = 14336 + BA_PAD


def _dot(a, b, dims="nn", prec=None):
    if prec is None:
        a, b = a.astype(BF16), b.astype(BF16)
    return lax.dot_general(a, b, _DIMS[dims], preferred_element_type=F32, precision=prec)


def _split(a):
    hi = a.astype(BF16)
    return hi, (a - hi.astype(F32)).astype(BF16)


def _dot3(a, b, dims="nn"):
    (ah, al), (bh, bl) = _split(a), _split(b)
    d = functools.partial(lax.dot_general, dimension_numbers=_DIMS[dims], preferred_element_type=F32)
    return d(ah, bh) + (d(ah, bl) + d(al, bh))


def _iota2(shape, dim):
    return lax.broadcasted_iota(jnp.int32, shape, dim)


def _conv_taps(xs, tr, k):
    if k == 0:
        return xs[8:8 + tr]
    return pltpu.roll(xs, k, 0)[8:8 + tr]


def _conv_silu_parts(xv, halo, wv, first):
    tr = xv.shape[0]
    xs = jnp.concatenate([jnp.where(first, 0.0, halo), xv], axis=0)
    taps = [_conv_taps(xs, tr, 3 - j) for j in range(4)]
    c = taps[0] * wv[0:1] + taps[1] * wv[1:2] + taps[2] * wv[2:3] + taps[3] * wv[3:4]
    return taps, c


def gdn_prep_fwd(proj, conv_w, alog_pad, dt_pad):
    s = proj.shape[0]
    tr = min(256, s)
    w = NHEAD * HEAD

    def body(x_ref, halo_ref, ba_ref, w_ref, al_ref, dt_ref, q_ref, k_ref, v_ref, bg_ref, gcol_ref, gt_ref):
        first = pl.program_id(0) == 0
        _, c = _conv_silu_parts(x_ref[...], halo_ref[...], w_ref[...], first)
        sv = _silu(c)
        for h in range(NHEAD):
            sl = slice(h * HEAD, (h + 1) * HEAD)
            qh, kh = sv[:, h * HEAD:(h + 1) * HEAD], sv[:, w + h * HEAD:w + (h + 1) * HEAD]
            q_ref[:, sl] = qh * lax.rsqrt(jnp.sum(qh * qh, axis=-1, keepdims=True) + EPS) * (HEAD ** -0.5)
            k_ref[:, sl] = kh * lax.rsqrt(jnp.sum(kh * kh, axis=-1, keepdims=True) + EPS)
        v_ref[...] = sv[:, 2 * w:]
        ba = ba_ref[...]
        lane = _iota2(ba.shape, 1)
        beta = _sigmoid(ba)
        g = -jnp.exp(al_ref[...]) * _softplus(ba + dt_ref[...])
        bg = jnp.where(lane < NHEAD, beta, jnp.where(lane < 2 * NHEAD, g, 0.0))
        bg_ref[...] = bg
        er, ec = _iota2((BA_W, BA_W), 0), _iota2((BA_W, BA_W), 1)
        expand = jnp.where((er == NHEAD + ec // 8) & (ec < 8 * NHEAD), 1.0, 0.0)
        grep = _dot(bg, expand, prec=HI)
        lr, lc = _iota2((tr, tr), 0), _iota2((tr, tr), 1)
        tril = jnp.where((lr // CHUNK == lc // CHUNK) & (lr >= lc), 1.0, 0.0)
        gc = _dot(tril, grep, prec=HI)
        gcol_ref[...] = gc
        gt_ref[...] = gc.T

    nb8 = tr // 8
    return pl.pallas_call(
        body, name="gdn_prep_fwd", grid=(s // tr,),
        in_specs=[pl.BlockSpec((tr, QKV_W), lambda i: (i, QKV_CB)),
                  pl.BlockSpec((8, QKV_W), lambda i: (jnp.maximum(i * nb8 - 1, 0), QKV_CB)),
                  pl.BlockSpec((tr, BA_W), lambda i: (i, BA_CB)),
                  pl.BlockSpec(conv_w.shape, lambda i: (0, 0)),
                  pl.BlockSpec((1, BA_W), lambda i: (0, 0)), pl.BlockSpec((1, BA_W), lambda i: (0, 0))],
        out_specs=[pl.BlockSpec((tr, w), lambda i: (i, 0))] * 3 + [pl.BlockSpec((tr, BA_W), lambda i: (i, 0))] * 2
        + [pl.BlockSpec((BA_W, tr), lambda i: (0, i))],
        out_shape=[jax.ShapeDtypeStruct((s, w), F32)] * 3 + [jax.ShapeDtypeStruct((s, BA_W), F32)] * 2
        + [jax.ShapeDtypeStruct((BA_W, s), F32)],
        compiler_params=_cparams(("parallel",)),
    )(proj, proj, proj, conv_w, alog_pad, dt_pad)


def _chunk_common(qh, kh, bgv, gcolv, gtv, h):
    beta = bgv[:, h:h + 1]
    gcc = gcolv[:, 8 * h:8 * h + 1]
    gcr = jnp.concatenate([gtv[8 * h:8 * h + 8, :]] * (CHUNK // 8), axis=0)
    ii, jj = _iota2((CHUNK, CHUNK), 0), _iota2((CHUNK, CHUNK), 1)
    incl, strict = ii >= jj, ii > jj
    decay = jnp.where(incl, jnp.exp(jnp.where(incl, gcc - gcr, 0.0)), 0.0)
    gl = gcr[:, CHUNK - 1:CHUNK]
    return beta, gcc, decay, strict, gl


def gdn_intra_fwd(q, k, v, bg, gcol, gt3):
    s, w = q.shape
    n = s // CHUNK

    def body(q_ref, k_ref, v_ref, bg_ref, gcol_ref, gt_ref, u_ref, w_ref, qd_ref, kd_ref, qk_ref, t_ref):
        bgv, gcolv, gtv = bg_ref[...], gcol_ref[...], gt_ref[0]
        ii, jj = _iota2((CHUNK, CHUNK), 0), _iota2((CHUNK, CHUNK), 1)
        eye = jnp.where(ii == jj, 1.0, 0.0)
        ps, ts, rhs = [], [], []
        for h in range(NHEAD):
            sl = slice(h * HEAD, (h + 1) * HEAD)
            qh, kh, vh = q_ref[:, sl], k_ref[:, sl], v_ref[:, sl]
            beta, gcc, decay, strict, gl = _chunk_common(qh, kh, bgv, gcolv, gtv, h)
            kb = kh * beta
            eg = jnp.exp(gcc)
            p = -jnp.where(strict, _dot(kb, kh, "nt") * decay, 0.0)
            ps.append(p)
            ts.append(eye + p)
            rhs.append((vh * beta, kb * eg))
            qd_ref[:, sl] = qh * eg
            kd_ref[:, sl] = kh * jnp.exp(gl - gcc)
            qk_ref[0, h] = _dot(qh, kh, "nt") * decay
        for _ in range(5):
            ps = [_dot3(p, p) for p in ps]
            ts = [t + _dot3(t, p) for t, p in zip(ts, ps)]
        for h in range(NHEAD):
            sl = slice(h * HEAD, (h + 1) * HEAD)
            u_ref[:, sl] = _dot3(ts[h], rhs[h][0])
            w_ref[:, sl] = _dot3(ts[h], rhs[h][1])
            t_ref[0, h] = ts[h]

    tok = pl.BlockSpec((CHUNK, w), lambda i: (i, 0))
    sm = pl.BlockSpec((CHUNK, BA_W), lambda i: (i, 0))
    sq = pl.BlockSpec((1, NHEAD, CHUNK, CHUNK), lambda i: (i, 0, 0, 0))
    return pl.pallas_call(
        body, name="gdn_intra_fwd", grid=(n,),
        in_specs=[tok, tok, tok, sm, sm, pl.BlockSpec((1, BA_W, CHUNK), lambda i: (i, 0, 0))],
        out_specs=[tok] * 4 + [sq, sq],
        out_shape=[jax.ShapeDtypeStruct((s, w), F32)] * 4 + [jax.ShapeDtypeStruct((n, NHEAD, CHUNK, CHUNK), F32)] * 2,
        compiler_params=_cparams(("parallel",)),
    )(q, k, v, bg, gcol, gt3)


def _state_decay(gtv, h):
    g8 = gtv[8 * h:8 * h + 8, CHUNK - 1:CHUNK]
    return jnp.exp(jnp.concatenate([g8] * (HEAD // 8), axis=0))


def gdn_seq_fwd(u, wd, qd, kd, qk, gt3):
    s, w = u.shape
    n = s // CHUNK

    def body(u_ref, w_ref, qd_ref, kd_ref, qk_ref, gt_ref, o_ref, st_ref, s_ref):
        @pl.when(pl.program_id(0) == 0)
        def _():
            s_ref[...] = jnp.zeros_like(s_ref)

        gtv = gt_ref[0]
        cols = [slice(h * HEAD, (h + 1) * HEAD) for h in range(NHEAD)]
        states = [s_ref[h] for h in range(NHEAD)]
        for h in range(NHEAD):
            st_ref[0, h] = states[h]
        vns = [u_ref[:, cols[h]] - _dot(w_ref[:, cols[h]], states[h]) for h in range(NHEAD)]
        from_state = [_dot(qd_ref[:, cols[h]], states[h]) for h in range(NHEAD)]
        for h in range(NHEAD):
            o_ref[:, cols[h]] = from_state[h] + _dot(qk_ref[0, h], vns[h])
        for h in range(NHEAD):
            s_ref[h] = states[h] * _state_decay(gtv, h) + _dot(kd_ref[:, cols[h]], vns[h], "tn")

    tok = pl.BlockSpec((CHUNK, w), lambda i: (i, 0))
    return pl.pallas_call(
        body, name="gdn_seq_fwd", grid=(n,),
        in_specs=[tok] * 4 + [pl.BlockSpec((1, NHEAD, CHUNK, CHUNK), lambda i: (i, 0, 0, 0)),
                              pl.BlockSpec((1, BA_W, CHUNK), lambda i: (i, 0, 0))],
        out_specs=[tok, pl.BlockSpec((1, NHEAD, HEAD, HEAD), lambda i: (i, 0, 0, 0))],
        out_shape=[jax.ShapeDtypeStruct((s, w), F32), jax.ShapeDtypeStruct((n, NHEAD, HEAD, HEAD), F32)],
        scratch_shapes=[pltpu.VMEM((NHEAD, HEAD, HEAD), F32)],
        compiler_params=_cparams(("arbitrary",)),
    )(u, wd, qd, kd, qk, gt3)


def gdn_seq_bwd(do, u, wd, qd, kd, qk, gt3, states):
    s, w = u.shape
    n = s // CHUNK

    def body(do_ref, u_ref, w_ref, qd_ref, kd_ref, qk_ref, gt_ref, st_ref,
             du_ref, dw_ref, dqd_ref, dkd_ref, dqk_ref, dgl_ref, ds_ref):
        @pl.when(pl.program_id(0) == 0)
        def _():
            ds_ref[...] = jnp.zeros_like(ds_ref)

        gtv = gt_ref[0]
        heads = range(NHEAD)
        cols = [slice(h * HEAD, (h + 1) * HEAD) for h in heads]
        sts = [st_ref[0, h] for h in heads]
        dsps = [ds_ref[h] for h in heads]
        vns = [u_ref[:, cols[h]] - _dot(w_ref[:, cols[h]], sts[h]) for h in heads]
        dvns = [_dot(qk_ref[0, h], do_ref[:, cols[h]], "tn") + _dot(kd_ref[:, cols[h]], dsps[h]) for h in heads]
        for h in heads:
            du_ref[:, cols[h]] = dvns[h]
            dqd_ref[:, cols[h]] = _dot(do_ref[:, cols[h]], sts[h], "nt")
        for h in heads:
            dw_ref[:, cols[h]] = -_dot(dvns[h], sts[h], "nt")
            dkd_ref[:, cols[h]] = _dot(vns[h], dsps[h], "nt")
            dqk_ref[0, h] = _dot(do_ref[:, cols[h]], vns[h], "nt")
        for h in heads:
            ds_ref[h] = (dsps[h] * _state_decay(gtv, h) + _dot(qd_ref[:, cols[h]], do_ref[:, cols[h]], "tn")
                         - _dot(w_ref[:, cols[h]], dvns[h], "tn"))
        dgl_ref[0] = jnp.concatenate([_colsum(sts[h] * dsps[h]) for h in heads], axis=0)

    tok = pl.BlockSpec((CHUNK, w), lambda i: (n - 1 - i, 0))
    sq = pl.BlockSpec((1, NHEAD, CHUNK, CHUNK), lambda i: (n - 1 - i, 0, 0, 0))
    return pl.pallas_call(
        body, name="gdn_seq_bwd", grid=(n,),
        in_specs=[tok] * 5 + [sq, pl.BlockSpec((1, BA_W, CHUNK), lambda i: (n - 1 - i, 0, 0)),
                              pl.BlockSpec((1, NHEAD, HEAD, HEAD), lambda i: (n - 1 - i, 0, 0, 0))],
        out_specs=[tok] * 4 + [sq, pl.BlockSpec((1, NHEAD, HEAD), lambda i: (n - 1 - i, 0, 0))],
        out_shape=[jax.ShapeDtypeStruct((s, w), F32)] * 4 + [jax.ShapeDtypeStruct((n, NHEAD, CHUNK, CHUNK), F32),
                                                            jax.ShapeDtypeStruct((n, NHEAD, HEAD), F32)],
        scratch_shapes=[pltpu.VMEM((NHEAD, HEAD, HEAD), F32)],
        compiler_params=_cparams(("arbitrary",)),
    )(do, u, wd, qd, kd, qk, gt3, states)


def gdn_intra_bwd(q, k, v, bg, gcol, gt3, tinv, du, dw, dqd, dkd, dqk, dgl):
    s, w = q.shape
    n = s // CHUNK

    def body(q_ref, k_ref, v_ref, bg_ref, gcol_ref, gt_ref, t_ref, du_ref, dw_ref, dqd_ref, dkd_ref, dqk_ref, dgl_ref,
             dq_ref, dk_ref, dv_ref, dbg_ref):
        bgv, gcolv, gtv, dglv = bg_ref[...], gcol_ref[...], gt_ref[0], dgl_ref[0]
        ii, jj = _iota2((CHUNK, CHUNK), 0), _iota2((CHUNK, CHUNK), 1)
        triu = jnp.where(ii <= jj, 1.0, 0.0)
        ones = jnp.ones((CHUNK, BA_W), F32)
        lane = _iota2((CHUNK, BA_W), 1)
        row = _iota2((CHUNK, 1), 0)
        dbg = jnp.zeros((CHUNK, BA_W), F32)
        first = []
        for h in range(NHEAD):
            sl = slice(h * HEAD, (h + 1) * HEAD)
            qh, kh, vh = q_ref[:, sl], k_ref[:, sl], v_ref[:, sl]
            beta, gcc, decay, strict, gl = _chunk_common(qh, kh, bgv, gcolv, gtv, h)
            kb = kh * beta
            eg = jnp.exp(gcc)
            rv, rk = vh * beta, kb * eg
            t, duh, dwh = t_ref[0, h], du_ref[:, sl], dw_ref[:, sl]
            first.append((_dot3(duh, rv, "nt") + _dot3(dwh, rk, "nt"), _dot3(t, duh, "tn"), _dot3(t, dwh, "tn"),
                          _dot(kb, kh, "nt"), _dot(qh, kh, "nt")))
        second = [_dot3(t_ref[0, h], first[h][0], "tn") for h in range(NHEAD)]
        third = [_dot3(second[h], t_ref[0, h], "nt") for h in range(NHEAD)]
        for h in range(NHEAD):
            sl = slice(h * HEAD, (h + 1) * HEAD)
            qh, kh, vh = q_ref[:, sl], k_ref[:, sl], v_ref[:, sl]
            beta, gcc, decay, strict, gl = _chunk_common(qh, kh, bgv, gcolv, gtv, h)
            dqdh, dkdh, dqkh = dqd_ref[:, sl], dkd_ref[:, sl], dqk_ref[0, h]
            kb = kh * beta
            eg = jnp.exp(gcc)
            ekd = jnp.exp(gl - gcc)
            rk = kb * eg
            _, drv, drk, m, p = first[h]
            da = jnp.where(strict, -third[h], 0.0)
            dm = da * decay
            dpm = dqkh * decay
            dkb = _dot(dm, kh) + drk * eg
            dq = _dot(dpm, kh) + dqdh * eg
            dk = _dot(dm, kb, "tn") + _dot(dpm, qh, "tn") + dkdh * ekd + dkb * beta
            e = (da * m + dqkh * p) * decay
            sk = jnp.sum(dkdh * kh * ekd, axis=-1, keepdims=True)
            dgc = (jnp.sum(e, axis=-1, keepdims=True) - _dot3(e, ones, "tn")[:, 0:1]
                   + jnp.sum(dqdh * qh * eg, axis=-1, keepdims=True) - sk + jnp.sum(drk * rk, axis=-1, keepdims=True))
            dglast = jnp.sum(sk, axis=0, keepdims=True) + jnp.sum(dglv[h:h + 1, :], axis=-1, keepdims=True) * jnp.exp(gl)
            dgc = dgc + jnp.where(row == CHUNK - 1, dglast, 0.0)
            dg = _dot3(triu, dgc * ones)
            dbeta = jnp.sum(dkb * kh, axis=-1, keepdims=True) + jnp.sum(drv * vh, axis=-1, keepdims=True)
            dbg = dbg + jnp.where(lane == h, dbeta, 0.0) + jnp.where(lane == NHEAD + h, dg, 0.0)
            dq_ref[:, sl] = dq
            dk_ref[:, sl] = dk
            dv_ref[:, sl] = drv * beta
        dbg_ref[...] = dbg

    tok = pl.BlockSpec((CHUNK, w), lambda i: (i, 0))
    sm = pl.BlockSpec((CHUNK, BA_W), lambda i: (i, 0))
    sq = pl.BlockSpec((1, NHEAD, CHUNK, CHUNK), lambda i: (i, 0, 0, 0))
    return pl.pallas_call(
        body, name="gdn_intra_bwd", grid=(n,),
        in_specs=[tok, tok, tok, sm, sm, pl.BlockSpec((1, BA_W, CHUNK), lambda i: (i, 0, 0)), sq,
                  tok, tok, tok, tok, sq, pl.BlockSpec((1, NHEAD, HEAD), lambda i: (i, 0, 0))],
        out_specs=[tok] * 3 + [sm],
        out_shape=[jax.ShapeDtypeStruct((s, w), F32)] * 3 + [jax.ShapeDtypeStruct((s, BA_W), F32)],
        compiler_params=_cparams(("parallel",)),
    )(q, k, v, bg, gcol, gt3, tinv, du, dw, dqd, dkd, dqk, dgl)


def gdn_prep_bwd1(proj, conv_w, alog_pad, dt_pad, dq, dk, dv, dbg, dproj):
    s = proj.shape[0]
    tr = min(256, s)
    w = NHEAD * HEAD
    pad_w = BA_PAD

    def body(x_ref, halo_ref, ba_ref, w_ref, al_ref, dt_ref, dq_ref, dk_ref, dv_ref, dbg_ref, buf_ref,
             dc_ref, dba_ref, dw0_ref, dw1_ref, dw2_ref, dw3_ref, dal_ref, ddt_ref):
        i = pl.program_id(0)
        taps, c = _conv_silu_parts(x_ref[...], halo_ref[...], w_ref[...], i == 0)
        sv, dsv = _silu(c), _dsilu(c)
        for h in range(NHEAD):
            for base, d_ref, scale in ((0, dq_ref, HEAD ** -0.5), (w, dk_ref, 1.0)):
                sl = slice(base + h * HEAD, base + (h + 1) * HEAD)
                sh = sv[:, sl]
                dn = d_ref[:, h * HEAD:(h + 1) * HEAD]
                r = lax.rsqrt(jnp.sum(sh * sh, axis=-1, keepdims=True) + EPS)
                dsh = scale * (r * dn - sh * (r * r * r) * jnp.sum(dn * sh, axis=-1, keepdims=True))
                dc_ref[:, sl] = dsh * dsv[:, sl]
        dc_ref[:, 2 * w:] = dv_ref[...] * dsv[:, 2 * w:]
        dc = dc_ref[...]
        ba, dbgv = ba_ref[...], dbg_ref[...]
        lane = _iota2(ba.shape, 1)
        beta = _sigmoid(ba)
        ea = jnp.exp(al_ref[...])
        z = ba + dt_ref[...]
        g = -ea * _softplus(z)
        is_g = (lane >= NHEAD) & (lane < 2 * NHEAD)
        da_raw = jnp.where(is_g, dbgv * (-ea) * _sigmoid(z), 0.0)
        dba = jnp.where(lane < NHEAD, dbgv * beta * (1.0 - beta), da_raw)
        dba_ref[...] = jnp.concatenate([dba, jnp.zeros((tr, pad_w - BA_W), F32)], axis=1).astype(BF16)
        partial = [_colsum(dc * tp) for tp in taps] + [_colsum(jnp.where(is_g, dbgv * g, 0.0)), _colsum(da_raw)]
        red_refs = (dw0_ref, dw1_ref, dw2_ref, dw3_ref, dal_ref, ddt_ref)

        @pl.when(i == 0)
        def _():
            for r_, v_ in zip(red_refs, partial):
                r_[...] = v_

        @pl.when(i > 0)
        def _():
            for r_, v_ in zip(red_refs, partial):
                r_[...] += v_

    nb8 = tr // 8
    tok = pl.BlockSpec((tr, w), lambda i: (i, 0))
    one = lambda width: pl.BlockSpec((1, width), lambda i: (0, 0))
    return pl.pallas_call(
        body, name="gdn_prep_bwd1", grid=(s // tr,),
        in_specs=[pl.BlockSpec((tr, QKV_W), lambda i: (i, QKV_CB)),
                  pl.BlockSpec((8, QKV_W), lambda i: (jnp.maximum(i * nb8 - 1, 0), QKV_CB)),
                  pl.BlockSpec((tr, BA_W), lambda i: (i, BA_CB)),
                  pl.BlockSpec(conv_w.shape, lambda i: (0, 0)), one(BA_W), one(BA_W),
                  tok, tok, tok, pl.BlockSpec((tr, BA_W), lambda i: (i, 0)), pl.BlockSpec(memory_space=pl.ANY)],
        out_specs=[pl.BlockSpec((tr, QKV_W), lambda i: (i, 0)), pl.BlockSpec((tr, pad_w), lambda i: (i, BA_PAD_CB))]
        + [one(QKV_W)] * 4 + [one(BA_W)] * 2,
        out_shape=[jax.ShapeDtypeStruct((s, QKV_W), F32), jax.ShapeDtypeStruct(dproj.shape, dproj.dtype)]
        + [jax.ShapeDtypeStruct((1, QKV_W), F32)] * 4 + [jax.ShapeDtypeStruct((1, BA_W), F32)] * 2,
        input_output_aliases={10: 1}, compiler_params=_cparams(("arbitrary",)),
    )(proj, proj, proj, conv_w, alog_pad, dt_pad, dq, dk, dv, dbg, dproj)


def gdn_prep_bwd2(dc, conv_w, dproj):
    s = dc.shape[0]
    tr = min(256, s)
    nblk = s // tr
    nb8 = tr // 8

    def body(dc_ref, halo_ref, w_ref, buf_ref, o_ref):
        last = pl.program_id(0) == nblk - 1
        wv = w_ref[...]
        xs = jnp.concatenate([dc_ref[...], jnp.where(last, 0.0, halo_ref[...])], axis=0)
        acc = xs[:tr] * wv[3:4]
        for j in range(3):
            acc = acc + pltpu.roll(xs, tr + 8 - (3 - j), 0)[:tr] * wv[j:j + 1]
        o_ref[...] = acc.astype(BF16)

    return pl.pallas_call(
        body, name="gdn_prep_bwd2", grid=(nblk,),
        in_specs=[pl.BlockSpec((tr, QKV_W), lambda i: (i, 0)),
                  pl.BlockSpec((8, QKV_W), lambda i: (jnp.minimum((i + 1) * nb8, s // 8 - 1), 0)),
                  pl.BlockSpec(conv_w.shape, lambda i: (0, 0)), pl.BlockSpec(memory_space=pl.ANY)],
        out_specs=pl.BlockSpec((tr, QKV_W), lambda i: (i, QKV_CB)),
        out_shape=jax.ShapeDtypeStruct(dproj.shape, dproj.dtype), input_output_aliases={3: 0},
        compiler_params=_cparams(("parallel",)),
    )(dc, dc, conv_w, dproj)


S5_W = S5_GROUPS * S5_STATE
S5_IN = S5_GROUPS * S5_GROUP
S5_TILES = 8
S5_TW, S5_TI = S5_W // S5_TILES, S5_IN // S5_TILES


def _s5_param_math(lr, li, ldt, br, bi):
    pr, pc = _iota2((S5_STATE, S5_STATE * S5_GROUP), 0), _iota2((S5_STATE, S5_STATE * S5_GROUP), 1)
    rep = jnp.where(pc // S5_GROUP == pr, 1.0, 0.0)
    dt = jnp.exp(ldt)
    mag = jnp.exp(lr * dt)
    ab_re, ab_im = mag * jnp.cos(li * dt), mag * jnp.sin(li * dt)
    den = lr * lr + li * li
    nr, ni = ab_re - 1.0, ab_im
    coef_re = (nr * lr + ni * li) / den
    coef_im = (ni * lr - nr * li) / den
    cr, ci = _dot(coef_re, rep, prec=HI), _dot(coef_im, rep, prec=HI)
    return ab_re, ab_im, cr * br - ci * bi, cr * bi + ci * br


def s5_param_fwd(lr, li, ldt, br, bi):
    def body(lr_ref, li_ref, ldt_ref, br_ref, bi_ref, ar_ref, ai_ref, bbr_ref, bbi_ref):
        res = _s5_param_math(lr_ref[...], li_ref[...], ldt_ref[...], br_ref[...], bi_ref[...])
        for r, v in zip((ar_ref, ai_ref, bbr_ref, bbi_ref), res):
            r[...] = v

    return pl.pallas_call(
        body, name="s5_param_fwd",
        out_shape=[jax.ShapeDtypeStruct(lr.shape, F32)] * 2 + [jax.ShapeDtypeStruct(br.shape, F32)] * 2,
        compiler_params=_cparams(),
    )(lr, li, ldt, br, bi)


def s5_param_bwd(lr, li, ldt, br, bi, dar, dai, dbbr, dbbi):
    def body(lr_ref, li_ref, ldt_ref, br_ref, bi_ref, dar_ref, dai_ref, dbbr_ref, dbbi_ref, *out_refs):
        _, vjp = jax.vjp(_s5_param_math, lr_ref[...], li_ref[...], ldt_ref[...], br_ref[...], bi_ref[...])
        for r, v in zip(out_refs, vjp((dar_ref[...], dai_ref[...], dbbr_ref[...], dbbi_ref[...]))):
            r[...] = v

    return pl.pallas_call(
        body, name="s5_param_bwd",
        out_shape=[jax.ShapeDtypeStruct(a.shape, F32) for a in (lr, li, ldt, br, bi)],
        compiler_params=_cparams(),
    )(lr, li, ldt, br, bi, dar, dai, dbbr, dbbi)


def _cmul(ar, ai, br, bi):
    return ar * br - ai * bi, ar * bi + ai * br


def _s5_power(ar, ai, steps):
    assert steps & (steps - 1) == 0
    for _ in range(steps.bit_length() - 1):
        ar, ai = _cmul(ar, ai, ar, ai)
    return ar, ai


def _s5_scan_rows(ar_ref, ai_ref, re_ref, im_ref, sr_ref, si_ref, tb, row0, reverse):
    quarter = S5_W // 4
    for qd in range(4):
        cs = slice(qd * quarter, (qd + 1) * quarter)
        are = jnp.broadcast_to(ar_ref[:, cs], (NSEG, quarter))
        aim = jnp.broadcast_to(ai_ref[:, cs], (NSEG, quarter))
        if reverse:
            aim = -aim

        def step(t, carry):
            h_r, h_i = carry
            tt = tb - 1 - t if reverse else t
            rows = pl.ds(pl.multiple_of(row0 + tt * NSEG, NSEG), NSEG)
            n_r = are * h_r - aim * h_i + re_ref[rows, cs]
            n_i = are * h_i + aim * h_r + im_ref[rows, cs]
            re_ref[rows, cs] = n_r
            im_ref[rows, cs] = n_i
            return n_r, n_i

        h_r, h_i = lax.fori_loop(0, tb, step, (sr_ref[:, cs], si_ref[:, cs]), unroll=8)
        sr_ref[:, cs] = h_r
        si_ref[:, cs] = h_i


def _s5_segment_carry(ar_ref, ai_ref, sr_ref, si_ref, steps, reverse):
    pr, pi = _s5_power(ar_ref[...], ai_ref[...], steps)
    if reverse:
        pi = -pi
    cur_r = jnp.zeros((1, S5_W), F32)
    cur_i = jnp.zeros((1, S5_W), F32)
    for s in (range(NSEG - 1, -1, -1) if reverse else range(NSEG)):
        e_r, e_i = sr_ref[s:s + 1, :], si_ref[s:s + 1, :]
        sr_ref[s:s + 1, :] = cur_r
        si_ref[s:s + 1, :] = cur_i
        nr, ni = _cmul(pr, pi, cur_r, cur_i)
        cur_r, cur_i = nr + e_r, ni + e_i


def _s5_blocks(s):
    steps = s // NSEG
    tb = min(32, steps)
    return steps, tb, NSEG * tb, steps // tb


def s5_scan_fwd(xp, a_re, a_im, bre, bim, cre, cim):
    s = xp.shape[0]
    steps, tb, rb, nb = _s5_blocks(s)

    def body(x_ref, ar_ref, ai_ref, bre_ref, bim_ref, cre_ref, cim_ref, y_ref, hsr_ref, hsi_ref,
             hr_ref, hi_ref, sr_ref, si_ref):
        ph, b = pl.program_id(0), pl.program_id(1)

        @pl.when((ph == 0) & (b == 0))
        def _():
            sr_ref[...] = jnp.zeros_like(sr_ref)
            si_ref[...] = jnp.zeros_like(si_ref)

        @pl.when((ph == 1) & (b == 0))
        def _():
            _s5_segment_carry(ar_ref, ai_ref, sr_ref, si_ref, steps, False)

        xv = x_ref[...].astype(BF16)
        for j in range(S5_TILES):
            xs = xv[:, j * S5_TI:(j + 1) * S5_TI]
            hr_ref[:, j * S5_TW:(j + 1) * S5_TW] = _dot(xs, bre_ref[j])
            hi_ref[:, j * S5_TW:(j + 1) * S5_TW] = _dot(xs, bim_ref[j])

        @pl.when(ph == 1)
        def _():
            hsr_ref[0] = sr_ref[...]
            hsi_ref[0] = si_ref[...]

        _s5_scan_rows(ar_ref, ai_ref, hr_ref, hi_ref, sr_ref, si_ref, tb, 0, False)

        @pl.when(ph == 1)
        def _():
            for j in range(S5_TILES):
                cs = slice(j * S5_TW, (j + 1) * S5_TW)
                y_ref[:, j * S5_TI:(j + 1) * S5_TI] = _dot(hr_ref[:, cs], cre_ref[j]) - _dot(hi_ref[:, cs], cim_ref[j])

    row = pl.BlockSpec((1, S5_W), lambda p, b: (0, 0))
    wb = pl.BlockSpec((S5_TILES, S5_TI, S5_TW), lambda p, b: (0, 0, 0))
    wc = pl.BlockSpec((S5_TILES, S5_TW, S5_TI), lambda p, b: (0, 0, 0))
    st = pl.BlockSpec((1, NSEG, S5_W), lambda p, b: (p * b, 0, 0))
    return pl.pallas_call(
        body, name="s5_scan_fwd", grid=(2, nb),
        in_specs=[pl.BlockSpec((rb, S5_IN), lambda p, b: (b, 0)), row, row, wb, wb, wc, wc],
        out_specs=[pl.BlockSpec((rb, S5_IN), lambda p, b: (p * b, 0)), st, st],
        out_shape=[jax.ShapeDtypeStruct((s, S5_IN), F32)] + [jax.ShapeDtypeStruct((nb, NSEG, S5_W), F32)] * 2,
        scratch_shapes=[pltpu.VMEM((rb, S5_W), F32)] * 2 + [pltpu.VMEM((NSEG, S5_W), F32)] * 2,
        compiler_params=_cparams(("arbitrary", "arbitrary")),
    )(xp, a_re, a_im, bre, bim, cre, cim)


def s5_scan_bwd(dyp, xp, a_re, a_im, bre, bim, cre_t, cim_t, hs_r, hs_i):
    s = xp.shape[0]
    steps, tb, rb, nb = _s5_blocks(s)

    def body(dy_ref, x_ref, ar_ref, ai_ref, bre_ref, bim_ref, crt_ref, cit_ref, hsr_ref, hsi_ref,
             dx_ref, dar_ref, dai_ref, dbr_ref, dbi_ref, dcr_ref, dci_ref,
             hr_ref, hi_ref, lr_ref, li_ref, sr_ref, si_ref, fr_ref, fi_ref, accr_ref, acci_ref):
        ph, b = pl.program_id(0), pl.program_id(1)

        @pl.when((ph == 0) & (b == 0))
        def _():
            sr_ref[...] = jnp.zeros_like(sr_ref)
            si_ref[...] = jnp.zeros_like(si_ref)

        @pl.when((ph == 1) & (b == 0))
        def _():
            _s5_segment_carry(ar_ref, ai_ref, sr_ref, si_ref, steps, True)
            for r in (accr_ref, acci_ref, dbr_ref, dbi_ref, dcr_ref, dci_ref):
                r[...] = jnp.zeros_like(r)

        dyv = dy_ref[...].astype(BF16)
        for j in range(S5_TILES):
            ds_ = dyv[:, j * S5_TI:(j + 1) * S5_TI]
            lr_ref[:, j * S5_TW:(j + 1) * S5_TW] = _dot(ds_, crt_ref[j])
            li_ref[:, j * S5_TW:(j + 1) * S5_TW] = -_dot(ds_, cit_ref[j])
        _s5_scan_rows(ar_ref, ai_ref, lr_ref, li_ref, sr_ref, si_ref, tb, 0, True)

        @pl.when(ph == 1)
        def _():
            xv = x_ref[...].astype(BF16)
            for j in range(S5_TILES):
                xs = xv[:, j * S5_TI:(j + 1) * S5_TI]
                hr_ref[NSEG:, j * S5_TW:(j + 1) * S5_TW] = _dot(xs, bre_ref[j])
                hi_ref[NSEG:, j * S5_TW:(j + 1) * S5_TW] = _dot(xs, bim_ref[j])
            hr_ref[0:NSEG, :] = hsr_ref[0]
            hi_ref[0:NSEG, :] = hsi_ref[0]
            fr_ref[...] = hsr_ref[0]
            fi_ref[...] = hsi_ref[0]
            _s5_scan_rows(ar_ref, ai_ref, hr_ref, hi_ref, fr_ref, fi_ref, tb, NSEG, False)
            lam_r, lam_i = lr_ref[...], li_ref[...]
            hp_r, hp_i = hr_ref[0:rb, :], hi_ref[0:rb, :]
            accr_ref[...] += jnp.sum((lam_r * hp_r + lam_i * hp_i).reshape(tb, NSEG, S5_W), axis=0)
            acci_ref[...] += jnp.sum((lam_i * hp_r - lam_r * hp_i).reshape(tb, NSEG, S5_W), axis=0)
            lam_rb, lam_ib = lam_r.astype(BF16), lam_i.astype(BF16)
            h_rb, h_ib = hr_ref[NSEG:, :].astype(BF16), hi_ref[NSEG:, :].astype(BF16)
            for j in range(S5_TILES):
                cs, ci = slice(j * S5_TW, (j + 1) * S5_TW), slice(j * S5_TI, (j + 1) * S5_TI)
                dbr_ref[j] += _dot(xv[:, ci], lam_rb[:, cs], "tn")
                dbi_ref[j] += _dot(xv[:, ci], lam_ib[:, cs], "tn")
                dx_ref[:, ci] = _dot(lam_rb[:, cs], bre_ref[j], "nt") + _dot(lam_ib[:, cs], bim_ref[j], "nt")
                dcr_ref[j] += _dot(h_rb[:, cs], dyv[:, ci], "tn")
                dci_ref[j] -= _dot(h_ib[:, cs], dyv[:, ci], "tn")

        @pl.when((ph == 1) & (b == nb - 1))
        def _():
            dar_ref[...] = jnp.sum(accr_ref[...], axis=0, keepdims=True)
            dai_ref[...] = jnp.sum(acci_ref[...], axis=0, keepdims=True)

    rev = lambda p, b: (nb - 1 - b, 0)
    row = pl.BlockSpec((1, S5_W), lambda p, b: (0, 0))
    wb = pl.BlockSpec((S5_TILES, S5_TI, S5_TW), lambda p, b: (0, 0, 0))
    wc = pl.BlockSpec((S5_TILES, S5_TW, S5_TI), lambda p, b: (0, 0, 0))
    st = pl.BlockSpec((1, NSEG, S5_W), lambda p, b: (nb - 1 - b, 0, 0))
    big = pltpu.VMEM((rb, S5_W), F32)
    big8 = pltpu.VMEM((rb + NSEG, S5_W), F32)
    small = pltpu.VMEM((NSEG, S5_W), F32)
    return pl.pallas_call(
        body, name="s5_scan_bwd", grid=(2, nb),
        in_specs=[pl.BlockSpec((rb, S5_IN), rev), pl.BlockSpec((rb, S5_IN), rev), row, row, wb, wb, wb, wb, st, st],
        out_specs=[pl.BlockSpec((rb, S5_IN), lambda p, b: (nb - 1 - p * b, 0)), row, row, wb, wb, wc, wc],
        out_shape=[jax.ShapeDtypeStruct((s, S5_IN), F32)] + [jax.ShapeDtypeStruct((1, S5_W), F32)] * 2
        + [jax.ShapeDtypeStruct((S5_TILES, S5_TI, S5_TW), F32)] * 2 + [jax.ShapeDtypeStruct((S5_TILES, S5_TW, S5_TI), F32)] * 2,
        scratch_shapes=[big8, big8, big, big, small, small, small, small, small, small],
        compiler_params=_cparams(("arbitrary", "arbitrary")),
    )(dyp, xp, a_re, a_im, bre, bim, cre_t, cim_t, hs_r, hs_i)


XA_DIM = 256
XA_W = XA_HEADS * XA_DIM


def _xa_probs(qh, kh):
    sc = _dot(qh, kh, "nt") * (XA_DIM ** -0.5)
    ex = jnp.exp(sc - jnp.max(sc, axis=-1, keepdims=True))
    return ex / jnp.sum(ex, axis=-1, keepdims=True)


def xa_fwd(proj, kv):
    s = proj.shape[0]
    tq = min(512, s)

    def body(q_ref, kv_ref, o_ref):
        for h in range(XA_HEADS):
            sl = slice(h * XA_DIM, (h + 1) * XA_DIM)
            p = _xa_probs(q_ref[:, sl], kv_ref[:, sl])
            o_ref[:, sl] = _dot(p, kv_ref[:, XA_W + h * XA_DIM:XA_W + (h + 1) * XA_DIM])

    return pl.pallas_call(
        body, name="xa_fwd", grid=(s // tq,),
        in_specs=[pl.BlockSpec((tq, XA_W), lambda i: (i, QC_CB)), pl.BlockSpec(kv.shape, lambda i: (0, 0))],
        out_specs=pl.BlockSpec((tq, XA_W), lambda i: (i, 0)),
        out_shape=jax.ShapeDtypeStruct((s, XA_W), F32),
        compiler_params=_cparams(("parallel",)),
    )(proj, kv)


def xa_bwd(do, proj, kv, dproj):
    s = proj.shape[0]
    tq = min(512, s)

    def body(do_ref, q_ref, kv_ref, buf_ref, dq_ref, dkv_ref):
        @pl.when(pl.program_id(0) == 0)
        def _():
            dkv_ref[...] = jnp.zeros_like(dkv_ref)

        for h in range(XA_HEADS):
            sl = slice(h * XA_DIM, (h + 1) * XA_DIM)
            sv = slice(XA_W + h * XA_DIM, XA_W + (h + 1) * XA_DIM)
            qh, kh, vh, doh = q_ref[:, sl], kv_ref[:, sl], kv_ref[:, sv], do_ref[:, sl]
            p = _xa_probs(qh, kh)
            dp = _dot(doh, vh, "nt")
            ds_ = p * (dp - jnp.sum(dp * p, axis=-1, keepdims=True)) * (XA_DIM ** -0.5)
            dq_ref[:, sl] = _dot(ds_, kh).astype(BF16)
            dkv_ref[:, sl] += _dot(ds_, qh, "tn")
            dkv_ref[:, sv] += _dot(p, doh, "tn")

    return pl.pallas_call(
        body, name="xa_bwd", grid=(s // tq,),
        in_specs=[pl.BlockSpec((tq, XA_W), lambda i: (i, 0)), pl.BlockSpec((tq, XA_W), lambda i: (i, QC_CB)),
                  pl.BlockSpec(kv.shape, lambda i: (0, 0)), pl.BlockSpec(memory_space=pl.ANY)],
        out_specs=[pl.BlockSpec((tq, XA_W), lambda i: (i, QC_CB)), pl.BlockSpec(kv.shape, lambda i: (0, 0))],
        out_shape=[jax.ShapeDtypeStruct(dproj.shape, dproj.dtype), jax.ShapeDtypeStruct(kv.shape, F32)],
        input_output_aliases={3: 0}, compiler_params=_cparams(("arbitrary",)),
    )(do, proj, kv, dproj)


def _adamw_math(wv, gv, mv, vv):
    m2 = ADAM_B1 * mv + (1.0 - ADAM_B1) * gv
    v2 = ADAM_B2 * vv + (1.0 - ADAM_B2) * (gv * gv)
    m_hat = m2 / (1.0 - ADAM_B1 ** ADAM_STEP)
    v_hat = v2 / (1.0 - ADAM_B2 ** ADAM_STEP)
    return -ADAM_LR * (m_hat / (jnp.sqrt(v_hat) + ADAM_EPS) + ADAM_WD * wv), m2, v2


def adamw(w, g, m, v, name):
    lead = (0,) * (w.ndim - 2)
    rows, cols = w.shape[-2:]
    tr = rows
    while tr * cols * 4 * 7 * 2 > 36 * 2 ** 20 and tr % 16 == 0:
        tr //= 2

    def body(w_ref, g_ref, m_ref, v_ref, d_ref, m2_ref, v2_ref):
        d_ref[...], m2_ref[...], v2_ref[...] = _adamw_math(w_ref[...], g_ref[...], m_ref[...], v_ref[...])

    spec = pl.BlockSpec((1,) * len(lead) + (tr, cols), lambda i: lead + (i, 0))
    return pl.pallas_call(
        body, name=name, grid=(rows // tr,), in_specs=[spec] * 4, out_specs=[spec] * 3,
        out_shape=[jax.ShapeDtypeStruct(w.shape, F32)] * 3, compiler_params=_cparams(("parallel",)),
    )(w, g, m, v)


def _seg_perm(a):
    s, w = a.shape
    return a.reshape(NSEG, s // NSEG, w).transpose(1, 0, 2).reshape(s, w)


def _seg_unperm(a):
    s, w = a.shape
    return a.reshape(s // NSEG, NSEG, w).transpose(1, 0, 2).reshape(s, w)


def _block_diag(t):
    nt, _, r, c = t.shape
    eye = jnp.eye(8, dtype=bool)
    return jnp.where(eye[None, :, None, :, None], t[:, :, :, None, :], 0.0).reshape(nt, 8 * r, 8 * c)


def _block_diag_inv(d, r, c):
    d5 = d.reshape(d.shape[0], 8, r, 8, c)
    return jnp.stack([d5[:, g, :, g, :] for g in range(8)], axis=1)


def _s5_b_tiles(bb):
    return _block_diag(bb.reshape(S5_TILES, 8, S5_STATE, S5_GROUP).transpose(0, 1, 3, 2))


def _s5_b_untile(d):
    return _block_diag_inv(d, S5_GROUP, S5_STATE).transpose(0, 1, 3, 2).reshape(S5_GROUPS, S5_STATE * S5_GROUP)


def _s5_c_tiles(c):
    return _block_diag(c.reshape(S5_TILES, 8, S5_GROUP, S5_STATE).transpose(0, 1, 3, 2))


def _s5_c_untile(d):
    return _block_diag_inv(d, S5_STATE, S5_GROUP).transpose(0, 1, 3, 2).reshape(S5_GROUPS, S5_GROUP, S5_STATE)


def s5_ssm_fwd(xb, lam_re, lam_im, log_dt, b_re, b_im, c_re, c_im):
    br, bi = b_re.reshape(S5_GROUPS, -1), b_im.reshape(S5_GROUPS, -1)
    ldt = log_dt.reshape(S5_GROUPS, 1)
    ab_re, ab_im, bb_re, bb_im = s5_param_fwd(lam_re, lam_im, ldt, br, bi)
    a_re, a_im = ab_re.reshape(1, S5_W), ab_im.reshape(1, S5_W)
    bre, bim = _s5_b_tiles(bb_re).astype(BF16), _s5_b_tiles(bb_im).astype(BF16)
    cre, cim = _s5_c_tiles(c_re).astype(BF16), _s5_c_tiles(c_im).astype(BF16)
    xp = _seg_perm(xb)
    yp, hs_r, hs_i = s5_scan_fwd(xp, a_re, a_im, bre, bim, cre, cim)
    saved = (xp, a_re, a_im, bre, bim, cre, cim, hs_r, hs_i, (lam_re, lam_im, ldt, br, bi))
    return _seg_unperm(yp), saved


def s5_ssm_bwd(dy, saved):
    xp, a_re, a_im, bre, bim, cre, cim, hs_r, hs_i, params = saved
    cre_t, cim_t = cre.transpose(0, 2, 1), cim.transpose(0, 2, 1)
    dxp, dar, dai, dbr, dbi, dcr, dci = s5_scan_bwd(_seg_perm(dy), xp, a_re, a_im, bre, bim, cre_t, cim_t, hs_r, hs_i)
    dlr, dli, dldt, db_re, db_im = s5_param_bwd(*params, dar.reshape(S5_GROUPS, S5_STATE), dai.reshape(S5_GROUPS, S5_STATE),
                                                _s5_b_untile(dbr), _s5_b_untile(dbi))
    shape_b = (S5_GROUPS, S5_STATE, S5_GROUP)
    return (_seg_unperm(dxp), dlr, dli, dldt.reshape(S5_GROUPS), db_re.reshape(shape_b), db_im.reshape(shape_b),
            _s5_c_untile(dcr), _s5_c_untile(dci))


_MESH = pl.DeviceIdType.MESH
_HBM = pl.BlockSpec(memory_space=pltpu.HBM)
N_DEV = 8


def _position():
    return lax.axis_index("x"), lax.axis_index("y"), lax.axis_index("c")


D2D_CHUNK_BYTES = 2 ** 20


def _chunk_rows(rows, cols, itemsize):
    return _row_tile(rows, 16, max(16, D2D_CHUNK_BYTES // (cols * itemsize)))


def _rows(start, size, unit=16):
    return pl.ds(pl.multiple_of(start, unit), size)


def _push_to_sibling(chunks, stages, recv_sems, store_sems, sibling, lag=2):
    in_slot, used, stores = {}, {}, []

    def push(q, slot):
        _, _, sid, land, _ = chunks[q]
        buf, send_sems, _ = stages[sid]
        return pltpu.make_async_remote_copy(src_ref=buf.at[slot], dst_ref=land, send_sem=send_sems.at[slot],
                                            recv_sem=recv_sems.at[q], device_id=sibling, device_id_type=_MESH)

    def receive(q):
        push(q, 0).wait_recv()
        st = pltpu.make_async_copy(chunks[q][3], chunks[q][4], store_sems.at[q])
        st.start()
        stores.append(st)

    for q, (pre, src, sid, _, _) in enumerate(chunks):
        if pre is not None:
            pre()
        slot = used.get(sid, 0) % 2
        used[sid] = used.get(sid, 0) + 1
        if (sid, slot) in in_slot:
            in_slot.pop((sid, slot)).wait_send()
        load = pltpu.make_async_copy(src, stages[sid][0].at[slot], stages[sid][2].at[slot])
        load.start()
        load.wait()
        cp = push(q, slot)
        cp.start()
        in_slot[(sid, slot)] = cp
        if q >= lag:
            receive(q - lag)
    for q in range(max(0, len(chunks) - lag), len(chunks)):
        receive(q)
    for cp in in_slot.values():
        cp.wait_send()
    for st in stores:
        st.wait()


def _stage_scratch(shapes_dtypes):
    out = []
    for shape, dtype in shapes_dtypes:
        out += [pltpu.VMEM((2,) + shape, dtype), pltpu.SemaphoreType.DMA((2,)), pltpu.SemaphoreType.DMA((2,))]
    return out


def allgather_weights(ws, convw, name):
    n = len(ws)
    extra = 0 if convw is None else 1
    halves = [w.shape[0] // 2 for w in ws]
    steps = [_chunk_rows(h, w.shape[1], w.dtype.itemsize) for h, w in zip(halves, ws)]
    per_peer = [h // s for h, s in zip(halves, steps)]
    nchunks = 3 * sum(per_peer)

    def body(*refs):
        w_refs = refs[:n]
        wo_refs = refs[n + extra:2 * n + extra]
        scratch = refs[2 * (n + extra):]
        send_sems, recv_sems, local_sems, fwd_recv_sems, store_sems = scratch[:5]
        lands = scratch[5:5 + n]
        stage_refs = scratch[5 + n:]
        stages = [tuple(stage_refs[3 * i:3 * i + 3]) for i in range(n)]
        x, y, c = _position()
        mine = 2 * x + y
        peers = [(1 - x, y), (x, 1 - y), (1 - x, 1 - y)]
        blocks = [2 * px + py for px, py in peers]
        local = [pltpu.make_async_copy(w_refs[i], wo_refs[i].at[mine], local_sems.at[i]) for i in range(n)]
        if extra:
            c_ref, co_ref = refs[n], refs[2 * n + 1]
            local.append(pltpu.make_async_copy(c_ref, co_ref.at[mine], local_sems.at[n]))
        for cp in local:
            cp.start()

        def ici(i, k, block):
            rows = _rows(c * halves[i], halves[i])
            return pltpu.make_async_remote_copy(src_ref=w_refs[i].at[rows, :], dst_ref=wo_refs[i].at[block, rows, :],
                                                send_sem=send_sems.at[3 * i + k], recv_sem=recv_sems.at[3 * i + k],
                                                device_id=(*peers[k], c), device_id_type=_MESH)

        def conv(k, block):
            return pltpu.make_async_remote_copy(src_ref=c_ref, dst_ref=co_ref.at[block], send_sem=send_sems.at[3 * n + k],
                                                recv_sem=recv_sems.at[3 * n + k], device_id=(*peers[k], c), device_id_type=_MESH)

        sends = [ici(i, k, mine) for k in range(3) for i in range(n)] + ([conv(k, mine) for k in range(3)] if extra else [])
        for cp in sends:
            cp.start()
        chunks = []
        for k in range(3):
            for i in range(n):
                for q in range(per_peer[i]):
                    pre = functools.partial(lambda i, k: ici(i, k, blocks[k]).wait_recv(), i, k) if q == 0 else None
                    src = wo_refs[i].at[blocks[k], _rows(c * halves[i] + q * steps[i], steps[i]), :]
                    out = wo_refs[i].at[blocks[k], _rows((1 - c) * halves[i] + q * steps[i], steps[i]), :]
                    chunks.append((pre, src, i, lands[i].at[k * per_peer[i] + q], out))
        _push_to_sibling(chunks, stages, fwd_recv_sems, store_sems, (x, y, 1 - c))
        if extra:
            for k in range(3):
                conv(k, blocks[k]).wait_recv()
        for cp in sends:
            cp.wait_send()
        for cp in local:
            cp.wait()

    nsem = 3 * (n + extra)
    scratch = [pltpu.SemaphoreType.DMA((nsem,)), pltpu.SemaphoreType.DMA((nsem,)), pltpu.SemaphoreType.DMA((n + extra,)),
               pltpu.SemaphoreType.DMA((nchunks,)), pltpu.SemaphoreType.DMA((nchunks,))]
    scratch += [pltpu.VMEM((3 * p, s, w.shape[1]), w.dtype) for p, s, w in zip(per_peer, steps, ws)]
    scratch += _stage_scratch([((s, w.shape[1]), w.dtype) for s, w in zip(steps, ws)])
    operands = list(ws) + ([convw] if extra else [])
    return pl.pallas_call(
        body, name=name, in_specs=[_HBM] * len(operands), out_specs=[_HBM] * len(operands),
        out_shape=[jax.ShapeDtypeStruct((4,) + w.shape, w.dtype) for w in operands],
        scratch_shapes=scratch, compiler_params=pltpu.CompilerParams(vmem_limit_bytes=VMEM_LIMIT),
    )(*operands)


def exchange_cores(gs, name):
    n = len(gs)
    halves = [g.shape[1] // 2 for g in gs]
    steps = [_chunk_rows(h, g.shape[2], g.dtype.itemsize) for h, g in zip(halves, gs)]
    per_shard = [h // s for h, s in zip(halves, steps)]
    nchunks = 4 * sum(per_shard)

    def body(*refs):
        g_refs, got_refs, scratch = refs[:n], refs[n:2 * n], refs[2 * n:]
        recv_sems, store_sems = scratch[:2]
        lands = scratch[2:2 + n]
        stage_refs = scratch[2 + n:2 + 4 * n]
        stages = [tuple(stage_refs[3 * i:3 * i + 3]) for i in range(n)]
        x, y, c = _position()
        chunks = []
        for i in range(n):
            for j in range(4):
                for q in range(per_shard[i]):
                    src = g_refs[i].at[j, _rows((1 - c) * halves[i] + q * steps[i], steps[i]), :]
                    out = got_refs[i].at[j, pl.ds(q * steps[i], steps[i]), :]
                    chunks.append((None, src, i, lands[i].at[j * per_shard[i] + q], out))
        _push_to_sibling(chunks, stages, recv_sems, store_sems, (x, y, 1 - c))

    scratch = [pltpu.SemaphoreType.DMA((nchunks,)), pltpu.SemaphoreType.DMA((nchunks,))]
    scratch += [pltpu.VMEM((4 * p, s, g.shape[2]), g.dtype) for p, s, g in zip(per_shard, steps, gs)]
    scratch += _stage_scratch([((s, g.shape[2]), g.dtype) for s, g in zip(steps, gs)])
    return pl.pallas_call(
        body, name=name, in_specs=[_HBM] * n, out_specs=[_HBM] * n,
        out_shape=[jax.ShapeDtypeStruct((4, h, g.shape[2]), g.dtype) for h, g in zip(halves, gs)],
        scratch_shapes=scratch, compiler_params=pltpu.CompilerParams(vmem_limit_bytes=VMEM_LIMIT),
    )(*gs)


def chips_side(cs):
    n = len(cs)

    def copies(c_refs, o_refs, sems):
        send_sems, recv_sems, local_sems = sems
        x, y, c = _position()
        mine = 2 * x + y
        peers = [(1 - x, y), (x, 1 - y), (1 - x, 1 - y)]
        blocks = [2 * px + py for px, py in peers]
        local = [pltpu.make_async_copy(c_refs[i].at[mine], o_refs[i].at[mine], local_sems.at[i]) for i in range(n)]

        def copy(i, k, sending):
            return pltpu.make_async_remote_copy(src_ref=c_refs[i].at[blocks[k]], dst_ref=o_refs[i].at[mine if sending else blocks[k]],
                                                send_sem=send_sems.at[3 * i + k], recv_sem=recv_sems.at[3 * i + k],
                                                device_id=(*peers[k], c), device_id_type=_MESH)

        sends = [copy(i, k, True) for k in range(3) for i in range(n)]
        return local, sends, lambda: [copy(i, k, False) for k in range(3) for i in range(n)]

    def start(*refs):
        local, sends, _ = copies(*refs)
        for cp in local + sends:
            cp.start()

    def finish(*refs):
        local, sends, arrivals = copies(*refs)
        for cp in arrivals():
            cp.wait_recv()
        for cp in sends:
            cp.wait_send()
        for cp in local:
            cp.wait()

    scratch = [pltpu.SemaphoreType.DMA((3 * n,)), pltpu.SemaphoreType.DMA((3 * n,)), pltpu.SemaphoreType.DMA((n,))]
    return Side(list(cs), [jax.ShapeDtypeStruct(a.shape, a.dtype) for a in cs], scratch, start, finish)


def gather_side(ws, convw):
    n = len(ws)
    halves = [w.shape[0] // 2 for w in ws]

    def copies(in_refs, out_refs, sems):
        w_refs, c_ref, wo_refs, co_ref = in_refs[:n], in_refs[n], out_refs[:n], out_refs[n]
        send_sems, recv_sems, local_sems = sems
        x, y, c = _position()
        mine = 2 * x + y
        peers = [(1 - x, y), (x, 1 - y), (1 - x, 1 - y)]
        blocks = [2 * px + py for px, py in peers]
        local = [pltpu.make_async_copy(w_refs[i], wo_refs[i].at[mine], local_sems.at[i]) for i in range(n)]
        local.append(pltpu.make_async_copy(c_ref, co_ref.at[mine], local_sems.at[n]))

        def ici(i, k, block):
            rows = _rows(c * halves[i], halves[i])
            return pltpu.make_async_remote_copy(src_ref=w_refs[i].at[rows, :], dst_ref=wo_refs[i].at[block, rows, :],
                                                send_sem=send_sems.at[3 * i + k], recv_sem=recv_sems.at[3 * i + k],
                                                device_id=(*peers[k], c), device_id_type=_MESH)

        def conv(k, block):
            return pltpu.make_async_remote_copy(src_ref=c_ref, dst_ref=co_ref.at[block], send_sem=send_sems.at[3 * n + k],
                                                recv_sem=recv_sems.at[3 * n + k], device_id=(*peers[k], c), device_id_type=_MESH)

        sends = [ici(i, k, mine) for k in range(3) for i in range(n)] + [conv(k, mine) for k in range(3)]
        return local, sends, lambda: ([ici(i, k, blocks[k]) for k in range(3) for i in range(n)]
                                      + [conv(k, blocks[k]) for k in range(3)])

    def start(*refs):
        local, sends, _ = copies(*refs)
        for cp in local + sends:
            cp.start()

    def finish(*refs):
        local, sends, arrivals = copies(*refs)
        for cp in arrivals():
            cp.wait_recv()
        for cp in sends:
            cp.wait_send()
        for cp in local:
            cp.wait()

    nsem = 3 * n + 3
    scratch = [pltpu.SemaphoreType.DMA((nsem,)), pltpu.SemaphoreType.DMA((nsem,)), pltpu.SemaphoreType.DMA((n + 1,))]
    operands = list(ws) + [convw]
    return Side(operands, [jax.ShapeDtypeStruct((4,) + w.shape, w.dtype) for w in operands], scratch, start, finish)


def forward_halves(stacked):
    n = len(stacked)
    halves = [w.shape[1] // 2 for w in stacked]
    steps = [_chunk_rows(h, w.shape[2], w.dtype.itemsize) for h, w in zip(halves, stacked)]
    per_peer = [h // s for h, s in zip(halves, steps)]
    nchunks = 3 * sum(per_peer)

    def body(*refs):
        w_refs, o_refs = refs[:n], refs[n:2 * n]
        scratch = refs[2 * n:]
        recv_sems, store_sems = scratch[:2]
        lands = scratch[2:2 + n]
        stages = [tuple(scratch[2 + n + 3 * i:2 + n + 3 * i + 3]) for i in range(n)]
        x, y, c = _position()
        blocks = [2 * px + py for px, py in ((1 - x, y), (x, 1 - y), (1 - x, 1 - y))]
        chunks = []
        for k in range(3):
            for i in range(n):
                for q in range(per_peer[i]):
                    src = w_refs[i].at[blocks[k], _rows(c * halves[i] + q * steps[i], steps[i]), :]
                    out = o_refs[i].at[blocks[k], _rows((1 - c) * halves[i] + q * steps[i], steps[i]), :]
                    chunks.append((None, src, i, lands[i].at[k * per_peer[i] + q], out))
        _push_to_sibling(chunks, stages, recv_sems, store_sems, (x, y, 1 - c))

    scratch = [pltpu.SemaphoreType.DMA((nchunks,)), pltpu.SemaphoreType.DMA((nchunks,))]
    scratch += [pltpu.VMEM((3 * p, s, w.shape[2]), w.dtype) for p, s, w in zip(per_peer, steps, stacked)]
    scratch += _stage_scratch([((s, w.shape[2]), w.dtype) for s, w in zip(steps, stacked)])
    return pl.pallas_call(
        body, name="forward_halves", in_specs=[_HBM] * n, out_specs=[_HBM] * n,
        out_shape=[jax.ShapeDtypeStruct(w.shape, w.dtype) for w in stacked], input_output_aliases={i: i for i in range(n)},
        scratch_shapes=scratch, compiler_params=pltpu.CompilerParams(vmem_limit_bytes=VMEM_LIMIT),
    )(*stacked)


def small_side(small):
    def copies(in_refs, out_refs, sems):
        (s_ref,), (so_ref,), (send_sems, recv_sems, local_sem) = in_refs, out_refs, sems
        x, y, c = _position()
        me = 4 * x + 2 * y + c
        local = pltpu.make_async_copy(s_ref, so_ref.at[me], local_sem)

        def copy(r, sending):
            px, py, pc = (1 - x if r & 4 else x, 1 - y if r & 2 else y, 1 - c if r & 1 else c)
            slot = me if sending else 4 * px + 2 * py + pc
            return pltpu.make_async_remote_copy(src_ref=s_ref, dst_ref=so_ref.at[slot], send_sem=send_sems.at[r - 1],
                                                recv_sem=recv_sems.at[r - 1], device_id=(px, py, pc), device_id_type=_MESH)

        return local, [copy(r, True) for r in range(1, N_DEV)], lambda: [copy(r, False) for r in range(1, N_DEV)]

    def start(*refs):
        local, sends, _ = copies(*refs)
        for cp in [local] + sends:
            cp.start()

    def finish(*refs):
        local, sends, arrivals = copies(*refs)
        for cp in arrivals():
            cp.wait_recv()
        for cp in sends:
            cp.wait_send()
        local.wait()

    scratch = [pltpu.SemaphoreType.DMA((N_DEV - 1,)), pltpu.SemaphoreType.DMA((N_DEV - 1,)), pltpu.SemaphoreType.DMA]
    return Side([small], [jax.ShapeDtypeStruct((N_DEV,) + small.shape, small.dtype)], scratch, start, finish)


def combine_sides(a, b):
    na, oa, sa = len(a.operands), len(a.out_shapes), len(a.scratch)

    def split(ins, outs, scr):
        return (ins[:na], outs[:oa], scr[:sa]), (ins[na:], outs[oa:], scr[sa:])

    def start(*refs):
        ra, rb = split(*refs)
        a.start(*ra)
        b.start(*rb)

    def finish(*refs):
        ra, rb = split(*refs)
        a.finish(*ra)
        b.finish(*rb)

    return Side(a.operands + b.operands, a.out_shapes + b.out_shapes, a.scratch + b.scratch, start, finish)


def exchange_small(small):
    side = small_side(small)

    def body(s_ref, so_ref, *sems):
        side.start((s_ref,), (so_ref,), sems)
        side.finish((s_ref,), (so_ref,), sems)

    return pl.pallas_call(
        body, name="exchange_small", in_specs=[_HBM], out_specs=_HBM, out_shape=side.out_shapes[0], scratch_shapes=side.scratch,
    )(small)


def pair_sum(core, g, got, name):
    nb, rows, cols = got.shape
    tr = _row_tile(rows, 16, max(16, (2 * 2 ** 20) // (cols * g.dtype.itemsize)))
    nblk = rows // tr

    def body(c_ref, a_ref, b_ref, o_ref):
        o_ref[...] = (a_ref[...].astype(F32) + b_ref[...].astype(F32)).astype(o_ref.dtype)

    spec = pl.BlockSpec((1, tr, cols), lambda j, i, c_ref: (j, i, 0))
    mine = pl.BlockSpec((1, tr, cols), lambda j, i, c_ref: (j, c_ref[0] * nblk + i, 0))
    return pl.pallas_call(
        body, name=name,
        grid_spec=pltpu.PrefetchScalarGridSpec(num_scalar_prefetch=1, grid=(nb, nblk), in_specs=[mine, spec], out_specs=spec),
        out_shape=jax.ShapeDtypeStruct(got.shape, g.dtype), compiler_params=_cparams(("parallel", "parallel")),
    )(core, g, got)


def sum_chips(core, pieces, name):
    nb, rows, cols = pieces.shape
    tr = _row_tile(rows, 16, max(16, (6 * 2 ** 20) // (nb * cols * pieces.dtype.itemsize)))
    nblk = rows // tr

    def body(c_ref, p_ref, o_ref):
        acc = p_ref[0].astype(F32)
        for i in range(1, nb):
            acc = acc + p_ref[i].astype(F32)
        o_ref[0] = acc

    return pl.pallas_call(
        body, name=name,
        grid_spec=pltpu.PrefetchScalarGridSpec(
            num_scalar_prefetch=1, grid=(nblk,),
            in_specs=[pl.BlockSpec((nb, tr, cols), lambda i, c_ref: (0, i, 0))],
            out_specs=pl.BlockSpec((1, tr, cols), lambda i, c_ref: (0, c_ref[0] * nblk + i, 0))),
        out_shape=jax.ShapeDtypeStruct((1, 2 * rows, cols), F32), compiler_params=_cparams(("parallel",)),
    )(core, pieces)


def sibling_exchange(fulls):
    n = len(fulls)
    halves = [f.shape[1] // 2 for f in fulls]
    steps = [_chunk_rows(h, f.shape[2], f.dtype.itemsize) for h, f in zip(halves, fulls)]
    counts = [h // s for h, s in zip(halves, steps)]
    nchunks = sum(counts)

    def body(*refs):
        f_refs, o_refs = refs[:n], refs[n:2 * n]
        scratch = refs[2 * n:]
        recv_sems, store_sems = scratch[:2]
        lands = scratch[2:2 + n]
        stages = [tuple(scratch[2 + n + 3 * i:2 + n + 3 * i + 3]) for i in range(n)]
        x, y, c = _position()
        chunks = []
        for i in range(n):
            for q in range(counts[i]):
                src = f_refs[i].at[0, _rows(c * halves[i] + q * steps[i], steps[i]), :]
                out = o_refs[i].at[0, _rows((1 - c) * halves[i] + q * steps[i], steps[i]), :]
                chunks.append((None, src, i, lands[i].at[q], out))
        _push_to_sibling(chunks, stages, recv_sems, store_sems, (x, y, 1 - c))

    scratch = [pltpu.SemaphoreType.DMA((nchunks,)), pltpu.SemaphoreType.DMA((nchunks,))]
    scratch += [pltpu.VMEM((k, s, f.shape[2]), f.dtype) for k, s, f in zip(counts, steps, fulls)]
    scratch += _stage_scratch([((s, f.shape[2]), f.dtype) for s, f in zip(steps, fulls)])
    return pl.pallas_call(
        body, name="sibling_exchange", in_specs=[_HBM] * n, out_specs=[_HBM] * n,
        out_shape=[jax.ShapeDtypeStruct(f.shape, f.dtype) for f in fulls],
        input_output_aliases={i: i for i in range(n)},
        scratch_shapes=scratch, compiler_params=pltpu.CompilerParams(vmem_limit_bytes=VMEM_LIMIT),
    )(*fulls)


def _row_tile(rows, unit, max_rows):
    best = unit
    for t in range(unit, min(rows, max_rows) + 1, unit):
        if rows % t == 0:
            best = t
    return best


def sum_pieces(pieces, name):
    n, rows, cols = pieces.shape
    tr = _row_tile(rows, 16, max(16, (6 * 2 ** 20) // (n * cols * pieces.dtype.itemsize)))

    def body(p_ref, o_ref):
        acc = p_ref[0].astype(F32)
        for i in range(1, n):
            acc = acc + p_ref[i].astype(F32)
        o_ref[...] = acc

    return pl.pallas_call(
        body, name=name, grid=(rows // tr,),
        in_specs=[pl.BlockSpec((n, tr, cols), lambda i: (0, i, 0))], out_specs=pl.BlockSpec((tr, cols), lambda i: (i, 0)),
        out_shape=jax.ShapeDtypeStruct((rows, cols), F32), compiler_params=_cparams(("parallel",)),
    )(pieces)


BIG = ("w_in", "s5_w_glu", "w_kv_mem", "w_br_a", "w_br_b", "w_br_c", "w_out")
COL_SHARDED = ("w_in", "s5_w_glu", "w_br_a", "w_br_b", "w_br_c")
SMALL = ("gdn_a_log", "gdn_dt_bias", "gdn_norm_g", "s5_lambda_re", "s5_lambda_im", "s5_log_dt",
         "s5_b_re", "s5_b_im", "s5_c_re", "s5_c_im", "s5_d", "mem_norm_g", "final_g", "norm_g")
WEIGHTS = ("norm_g", "w_in", "conv_w", "gdn_a_log", "gdn_dt_bias", "gdn_norm_g", "s5_lambda_re", "s5_lambda_im",
           "s5_log_dt", "s5_b_re", "s5_b_im", "s5_c_re", "s5_c_im", "s5_d", "s5_w_glu", "mem_norm_g", "w_kv_mem",
           "w_br_a", "w_br_b", "w_br_c", "w_out", "final_g")
W_IN_SPLIT = 4096


W_IN_COLS = PROJ_W - BA_PAD + 2 * NHEAD
W_IN_GATES = W_IN_COLS - GATE_W
W_IN_MOVES = ((0, W_IN_SPLIT, GATE_W), (W_IN_SPLIT, W_IN_SPLIT + 2 * NHEAD, PROJ_W - BA_PAD - W_IN_SPLIT),
              (W_IN_SPLIT + 2 * NHEAD, W_IN_GATES, GATE_W - 2 * NHEAD), (W_IN_GATES, W_IN_COLS, -W_IN_GATES))


def _pack_w_in(shards):
    cs = shards.shape[2]
    parts = []
    for a, b, _ in sorted(W_IN_MOVES, key=lambda move: move[0] + move[2]):
        while a < b:
            j = a // cs
            hi = min(b, (j + 1) * cs)
            parts.append(shards[j, :, a - j * cs:hi - j * cs])
            a = hi
    parts.append(jnp.zeros((shards.shape[1], BA_PAD - 2 * NHEAD), shards.dtype))
    return jnp.concatenate(parts, axis=1)


def _unpack_w_in(wp):
    cs = W_IN_COLS // 4
    shards = []
    for j in range(4):
        parts = []
        for lo, hi, shift in W_IN_MOVES:
            s, e = max(j * cs, lo), min((j + 1) * cs, hi)
            if s < e:
                parts.append(wp[:, s + shift:e + shift])
        shards.append(jnp.concatenate(parts, axis=1))
    return jnp.stack(shards)


def _pack_small(arrs):
    parts = []
    for a in arrs:
        f = a.reshape(-1).astype(F32)
        parts.append(jnp.pad(f, (0, (-f.shape[0]) % 128)))
    flat = jnp.concatenate(parts)
    rows = flat.shape[0] // 128
    return jnp.pad(flat.reshape(rows, 128), ((0, (-rows) % 16), (0, 0)))


def _unpack_small(flat2d, shapes):
    f = flat2d.reshape(-1)
    out, off = [], 0
    for shp in shapes:
        n = math.prod(shp)
        out.append(f[off:off + n].reshape(shp))
        off += n + (-n) % 128
    return out


def kernel(x, mem, norm_g, w_in, conv_w, gdn_a_log, gdn_dt_bias, gdn_norm_g, s5_lambda_re, s5_lambda_im, s5_log_dt, s5_b_re, s5_b_im, s5_c_re, s5_c_im, s5_d, s5_w_glu, mem_norm_g, w_kv_mem, w_br_a, w_br_b, w_br_c, w_out, final_g, loss_target, m_norm_g, m_w_in, m_conv_w, m_gdn_a_log, m_gdn_dt_bias, m_gdn_norm_g, m_s5_lambda_re, m_s5_lambda_im, m_s5_log_dt, m_s5_b_re, m_s5_b_im, m_s5_c_re, m_s5_c_im, m_s5_d, m_s5_w_glu, m_mem_norm_g, m_w_kv_mem, m_w_br_a, m_w_br_b, m_w_br_c, m_w_out, m_final_g, v_norm_g, v_w_in, v_conv_w, v_gdn_a_log, v_gdn_dt_bias, v_gdn_norm_g, v_s5_lambda_re, v_s5_lambda_im, v_s5_log_dt, v_s5_b_re, v_s5_b_im, v_s5_c_re, v_s5_c_im, v_s5_d, v_s5_w_glu, v_mem_norm_g, v_w_kv_mem, v_w_br_a, v_w_br_b, v_w_br_c, v_w_out, v_final_g):
    wts = dict(norm_g=norm_g, w_in=w_in, conv_w=conv_w, gdn_a_log=gdn_a_log, gdn_dt_bias=gdn_dt_bias, gdn_norm_g=gdn_norm_g,
               s5_lambda_re=s5_lambda_re, s5_lambda_im=s5_lambda_im, s5_log_dt=s5_log_dt, s5_b_re=s5_b_re, s5_b_im=s5_b_im,
               s5_c_re=s5_c_re, s5_c_im=s5_c_im, s5_d=s5_d, s5_w_glu=s5_w_glu, mem_norm_g=mem_norm_g, w_kv_mem=w_kv_mem,
               w_br_a=w_br_a, w_br_b=w_br_b, w_br_c=w_br_c, w_out=w_out, final_g=final_g)
    mom = dict(norm_g=m_norm_g, w_in=m_w_in, conv_w=m_conv_w, gdn_a_log=m_gdn_a_log, gdn_dt_bias=m_gdn_dt_bias,
               gdn_norm_g=m_gdn_norm_g, s5_lambda_re=m_s5_lambda_re, s5_lambda_im=m_s5_lambda_im, s5_log_dt=m_s5_log_dt,
               s5_b_re=m_s5_b_re, s5_b_im=m_s5_b_im, s5_c_re=m_s5_c_re, s5_c_im=m_s5_c_im, s5_d=m_s5_d, s5_w_glu=m_s5_w_glu,
               mem_norm_g=m_mem_norm_g, w_kv_mem=m_w_kv_mem, w_br_a=m_w_br_a, w_br_b=m_w_br_b, w_br_c=m_w_br_c, w_out=m_w_out,
               final_g=m_final_g)
    vel = dict(norm_g=v_norm_g, w_in=v_w_in, conv_w=v_conv_w, gdn_a_log=v_gdn_a_log, gdn_dt_bias=v_gdn_dt_bias,
               gdn_norm_g=v_gdn_norm_g, s5_lambda_re=v_s5_lambda_re, s5_lambda_im=v_s5_lambda_im, s5_log_dt=v_s5_log_dt,
               s5_b_re=v_s5_b_re, s5_b_im=v_s5_b_im, s5_c_re=v_s5_c_re, s5_c_im=v_s5_c_im, s5_d=v_s5_d, s5_w_glu=v_s5_w_glu,
               mem_norm_g=v_mem_norm_g, w_kv_mem=v_w_kv_mem, w_br_a=v_w_br_a, w_br_b=v_w_br_b, w_br_c=v_w_br_c, w_out=v_w_out,
               final_g=v_final_g)
    x2, mem2, tgt = x[0], mem[0], loss_target[0]
    s, d = x2.shape
    n_chunks = s // CHUNK

    shards = [wts[n][0].astype(BF16) for n in BIG]
    wp = _pack_w_in(allgather_weights(shards[:1], None, "allgather_w_in")[0])
    mm = functools.partial(matmul, tm=1024, tn=1024)
    u, r1 = rms_fwd(x2, norm_g, "rms_fwd_x")
    proj, *rest, cg = matmul(u, wp, mode="nn", out_dtype=F32, tm=2048, tn=1024, tk=2048, name="mm_proj",
                             side=gather_side(shards[1:], conv_w[0]))
    full = {}
    for n, wg in zip(BIG[1:], forward_halves(rest)):
        rows, cols = wg.shape[1:]
        full[n] = wg.transpose(1, 0, 2).reshape(rows, 4 * cols) if n in COL_SHARDED else wg.reshape(4 * rows, cols)
    conv_full = cg.transpose(1, 0, 2).reshape(conv_w.shape[1], -1)
    alog_pad = jnp.pad(gdn_a_log, ((0, 0), (NHEAD, BA_W - 2 * NHEAD)))
    dt_pad = jnp.pad(gdn_dt_bias, ((0, 0), (NHEAD, BA_W - 2 * NHEAD)))

    q, k, v, bg, gcol, gt = gdn_prep_fwd(proj, conv_full, alog_pad, dt_pad)
    gt3 = gt.reshape(BA_W, n_chunks, CHUNK).transpose(1, 0, 2)
    gu, gw, qd, kd, qk, tinv = gdn_intra_fwd(q, k, v, bg, gcol, gt3)
    o_raw, states = gdn_seq_fwd(gu, gw, qd, kd, qk, gt3)
    ga = gdn_out_fwd(o_raw, proj, ZA_CB, gdn_norm_g)

    xb = proj[:, XB_CB * S5_IN:(XB_CB + 1) * S5_IN]
    y_ssm, s5_saved = s5_ssm_fwd(xb, s5_lambda_re[0], s5_lambda_im[0], s5_log_dt[0], s5_b_re[0], s5_b_im[0],
                                 s5_c_re[0], s5_c_im[0])
    yb = s5_act_fwd(y_ssm, proj, XB_CB, s5_d)
    glu = mm(yb, full["s5_w_glu"], mode="nn", out_dtype=BF16, tk=1024, name="mm_glu")
    gb = s5_glu_fwd(glu, proj, ZB_CB)

    mem_n, rm = rms_fwd(mem2, mem_norm_g, "rms_fwd_mem")
    kv = mm(mem_n, full["w_kv_mem"], mode="nn", out_dtype=BF16, tk=2048, name="mm_kv")
    o_c = xa_fwd(proj, kv)
    gcx = gate_fwd(o_c, proj, ZC_CB, "gate_fwd_c")

    pa = mm(ga, full["w_br_a"], mode="nn", out_dtype=BF16, tk=1024, name="mm_pa")
    pb = mm(gb, full["w_br_b"], mode="nn", out_dtype=BF16, tk=1024, name="mm_pb")
    pc = mm(gcx, full["w_br_c"], mode="nn", out_dtype=BF16, tk=1024, name="mm_pc")
    merged = merge_fwd(pa, pb, pc, proj, GATE_CB)
    hres = mm(merged, full["w_out"], mode="nn", out_dtype=F32, tk=2048, name="mm_out")
    dh, dhb, loss_part, d_final_g = final_stage(x2, hres, tgt, final_g.reshape(1, d))

    gfull = {}
    dmerged = mm(dhb, full["w_out"], mode="nt", out_dtype=BF16, tk=2048, name="mm_dmerged")
    gfull["w_out"] = mm(merged, dhb, mode="tn", out_dtype=BF16, tk=1024, name="mm_dw_out")
    dproj = lax.empty((s, PROJ_W), BF16)
    dpa, dpb, dpc, dproj = merge_bwd(dmerged, pa, pb, pc, proj, GATE_CB, dproj)
    dga = mm(dpa, full["w_br_a"], mode="nt", out_dtype=BF16, tk=2048, name="mm_dga")
    dgb = mm(dpb, full["w_br_b"], mode="nt", out_dtype=BF16, tk=2048, name="mm_dgb")
    dgc = mm(dpc, full["w_br_c"], mode="nt", out_dtype=BF16, tk=2048, name="mm_dgc")
    gfull["w_br_a"] = mm(ga, dpa, mode="tn", out_dtype=BF16, tk=1024, name="mm_dw_a")
    gfull["w_br_b"] = mm(gb, dpb, mode="tn", out_dtype=BF16, tk=1024, name="mm_dw_b")
    gfull["w_br_c"] = mm(gcx, dpc, mode="tn", out_dtype=BF16, tk=1024, name="mm_dw_c")

    do_raw, dproj, d_gdn_norm = gdn_out_bwd(dga, o_raw, proj, ZA_CB, gdn_norm_g, dproj)
    du_, dw_, dqd, dkd, dqk, dgl = gdn_seq_bwd(do_raw, gu, gw, qd, kd, qk, gt3, states)
    dq, dk, dv, dbg = gdn_intra_bwd(q, k, v, bg, gcol, gt3, tinv, du_, dw_, dqd, dkd, dqk, dgl)
    dc, dproj, dcw0, dcw1, dcw2, dcw3, d_alog, d_dt = gdn_prep_bwd1(proj, conv_full, alog_pad, dt_pad, dq, dk, dv, dbg, dproj)
    dproj = gdn_prep_bwd2(dc, conv_full, dproj)
    d_conv = jnp.concatenate([dcw0, dcw1, dcw2, dcw3], axis=0)

    dval, dgate, dproj = s5_glu_bwd(dgb, glu, proj, ZB_CB, dproj)
    dglu = jnp.concatenate([dval, dgate], axis=1)
    dyb = mm(dglu, full["s5_w_glu"], mode="nt", out_dtype=BF16, tk=2048, name="mm_dyb")
    gfull["s5_w_glu"] = mm(yb, dglu, mode="tn", out_dtype=BF16, tk=1024, name="mm_dw_glu")
    dy_ssm, dxb_direct, d_s5_d = s5_act_bwd(dyb, y_ssm, proj, XB_CB, s5_d)
    dxb_scan, d_lre, d_lim, d_ldt, d_bre, d_bim, d_cre, d_cim = s5_ssm_bwd(dy_ssm, s5_saved)
    dproj = add_into(dxb_direct, dxb_scan, "s5_dxb", dproj, XB_CB)

    do_c, dproj = gate_bwd(dgc, o_c, proj, ZC_CB, "gate_bwd_c", dproj)
    dproj, dkv = xa_bwd(do_c, proj, kv, dproj)
    gfull["w_kv_mem"] = mm(mem_n, dkv, mode="tn", out_dtype=BF16, tk=256, name="mm_dw_kv")
    dmem_n = mm(dkv, full["w_kv_mem"], mode="nt", out_dtype=F32, tk=2048, name="mm_dmem")
    d_mem_norm = rms_bwd_g(dmem_n, mem2, rm, "rms_bwd_mem")

    core = lax.axis_index("c").astype(jnp.int32).reshape(1)
    by_shard = []
    for n in BIG[1:]:
        rows, cols = wts[n].shape[1:]
        g = gfull[n]
        by_shard.append(g.reshape(rows, 4, cols).transpose(1, 0, 2) if n in COL_SHARDED else g.reshape(4, rows, cols))
    got_rest = exchange_cores(by_shard, "exchange_cores_rest")
    chip_rest = [pair_sum(core, g, r, "sum_cores_" + n) for n, g, r in zip(BIG[1:], by_shard, got_rest)]

    small_g = dict(gdn_a_log=d_alog[:, NHEAD:2 * NHEAD], gdn_dt_bias=d_dt[:, NHEAD:2 * NHEAD], gdn_norm_g=d_gdn_norm,
                   s5_lambda_re=d_lre, s5_lambda_im=d_lim, s5_log_dt=d_ldt, s5_b_re=d_bre, s5_b_im=d_bim,
                   s5_c_re=d_cre, s5_c_im=d_cim, s5_d=d_s5_d, mem_norm_g=d_mem_norm, final_g=d_final_g)
    early = SMALL[:-1]
    small_send = _pack_small([small_g[n] for n in early] + [d_conv, loss_part])
    dwp, *from_chips_rest, got_small = matmul(u.T, dproj, mode="nn", out_dtype=BF16, tm=2048, tn=1024, tk=1024, name="mm_dw_in",
                                              side=combine_sides(chips_side(chip_rest), small_side(small_send)))
    w_in_shards = _unpack_w_in(dwp)
    got_in, = exchange_cores([w_in_shards], "exchange_cores_w_in")
    chip_in = pair_sum(core, w_in_shards, got_in, "sum_cores_w_in")
    du, from_chips_in = matmul(dproj, wp, mode="nt", out_dtype=F32, tm=2048, tn=1024, tk=512, name="mm_du",
                               side=chips_side([chip_in]))
    grad_x, d_norm_g = rms_bwd_x(du, x2, r1, norm_g, dh)
    from_chips = [from_chips_in] + from_chips_rest
    fulls = [sum_chips(core, a, "sum_chips_" + n) for n, a in zip(BIG, from_chips)]
    small_sum = sum_pieces(got_small, "sum_small")
    norm_sum = sum_pieces(exchange_small(_pack_small([d_norm_g])), "sum_norm_g")
    grads = dict(zip(BIG, sibling_exchange(fulls)))
    small_shapes = [wts[n].shape for n in early] + [d_conv.shape, (1, 1)]
    *small_list, conv_g_full, loss_sum = _unpack_small(small_sum, small_shapes)
    grads.update(zip(early, small_list))
    grads["norm_g"], = _unpack_small(norm_sum, [norm_g.shape])
    cw = conv_w.shape[2]
    shard_idx = 2 * lax.axis_index("x") + lax.axis_index("y")
    grads["conv_w"] = lax.dynamic_slice(conv_g_full, (0, shard_idx * cw), (conv_w.shape[1], cw))[None]

    delta, new_m, new_v = {}, {}, {}
    for n in BIG + ("conv_w",):
        delta[n], new_m[n], new_v[n] = adamw(wts[n], grads[n], mom[n], vel[n], "adamw_" + n)
    res = adamw(*[_pack_small([src[n] for n in SMALL]) for src in (wts, grads, mom, vel)], "adamw_small")
    shapes = [wts[n].shape for n in SMALL]
    for dst, flat in zip((delta, new_m, new_v), res):
        dst.update(zip(SMALL, _unpack_small(flat, shapes)))
    for n in SMALL:
        grads[n] = grads[n].reshape(wts[n].shape)

    return (loss_sum.reshape(()), grad_x.reshape(x.shape), *[grads[n] for n in WEIGHTS], *[delta[n] for n in WEIGHTS],
            *[new_m[n] for n in WEIGHTS], *[new_v[n] for n in WEIGHTS])
```

```python
import functools
import math

import jax
import jax.numpy as jnp
from jax import lax
from jax.experimental import pallas as pl
from jax.experimental.pallas import tpu as pltpu

F32 = jnp.float32
BF16 = jnp.bfloat16
HI = lax.Precision.HIGHEST

EPS = 1e-6
CHUNK = 64
HEAD = 128
NHEAD = 8
XA_HEADS = 4
S5_GROUPS = 64
S5_STATE = 64
S5_GROUP = 16
NSEG = 8
ADAM_LR, ADAM_B1, ADAM_B2, ADAM_EPS, ADAM_WD, ADAM_STEP = 0.001, 0.9, 0.999, 1e-08, 0.01, 10
VMEM_LIMIT = 56 * 2 ** 20


def _cparams(sem=None):
    return pltpu.CompilerParams(dimension_semantics=sem, vmem_limit_bytes=VMEM_LIMIT)


def _sigmoid(x):
    return 1.0 / (1.0 + jnp.exp(-x))


def _silu(x):
    return x * _sigmoid(x)


def _dsilu(x):
    s = _sigmoid(x)
    return s * (1.0 + x * (1.0 - s))


def _softplus(x):
    return jnp.maximum(x, 0.0) + jnp.log(1.0 + jnp.exp(-jnp.abs(x)))


_GELU_C = math.sqrt(2.0 / math.pi)


def _gelu(x):
    return 0.5 * x * (1.0 + jnp.tanh(_GELU_C * (x + 0.044715 * x * x * x)))


def _dgelu(x):
    t = jnp.tanh(_GELU_C * (x + 0.044715 * x * x * x))
    return 0.5 * (1.0 + t) + 0.5 * x * (1.0 - t * t) * _GELU_C * (1.0 + 3.0 * 0.044715 * x * x)


_DIMS = {"nn": (((1,), (0,)), ((), ())), "nt": (((1,), (1,)), ((), ())), "tn": (((0,), (0,)), ((), ()))}


class Side:
    def __init__(self, operands, out_shapes, scratch, start, finish):
        self.operands, self.out_shapes, self.scratch, self.start, self.finish = operands, out_shapes, scratch, start, finish


def matmul(a, b, *, mode, out_dtype, tm, tn, tk, name, side=None):
    if mode == "nn":
        (m, k), n = a.shape, b.shape[1]
    elif mode == "nt":
        (m, k), n = a.shape, b.shape[0]
    else:
        (k, m), n = a.shape, b.shape[1]
    tm, tn, tk = min(tm, m), min(tn, n), min(tk, k)
    assert m % tm == 0 and n % tn == 0 and k % tk == 0, (name, m, n, k, tm, tn, tk)
    grid = (m // tm, n // tn, k // tk)
    nk = grid[2]
    dims = _DIMS[mode]
    n_in = 0 if side is None else len(side.operands)
    n_out = 0 if side is None else len(side.out_shapes)
    n_acc = 0 if nk == 1 else 1

    def body(*refs):
        a_ref, b_ref, o_ref = refs[0], refs[1], refs[2 + n_in]
        scratch = refs[3 + n_in + n_out:]
        side_refs = (refs[2:2 + n_in], refs[3 + n_in:3 + n_in + n_out], scratch[n_acc:])
        ids = [pl.program_id(d) for d in range(3)]
        if side is not None:
            @pl.when((ids[0] == 0) & (ids[1] == 0) & (ids[2] == 0))
            def _():
                side.start(*side_refs)

        prod = lax.dot_general(a_ref[...].astype(BF16), b_ref[...].astype(BF16), dims, preferred_element_type=F32)
        if nk == 1:
            o_ref[...] = prod.astype(out_dtype)
        else:
            acc_ref = scratch[0]

            @pl.when(ids[2] == 0)
            def _():
                acc_ref[...] = prod

            @pl.when(ids[2] > 0)
            def _():
                acc_ref[...] += prod

            @pl.when(ids[2] == nk - 1)
            def _():
                o_ref[...] = acc_ref[...].astype(out_dtype)

        if side is not None:
            @pl.when((ids[0] == grid[0] - 1) & (ids[1] == grid[1] - 1) & (ids[2] == nk - 1))
            def _():
                side.finish(*side_refs)

    a_spec = pl.BlockSpec((tk, tm), lambda i, j, q: (q, i)) if mode == "tn" else pl.BlockSpec((tm, tk), lambda i, j, q: (i, q))
    b_spec = pl.BlockSpec((tn, tk), lambda i, j, q: (j, q)) if mode == "nt" else pl.BlockSpec((tk, tn), lambda i, j, q: (q, j))
    o_spec = pl.BlockSpec((tm, tn), lambda i, j, q: (i, j))
    o_shape = jax.ShapeDtypeStruct((m, n), out_dtype)
    acc = [] if nk == 1 else [pltpu.VMEM((tm, tn), F32)]
    if side is None:
        return pl.pallas_call(
            body, name=name, grid=grid, in_specs=[a_spec, b_spec], out_specs=o_spec, out_shape=o_shape, scratch_shapes=acc,
            compiler_params=_cparams(("parallel", "parallel", "arbitrary")),
        )(a, b)
    hbm = pl.BlockSpec(memory_space=pltpu.HBM)
    return pl.pallas_call(
        body, name=name, grid=grid, in_specs=[a_spec, b_spec] + [hbm] * n_in, out_specs=[o_spec] + [hbm] * n_out,
        out_shape=[o_shape] + list(side.out_shapes), scratch_shapes=acc + list(side.scratch),
        compiler_params=_cparams(("arbitrary", "arbitrary", "arbitrary")),
    )(a, b, *side.operands)


def rowwise(fn, ins, outs, *, rows, tr, name, consts=(), reds=(), into=None):
    tr = min(tr, rows)
    assert rows % tr == 0, (name, rows, tr)
    n_in, n_c, n_o = len(ins), len(consts), len(outs)
    n_buf = 0 if into is None else 1

    def body(*refs):
        vals = [r[...].astype(F32) for r in refs[:n_in + n_c]]
        res = fn(*vals)
        o_refs = refs[n_in + n_c + n_buf:]
        for r, v in zip(o_refs[:n_o], res[:n_o]):
            r[...] = v.astype(r.dtype)
        if reds:
            i = pl.program_id(0)

            @pl.when(i == 0)
            def _():
                for r, v in zip(o_refs[n_o:], res[n_o:]):
                    r[...] = v.astype(r.dtype)

            @pl.when(i > 0)
            def _():
                for r, v in zip(o_refs[n_o:], res[n_o:]):
                    r[...] += v.astype(r.dtype)

    in_specs = [pl.BlockSpec((tr, w), functools.partial(lambda i, cb: (i, cb), cb=cb)) for (_, w, cb) in ins]
    in_specs += [pl.BlockSpec(c.shape, lambda i: (0, 0)) for c in consts]
    out_specs = [pl.BlockSpec((tr, w), lambda i: (i, 0)) for (w, _) in outs]
    out_specs += [pl.BlockSpec(s, lambda i: (0, 0)) for (s, _) in reds]
    out_shape = [jax.ShapeDtypeStruct((rows, w), d) for (w, d) in outs]
    out_shape += [jax.ShapeDtypeStruct(s, d) for (s, d) in reds]
    operands = [a for (a, _, _) in ins] + list(consts)
    aliases = {}
    if into is not None:
        buf, pos, cb = into
        assert buf.dtype == outs[pos][1] and buf.shape[0] == rows, (name, buf.shape, buf.dtype)
        in_specs.append(pl.BlockSpec(memory_space=pl.ANY))
        operands.append(buf)
        out_specs[pos] = pl.BlockSpec((tr, outs[pos][0]), functools.partial(lambda i, cb: (i, cb), cb=cb))
        out_shape[pos] = jax.ShapeDtypeStruct(buf.shape, buf.dtype)
        aliases = {len(operands) - 1: pos}
    return pl.pallas_call(
        body, name=name, grid=(rows // tr,), in_specs=in_specs, out_specs=out_specs, out_shape=out_shape,
        input_output_aliases=aliases, compiler_params=_cparams(("arbitrary",) if reds else ("parallel",)),
    )(*operands)


def _colsum(x):
    return jnp.sum(x, axis=0, keepdims=True)


def rms_fwd(x, g, name):
    s, d = x.shape

    def fn(xv, gv):
        r = lax.rsqrt(jnp.mean(xv * xv, axis=-1, keepdims=True) + EPS)
        return xv * r * gv, r

    return rowwise(fn, [(x, d, 0)], [(d, BF16), (1, F32)], rows=s, tr=256, name=name, consts=[g])


def rms_bwd_x(du, x, r, g, dh):
    s, d = x.shape

    def fn(duv, xv, rv, dhv, gv):
        dyg = duv * gv
        dx = rv * dyg - xv * (rv * rv * rv) * jnp.mean(dyg * xv, axis=-1, keepdims=True)
        return dhv + dx, _colsum(duv * xv * rv)

    return rowwise(fn, [(du, d, 0), (x, d, 0), (r, 1, 0), (dh, d, 0)], [(d, F32)], rows=s, tr=256,
                   name="rms_bwd_x", consts=[g], reds=[((1, d), F32)])


def rms_bwd_g(du, x, r, name):
    s, d = x.shape

    def fn(duv, xv, rv):
        return (_colsum(duv * xv * rv),)

    return rowwise(fn, [(du, d, 0), (x, d, 0), (r, 1, 0)], [], rows=s, tr=256, name=name, reds=[((1, d), F32)])[0]


def final_stage(x, hres, target, g):
    s, d = x.shape

    def fn(xv, hv, tv, gv):
        h = xv + hv
        r = lax.rsqrt(jnp.mean(h * h, axis=-1, keepdims=True) + EPS)
        y = h * r * gv
        e = y - tv
        loss = 0.5 * jnp.sum(jnp.sum(e * e, axis=-1, keepdims=True), axis=0, keepdims=True) / d
        dy = e / d
        dyg = dy * gv
        dh = r * dyg - h * (r * r * r) * jnp.mean(dyg * h, axis=-1, keepdims=True)
        return dh, dh, loss, _colsum(dy * h * r)

    return rowwise(fn, [(x, d, 0), (hres, d, 0), (target, d, 0)], [(d, F32), (d, BF16)], rows=s, tr=256,
                   name="final_stage", consts=[g], reds=[((1, 1), F32), ((1, d), F32)])


def merge_fwd(pa, pb, pc, proj, gate_cb):
    s, d = pa.shape

    def fn(a, b, c, g0, g1, g2):
        return (_sigmoid(g0) * a + _sigmoid(g1) * b + _sigmoid(g2) * c,)

    ins = [(pa, d, 0), (pb, d, 0), (pc, d, 0)] + [(proj, d, gate_cb + i) for i in range(3)]
    return rowwise(fn, ins, [(d, BF16)], rows=s, tr=256, name="merge_fwd")[0]


def merge_bwd(dm, pa, pb, pc, proj, gate_cb, dproj):
    s, d = pa.shape

    def fn(dmv, a, b, c, g0, g1, g2):
        s0, s1, s2 = _sigmoid(g0), _sigmoid(g1), _sigmoid(g2)
        dgates = [dmv * a * s0 * (1.0 - s0), dmv * b * s1 * (1.0 - s1), dmv * c * s2 * (1.0 - s2)]
        return dmv * s0, dmv * s1, dmv * s2, jnp.concatenate(dgates, axis=1)

    ins = [(dm, d, 0), (pa, d, 0), (pb, d, 0), (pc, d, 0)] + [(proj, d, gate_cb + i) for i in range(3)]
    return rowwise(fn, ins, [(d, BF16)] * 3 + [(3 * d, BF16)], rows=s, tr=128, name="merge_bwd", into=(dproj, 3, gate_cb // 3))


def gate_fwd(o, proj, z_cb, name):
    s, w = o.shape

    def fn(ov, zv):
        return (ov * _silu(zv),)

    return rowwise(fn, [(o, w, 0), (proj, w, z_cb)], [(w, BF16)], rows=s, tr=512, name=name)[0]


def gate_bwd(dgo, o, proj, z_cb, name, dproj):
    s, w = o.shape

    def fn(dv, ov, zv):
        return dv * _silu(zv), dv * ov * _dsilu(zv)

    return rowwise(fn, [(dgo, w, 0), (o, w, 0), (proj, w, z_cb)], [(w, F32), (w, BF16)], rows=s, tr=512, name=name,
                   into=(dproj, 1, z_cb))


def gdn_out_fwd(o_raw, proj, z_cb, gn):
    s, w = o_raw.shape

    def fn(ov, zv, gv):
        outs = []
        for h in range(NHEAD):
            oh = ov[:, h * HEAD:(h + 1) * HEAD]
            r = lax.rsqrt(jnp.mean(oh * oh, axis=-1, keepdims=True) + EPS)
            outs.append(oh * r * gv)
        return (jnp.concatenate(outs, axis=1) * _silu(zv),)

    return rowwise(fn, [(o_raw, w, 0), (proj, w, z_cb)], [(w, BF16)], rows=s, tr=512, name="gdn_out_fwd", consts=[gn])[0]


def gdn_out_bwd(dga, o_raw, proj, z_cb, gn, dproj):
    s, w = o_raw.shape

    def fn(dv, ov, zv, gv):
        sz, dsz = _silu(zv), _dsilu(zv)
        do_l, dz_l = [], []
        dg = jnp.zeros((1, HEAD), F32)
        for h in range(NHEAD):
            sl = slice(h * HEAD, (h + 1) * HEAD)
            oh, dgh = ov[:, sl], dv[:, sl]
            r = lax.rsqrt(jnp.mean(oh * oh, axis=-1, keepdims=True) + EPS)
            on = oh * r * gv
            don = dgh * sz[:, sl]
            dz_l.append(dgh * on * dsz[:, sl])
            dg = dg + _colsum(don * oh * r)
            dyg = don * gv
            do_l.append(r * dyg - oh * (r * r * r) * jnp.mean(dyg * oh, axis=-1, keepdims=True))
        return jnp.concatenate(do_l, axis=1), jnp.concatenate(dz_l, axis=1), dg

    return rowwise(fn, [(dga, w, 0), (o_raw, w, 0), (proj, w, z_cb)], [(w, F32), (w, BF16)], rows=s, tr=512,
                   name="gdn_out_bwd", consts=[gn], reds=[((1, HEAD), F32)], into=(dproj, 1, z_cb))


def s5_act_fwd(y_ssm, proj, xb_cb, dvec):
    s, w = y_ssm.shape

    def fn(yv, xv, dv):
        return (_gelu(yv + dv * xv),)

    return rowwise(fn, [(y_ssm, w, 0), (proj, w, xb_cb)], [(w, BF16)], rows=s, tr=512, name="s5_act_fwd", consts=[dvec])[0]


def s5_act_bwd(dyb, y_ssm, proj, xb_cb, dvec):
    s, w = y_ssm.shape

    def fn(dv_, yv, xv, dv):
        dpre = dv_ * _dgelu(yv + dv * xv)
        return dpre, dpre * dv, _colsum(dpre * xv)

    return rowwise(fn, [(dyb, w, 0), (y_ssm, w, 0), (proj, w, xb_cb)], [(w, F32), (w, F32)], rows=s, tr=512,
                   name="s5_act_bwd", consts=[dvec], reds=[((1, w), F32)])


def s5_glu_fwd(glu, proj, z_cb):
    s, w2 = glu.shape
    w = w2 // 2

    def fn(val, gate, zv):
        return (val * _sigmoid(gate) * _silu(zv),)

    return rowwise(fn, [(glu, w, 0), (glu, w, 1), (proj, w, z_cb)], [(w, BF16)], rows=s, tr=512, name="s5_glu_fwd")[0]


def s5_glu_bwd(dgb, glu, proj, z_cb, dproj):
    s, w2 = glu.shape
    w = w2 // 2

    def fn(dv, val, gate, zv):
        sg = _sigmoid(gate)
        ob = val * sg
        dob = dv * _silu(zv)
        return dob * sg, dob * val * sg * (1.0 - sg), dv * ob * _dsilu(zv)

    return rowwise(fn, [(dgb, w, 0), (glu, w, 0), (glu, w, 1), (proj, w, z_cb)], [(w, BF16)] * 3, rows=s, tr=512,
                   name="s5_glu_bwd", into=(dproj, 2, z_cb))


def add_into(a, b, name, dproj, cb):
    s, w = a.shape

    def fn(av, bv):
        return (av + bv,)

    return rowwise(fn, [(a, w, 0), (b, w, 0)], [(w, BF16)], rows=s, tr=512, name=name, into=(dproj, 0, cb))[0]


GATE_W, GATE_CB = 6144, 0
QKV_W, QKV_CB = 3072, 2
ZA_CB, XB_CB, ZB_CB, QC_CB, ZC_CB = 9, 10, 11, 12, 13
BA_CB, BA_W = 112, 128
BA_PAD, BA_PAD_CB = 1024, 14
PROJ_W = 14336 + BA_PAD


def _dot(a, b, dims="nn", prec=None):
    if prec is None:
        a, b = a.astype(BF16), b.astype(BF16)
    return lax.dot_general(a, b, _DIMS[dims], preferred_element_type=F32, precision=prec)


def _split(a):
    hi = a.astype(BF16)
    return hi, (a - hi.astype(F32)).astype(BF16)


def _dot3(a, b, dims="nn"):
    (ah, al), (bh, bl) = _split(a), _split(b)
    d = functools.partial(lax.dot_general, dimension_numbers=_DIMS[dims], preferred_element_type=F32)
    return d(ah, bh) + (d(ah, bl) + d(al, bh))


def _iota2(shape, dim):
    return lax.broadcasted_iota(jnp.int32, shape, dim)


def _conv_taps(xs, tr, k):
    if k == 0:
        return xs[8:8 + tr]
    return pltpu.roll(xs, k, 0)[8:8 + tr]


def _conv_silu_parts(xv, halo, wv, first):
    tr = xv.shape[0]
    xs = jnp.concatenate([jnp.where(first, 0.0, halo), xv], axis=0)
    taps = [_conv_taps(xs, tr, 3 - j) for j in range(4)]
    c = taps[0] * wv[0:1] + taps[1] * wv[1:2] + taps[2] * wv[2:3] + taps[3] * wv[3:4]
    return taps, c


def gdn_prep_fwd(proj, conv_w, alog_pad, dt_pad):
    s = proj.shape[0]
    tr = min(256, s)
    w = NHEAD * HEAD

    def body(x_ref, halo_ref, ba_ref, w_ref, al_ref, dt_ref, q_ref, k_ref, v_ref, bg_ref, gcol_ref, gt_ref):
        first = pl.program_id(0) == 0
        _, c = _conv_silu_parts(x_ref[...], halo_ref[...], w_ref[...], first)
        sv = _silu(c)
        for h in range(NHEAD):
            sl = slice(h * HEAD, (h + 1) * HEAD)
            qh, kh = sv[:, h * HEAD:(h + 1) * HEAD], sv[:, w + h * HEAD:w + (h + 1) * HEAD]
            q_ref[:, sl] = qh * lax.rsqrt(jnp.sum(qh * qh, axis=-1, keepdims=True) + EPS) * (HEAD ** -0.5)
            k_ref[:, sl] = kh * lax.rsqrt(jnp.sum(kh * kh, axis=-1, keepdims=True) + EPS)
        v_ref[...] = sv[:, 2 * w:]
        ba = ba_ref[...]
        lane = _iota2(ba.shape, 1)
        beta = _sigmoid(ba)
        g = -jnp.exp(al_ref[...]) * _softplus(ba + dt_ref[...])
        bg = jnp.where(lane < NHEAD, beta, jnp.where(lane < 2 * NHEAD, g, 0.0))
        bg_ref[...] = bg
        er, ec = _iota2((BA_W, BA_W), 0), _iota2((BA_W, BA_W), 1)
        expand = jnp.where((er == NHEAD + ec // 8) & (ec < 8 * NHEAD), 1.0, 0.0)
        grep = _dot(bg, expand, prec=HI)
        lr, lc = _iota2((tr, tr), 0), _iota2((tr, tr), 1)
        tril = jnp.where((lr // CHUNK == lc // CHUNK) & (lr >= lc), 1.0, 0.0)
        gc = _dot(tril, grep, prec=HI)
        gcol_ref[...] = gc
        gt_ref[...] = gc.T

    nb8 = tr // 8
    return pl.pallas_call(
        body, name="gdn_prep_fwd", grid=(s // tr,),
        in_specs=[pl.BlockSpec((tr, QKV_W), lambda i: (i, QKV_CB)),
                  pl.BlockSpec((8, QKV_W), lambda i: (jnp.maximum(i * nb8 - 1, 0), QKV_CB)),
                  pl.BlockSpec((tr, BA_W), lambda i: (i, BA_CB)),
                  pl.BlockSpec(conv_w.shape, lambda i: (0, 0)),
                  pl.BlockSpec((1, BA_W), lambda i: (0, 0)), pl.BlockSpec((1, BA_W), lambda i: (0, 0))],
        out_specs=[pl.BlockSpec((tr, w), lambda i: (i, 0))] * 3 + [pl.BlockSpec((tr, BA_W), lambda i: (i, 0))] * 2
        + [pl.BlockSpec((BA_W, tr), lambda i: (0, i))],
        out_shape=[jax.ShapeDtypeStruct((s, w), F32)] * 3 + [jax.ShapeDtypeStruct((s, BA_W), F32)] * 2
        + [jax.ShapeDtypeStruct((BA_W, s), F32)],
        compiler_params=_cparams(("parallel",)),
    )(proj, proj, proj, conv_w, alog_pad, dt_pad)


def _chunk_common(qh, kh, bgv, gcolv, gtv, h):
    beta = bgv[:, h:h + 1]
    gcc = gcolv[:, 8 * h:8 * h + 1]
    gcr = jnp.concatenate([gtv[8 * h:8 * h + 8, :]] * (CHUNK // 8), axis=0)
    ii, jj = _iota2((CHUNK, CHUNK), 0), _iota2((CHUNK, CHUNK), 1)
    incl, strict = ii >= jj, ii > jj
    decay = jnp.where(incl, jnp.exp(jnp.where(incl, gcc - gcr, 0.0)), 0.0)
    gl = gcr[:, CHUNK - 1:CHUNK]
    return beta, gcc, decay, strict, gl


def gdn_intra_fwd(q, k, v, bg, gcol, gt3):
    s, w = q.shape
    n = s // CHUNK

    def body(q_ref, k_ref, v_ref, bg_ref, gcol_ref, gt_ref, u_ref, w_ref, qd_ref, kd_ref, qk_ref, t_ref):
        bgv, gcolv, gtv = bg_ref[...], gcol_ref[...], gt_ref[0]
        ii, jj = _iota2((CHUNK, CHUNK), 0), _iota2((CHUNK, CHUNK), 1)
        eye = jnp.where(ii == jj, 1.0, 0.0)
        ps, ts, rhs = [], [], []
        for h in range(NHEAD):
            sl = slice(h * HEAD, (h + 1) * HEAD)
            qh, kh, vh = q_ref[:, sl], k_ref[:, sl], v_ref[:, sl]
            beta, gcc, decay, strict, gl = _chunk_common(qh, kh, bgv, gcolv, gtv, h)
            kb = kh * beta
            eg = jnp.exp(gcc)
            p = -jnp.where(strict, _dot(kb, kh, "nt") * decay, 0.0)
            ps.append(p)
            ts.append(eye + p)
            rhs.append((vh * beta, kb * eg))
            qd_ref[:, sl] = qh * eg
            kd_ref[:, sl] = kh * jnp.exp(gl - gcc)
            qk_ref[0, h] = _dot(qh, kh, "nt") * decay
        for _ in range(5):
            ps = [_dot3(p, p) for p in ps]
            ts = [t + _dot3(t, p) for t, p in zip(ts, ps)]
        for h in range(NHEAD):
            sl = slice(h * HEAD, (h + 1) * HEAD)
            u_ref[:, sl] = _dot3(ts[h], rhs[h][0])
            w_ref[:, sl] = _dot3(ts[h], rhs[h][1])
            t_ref[0, h] = ts[h]

    tok = pl.BlockSpec((CHUNK, w), lambda i: (i, 0))
    sm = pl.BlockSpec((CHUNK, BA_W), lambda i: (i, 0))
    sq = pl.BlockSpec((1, NHEAD, CHUNK, CHUNK), lambda i: (i, 0, 0, 0))
    return pl.pallas_call(
        body, name="gdn_intra_fwd", grid=(n,),
        in_specs=[tok, tok, tok, sm, sm, pl.BlockSpec((1, BA_W, CHUNK), lambda i: (i, 0, 0))],
        out_specs=[tok] * 4 + [sq, sq],
        out_shape=[jax.ShapeDtypeStruct((s, w), F32)] * 4 + [jax.ShapeDtypeStruct((n, NHEAD, CHUNK, CHUNK), F32)] * 2,
        compiler_params=_cparams(("parallel",)),
    )(q, k, v, bg, gcol, gt3)


def _state_decay(gtv, h):
    g8 = gtv[8 * h:8 * h + 8, CHUNK - 1:CHUNK]
    return jnp.exp(jnp.concatenate([g8] * (HEAD // 8), axis=0))


def gdn_seq_fwd(u, wd, qd, kd, qk, gt3):
    s, w = u.shape
    n = s // CHUNK

    def body(u_ref, w_ref, qd_ref, kd_ref, qk_ref, gt_ref, o_ref, st_ref, s_ref):
        @pl.when(pl.program_id(0) == 0)
        def _():
            s_ref[...] = jnp.zeros_like(s_ref)

        gtv = gt_ref[0]
        cols = [slice(h * HEAD, (h + 1) * HEAD) for h in range(NHEAD)]
        states = [s_ref[h] for h in range(NHEAD)]
        for h in range(NHEAD):
            st_ref[0, h] = states[h]
        vns = [u_ref[:, cols[h]] - _dot(w_ref[:, cols[h]], states[h]) for h in range(NHEAD)]
        from_state = [_dot(qd_ref[:, cols[h]], states[h]) for h in range(NHEAD)]
        for h in range(NHEAD):
            o_ref[:, cols[h]] = from_state[h] + _dot(qk_ref[0, h], vns[h])
        for h in range(NHEAD):
            s_ref[h] = states[h] * _state_decay(gtv, h) + _dot(kd_ref[:, cols[h]], vns[h], "tn")

    tok = pl.BlockSpec((CHUNK, w), lambda i: (i, 0))
    return pl.pallas_call(
        body, name="gdn_seq_fwd", grid=(n,),
        in_specs=[tok] * 4 + [pl.BlockSpec((1, NHEAD, CHUNK, CHUNK), lambda i: (i, 0, 0, 0)),
                              pl.BlockSpec((1, BA_W, CHUNK), lambda i: (i, 0, 0))],
        out_specs=[tok, pl.BlockSpec((1, NHEAD, HEAD, HEAD), lambda i: (i, 0, 0, 0))],
        out_shape=[jax.ShapeDtypeStruct((s, w), F32), jax.ShapeDtypeStruct((n, NHEAD, HEAD, HEAD), F32)],
        scratch_shapes=[pltpu.VMEM((NHEAD, HEAD, HEAD), F32)],
        compiler_params=_cparams(("arbitrary",)),
    )(u, wd, qd, kd, qk, gt3)


def gdn_seq_bwd(do, u, wd, qd, kd, qk, gt3, states):
    s, w = u.shape
    n = s // CHUNK

    def body(do_ref, u_ref, w_ref, qd_ref, kd_ref, qk_ref, gt_ref, st_ref,
             du_ref, dw_ref, dqd_ref, dkd_ref, dqk_ref, dgl_ref, ds_ref):
        @pl.when(pl.program_id(0) == 0)
        def _():
            ds_ref[...] = jnp.zeros_like(ds_ref)

        gtv = gt_ref[0]
        heads = range(NHEAD)
        cols = [slice(h * HEAD, (h + 1) * HEAD) for h in heads]
        sts = [st_ref[0, h] for h in heads]
        dsps = [ds_ref[h] for h in heads]
        vns = [u_ref[:, cols[h]] - _dot(w_ref[:, cols[h]], sts[h]) for h in heads]
        dvns = [_dot(qk_ref[0, h], do_ref[:, cols[h]], "tn") + _dot(kd_ref[:, cols[h]], dsps[h]) for h in heads]
        for h in heads:
            du_ref[:, cols[h]] = dvns[h]
            dqd_ref[:, cols[h]] = _dot(do_ref[:, cols[h]], sts[h], "nt")
        for h in heads:
            dw_ref[:, cols[h]] = -_dot(dvns[h], sts[h], "nt")
            dkd_ref[:, cols[h]] = _dot(vns[h], dsps[h], "nt")
            dqk_ref[0, h] = _dot(do_ref[:, cols[h]], vns[h], "nt")
        for h in heads:
            ds_ref[h] = (dsps[h] * _state_decay(gtv, h) + _dot(qd_ref[:, cols[h]], do_ref[:, cols[h]], "tn")
                         - _dot(w_ref[:, cols[h]], dvns[h], "tn"))
        dgl_ref[0] = jnp.concatenate([_colsum(sts[h] * dsps[h]) for h in heads], axis=0)

    tok = pl.BlockSpec((CHUNK, w), lambda i: (n - 1 - i, 0))
    sq = pl.BlockSpec((1, NHEAD, CHUNK, CHUNK), lambda i: (n - 1 - i, 0, 0, 0))
    return pl.pallas_call(
        body, name="gdn_seq_bwd", grid=(n,),
        in_specs=[tok] * 5 + [sq, pl.BlockSpec((1, BA_W, CHUNK), lambda i: (n - 1 - i, 0, 0)),
                              pl.BlockSpec((1, NHEAD, HEAD, HEAD), lambda i: (n - 1 - i, 0, 0, 0))],
        out_specs=[tok] * 4 + [sq, pl.BlockSpec((1, NHEAD, HEAD), lambda i: (n - 1 - i, 0, 0))],
        out_shape=[jax.ShapeDtypeStruct((s, w), F32)] * 4 + [jax.ShapeDtypeStruct((n, NHEAD, CHUNK, CHUNK), F32),
                                                            jax.ShapeDtypeStruct((n, NHEAD, HEAD), F32)],
        scratch_shapes=[pltpu.VMEM((NHEAD, HEAD, HEAD), F32)],
        compiler_params=_cparams(("arbitrary",)),
    )(do, u, wd, qd, kd, qk, gt3, states)


def gdn_intra_bwd(q, k, v, bg, gcol, gt3, tinv, du, dw, dqd, dkd, dqk, dgl):
    s, w = q.shape
    n = s // CHUNK

    def body(q_ref, k_ref, v_ref, bg_ref, gcol_ref, gt_ref, t_ref, du_ref, dw_ref, dqd_ref, dkd_ref, dqk_ref, dgl_ref,
             dq_ref, dk_ref, dv_ref, dbg_ref):
        bgv, gcolv, gtv, dglv = bg_ref[...], gcol_ref[...], gt_ref[0], dgl_ref[0]
        ii, jj = _iota2((CHUNK, CHUNK), 0), _iota2((CHUNK, CHUNK), 1)
        triu = jnp.where(ii <= jj, 1.0, 0.0)
        ones = jnp.ones((CHUNK, BA_W), F32)
        lane = _iota2((CHUNK, BA_W), 1)
        row = _iota2((CHUNK, 1), 0)
        dbg = jnp.zeros((CHUNK, BA_W), F32)
        first = []
        for h in range(NHEAD):
            sl = slice(h * HEAD, (h + 1) * HEAD)
            qh, kh, vh = q_ref[:, sl], k_ref[:, sl], v_ref[:, sl]
            beta, gcc, decay, strict, gl = _chunk_common(qh, kh, bgv, gcolv, gtv, h)
            kb = kh * beta
            eg = jnp.exp(gcc)
            rv, rk = vh * beta, kb * eg
            t, duh, dwh = t_ref[0, h], du_ref[:, sl], dw_ref[:, sl]
            first.append((_dot3(duh, rv, "nt") + _dot3(dwh, rk, "nt"), _dot3(t, duh, "tn"), _dot3(t, dwh, "tn"),
                          _dot(kb, kh, "nt"), _dot(qh, kh, "nt")))
        second = [_dot3(t_ref[0, h], first[h][0], "tn") for h in range(NHEAD)]
        third = [_dot3(second[h], t_ref[0, h], "nt") for h in range(NHEAD)]
        for h in range(NHEAD):
            sl = slice(h * HEAD, (h + 1) * HEAD)
            qh, kh, vh = q_ref[:, sl], k_ref[:, sl], v_ref[:, sl]
            beta, gcc, decay, strict, gl = _chunk_common(qh, kh, bgv, gcolv, gtv, h)
            dqdh, dkdh, dqkh = dqd_ref[:, sl], dkd_ref[:, sl], dqk_ref[0, h]
            kb = kh * beta
            eg = jnp.exp(gcc)
            ekd = jnp.exp(gl - gcc)
            rk = kb * eg
            _, drv, drk, m, p = first[h]
            da = jnp.where(strict, -third[h], 0.0)
            dm = da * decay
            dpm = dqkh * decay
            dkb = _dot(dm, kh) + drk * eg
            dq = _dot(dpm, kh) + dqdh * eg
            dk = _dot(dm, kb, "tn") + _dot(dpm, qh, "tn") + dkdh * ekd + dkb * beta
            e = (da * m + dqkh * p) * decay
            sk = jnp.sum(dkdh * kh * ekd, axis=-1, keepdims=True)
            dgc = (jnp.sum(e, axis=-1, keepdims=True) - _dot3(e, ones, "tn")[:, 0:1]
                   + jnp.sum(dqdh * qh * eg, axis=-1, keepdims=True) - sk + jnp.sum(drk * rk, axis=-1, keepdims=True))
            dglast = jnp.sum(sk, axis=0, keepdims=True) + jnp.sum(dglv[h:h + 1, :], axis=-1, keepdims=True) * jnp.exp(gl)
            dgc = dgc + jnp.where(row == CHUNK - 1, dglast, 0.0)
            dg = _dot3(triu, dgc * ones)
            dbeta = jnp.sum(dkb * kh, axis=-1, keepdims=True) + jnp.sum(drv * vh, axis=-1, keepdims=True)
            dbg = dbg + jnp.where(lane == h, dbeta, 0.0) + jnp.where(lane == NHEAD + h, dg, 0.0)
            dq_ref[:, sl] = dq
            dk_ref[:, sl] = dk
            dv_ref[:, sl] = drv * beta
        dbg_ref[...] = dbg

    tok = pl.BlockSpec((CHUNK, w), lambda i: (i, 0))
    sm = pl.BlockSpec((CHUNK, BA_W), lambda i: (i, 0))
    sq = pl.BlockSpec((1, NHEAD, CHUNK, CHUNK), lambda i: (i, 0, 0, 0))
    return pl.pallas_call(
        body, name="gdn_intra_bwd", grid=(n,),
        in_specs=[tok, tok, tok, sm, sm, pl.BlockSpec((1, BA_W, CHUNK), lambda i: (i, 0, 0)), sq,
                  tok, tok, tok, tok, sq, pl.BlockSpec((1, NHEAD, HEAD), lambda i: (i, 0, 0))],
        out_specs=[tok] * 3 + [sm],
        out_shape=[jax.ShapeDtypeStruct((s, w), F32)] * 3 + [jax.ShapeDtypeStruct((s, BA_W), F32)],
        compiler_params=_cparams(("parallel",)),
    )(q, k, v, bg, gcol, gt3, tinv, du, dw, dqd, dkd, dqk, dgl)


def gdn_prep_bwd1(proj, conv_w, alog_pad, dt_pad, dq, dk, dv, dbg, dproj):
    s = proj.shape[0]
    tr = min(256, s)
    w = NHEAD * HEAD
    pad_w = BA_PAD

    def body(x_ref, halo_ref, ba_ref, w_ref, al_ref, dt_ref, dq_ref, dk_ref, dv_ref, dbg_ref, buf_ref,
             dc_ref, dba_ref, dw0_ref, dw1_ref, dw2_ref, dw3_ref, dal_ref, ddt_ref):
        i = pl.program_id(0)
        taps, c = _conv_silu_parts(x_ref[...], halo_ref[...], w_ref[...], i == 0)
        sv, dsv = _silu(c), _dsilu(c)
        for h in range(NHEAD):
            for base, d_ref, scale in ((0, dq_ref, HEAD ** -0.5), (w, dk_ref, 1.0)):
                sl = slice(base + h * HEAD, base + (h + 1) * HEAD)
                sh = sv[:, sl]
                dn = d_ref[:, h * HEAD:(h + 1) * HEAD]
                r = lax.rsqrt(jnp.sum(sh * sh, axis=-1, keepdims=True) + EPS)
                dsh = scale * (r * dn - sh * (r * r * r) * jnp.sum(dn * sh, axis=-1, keepdims=True))
                dc_ref[:, sl] = dsh * dsv[:, sl]
        dc_ref[:, 2 * w:] = dv_ref[...] * dsv[:, 2 * w:]
        dc = dc_ref[...]
        ba, dbgv = ba_ref[...], dbg_ref[...]
        lane = _iota2(ba.shape, 1)
        beta = _sigmoid(ba)
        ea = jnp.exp(al_ref[...])
        z = ba + dt_ref[...]
        g = -ea * _softplus(z)
        is_g = (lane >= NHEAD) & (lane < 2 * NHEAD)
        da_raw = jnp.where(is_g, dbgv * (-ea) * _sigmoid(z), 0.0)
        dba = jnp.where(lane < NHEAD, dbgv * beta * (1.0 - beta), da_raw)
        dba_ref[...] = jnp.concatenate([dba, jnp.zeros((tr, pad_w - BA_W), F32)], axis=1).astype(BF16)
        partial = [_colsum(dc * tp) for tp in taps] + [_colsum(jnp.where(is_g, dbgv * g, 0.0)), _colsum(da_raw)]
        red_refs = (dw0_ref, dw1_ref, dw2_ref, dw3_ref, dal_ref, ddt_ref)

        @pl.when(i == 0)
        def _():
            for r_, v_ in zip(red_refs, partial):
                r_[...] = v_

        @pl.when(i > 0)
        def _():
            for r_, v_ in zip(red_refs, partial):
                r_[...] += v_

    nb8 = tr // 8
    tok = pl.BlockSpec((tr, w), lambda i: (i, 0))
    one = lambda width: pl.BlockSpec((1, width), lambda i: (0, 0))
    return pl.pallas_call(
        body, name="gdn_prep_bwd1", grid=(s // tr,),
        in_specs=[pl.BlockSpec((tr, QKV_W), lambda i: (i, QKV_CB)),
                  pl.BlockSpec((8, QKV_W), lambda i: (jnp.maximum(i * nb8 - 1, 0), QKV_CB)),
                  pl.BlockSpec((tr, BA_W), lambda i: (i, BA_CB)),
                  pl.BlockSpec(conv_w.shape, lambda i: (0, 0)), one(BA_W), one(BA_W),
                  tok, tok, tok, pl.BlockSpec((tr, BA_W), lambda i: (i, 0)), pl.BlockSpec(memory_space=pl.ANY)],
        out_specs=[pl.BlockSpec((tr, QKV_W), lambda i: (i, 0)), pl.BlockSpec((tr, pad_w), lambda i: (i, BA_PAD_CB))]
        + [one(QKV_W)] * 4 + [one(BA_W)] * 2,
        out_shape=[jax.ShapeDtypeStruct((s, QKV_W), F32), jax.ShapeDtypeStruct(dproj.shape, dproj.dtype)]
        + [jax.ShapeDtypeStruct((1, QKV_W), F32)] * 4 + [jax.ShapeDtypeStruct((1, BA_W), F32)] * 2,
        input_output_aliases={10: 1}, compiler_params=_cparams(("arbitrary",)),
    )(proj, proj, proj, conv_w, alog_pad, dt_pad, dq, dk, dv, dbg, dproj)


def gdn_prep_bwd2(dc, conv_w, dproj):
    s = dc.shape[0]
    tr = min(256, s)
    nblk = s // tr
    nb8 = tr // 8

    def body(dc_ref, halo_ref, w_ref, buf_ref, o_ref):
        last = pl.program_id(0) == nblk - 1
        wv = w_ref[...]
        xs = jnp.concatenate([dc_ref[...], jnp.where(last, 0.0, halo_ref[...])], axis=0)
        acc = xs[:tr] * wv[3:4]
        for j in range(3):
            acc = acc + pltpu.roll(xs, tr + 8 - (3 - j), 0)[:tr] * wv[j:j + 1]
        o_ref[...] = acc.astype(BF16)

    return pl.pallas_call(
        body, name="gdn_prep_bwd2", grid=(nblk,),
        in_specs=[pl.BlockSpec((tr, QKV_W), lambda i: (i, 0)),
                  pl.BlockSpec((8, QKV_W), lambda i: (jnp.minimum((i + 1) * nb8, s // 8 - 1), 0)),
                  pl.BlockSpec(conv_w.shape, lambda i: (0, 0)), pl.BlockSpec(memory_space=pl.ANY)],
        out_specs=pl.BlockSpec((tr, QKV_W), lambda i: (i, QKV_CB)),
        out_shape=jax.ShapeDtypeStruct(dproj.shape, dproj.dtype), input_output_aliases={3: 0},
        compiler_params=_cparams(("parallel",)),
    )(dc, dc, conv_w, dproj)


S5_W = S5_GROUPS * S5_STATE
S5_IN = S5_GROUPS * S5_GROUP
S5_TILES = 8
S5_TW, S5_TI = S5_W // S5_TILES, S5_IN // S5_TILES


def _s5_param_math(lr, li, ldt, br, bi):
    pr, pc = _iota2((S5_STATE, S5_STATE * S5_GROUP), 0), _iota2((S5_STATE, S5_STATE * S5_GROUP), 1)
    rep = jnp.where(pc // S5_GROUP == pr, 1.0, 0.0)
    dt = jnp.exp(ldt)
    mag = jnp.exp(lr * dt)
    ab_re, ab_im = mag * jnp.cos(li * dt), mag * jnp.sin(li * dt)
    den = lr * lr + li * li
    nr, ni = ab_re - 1.0, ab_im
    coef_re = (nr * lr + ni * li) / den
    coef_im = (ni * lr - nr * li) / den
    cr, ci = _dot(coef_re, rep, prec=HI), _dot(coef_im, rep, prec=HI)
    return ab_re, ab_im, cr * br - ci * bi, cr * bi + ci * br


def s5_param_fwd(lr, li, ldt, br, bi):
    def body(lr_ref, li_ref, ldt_ref, br_ref, bi_ref, ar_ref, ai_ref, bbr_ref, bbi_ref):
        res = _s5_param_math(lr_ref[...], li_ref[...], ldt_ref[...], br_ref[...], bi_ref[...])
        for r, v in zip((ar_ref, ai_ref, bbr_ref, bbi_ref), res):
            r[...] = v

    return pl.pallas_call(
        body, name="s5_param_fwd",
        out_shape=[jax.ShapeDtypeStruct(lr.shape, F32)] * 2 + [jax.ShapeDtypeStruct(br.shape, F32)] * 2,
        compiler_params=_cparams(),
    )(lr, li, ldt, br, bi)


def s5_param_bwd(lr, li, ldt, br, bi, dar, dai, dbbr, dbbi):
    def body(lr_ref, li_ref, ldt_ref, br_ref, bi_ref, dar_ref, dai_ref, dbbr_ref, dbbi_ref, *out_refs):
        _, vjp = jax.vjp(_s5_param_math, lr_ref[...], li_ref[...], ldt_ref[...], br_ref[...], bi_ref[...])
        for r, v in zip(out_refs, vjp((dar_ref[...], dai_ref[...], dbbr_ref[...], dbbi_ref[...]))):
            r[...] = v

    return pl.pallas_call(
        body, name="s5_param_bwd",
        out_shape=[jax.ShapeDtypeStruct(a.shape, F32) for a in (lr, li, ldt, br, bi)],
        compiler_params=_cparams(),
    )(lr, li, ldt, br, bi, dar, dai, dbbr, dbbi)


def _cmul(ar, ai, br, bi):
    return ar * br - ai * bi, ar * bi + ai * br


def _s5_power(ar, ai, steps):
    assert steps & (steps - 1) == 0
    for _ in range(steps.bit_length() - 1):
        ar, ai = _cmul(ar, ai, ar, ai)
    return ar, ai


def _s5_scan_rows(ar_ref, ai_ref, re_ref, im_ref, sr_ref, si_ref, tb, row0, reverse):
    quarter = S5_W // 4
    for qd in range(4):
        cs = slice(qd * quarter, (qd + 1) * quarter)
        are = jnp.broadcast_to(ar_ref[:, cs], (NSEG, quarter))
        aim = jnp.broadcast_to(ai_ref[:, cs], (NSEG, quarter))
        if reverse:
            aim = -aim

        def step(t, carry):
            h_r, h_i = carry
            tt = tb - 1 - t if reverse else t
            rows = pl.ds(pl.multiple_of(row0 + tt * NSEG, NSEG), NSEG)
            n_r = are * h_r - aim * h_i + re_ref[rows, cs]
            n_i = are * h_i + aim * h_r + im_ref[rows, cs]
            re_ref[rows, cs] = n_r
            im_ref[rows, cs] = n_i
            return n_r, n_i

        h_r, h_i = lax.fori_loop(0, tb, step, (sr_ref[:, cs], si_ref[:, cs]), unroll=8)
        sr_ref[:, cs] = h_r
        si_ref[:, cs] = h_i


def _s5_segment_carry(ar_ref, ai_ref, sr_ref, si_ref, steps, reverse):
    pr, pi = _s5_power(ar_ref[...], ai_ref[...], steps)
    if reverse:
        pi = -pi
    cur_r = jnp.zeros((1, S5_W), F32)
    cur_i = jnp.zeros((1, S5_W), F32)
    for s in (range(NSEG - 1, -1, -1) if reverse else range(NSEG)):
        e_r, e_i = sr_ref[s:s + 1, :], si_ref[s:s + 1, :]
        sr_ref[s:s + 1, :] = cur_r
        si_ref[s:s + 1, :] = cur_i
        nr, ni = _cmul(pr, pi, cur_r, cur_i)
        cur_r, cur_i = nr + e_r, ni + e_i


def _s5_blocks(s):
    steps = s // NSEG
    tb = min(32, steps)
    return steps, tb, NSEG * tb, steps // tb


def s5_scan_fwd(xp, a_re, a_im, bre, bim, cre, cim):
    s = xp.shape[0]
    steps, tb, rb, nb = _s5_blocks(s)

    def body(x_ref, ar_ref, ai_ref, bre_ref, bim_ref, cre_ref, cim_ref, y_ref, hsr_ref, hsi_ref,
             hr_ref, hi_ref, sr_ref, si_ref):
        ph, b = pl.program_id(0), pl.program_id(1)

        @pl.when((ph == 0) & (b == 0))
        def _():
            sr_ref[...] = jnp.zeros_like(sr_ref)
            si_ref[...] = jnp.zeros_like(si_ref)

        @pl.when((ph == 1) & (b == 0))
        def _():
            _s5_segment_carry(ar_ref, ai_ref, sr_ref, si_ref, steps, False)

        xv = x_ref[...].astype(BF16)
        for j in range(S5_TILES):
            xs = xv[:, j * S5_TI:(j + 1) * S5_TI]
            hr_ref[:, j * S5_TW:(j + 1) * S5_TW] = _dot(xs, bre_ref[j])
            hi_ref[:, j * S5_TW:(j + 1) * S5_TW] = _dot(xs, bim_ref[j])

        @pl.when(ph == 1)
        def _():
            hsr_ref[0] = sr_ref[...]
            hsi_ref[0] = si_ref[...]

        _s5_scan_rows(ar_ref, ai_ref, hr_ref, hi_ref, sr_ref, si_ref, tb, 0, False)

        @pl.when(ph == 1)
        def _():
            for j in range(S5_TILES):
                cs = slice(j * S5_TW, (j + 1) * S5_TW)
                y_ref[:, j * S5_TI:(j + 1) * S5_TI] = _dot(hr_ref[:, cs], cre_ref[j]) - _dot(hi_ref[:, cs], cim_ref[j])

    row = pl.BlockSpec((1, S5_W), lambda p, b: (0, 0))
    wb = pl.BlockSpec((S5_TILES, S5_TI, S5_TW), lambda p, b: (0, 0, 0))
    wc = pl.BlockSpec((S5_TILES, S5_TW, S5_TI), lambda p, b: (0, 0, 0))
    st = pl.BlockSpec((1, NSEG, S5_W), lambda p, b: (p * b, 0, 0))
    return pl.pallas_call(
        body, name="s5_scan_fwd", grid=(2, nb),
        in_specs=[pl.BlockSpec((rb, S5_IN), lambda p, b: (b, 0)), row, row, wb, wb, wc, wc],
        out_specs=[pl.BlockSpec((rb, S5_IN), lambda p, b: (p * b, 0)), st, st],
        out_shape=[jax.ShapeDtypeStruct((s, S5_IN), F32)] + [jax.ShapeDtypeStruct((nb, NSEG, S5_W), F32)] * 2,
        scratch_shapes=[pltpu.VMEM((rb, S5_W), F32)] * 2 + [pltpu.VMEM((NSEG, S5_W), F32)] * 2,
        compiler_params=_cparams(("arbitrary", "arbitrary")),
    )(xp, a_re, a_im, bre, bim, cre, cim)


def s5_scan_bwd(dyp, xp, a_re, a_im, bre, bim, cre_t, cim_t, hs_r, hs_i):
    s = xp.shape[0]
    steps, tb, rb, nb = _s5_blocks(s)

    def body(dy_ref, x_ref, ar_ref, ai_ref, bre_ref, bim_ref, crt_ref, cit_ref, hsr_ref, hsi_ref,
             dx_ref, dar_ref, dai_ref, dbr_ref, dbi_ref, dcr_ref, dci_ref,
             hr_ref, hi_ref, lr_ref, li_ref, sr_ref, si_ref, fr_ref, fi_ref, accr_ref, acci_ref):
        ph, b = pl.program_id(0), pl.program_id(1)

        @pl.when((ph == 0) & (b == 0))
        def _():
            sr_ref[...] = jnp.zeros_like(sr_ref)
            si_ref[...] = jnp.zeros_like(si_ref)

        @pl.when((ph == 1) & (b == 0))
        def _():
            _s5_segment_carry(ar_ref, ai_ref, sr_ref, si_ref, steps, True)
            for r in (accr_ref, acci_ref, dbr_ref, dbi_ref, dcr_ref, dci_ref):
                r[...] = jnp.zeros_like(r)

        dyv = dy_ref[...].astype(BF16)
        for j in range(S5_TILES):
            ds_ = dyv[:, j * S5_TI:(j + 1) * S5_TI]
            lr_ref[:, j * S5_TW:(j + 1) * S5_TW] = _dot(ds_, crt_ref[j])
            li_ref[:, j * S5_TW:(j + 1) * S5_TW] = -_dot(ds_, cit_ref[j])
        _s5_scan_rows(ar_ref, ai_ref, lr_ref, li_ref, sr_ref, si_ref, tb, 0, True)

        @pl.when(ph == 1)
        def _():
            xv = x_ref[...].astype(BF16)
            for j in range(S5_TILES):
                xs = xv[:, j * S5_TI:(j + 1) * S5_TI]
                hr_ref[NSEG:, j * S5_TW:(j + 1) * S5_TW] = _dot(xs, bre_ref[j])
                hi_ref[NSEG:, j * S5_TW:(j + 1) * S5_TW] = _dot(xs, bim_ref[j])
            hr_ref[0:NSEG, :] = hsr_ref[0]
            hi_ref[0:NSEG, :] = hsi_ref[0]
            fr_ref[...] = hsr_ref[0]
            fi_ref[...] = hsi_ref[0]
            _s5_scan_rows(ar_ref, ai_ref, hr_ref, hi_ref, fr_ref, fi_ref, tb, NSEG, False)
            lam_r, lam_i = lr_ref[...], li_ref[...]
            hp_r, hp_i = hr_ref[0:rb, :], hi_ref[0:rb, :]
            accr_ref[...] += jnp.sum((lam_r * hp_r + lam_i * hp_i).reshape(tb, NSEG, S5_W), axis=0)
            acci_ref[...] += jnp.sum((lam_i * hp_r - lam_r * hp_i).reshape(tb, NSEG, S5_W), axis=0)
            lam_rb, lam_ib = lam_r.astype(BF16), lam_i.astype(BF16)
            h_rb, h_ib = hr_ref[NSEG:, :].astype(BF16), hi_ref[NSEG:, :].astype(BF16)
            for j in range(S5_TILES):
                cs, ci = slice(j * S5_TW, (j + 1) * S5_TW), slice(j * S5_TI, (j + 1) * S5_TI)
                dbr_ref[j] += _dot(xv[:, ci], lam_rb[:, cs], "tn")
                dbi_ref[j] += _dot(xv[:, ci], lam_ib[:, cs], "tn")
                dx_ref[:, ci] = _dot(lam_rb[:, cs], bre_ref[j], "nt") + _dot(lam_ib[:, cs], bim_ref[j], "nt")
                dcr_ref[j] += _dot(h_rb[:, cs], dyv[:, ci], "tn")
                dci_ref[j] -= _dot(h_ib[:, cs], dyv[:, ci], "tn")

        @pl.when((ph == 1) & (b == nb - 1))
        def _():
            dar_ref[...] = jnp.sum(accr_ref[...], axis=0, keepdims=True)
            dai_ref[...] = jnp.sum(acci_ref[...], axis=0, keepdims=True)

    rev = lambda p, b: (nb - 1 - b, 0)
    row = pl.BlockSpec((1, S5_W), lambda p, b: (0, 0))
    wb = pl.BlockSpec((S5_TILES, S5_TI, S5_TW), lambda p, b: (0, 0, 0))
    wc = pl.BlockSpec((S5_TILES, S5_TW, S5_TI), lambda p, b: (0, 0, 0))
    st = pl.BlockSpec((1, NSEG, S5_W), lambda p, b: (nb - 1 - b, 0, 0))
    big = pltpu.VMEM((rb, S5_W), F32)
    big8 = pltpu.VMEM((rb + NSEG, S5_W), F32)
    small = pltpu.VMEM((NSEG, S5_W), F32)
    return pl.pallas_call(
        body, name="s5_scan_bwd", grid=(2, nb),
        in_specs=[pl.BlockSpec((rb, S5_IN), rev), pl.BlockSpec((rb, S5_IN), rev), row, row, wb, wb, wb, wb, st, st],
        out_specs=[pl.BlockSpec((rb, S5_IN), lambda p, b: (nb - 1 - p * b, 0)), row, row, wb, wb, wc, wc],
        out_shape=[jax.ShapeDtypeStruct((s, S5_IN), F32)] + [jax.ShapeDtypeStruct((1, S5_W), F32)] * 2
        + [jax.ShapeDtypeStruct((S5_TILES, S5_TI, S5_TW), F32)] * 2 + [jax.ShapeDtypeStruct((S5_TILES, S5_TW, S5_TI), F32)] * 2,
        scratch_shapes=[big8, big8, big, big, small, small, small, small, small, small],
        compiler_params=_cparams(("arbitrary", "arbitrary")),
    )(dyp, xp, a_re, a_im, bre, bim, cre_t, cim_t, hs_r, hs_i)


XA_DIM = 256
XA_W = XA_HEADS * XA_DIM


def _xa_probs(qh, kh):
    sc = _dot(qh, kh, "nt") * (XA_DIM ** -0.5)
    ex = jnp.exp(sc - jnp.max(sc, axis=-1, keepdims=True))
    return ex / jnp.sum(ex, axis=-1, keepdims=True)


def xa_fwd(proj, kv):
    s = proj.shape[0]
    tq = min(512, s)

    def body(q_ref, kv_ref, o_ref):
        for h in range(XA_HEADS):
            sl = slice(h * XA_DIM, (h + 1) * XA_DIM)
            p = _xa_probs(q_ref[:, sl], kv_ref[:, sl])
            o_ref[:, sl] = _dot(p, kv_ref[:, XA_W + h * XA_DIM:XA_W + (h + 1) * XA_DIM])

    return pl.pallas_call(
        body, name="xa_fwd", grid=(s // tq,),
        in_specs=[pl.BlockSpec((tq, XA_W), lambda i: (i, QC_CB)), pl.BlockSpec(kv.shape, lambda i: (0, 0))],
        out_specs=pl.BlockSpec((tq, XA_W), lambda i: (i, 0)),
        out_shape=jax.ShapeDtypeStruct((s, XA_W), F32),
        compiler_params=_cparams(("parallel",)),
    )(proj, kv)


def xa_bwd(do, proj, kv, dproj):
    s = proj.shape[0]
    tq = min(512, s)

    def body(do_ref, q_ref, kv_ref, buf_ref, dq_ref, dkv_ref):
        @pl.when(pl.program_id(0) == 0)
        def _():
            dkv_ref[...] = jnp.zeros_like(dkv_ref)

        for h in range(XA_HEADS):
            sl = slice(h * XA_DIM, (h + 1) * XA_DIM)
            sv = slice(XA_W + h * XA_DIM, XA_W + (h + 1) * XA_DIM)
            qh, kh, vh, doh = q_ref[:, sl], kv_ref[:, sl], kv_ref[:, sv], do_ref[:, sl]
            p = _xa_probs(qh, kh)
            dp = _dot(doh, vh, "nt")
            ds_ = p * (dp - jnp.sum(dp * p, axis=-1, keepdims=True)) * (XA_DIM ** -0.5)
            dq_ref[:, sl] = _dot(ds_, kh).astype(BF16)
            dkv_ref[:, sl] += _dot(ds_, qh, "tn")
            dkv_ref[:, sv] += _dot(p, doh, "tn")

    return pl.pallas_call(
        body, name="xa_bwd", grid=(s // tq,),
        in_specs=[pl.BlockSpec((tq, XA_W), lambda i: (i, 0)), pl.BlockSpec((tq, XA_W), lambda i: (i, QC_CB)),
                  pl.BlockSpec(kv.shape, lambda i: (0, 0)), pl.BlockSpec(memory_space=pl.ANY)],
        out_specs=[pl.BlockSpec((tq, XA_W), lambda i: (i, QC_CB)), pl.BlockSpec(kv.shape, lambda i: (0, 0))],
        out_shape=[jax.ShapeDtypeStruct(dproj.shape, dproj.dtype), jax.ShapeDtypeStruct(kv.shape, F32)],
        input_output_aliases={3: 0}, compiler_params=_cparams(("arbitrary",)),
    )(do, proj, kv, dproj)


def _adamw_math(wv, gv, mv, vv):
    m2 = ADAM_B1 * mv + (1.0 - ADAM_B1) * gv
    v2 = ADAM_B2 * vv + (1.0 - ADAM_B2) * (gv * gv)
    m_hat = m2 / (1.0 - ADAM_B1 ** ADAM_STEP)
    v_hat = v2 / (1.0 - ADAM_B2 ** ADAM_STEP)
    return -ADAM_LR * (m_hat / (jnp.sqrt(v_hat) + ADAM_EPS) + ADAM_WD * wv), m2, v2


def adamw(w, g, m, v, name):
    lead = (0,) * (w.ndim - 2)
    rows, cols = w.shape[-2:]
    tr = rows
    while tr * cols * 4 * 7 * 2 > 36 * 2 ** 20 and tr % 16 == 0:
        tr //= 2

    def body(w_ref, g_ref, m_ref, v_ref, d_ref, m2_ref, v2_ref):
        d_ref[...], m2_ref[...], v2_ref[...] = _adamw_math(w_ref[...], g_ref[...], m_ref[...], v_ref[...])

    spec = pl.BlockSpec((1,) * len(lead) + (tr, cols), lambda i: lead + (i, 0))
    return pl.pallas_call(
        body, name=name, grid=(rows // tr,), in_specs=[spec] * 4, out_specs=[spec] * 3,
        out_shape=[jax.ShapeDtypeStruct(w.shape, F32)] * 3, compiler_params=_cparams(("parallel",)),
    )(w, g, m, v)


def _seg_perm(a):
    s, w = a.shape
    return a.reshape(NSEG, s // NSEG, w).transpose(1, 0, 2).reshape(s, w)


def _seg_unperm(a):
    s, w = a.shape
    return a.reshape(s // NSEG, NSEG, w).transpose(1, 0, 2).reshape(s, w)


def _block_diag(t):
    nt, _, r, c = t.shape
    eye = jnp.eye(8, dtype=bool)
    return jnp.where(eye[None, :, None, :, None], t[:, :, :, None, :], 0.0).reshape(nt, 8 * r, 8 * c)


def _block_diag_inv(d, r, c):
    d5 = d.reshape(d.shape[0], 8, r, 8, c)
    return jnp.stack([d5[:, g, :, g, :] for g in range(8)], axis=1)


def _s5_b_tiles(bb):
    return _block_diag(bb.reshape(S5_TILES, 8, S5_STATE, S5_GROUP).transpose(0, 1, 3, 2))


def _s5_b_untile(d):
    return _block_diag_inv(d, S5_GROUP, S5_STATE).transpose(0, 1, 3, 2).reshape(S5_GROUPS, S5_STATE * S5_GROUP)


def _s5_c_tiles(c):
    return _block_diag(c.reshape(S5_TILES, 8, S5_GROUP, S5_STATE).transpose(0, 1, 3, 2))


def _s5_c_untile(d):
    return _block_diag_inv(d, S5_STATE, S5_GROUP).transpose(0, 1, 3, 2).reshape(S5_GROUPS, S5_GROUP, S5_STATE)


def s5_ssm_fwd(xb, lam_re, lam_im, log_dt, b_re, b_im, c_re, c_im):
    br, bi = b_re.reshape(S5_GROUPS, -1), b_im.reshape(S5_GROUPS, -1)
    ldt = log_dt.reshape(S5_GROUPS, 1)
    ab_re, ab_im, bb_re, bb_im = s5_param_fwd(lam_re, lam_im, ldt, br, bi)
    a_re, a_im = ab_re.reshape(1, S5_W), ab_im.reshape(1, S5_W)
    bre, bim = _s5_b_tiles(bb_re).astype(BF16), _s5_b_tiles(bb_im).astype(BF16)
    cre, cim = _s5_c_tiles(c_re).astype(BF16), _s5_c_tiles(c_im).astype(BF16)
    xp = _seg_perm(xb)
    yp, hs_r, hs_i = s5_scan_fwd(xp, a_re, a_im, bre, bim, cre, cim)
    saved = (xp, a_re, a_im, bre, bim, cre, cim, hs_r, hs_i, (lam_re, lam_im, ldt, br, bi))
    return _seg_unperm(yp), saved


def s5_ssm_bwd(dy, saved):
    xp, a_re, a_im, bre, bim, cre, cim, hs_r, hs_i, params = saved
    cre_t, cim_t = cre.transpose(0, 2, 1), cim.transpose(0, 2, 1)
    dxp, dar, dai, dbr, dbi, dcr, dci = s5_scan_bwd(_seg_perm(dy), xp, a_re, a_im, bre, bim, cre_t, cim_t, hs_r, hs_i)
    dlr, dli, dldt, db_re, db_im = s5_param_bwd(*params, dar.reshape(S5_GROUPS, S5_STATE), dai.reshape(S5_GROUPS, S5_STATE),
                                                _s5_b_untile(dbr), _s5_b_untile(dbi))
    shape_b = (S5_GROUPS, S5_STATE, S5_GROUP)
    return (_seg_unperm(dxp), dlr, dli, dldt.reshape(S5_GROUPS), db_re.reshape(shape_b), db_im.reshape(shape_b),
            _s5_c_untile(dcr), _s5_c_untile(dci))


_MESH = pl.DeviceIdType.MESH
_HBM = pl.BlockSpec(memory_space=pltpu.HBM)
N_DEV = 8


def _position():
    return lax.axis_index("x"), lax.axis_index("y"), lax.axis_index("c")


D2D_CHUNK_BYTES = 2 ** 20


def _chunk_rows(rows, cols, itemsize):
    return _row_tile(rows, 16, max(16, D2D_CHUNK_BYTES // (cols * itemsize)))


def _rows(start, size, unit=16):
    return pl.ds(pl.multiple_of(start, unit), size)


def _push_to_sibling(chunks, stages, recv_sems, store_sems, sibling, lag=2, add=None):
    in_slot, used, stores, summed, busy = {}, {}, {}, {}, {}

    def push(q, slot):
        _, _, sid, land, _ = chunks[q]
        buf, send_sems, _ = stages[sid]
        return pltpu.make_async_remote_copy(src_ref=buf.at[slot], dst_ref=land, send_sem=send_sems.at[slot],
                                            recv_sem=recv_sems.at[q], device_id=sibling, device_id_type=_MESH)

    def receive(q):
        _, _, sid, land, out = chunks[q]
        if add is None:
            push(q, 0).wait_recv()
            st = pltpu.make_async_copy(land, out, store_sems.at[q])
        else:
            buf, load_sems = add[1][sid]
            slot = summed.get(sid, 0) % 2
            summed[sid] = summed.get(sid, 0) + 1
            if (sid, slot) in busy:
                stores.pop(busy.pop((sid, slot))).wait()
            own = pltpu.make_async_copy(add[0][q], buf.at[slot], load_sems.at[slot])
            own.start()
            push(q, 0).wait_recv()
            own.wait()
            buf[slot] = (buf[slot].astype(F32) + land[...].astype(F32)).astype(buf.dtype)
            st = pltpu.make_async_copy(buf.at[slot], out, store_sems.at[q])
            busy[(sid, slot)] = q
        st.start()
        stores[q] = st

    for q, (pre, src, sid, _, _) in enumerate(chunks):
        if pre is not None:
            pre()
        slot = used.get(sid, 0) % 2
        used[sid] = used.get(sid, 0) + 1
        if (sid, slot) in in_slot:
            in_slot.pop((sid, slot)).wait_send()
        load = pltpu.make_async_copy(src, stages[sid][0].at[slot], stages[sid][2].at[slot])
        load.start()
        load.wait()
        cp = push(q, slot)
        cp.start()
        in_slot[(sid, slot)] = cp
        if q >= lag:
            receive(q - lag)
    for q in range(max(0, len(chunks) - lag), len(chunks)):
        receive(q)
    for cp in in_slot.values():
        cp.wait_send()
    for st in stores.values():
        st.wait()


def _stage_scratch(shapes_dtypes):
    out = []
    for shape, dtype in shapes_dtypes:
        out += [pltpu.VMEM((2,) + shape, dtype), pltpu.SemaphoreType.DMA((2,)), pltpu.SemaphoreType.DMA((2,))]
    return out


def allgather_weights(ws, convw, name):
    n = len(ws)
    extra = 0 if convw is None else 1
    halves = [w.shape[0] // 2 for w in ws]
    steps = [_chunk_rows(h, w.shape[1], w.dtype.itemsize) for h, w in zip(halves, ws)]
    per_peer = [h // s for h, s in zip(halves, steps)]
    nchunks = 3 * sum(per_peer)

    def body(*refs):
        w_refs = refs[:n]
        wo_refs = refs[n + extra:2 * n + extra]
        scratch = refs[2 * (n + extra):]
        send_sems, recv_sems, local_sems, fwd_recv_sems, store_sems = scratch[:5]
        lands = scratch[5:5 + n]
        stage_refs = scratch[5 + n:]
        stages = [tuple(stage_refs[3 * i:3 * i + 3]) for i in range(n)]
        x, y, c = _position()
        mine = 2 * x + y
        peers = [(1 - x, y), (x, 1 - y), (1 - x, 1 - y)]
        blocks = [2 * px + py for px, py in peers]
        local = [pltpu.make_async_copy(w_refs[i], wo_refs[i].at[mine], local_sems.at[i]) for i in range(n)]
        if extra:
            c_ref, co_ref = refs[n], refs[2 * n + 1]
            local.append(pltpu.make_async_copy(c_ref, co_ref.at[mine], local_sems.at[n]))
        for cp in local:
            cp.start()

        def ici(i, k, block):
            rows = _rows(c * halves[i], halves[i])
            return pltpu.make_async_remote_copy(src_ref=w_refs[i].at[rows, :], dst_ref=wo_refs[i].at[block, rows, :],
                                                send_sem=send_sems.at[3 * i + k], recv_sem=recv_sems.at[3 * i + k],
                                                device_id=(*peers[k], c), device_id_type=_MESH)

        def conv(k, block):
            return pltpu.make_async_remote_copy(src_ref=c_ref, dst_ref=co_ref.at[block], send_sem=send_sems.at[3 * n + k],
                                                recv_sem=recv_sems.at[3 * n + k], device_id=(*peers[k], c), device_id_type=_MESH)

        sends = [ici(i, k, mine) for k in range(3) for i in range(n)] + ([conv(k, mine) for k in range(3)] if extra else [])
        for cp in sends:
            cp.start()
        chunks = []
        for k in range(3):
            for i in range(n):
                for q in range(per_peer[i]):
                    pre = functools.partial(lambda i, k: ici(i, k, blocks[k]).wait_recv(), i, k) if q == 0 else None
                    src = wo_refs[i].at[blocks[k], _rows(c * halves[i] + q * steps[i], steps[i]), :]
                    out = wo_refs[i].at[blocks[k], _rows((1 - c) * halves[i] + q * steps[i], steps[i]), :]
                    chunks.append((pre, src, i, lands[i].at[k * per_peer[i] + q], out))
        _push_to_sibling(chunks, stages, fwd_recv_sems, store_sems, (x, y, 1 - c))
        if extra:
            for k in range(3):
                conv(k, blocks[k]).wait_recv()
        for cp in sends:
            cp.wait_send()
        for cp in local:
            cp.wait()

    nsem = 3 * (n + extra)
    scratch = [pltpu.SemaphoreType.DMA((nsem,)), pltpu.SemaphoreType.DMA((nsem,)), pltpu.SemaphoreType.DMA((n + extra,)),
               pltpu.SemaphoreType.DMA((nchunks,)), pltpu.SemaphoreType.DMA((nchunks,))]
    scratch += [pltpu.VMEM((3 * p, s, w.shape[1]), w.dtype) for p, s, w in zip(per_peer, steps, ws)]
    scratch += _stage_scratch([((s, w.shape[1]), w.dtype) for s, w in zip(steps, ws)])
    operands = list(ws) + ([convw] if extra else [])
    return pl.pallas_call(
        body, name=name, in_specs=[_HBM] * len(operands), out_specs=[_HBM] * len(operands),
        out_shape=[jax.ShapeDtypeStruct((4,) + w.shape, w.dtype) for w in operands],
        scratch_shapes=scratch, compiler_params=pltpu.CompilerParams(vmem_limit_bytes=VMEM_LIMIT),
    )(*operands)


def exchange_cores(gs, name):
    n = len(gs)
    halves = [g.shape[1] // 2 for g in gs]
    steps = [_chunk_rows(h, g.shape[2], g.dtype.itemsize) for h, g in zip(halves, gs)]
    per_shard = [h // s for h, s in zip(halves, steps)]
    nchunks = 4 * sum(per_shard)

    def body(*refs):
        g_refs, got_refs, scratch = refs[:n], refs[n:2 * n], refs[2 * n:]
        recv_sems, store_sems = scratch[:2]
        lands = scratch[2:2 + n]
        stage_refs = scratch[2 + n:2 + 4 * n]
        stages = [tuple(stage_refs[3 * i:3 * i + 3]) for i in range(n)]
        own_stages = [tuple(scratch[2 + 4 * n + 2 * i:2 + 4 * n + 2 * i + 2]) for i in range(n)]
        x, y, c = _position()
        chunks, owns = [], []
        for i in range(n):
            for j in range(4):
                for q in range(per_shard[i]):
                    src = g_refs[i].at[j, _rows((1 - c) * halves[i] + q * steps[i], steps[i]), :]
                    out = got_refs[i].at[j, pl.ds(q * steps[i], steps[i]), :]
                    chunks.append((None, src, i, lands[i].at[j * per_shard[i] + q], out))
                    owns.append(g_refs[i].at[j, _rows(c * halves[i] + q * steps[i], steps[i]), :])
        _push_to_sibling(chunks, stages, recv_sems, store_sems, (x, y, 1 - c), add=(owns, own_stages))

    scratch = [pltpu.SemaphoreType.DMA((nchunks,)), pltpu.SemaphoreType.DMA((nchunks,))]
    scratch += [pltpu.VMEM((4 * p, s, g.shape[2]), g.dtype) for p, s, g in zip(per_shard, steps, gs)]
    scratch += _stage_scratch([((s, g.shape[2]), g.dtype) for s, g in zip(steps, gs)])
    for s, g in zip(steps, gs):
        scratch += [pltpu.VMEM((2, s, g.shape[2]), g.dtype), pltpu.SemaphoreType.DMA((2,))]
    return pl.pallas_call(
        body, name=name, in_specs=[_HBM] * n, out_specs=[_HBM] * n,
        out_shape=[jax.ShapeDtypeStruct((4, h, g.shape[2]), g.dtype) for h, g in zip(halves, gs)],
        scratch_shapes=scratch, compiler_params=pltpu.CompilerParams(vmem_limit_bytes=VMEM_LIMIT),
    )(*gs)


def chips_side(cs):
    n = len(cs)

    def copies(c_refs, o_refs, sems):
        send_sems, recv_sems, local_sems = sems
        x, y, c = _position()
        mine = 2 * x + y
        peers = [(1 - x, y), (x, 1 - y), (1 - x, 1 - y)]
        blocks = [2 * px + py for px, py in peers]
        local = [pltpu.make_async_copy(c_refs[i].at[mine], o_refs[i].at[mine], local_sems.at[i]) for i in range(n)]

        def copy(i, k, sending):
            return pltpu.make_async_remote_copy(src_ref=c_refs[i].at[blocks[k]], dst_ref=o_refs[i].at[mine if sending else blocks[k]],
                                                send_sem=send_sems.at[3 * i + k], recv_sem=recv_sems.at[3 * i + k],
                                                device_id=(*peers[k], c), device_id_type=_MESH)

        sends = [copy(i, k, True) for k in range(3) for i in range(n)]
        return local, sends, lambda: [copy(i, k, False) for k in range(3) for i in range(n)]

    def start(*refs):
        local, sends, _ = copies(*refs)
        for cp in local + sends:
            cp.start()

    def finish(*refs):
        local, sends, arrivals = copies(*refs)
        for cp in arrivals():
            cp.wait_recv()
        for cp in sends:
            cp.wait_send()
        for cp in local:
            cp.wait()

    scratch = [pltpu.SemaphoreType.DMA((3 * n,)), pltpu.SemaphoreType.DMA((3 * n,)), pltpu.SemaphoreType.DMA((n,))]
    return Side(list(cs), [jax.ShapeDtypeStruct(a.shape, a.dtype) for a in cs], scratch, start, finish)


def gather_side(ws, convw):
    n = len(ws)
    halves = [w.shape[0] // 2 for w in ws]

    def copies(in_refs, out_refs, sems):
        w_refs, c_ref, wo_refs, co_ref = in_refs[:n], in_refs[n], out_refs[:n], out_refs[n]
        send_sems, recv_sems, local_sems = sems
        x, y, c = _position()
        mine = 2 * x + y
        peers = [(1 - x, y), (x, 1 - y), (1 - x, 1 - y)]
        blocks = [2 * px + py for px, py in peers]
        local = [pltpu.make_async_copy(w_refs[i], wo_refs[i].at[mine], local_sems.at[i]) for i in range(n)]
        local.append(pltpu.make_async_copy(c_ref, co_ref.at[mine], local_sems.at[n]))

        def ici(i, k, block):
            rows = _rows(c * halves[i], halves[i])
            return pltpu.make_async_remote_copy(src_ref=w_refs[i].at[rows, :], dst_ref=wo_refs[i].at[block, rows, :],
                                                send_sem=send_sems.at[3 * i + k], recv_sem=recv_sems.at[3 * i + k],
                                                device_id=(*peers[k], c), device_id_type=_MESH)

        def conv(k, block):
            return pltpu.make_async_remote_copy(src_ref=c_ref, dst_ref=co_ref.at[block], send_sem=send_sems.at[3 * n + k],
                                                recv_sem=recv_sems.at[3 * n + k], device_id=(*peers[k], c), device_id_type=_MESH)

        sends = [ici(i, k, mine) for k in range(3) for i in range(n)] + [conv(k, mine) for k in range(3)]
        return local, sends, lambda: ([ici(i, k, blocks[k]) for k in range(3) for i in range(n)]
                                      + [conv(k, blocks[k]) for k in range(3)])

    def start(*refs):
        local, sends, _ = copies(*refs)
        for cp in local + sends:
            cp.start()

    def finish(*refs):
        local, sends, arrivals = copies(*refs)
        for cp in arrivals():
            cp.wait_recv()
        for cp in sends:
            cp.wait_send()
        for cp in local:
            cp.wait()

    nsem = 3 * n + 3
    scratch = [pltpu.SemaphoreType.DMA((nsem,)), pltpu.SemaphoreType.DMA((nsem,)), pltpu.SemaphoreType.DMA((n + 1,))]
    operands = list(ws) + [convw]
    return Side(operands, [jax.ShapeDtypeStruct((4,) + w.shape, w.dtype) for w in operands], scratch, start, finish)


def forward_halves(stacked):
    n = len(stacked)
    halves = [w.shape[1] // 2 for w in stacked]
    steps = [_chunk_rows(h, w.shape[2], w.dtype.itemsize) for h, w in zip(halves, stacked)]
    per_peer = [h // s for h, s in zip(halves, steps)]
    nchunks = 3 * sum(per_peer)

    def body(*refs):
        w_refs, o_refs = refs[:n], refs[n:2 * n]
        scratch = refs[2 * n:]
        recv_sems, store_sems = scratch[:2]
        lands = scratch[2:2 + n]
        stages = [tuple(scratch[2 + n + 3 * i:2 + n + 3 * i + 3]) for i in range(n)]
        x, y, c = _position()
        blocks = [2 * px + py for px, py in ((1 - x, y), (x, 1 - y), (1 - x, 1 - y))]
        chunks = []
        for k in range(3):
            for i in range(n):
                for q in range(per_peer[i]):
                    src = w_refs[i].at[blocks[k], _rows(c * halves[i] + q * steps[i], steps[i]), :]
                    out = o_refs[i].at[blocks[k], _rows((1 - c) * halves[i] + q * steps[i], steps[i]), :]
                    chunks.append((None, src, i, lands[i].at[k * per_peer[i] + q], out))
        _push_to_sibling(chunks, stages, recv_sems, store_sems, (x, y, 1 - c))

    scratch = [pltpu.SemaphoreType.DMA((nchunks,)), pltpu.SemaphoreType.DMA((nchunks,))]
    scratch += [pltpu.VMEM((3 * p, s, w.shape[2]), w.dtype) for p, s, w in zip(per_peer, steps, stacked)]
    scratch += _stage_scratch([((s, w.shape[2]), w.dtype) for s, w in zip(steps, stacked)])
    return pl.pallas_call(
        body, name="forward_halves", in_specs=[_HBM] * n, out_specs=[_HBM] * n,
        out_shape=[jax.ShapeDtypeStruct(w.shape, w.dtype) for w in stacked], input_output_aliases={i: i for i in range(n)},
        scratch_shapes=scratch, compiler_params=pltpu.CompilerParams(vmem_limit_bytes=VMEM_LIMIT),
    )(*stacked)


def small_side(small):
    def copies(in_refs, out_refs, sems):
        (s_ref,), (so_ref,), (send_sems, recv_sems, local_sem) = in_refs, out_refs, sems
        x, y, c = _position()
        me = 4 * x + 2 * y + c
        local = pltpu.make_async_copy(s_ref, so_ref.at[me], local_sem)

        def copy(r, sending):
            px, py, pc = (1 - x if r & 4 else x, 1 - y if r & 2 else y, 1 - c if r & 1 else c)
            slot = me if sending else 4 * px + 2 * py + pc
            return pltpu.make_async_remote_copy(src_ref=s_ref, dst_ref=so_ref.at[slot], send_sem=send_sems.at[r - 1],
                                                recv_sem=recv_sems.at[r - 1], device_id=(px, py, pc), device_id_type=_MESH)

        return local, [copy(r, True) for r in range(1, N_DEV)], lambda: [copy(r, False) for r in range(1, N_DEV)]

    def start(*refs):
        local, sends, _ = copies(*refs)
        for cp in [local] + sends:
            cp.start()

    def finish(*refs):
        local, sends, arrivals = copies(*refs)
        for cp in arrivals():
            cp.wait_recv()
        for cp in sends:
            cp.wait_send()
        local.wait()

    scratch = [pltpu.SemaphoreType.DMA((N_DEV - 1,)), pltpu.SemaphoreType.DMA((N_DEV - 1,)), pltpu.SemaphoreType.DMA]
    return Side([small], [jax.ShapeDtypeStruct((N_DEV,) + small.shape, small.dtype)], scratch, start, finish)


def combine_sides(a, b):
    na, oa, sa = len(a.operands), len(a.out_shapes), len(a.scratch)

    def split(ins, outs, scr):
        return (ins[:na], outs[:oa], scr[:sa]), (ins[na:], outs[oa:], scr[sa:])

    def start(*refs):
        ra, rb = split(*refs)
        a.start(*ra)
        b.start(*rb)

    def finish(*refs):
        ra, rb = split(*refs)
        a.finish(*ra)
        b.finish(*rb)

    return Side(a.operands + b.operands, a.out_shapes + b.out_shapes, a.scratch + b.scratch, start, finish)


def exchange_small(small):
    side = small_side(small)

    def body(s_ref, so_ref, *sems):
        side.start((s_ref,), (so_ref,), sems)
        side.finish((s_ref,), (so_ref,), sems)

    return pl.pallas_call(
        body, name="exchange_small", in_specs=[_HBM], out_specs=_HBM, out_shape=side.out_shapes[0], scratch_shapes=side.scratch,
    )(small)


def pair_sum(core, g, got, name):
    nb, rows, cols = got.shape
    tr = _row_tile(rows, 16, max(16, (2 * 2 ** 20) // (cols * g.dtype.itemsize)))
    nblk = rows // tr

    def body(c_ref, a_ref, b_ref, o_ref):
        o_ref[...] = (a_ref[...].astype(F32) + b_ref[...].astype(F32)).astype(o_ref.dtype)

    spec = pl.BlockSpec((1, tr, cols), lambda j, i, c_ref: (j, i, 0))
    mine = pl.BlockSpec((1, tr, cols), lambda j, i, c_ref: (j, c_ref[0] * nblk + i, 0))
    return pl.pallas_call(
        body, name=name,
        grid_spec=pltpu.PrefetchScalarGridSpec(num_scalar_prefetch=1, grid=(nb, nblk), in_specs=[mine, spec], out_specs=spec),
        out_shape=jax.ShapeDtypeStruct(got.shape, g.dtype), compiler_params=_cparams(("parallel", "parallel")),
    )(core, g, got)


def sum_chips(core, pieces, name):
    nb, rows, cols = pieces.shape
    tr = _row_tile(rows, 16, max(16, (6 * 2 ** 20) // (nb * cols * pieces.dtype.itemsize)))
    nblk = rows // tr

    def body(c_ref, p_ref, o_ref):
        acc = p_ref[0].astype(F32)
        for i in range(1, nb):
            acc = acc + p_ref[i].astype(F32)
        o_ref[0] = acc

    return pl.pallas_call(
        body, name=name,
        grid_spec=pltpu.PrefetchScalarGridSpec(
            num_scalar_prefetch=1, grid=(nblk,),
            in_specs=[pl.BlockSpec((nb, tr, cols), lambda i, c_ref: (0, i, 0))],
            out_specs=pl.BlockSpec((1, tr, cols), lambda i, c_ref: (0, c_ref[0] * nblk + i, 0))),
        out_shape=jax.ShapeDtypeStruct((1, 2 * rows, cols), F32), compiler_params=_cparams(("parallel",)),
    )(core, pieces)


def sibling_exchange(fulls):
    n = len(fulls)
    halves = [f.shape[1] // 2 for f in fulls]
    steps = [_chunk_rows(h, f.shape[2], f.dtype.itemsize) for h, f in zip(halves, fulls)]
    counts = [h // s for h, s in zip(halves, steps)]
    nchunks = sum(counts)

    def body(*refs):
        f_refs, o_refs = refs[:n], refs[n:2 * n]
        scratch = refs[2 * n:]
        recv_sems, store_sems = scratch[:2]
        lands = scratch[2:2 + n]
        stages = [tuple(scratch[2 + n + 3 * i:2 + n + 3 * i + 3]) for i in range(n)]
        x, y, c = _position()
        chunks = []
        for i in range(n):
            for q in range(counts[i]):
                src = f_refs[i].at[0, _rows(c * halves[i] + q * steps[i], steps[i]), :]
                out = o_refs[i].at[0, _rows((1 - c) * halves[i] + q * steps[i], steps[i]), :]
                chunks.append((None, src, i, lands[i].at[q], out))
        _push_to_sibling(chunks, stages, recv_sems, store_sems, (x, y, 1 - c))

    scratch = [pltpu.SemaphoreType.DMA((nchunks,)), pltpu.SemaphoreType.DMA((nchunks,))]
    scratch += [pltpu.VMEM((k, s, f.shape[2]), f.dtype) for k, s, f in zip(counts, steps, fulls)]
    scratch += _stage_scratch([((s, f.shape[2]), f.dtype) for s, f in zip(steps, fulls)])
    return pl.pallas_call(
        body, name="sibling_exchange", in_specs=[_HBM] * n, out_specs=[_HBM] * n,
        out_shape=[jax.ShapeDtypeStruct(f.shape, f.dtype) for f in fulls],
        input_output_aliases={i: i for i in range(n)},
        scratch_shapes=scratch, compiler_params=pltpu.CompilerParams(vmem_limit_bytes=VMEM_LIMIT),
    )(*fulls)


def _row_tile(rows, unit, max_rows):
    best = unit
    for t in range(unit, min(rows, max_rows) + 1, unit):
        if rows % t == 0:
            best = t
    return best


def sum_pieces(pieces, name):
    n, rows, cols = pieces.shape
    tr = _row_tile(rows, 16, max(16, (6 * 2 ** 20) // (n * cols * pieces.dtype.itemsize)))

    def body(p_ref, o_ref):
        acc = p_ref[0].astype(F32)
        for i in range(1, n):
            acc = acc + p_ref[i].astype(F32)
        o_ref[...] = acc

    return pl.pallas_call(
        body, name=name, grid=(rows // tr,),
        in_specs=[pl.BlockSpec((n, tr, cols), lambda i: (0, i, 0))], out_specs=pl.BlockSpec((tr, cols), lambda i: (i, 0)),
        out_shape=jax.ShapeDtypeStruct((rows, cols), F32), compiler_params=_cparams(("parallel",)),
    )(pieces)


BIG = ("w_in", "s5_w_glu", "w_kv_mem", "w_br_a", "w_br_b", "w_br_c", "w_out")
COL_SHARDED = ("w_in", "s5_w_glu", "w_br_a", "w_br_b", "w_br_c")
SMALL = ("gdn_a_log", "gdn_dt_bias", "gdn_norm_g", "s5_lambda_re", "s5_lambda_im", "s5_log_dt",
         "s5_b_re", "s5_b_im", "s5_c_re", "s5_c_im", "s5_d", "mem_norm_g", "final_g", "norm_g")
WEIGHTS = ("norm_g", "w_in", "conv_w", "gdn_a_log", "gdn_dt_bias", "gdn_norm_g", "s5_lambda_re", "s5_lambda_im",
           "s5_log_dt", "s5_b_re", "s5_b_im", "s5_c_re", "s5_c_im", "s5_d", "s5_w_glu", "mem_norm_g", "w_kv_mem",
           "w_br_a", "w_br_b", "w_br_c", "w_out", "final_g")
W_IN_SPLIT = 4096


W_IN_COLS = PROJ_W - BA_PAD + 2 * NHEAD
W_IN_GATES = W_IN_COLS - GATE_W
W_IN_MOVES = ((0, W_IN_SPLIT, GATE_W), (W_IN_SPLIT, W_IN_SPLIT + 2 * NHEAD, PROJ_W - BA_PAD - W_IN_SPLIT),
              (W_IN_SPLIT + 2 * NHEAD, W_IN_GATES, GATE_W - 2 * NHEAD), (W_IN_GATES, W_IN_COLS, -W_IN_GATES))


def _pack_w_in(shards):
    cs = shards.shape[2]
    parts = []
    for a, b, _ in sorted(W_IN_MOVES, key=lambda move: move[0] + move[2]):
        while a < b:
            j = a // cs
            hi = min(b, (j + 1) * cs)
            parts.append(shards[j, :, a - j * cs:hi - j * cs])
            a = hi
    parts.append(jnp.zeros((shards.shape[1], BA_PAD - 2 * NHEAD), shards.dtype))
    return jnp.concatenate(parts, axis=1)


def _unpack_w_in(wp):
    cs = W_IN_COLS // 4
    shards = []
    for j in range(4):
        parts = []
        for lo, hi, shift in W_IN_MOVES:
            s, e = max(j * cs, lo), min((j + 1) * cs, hi)
            if s < e:
                parts.append(wp[:, s + shift:e + shift])
        shards.append(jnp.concatenate(parts, axis=1))
    return jnp.stack(shards)


def _pack_small(arrs):
    parts = []
    for a in arrs:
        f = a.reshape(-1).astype(F32)
        parts.append(jnp.pad(f, (0, (-f.shape[0]) % 128)))
    flat = jnp.concatenate(parts)
    rows = flat.shape[0] // 128
    return jnp.pad(flat.reshape(rows, 128), ((0, (-rows) % 16), (0, 0)))


def _unpack_small(flat2d, shapes):
    f = flat2d.reshape(-1)
    out, off = [], 0
    for shp in shapes:
        n = math.prod(shp)
        out.append(f[off:off + n].reshape(shp))
        off += n + (-n) % 128
    return out


def kernel(x, mem, norm_g, w_in, conv_w, gdn_a_log, gdn_dt_bias, gdn_norm_g, s5_lambda_re, s5_lambda_im, s5_log_dt, s5_b_re, s5_b_im, s5_c_re, s5_c_im, s5_d, s5_w_glu, mem_norm_g, w_kv_mem, w_br_a, w_br_b, w_br_c, w_out, final_g, loss_target, m_norm_g, m_w_in, m_conv_w, m_gdn_a_log, m_gdn_dt_bias, m_gdn_norm_g, m_s5_lambda_re, m_s5_lambda_im, m_s5_log_dt, m_s5_b_re, m_s5_b_im, m_s5_c_re, m_s5_c_im, m_s5_d, m_s5_w_glu, m_mem_norm_g, m_w_kv_mem, m_w_br_a, m_w_br_b, m_w_br_c, m_w_out, m_final_g, v_norm_g, v_w_in, v_conv_w, v_gdn_a_log, v_gdn_dt_bias, v_gdn_norm_g, v_s5_lambda_re, v_s5_lambda_im, v_s5_log_dt, v_s5_b_re, v_s5_b_im, v_s5_c_re, v_s5_c_im, v_s5_d, v_s5_w_glu, v_mem_norm_g, v_w_kv_mem, v_w_br_a, v_w_br_b, v_w_br_c, v_w_out, v_final_g):
    wts = dict(norm_g=norm_g, w_in=w_in, conv_w=conv_w, gdn_a_log=gdn_a_log, gdn_dt_bias=gdn_dt_bias, gdn_norm_g=gdn_norm_g,
               s5_lambda_re=s5_lambda_re, s5_lambda_im=s5_lambda_im, s5_log_dt=s5_log_dt, s5_b_re=s5_b_re, s5_b_im=s5_b_im,
               s5_c_re=s5_c_re, s5_c_im=s5_c_im, s5_d=s5_d, s5_w_glu=s5_w_glu, mem_norm_g=mem_norm_g, w_kv_mem=w_kv_mem,
               w_br_a=w_br_a, w_br_b=w_br_b, w_br_c=w_br_c, w_out=w_out, final_g=final_g)
    mom = dict(norm_g=m_norm_g, w_in=m_w_in, conv_w=m_conv_w, gdn_a_log=m_gdn_a_log, gdn_dt_bias=m_gdn_dt_bias,
               gdn_norm_g=m_gdn_norm_g, s5_lambda_re=m_s5_lambda_re, s5_lambda_im=m_s5_lambda_im, s5_log_dt=m_s5_log_dt,
               s5_b_re=m_s5_b_re, s5_b_im=m_s5_b_im, s5_c_re=m_s5_c_re, s5_c_im=m_s5_c_im, s5_d=m_s5_d, s5_w_glu=m_s5_w_glu,
               mem_norm_g=m_mem_norm_g, w_kv_mem=m_w_kv_mem, w_br_a=m_w_br_a, w_br_b=m_w_br_b, w_br_c=m_w_br_c, w_out=m_w_out,
               final_g=m_final_g)
    vel = dict(norm_g=v_norm_g, w_in=v_w_in, conv_w=v_conv_w, gdn_a_log=v_gdn_a_log, gdn_dt_bias=v_gdn_dt_bias,
               gdn_norm_g=v_gdn_norm_g, s5_lambda_re=v_s5_lambda_re, s5_lambda_im=v_s5_lambda_im, s5_log_dt=v_s5_log_dt,
               s5_b_re=v_s5_b_re, s5_b_im=v_s5_b_im, s5_c_re=v_s5_c_re, s5_c_im=v_s5_c_im, s5_d=v_s5_d, s5_w_glu=v_s5_w_glu,
               mem_norm_g=v_mem_norm_g, w_kv_mem=v_w_kv_mem, w_br_a=v_w_br_a, w_br_b=v_w_br_b, w_br_c=v_w_br_c, w_out=v_w_out,
               final_g=v_final_g)
    x2, mem2, tgt = x[0], mem[0], loss_target[0]
    s, d = x2.shape
    n_chunks = s // CHUNK

    shards = [wts[n][0].astype(BF16) for n in BIG]
    wp = _pack_w_in(allgather_weights(shards[:1], None, "allgather_w_in")[0])
    mm = functools.partial(matmul, tm=1024, tn=1024)
    u, r1 = rms_fwd(x2, norm_g, "rms_fwd_x")
    proj, *rest, cg = matmul(u, wp, mode="nn", out_dtype=F32, tm=2048, tn=1024, tk=2048, name="mm_proj",
                             side=gather_side(shards[1:], conv_w[0]))
    full = {}
    for n, wg in zip(BIG[1:], forward_halves(rest)):
        rows, cols = wg.shape[1:]
        full[n] = wg.transpose(1, 0, 2).reshape(rows, 4 * cols) if n in COL_SHARDED else wg.reshape(4 * rows, cols)
    conv_full = cg.transpose(1, 0, 2).reshape(conv_w.shape[1], -1)
    alog_pad = jnp.pad(gdn_a_log, ((0, 0), (NHEAD, BA_W - 2 * NHEAD)))
    dt_pad = jnp.pad(gdn_dt_bias, ((0, 0), (NHEAD, BA_W - 2 * NHEAD)))

    q, k, v, bg, gcol, gt = gdn_prep_fwd(proj, conv_full, alog_pad, dt_pad)
    gt3 = gt.reshape(BA_W, n_chunks, CHUNK).transpose(1, 0, 2)
    gu, gw, qd, kd, qk, tinv = gdn_intra_fwd(q, k, v, bg, gcol, gt3)
    o_raw, states = gdn_seq_fwd(gu, gw, qd, kd, qk, gt3)
    ga = gdn_out_fwd(o_raw, proj, ZA_CB, gdn_norm_g)

    xb = proj[:, XB_CB * S5_IN:(XB_CB + 1) * S5_IN]
    y_ssm, s5_saved = s5_ssm_fwd(xb, s5_lambda_re[0], s5_lambda_im[0], s5_log_dt[0], s5_b_re[0], s5_b_im[0],
                                 s5_c_re[0], s5_c_im[0])
    yb = s5_act_fwd(y_ssm, proj, XB_CB, s5_d)
    glu = mm(yb, full["s5_w_glu"], mode="nn", out_dtype=BF16, tk=1024, name="mm_glu")
    gb = s5_glu_fwd(glu, proj, ZB_CB)

    mem_n, rm = rms_fwd(mem2, mem_norm_g, "rms_fwd_mem")
    kv = mm(mem_n, full["w_kv_mem"], mode="nn", out_dtype=BF16, tk=2048, name="mm_kv")
    o_c = xa_fwd(proj, kv)
    gcx = gate_fwd(o_c, proj, ZC_CB, "gate_fwd_c")

    pa = mm(ga, full["w_br_a"], mode="nn", out_dtype=BF16, tk=1024, name="mm_pa")
    pb = mm(gb, full["w_br_b"], mode="nn", out_dtype=BF16, tk=1024, name="mm_pb")
    pc = mm(gcx, full["w_br_c"], mode="nn", out_dtype=BF16, tk=1024, name="mm_pc")
    merged = merge_fwd(pa, pb, pc, proj, GATE_CB)
    hres = mm(merged, full["w_out"], mode="nn", out_dtype=F32, tk=2048, name="mm_out")
    dh, dhb, loss_part, d_final_g = final_stage(x2, hres, tgt, final_g.reshape(1, d))

    gfull = {}
    dmerged = mm(dhb, full["w_out"], mode="nt", out_dtype=BF16, tk=2048, name="mm_dmerged")
    gfull["w_out"] = mm(merged, dhb, mode="tn", out_dtype=BF16, tk=1024, name="mm_dw_out")
    dproj = lax.empty((s, PROJ_W), BF16)
    dpa, dpb, dpc, dproj = merge_bwd(dmerged, pa, pb, pc, proj, GATE_CB, dproj)
    dga = mm(dpa, full["w_br_a"], mode="nt", out_dtype=BF16, tk=2048, name="mm_dga")
    dgb = mm(dpb, full["w_br_b"], mode="nt", out_dtype=BF16, tk=2048, name="mm_dgb")
    dgc = mm(dpc, full["w_br_c"], mode="nt", out_dtype=BF16, tk=2048, name="mm_dgc")
    gfull["w_br_a"] = mm(ga, dpa, mode="tn", out_dtype=BF16, tk=1024, name="mm_dw_a")
    gfull["w_br_b"] = mm(gb, dpb, mode="tn", out_dtype=BF16, tk=1024, name="mm_dw_b")
    gfull["w_br_c"] = mm(gcx, dpc, mode="tn", out_dtype=BF16, tk=1024, name="mm_dw_c")

    do_raw, dproj, d_gdn_norm = gdn_out_bwd(dga, o_raw, proj, ZA_CB, gdn_norm_g, dproj)
    du_, dw_, dqd, dkd, dqk, dgl = gdn_seq_bwd(do_raw, gu, gw, qd, kd, qk, gt3, states)
    dq, dk, dv, dbg = gdn_intra_bwd(q, k, v, bg, gcol, gt3, tinv, du_, dw_, dqd, dkd, dqk, dgl)
    dc, dproj, dcw0, dcw1, dcw2, dcw3, d_alog, d_dt = gdn_prep_bwd1(proj, conv_full, alog_pad, dt_pad, dq, dk, dv, dbg, dproj)
    dproj = gdn_prep_bwd2(dc, conv_full, dproj)
    d_conv = jnp.concatenate([dcw0, dcw1, dcw2, dcw3], axis=0)

    dval, dgate, dproj = s5_glu_bwd(dgb, glu, proj, ZB_CB, dproj)
    dglu = jnp.concatenate([dval, dgate], axis=1)
    dyb = mm(dglu, full["s5_w_glu"], mode="nt", out_dtype=BF16, tk=2048, name="mm_dyb")
    gfull["s5_w_glu"] = mm(yb, dglu, mode="tn", out_dtype=BF16, tk=1024, name="mm_dw_glu")
    dy_ssm, dxb_direct, d_s5_d = s5_act_bwd(dyb, y_ssm, proj, XB_CB, s5_d)
    dxb_scan, d_lre, d_lim, d_ldt, d_bre, d_bim, d_cre, d_cim = s5_ssm_bwd(dy_ssm, s5_saved)
    dproj = add_into(dxb_direct, dxb_scan, "s5_dxb", dproj, XB_CB)

    do_c, dproj = gate_bwd(dgc, o_c, proj, ZC_CB, "gate_bwd_c", dproj)
    dproj, dkv = xa_bwd(do_c, proj, kv, dproj)
    gfull["w_kv_mem"] = mm(mem_n, dkv, mode="tn", out_dtype=BF16, tk=256, name="mm_dw_kv")
    dmem_n = mm(dkv, full["w_kv_mem"], mode="nt", out_dtype=F32, tk=2048, name="mm_dmem")
    d_mem_norm = rms_bwd_g(dmem_n, mem2, rm, "rms_bwd_mem")

    core = lax.axis_index("c").astype(jnp.int32).reshape(1)
    by_shard = []
    for n in BIG[1:]:
        rows, cols = wts[n].shape[1:]
        g = gfull[n]
        by_shard.append(g.reshape(rows, 4, cols).transpose(1, 0, 2) if n in COL_SHARDED else g.reshape(4, rows, cols))
    chip_rest = exchange_cores(by_shard, "exchange_cores_rest")

    small_g = dict(gdn_a_log=d_alog[:, NHEAD:2 * NHEAD], gdn_dt_bias=d_dt[:, NHEAD:2 * NHEAD], gdn_norm_g=d_gdn_norm,
                   s5_lambda_re=d_lre, s5_lambda_im=d_lim, s5_log_dt=d_ldt, s5_b_re=d_bre, s5_b_im=d_bim,
                   s5_c_re=d_cre, s5_c_im=d_cim, s5_d=d_s5_d, mem_norm_g=d_mem_norm, final_g=d_final_g)
    early = SMALL[:-1]
    small_send = _pack_small([small_g[n] for n in early] + [d_conv, loss_part])
    dwp, *from_chips_rest, got_small = matmul(u.T, dproj, mode="nn", out_dtype=BF16, tm=2048, tn=1024, tk=1024, name="mm_dw_in",
                                              side=combine_sides(chips_side(chip_rest), small_side(small_send)))
    w_in_shards = _unpack_w_in(dwp)
    chip_in, = exchange_cores([w_in_shards], "exchange_cores_w_in")
    du, from_chips_in = matmul(dproj, wp, mode="nt", out_dtype=F32, tm=2048, tn=1024, tk=512, name="mm_du",
                               side=chips_side([chip_in]))
    grad_x, d_norm_g = rms_bwd_x(du, x2, r1, norm_g, dh)
    from_chips = [from_chips_in] + from_chips_rest
    fulls = [sum_chips(core, a, "sum_chips_" + n) for n, a in zip(BIG, from_chips)]
    small_sum = sum_pieces(got_small, "sum_small")
    norm_sum = sum_pieces(exchange_small(_pack_small([d_norm_g])), "sum_norm_g")
    grads = dict(zip(BIG, sibling_exchange(fulls)))
    small_shapes = [wts[n].shape for n in early] + [d_conv.shape, (1, 1)]
    *small_list, conv_g_full, loss_sum = _unpack_small(small_sum, small_shapes)
    grads.update(zip(early, small_list))
    grads["norm_g"], = _unpack_small(norm_sum, [norm_g.shape])
    cw = conv_w.shape[2]
    shard_idx = 2 * lax.axis_index("x") + lax.axis_index("y")
    grads["conv_w"] = lax.dynamic_slice(conv_g_full, (0, shard_idx * cw), (conv_w.shape[1], cw))[None]

    delta, new_m, new_v = {}, {}, {}
    for n in BIG + ("conv_w",):
        delta[n], new_m[n], new_v[n] = adamw(wts[n], grads[n], mom[n], vel[n], "adamw_" + n)
    res = adamw(*[_pack_small([src[n] for n in SMALL]) for src in (wts, grads, mom, vel)], "adamw_small")
    shapes = [wts[n].shape for n in SMALL]
    for dst, flat in zip((delta, new_m, new_v), res):
        dst.update(zip(SMALL, _unpack_small(flat, shapes)))
    for n in SMALL:
        grads[n] = grads[n].reshape(wts[n].shape)

    return (loss_sum.reshape(()), grad_x.reshape(x.shape), *[grads[n] for n in WEIGHTS], *[delta[n] for n in WEIGHTS],
            *[new_m[n] for n in WEIGHTS], *[new_v[n] for n in WEIGHTS])
```
